```python
import jax, jax.numpy as jnp
from jax import lax
import numpy as np

D_MODEL = 1024
BATCH = 8
SEQ = 4096
DEPTH = 2

HEAD_DIM = 64
N_HEADS = D_MODEL // HEAD_DIM
N_SB_HEADS = N_HEADS // 2
N_CA_HEADS = N_HEADS - N_SB_HEADS
D_SB = N_SB_HEADS * HEAD_DIM
D_CA = N_CA_HEADS * HEAD_DIM
D_IN = 3 * D_SB + 3 * D_CA
D_FF = 4 * D_MODEL
CHUNK = 64
LEFT_CHUNKS = 8
BAND = (LEFT_CHUNKS + 1) * CHUNK
REL_CLIP = 128
N_REL = 2 * REL_CLIP + 1
Q_BLOCK = 128
EPS = 1e-6
NEG_INF = -1e30

kernel_name = "hybrid_stickbreak_chunkrel_adaln_encoder"


def rmsnorm(x, g):
    xf = x.astype(jnp.float32)
    y = xf * lax.rsqrt(jnp.mean(xf * xf, axis=-1, keepdims=True) + EPS)
    return (y * g.astype(jnp.float32)).astype(x.dtype)


def stick_breaking_attention(q, k, v):
    B, S, H, d = q.shape
    scale = d ** -0.5
    outs = []
    for start in range(0, S, Q_BLOCK):
        end = start + Q_BLOCK
        qb = q[:, start:end]
        kb = k[:, :end]
        vb = v[:, :end]
        z = jnp.einsum('bqhd,bkhd->bhqk', qb, kb).astype(jnp.float32) * scale
        t_idx = start + jnp.arange(Q_BLOCK)[:, None]
        s_idx = jnp.arange(end)[None, :]
        strict = s_idx < t_idx
        log_beta = jax.nn.log_sigmoid(z)
        log_1m_beta = jnp.where(strict, jax.nn.log_sigmoid(-z), 0.0)
        suffix = lax.cumsum(log_1m_beta, axis=3, reverse=True) - log_1m_beta
        w = jnp.where(strict, jnp.exp(log_beta + suffix), 0.0)
        outs.append(jnp.einsum('bhqk,bkhd->bqhd', w.astype(v.dtype), vb))
    return jnp.concatenate(outs, axis=1)


def chunked_relpos_attention(q, k, v, rel_bias):
    B, S, H, d = q.shape
    nc = S // CHUNK
    pad = LEFT_CHUNKS * CHUNK
    kp = jnp.pad(k, ((0, 0), (pad, 0), (0, 0), (0, 0))).reshape(B, nc + LEFT_CHUNKS, CHUNK, H, d)
    vp = jnp.pad(v, ((0, 0), (pad, 0), (0, 0), (0, 0))).reshape(B, nc + LEFT_CHUNKS, CHUNK, H, d)
    k_band = jnp.concatenate([kp[:, i:i + nc] for i in range(LEFT_CHUNKS + 1)], axis=2)
    v_band = jnp.concatenate([vp[:, i:i + nc] for i in range(LEFT_CHUNKS + 1)], axis=2)
    qc = q.reshape(B, nc, CHUNK, H, d)
    s = jnp.einsum('bnqhd,bnkhd->bnhqk', qc, k_band).astype(jnp.float32) * (d ** -0.5)
    qi = jnp.arange(CHUNK)[:, None]
    kj = jnp.arange(BAND)[None, :]
    rel = qi + pad - kj
    rel_idx = jnp.clip(rel, -REL_CLIP, REL_CLIP) + REL_CLIP
    bias = rel_bias[:, rel_idx].astype(jnp.float32)
    s = s + bias[None, None]
    key_pos = jnp.arange(nc)[:, None] * CHUNK + kj - pad
    valid = key_pos >= 0
    s = jnp.where(valid[None, :, None, None, :], s, NEG_INF)
    p = jax.nn.softmax(s, axis=-1)
    out = jnp.einsum('bnhqk,bnkhd->bnqhd', p.astype(v.dtype), v_band)
    return out.reshape(B, S, H, d)


def _fwd_setup_inputs(seed: int = 0) -> dict:
    key = jax.random.key(seed)
    ks = jax.random.split(key, 16)
    f32 = jnp.float32
    x = jax.random.normal(ks[0], (BATCH, SEQ, D_MODEL), f32)
    c = jax.random.normal(ks[1], (BATCH, D_MODEL), f32)
    g_norm1 = 1.0 + 0.01 * jax.random.normal(ks[2], (DEPTH, D_MODEL), f32)
    w_in = jax.random.normal(ks[3], (DEPTH, D_MODEL, D_IN), f32) * D_MODEL ** -0.5
    g_q = 1.0 + 0.01 * jax.random.normal(ks[4], (DEPTH, HEAD_DIM), f32)
    g_k = 1.0 + 0.01 * jax.random.normal(ks[5], (DEPTH, HEAD_DIM), f32)
    rel_bias = 0.1 * jax.random.normal(ks[6], (DEPTH, N_CA_HEADS, N_REL), f32)
    w_o = jax.random.normal(ks[7], (DEPTH, D_MODEL, D_MODEL), f32) * D_MODEL ** -0.5
    g_norm2 = 1.0 + 0.01 * jax.random.normal(ks[8], (DEPTH, D_MODEL), f32)
    w1 = jax.random.normal(ks[9], (DEPTH, D_MODEL, D_FF), f32) * D_MODEL ** -0.5
    w2 = jax.random.normal(ks[10], (DEPTH, D_FF, D_MODEL), f32) * D_FF ** -0.5
    w_ada = jax.random.normal(ks[11], (DEPTH, D_MODEL, 6 * D_MODEL), f32) * (0.5 * D_MODEL ** -0.5)
    b_ada = 0.01 * jax.random.normal(ks[12], (DEPTH, 6 * D_MODEL), f32)
    return {"x": x, "c": c, "g_norm1": g_norm1, "w_in": w_in, "g_q": g_q, "g_k": g_k,
            "rel_bias": rel_bias, "w_o": w_o, "g_norm2": g_norm2, "w1": w1, "w2": w2,
            "w_ada": w_ada, "b_ada": b_ada}


def _fwd_reference(x, c, g_norm1, w_in, g_q, g_k, rel_bias, w_o, g_norm2, w1, w2, w_ada, b_ada):
    B, S, D = x.shape
    split_pts = [D_SB, 2 * D_SB, 3 * D_SB, 3 * D_SB + D_CA, 3 * D_SB + 2 * D_CA]
    c_act = jax.nn.silu(c)
    for l in range(DEPTH):
        mod = c_act @ w_ada[l] + b_ada[l]
        sh1, sc1, gt1, sh2, sc2, gt2 = [m[:, None, :] for m in jnp.split(mod, 6, axis=-1)]
        h = rmsnorm(x, g_norm1[l]) * (1.0 + sc1) + sh1
        proj = h @ w_in[l]
        q_sb, k_sb, v_sb, q_ca, k_ca, v_ca = jnp.split(proj, split_pts, axis=-1)
        hs = lambda t, n: t.reshape(B, S, n, HEAD_DIM)
        o_sb = stick_breaking_attention(hs(q_sb, N_SB_HEADS), hs(k_sb, N_SB_HEADS), hs(v_sb, N_SB_HEADS))
        q_ca = rmsnorm(hs(q_ca, N_CA_HEADS), g_q[l])
        k_ca = rmsnorm(hs(k_ca, N_CA_HEADS), g_k[l])
        o_ca = chunked_relpos_attention(q_ca, k_ca, hs(v_ca, N_CA_HEADS), rel_bias[l])
        mixed = jnp.concatenate([o_sb.reshape(B, S, D_SB), o_ca.reshape(B, S, D_CA)], axis=-1)
        x = x + gt1 * (mixed @ w_o[l])
        h = rmsnorm(x, g_norm2[l]) * (1.0 + sc2) + sh2
        x = x + gt2 * (jnp.square(jax.nn.relu(h @ w1[l])) @ w2[l])
    return x


import jax as _jax
import jax.numpy as _jnp

TWIN_FORMAT = 'train_step'
FWD_PARAMS = ['x', 'c', 'g_norm1', 'w_in', 'g_q', 'g_k', 'rel_bias', 'w_o', 'g_norm2', 'w1', 'w2', 'w_ada', 'b_ada']
TWIN_WEIGHTS = ['g_norm1', 'w_in', 'g_q', 'g_k', 'rel_bias', 'w_o', 'g_norm2', 'w1', 'w2', 'w_ada', 'b_ada']
TWIN_DIFF_INPUT = 'x'
TWIN_INPUTS = ['x', 'c', 'g_norm1', 'w_in', 'g_q', 'g_k', 'rel_bias', 'w_o', 'g_norm2', 'w1', 'w2', 'w_ada', 'b_ada', 'loss_target', 'm_g_norm1', 'm_w_in', 'm_g_q', 'm_g_k', 'm_rel_bias', 'm_w_o', 'm_g_norm2', 'm_w1', 'm_w2', 'm_w_ada', 'm_b_ada', 'v_g_norm1', 'v_w_in', 'v_g_q', 'v_g_k', 'v_rel_bias', 'v_w_o', 'v_g_norm2', 'v_w1', 'v_w2', 'v_w_ada', 'v_b_ada']
TWIN_OUTPUTS = ['loss', 'grad_x', 'grad_g_norm1', 'grad_w_in', 'grad_g_q', 'grad_g_k', 'grad_rel_bias', 'grad_w_o', 'grad_g_norm2', 'grad_w1', 'grad_w2', 'grad_w_ada', 'grad_b_ada', 'delta_g_norm1', 'delta_w_in', 'delta_g_q', 'delta_g_k', 'delta_rel_bias', 'delta_w_o', 'delta_g_norm2', 'delta_w1', 'delta_w2', 'delta_w_ada', 'delta_b_ada', 'new_m_g_norm1', 'new_m_w_in', 'new_m_g_q', 'new_m_g_k', 'new_m_rel_bias', 'new_m_w_o', 'new_m_g_norm2', 'new_m_w1', 'new_m_w2', 'new_m_w_ada', 'new_m_b_ada', 'new_v_g_norm1', 'new_v_w_in', 'new_v_g_q', 'new_v_g_k', 'new_v_rel_bias', 'new_v_w_o', 'new_v_g_norm2', 'new_v_w1', 'new_v_w2', 'new_v_w_ada', 'new_v_b_ada']
TWIN_LEAF_KINDS = {'loss': 'loss', 'grad_x': 'grad_x', 'grad_g_norm1': 'grad_w', 'grad_w_in': 'grad_w', 'grad_g_q': 'grad_w', 'grad_g_k': 'grad_w', 'grad_rel_bias': 'grad_w', 'grad_w_o': 'grad_w', 'grad_g_norm2': 'grad_w', 'grad_w1': 'grad_w', 'grad_w2': 'grad_w', 'grad_w_ada': 'grad_w', 'grad_b_ada': 'grad_w', 'delta_g_norm1': 'delta_w', 'delta_w_in': 'delta_w', 'delta_g_q': 'delta_w', 'delta_g_k': 'delta_w', 'delta_rel_bias': 'delta_w', 'delta_w_o': 'delta_w', 'delta_g_norm2': 'delta_w', 'delta_w1': 'delta_w', 'delta_w2': 'delta_w', 'delta_w_ada': 'delta_w', 'delta_b_ada': 'delta_w', 'new_m_g_norm1': 'new_m', 'new_m_w_in': 'new_m', 'new_m_g_q': 'new_m', 'new_m_g_k': 'new_m', 'new_m_rel_bias': 'new_m', 'new_m_w_o': 'new_m', 'new_m_g_norm2': 'new_m', 'new_m_w1': 'new_m', 'new_m_w2': 'new_m', 'new_m_w_ada': 'new_m', 'new_m_b_ada': 'new_m', 'new_v_g_norm1': 'new_v', 'new_v_w_in': 'new_v', 'new_v_g_q': 'new_v', 'new_v_g_k': 'new_v', 'new_v_rel_bias': 'new_v', 'new_v_w_o': 'new_v', 'new_v_g_norm2': 'new_v', 'new_v_w1': 'new_v', 'new_v_w2': 'new_v', 'new_v_w_ada': 'new_v', 'new_v_b_ada': 'new_v'}


def _forward(args):
    return _fwd_reference(*[args[k] for k in FWD_PARAMS])


def _output_shape():
    def fwd():
        inp = _fwd_setup_inputs(0)
        return _fwd_reference(*[inp[k] for k in FWD_PARAMS])
    out = _jax.eval_shape(fwd)
    return out.shape, out.dtype

N_MICROBATCH = 1
ADAM_LR = 0.001
ADAM_B1 = 0.9
ADAM_B2 = 0.999
ADAM_EPS = 1e-08
ADAM_WD = 0.01
ADAM_STEP = 10
PER_EXAMPLE_BATCH_AXIS = {'x': 0, 'c': 0, 'loss_target': 0}
SHARED_INPUTS = []
_WEIGHT_DTYPES = {'g_norm1': _jnp.float32, 'w_in': _jnp.float32, 'g_q': _jnp.float32, 'g_k': _jnp.float32, 'rel_bias': _jnp.float32, 'w_o': _jnp.float32, 'g_norm2': _jnp.float32, 'w1': _jnp.float32, 'w2': _jnp.float32, 'w_ada': _jnp.float32, 'b_ada': _jnp.float32}
MOMENT_SCALE = {'g_norm1': 1.034321e+00, 'w_in': 2.898403e-01, 'g_q': 1.104153e-01, 'g_k': 1.104078e-01, 'rel_bias': 7.759282e-03, 'w_o': 4.807143e-01, 'g_norm2': 1.139299e+01, 'w1': 3.654553e-01, 'w2': 1.423881e+00, 'w_ada': 2.516917e+00, 'b_ada': 6.260582e+00}


def _to_microbatches(a, axis):
    t = _jnp.moveaxis(a, axis, 0)
    t = t.reshape((N_MICROBATCH, t.shape[0] // N_MICROBATCH) + t.shape[1:])
    return _jnp.moveaxis(t, 1, axis + 1)


def setup_inputs(seed: int = 0) -> dict:
    inp = _fwd_setup_inputs(seed)
    key = _jax.random.fold_in(_jax.random.key(seed), 7919)
    shape, _ = _output_shape()
    out = dict(inp)
    out["loss_target"] = _jax.random.normal(_jax.random.fold_in(key, 0), shape, _jnp.float32)
    for i, name in enumerate(TWIN_WEIGHTS):
        w = inp[name].astype(_jnp.float32)
        if MOMENT_SCALE is None:
            s = _jnp.sqrt(_jnp.mean(_jnp.square(w)) + 1e-30)
        else:
            s = MOMENT_SCALE[name]
        km, kv = _jax.random.split(_jax.random.fold_in(key, i + 1))
        out[name] = w
        out["m_" + name] = s * _jax.random.normal(km, w.shape, _jnp.float32)
        out["v_" + name] = (s * s) * _jax.random.uniform(kv, w.shape, _jnp.float32, 0.5, 1.5)
    if N_MICROBATCH > 1:
        for name, axis in PER_EXAMPLE_BATCH_AXIS.items():
            out[name] = _to_microbatches(out[name], axis)
    return {'x': out['x'], 'c': out['c'], 'g_norm1': out['g_norm1'], 'w_in': out['w_in'], 'g_q': out['g_q'], 'g_k': out['g_k'], 'rel_bias': out['rel_bias'], 'w_o': out['w_o'], 'g_norm2': out['g_norm2'], 'w1': out['w1'], 'w2': out['w2'], 'w_ada': out['w_ada'], 'b_ada': out['b_ada'], 'loss_target': out['loss_target'], 'm_g_norm1': out['m_g_norm1'], 'm_w_in': out['m_w_in'], 'm_g_q': out['m_g_q'], 'm_g_k': out['m_g_k'], 'm_rel_bias': out['m_rel_bias'], 'm_w_o': out['m_w_o'], 'm_g_norm2': out['m_g_norm2'], 'm_w1': out['m_w1'], 'm_w2': out['m_w2'], 'm_w_ada': out['m_w_ada'], 'm_b_ada': out['m_b_ada'], 'v_g_norm1': out['v_g_norm1'], 'v_w_in': out['v_w_in'], 'v_g_q': out['v_g_q'], 'v_g_k': out['v_g_k'], 'v_rel_bias': out['v_rel_bias'], 'v_w_o': out['v_w_o'], 'v_g_norm2': out['v_g_norm2'], 'v_w1': out['v_w1'], 'v_w2': out['v_w2'], 'v_w_ada': out['v_w_ada'], 'v_b_ada': out['v_b_ada']}


def _loss(weights, diff, rest, loss_target):
    with _jax.named_scope("forward"):
        args = {**rest, TWIN_DIFF_INPUT: diff, **{k: w.astype(_WEIGHT_DTYPES[k]) for k, w in weights.items()}}
        y = _forward(args)
    with _jax.named_scope("loss_head"):
        err = _jnp.square(y.astype(_jnp.float32) - loss_target)
        return 0.5 * _jnp.sum(_jnp.mean(err, axis=-1)) if err.ndim else 0.5 * err


def _adamw(w, g, m, v):
    m = ADAM_B1 * m + (1.0 - ADAM_B1) * g
    v = ADAM_B2 * v + (1.0 - ADAM_B2) * _jnp.square(g)
    m_hat = m / (1.0 - ADAM_B1 ** ADAM_STEP)
    v_hat = v / (1.0 - ADAM_B2 ** ADAM_STEP)
    delta = -ADAM_LR * (m_hat / (_jnp.sqrt(v_hat) + ADAM_EPS) + ADAM_WD * w)
    return delta, m, v


def reference(x, c, g_norm1, w_in, g_q, g_k, rel_bias, w_o, g_norm2, w1, w2, w_ada, b_ada, loss_target, m_g_norm1, m_w_in, m_g_q, m_g_k, m_rel_bias, m_w_o, m_g_norm2, m_w1, m_w2, m_w_ada, m_b_ada, v_g_norm1, v_w_in, v_g_q, v_g_k, v_rel_bias, v_w_o, v_g_norm2, v_w1, v_w2, v_w_ada, v_b_ada):
    given = dict(x=x, c=c, g_norm1=g_norm1, w_in=w_in, g_q=g_q, g_k=g_k, rel_bias=rel_bias, w_o=w_o, g_norm2=g_norm2, w1=w1, w2=w2, w_ada=w_ada, b_ada=b_ada, loss_target=loss_target, m_g_norm1=m_g_norm1, m_w_in=m_w_in, m_g_q=m_g_q, m_g_k=m_g_k, m_rel_bias=m_rel_bias, m_w_o=m_w_o, m_g_norm2=m_g_norm2, m_w1=m_w1, m_w2=m_w2, m_w_ada=m_w_ada, m_b_ada=m_b_ada, v_g_norm1=v_g_norm1, v_w_in=v_w_in, v_g_q=v_g_q, v_g_k=v_g_k, v_rel_bias=v_rel_bias, v_w_o=v_w_o, v_g_norm2=v_g_norm2, v_w1=v_w1, v_w2=v_w2, v_w_ada=v_w_ada, v_b_ada=v_b_ada)
    weights = {n: given[n] for n in TWIN_WEIGHTS}
    shared = {n: given[n] for n in SHARED_INPUTS}
    per_example = {n: given[n] for n in ['x', 'c']}
    grad_fn = _jax.value_and_grad(_loss, argnums=(0, 1))

    def one_microbatch(ex, loss_target):
        ex = dict(ex)
        diff = ex.pop(TWIN_DIFF_INPUT)
        return grad_fn(weights, diff, {**shared, **ex}, loss_target)

    if N_MICROBATCH == 1:
        loss, (grad_w, grad_x) = one_microbatch(per_example, given["loss_target"])
    else:
        def body(carry, xs):
            loss_sum, grad_sum = carry
            l_k, (gw_k, gx_k) = one_microbatch(xs[0], xs[1])
            with _jax.named_scope("update"):
                return (loss_sum + l_k, _jax.tree.map(_jnp.add, grad_sum, gw_k)), gx_k

        init = (_jnp.zeros((), _jnp.float32), _jax.tree.map(_jnp.zeros_like, weights))
        (loss, grad_w), grad_x = _jax.lax.scan(body, init, (per_example, given["loss_target"]))
    with _jax.named_scope("update"):
        delta_w, new_m, new_v = {}, {}, {}
        for n in TWIN_WEIGHTS:
            delta_w[n], new_m[n], new_v[n] = _adamw(weights[n], grad_w[n], given["m_" + n], given["v_" + n])
    return (loss, grad_x, *[grad_w[n] for n in TWIN_WEIGHTS], *[delta_w[n] for n in TWIN_WEIGHTS],
            *[new_m[n] for n in TWIN_WEIGHTS], *[new_v[n] for n in TWIN_WEIGHTS])
```

```python
import functools

import jax
import jax.numpy as jnp
from jax import lax
from jax.experimental import pallas as pl
from jax.experimental.pallas import tpu as pltpu

F32 = jnp.float32
BF16 = jnp.bfloat16
MESH = pl.DeviceIdType.MESH

EPS = 1e-6
NEG_INF = -1e30
HEAD_DIM = 64
CHUNK = 64
LEFT_CHUNKS = 8
PAD = LEFT_CHUNKS * CHUNK
BAND = PAD + CHUNK
REL_CLIP = 128
SCALE = HEAD_DIM ** -0.5
LANES = 128
N_CHIPS = 4
N_DEV = 8

ADAM_LR = 0.001
ADAM_B1 = 0.9
ADAM_B2 = 0.999
ADAM_EPS = 1e-08
ADAM_WD = 0.01
ADAM_STEP = 10

VMEM_LIMIT = 48 * 1024 * 1024
ROW_TILE = 256
SB_TILE = 256
CA_TILE = 256

NT = (((1,), (1,)), ((), ()))
TN = (((0,), (0,)), ((), ()))
NN = (((1,), (0,)), ((), ()))


def _tile(n, pref):
    best = None
    t = LANES
    while t <= min(n, pref):
        if n % t == 0:
            best = t
        t += LANES
    return best if best is not None else n


def _params(*sem):
    return pltpu.CompilerParams(dimension_semantics=sem, vmem_limit_bytes=VMEM_LIMIT)


def _split_dot(x, t, passes):
    acc = None
    r = x
    for i in range(passes):
        hi = r.astype(BF16)
        d = jnp.dot(hi, t, preferred_element_type=F32)
        acc = d if acc is None else acc + d
        if i + 1 < passes:
            r = r - hi.astype(F32)
    return acc


def _matmul(name, a, b, *, dims, grid, a_spec, b_spec, out_shapes, out_specs, acc_shape,
            epi=None, extras=(), extra_specs=()):
    nk = grid[2]
    n_ex = len(extras)
    n_out = len(out_shapes)

    def body(*refs):
        a_ref, b_ref = refs[0], refs[1]
        ex = refs[2:2 + n_ex]
        outs = refs[2 + n_ex:2 + n_ex + n_out]
        acc = refs[-1]
        k = pl.program_id(2)

        @pl.when(k == 0)
        def _():
            acc[...] = jnp.zeros_like(acc)

        acc[...] += lax.dot_general(a_ref[...].astype(BF16), b_ref[...].astype(BF16), dims,
                                    preferred_element_type=F32)

        @pl.when(k == nk - 1)
        def _():
            res = epi(acc[...], *[e[...] for e in ex]) if epi is not None else (acc[...],)
            for o, r in zip(outs, res):
                o[...] = r.astype(o.dtype)

    return pl.pallas_call(
        body, grid=grid, in_specs=[a_spec, b_spec, *extra_specs], out_specs=out_specs,
        out_shape=out_shapes, scratch_shapes=[pltpu.VMEM(acc_shape, F32)],
        compiler_params=_params("parallel", "parallel", "arbitrary"), name=name,
    )(a, b, *extras)


def _mm_nn_w(name, a, wg, layer, colshard, out_dtypes, epi=None, extras=(), extra_specs_fn=None):
    m = a.shape[0]
    _, _, r, c = wg.shape
    tm = min(512, m)
    if colshard:
        k, n = r, N_CHIPS * c
        tk, tn = k, _tile(c, 768)
        npc = c // tn
        b_spec = pl.BlockSpec((None, None, tk, tn), lambda i, j, kk: (layer, j // npc, kk, j % npc))
    else:
        k, n = N_CHIPS * r, c
        tk, tn = _tile(r, 1024), _tile(c, 512)
        kpc = r // tk
        b_spec = pl.BlockSpec((None, None, tk, tn), lambda i, j, kk: (layer, kk // kpc, kk % kpc, j))
    assert a.shape[1] == k
    grid = (m // tm, n // tn, k // tk)
    out_spec = pl.BlockSpec((tm, tn), lambda i, j, kk: (i, j))
    ex_specs = extra_specs_fn(tm, tn) if extra_specs_fn is not None else ()
    return _matmul(name, a, wg, dims=NN, grid=grid,
                   a_spec=pl.BlockSpec((tm, tk), lambda i, j, kk: (i, kk)), b_spec=b_spec,
                   out_shapes=[jax.ShapeDtypeStruct((m, n), d) for d in out_dtypes],
                   out_specs=[out_spec for _ in out_dtypes], acc_shape=(tm, tn),
                   epi=epi, extras=extras, extra_specs=ex_specs)


def _mm_nt_w(name, a, wg, layer, colshard, out_dtype, epi=None, extras=(), extra_specs_fn=None):
    m = a.shape[0]
    _, _, r, c = wg.shape
    tm = min(512, m)
    if colshard:
        kdim, n = N_CHIPS * c, r
        tk, tn = _tile(c, 1024), _tile(r, 512)
        kpc = c // tk
        b_spec = pl.BlockSpec((None, None, tn, tk), lambda i, j, kk: (layer, kk // kpc, j, kk % kpc))
    else:
        kdim, n = c, N_CHIPS * r
        tk, tn = c, _tile(r, 512)
        npc = r // tn
        b_spec = pl.BlockSpec((None, None, tn, tk), lambda i, j, kk: (layer, j // npc, j % npc, kk))
    assert a.shape[1] == kdim
    grid = (m // tm, n // tn, kdim // tk)
    ex_specs = extra_specs_fn(tm, tn) if extra_specs_fn is not None else ()
    return _matmul(name, a, wg, dims=NT, grid=grid,
                   a_spec=pl.BlockSpec((tm, tk), lambda i, j, kk: (i, kk)), b_spec=b_spec,
                   out_shapes=[jax.ShapeDtypeStruct((m, n), out_dtype)],
                   out_specs=[pl.BlockSpec((tm, tn), lambda i, j, kk: (i, j))], acc_shape=(tm, tn),
                   epi=epi, extras=extras, extra_specs=ex_specs)[0]


def _mm_tn(name, a, b, colshard):
    s, m = a.shape
    n = b.shape[1]
    tk = min(512, s)
    if colshard:
        c = n // N_CHIPS
        tm, tn = _tile(m, 512), _tile(c, 768)
        npc = c // tn
        out_shape = (N_CHIPS, m, c)
        out_spec = pl.BlockSpec((None, tm, tn), lambda i, j, kk: (j // npc, i, j % npc))
    else:
        r = m // N_CHIPS
        tm, tn = _tile(r, 512), _tile(n, 512)
        mpc = r // tm
        out_shape = (N_CHIPS, r, n)
        out_spec = pl.BlockSpec((None, tm, tn), lambda i, j, kk: (i // mpc, i % mpc, j))
    grid = (m // tm, n // tn, s // tk)
    return _matmul(name, a, b, dims=TN, grid=grid,
                   a_spec=pl.BlockSpec((tk, tm), lambda i, j, kk: (kk, i)),
                   b_spec=pl.BlockSpec((tk, tn), lambda i, j, kk: (kk, j)),
                   out_shapes=[jax.ShapeDtypeStruct(out_shape, F32)], out_specs=[out_spec],
                   acc_shape=(tm, tn))[0]


def _row_spec(ts, d):
    return pl.BlockSpec((ts, d), lambda i: (i, 0))


def _vec_spec(d):
    return pl.BlockSpec((1, d), lambda i: (0, 0))


def _norm_mod(name, x, g, sc, sh):
    s, d = x.shape
    ts = min(ROW_TILE, s)

    def body(x_ref, g_ref, sc_ref, sh_ref, h_ref):
        xv = x_ref[...]
        r = lax.rsqrt(jnp.mean(xv * xv, axis=-1, keepdims=True) + EPS)
        h_ref[...] = (xv * r * g_ref[...] * (1.0 + sc_ref[...]) + sh_ref[...]).astype(BF16)

    return pl.pallas_call(
        body, grid=(s // ts,), in_specs=[_row_spec(ts, d), _vec_spec(d), _vec_spec(d), _vec_spec(d)],
        out_specs=_row_spec(ts, d), out_shape=jax.ShapeDtypeStruct((s, d), BF16),
        compiler_params=_params("parallel"), name=name)(x, g, sc, sh)


def _norm_mod_bwd(name, dh, x, g, sc, dres):
    s, d = x.shape
    ts = min(ROW_TILE, s)

    def body(dh_ref, x_ref, g_ref, sc_ref, dres_ref, dx_ref, dsh_ref, dsc_ref, dg_ref):
        @pl.when(pl.program_id(0) == 0)
        def _():
            dsh_ref[...] = jnp.zeros_like(dsh_ref)
            dsc_ref[...] = jnp.zeros_like(dsc_ref)
            dg_ref[...] = jnp.zeros_like(dg_ref)

        xv = x_ref[...]
        r = lax.rsqrt(jnp.mean(xv * xv, axis=-1, keepdims=True) + EPS)
        xhat = xv * r
        dhv = dh_ref[...]
        gv = g_ref[...]
        opsc = 1.0 + sc_ref[...]
        dxhat = dhv * (gv * opsc)
        mdot = jnp.mean(dxhat * xhat, axis=-1, keepdims=True)
        dx_ref[...] = dres_ref[...] + r * (dxhat - xhat * mdot)
        dhx = dhv * xhat
        dsh_ref[...] += jnp.sum(dhv, axis=0, keepdims=True)
        dsc_ref[...] += jnp.sum(dhx * gv, axis=0, keepdims=True)
        dg_ref[...] += jnp.sum(dhx * opsc, axis=0, keepdims=True)

    vec = jax.ShapeDtypeStruct((1, d), F32)
    return pl.pallas_call(
        body, grid=(s // ts,),
        in_specs=[_row_spec(ts, d), _row_spec(ts, d), _vec_spec(d), _vec_spec(d), _row_spec(ts, d)],
        out_specs=[_row_spec(ts, d), _vec_spec(d), _vec_spec(d), _vec_spec(d)],
        out_shape=[jax.ShapeDtypeStruct((s, d), F32), vec, vec, vec],
        compiler_params=_params("arbitrary"), name=name)(dh, x, g, sc, dres)


def _gate_bwd(name, dx, m, gt):
    s, d = dx.shape
    ts = min(ROW_TILE, s)

    def body(dx_ref, m_ref, gt_ref, dm_ref, dgt_ref):
        @pl.when(pl.program_id(0) == 0)
        def _():
            dgt_ref[...] = jnp.zeros_like(dgt_ref)

        dxv = dx_ref[...]
        dm_ref[...] = (dxv * gt_ref[...]).astype(BF16)
        dgt_ref[...] += jnp.sum(dxv * m_ref[...].astype(F32), axis=0, keepdims=True)

    return pl.pallas_call(
        body, grid=(s // ts,), in_specs=[_row_spec(ts, d), _row_spec(ts, d), _vec_spec(d)],
        out_specs=[_row_spec(ts, d), _vec_spec(d)],
        out_shape=[jax.ShapeDtypeStruct((s, d), BF16), jax.ShapeDtypeStruct((1, d), F32)],
        compiler_params=_params("arbitrary"), name=name)(dx, m, gt)


def _loss_grad(name, y, target):
    s, d = y.shape
    ts = min(ROW_TILE, s)

    def body(y_ref, t_ref, dy_ref, sq_ref):
        @pl.when(pl.program_id(0) == 0)
        def _():
            sq_ref[...] = jnp.zeros_like(sq_ref)

        e = y_ref[...] - t_ref[...]
        dy_ref[...] = e * (1.0 / d)
        sq_ref[...] += jnp.sum(e * e, axis=0, keepdims=True)

    return pl.pallas_call(
        body, grid=(s // ts,), in_specs=[_row_spec(ts, d), _row_spec(ts, d)],
        out_specs=[_row_spec(ts, d), _vec_spec(d)],
        out_shape=[jax.ShapeDtypeStruct((s, d), F32), jax.ShapeDtypeStruct((1, d), F32)],
        compiler_params=_params("arbitrary"), name=name)(y, target)


def _adamw(name, w, g, m, v):
    r, c = w.shape
    tr = r
    for cand in (512, 256, 128, 64, 32, 16, 8):
        if r % cand == 0 and cand * c * 4 <= 2 * 1024 * 1024:
            tr = cand
            break

    def body(w_ref, g_ref, m_ref, v_ref, d_ref, mo_ref, vo_ref):
        gv = g_ref[...]
        m2 = ADAM_B1 * m_ref[...] + (1.0 - ADAM_B1) * gv
        v2 = ADAM_B2 * v_ref[...] + (1.0 - ADAM_B2) * (gv * gv)
        m_hat = m2 / (1.0 - ADAM_B1 ** ADAM_STEP)
        v_hat = v2 / (1.0 - ADAM_B2 ** ADAM_STEP)
        d_ref[...] = -ADAM_LR * (m_hat / (jnp.sqrt(v_hat) + ADAM_EPS) + ADAM_WD * w_ref[...])
        mo_ref[...] = m2
        vo_ref[...] = v2

    spec = pl.BlockSpec((tr, c), lambda i: (i, 0))
    shp = jax.ShapeDtypeStruct((r, c), F32)
    return pl.pallas_call(
        body, grid=(r // tr,), in_specs=[spec] * 4, out_specs=[spec] * 3, out_shape=[shp] * 3,
        compiler_params=_params("parallel"), name=name)(w, g, m, v)


def _sum_rows8(name, x):
    n, r, c = x.shape

    def body(x_ref, o_ref):
        acc = x_ref[0]
        for i in range(1, n):
            acc = acc + x_ref[i]
        o_ref[...] = acc

    return pl.pallas_call(
        body, out_shape=jax.ShapeDtypeStruct((r, c), F32),
        in_specs=[pl.BlockSpec(memory_space=pltpu.VMEM)], out_specs=pl.BlockSpec(memory_space=pltpu.VMEM),
        name=name)(x)


def _outer_sum(name, ct, dm):
    d, n_seq = ct.shape
    n = dm.shape[1]
    tr = min(256, d)

    def body(ct_ref, dm_ref, o_ref):
        acc = ct_ref[:, 0:1] * dm_ref[0:1, :]
        for s_i in range(1, n_seq):
            acc = acc + ct_ref[:, s_i:s_i + 1] * dm_ref[s_i:s_i + 1, :]
        o_ref[...] = acc

    return pl.pallas_call(
        body, grid=(d // tr,),
        in_specs=[pl.BlockSpec((tr, n_seq), lambda i: (i, 0)), pl.BlockSpec((n_seq, n), lambda i: (0, 0))],
        out_specs=pl.BlockSpec((tr, n), lambda i: (i, 0)), out_shape=jax.ShapeDtypeStruct((d, n), F32),
        compiler_params=_params("parallel"), name=name)(ct, dm)


def _ada_mod(name, c_act, w_ada_l, bias):
    n_seq, d = c_act.shape
    n = w_ada_l.shape[1]
    tn = _tile(n, 768)

    def epi(acc, b):
        return (acc + b,)

    return _matmul(name, c_act, w_ada_l, dims=NN, grid=(1, n // tn, 1),
                   a_spec=pl.BlockSpec((n_seq, d), lambda i, j, kk: (0, 0)),
                   b_spec=pl.BlockSpec((d, tn), lambda i, j, kk: (0, j)),
                   out_shapes=[jax.ShapeDtypeStruct((n_seq, n), F32)],
                   out_specs=[pl.BlockSpec((n_seq, tn), lambda i, j, kk: (0, j))], acc_shape=(n_seq, tn),
                   epi=epi, extras=(bias,), extra_specs=(pl.BlockSpec((1, tn), lambda i, j, kk: (0, j)),))[0]


def _silu_rows(name, c_row):
    d = c_row.shape[1]

    def body(c_ref, o_ref):
        cv = c_ref[...]
        o_ref[...] = jnp.broadcast_to(cv / (1.0 + jnp.exp(-cv)), (8, d))

    return pl.pallas_call(
        body, out_shape=jax.ShapeDtypeStruct((8, d), F32),
        in_specs=[pl.BlockSpec(memory_space=pltpu.VMEM)], out_specs=pl.BlockSpec(memory_space=pltpu.VMEM),
        name=name)(c_row)


def _sb_masks(t):
    row = lax.broadcasted_iota(jnp.int32, (t, t), 0)
    col = lax.broadcasted_iota(jnp.int32, (t, t), 1)
    strict = col < row
    tri = (row > col).astype(BF16)
    return strict, tri


def _log_sigmoids(z):
    sp = jnp.log(1.0 + jnp.exp(-jnp.abs(z)))
    return jnp.minimum(z, 0.0) - sp, jnp.minimum(-z, 0.0) - sp, sp


def _sb_fwd(name, proj, n_pairs):
    s = proj.shape[0]
    t = min(SB_TILE, s)
    nq = s // t

    def body(q_ref, k_ref, v_ref, o_ref):
        qi = pl.program_id(1)
        strict, tri = _sb_masks(t)
        for h in range(2):
            sl = slice(h * HEAD_DIM, (h + 1) * HEAD_DIM)
            q = (q_ref[:, sl] * SCALE).astype(BF16)

            def block(kb, carry, masked):
                acc, run = carry
                ks = pl.multiple_of(kb * t, t)
                k = k_ref[pl.ds(ks, t), sl].astype(BF16)
                v = v_ref[pl.ds(ks, t), sl].astype(BF16)
                z = lax.dot_general(q, k, NT, preferred_element_type=F32)
                lb, lm, _ = _log_sigmoids(z)
                if masked:
                    lm = jnp.where(strict, lm, 0.0)
                w = jnp.exp(lb + _split_dot(lm, tri, 2) + run)
                if masked:
                    w = jnp.where(strict, w, 0.0)
                acc = acc + jnp.dot(w.astype(BF16), v, preferred_element_type=F32)
                run = run + jnp.sum(lm, axis=1, keepdims=True)
                return acc, run

            carry = block(qi, (jnp.zeros((t, HEAD_DIM), F32), jnp.zeros((t, 1), F32)), True)
            carry = lax.fori_loop(0, qi, lambda i, c: block(qi - 1 - i, c, False), carry)
            o_ref[:, sl] = carry[0]

    return pl.pallas_call(
        body, grid=(n_pairs, nq),
        in_specs=[pl.BlockSpec((t, LANES), lambda p, i: (i, p)),
                  pl.BlockSpec((s, LANES), lambda p, i: (0, n_pairs + p)),
                  pl.BlockSpec((s, LANES), lambda p, i: (0, 2 * n_pairs + p))],
        out_specs=pl.BlockSpec((t, LANES), lambda p, i: (i, p)),
        out_shape=jax.ShapeDtypeStruct((s, n_pairs * LANES), F32),
        compiler_params=_params("parallel", "arbitrary"), name=name)(proj, proj, proj)


def _sb_bwd(name, proj, o_sb, dmixed, n_pairs):
    s = proj.shape[0]
    t = min(SB_TILE, s)
    nq = s // t

    def body(q_ref, k_ref, v_ref, o_ref, do_ref, dq_ref, dk_ref, dv_ref):
        qi = pl.program_id(1)

        @pl.when(qi == 0)
        def _():
            dk_ref[...] = jnp.zeros_like(dk_ref)
            dv_ref[...] = jnp.zeros_like(dv_ref)

        strict, tri = _sb_masks(t)
        for h in range(2):
            sl = slice(h * HEAD_DIM, (h + 1) * HEAD_DIM)
            q = (q_ref[:, sl] * SCALE).astype(BF16)
            do32 = do_ref[:, sl]
            do = do32.astype(BF16)
            dsum = jnp.sum(do.astype(F32) * o_ref[:, sl], axis=1, keepdims=True)

            def block(kb, carry, masked):
                dq, run_f, run_e = carry
                ks = pl.multiple_of(kb * t, t)
                k = k_ref[pl.ds(ks, t), sl].astype(BF16)
                v = v_ref[pl.ds(ks, t), sl].astype(BF16)
                z = lax.dot_general(q, k, NT, preferred_element_type=F32)
                lb, lm, sp = _log_sigmoids(z)
                if masked:
                    lm = jnp.where(strict, lm, 0.0)
                w = jnp.exp(lb + _split_dot(lm, tri, 2) + run_f)
                if masked:
                    w = jnp.where(strict, w, 0.0)
                wb = w.astype(BF16)
                e = wb.astype(F32) * lax.dot_general(do, v, NT, preferred_element_type=F32)
                left = dsum - (_split_dot(e, tri, 2) + e + run_e)
                ez = jnp.exp(-jnp.abs(z))
                inv = 1.0 / (1.0 + ez)
                beta = jnp.where(z >= 0.0, 1.0, ez) * inv
                one_m_beta = jnp.where(z >= 0.0, ez, 1.0) * inv
                dz = e * one_m_beta - left * beta
                if masked:
                    dz = jnp.where(strict, dz, 0.0)
                dzb = dz.astype(BF16)
                dv_ref[pl.ds(ks, t), sl] += lax.dot_general(wb, do, TN, preferred_element_type=F32)
                dk_ref[pl.ds(ks, t), sl] += lax.dot_general(dzb, q, TN, preferred_element_type=F32)
                dq = dq + jnp.dot(dzb, k, preferred_element_type=F32)
                run_f = run_f + jnp.sum(lm, axis=1, keepdims=True)
                run_e = run_e + jnp.sum(e, axis=1, keepdims=True)
                return dq, run_f, run_e

            zero = jnp.zeros((t, 1), F32)
            carry = block(qi, (jnp.zeros((t, HEAD_DIM), F32), zero, zero), True)
            carry = lax.fori_loop(0, qi, lambda i, c: block(qi - 1 - i, c, False), carry)
            dq_ref[:, sl] = carry[0] * SCALE

    blk = pl.BlockSpec((t, LANES), lambda p, i: (i, p))
    full = pl.BlockSpec((s, LANES), lambda p, i: (0, p))
    shp = jax.ShapeDtypeStruct((s, n_pairs * LANES), F32)
    return pl.pallas_call(
        body, grid=(n_pairs, nq),
        in_specs=[blk,
                  pl.BlockSpec((s, LANES), lambda p, i: (0, n_pairs + p)),
                  pl.BlockSpec((s, LANES), lambda p, i: (0, 2 * n_pairs + p)),
                  blk, blk],
        out_specs=[blk, full, full], out_shape=[shp, shp, shp],
        compiler_params=_params("parallel", "arbitrary"), name=name)(proj, proj, proj, o_sb, dmixed)


def _group_mean_matrix():
    row = lax.broadcasted_iota(jnp.int32, (LANES, LANES), 0)
    col = lax.broadcasted_iota(jnp.int32, (LANES, LANES), 1)
    same_head = (row < HEAD_DIM) == (col < HEAD_DIM)
    return jnp.where(same_head, 1.0 / HEAD_DIM, 0.0).astype(BF16)


def _qk_norm(name, proj, gq2, gk2, n_pairs):
    s = proj.shape[0]
    ts = min(ROW_TILE, s)
    pb = PAD // ts
    c0 = 3 * n_pairs

    def body(q_ref, k_ref, v_ref, gq_ref, gk_ref, qn_ref, kn_ref, vp_ref):
        r_i = pl.program_id(1)

        @pl.when(r_i < pb)
        def _():
            qn_ref[...] = jnp.zeros_like(qn_ref)
            kn_ref[...] = jnp.zeros_like(kn_ref)
            vp_ref[...] = jnp.zeros_like(vp_ref)

        @pl.when(r_i >= pb)
        def _():
            gm = _group_mean_matrix()
            for src, g_ref, dst in ((q_ref, gq_ref, qn_ref), (k_ref, gk_ref, kn_ref)):
                xv = src[...]
                r = lax.rsqrt(_split_dot(xv * xv, gm, 3) + EPS)
                dst[...] = (xv * r * g_ref[...]).astype(BF16)
            vp_ref[...] = v_ref[...].astype(BF16)

    def in_spec(off):
        return pl.BlockSpec((ts, LANES), lambda p, i: (jnp.maximum(i - pb, 0), c0 + off * n_pairs + p))

    gspec = pl.BlockSpec((1, LANES), lambda p, i: (0, 0))
    ospec = pl.BlockSpec((ts, LANES), lambda p, i: (i, p))
    shp = jax.ShapeDtypeStruct((s + PAD, n_pairs * LANES), BF16)
    return pl.pallas_call(
        body, grid=(n_pairs, (s + PAD) // ts),
        in_specs=[in_spec(0), in_spec(1), in_spec(2), gspec, gspec],
        out_specs=[ospec, ospec, ospec], out_shape=[shp, shp, shp],
        compiler_params=_params("parallel", "arbitrary"), name=name)(proj, proj, proj, gq2, gk2)


def _qk_norm_bwd(name, dqn, dkn, proj, gq2, gk2, n_pairs):
    s = proj.shape[0]
    ts = min(ROW_TILE, s)
    pb = PAD // ts
    c0 = 3 * n_pairs
    n_r = s // ts

    def body(dqn_ref, dkn_ref, q_ref, k_ref, gq_ref, gk_ref, dq_ref, dk_ref, dgq_ref, dgk_ref):
        first = jnp.logical_and(pl.program_id(0) == 0, pl.program_id(1) == 0)
        last = jnp.logical_and(pl.program_id(0) == n_pairs - 1, pl.program_id(1) == n_r - 1)

        @pl.when(first)
        def _():
            dgq_ref[...] = jnp.zeros_like(dgq_ref)
            dgk_ref[...] = jnp.zeros_like(dgk_ref)

        gm = _group_mean_matrix()
        for dn_ref, x_ref, g_ref, dx_ref, dg_ref in ((dqn_ref, q_ref, gq_ref, dq_ref, dgq_ref),
                                                     (dkn_ref, k_ref, gk_ref, dk_ref, dgk_ref)):
            xv = x_ref[...]
            r = lax.rsqrt(_split_dot(xv * xv, gm, 3) + EPS)
            xhat = xv * r
            dn = dn_ref[...]
            tg = dn * g_ref[...]
            md = _split_dot(tg * xhat, gm, 3)
            dx_ref[...] = (r * (tg - xhat * md)).astype(BF16)
            dg_ref[...] += jnp.sum(dn * xhat, axis=0, keepdims=True)

        @pl.when(last)
        def _():
            for dg_ref in (dgq_ref, dgk_ref):
                gv = dg_ref[...]
                dg_ref[...] = gv + pltpu.roll(gv, HEAD_DIM, axis=1)

    def x_spec(off):
        return pl.BlockSpec((ts, LANES), lambda p, i: (i, c0 + off * n_pairs + p))

    dn_q = pl.BlockSpec((ts, LANES), lambda p, i: (i, p))
    dn_k = pl.BlockSpec((ts, LANES), lambda p, i: (i + pb, p))
    gspec = pl.BlockSpec((1, LANES), lambda p, i: (0, 0))
    ospec = pl.BlockSpec((ts, LANES), lambda p, i: (i, p))
    shp = jax.ShapeDtypeStruct((s, n_pairs * LANES), BF16)
    vec = jax.ShapeDtypeStruct((1, LANES), F32)
    return pl.pallas_call(
        body, grid=(n_pairs, n_r),
        in_specs=[dn_q, dn_k, x_spec(0), x_spec(1), gspec, gspec],
        out_specs=[ospec, ospec, gspec, gspec], out_shape=[shp, shp, vec, vec],
        compiler_params=_params("arbitrary", "arbitrary"), name=name)(dqn, dkn, proj, proj, gq2, gk2)


def _ca_scores(qc, kw, bias, n):
    sc = lax.dot_general(qc, kw, NT, preferred_element_type=F32) * SCALE + bias
    jpos = lax.broadcasted_iota(jnp.int32, (CHUNK, BAND), 1)
    return jnp.where(n * CHUNK + jpos >= PAD, sc, NEG_INF)


def _ca_fwd(name, qn, knp, vp, bias_t, n_pairs):
    sp_rows = knp.shape[0]
    s = sp_rows - PAD
    t = min(CA_TILE, s)
    cpb = t // CHUNK
    pb = PAD // t

    def body(q_ref, k_ref, v_ref, b_ref, o_ref):
        qi = pl.program_id(1)
        for h in range(2):
            sl = slice(h * HEAD_DIM, (h + 1) * HEAD_DIM)

            def chunk(ci, carry):
                n = qi * cpb + ci
                r0 = pl.multiple_of(ci * CHUNK, CHUNK)
                ws = pl.multiple_of(n * CHUNK, CHUNK)
                qc = q_ref[pl.ds(r0, CHUNK), sl]
                kw = k_ref[pl.ds(ws, BAND), sl]
                vw = v_ref[pl.ds(ws, BAND), sl]
                sc = _ca_scores(qc, kw, b_ref[h], n)
                p = jnp.exp(sc - jnp.max(sc, axis=1, keepdims=True))
                den = jnp.sum(p, axis=1, keepdims=True)
                o_ref[pl.ds(r0, CHUNK), sl] = jnp.dot(p.astype(BF16), vw, preferred_element_type=F32) / den
                return carry

            lax.fori_loop(0, cpb, chunk, 0)

    full = pl.BlockSpec((sp_rows, LANES), lambda p, i: (0, p))
    return pl.pallas_call(
        body, grid=(n_pairs, s // t),
        in_specs=[pl.BlockSpec((t, LANES), lambda p, i: (i + pb, p)), full, full,
                  pl.BlockSpec((2, CHUNK, BAND), lambda p, i: (p, 0, 0))],
        out_specs=pl.BlockSpec((t, LANES), lambda p, i: (i, p)),
        out_shape=jax.ShapeDtypeStruct((s, n_pairs * LANES), F32),
        compiler_params=_params("parallel", "arbitrary"), name=name)(qn, knp, vp, bias_t)


def _ca_bwd(name, qn, knp, vp, bias_t, dmixed, n_pairs, col0):
    sp_rows = knp.shape[0]
    s = sp_rows - PAD
    t = min(CA_TILE, s)
    cpb = t // CHUNK
    pb = PAD // t

    def body(q_ref, k_ref, v_ref, b_ref, do_ref, dq_ref, dk_ref, dv_ref, db_ref):
        qi = pl.program_id(1)

        @pl.when(qi == 0)
        def _():
            dk_ref[...] = jnp.zeros_like(dk_ref)
            dv_ref[...] = jnp.zeros_like(dv_ref)
            db_ref[...] = jnp.zeros_like(db_ref)

        for h in range(2):
            sl = slice(h * HEAD_DIM, (h + 1) * HEAD_DIM)

            def chunk(ci, carry):
                n = qi * cpb + ci
                r0 = pl.multiple_of(ci * CHUNK, CHUNK)
                ws = pl.multiple_of(n * CHUNK, CHUNK)
                qc = q_ref[pl.ds(r0, CHUNK), sl]
                kw = k_ref[pl.ds(ws, BAND), sl]
                vw = v_ref[pl.ds(ws, BAND), sl]
                sc = _ca_scores(qc, kw, b_ref[h], n)
                p = jnp.exp(sc - jnp.max(sc, axis=1, keepdims=True))
                p = p / jnp.sum(p, axis=1, keepdims=True)
                do = do_ref[pl.ds(r0, CHUNK), sl].astype(BF16)
                dp = lax.dot_general(do, vw, NT, preferred_element_type=F32)
                ds = p * (dp - jnp.sum(p * dp, axis=1, keepdims=True))
                dsb = ds.astype(BF16)
                dv_ref[pl.ds(ws, BAND), sl] += lax.dot_general(p.astype(BF16), do, TN, preferred_element_type=F32)
                dk_ref[pl.ds(ws, BAND), sl] += lax.dot_general(dsb, qc, TN, preferred_element_type=F32) * SCALE
                dq_ref[pl.ds(r0, CHUNK), sl] = jnp.dot(dsb, kw, preferred_element_type=F32) * SCALE
                db_ref[h] += ds
                return carry

            lax.fori_loop(0, cpb, chunk, 0)

    full_in = pl.BlockSpec((sp_rows, LANES), lambda p, i: (0, p))
    btile = pl.BlockSpec((2, CHUNK, BAND), lambda p, i: (p, 0, 0))
    pad_shape = jax.ShapeDtypeStruct((sp_rows, n_pairs * LANES), F32)
    return pl.pallas_call(
        body, grid=(n_pairs, s // t),
        in_specs=[pl.BlockSpec((t, LANES), lambda p, i: (i + pb, p)), full_in, full_in, btile,
                  pl.BlockSpec((t, LANES), lambda p, i: (i, col0 + p))],
        out_specs=[pl.BlockSpec((t, LANES), lambda p, i: (i, p)), full_in, full_in, btile],
        out_shape=[jax.ShapeDtypeStruct((s, n_pairs * LANES), F32), pad_shape, pad_shape,
                   jax.ShapeDtypeStruct(bias_t.shape, F32)],
        compiler_params=_params("parallel", "arbitrary"), name=name)(qn, knp, vp, bias_t, dmixed)


def _bias_tile_index():
    i = jnp.arange(CHUNK)[:, None]
    j = jnp.arange(BAND)[None, :]
    return jnp.clip(i + PAD - j, -REL_CLIP, REL_CLIP) + REL_CLIP


def _rel_bias_grad(name, db_t):
    _, n_h, _ = db_t.shape
    n_out = 3 * LANES
    assert CHUNK == 64 and REL_CLIP == 128 and LEFT_CHUNKS == 8

    def body(z_ref, o_ref):
        a_idx = lax.broadcasted_iota(jnp.int32, (LANES, n_out), 0)
        r_idx = lax.broadcasted_iota(jnp.int32, (LANES, n_out), 1)
        out = jnp.zeros((n_h, n_out), F32)
        for mm, base in ((0, 191), (1, 255), (2, 319)):
            b = LEFT_CHUNKS - mm
            acc = jnp.zeros((n_h, LANES), F32)
            for i in range(CHUNK):
                row = z_ref[i, :, b * CHUNK:(b + 1) * CHUNK]
                rowp = jnp.concatenate([row, jnp.zeros((n_h, LANES - CHUNK), F32)], axis=1)
                acc = acc + pltpu.roll(rowp, CHUNK - 1 - i, axis=1)
            place = (r_idx == jnp.minimum(base - a_idx, 2 * REL_CLIP)).astype(BF16)
            out = out + _split_dot(acc, place, 3)
        far = jnp.zeros((n_h, 1), F32)
        for i in range(CHUNK):
            far = far + jnp.sum(z_ref[i, :, 0:(LEFT_CHUNKS - 2) * CHUNK], axis=1, keepdims=True)
        lane = lax.broadcasted_iota(jnp.int32, (n_h, n_out), 1)
        o_ref[...] = out + jnp.where(lane == 2 * REL_CLIP, far, 0.0)

    return pl.pallas_call(
        body, out_shape=jax.ShapeDtypeStruct((n_h, n_out), F32),
        in_specs=[pl.BlockSpec(memory_space=pltpu.VMEM)], out_specs=pl.BlockSpec(memory_space=pltpu.VMEM),
        name=name)(db_t)


def _mesh_pos():
    return lax.axis_index("x"), lax.axis_index("y"), lax.axis_index("c")


def _other_chips(x, y):
    return [(1 - x, y), (x, 1 - y), (1 - x, 1 - y)]


def _all_gather_small(name, blk):
    m_per, n = blk.shape

    def body(x_ref, out_ref, send_sems, recv_sems, local_sem):
        x, y, c = _mesh_pos()
        me, sibling = (x, y, c), (x, y, 1 - c)
        chips = _other_chips(x, y)

        def rows(px, py, pc):
            return out_ref.at[pl.ds((4 * px + 2 * py + pc) * m_per, m_per), :]

        def copy(k, block, to, src=None):
            return pltpu.make_async_remote_copy(
                src_ref=rows(*block) if src is None else src, dst_ref=rows(*block),
                send_sem=send_sems.at[k], recv_sem=recv_sems.at[k], device_id=to, device_id_type=MESH)

        mine = pltpu.make_async_copy(x_ref, rows(*me), local_sem)
        mine.start()
        first = [copy(0, me, sibling, src=x_ref)]
        first += [copy(1 + j, me, (*chip, c), src=x_ref) for j, chip in enumerate(chips)]
        for cp in first:
            cp.start()
        passed = [copy(4 + j, (*chip, c), sibling) for j, chip in enumerate(chips)]
        for j, chip in enumerate(chips):
            copy(1 + j, (*chip, c), me).wait_recv()
            passed[j].start()
        copy(0, sibling, me).wait_recv()
        for j, chip in enumerate(chips):
            copy(4 + j, (*chip, 1 - c), me).wait_recv()
        for cp in first + passed:
            cp.wait_send()
        mine.wait()

    return pl.pallas_call(
        body, out_shape=jax.ShapeDtypeStruct((N_DEV * m_per, n), blk.dtype),
        in_specs=[pl.BlockSpec(memory_space=pltpu.VMEM)], out_specs=pl.BlockSpec(memory_space=pltpu.VMEM),
        scratch_shapes=[pltpu.SemaphoreType.DMA((7,)), pltpu.SemaphoreType.DMA((7,)), pltpu.SemaphoreType.DMA],
        compiler_params=pltpu.CompilerParams(vmem_limit_bytes=VMEM_LIMIT), name=name)(blk)


HBM_SPEC = pl.BlockSpec(memory_space=pltpu.HBM)


def _all_gather_weights(name, shards):
    n = len(shards)

    def body(*refs):
        srcs, outs = refs[:n], refs[n:2 * n]
        send_sems, recv_sems, local_sems = refs[2 * n:]
        x, y, c = _mesh_pos()
        me, sibling = (x, y, c), (x, y, 1 - c)
        chips = _other_chips(x, y)

        def win(t, px, py, pc):
            rh = shards[t].shape[1] // 2
            return outs[t].at[:, 2 * px + py, pl.ds(pc * rh, rh), :]

        def own(t):
            rh = shards[t].shape[1] // 2
            return srcs[t].at[:, pl.ds(c * rh, rh), :]

        def copy(t, k, block, to, src=None):
            return pltpu.make_async_remote_copy(
                src_ref=win(t, *block) if src is None else src, dst_ref=win(t, *block),
                send_sem=send_sems.at[t, k], recv_sem=recv_sems.at[t, k], device_id=to, device_id_type=MESH)

        mine = [pltpu.make_async_copy(own(t), win(t, *me), local_sems.at[t]) for t in range(n)]
        for cp in mine:
            cp.start()
        first = []
        for t in range(n):
            first.append(copy(t, 0, me, sibling, src=own(t)))
            first += [copy(t, 1 + j, me, (*chip, c), src=own(t)) for j, chip in enumerate(chips)]
        for cp in first:
            cp.start()
        passed = []
        for j, chip in enumerate(chips):
            for t in range(n):
                copy(t, 1 + j, (*chip, c), me).wait_recv()
                fwd = copy(t, 4 + j, (*chip, c), sibling)
                fwd.start()
                passed.append(fwd)
        for t in range(n):
            copy(t, 0, sibling, me).wait_recv()
            for j, chip in enumerate(chips):
                copy(t, 4 + j, (*chip, 1 - c), me).wait_recv()
        for cp in first + passed:
            cp.wait_send()
        for cp in mine:
            cp.wait()

    out_shape = [jax.ShapeDtypeStruct((w.shape[0], N_CHIPS, w.shape[1], w.shape[2]), w.dtype) for w in shards]
    return pl.pallas_call(
        body, out_shape=out_shape, in_specs=[HBM_SPEC] * n, out_specs=[HBM_SPEC] * n,
        scratch_shapes=[pltpu.SemaphoreType.DMA((n, 7)), pltpu.SemaphoreType.DMA((n, 7)),
                        pltpu.SemaphoreType.DMA((n,))],
        name=name)(*shards)


def _exchange_sibling_halves(name, grads):
    n = len(grads)

    def body(*refs):
        srcs, outs = refs[:n], refs[n:2 * n]
        send_sems, recv_sems = refs[2 * n:]
        x, y, c = _mesh_pos()
        cps = []
        for t in range(n):
            rh = grads[t].shape[1] // 2
            cps.append(pltpu.make_async_remote_copy(
                src_ref=srcs[t].at[:, pl.ds((1 - c) * rh, rh), :], dst_ref=outs[t],
                send_sem=send_sems.at[t], recv_sem=recv_sems.at[t], device_id=(x, y, 1 - c), device_id_type=MESH))
        for cp in cps:
            cp.start()
        for cp in cps:
            cp.wait()

    out_shape = [jax.ShapeDtypeStruct((g.shape[0], g.shape[1] // 2, g.shape[2]), g.dtype) for g in grads]
    return pl.pallas_call(
        body, out_shape=out_shape, in_specs=[HBM_SPEC] * n, out_specs=[HBM_SPEC] * n,
        scratch_shapes=[pltpu.SemaphoreType.DMA((n,)), pltpu.SemaphoreType.DMA((n,))],
        name=name)(*grads)


def _add_halves(name, g, recv, c_idx):
    nb, r, c = g.shape
    rh = r // 2
    tr = rh
    for cand in (512, 256, 128, 64):
        if rh % cand == 0:
            tr = cand
            break
    g4 = g.reshape(nb, 2, rh, c)

    def body(c_ref, g_ref, r_ref, o_ref):
        o_ref[...] = (g_ref[...] + r_ref[...]).astype(BF16)

    grid_spec = pltpu.PrefetchScalarGridSpec(
        num_scalar_prefetch=1, grid=(nb, rh // tr),
        in_specs=[pl.BlockSpec((None, None, tr, c), lambda j, i, cr: (j, cr[0], i, 0)),
                  pl.BlockSpec((None, tr, c), lambda j, i, cr: (j, i, 0))],
        out_specs=pl.BlockSpec((None, tr, c), lambda j, i, cr: (j, i, 0)))
    return pl.pallas_call(
        body, grid_spec=grid_spec, out_shape=jax.ShapeDtypeStruct((nb, rh, c), BF16),
        compiler_params=_params("parallel", "parallel"), name=name)(c_idx, g4, recv)


def _exchange_chips(name, parts):
    n = len(parts)

    def body(*refs):
        srcs, outs = refs[:n], refs[n:2 * n]
        send_sems, recv_sems, local_sems = refs[2 * n:]
        x, y, c = _mesh_pos()
        my_j = 2 * x + y
        chips = _other_chips(x, y)
        local = [pltpu.make_async_copy(srcs[t].at[my_j], outs[t].at[my_j], local_sems.at[t]) for t in range(n)]
        for cp in local:
            cp.start()
        cps = []
        for t in range(n):
            for k, (px, py) in enumerate(chips):
                cps.append(pltpu.make_async_remote_copy(
                    src_ref=srcs[t].at[2 * px + py], dst_ref=outs[t].at[my_j],
                    send_sem=send_sems.at[t, k], recv_sem=recv_sems.at[t, k],
                    device_id=(px, py, c), device_id_type=MESH))
        for cp in cps:
            cp.start()
        for cp in cps:
            cp.wait()
        for cp in local:
            cp.wait()

    out_shape = [jax.ShapeDtypeStruct(p.shape, p.dtype) for p in parts]
    return pl.pallas_call(
        body, out_shape=out_shape, in_specs=[HBM_SPEC] * n, out_specs=[HBM_SPEC] * n,
        scratch_shapes=[pltpu.SemaphoreType.DMA((n, 3)), pltpu.SemaphoreType.DMA((n, 3)),
                        pltpu.SemaphoreType.DMA((n,))],
        name=name)(*parts)


def _sum_chips(name, recv):
    nb, rh, c = recv.shape
    tr = rh
    for cand in (512, 256, 128, 64):
        if rh % cand == 0:
            tr = cand
            break

    def body(r_ref, o_ref):
        acc = r_ref[0].astype(F32)
        for j in range(1, nb):
            acc = acc + r_ref[j].astype(F32)
        o_ref[...] = acc

    return pl.pallas_call(
        body, grid=(rh // tr,), in_specs=[pl.BlockSpec((nb, tr, c), lambda i: (0, i, 0))],
        out_specs=pl.BlockSpec((tr, c), lambda i: (i, 0)), out_shape=jax.ShapeDtypeStruct((rh, c), F32),
        compiler_params=_params("parallel"), name=name)(recv)


def _share_with_sibling(name, halves, n_layers):
    flat = [h for per_w in halves for h in per_w]
    n = len(flat)
    n_w = len(halves)

    def body(*refs):
        srcs, outs = refs[:n], refs[n:n + n_w]
        send_sems, recv_sems, local_sems = refs[n + n_w:]
        x, y, c = _mesh_pos()
        local, cps = [], []
        for w_i in range(n_w):
            for l_i in range(n_layers):
                t = w_i * n_layers + l_i
                rh = flat[t].shape[0]
                dst = outs[w_i].at[l_i, pl.ds(c * rh, rh), :]
                local.append(pltpu.make_async_copy(srcs[t], dst, local_sems.at[t]))
                cps.append(pltpu.make_async_remote_copy(
                    src_ref=srcs[t], dst_ref=dst, send_sem=send_sems.at[t], recv_sem=recv_sems.at[t],
                    device_id=(x, y, 1 - c), device_id_type=MESH))
        for cp in local + cps:
            cp.start()
        for t, cp in enumerate(cps):
            cp.wait_send()
            rh = flat[t].shape[0]
            w_i, l_i = divmod(t, n_layers)
            pltpu.make_async_remote_copy(
                src_ref=srcs[t], dst_ref=outs[w_i].at[l_i, pl.ds((1 - c) * rh, rh), :],
                send_sem=send_sems.at[t], recv_sem=recv_sems.at[t],
                device_id=(x, y, 1 - c), device_id_type=MESH).wait_recv()
        for cp in local:
            cp.wait()

    out_shape = [jax.ShapeDtypeStruct((n_layers, 2 * per_w[0].shape[0], per_w[0].shape[1]), F32) for per_w in halves]
    return pl.pallas_call(
        body, out_shape=out_shape, in_specs=[HBM_SPEC] * n, out_specs=[HBM_SPEC] * n_w,
        scratch_shapes=[pltpu.SemaphoreType.DMA((n,)), pltpu.SemaphoreType.DMA((n,)),
                        pltpu.SemaphoreType.DMA((n,))],
        name=name)(*flat)


def _row(v):
    return v.reshape(1, -1)


def kernel(x, c, g_norm1, w_in, g_q, g_k, rel_bias, w_o, g_norm2, w1, w2, w_ada, b_ada, loss_target, m_g_norm1, m_w_in, m_g_q, m_g_k, m_rel_bias, m_w_o, m_g_norm2, m_w1, m_w2, m_w_ada, m_b_ada, v_g_norm1, v_w_in, v_g_q, v_g_k, v_rel_bias, v_w_o, v_g_norm2, v_w1, v_w2, v_w_ada, v_b_ada):
    _, s, d = x.shape
    n_layers = g_norm1.shape[0]
    n_pairs = (d // 2) // LANES
    n_ca_heads = rel_bias.shape[1]
    n_rel = rel_bias.shape[2]
    assert n_rel == 2 * REL_CLIP + 1 and g_q.shape[1] == HEAD_DIM and n_ca_heads == 2 * n_pairs
    ada_c = w_ada.shape[2]

    ax, ay, ac = _mesh_pos()
    chip = 2 * ax + ay
    dev = 4 * ax + 2 * ay + ac
    c_idx = jnp.reshape(ac, (1,)).astype(jnp.int32)

    c_act_all = _all_gather_small("ag_c", _silu_rows("silu_c", c)).reshape(N_DEV, 8, d)[:, 0, :]
    mod_parts = []
    for l in range(n_layers):
        bias = lax.dynamic_slice_in_dim(b_ada[l], chip * ada_c, ada_c).reshape(1, ada_c)
        mod_parts.append(_ada_mod(f"ada_mod{l}", c_act_all, w_ada[l], bias))
    mod_all = _all_gather_small("ag_mod", jnp.concatenate(mod_parts, axis=1))
    mod_all = mod_all.reshape(N_CHIPS, 2, N_DEV, n_layers, ada_c)[:, 0]
    mod_all = jnp.transpose(mod_all, (1, 2, 0, 3)).reshape(N_DEV, n_layers, N_CHIPS * ada_c)
    mod = lax.dynamic_index_in_dim(mod_all, dev, axis=0, keepdims=False)

    wg_in, wg_o, wg_1, wg_2 = _all_gather_weights(
        "ag_weights", [w_in.astype(BF16), w_o.astype(BF16), w1.astype(BF16), w2.astype(BF16)])

    bias_idx = _bias_tile_index()
    xs = x[0]
    saved = []
    for l in range(n_layers):
        sh1, sc1, gt1, sh2, sc2, gt2 = [_row(mod[l, i * d:(i + 1) * d]) for i in range(6)]
        gq2 = _row(jnp.tile(g_q[l], 2))
        gk2 = _row(jnp.tile(g_k[l], 2))
        bias_t = rel_bias[l][:, bias_idx]
        h1 = _norm_mod(f"norm1_{l}", xs, _row(g_norm1[l]), sc1, sh1)
        proj = _mm_nn_w(f"proj_{l}", h1, wg_in, l, True, [F32])[0]
        o_sb = _sb_fwd(f"sb_fwd_{l}", proj, n_pairs)
        qn, knp, vp = _qk_norm(f"qk_norm_{l}", proj, gq2, gk2, n_pairs)
        o_ca = _ca_fwd(f"ca_fwd_{l}", qn, knp, vp, bias_t, n_pairs)
        mixed = jnp.concatenate([o_sb, o_ca], axis=1)

        def res_epi(acc, res, gt):
            return res + gt * acc, acc

        def res_specs(tm, tn):
            return (pl.BlockSpec((tm, tn), lambda i, j, kk: (i, j)), pl.BlockSpec((1, tn), lambda i, j, kk: (0, j)))

        x1, m1 = _mm_nn_w(f"attn_out_{l}", mixed, wg_o, l, False, [F32, BF16], epi=res_epi,
                          extras=(xs, gt1), extra_specs_fn=res_specs)
        h2 = _norm_mod(f"norm2_{l}", x1, _row(g_norm2[l]), sc2, sh2)

        def act_epi(acc):
            r = jnp.maximum(acc, 0.0)
            return acc, r * r

        u, a = _mm_nn_w(f"mlp_up_{l}", h2, wg_1, l, True, [BF16, BF16], epi=act_epi)
        x2, m2 = _mm_nn_w(f"mlp_down_{l}", a, wg_2, l, False, [F32, BF16], epi=res_epi,
                          extras=(x1, gt2), extra_specs_fn=res_specs)
        saved.append(dict(x0=xs, h1=h1, proj=proj, o_sb=o_sb, qn=qn, knp=knp, vp=vp, bias_t=bias_t, mixed=mixed,
                          m1=m1, x1=x1, h2=h2, u=u, a=a, m2=m2, gq2=gq2, gk2=gk2,
                          sc1=sc1, gt1=gt1, sc2=sc2, gt2=gt2))
        xs = x2

    dx, sq = _loss_grad("loss", xs, loss_target[0])
    loss = lax.psum(0.5 * jnp.sum(sq) / d, ("x", "y", "c"))

    g_in, g_o, g_1, g_2 = [None] * n_layers, [None] * n_layers, [None] * n_layers, [None] * n_layers
    dmod, dg1, dg2, dgq, dgk, drel = [], [], [], [], [], []
    for l in reversed(range(n_layers)):
        sv = saved[l]
        dm2, dgt2 = _gate_bwd(f"gate2_bwd_{l}", dx, sv["m2"], sv["gt2"])

        def act_bwd_epi(acc, u_t):
            return (acc * (2.0 * jnp.maximum(u_t.astype(F32), 0.0)),)

        def tile_specs(tm, tn):
            return (pl.BlockSpec((tm, tn), lambda i, j, kk: (i, j)),)

        du = _mm_nt_w(f"mlp_down_bwd_{l}", dm2, wg_2, l, False, BF16, epi=act_bwd_epi, extras=(sv["u"],),
                      extra_specs_fn=tile_specs)
        g_2[l] = _mm_tn(f"w2_grad_{l}", sv["a"], dm2, False)
        dh2 = _mm_nt_w(f"mlp_up_bwd_{l}", du, wg_1, l, True, F32)
        g_1[l] = _mm_tn(f"w1_grad_{l}", sv["h2"], du, True)
        dx1, dsh2, dsc2, dgn2 = _norm_mod_bwd(f"norm2_bwd_{l}", dh2, sv["x1"], _row(g_norm2[l]), sv["sc2"], dx)
        dm1, dgt1 = _gate_bwd(f"gate1_bwd_{l}", dx1, sv["m1"], sv["gt1"])
        dmixed = _mm_nt_w(f"attn_out_bwd_{l}", dm1, wg_o, l, False, F32)
        g_o[l] = _mm_tn(f"wo_grad_{l}", sv["mixed"], dm1, False)
        dq_sb, dk_sb, dv_sb = _sb_bwd(f"sb_bwd_{l}", sv["proj"], sv["o_sb"], dmixed, n_pairs)
        dqn, dknp, dvp, dbias_t = _ca_bwd(f"ca_bwd_{l}", sv["qn"], sv["knp"], sv["vp"], sv["bias_t"], dmixed,
                                          n_pairs, n_pairs)
        dq_ca, dk_ca, dgq_l, dgk_l = _qk_norm_bwd(f"qk_norm_bwd_{l}", dqn, dknp, sv["proj"], sv["gq2"], sv["gk2"],
                                                  n_pairs)
        drel_l = _rel_bias_grad(f"rel_bias_grad_{l}", jnp.transpose(dbias_t, (1, 0, 2)))[:, :n_rel]
        dproj = jnp.concatenate([dq_sb.astype(BF16), dk_sb.astype(BF16), dv_sb.astype(BF16), dq_ca, dk_ca,
                                 dvp[PAD:].astype(BF16)], axis=1)
        g_in[l] = _mm_tn(f"win_grad_{l}", sv["h1"], dproj, True)
        dh1 = _mm_nt_w(f"proj_bwd_{l}", dproj, wg_in, l, True, F32)
        dx, dsh1, dsc1, dgn1 = _norm_mod_bwd(f"norm1_bwd_{l}", dh1, sv["x0"], _row(g_norm1[l]), sv["sc1"], dx1)
        dmod.insert(0, jnp.concatenate([dsh1, dsc1, dgt1, dsh2, dsc2, dgt2], axis=1)[0])
        dg1.insert(0, dgn1[0])
        dg2.insert(0, dgn2[0])
        dgq.insert(0, dgq_l[0, :HEAD_DIM])
        dgk.insert(0, dgk_l[0, :HEAD_DIM])
        drel.insert(0, drel_l.reshape(-1))
    grad_x = dx[None]

    small = jnp.concatenate([jnp.concatenate(dmod), jnp.concatenate(dg1), jnp.concatenate(dg2),
                             jnp.concatenate(dgq), jnp.concatenate(dgk), jnp.concatenate(drel)])
    n_small = small.shape[0]
    n_pack = -(-n_small // (8 * LANES)) * (8 * LANES)
    small = jnp.pad(small, (0, n_pack - n_small)).reshape(8, n_pack // 8)
    small_all = _all_gather_small("ag_small", small).reshape(N_DEV, 8, n_pack // 8)
    small_sum = _sum_rows8("sum_small", small_all).reshape(-1)
    dmod_all = small_all.reshape(N_DEV, n_pack)[:, :n_layers * 6 * d].reshape(N_DEV, n_layers, 6 * d)

    off = 0

    def take(n, shape):
        nonlocal off
        out = small_sum[off:off + n].reshape(shape)
        off += n
        return out

    grad_b_ada = take(n_layers * 6 * d, (n_layers, 6 * d))
    grad_g_norm1 = take(n_layers * d, (n_layers, d))
    grad_g_norm2 = take(n_layers * d, (n_layers, d))
    grad_g_q = take(n_layers * HEAD_DIM, (n_layers, HEAD_DIM))
    grad_g_k = take(n_layers * HEAD_DIM, (n_layers, HEAD_DIM))
    grad_rel_bias = take(n_layers * n_ca_heads * n_rel, (n_layers, n_ca_heads, n_rel))

    c_act_t = jnp.transpose(c_act_all)
    grad_w_ada = jnp.stack([
        _outer_sum(f"wada_grad_{l}", c_act_t,
                   lax.dynamic_slice_in_dim(dmod_all[:, l, :], chip * ada_c, ada_c, axis=1))
        for l in range(n_layers)])

    partial = [g for per_w in (g_in, g_o, g_1, g_2) for g in per_w]
    from_sibling = _exchange_sibling_halves("rs_sibling", partial)
    chip_sums = [_add_halves(f"rs_add_{i}", g, r, c_idx) for i, (g, r) in enumerate(zip(partial, from_sibling))]
    from_chips = _exchange_chips("rs_chips", chip_sums)
    reduced = [_sum_chips(f"rs_sum_{i}", r) for i, r in enumerate(from_chips)]
    halves = [reduced[i * n_layers:(i + 1) * n_layers] for i in range(4)]
    grad_w_in, grad_w_o, grad_w1, grad_w2 = _share_with_sibling("rs_share", halves, n_layers)

    def adam_big(name, w, g, m, v):
        shp = w.shape
        outs = _adamw(name, w.reshape(-1, shp[-1]), g.reshape(-1, shp[-1]), m.reshape(-1, shp[-1]),
                      v.reshape(-1, shp[-1]))
        return [o.reshape(shp) for o in outs]

    def pack(arrs):
        flat = jnp.concatenate([a.reshape(-1) for a in arrs])
        n = flat.shape[0]
        n_p = -(-n // (8 * LANES)) * (8 * LANES)
        return jnp.pad(flat, (0, n_p - n), constant_values=1.0).reshape(8, n_p // 8)

    small_w = [g_norm1, g_q, g_k, rel_bias, g_norm2, b_ada]
    small_g = [grad_g_norm1, grad_g_q, grad_g_k, grad_rel_bias, grad_g_norm2, grad_b_ada]
    small_m = [m_g_norm1, m_g_q, m_g_k, m_rel_bias, m_g_norm2, m_b_ada]
    small_v = [v_g_norm1, v_g_q, v_g_k, v_rel_bias, v_g_norm2, v_b_ada]
    packed = _adamw("adamw_small", pack(small_w), pack(small_g), pack(small_m), pack(small_v))

    def unpack(p):
        flat = p.reshape(-1)
        res, o = [], 0
        for a in small_w:
            res.append(flat[o:o + a.size].reshape(a.shape))
            o += a.size
        return res

    sd, sm, sv_ = unpack(packed[0]), unpack(packed[1]), unpack(packed[2])
    big = {
        "w_in": adam_big("adamw_w_in", w_in, grad_w_in, m_w_in, v_w_in),
        "w_o": adam_big("adamw_w_o", w_o, grad_w_o, m_w_o, v_w_o),
        "w1": adam_big("adamw_w1", w1, grad_w1, m_w1, v_w1),
        "w2": adam_big("adamw_w2", w2, grad_w2, m_w2, v_w2),
        "w_ada": adam_big("adamw_w_ada", w_ada, grad_w_ada, m_w_ada, v_w_ada),
    }

    def ordered(kind):
        sm_list = (sd, sm, sv_)[kind]
        return [sm_list[0], big["w_in"][kind], sm_list[1], sm_list[2], sm_list[3], big["w_o"][kind], sm_list[4],
                big["w1"][kind], big["w2"][kind], big["w_ada"][kind], sm_list[5]]

    grads = [grad_g_norm1, grad_w_in, grad_g_q, grad_g_k, grad_rel_bias, grad_w_o, grad_g_norm2, grad_w1, grad_w2,
             grad_w_ada, grad_b_ada]
    return (loss, grad_x, *grads, *ordered(0), *ordered(1), *ordered(2))
```

```python
import functools

import jax
import jax.numpy as jnp
from jax import lax
from jax.experimental import pallas as pl
from jax.experimental.pallas import tpu as pltpu

F32 = jnp.float32
BF16 = jnp.bfloat16
MESH = pl.DeviceIdType.MESH

EPS = 1e-6
NEG_INF = -1e30
HEAD_DIM = 64
CHUNK = 64
LEFT_CHUNKS = 8
PAD = LEFT_CHUNKS * CHUNK
BAND = PAD + CHUNK
REL_CLIP = 128
SCALE = HEAD_DIM ** -0.5
LANES = 128
N_CHIPS = 4
N_DEV = 8

ADAM_LR = 0.001
ADAM_B1 = 0.9
ADAM_B2 = 0.999
ADAM_EPS = 1e-08
ADAM_WD = 0.01
ADAM_STEP = 10

VMEM_LIMIT = 48 * 1024 * 1024
ROW_TILE = 256
SB_TILE = 256
SB_STRIP = 32
CA_TILE = 256

NT = (((1,), (1,)), ((), ()))
TN = (((0,), (0,)), ((), ()))
NN = (((1,), (0,)), ((), ()))


def _tile(n, pref):
    best = None
    t = LANES
    while t <= min(n, pref):
        if n % t == 0:
            best = t
        t += LANES
    return best if best is not None else n


def _params(*sem):
    return pltpu.CompilerParams(dimension_semantics=sem, vmem_limit_bytes=VMEM_LIMIT)


def _split_dot(x, t, passes):
    acc = None
    r = x
    for i in range(passes):
        hi = r.astype(BF16)
        d = jnp.dot(hi, t, preferred_element_type=F32)
        acc = d if acc is None else acc + d
        if i + 1 < passes:
            r = r - hi.astype(F32)
    return acc


def _matmul(name, a, b, *, dims, grid, a_spec, b_spec, out_shapes, out_specs, acc_shape,
            epi=None, extras=(), extra_specs=()):
    nk = grid[2]
    n_ex = len(extras)
    n_out = len(out_shapes)

    def body(*refs):
        a_ref, b_ref = refs[0], refs[1]
        ex = refs[2:2 + n_ex]
        outs = refs[2 + n_ex:2 + n_ex + n_out]
        acc = refs[-1]
        k = pl.program_id(2)

        @pl.when(k == 0)
        def _():
            acc[...] = jnp.zeros_like(acc)

        acc[...] += lax.dot_general(a_ref[...].astype(BF16), b_ref[...].astype(BF16), dims,
                                    preferred_element_type=F32)

        @pl.when(k == nk - 1)
        def _():
            res = epi(acc[...], *[e[...] for e in ex]) if epi is not None else (acc[...],)
            for o, r in zip(outs, res):
                o[...] = r.astype(o.dtype)

    return pl.pallas_call(
        body, grid=grid, in_specs=[a_spec, b_spec, *extra_specs], out_specs=out_specs,
        out_shape=out_shapes, scratch_shapes=[pltpu.VMEM(acc_shape, F32)],
        compiler_params=_params("parallel", "parallel", "arbitrary"), name=name,
    )(a, b, *extras)


def _mm_nn_w(name, a, wg, layer, colshard, out_dtypes, epi=None, extras=(), extra_specs_fn=None):
    m = a.shape[0]
    _, _, r, c = wg.shape
    tm = min(512, m)
    if colshard:
        k, n = r, N_CHIPS * c
        tk, tn = k, _tile(c, 768)
        npc = c // tn
        b_spec = pl.BlockSpec((None, None, tk, tn), lambda i, j, kk: (layer, j // npc, kk, j % npc))
    else:
        k, n = N_CHIPS * r, c
        tk, tn = _tile(r, 1024), _tile(c, 512)
        kpc = r // tk
        b_spec = pl.BlockSpec((None, None, tk, tn), lambda i, j, kk: (layer, kk // kpc, kk % kpc, j))
    assert a.shape[1] == k
    grid = (m // tm, n // tn, k // tk)
    out_spec = pl.BlockSpec((tm, tn), lambda i, j, kk: (i, j))
    ex_specs = extra_specs_fn(tm, tn) if extra_specs_fn is not None else ()
    return _matmul(name, a, wg, dims=NN, grid=grid,
                   a_spec=pl.BlockSpec((tm, tk), lambda i, j, kk: (i, kk)), b_spec=b_spec,
                   out_shapes=[jax.ShapeDtypeStruct((m, n), d) for d in out_dtypes],
                   out_specs=[out_spec for _ in out_dtypes], acc_shape=(tm, tn),
                   epi=epi, extras=extras, extra_specs=ex_specs)


def _mm_nt_w(name, a, wg, layer, colshard, out_dtype, epi=None, extras=(), extra_specs_fn=None):
    m = a.shape[0]
    _, _, r, c = wg.shape
    tm = min(512, m)
    if colshard:
        kdim, n = N_CHIPS * c, r
        tk, tn = _tile(c, 1024), _tile(r, 512)
        kpc = c // tk
        b_spec = pl.BlockSpec((None, None, tn, tk), lambda i, j, kk: (layer, kk // kpc, j, kk % kpc))
    else:
        kdim, n = c, N_CHIPS * r
        tk, tn = c, _tile(r, 512)
        npc = r // tn
        b_spec = pl.BlockSpec((None, None, tn, tk), lambda i, j, kk: (layer, j // npc, j % npc, kk))
    assert a.shape[1] == kdim
    grid = (m // tm, n // tn, kdim // tk)
    ex_specs = extra_specs_fn(tm, tn) if extra_specs_fn is not None else ()
    return _matmul(name, a, wg, dims=NT, grid=grid,
                   a_spec=pl.BlockSpec((tm, tk), lambda i, j, kk: (i, kk)), b_spec=b_spec,
                   out_shapes=[jax.ShapeDtypeStruct((m, n), out_dtype)],
                   out_specs=[pl.BlockSpec((tm, tn), lambda i, j, kk: (i, j))], acc_shape=(tm, tn),
                   epi=epi, extras=extras, extra_specs=ex_specs)[0]


def _mm_tn(name, a, b, colshard):
    s, m = a.shape
    n = b.shape[1]
    tk = min(512, s)
    if colshard:
        c = n // N_CHIPS
        tm, tn = _tile(m, 512), _tile(c, 768)
        npc = c // tn
        out_shape = (N_CHIPS, m, c)
        out_spec = pl.BlockSpec((None, tm, tn), lambda i, j, kk: (j // npc, i, j % npc))
    else:
        r = m // N_CHIPS
        tm, tn = _tile(r, 512), _tile(n, 512)
        mpc = r // tm
        out_shape = (N_CHIPS, r, n)
        out_spec = pl.BlockSpec((None, tm, tn), lambda i, j, kk: (i // mpc, i % mpc, j))
    grid = (m // tm, n // tn, s // tk)
    return _matmul(name, a, b, dims=TN, grid=grid,
                   a_spec=pl.BlockSpec((tk, tm), lambda i, j, kk: (kk, i)),
                   b_spec=pl.BlockSpec((tk, tn), lambda i, j, kk: (kk, j)),
                   out_shapes=[jax.ShapeDtypeStruct(out_shape, F32)], out_specs=[out_spec],
                   acc_shape=(tm, tn))[0]


def _row_spec(ts, d):
    return pl.BlockSpec((ts, d), lambda i: (i, 0))


def _vec_spec(d):
    return pl.BlockSpec((1, d), lambda i: (0, 0))


def _norm_mod(name, x, g, sc, sh):
    s, d = x.shape
    ts = min(ROW_TILE, s)

    def body(x_ref, g_ref, sc_ref, sh_ref, h_ref):
        xv = x_ref[...]
        r = lax.rsqrt(jnp.mean(xv * xv, axis=-1, keepdims=True) + EPS)
        h_ref[...] = (xv * r * g_ref[...] * (1.0 + sc_ref[...]) + sh_ref[...]).astype(BF16)

    return pl.pallas_call(
        body, grid=(s // ts,), in_specs=[_row_spec(ts, d), _vec_spec(d), _vec_spec(d), _vec_spec(d)],
        out_specs=_row_spec(ts, d), out_shape=jax.ShapeDtypeStruct((s, d), BF16),
        compiler_params=_params("parallel"), name=name)(x, g, sc, sh)


def _norm_mod_bwd(name, dh, x, g, sc, dres):
    s, d = x.shape
    ts = min(ROW_TILE, s)

    def body(dh_ref, x_ref, g_ref, sc_ref, dres_ref, dx_ref, dsh_ref, dsc_ref, dg_ref):
        @pl.when(pl.program_id(0) == 0)
        def _():
            dsh_ref[...] = jnp.zeros_like(dsh_ref)
            dsc_ref[...] = jnp.zeros_like(dsc_ref)
            dg_ref[...] = jnp.zeros_like(dg_ref)

        xv = x_ref[...]
        r = lax.rsqrt(jnp.mean(xv * xv, axis=-1, keepdims=True) + EPS)
        xhat = xv * r
        dhv = dh_ref[...]
        gv = g_ref[...]
        opsc = 1.0 + sc_ref[...]
        dxhat = dhv * (gv * opsc)
        mdot = jnp.mean(dxhat * xhat, axis=-1, keepdims=True)
        dx_ref[...] = dres_ref[...] + r * (dxhat - xhat * mdot)
        dhx = dhv * xhat
        dsh_ref[...] += jnp.sum(dhv, axis=0, keepdims=True)
        dsc_ref[...] += jnp.sum(dhx * gv, axis=0, keepdims=True)
        dg_ref[...] += jnp.sum(dhx * opsc, axis=0, keepdims=True)

    vec = jax.ShapeDtypeStruct((1, d), F32)
    return pl.pallas_call(
        body, grid=(s // ts,),
        in_specs=[_row_spec(ts, d), _row_spec(ts, d), _vec_spec(d), _vec_spec(d), _row_spec(ts, d)],
        out_specs=[_row_spec(ts, d), _vec_spec(d), _vec_spec(d), _vec_spec(d)],
        out_shape=[jax.ShapeDtypeStruct((s, d), F32), vec, vec, vec],
        compiler_params=_params("arbitrary"), name=name)(dh, x, g, sc, dres)


def _gate_bwd(name, dx, m, gt):
    s, d = dx.shape
    ts = min(ROW_TILE, s)

    def body(dx_ref, m_ref, gt_ref, dm_ref, dgt_ref):
        @pl.when(pl.program_id(0) == 0)
        def _():
            dgt_ref[...] = jnp.zeros_like(dgt_ref)

        dxv = dx_ref[...]
        dm_ref[...] = (dxv * gt_ref[...]).astype(BF16)
        dgt_ref[...] += jnp.sum(dxv * m_ref[...].astype(F32), axis=0, keepdims=True)

    return pl.pallas_call(
        body, grid=(s // ts,), in_specs=[_row_spec(ts, d), _row_spec(ts, d), _vec_spec(d)],
        out_specs=[_row_spec(ts, d), _vec_spec(d)],
        out_shape=[jax.ShapeDtypeStruct((s, d), BF16), jax.ShapeDtypeStruct((1, d), F32)],
        compiler_params=_params("arbitrary"), name=name)(dx, m, gt)


def _loss_grad(name, y, target):
    s, d = y.shape
    ts = min(ROW_TILE, s)

    def body(y_ref, t_ref, dy_ref, sq_ref):
        @pl.when(pl.program_id(0) == 0)
        def _():
            sq_ref[...] = jnp.zeros_like(sq_ref)

        e = y_ref[...] - t_ref[...]
        dy_ref[...] = e * (1.0 / d)
        sq_ref[...] += jnp.sum(e * e, axis=0, keepdims=True)

    return pl.pallas_call(
        body, grid=(s // ts,), in_specs=[_row_spec(ts, d), _row_spec(ts, d)],
        out_specs=[_row_spec(ts, d), _vec_spec(d)],
        out_shape=[jax.ShapeDtypeStruct((s, d), F32), jax.ShapeDtypeStruct((1, d), F32)],
        compiler_params=_params("arbitrary"), name=name)(y, target)


def _adamw(name, w, g, m, v):
    r, c = w.shape
    tr = r
    for cand in (512, 256, 128, 64, 32, 16, 8):
        if r % cand == 0 and cand * c * 4 <= 2 * 1024 * 1024:
            tr = cand
            break

    def body(w_ref, g_ref, m_ref, v_ref, d_ref, mo_ref, vo_ref):
        gv = g_ref[...]
        m2 = ADAM_B1 * m_ref[...] + (1.0 - ADAM_B1) * gv
        v2 = ADAM_B2 * v_ref[...] + (1.0 - ADAM_B2) * (gv * gv)
        m_hat = m2 / (1.0 - ADAM_B1 ** ADAM_STEP)
        v_hat = v2 / (1.0 - ADAM_B2 ** ADAM_STEP)
        d_ref[...] = -ADAM_LR * (m_hat / (jnp.sqrt(v_hat) + ADAM_EPS) + ADAM_WD * w_ref[...])
        mo_ref[...] = m2
        vo_ref[...] = v2

    spec = pl.BlockSpec((tr, c), lambda i: (i, 0))
    shp = jax.ShapeDtypeStruct((r, c), F32)
    return pl.pallas_call(
        body, grid=(r // tr,), in_specs=[spec] * 4, out_specs=[spec] * 3, out_shape=[shp] * 3,
        compiler_params=_params("parallel"), name=name)(w, g, m, v)


def _sum_rows8(name, x):
    n, r, c = x.shape

    def body(x_ref, o_ref):
        acc = x_ref[0]
        for i in range(1, n):
            acc = acc + x_ref[i]
        o_ref[...] = acc

    return pl.pallas_call(
        body, out_shape=jax.ShapeDtypeStruct((r, c), F32),
        in_specs=[pl.BlockSpec(memory_space=pltpu.VMEM)], out_specs=pl.BlockSpec(memory_space=pltpu.VMEM),
        name=name)(x)


def _outer_sum(name, ct, dm):
    d, n_seq = ct.shape
    n = dm.shape[1]
    tr = min(256, d)

    def body(ct_ref, dm_ref, o_ref):
        acc = ct_ref[:, 0:1] * dm_ref[0:1, :]
        for s_i in range(1, n_seq):
            acc = acc + ct_ref[:, s_i:s_i + 1] * dm_ref[s_i:s_i + 1, :]
        o_ref[...] = acc

    return pl.pallas_call(
        body, grid=(d // tr,),
        in_specs=[pl.BlockSpec((tr, n_seq), lambda i: (i, 0)), pl.BlockSpec((n_seq, n), lambda i: (0, 0))],
        out_specs=pl.BlockSpec((tr, n), lambda i: (i, 0)), out_shape=jax.ShapeDtypeStruct((d, n), F32),
        compiler_params=_params("parallel"), name=name)(ct, dm)


def _ada_mod(name, c_act, w_ada_l, bias):
    n_seq, d = c_act.shape
    n = w_ada_l.shape[1]
    tn = _tile(n, 768)

    def epi(acc, b):
        return (acc + b,)

    return _matmul(name, c_act, w_ada_l, dims=NN, grid=(1, n // tn, 1),
                   a_spec=pl.BlockSpec((n_seq, d), lambda i, j, kk: (0, 0)),
                   b_spec=pl.BlockSpec((d, tn), lambda i, j, kk: (0, j)),
                   out_shapes=[jax.ShapeDtypeStruct((n_seq, n), F32)],
                   out_specs=[pl.BlockSpec((n_seq, tn), lambda i, j, kk: (0, j))], acc_shape=(n_seq, tn),
                   epi=epi, extras=(bias,), extra_specs=(pl.BlockSpec((1, tn), lambda i, j, kk: (0, j)),))[0]


def _silu_rows(name, c_row):
    d = c_row.shape[1]

    def body(c_ref, o_ref):
        cv = c_ref[...]
        o_ref[...] = jnp.broadcast_to(cv / (1.0 + jnp.exp(-cv)), (8, d))

    return pl.pallas_call(
        body, out_shape=jax.ShapeDtypeStruct((8, d), F32),
        in_specs=[pl.BlockSpec(memory_space=pltpu.VMEM)], out_specs=pl.BlockSpec(memory_space=pltpu.VMEM),
        name=name)(c_row)


def _sb_masks(t):
    row = lax.broadcasted_iota(jnp.int32, (t, t), 0)
    col = lax.broadcasted_iota(jnp.int32, (t, t), 1)
    strict = col < row
    tri = (row > col).astype(BF16)
    return strict, tri


def _strict_mask(rows, cols, row0):
    row = lax.broadcasted_iota(jnp.int32, (rows, cols), 0) + row0
    col = lax.broadcasted_iota(jnp.int32, (rows, cols), 1)
    return col < row


def _log_sigmoids(z):
    sp = jnp.log(1.0 + jnp.exp(-jnp.abs(z)))
    return jnp.minimum(z, 0.0) - sp, jnp.minimum(-z, 0.0) - sp, sp


def _sb_fwd(name, proj, n_pairs):
    s = proj.shape[0]
    t = min(SB_TILE, s)
    nq = s // t

    heads = [slice(h * HEAD_DIM, (h + 1) * HEAD_DIM) for h in range(LANES // HEAD_DIM)]

    def body(q_ref, k_ref, v_ref, o_ref):
        qi = pl.program_id(1)
        strict, tri = _sb_masks(t)
        qs = [(q_ref[:, sl] * SCALE).astype(BF16) for sl in heads]

        def block(kb, carry, masked):
            ks = pl.multiple_of(kb * t, t)
            ks_ = [k_ref[pl.ds(ks, t), sl].astype(BF16) for sl in heads]
            vs_ = [v_ref[pl.ds(ks, t), sl].astype(BF16) for sl in heads]
            zs = [lax.dot_general(q, k, NT, preferred_element_type=F32) for q, k in zip(qs, ks_)]
            lbs, lms = [], []
            for z in zs:
                lb, lm, _ = _log_sigmoids(z)
                if masked:
                    lm = jnp.where(strict, lm, 0.0)
                lbs.append(lb)
                lms.append(lm)
            css = [_split_dot(lm, tri, 2) for lm in lms]
            ws = []
            for lb, cs, (_, run) in zip(lbs, css, carry):
                w = jnp.exp(lb + cs + run)
                if masked:
                    w = jnp.where(strict, w, 0.0)
                ws.append(w.astype(BF16))
            pvs = [jnp.dot(w, v, preferred_element_type=F32) for w, v in zip(ws, vs_)]
            return tuple((acc + pv, run + jnp.sum(lm, axis=1, keepdims=True))
                         for (acc, run), pv, lm in zip(carry, pvs, lms))

        init = tuple((jnp.zeros((t, HEAD_DIM), F32), jnp.zeros((t, 1), F32)) for _ in heads)
        carry = block(qi, init, True)
        carry = lax.fori_loop(0, qi, lambda i, c: block(qi - 1 - i, c, False), carry)
        for (acc, _), sl in zip(carry, heads):
            o_ref[:, sl] = acc

    return pl.pallas_call(
        body, grid=(n_pairs, nq),
        in_specs=[pl.BlockSpec((t, LANES), lambda p, i: (i, p)),
                  pl.BlockSpec((s, LANES), lambda p, i: (0, n_pairs + p)),
                  pl.BlockSpec((s, LANES), lambda p, i: (0, 2 * n_pairs + p))],
        out_specs=pl.BlockSpec((t, LANES), lambda p, i: (i, p)),
        out_shape=jax.ShapeDtypeStruct((s, n_pairs * LANES), F32),
        compiler_params=_params("parallel", "arbitrary"), name=name)(proj, proj, proj)


def _sb_bwd(name, proj, o_sb, dmixed, n_pairs):
    s = proj.shape[0]
    t = min(SB_TILE, s)
    nq = s // t
    heads = [slice(h * HEAD_DIM, (h + 1) * HEAD_DIM) for h in range(LANES // HEAD_DIM)]

    def body(q_ref, k_ref, v_ref, o_ref, do_ref, dq_ref, dk_ref, dv_ref):
        qi = pl.program_id(1)

        @pl.when(qi == 0)
        def _():
            dk_ref[...] = jnp.zeros_like(dk_ref)
            dv_ref[...] = jnp.zeros_like(dv_ref)

        strict, tri = _sb_masks(t)
        qs = [(q_ref[:, sl] * SCALE).astype(BF16) for sl in heads]
        dos = [do_ref[:, sl].astype(BF16) for sl in heads]
        dsums = [jnp.sum(do.astype(F32) * o_ref[:, sl], axis=1, keepdims=True) for do, sl in zip(dos, heads)]

        def block(kb, carry, masked):
            ks = pl.multiple_of(kb * t, t)
            nh = range(len(heads))
            ks_ = [k_ref[pl.ds(ks, t), sl].astype(BF16) for sl in heads]
            vs_ = [v_ref[pl.ds(ks, t), sl].astype(BF16) for sl in heads]
            zs = [lax.dot_general(qs[h], ks_[h], NT, preferred_element_type=F32) for h in nh]
            dws = [lax.dot_general(dos[h], vs_[h], NT, preferred_element_type=F32) for h in nh]
            lbs, lms, betas, ombs = [], [], [], []
            for z in zs:
                ez = jnp.exp(-jnp.abs(z))
                sp = jnp.log(1.0 + ez)
                lb = jnp.minimum(z, 0.0) - sp
                lm = lb - z
                if masked:
                    lm = jnp.where(strict, lm, 0.0)
                inv = 1.0 / (1.0 + ez)
                pos = z >= 0.0
                lbs.append(lb)
                lms.append(lm)
                betas.append(jnp.where(pos, 1.0, ez) * inv)
                ombs.append(jnp.where(pos, ez, 1.0) * inv)
            css = [_split_dot(lm, tri, 2) for lm in lms]
            wbs, es = [], []
            for h in nh:
                w = jnp.exp(lbs[h] + css[h] + carry[h][1])
                if masked:
                    w = jnp.where(strict, w, 0.0)
                wb = w.astype(BF16)
                wbs.append(wb)
                es.append(wb.astype(F32) * dws[h])
            for h, sl in enumerate(heads):
                dv_ref[pl.ds(ks, t), sl] += lax.dot_general(wbs[h], dos[h], TN, preferred_element_type=F32)
            ecs = [_split_dot(e, tri, 2) for e in es]
            dzbs = []
            for h in nh:
                left = dsums[h] - (ecs[h] + es[h] + carry[h][2])
                dz = es[h] * ombs[h] - left * betas[h]
                if masked:
                    dz = jnp.where(strict, dz, 0.0)
                dzbs.append(dz.astype(BF16))
            out = []
            for h, sl in enumerate(heads):
                dk_ref[pl.ds(ks, t), sl] += lax.dot_general(dzbs[h], qs[h], TN, preferred_element_type=F32)
                dq = carry[h][0] + jnp.dot(dzbs[h], ks_[h], preferred_element_type=F32)
                out.append((dq, carry[h][1] + jnp.sum(lms[h], axis=1, keepdims=True),
                            carry[h][2] + jnp.sum(es[h], axis=1, keepdims=True)))
            return tuple(out)

        zero = jnp.zeros((t, 1), F32)
        init = tuple((jnp.zeros((t, HEAD_DIM), F32), zero, zero) for _ in heads)
        carry = block(qi, init, True)
        carry = lax.fori_loop(0, qi, lambda i, c: block(qi - 1 - i, c, False), carry)
        for (dq, _, _), sl in zip(carry, heads):
            dq_ref[:, sl] = dq * SCALE

    blk = pl.BlockSpec((t, LANES), lambda p, i: (i, p))
    full = pl.BlockSpec((s, LANES), lambda p, i: (0, p))
    shp = jax.ShapeDtypeStruct((s, n_pairs * LANES), F32)
    return pl.pallas_call(
        body, grid=(n_pairs, nq),
        in_specs=[blk,
                  pl.BlockSpec((s, LANES), lambda p, i: (0, n_pairs + p)),
                  pl.BlockSpec((s, LANES), lambda p, i: (0, 2 * n_pairs + p)),
                  blk, blk],
        out_specs=[blk, full, full], out_shape=[shp, shp, shp],
        compiler_params=_params("parallel", "arbitrary"), name=name)(proj, proj, proj, o_sb, dmixed)


def _group_mean_matrix():
    row = lax.broadcasted_iota(jnp.int32, (LANES, LANES), 0)
    col = lax.broadcasted_iota(jnp.int32, (LANES, LANES), 1)
    same_head = (row < HEAD_DIM) == (col < HEAD_DIM)
    return jnp.where(same_head, 1.0 / HEAD_DIM, 0.0).astype(BF16)


def _qk_norm(name, proj, gq2, gk2, n_pairs):
    s = proj.shape[0]
    ts = min(ROW_TILE, s)
    pb = PAD // ts
    c0 = 3 * n_pairs

    def body(q_ref, k_ref, v_ref, gq_ref, gk_ref, qn_ref, kn_ref, vp_ref):
        r_i = pl.program_id(1)

        @pl.when(r_i < pb)
        def _():
            qn_ref[...] = jnp.zeros_like(qn_ref)
            kn_ref[...] = jnp.zeros_like(kn_ref)
            vp_ref[...] = jnp.zeros_like(vp_ref)

        @pl.when(r_i >= pb)
        def _():
            gm = _group_mean_matrix()
            for src, g_ref, dst in ((q_ref, gq_ref, qn_ref), (k_ref, gk_ref, kn_ref)):
                xv = src[...]
                r = lax.rsqrt(_split_dot(xv * xv, gm, 3) + EPS)
                dst[...] = (xv * r * g_ref[...]).astype(BF16)
            vp_ref[...] = v_ref[...].astype(BF16)

    def in_spec(off):
        return pl.BlockSpec((ts, LANES), lambda p, i: (jnp.maximum(i - pb, 0), c0 + off * n_pairs + p))

    gspec = pl.BlockSpec((1, LANES), lambda p, i: (0, 0))
    ospec = pl.BlockSpec((ts, LANES), lambda p, i: (i, p))
    shp = jax.ShapeDtypeStruct((s + PAD, n_pairs * LANES), BF16)
    return pl.pallas_call(
        body, grid=(n_pairs, (s + PAD) // ts),
        in_specs=[in_spec(0), in_spec(1), in_spec(2), gspec, gspec],
        out_specs=[ospec, ospec, ospec], out_shape=[shp, shp, shp],
        compiler_params=_params("parallel", "arbitrary"), name=name)(proj, proj, proj, gq2, gk2)


def _qk_norm_bwd(name, dqn, dkn, proj, gq2, gk2, n_pairs):
    s = proj.shape[0]
    ts = min(ROW_TILE, s)
    pb = PAD // ts
    c0 = 3 * n_pairs
    n_r = s // ts

    def body(dqn_ref, dkn_ref, q_ref, k_ref, gq_ref, gk_ref, dq_ref, dk_ref, dgq_ref, dgk_ref):
        first = jnp.logical_and(pl.program_id(0) == 0, pl.program_id(1) == 0)
        last = jnp.logical_and(pl.program_id(0) == n_pairs - 1, pl.program_id(1) == n_r - 1)

        @pl.when(first)
        def _():
            dgq_ref[...] = jnp.zeros_like(dgq_ref)
            dgk_ref[...] = jnp.zeros_like(dgk_ref)

        gm = _group_mean_matrix()
        for dn_ref, x_ref, g_ref, dx_ref, dg_ref in ((dqn_ref, q_ref, gq_ref, dq_ref, dgq_ref),
                                                     (dkn_ref, k_ref, gk_ref, dk_ref, dgk_ref)):
            xv = x_ref[...]
            r = lax.rsqrt(_split_dot(xv * xv, gm, 3) + EPS)
            xhat = xv * r
            dn = dn_ref[...]
            tg = dn * g_ref[...]
            md = _split_dot(tg * xhat, gm, 3)
            dx_ref[...] = (r * (tg - xhat * md)).astype(BF16)
            dg_ref[...] += jnp.sum(dn * xhat, axis=0, keepdims=True)

        @pl.when(last)
        def _():
            for dg_ref in (dgq_ref, dgk_ref):
                gv = dg_ref[...]
                dg_ref[...] = gv + pltpu.roll(gv, HEAD_DIM, axis=1)

    def x_spec(off):
        return pl.BlockSpec((ts, LANES), lambda p, i: (i, c0 + off * n_pairs + p))

    dn_q = pl.BlockSpec((ts, LANES), lambda p, i: (i, p))
    dn_k = pl.BlockSpec((ts, LANES), lambda p, i: (i + pb, p))
    gspec = pl.BlockSpec((1, LANES), lambda p, i: (0, 0))
    ospec = pl.BlockSpec((ts, LANES), lambda p, i: (i, p))
    shp = jax.ShapeDtypeStruct((s, n_pairs * LANES), BF16)
    vec = jax.ShapeDtypeStruct((1, LANES), F32)
    return pl.pallas_call(
        body, grid=(n_pairs, n_r),
        in_specs=[dn_q, dn_k, x_spec(0), x_spec(1), gspec, gspec],
        out_specs=[ospec, ospec, gspec, gspec], out_shape=[shp, shp, vec, vec],
        compiler_params=_params("arbitrary", "arbitrary"), name=name)(dqn, dkn, proj, proj, gq2, gk2)


def _ca_scores(qc, kw, bias, n):
    sc = lax.dot_general(qc, kw, NT, preferred_element_type=F32) * SCALE + bias
    jpos = lax.broadcasted_iota(jnp.int32, (CHUNK, BAND), 1)
    return jnp.where(n * CHUNK + jpos >= PAD, sc, NEG_INF)


def _ca_fwd(name, qn, knp, vp, bias_t, n_pairs):
    sp_rows = knp.shape[0]
    s = sp_rows - PAD
    t = min(CA_TILE, s)
    cpb = t // CHUNK
    pb = PAD // t

    def body(q_ref, k_ref, v_ref, b_ref, o_ref):
        qi = pl.program_id(1)
        work = [(ci, h) for ci in range(cpb) for h in range(LANES // HEAD_DIM)]
        scs, vws = [], []
        for ci, h in work:
            n = qi * cpb + ci
            ws = pl.multiple_of(n * CHUNK, CHUNK)
            sl = slice(h * HEAD_DIM, (h + 1) * HEAD_DIM)
            scs.append(_ca_scores(q_ref[ci * CHUNK:(ci + 1) * CHUNK, sl], k_ref[pl.ds(ws, BAND), sl], b_ref[h], n))
            vws.append(v_ref[pl.ds(ws, BAND), sl])
        ps, dens = [], []
        for sc in scs:
            p = jnp.exp(sc - jnp.max(sc, axis=1, keepdims=True))
            dens.append(jnp.sum(p, axis=1, keepdims=True))
            ps.append(p.astype(BF16))
        for (ci, h), p, den, vw in zip(work, ps, dens, vws):
            sl = slice(h * HEAD_DIM, (h + 1) * HEAD_DIM)
            o_ref[ci * CHUNK:(ci + 1) * CHUNK, sl] = jnp.dot(p, vw, preferred_element_type=F32) / den

    full = pl.BlockSpec((sp_rows, LANES), lambda p, i: (0, p))
    return pl.pallas_call(
        body, grid=(n_pairs, s // t),
        in_specs=[pl.BlockSpec((t, LANES), lambda p, i: (i + pb, p)), full, full,
                  pl.BlockSpec((2, CHUNK, BAND), lambda p, i: (p, 0, 0))],
        out_specs=pl.BlockSpec((t, LANES), lambda p, i: (i, p)),
        out_shape=jax.ShapeDtypeStruct((s, n_pairs * LANES), F32),
        compiler_params=_params("parallel", "arbitrary"), name=name)(qn, knp, vp, bias_t)


def _ca_bwd(name, qn, knp, vp, bias_t, dmixed, n_pairs, col0):
    sp_rows = knp.shape[0]
    s = sp_rows - PAD
    t = min(CA_TILE, s)
    cpb = t // CHUNK
    pb = PAD // t

    def body(q_ref, k_ref, v_ref, b_ref, do_ref, dq_ref, dk_ref, dv_ref, db_ref):
        qi = pl.program_id(1)

        @pl.when(qi == 0)
        def _():
            dk_ref[...] = jnp.zeros_like(dk_ref)
            dv_ref[...] = jnp.zeros_like(dv_ref)
            db_ref[...] = jnp.zeros_like(db_ref)

        work = [(ci, h) for ci in range(cpb) for h in range(LANES // HEAD_DIM)]
        qcs, kws, dos, scs, dps, wss = [], [], [], [], [], []
        for ci, h in work:
            n = qi * cpb + ci
            ws = pl.multiple_of(n * CHUNK, CHUNK)
            sl = slice(h * HEAD_DIM, (h + 1) * HEAD_DIM)
            qc = q_ref[ci * CHUNK:(ci + 1) * CHUNK, sl]
            kw = k_ref[pl.ds(ws, BAND), sl]
            do = do_ref[ci * CHUNK:(ci + 1) * CHUNK, sl].astype(BF16)
            scs.append(_ca_scores(qc, kw, b_ref[h], n))
            dps.append(lax.dot_general(do, v_ref[pl.ds(ws, BAND), sl], NT, preferred_element_type=F32))
            qcs.append(qc)
            kws.append(kw)
            dos.append(do)
            wss.append(ws)
        pbs, dss = [], []
        for sc, dp in zip(scs, dps):
            p = jnp.exp(sc - jnp.max(sc, axis=1, keepdims=True))
            p = p / jnp.sum(p, axis=1, keepdims=True)
            dss.append(p * (dp - jnp.sum(p * dp, axis=1, keepdims=True)))
            pbs.append(p.astype(BF16))
        for i, (ci, h) in enumerate(work):
            sl = slice(h * HEAD_DIM, (h + 1) * HEAD_DIM)
            dsb = dss[i].astype(BF16)
            dv_ref[pl.ds(wss[i], BAND), sl] += lax.dot_general(pbs[i], dos[i], TN, preferred_element_type=F32)
            dk_ref[pl.ds(wss[i], BAND), sl] += lax.dot_general(dsb, qcs[i], TN, preferred_element_type=F32) * SCALE
            dq_ref[ci * CHUNK:(ci + 1) * CHUNK, sl] = jnp.dot(dsb, kws[i], preferred_element_type=F32) * SCALE
        for h in range(LANES // HEAD_DIM):
            total = dss[h]
            for ci in range(1, cpb):
                total = total + dss[ci * (LANES // HEAD_DIM) + h]
            db_ref[h] += total

    full_in = pl.BlockSpec((sp_rows, LANES), lambda p, i: (0, p))
    btile = pl.BlockSpec((2, CHUNK, BAND), lambda p, i: (p, 0, 0))
    pad_shape = jax.ShapeDtypeStruct((sp_rows, n_pairs * LANES), F32)
    return pl.pallas_call(
        body, grid=(n_pairs, s // t),
        in_specs=[pl.BlockSpec((t, LANES), lambda p, i: (i + pb, p)), full_in, full_in, btile,
                  pl.BlockSpec((t, LANES), lambda p, i: (i, col0 + p))],
        out_specs=[pl.BlockSpec((t, LANES), lambda p, i: (i, p)), full_in, full_in, btile],
        out_shape=[jax.ShapeDtypeStruct((s, n_pairs * LANES), F32), pad_shape, pad_shape,
                   jax.ShapeDtypeStruct(bias_t.shape, F32)],
        compiler_params=_params("parallel", "arbitrary"), name=name)(qn, knp, vp, bias_t, dmixed)


def _bias_tile(rb):
    n_h = rb.shape[0]
    far = jnp.broadcast_to(rb[:, 2 * REL_CLIP:], (n_h, PAD + CHUNK - REL_CLIP))
    near = rb[:, REL_CLIP - (CHUNK - 1):2 * REL_CLIP][:, ::-1]
    ext = jnp.concatenate([far, near], axis=1)
    return jnp.stack([ext[:, CHUNK - 1 - i:CHUNK - 1 - i + BAND] for i in range(CHUNK)], axis=1)


def _rel_bias_grad(name, db_t):
    _, n_h, _ = db_t.shape
    n_out = 3 * LANES
    assert CHUNK == 64 and REL_CLIP == 128 and LEFT_CHUNKS == 8

    def body(z_ref, o_ref):
        a_idx = lax.broadcasted_iota(jnp.int32, (LANES, n_out), 0)
        r_idx = lax.broadcasted_iota(jnp.int32, (LANES, n_out), 1)
        out = jnp.zeros((n_h, n_out), F32)
        for mm, base in ((0, 191), (1, 255), (2, 319)):
            b = LEFT_CHUNKS - mm
            acc = jnp.zeros((n_h, LANES), F32)
            for i in range(CHUNK):
                row = z_ref[i, :, b * CHUNK:(b + 1) * CHUNK]
                rowp = jnp.concatenate([row, jnp.zeros((n_h, LANES - CHUNK), F32)], axis=1)
                acc = acc + pltpu.roll(rowp, CHUNK - 1 - i, axis=1)
            place = (r_idx == jnp.minimum(base - a_idx, 2 * REL_CLIP)).astype(BF16)
            out = out + _split_dot(acc, place, 3)
        far = jnp.zeros((n_h, 1), F32)
        for i in range(CHUNK):
            far = far + jnp.sum(z_ref[i, :, 0:(LEFT_CHUNKS - 2) * CHUNK], axis=1, keepdims=True)
        lane = lax.broadcasted_iota(jnp.int32, (n_h, n_out), 1)
        o_ref[...] = out + jnp.where(lane == 2 * REL_CLIP, far, 0.0)

    return pl.pallas_call(
        body, out_shape=jax.ShapeDtypeStruct((n_h, n_out), F32),
        in_specs=[pl.BlockSpec(memory_space=pltpu.VMEM)], out_specs=pl.BlockSpec(memory_space=pltpu.VMEM),
        name=name)(db_t)


def _mesh_pos():
    return lax.axis_index("x"), lax.axis_index("y"), lax.axis_index("c")


def _other_chips(x, y):
    return [(1 - x, y), (x, 1 - y), (1 - x, 1 - y)]


def _all_gather_small(name, blk):
    m_per, n = blk.shape

    def body(x_ref, out_ref, send_sems, recv_sems, local_sem):
        x, y, c = _mesh_pos()
        me, sibling = (x, y, c), (x, y, 1 - c)
        chips = _other_chips(x, y)

        def rows(px, py, pc):
            return out_ref.at[pl.ds((4 * px + 2 * py + pc) * m_per, m_per), :]

        def copy(k, block, to, src=None):
            return pltpu.make_async_remote_copy(
                src_ref=rows(*block) if src is None else src, dst_ref=rows(*block),
                send_sem=send_sems.at[k], recv_sem=recv_sems.at[k], device_id=to, device_id_type=MESH)

        mine = pltpu.make_async_copy(x_ref, rows(*me), local_sem)
        mine.start()
        first = [copy(0, me, sibling, src=x_ref)]
        first += [copy(1 + j, me, (*chip, c), src=x_ref) for j, chip in enumerate(chips)]
        for cp in first:
            cp.start()
        passed = [copy(4 + j, (*chip, c), sibling) for j, chip in enumerate(chips)]
        for j, chip in enumerate(chips):
            copy(1 + j, (*chip, c), me).wait_recv()
            passed[j].start()
        copy(0, sibling, me).wait_recv()
        for j, chip in enumerate(chips):
            copy(4 + j, (*chip, 1 - c), me).wait_recv()
        for cp in first + passed:
            cp.wait_send()
        mine.wait()

    return pl.pallas_call(
        body, out_shape=jax.ShapeDtypeStruct((N_DEV * m_per, n), blk.dtype),
        in_specs=[pl.BlockSpec(memory_space=pltpu.VMEM)], out_specs=pl.BlockSpec(memory_space=pltpu.VMEM),
        scratch_shapes=[pltpu.SemaphoreType.DMA((7,)), pltpu.SemaphoreType.DMA((7,)), pltpu.SemaphoreType.DMA],
        compiler_params=pltpu.CompilerParams(vmem_limit_bytes=VMEM_LIMIT), name=name)(blk)


HBM_SPEC = pl.BlockSpec(memory_space=pltpu.HBM)


def _all_gather_weights(name, shards):
    n = len(shards)

    def body(*refs):
        srcs, outs = refs[:n], refs[n:2 * n]
        send_sems, recv_sems, local_sems = refs[2 * n:]
        x, y, c = _mesh_pos()
        me, sibling = (x, y, c), (x, y, 1 - c)
        chips = _other_chips(x, y)

        def win(t, px, py, pc):
            rh = shards[t].shape[1] // 2
            return outs[t].at[:, 2 * px + py, pl.ds(pc * rh, rh), :]

        def own(t):
            rh = shards[t].shape[1] // 2
            return srcs[t].at[:, pl.ds(c * rh, rh), :]

        def copy(t, k, block, to, src=None):
            return pltpu.make_async_remote_copy(
                src_ref=win(t, *block) if src is None else src, dst_ref=win(t, *block),
                send_sem=send_sems.at[t, k], recv_sem=recv_sems.at[t, k], device_id=to, device_id_type=MESH)

        mine = [pltpu.make_async_copy(own(t), win(t, *me), local_sems.at[t]) for t in range(n)]
        for cp in mine:
            cp.start()
        first = []
        for t in range(n):
            first.append(copy(t, 0, me, sibling, src=own(t)))
            first += [copy(t, 1 + j, me, (*chip, c), src=own(t)) for j, chip in enumerate(chips)]
        for cp in first:
            cp.start()
        passed = []
        for j, chip in enumerate(chips):
            for t in range(n):
                copy(t, 1 + j, (*chip, c), me).wait_recv()
                fwd = copy(t, 4 + j, (*chip, c), sibling)
                fwd.start()
                passed.append(fwd)
        for t in range(n):
            copy(t, 0, sibling, me).wait_recv()
            for j, chip in enumerate(chips):
                copy(t, 4 + j, (*chip, 1 - c), me).wait_recv()
        for cp in first + passed:
            cp.wait_send()
        for cp in mine:
            cp.wait()

    out_shape = [jax.ShapeDtypeStruct((w.shape[0], N_CHIPS, w.shape[1], w.shape[2]), w.dtype) for w in shards]
    return pl.pallas_call(
        body, out_shape=out_shape, in_specs=[HBM_SPEC] * n, out_specs=[HBM_SPEC] * n,
        scratch_shapes=[pltpu.SemaphoreType.DMA((n, 7)), pltpu.SemaphoreType.DMA((n, 7)),
                        pltpu.SemaphoreType.DMA((n,))],
        name=name)(*shards)


def _exchange_sibling_halves(name, grads):
    n = len(grads)

    def body(*refs):
        srcs, outs = refs[:n], refs[n:2 * n]
        send_sems, recv_sems = refs[2 * n:]
        x, y, c = _mesh_pos()
        cps = []
        for t in range(n):
            rh = grads[t].shape[1] // 2
            cps.append(pltpu.make_async_remote_copy(
                src_ref=srcs[t].at[:, pl.ds((1 - c) * rh, rh), :], dst_ref=outs[t],
                send_sem=send_sems.at[t], recv_sem=recv_sems.at[t], device_id=(x, y, 1 - c), device_id_type=MESH))
        for cp in cps:
            cp.start()
        for cp in cps:
            cp.wait()

    out_shape = [jax.ShapeDtypeStruct((g.shape[0], g.shape[1] // 2, g.shape[2]), g.dtype) for g in grads]
    return pl.pallas_call(
        body, out_shape=out_shape, in_specs=[HBM_SPEC] * n, out_specs=[HBM_SPEC] * n,
        scratch_shapes=[pltpu.SemaphoreType.DMA((n,)), pltpu.SemaphoreType.DMA((n,))],
        name=name)(*grads)


def _add_halves(name, g, recv, c_idx):
    nb, r, c = g.shape
    rh = r // 2
    tr = rh
    for cand in (512, 256, 128, 64):
        if rh % cand == 0:
            tr = cand
            break
    g4 = g.reshape(nb, 2, rh, c)

    def body(c_ref, g_ref, r_ref, o_ref):
        o_ref[...] = (g_ref[...] + r_ref[...]).astype(BF16)

    grid_spec = pltpu.PrefetchScalarGridSpec(
        num_scalar_prefetch=1, grid=(nb, rh // tr),
        in_specs=[pl.BlockSpec((None, None, tr, c), lambda j, i, cr: (j, cr[0], i, 0)),
                  pl.BlockSpec((None, tr, c), lambda j, i, cr: (j, i, 0))],
        out_specs=pl.BlockSpec((None, tr, c), lambda j, i, cr: (j, i, 0)))
    return pl.pallas_call(
        body, grid_spec=grid_spec, out_shape=jax.ShapeDtypeStruct((nb, rh, c), BF16),
        compiler_params=_params("parallel", "parallel"), name=name)(c_idx, g4, recv)


def _exchange_chips(name, parts):
    n = len(parts)

    def body(*refs):
        srcs, outs = refs[:n], refs[n:2 * n]
        send_sems, recv_sems, local_sems = refs[2 * n:]
        x, y, c = _mesh_pos()
        my_j = 2 * x + y
        chips = _other_chips(x, y)
        local = [pltpu.make_async_copy(srcs[t].at[my_j], outs[t].at[my_j], local_sems.at[t]) for t in range(n)]
        for cp in local:
            cp.start()
        cps = []
        for t in range(n):
            for k, (px, py) in enumerate(chips):
                cps.append(pltpu.make_async_remote_copy(
                    src_ref=srcs[t].at[2 * px + py], dst_ref=outs[t].at[my_j],
                    send_sem=send_sems.at[t, k], recv_sem=recv_sems.at[t, k],
                    device_id=(px, py, c), device_id_type=MESH))
        for cp in cps:
            cp.start()
        for cp in cps:
            cp.wait()
        for cp in local:
            cp.wait()

    out_shape = [jax.ShapeDtypeStruct(p.shape, p.dtype) for p in parts]
    return pl.pallas_call(
        body, out_shape=out_shape, in_specs=[HBM_SPEC] * n, out_specs=[HBM_SPEC] * n,
        scratch_shapes=[pltpu.SemaphoreType.DMA((n, 3)), pltpu.SemaphoreType.DMA((n, 3)),
                        pltpu.SemaphoreType.DMA((n,))],
        name=name)(*parts)


def _sum_chips(name, recv):
    nb, rh, c = recv.shape
    tr = rh
    for cand in (512, 256, 128, 64):
        if rh % cand == 0:
            tr = cand
            break

    def body(r_ref, o_ref):
        acc = r_ref[0].astype(F32)
        for j in range(1, nb):
            acc = acc + r_ref[j].astype(F32)
        o_ref[...] = acc

    return pl.pallas_call(
        body, grid=(rh // tr,), in_specs=[pl.BlockSpec((nb, tr, c), lambda i: (0, i, 0))],
        out_specs=pl.BlockSpec((tr, c), lambda i: (i, 0)), out_shape=jax.ShapeDtypeStruct((rh, c), F32),
        compiler_params=_params("parallel"), name=name)(recv)


def _share_with_sibling(name, halves, n_layers):
    flat = [h for per_w in halves for h in per_w]
    n = len(flat)
    n_w = len(halves)

    def body(*refs):
        srcs, outs = refs[:n], refs[n:n + n_w]
        send_sems, recv_sems, local_sems = refs[n + n_w:]
        x, y, c = _mesh_pos()
        local, cps = [], []
        for w_i in range(n_w):
            for l_i in range(n_layers):
                t = w_i * n_layers + l_i
                rh = flat[t].shape[0]
                dst = outs[w_i].at[l_i, pl.ds(c * rh, rh), :]
                local.append(pltpu.make_async_copy(srcs[t], dst, local_sems.at[t]))
                cps.append(pltpu.make_async_remote_copy(
                    src_ref=srcs[t], dst_ref=dst, send_sem=send_sems.at[t], recv_sem=recv_sems.at[t],
                    device_id=(x, y, 1 - c), device_id_type=MESH))
        for cp in local + cps:
            cp.start()
        for t, cp in enumerate(cps):
            cp.wait_send()
            rh = flat[t].shape[0]
            w_i, l_i = divmod(t, n_layers)
            pltpu.make_async_remote_copy(
                src_ref=srcs[t], dst_ref=outs[w_i].at[l_i, pl.ds((1 - c) * rh, rh), :],
                send_sem=send_sems.at[t], recv_sem=recv_sems.at[t],
                device_id=(x, y, 1 - c), device_id_type=MESH).wait_recv()
        for cp in local:
            cp.wait()

    out_shape = [jax.ShapeDtypeStruct((n_layers, 2 * per_w[0].shape[0], per_w[0].shape[1]), F32) for per_w in halves]
    return pl.pallas_call(
        body, out_shape=out_shape, in_specs=[HBM_SPEC] * n, out_specs=[HBM_SPEC] * n_w,
        scratch_shapes=[pltpu.SemaphoreType.DMA((n,)), pltpu.SemaphoreType.DMA((n,)),
                        pltpu.SemaphoreType.DMA((n,))],
        name=name)(*flat)


def _row(v):
    return v.reshape(1, -1)


def kernel(x, c, g_norm1, w_in, g_q, g_k, rel_bias, w_o, g_norm2, w1, w2, w_ada, b_ada, loss_target, m_g_norm1, m_w_in, m_g_q, m_g_k, m_rel_bias, m_w_o, m_g_norm2, m_w1, m_w2, m_w_ada, m_b_ada, v_g_norm1, v_w_in, v_g_q, v_g_k, v_rel_bias, v_w_o, v_g_norm2, v_w1, v_w2, v_w_ada, v_b_ada):
    _, s, d = x.shape
    n_layers = g_norm1.shape[0]
    n_pairs = (d // 2) // LANES
    n_ca_heads = rel_bias.shape[1]
    n_rel = rel_bias.shape[2]
    assert n_rel == 2 * REL_CLIP + 1 and g_q.shape[1] == HEAD_DIM and n_ca_heads == 2 * n_pairs
    ada_c = w_ada.shape[2]

    ax, ay, ac = _mesh_pos()
    chip = 2 * ax + ay
    dev = 4 * ax + 2 * ay + ac
    c_idx = jnp.reshape(ac, (1,)).astype(jnp.int32)

    c_act_all = _all_gather_small("ag_c", _silu_rows("silu_c", c)).reshape(N_DEV, 8, d)[:, 0, :]
    mod_parts = []
    for l in range(n_layers):
        bias = lax.dynamic_slice_in_dim(b_ada[l], chip * ada_c, ada_c).reshape(1, ada_c)
        mod_parts.append(_ada_mod(f"ada_mod{l}", c_act_all, w_ada[l], bias))
    mod_all = _all_gather_small("ag_mod", jnp.concatenate(mod_parts, axis=1))
    mod_all = mod_all.reshape(N_CHIPS, 2, N_DEV, n_layers, ada_c)[:, 0]
    mod_all = jnp.transpose(mod_all, (1, 2, 0, 3)).reshape(N_DEV, n_layers, N_CHIPS * ada_c)
    mod = lax.dynamic_index_in_dim(mod_all, dev, axis=0, keepdims=False)

    wg_in, wg_o, wg_1, wg_2 = _all_gather_weights(
        "ag_weights", [w_in.astype(BF16), w_o.astype(BF16), w1.astype(BF16), w2.astype(BF16)])

    xs = x[0]
    saved = []
    for l in range(n_layers):
        sh1, sc1, gt1, sh2, sc2, gt2 = [_row(mod[l, i * d:(i + 1) * d]) for i in range(6)]
        gq2 = _row(jnp.tile(g_q[l], 2))
        gk2 = _row(jnp.tile(g_k[l], 2))
        bias_t = _bias_tile(rel_bias[l])
        h1 = _norm_mod(f"norm1_{l}", xs, _row(g_norm1[l]), sc1, sh1)
        proj = _mm_nn_w(f"proj_{l}", h1, wg_in, l, True, [F32])[0]
        o_sb = _sb_fwd(f"sb_fwd_{l}", proj, n_pairs)
        qn, knp, vp = _qk_norm(f"qk_norm_{l}", proj, gq2, gk2, n_pairs)
        o_ca = _ca_fwd(f"ca_fwd_{l}", qn, knp, vp, bias_t, n_pairs)
        mixed = jnp.concatenate([o_sb, o_ca], axis=1)

        def res_epi(acc, res, gt):
            return res + gt * acc, acc

        def res_specs(tm, tn):
            return (pl.BlockSpec((tm, tn), lambda i, j, kk: (i, j)), pl.BlockSpec((1, tn), lambda i, j, kk: (0, j)))

        x1, m1 = _mm_nn_w(f"attn_out_{l}", mixed, wg_o, l, False, [F32, BF16], epi=res_epi,
                          extras=(xs, gt1), extra_specs_fn=res_specs)
        h2 = _norm_mod(f"norm2_{l}", x1, _row(g_norm2[l]), sc2, sh2)

        def act_epi(acc):
            r = jnp.maximum(acc, 0.0)
            return acc, r * r

        u, a = _mm_nn_w(f"mlp_up_{l}", h2, wg_1, l, True, [BF16, BF16], epi=act_epi)
        x2, m2 = _mm_nn_w(f"mlp_down_{l}", a, wg_2, l, False, [F32, BF16], epi=res_epi,
                          extras=(x1, gt2), extra_specs_fn=res_specs)
        saved.append(dict(x0=xs, h1=h1, proj=proj, o_sb=o_sb, qn=qn, knp=knp, vp=vp, bias_t=bias_t, mixed=mixed,
                          m1=m1, x1=x1, h2=h2, u=u, a=a, m2=m2, gq2=gq2, gk2=gk2,
                          sc1=sc1, gt1=gt1, sc2=sc2, gt2=gt2))
        xs = x2

    dx, sq = _loss_grad("loss", xs, loss_target[0])
    loss = lax.psum(0.5 * jnp.sum(sq) / d, ("x", "y", "c"))

    g_in, g_o, g_1, g_2 = [None] * n_layers, [None] * n_layers, [None] * n_layers, [None] * n_layers
    dmod, dg1, dg2, dgq, dgk, drel = [], [], [], [], [], []
    for l in reversed(range(n_layers)):
        sv = saved[l]
        dm2, dgt2 = _gate_bwd(f"gate2_bwd_{l}", dx, sv["m2"], sv["gt2"])

        def act_bwd_epi(acc, u_t):
            return (acc * (2.0 * jnp.maximum(u_t.astype(F32), 0.0)),)

        def tile_specs(tm, tn):
            return (pl.BlockSpec((tm, tn), lambda i, j, kk: (i, j)),)

        du = _mm_nt_w(f"mlp_down_bwd_{l}", dm2, wg_2, l, False, BF16, epi=act_bwd_epi, extras=(sv["u"],),
                      extra_specs_fn=tile_specs)
        g_2[l] = _mm_tn(f"w2_grad_{l}", sv["a"], dm2, False)
        dh2 = _mm_nt_w(f"mlp_up_bwd_{l}", du, wg_1, l, True, F32)
        g_1[l] = _mm_tn(f"w1_grad_{l}", sv["h2"], du, True)
        dx1, dsh2, dsc2, dgn2 = _norm_mod_bwd(f"norm2_bwd_{l}", dh2, sv["x1"], _row(g_norm2[l]), sv["sc2"], dx)
        dm1, dgt1 = _gate_bwd(f"gate1_bwd_{l}", dx1, sv["m1"], sv["gt1"])
        dmixed = _mm_nt_w(f"attn_out_bwd_{l}", dm1, wg_o, l, False, F32)
        g_o[l] = _mm_tn(f"wo_grad_{l}", sv["mixed"], dm1, False)
        dq_sb, dk_sb, dv_sb = _sb_bwd(f"sb_bwd_{l}", sv["proj"], sv["o_sb"], dmixed, n_pairs)
        dqn, dknp, dvp, dbias_t = _ca_bwd(f"ca_bwd_{l}", sv["qn"], sv["knp"], sv["vp"], sv["bias_t"], dmixed,
                                          n_pairs, n_pairs)
        dq_ca, dk_ca, dgq_l, dgk_l = _qk_norm_bwd(f"qk_norm_bwd_{l}", dqn, dknp, sv["proj"], sv["gq2"], sv["gk2"],
                                                  n_pairs)
        drel_l = _rel_bias_grad(f"rel_bias_grad_{l}", jnp.transpose(dbias_t, (1, 0, 2)))[:, :n_rel]
        dproj = jnp.concatenate([dq_sb.astype(BF16), dk_sb.astype(BF16), dv_sb.astype(BF16), dq_ca, dk_ca,
                                 dvp[PAD:].astype(BF16)], axis=1)
        g_in[l] = _mm_tn(f"win_grad_{l}", sv["h1"], dproj, True)
        dh1 = _mm_nt_w(f"proj_bwd_{l}", dproj, wg_in, l, True, F32)
        dx, dsh1, dsc1, dgn1 = _norm_mod_bwd(f"norm1_bwd_{l}", dh1, sv["x0"], _row(g_norm1[l]), sv["sc1"], dx1)
        dmod.insert(0, jnp.concatenate([dsh1, dsc1, dgt1, dsh2, dsc2, dgt2], axis=1)[0])
        dg1.insert(0, dgn1[0])
        dg2.insert(0, dgn2[0])
        dgq.insert(0, dgq_l[0, :HEAD_DIM])
        dgk.insert(0, dgk_l[0, :HEAD_DIM])
        drel.insert(0, drel_l.reshape(-1))
    grad_x = dx[None]

    small = jnp.concatenate([jnp.concatenate(dmod), jnp.concatenate(dg1), jnp.concatenate(dg2),
                             jnp.concatenate(dgq), jnp.concatenate(dgk), jnp.concatenate(drel)])
    n_small = small.shape[0]
    n_pack = -(-n_small // (8 * LANES)) * (8 * LANES)
    small = jnp.pad(small, (0, n_pack - n_small)).reshape(8, n_pack // 8)
    small_all = _all_gather_small("ag_small", small).reshape(N_DEV, 8, n_pack // 8)
    small_sum = _sum_rows8("sum_small", small_all).reshape(-1)
    dmod_all = small_all.reshape(N_DEV, n_pack)[:, :n_layers * 6 * d].reshape(N_DEV, n_layers, 6 * d)

    off = 0

    def take(n, shape):
        nonlocal off
        out = small_sum[off:off + n].reshape(shape)
        off += n
        return out

    grad_b_ada = take(n_layers * 6 * d, (n_layers, 6 * d))
    grad_g_norm1 = take(n_layers * d, (n_layers, d))
    grad_g_norm2 = take(n_layers * d, (n_layers, d))
    grad_g_q = take(n_layers * HEAD_DIM, (n_layers, HEAD_DIM))
    grad_g_k = take(n_layers * HEAD_DIM, (n_layers, HEAD_DIM))
    grad_rel_bias = take(n_layers * n_ca_heads * n_rel, (n_layers, n_ca_heads, n_rel))

    c_act_t = jnp.transpose(c_act_all)
    grad_w_ada = jnp.stack([
        _outer_sum(f"wada_grad_{l}", c_act_t,
                   lax.dynamic_slice_in_dim(dmod_all[:, l, :], chip * ada_c, ada_c, axis=1))
        for l in range(n_layers)])

    partial = [g for per_w in (g_in, g_o, g_1, g_2) for g in per_w]
    from_sibling = _exchange_sibling_halves("rs_sibling", partial)
    chip_sums = [_add_halves(f"rs_add_{i}", g, r, c_idx) for i, (g, r) in enumerate(zip(partial, from_sibling))]
    from_chips = _exchange_chips("rs_chips", chip_sums)
    reduced = [_sum_chips(f"rs_sum_{i}", r) for i, r in enumerate(from_chips)]
    halves = [reduced[i * n_layers:(i + 1) * n_layers] for i in range(4)]
    grad_w_in, grad_w_o, grad_w1, grad_w2 = _share_with_sibling("rs_share", halves, n_layers)

    def adam_big(name, w, g, m, v):
        shp = w.shape
        outs = _adamw(name, w.reshape(-1, shp[-1]), g.reshape(-1, shp[-1]), m.reshape(-1, shp[-1]),
                      v.reshape(-1, shp[-1]))
        return [o.reshape(shp) for o in outs]

    def pack(arrs):
        flat = jnp.concatenate([a.reshape(-1) for a in arrs])
        n = flat.shape[0]
        n_p = -(-n // (8 * LANES)) * (8 * LANES)
        return jnp.pad(flat, (0, n_p - n), constant_values=1.0).reshape(8, n_p // 8)

    small_w = [g_norm1, g_q, g_k, rel_bias, g_norm2, b_ada]
    small_g = [grad_g_norm1, grad_g_q, grad_g_k, grad_rel_bias, grad_g_norm2, grad_b_ada]
    small_m = [m_g_norm1, m_g_q, m_g_k, m_rel_bias, m_g_norm2, m_b_ada]
    small_v = [v_g_norm1, v_g_q, v_g_k, v_rel_bias, v_g_norm2, v_b_ada]
    packed = _adamw("adamw_small", pack(small_w), pack(small_g), pack(small_m), pack(small_v))

    def unpack(p):
        flat = p.reshape(-1)
        res, o = [], 0
        for a in small_w:
            res.append(flat[o:o + a.size].reshape(a.shape))
            o += a.size
        return res

    sd, sm, sv_ = unpack(packed[0]), unpack(packed[1]), unpack(packed[2])
    big = {
        "w_in": adam_big("adamw_w_in", w_in, grad_w_in, m_w_in, v_w_in),
        "w_o": adam_big("adamw_w_o", w_o, grad_w_o, m_w_o, v_w_o),
        "w1": adam_big("adamw_w1", w1, grad_w1, m_w1, v_w1),
        "w2": adam_big("adamw_w2", w2, grad_w2, m_w2, v_w2),
        "w_ada": adam_big("adamw_w_ada", w_ada, grad_w_ada, m_w_ada, v_w_ada),
    }

    def ordered(kind):
        sm_list = (sd, sm, sv_)[kind]
        return [sm_list[0], big["w_in"][kind], sm_list[1], sm_list[2], sm_list[3], big["w_o"][kind], sm_list[4],
                big["w1"][kind], big["w2"][kind], big["w_ada"][kind], sm_list[5]]

    grads = [grad_g_norm1, grad_w_in, grad_g_q, grad_g_k, grad_rel_bias, grad_w_o, grad_g_norm2, grad_w1, grad_w2,
             grad_w_ada, grad_b_ada]
    return (loss, grad_x, *grads, *ordered(0), *ordered(1), *ordered(2))
```

```python
import functools

import jax
import jax.numpy as jnp
from jax import lax
from jax.experimental import pallas as pl
from jax.experimental.pallas import tpu as pltpu

F32 = jnp.float32
BF16 = jnp.bfloat16
MESH = pl.DeviceIdType.MESH

EPS = 1e-6
NEG_INF = -1e30
HEAD_DIM = 64
CHUNK = 64
LEFT_CHUNKS = 8
PAD = LEFT_CHUNKS * CHUNK
BAND = PAD + CHUNK
REL_CLIP = 128
SCALE = HEAD_DIM ** -0.5
LANES = 128
N_CHIPS = 4
N_DEV = 8

ADAM_LR = 0.001
ADAM_B1 = 0.9
ADAM_B2 = 0.999
ADAM_EPS = 1e-08
ADAM_WD = 0.01
ADAM_STEP = 10

VMEM_LIMIT = 48 * 1024 * 1024
ROW_TILE = 256
SB_TILE = 256
CA_TILE = 256

NT = (((1,), (1,)), ((), ()))
TN = (((0,), (0,)), ((), ()))
NN = (((1,), (0,)), ((), ()))


def _tile(n, pref):
    best = None
    t = LANES
    while t <= min(n, pref):
        if n % t == 0:
            best = t
        t += LANES
    return best if best is not None else n


def _params(*sem):
    return pltpu.CompilerParams(dimension_semantics=sem, vmem_limit_bytes=VMEM_LIMIT)


def _split_dot(x, t, passes):
    acc = None
    r = x
    for i in range(passes):
        hi = r.astype(BF16)
        d = jnp.dot(hi, t, preferred_element_type=F32)
        acc = d if acc is None else acc + d
        if i + 1 < passes:
            r = r - hi.astype(F32)
    return acc


def _matmul(name, a, b, *, dims, grid, a_spec, b_spec, out_shapes, out_specs, acc_shape,
            epi=None, extras=(), extra_specs=()):
    nk = grid[2]
    n_ex = len(extras)
    n_out = len(out_shapes)

    def body(*refs):
        a_ref, b_ref = refs[0], refs[1]
        ex = refs[2:2 + n_ex]
        outs = refs[2 + n_ex:2 + n_ex + n_out]
        acc = refs[-1]
        k = pl.program_id(2)

        @pl.when(k == 0)
        def _():
            acc[...] = jnp.zeros_like(acc)

        acc[...] += lax.dot_general(a_ref[...].astype(BF16), b_ref[...].astype(BF16), dims,
                                    preferred_element_type=F32)

        @pl.when(k == nk - 1)
        def _():
            res = epi(acc[...], *[e[...] for e in ex]) if epi is not None else (acc[...],)
            for o, r in zip(outs, res):
                o[...] = r.astype(o.dtype)

    return pl.pallas_call(
        body, grid=grid, in_specs=[a_spec, b_spec, *extra_specs], out_specs=out_specs,
        out_shape=out_shapes, scratch_shapes=[pltpu.VMEM(acc_shape, F32)],
        compiler_params=_params("parallel", "parallel", "arbitrary"), name=name,
    )(a, b, *extras)


def _mm_nn_w(name, a, wg, layer, colshard, out_dtypes, epi=None, extras=(), extra_specs_fn=None):
    m = a.shape[0]
    _, _, r, c = wg.shape
    tm = min(512, m)
    if colshard:
        k, n = r, N_CHIPS * c
        tk, tn = k, _tile(c, 768)
        npc = c // tn
        b_spec = pl.BlockSpec((None, None, tk, tn), lambda i, j, kk: (layer, j // npc, kk, j % npc))
    else:
        k, n = N_CHIPS * r, c
        tk, tn = _tile(r, 1024), _tile(c, 512)
        kpc = r // tk
        b_spec = pl.BlockSpec((None, None, tk, tn), lambda i, j, kk: (layer, kk // kpc, kk % kpc, j))
    assert a.shape[1] == k
    grid = (m // tm, n // tn, k // tk)
    out_spec = pl.BlockSpec((tm, tn), lambda i, j, kk: (i, j))
    ex_specs = extra_specs_fn(tm, tn) if extra_specs_fn is not None else ()
    return _matmul(name, a, wg, dims=NN, grid=grid,
                   a_spec=pl.BlockSpec((tm, tk), lambda i, j, kk: (i, kk)), b_spec=b_spec,
                   out_shapes=[jax.ShapeDtypeStruct((m, n), d) for d in out_dtypes],
                   out_specs=[out_spec for _ in out_dtypes], acc_shape=(tm, tn),
                   epi=epi, extras=extras, extra_specs=ex_specs)


def _mm_nt_w(name, a, wg, layer, colshard, out_dtype, epi=None, extras=(), extra_specs_fn=None):
    m = a.shape[0]
    _, _, r, c = wg.shape
    tm = min(512, m)
    if colshard:
        kdim, n = N_CHIPS * c, r
        tk, tn = _tile(c, 1024), _tile(r, 512)
        kpc = c // tk
        b_spec = pl.BlockSpec((None, None, tn, tk), lambda i, j, kk: (layer, kk // kpc, j, kk % kpc))
    else:
        kdim, n = c, N_CHIPS * r
        tk, tn = c, _tile(r, 512)
        npc = r // tn
        b_spec = pl.BlockSpec((None, None, tn, tk), lambda i, j, kk: (layer, j // npc, j % npc, kk))
    assert a.shape[1] == kdim
    grid = (m // tm, n // tn, kdim // tk)
    ex_specs = extra_specs_fn(tm, tn) if extra_specs_fn is not None else ()
    return _matmul(name, a, wg, dims=NT, grid=grid,
                   a_spec=pl.BlockSpec((tm, tk), lambda i, j, kk: (i, kk)), b_spec=b_spec,
                   out_shapes=[jax.ShapeDtypeStruct((m, n), out_dtype)],
                   out_specs=[pl.BlockSpec((tm, tn), lambda i, j, kk: (i, j))], acc_shape=(tm, tn),
                   epi=epi, extras=extras, extra_specs=ex_specs)[0]


def _mm_tn(name, a, b, colshard):
    s, m = a.shape
    n = b.shape[1]
    tk = min(512, s)
    if colshard:
        c = n // N_CHIPS
        tm, tn = _tile(m, 512), _tile(c, 768)
        npc = c // tn
        out_shape = (N_CHIPS, m, c)
        out_spec = pl.BlockSpec((None, tm, tn), lambda i, j, kk: (j // npc, i, j % npc))
    else:
        r = m // N_CHIPS
        tm, tn = _tile(r, 512), _tile(n, 512)
        mpc = r // tm
        out_shape = (N_CHIPS, r, n)
        out_spec = pl.BlockSpec((None, tm, tn), lambda i, j, kk: (i // mpc, i % mpc, j))
    grid = (m // tm, n // tn, s // tk)
    return _matmul(name, a, b, dims=TN, grid=grid,
                   a_spec=pl.BlockSpec((tk, tm), lambda i, j, kk: (kk, i)),
                   b_spec=pl.BlockSpec((tk, tn), lambda i, j, kk: (kk, j)),
                   out_shapes=[jax.ShapeDtypeStruct(out_shape, F32)], out_specs=[out_spec],
                   acc_shape=(tm, tn))[0]


def _row_spec(ts, d):
    return pl.BlockSpec((ts, d), lambda i: (i, 0))


def _vec_spec(d):
    return pl.BlockSpec((1, d), lambda i: (0, 0))


def _norm_mod(name, x, g, sc, sh):
    s, d = x.shape
    ts = min(ROW_TILE, s)

    def body(x_ref, g_ref, sc_ref, sh_ref, h_ref):
        xv = x_ref[...]
        r = lax.rsqrt(jnp.mean(xv * xv, axis=-1, keepdims=True) + EPS)
        h_ref[...] = (xv * r * g_ref[...] * (1.0 + sc_ref[...]) + sh_ref[...]).astype(BF16)

    return pl.pallas_call(
        body, grid=(s // ts,), in_specs=[_row_spec(ts, d), _vec_spec(d), _vec_spec(d), _vec_spec(d)],
        out_specs=_row_spec(ts, d), out_shape=jax.ShapeDtypeStruct((s, d), BF16),
        compiler_params=_params("parallel"), name=name)(x, g, sc, sh)


def _norm_mod_bwd(name, dh, x, g, sc, dres):
    s, d = x.shape
    ts = min(ROW_TILE, s)

    def body(dh_ref, x_ref, g_ref, sc_ref, dres_ref, dx_ref, dsh_ref, dsc_ref, dg_ref):
        @pl.when(pl.program_id(0) == 0)
        def _():
            dsh_ref[...] = jnp.zeros_like(dsh_ref)
            dsc_ref[...] = jnp.zeros_like(dsc_ref)
            dg_ref[...] = jnp.zeros_like(dg_ref)

        xv = x_ref[...]
        r = lax.rsqrt(jnp.mean(xv * xv, axis=-1, keepdims=True) + EPS)
        xhat = xv * r
        dhv = dh_ref[...]
        gv = g_ref[...]
        opsc = 1.0 + sc_ref[...]
        dxhat = dhv * (gv * opsc)
        mdot = jnp.mean(dxhat * xhat, axis=-1, keepdims=True)
        dx_ref[...] = dres_ref[...] + r * (dxhat - xhat * mdot)
        dhx = dhv * xhat
        dsh_ref[...] += jnp.sum(dhv, axis=0, keepdims=True)
        dsc_ref[...] += jnp.sum(dhx * gv, axis=0, keepdims=True)
        dg_ref[...] += jnp.sum(dhx * opsc, axis=0, keepdims=True)

    vec = jax.ShapeDtypeStruct((1, d), F32)
    return pl.pallas_call(
        body, grid=(s // ts,),
        in_specs=[_row_spec(ts, d), _row_spec(ts, d), _vec_spec(d), _vec_spec(d), _row_spec(ts, d)],
        out_specs=[_row_spec(ts, d), _vec_spec(d), _vec_spec(d), _vec_spec(d)],
        out_shape=[jax.ShapeDtypeStruct((s, d), F32), vec, vec, vec],
        compiler_params=_params("arbitrary"), name=name)(dh, x, g, sc, dres)


def _gate_bwd(name, dx, m, gt):
    s, d = dx.shape
    ts = min(ROW_TILE, s)

    def body(dx_ref, m_ref, gt_ref, dm_ref, dgt_ref):
        @pl.when(pl.program_id(0) == 0)
        def _():
            dgt_ref[...] = jnp.zeros_like(dgt_ref)

        dxv = dx_ref[...]
        dm_ref[...] = (dxv * gt_ref[...]).astype(BF16)
        dgt_ref[...] += jnp.sum(dxv * m_ref[...].astype(F32), axis=0, keepdims=True)

    return pl.pallas_call(
        body, grid=(s // ts,), in_specs=[_row_spec(ts, d), _row_spec(ts, d), _vec_spec(d)],
        out_specs=[_row_spec(ts, d), _vec_spec(d)],
        out_shape=[jax.ShapeDtypeStruct((s, d), BF16), jax.ShapeDtypeStruct((1, d), F32)],
        compiler_params=_params("arbitrary"), name=name)(dx, m, gt)


def _loss_grad(name, y, target):
    s, d = y.shape
    ts = min(ROW_TILE, s)

    def body(y_ref, t_ref, dy_ref, sq_ref):
        @pl.when(pl.program_id(0) == 0)
        def _():
            sq_ref[...] = jnp.zeros_like(sq_ref)

        e = y_ref[...] - t_ref[...]
        dy_ref[...] = e * (1.0 / d)
        sq_ref[...] += jnp.sum(e * e, axis=0, keepdims=True)

    return pl.pallas_call(
        body, grid=(s // ts,), in_specs=[_row_spec(ts, d), _row_spec(ts, d)],
        out_specs=[_row_spec(ts, d), _vec_spec(d)],
        out_shape=[jax.ShapeDtypeStruct((s, d), F32), jax.ShapeDtypeStruct((1, d), F32)],
        compiler_params=_params("arbitrary"), name=name)(y, target)


def _adamw(name, w, g, m, v):
    r, c = w.shape
    tr = r
    for cand in (512, 256, 128, 64, 32, 16, 8):
        if r % cand == 0 and cand * c * 4 <= 2 * 1024 * 1024:
            tr = cand
            break

    def body(w_ref, g_ref, m_ref, v_ref, d_ref, mo_ref, vo_ref):
        gv = g_ref[...]
        m2 = ADAM_B1 * m_ref[...] + (1.0 - ADAM_B1) * gv
        v2 = ADAM_B2 * v_ref[...] + (1.0 - ADAM_B2) * (gv * gv)
        m_hat = m2 / (1.0 - ADAM_B1 ** ADAM_STEP)
        v_hat = v2 / (1.0 - ADAM_B2 ** ADAM_STEP)
        d_ref[...] = -ADAM_LR * (m_hat / (jnp.sqrt(v_hat) + ADAM_EPS) + ADAM_WD * w_ref[...])
        mo_ref[...] = m2
        vo_ref[...] = v2

    spec = pl.BlockSpec((tr, c), lambda i: (i, 0))
    shp = jax.ShapeDtypeStruct((r, c), F32)
    return pl.pallas_call(
        body, grid=(r // tr,), in_specs=[spec] * 4, out_specs=[spec] * 3, out_shape=[shp] * 3,
        compiler_params=_params("parallel"), name=name)(w, g, m, v)


def _sum_rows8(name, x):
    n, r, c = x.shape

    def body(x_ref, o_ref):
        acc = x_ref[0]
        for i in range(1, n):
            acc = acc + x_ref[i]
        o_ref[...] = acc

    return pl.pallas_call(
        body, out_shape=jax.ShapeDtypeStruct((r, c), F32),
        in_specs=[pl.BlockSpec(memory_space=pltpu.VMEM)], out_specs=pl.BlockSpec(memory_space=pltpu.VMEM),
        name=name)(x)


def _outer_sum(name, ct, dm):
    d, n_seq = ct.shape
    n = dm.shape[1]
    tr = min(256, d)

    def body(ct_ref, dm_ref, o_ref):
        acc = ct_ref[:, 0:1] * dm_ref[0:1, :]
        for s_i in range(1, n_seq):
            acc = acc + ct_ref[:, s_i:s_i + 1] * dm_ref[s_i:s_i + 1, :]
        o_ref[...] = acc

    return pl.pallas_call(
        body, grid=(d // tr,),
        in_specs=[pl.BlockSpec((tr, n_seq), lambda i: (i, 0)), pl.BlockSpec((n_seq, n), lambda i: (0, 0))],
        out_specs=pl.BlockSpec((tr, n), lambda i: (i, 0)), out_shape=jax.ShapeDtypeStruct((d, n), F32),
        compiler_params=_params("parallel"), name=name)(ct, dm)


def _ada_mod(name, c_act, w_ada_l, bias):
    n_seq, d = c_act.shape
    n = w_ada_l.shape[1]
    tn = _tile(n, 768)

    def epi(acc, b):
        return (acc + b,)

    return _matmul(name, c_act, w_ada_l, dims=NN, grid=(1, n // tn, 1),
                   a_spec=pl.BlockSpec((n_seq, d), lambda i, j, kk: (0, 0)),
                   b_spec=pl.BlockSpec((d, tn), lambda i, j, kk: (0, j)),
                   out_shapes=[jax.ShapeDtypeStruct((n_seq, n), F32)],
                   out_specs=[pl.BlockSpec((n_seq, tn), lambda i, j, kk: (0, j))], acc_shape=(n_seq, tn),
                   epi=epi, extras=(bias,), extra_specs=(pl.BlockSpec((1, tn), lambda i, j, kk: (0, j)),))[0]


def _silu_rows(name, c_row):
    d = c_row.shape[1]

    def body(c_ref, o_ref):
        cv = c_ref[...]
        o_ref[...] = jnp.broadcast_to(cv / (1.0 + jnp.exp(-cv)), (8, d))

    return pl.pallas_call(
        body, out_shape=jax.ShapeDtypeStruct((8, d), F32),
        in_specs=[pl.BlockSpec(memory_space=pltpu.VMEM)], out_specs=pl.BlockSpec(memory_space=pltpu.VMEM),
        name=name)(c_row)


def _sb_masks(t):
    row = lax.broadcasted_iota(jnp.int32, (t, t), 0)
    col = lax.broadcasted_iota(jnp.int32, (t, t), 1)
    strict = col < row
    tri = (row > col).astype(BF16)
    return strict, tri


def _col_minus_row(t):
    return lax.broadcasted_iota(jnp.int32, (t, t), 1) - lax.broadcasted_iota(jnp.int32, (t, t), 0)


def _log_sigmoids(z):
    sp = jnp.log(1.0 + jnp.exp(-jnp.abs(z)))
    return jnp.minimum(z, 0.0) - sp, jnp.minimum(-z, 0.0) - sp, sp


def _sb_fwd(name, proj, n_pairs):
    s = proj.shape[0]
    t = min(SB_TILE, s)
    nq = s // t

    heads = [slice(h * HEAD_DIM, (h + 1) * HEAD_DIM) for h in range(LANES // HEAD_DIM)]

    assert nq % 2 == 0

    def body(q_ref, k_ref, v_ref, o_ref):
        qi = pl.program_id(1)
        _, tri = _sb_masks(t)
        diff = _col_minus_row(t)
        qs = [(q_ref[:, sl] * SCALE).astype(BF16) for sl in heads]
        jd = qi // 2
        chains = [(sb, h) for sb in (1, 0) for h in range(len(heads))]

        def step(j, carry, masked):
            ks = [pl.multiple_of((2 * j + sb) * t, t) for sb, _ in chains]
            k_ = [k_ref[pl.ds(ks[i], t), heads[h]].astype(BF16) for i, (_, h) in enumerate(chains)]
            v_ = [v_ref[pl.ds(ks[i], t), heads[h]].astype(BF16) for i, (_, h) in enumerate(chains)]
            zs = [lax.dot_general(qs[h], k_[i], NT, preferred_element_type=F32) for i, (_, h) in enumerate(chains)]
            lbs, lms, keeps = [], [], []
            for (sb, _), z in zip(chains, zs):
                lb, lm, _ = _log_sigmoids(z)
                keep = None
                if masked:
                    keep = diff < (qi - 2 * j - sb) * t
                    lm = jnp.where(keep, lm, 0.0)
                lbs.append(lb)
                lms.append(lm)
                keeps.append(keep)
            css = [_split_dot(lm, tri, 2) for lm in lms]
            sums = [jnp.sum(lm, axis=1, keepdims=True) for lm in lms]
            ws = []
            for i, (sb, h) in enumerate(chains):
                run = carry[h][1] if sb == 1 else carry[h][1] + sums[h]
                w = jnp.exp(lbs[i] + css[i] + run)
                if masked:
                    w = jnp.where(keeps[i], w, 0.0)
                ws.append(w.astype(BF16))
            pvs = [jnp.dot(w, v, preferred_element_type=F32) for w, v in zip(ws, v_)]
            nh = len(heads)
            return tuple((carry[h][0] + pvs[h] + pvs[nh + h], carry[h][1] + sums[h] + sums[nh + h]) for h in range(nh))

        init = tuple((jnp.zeros((t, HEAD_DIM), F32), jnp.zeros((t, 1), F32)) for _ in heads)
        carry = step(jd, init, True)
        carry = lax.fori_loop(0, jd, lambda i, c: step(jd - 1 - i, c, False), carry)
        for (acc, _), sl in zip(carry, heads):
            o_ref[:, sl] = acc

    return pl.pallas_call(
        body, grid=(n_pairs, nq),
        in_specs=[pl.BlockSpec((t, LANES), lambda p, i: (i, p)),
                  pl.BlockSpec((s, LANES), lambda p, i: (0, n_pairs + p)),
                  pl.BlockSpec((s, LANES), lambda p, i: (0, 2 * n_pairs + p))],
        out_specs=pl.BlockSpec((t, LANES), lambda p, i: (i, p)),
        out_shape=jax.ShapeDtypeStruct((s, n_pairs * LANES), F32),
        compiler_params=_params("parallel", "arbitrary"), name=name)(proj, proj, proj)


def _sb_bwd(name, proj, o_sb, dmixed, n_pairs):
    s = proj.shape[0]
    t = min(SB_TILE, s)
    nq = s // t
    heads = [slice(h * HEAD_DIM, (h + 1) * HEAD_DIM) for h in range(LANES // HEAD_DIM)]

    def body(q_ref, k_ref, v_ref, o_ref, do_ref, dq_ref, dk_ref, dv_ref):
        qi = pl.program_id(1)

        @pl.when(qi == 0)
        def _():
            dk_ref[...] = jnp.zeros_like(dk_ref)
            dv_ref[...] = jnp.zeros_like(dv_ref)

        _, tri = _sb_masks(t)
        diff = _col_minus_row(t)
        qs = [(q_ref[:, sl] * SCALE).astype(BF16) for sl in heads]
        dos = [do_ref[:, sl].astype(BF16) for sl in heads]
        dsums = [jnp.sum(do.astype(F32) * o_ref[:, sl], axis=1, keepdims=True) for do, sl in zip(dos, heads)]
        jd = qi // 2
        nh = len(heads)
        chains = [(sb, h) for sb in (1, 0) for h in range(nh)]

        def step(j, carry, masked):
            ks = [pl.multiple_of((2 * j + sb) * t, t) for sb, _ in chains]
            k_ = [k_ref[pl.ds(ks[i], t), heads[h]].astype(BF16) for i, (_, h) in enumerate(chains)]
            v_ = [v_ref[pl.ds(ks[i], t), heads[h]].astype(BF16) for i, (_, h) in enumerate(chains)]
            zs = [lax.dot_general(qs[h], k_[i], NT, preferred_element_type=F32) for i, (_, h) in enumerate(chains)]
            dws = [lax.dot_general(dos[h], v_[i], NT, preferred_element_type=F32) for i, (_, h) in enumerate(chains)]
            lbs, lms, betas, ombs, keeps = [], [], [], [], []
            for (sb, _), z in zip(chains, zs):
                ez = jnp.exp(-jnp.abs(z))
                lb = jnp.minimum(z, 0.0) - jnp.log(1.0 + ez)
                lm = lb - z
                keep = None
                if masked:
                    keep = diff < (qi - 2 * j - sb) * t
                    lm = jnp.where(keep, lm, 0.0)
                inv = 1.0 / (1.0 + ez)
                pos = z >= 0.0
                lbs.append(lb)
                lms.append(lm)
                keeps.append(keep)
                betas.append(jnp.where(pos, 1.0, ez) * inv)
                ombs.append(jnp.where(pos, ez, 1.0) * inv)
            css = [_split_dot(lm, tri, 2) for lm in lms]
            sums_f = [jnp.sum(lm, axis=1, keepdims=True) for lm in lms]
            wbs, es = [], []
            for i, (sb, h) in enumerate(chains):
                run_f = carry[h][1] if sb == 1 else carry[h][1] + sums_f[h]
                w = jnp.exp(lbs[i] + css[i] + run_f)
                if masked:
                    w = jnp.where(keeps[i], w, 0.0)
                wb = w.astype(BF16)
                wbs.append(wb)
                es.append(wb.astype(F32) * dws[i])
            for i, (_, h) in enumerate(chains):
                dv_ref[pl.ds(ks[i], t), heads[h]] += lax.dot_general(wbs[i], dos[h], TN, preferred_element_type=F32)
            ecs = [_split_dot(e, tri, 2) for e in es]
            sums_e = [jnp.sum(e, axis=1, keepdims=True) for e in es]
            dzbs = []
            for i, (sb, h) in enumerate(chains):
                run_e = carry[h][2] if sb == 1 else carry[h][2] + sums_e[h]
                left = dsums[h] - (ecs[i] + es[i] + run_e)
                dz = es[i] * ombs[i] - left * betas[i]
                if masked:
                    dz = jnp.where(keeps[i], dz, 0.0)
                dzbs.append(dz.astype(BF16))
            dqs = []
            for i, (_, h) in enumerate(chains):
                dk_ref[pl.ds(ks[i], t), heads[h]] += lax.dot_general(dzbs[i], qs[h], TN, preferred_element_type=F32)
                dqs.append(jnp.dot(dzbs[i], k_[i], preferred_element_type=F32))
            return tuple((carry[h][0] + dqs[h] + dqs[nh + h], carry[h][1] + sums_f[h] + sums_f[nh + h],
                          carry[h][2] + sums_e[h] + sums_e[nh + h]) for h in range(nh))

        zero = jnp.zeros((t, 1), F32)
        init = tuple((jnp.zeros((t, HEAD_DIM), F32), zero, zero) for _ in heads)
        carry = step(jd, init, True)
        carry = lax.fori_loop(0, jd, lambda i, c: step(jd - 1 - i, c, False), carry)
        for (dq, _, _), sl in zip(carry, heads):
            dq_ref[:, sl] = dq * SCALE

    blk = pl.BlockSpec((t, LANES), lambda p, i: (i, p))
    full = pl.BlockSpec((s, LANES), lambda p, i: (0, p))
    shp = jax.ShapeDtypeStruct((s, n_pairs * LANES), F32)
    return pl.pallas_call(
        body, grid=(n_pairs, nq),
        in_specs=[blk,
                  pl.BlockSpec((s, LANES), lambda p, i: (0, n_pairs + p)),
                  pl.BlockSpec((s, LANES), lambda p, i: (0, 2 * n_pairs + p)),
                  blk, blk],
        out_specs=[blk, full, full], out_shape=[shp, shp, shp],
        compiler_params=_params("parallel", "arbitrary"), name=name)(proj, proj, proj, o_sb, dmixed)


def _group_mean_matrix():
    row = lax.broadcasted_iota(jnp.int32, (LANES, LANES), 0)
    col = lax.broadcasted_iota(jnp.int32, (LANES, LANES), 1)
    same_head = (row < HEAD_DIM) == (col < HEAD_DIM)
    return jnp.where(same_head, 1.0 / HEAD_DIM, 0.0).astype(BF16)


def _qk_norm(name, proj, gq2, gk2, n_pairs):
    s = proj.shape[0]
    ts = min(ROW_TILE, s)
    pb = PAD // ts
    c0 = 3 * n_pairs

    def body(q_ref, k_ref, v_ref, gq_ref, gk_ref, qn_ref, kn_ref, vp_ref):
        r_i = pl.program_id(1)

        @pl.when(r_i < pb)
        def _():
            qn_ref[...] = jnp.zeros_like(qn_ref)
            kn_ref[...] = jnp.zeros_like(kn_ref)
            vp_ref[...] = jnp.zeros_like(vp_ref)

        @pl.when(r_i >= pb)
        def _():
            gm = _group_mean_matrix()
            for src, g_ref, dst in ((q_ref, gq_ref, qn_ref), (k_ref, gk_ref, kn_ref)):
                xv = src[...]
                r = lax.rsqrt(_split_dot(xv * xv, gm, 3) + EPS)
                dst[...] = (xv * r * g_ref[...]).astype(BF16)
            vp_ref[...] = v_ref[...].astype(BF16)

    def in_spec(off):
        return pl.BlockSpec((ts, LANES), lambda p, i: (jnp.maximum(i - pb, 0), c0 + off * n_pairs + p))

    gspec = pl.BlockSpec((1, LANES), lambda p, i: (0, 0))
    ospec = pl.BlockSpec((ts, LANES), lambda p, i: (i, p))
    shp = jax.ShapeDtypeStruct((s + PAD, n_pairs * LANES), BF16)
    return pl.pallas_call(
        body, grid=(n_pairs, (s + PAD) // ts),
        in_specs=[in_spec(0), in_spec(1), in_spec(2), gspec, gspec],
        out_specs=[ospec, ospec, ospec], out_shape=[shp, shp, shp],
        compiler_params=_params("parallel", "arbitrary"), name=name)(proj, proj, proj, gq2, gk2)


def _qk_norm_bwd(name, dqn, dkn, proj, gq2, gk2, n_pairs):
    s = proj.shape[0]
    ts = min(ROW_TILE, s)
    pb = PAD // ts
    c0 = 3 * n_pairs
    n_r = s // ts

    def body(dqn_ref, dkn_ref, q_ref, k_ref, gq_ref, gk_ref, dq_ref, dk_ref, dgq_ref, dgk_ref):
        first = jnp.logical_and(pl.program_id(0) == 0, pl.program_id(1) == 0)
        last = jnp.logical_and(pl.program_id(0) == n_pairs - 1, pl.program_id(1) == n_r - 1)

        @pl.when(first)
        def _():
            dgq_ref[...] = jnp.zeros_like(dgq_ref)
            dgk_ref[...] = jnp.zeros_like(dgk_ref)

        gm = _group_mean_matrix()
        for dn_ref, x_ref, g_ref, dx_ref, dg_ref in ((dqn_ref, q_ref, gq_ref, dq_ref, dgq_ref),
                                                     (dkn_ref, k_ref, gk_ref, dk_ref, dgk_ref)):
            xv = x_ref[...]
            r = lax.rsqrt(_split_dot(xv * xv, gm, 3) + EPS)
            xhat = xv * r
            dn = dn_ref[...]
            tg = dn * g_ref[...]
            md = _split_dot(tg * xhat, gm, 3)
            dx_ref[...] = (r * (tg - xhat * md)).astype(BF16)
            dg_ref[...] += jnp.sum(dn * xhat, axis=0, keepdims=True)

        @pl.when(last)
        def _():
            for dg_ref in (dgq_ref, dgk_ref):
                gv = dg_ref[...]
                dg_ref[...] = gv + pltpu.roll(gv, HEAD_DIM, axis=1)

    def x_spec(off):
        return pl.BlockSpec((ts, LANES), lambda p, i: (i, c0 + off * n_pairs + p))

    dn_q = pl.BlockSpec((ts, LANES), lambda p, i: (i, p))
    dn_k = pl.BlockSpec((ts, LANES), lambda p, i: (i + pb, p))
    gspec = pl.BlockSpec((1, LANES), lambda p, i: (0, 0))
    ospec = pl.BlockSpec((ts, LANES), lambda p, i: (i, p))
    shp = jax.ShapeDtypeStruct((s, n_pairs * LANES), BF16)
    vec = jax.ShapeDtypeStruct((1, LANES), F32)
    return pl.pallas_call(
        body, grid=(n_pairs, n_r),
        in_specs=[dn_q, dn_k, x_spec(0), x_spec(1), gspec, gspec],
        out_specs=[ospec, ospec, gspec, gspec], out_shape=[shp, shp, vec, vec],
        compiler_params=_params("arbitrary", "arbitrary"), name=name)(dqn, dkn, proj, proj, gq2, gk2)


def _ca_scores(qc, kw, bias, n):
    sc = lax.dot_general(qc, kw, NT, preferred_element_type=F32) * SCALE + bias
    jpos = lax.broadcasted_iota(jnp.int32, (CHUNK, BAND), 1)
    return jnp.where(n * CHUNK + jpos >= PAD, sc, NEG_INF)


def _ca_fwd(name, qn, knp, vp, bias_t, n_pairs):
    sp_rows = knp.shape[0]
    s = sp_rows - PAD
    t = min(CA_TILE, s)
    cpb = t // CHUNK
    pb = PAD // t

    def body(q_ref, k_ref, v_ref, b_ref, o_ref):
        qi = pl.program_id(1)
        work = [(ci, h) for ci in range(cpb) for h in range(LANES // HEAD_DIM)]
        scs, vws = [], []
        for ci, h in work:
            n = qi * cpb + ci
            ws = pl.multiple_of(n * CHUNK, CHUNK)
            sl = slice(h * HEAD_DIM, (h + 1) * HEAD_DIM)
            scs.append(_ca_scores(q_ref[ci * CHUNK:(ci + 1) * CHUNK, sl], k_ref[pl.ds(ws, BAND), sl], b_ref[h], n))
            vws.append(v_ref[pl.ds(ws, BAND), sl])
        ps, dens = [], []
        for sc in scs:
            p = jnp.exp(sc - jnp.max(sc, axis=1, keepdims=True))
            dens.append(jnp.sum(p, axis=1, keepdims=True))
            ps.append(p.astype(BF16))
        for (ci, h), p, den, vw in zip(work, ps, dens, vws):
            sl = slice(h * HEAD_DIM, (h + 1) * HEAD_DIM)
            o_ref[ci * CHUNK:(ci + 1) * CHUNK, sl] = jnp.dot(p, vw, preferred_element_type=F32) / den

    full = pl.BlockSpec((sp_rows, LANES), lambda p, i: (0, p))
    return pl.pallas_call(
        body, grid=(n_pairs, s // t),
        in_specs=[pl.BlockSpec((t, LANES), lambda p, i: (i + pb, p)), full, full,
                  pl.BlockSpec((2, CHUNK, BAND), lambda p, i: (p, 0, 0))],
        out_specs=pl.BlockSpec((t, LANES), lambda p, i: (i, p)),
        out_shape=jax.ShapeDtypeStruct((s, n_pairs * LANES), F32),
        compiler_params=_params("parallel", "arbitrary"), name=name)(qn, knp, vp, bias_t)


def _ca_bwd(name, qn, knp, vp, bias_t, dmixed, n_pairs, col0):
    sp_rows = knp.shape[0]
    s = sp_rows - PAD
    t = min(CA_TILE, s)
    cpb = t // CHUNK
    pb = PAD // t

    def body(q_ref, k_ref, v_ref, b_ref, do_ref, dq_ref, dk_ref, dv_ref, db_ref):
        qi = pl.program_id(1)

        @pl.when(qi == 0)
        def _():
            dk_ref[...] = jnp.zeros_like(dk_ref)
            dv_ref[...] = jnp.zeros_like(dv_ref)
            db_ref[...] = jnp.zeros_like(db_ref)

        work = [(ci, h) for ci in range(cpb) for h in range(LANES // HEAD_DIM)]
        qcs, kws, dos, scs, dps, wss = [], [], [], [], [], []
        for ci, h in work:
            n = qi * cpb + ci
            ws = pl.multiple_of(n * CHUNK, CHUNK)
            sl = slice(h * HEAD_DIM, (h + 1) * HEAD_DIM)
            qc = q_ref[ci * CHUNK:(ci + 1) * CHUNK, sl]
            kw = k_ref[pl.ds(ws, BAND), sl]
            do = do_ref[ci * CHUNK:(ci + 1) * CHUNK, sl].astype(BF16)
            scs.append(_ca_scores(qc, kw, b_ref[h], n))
            dps.append(lax.dot_general(do, v_ref[pl.ds(ws, BAND), sl], NT, preferred_element_type=F32))
            qcs.append(qc)
            kws.append(kw)
            dos.append(do)
            wss.append(ws)
        pbs, dss = [], []
        for sc, dp in zip(scs, dps):
            p = jnp.exp(sc - jnp.max(sc, axis=1, keepdims=True))
            p = p / jnp.sum(p, axis=1, keepdims=True)
            dss.append(p * (dp - jnp.sum(p * dp, axis=1, keepdims=True)))
            pbs.append(p.astype(BF16))
        for i, (ci, h) in enumerate(work):
            sl = slice(h * HEAD_DIM, (h + 1) * HEAD_DIM)
            dsb = dss[i].astype(BF16)
            dv_ref[pl.ds(wss[i], BAND), sl] += lax.dot_general(pbs[i], dos[i], TN, preferred_element_type=F32)
            dk_ref[pl.ds(wss[i], BAND), sl] += lax.dot_general(dsb, qcs[i], TN, preferred_element_type=F32) * SCALE
            dq_ref[ci * CHUNK:(ci + 1) * CHUNK, sl] = jnp.dot(dsb, kws[i], preferred_element_type=F32) * SCALE
        for h in range(LANES // HEAD_DIM):
            total = dss[h]
            for ci in range(1, cpb):
                total = total + dss[ci * (LANES // HEAD_DIM) + h]
            db_ref[h] += total

    full_in = pl.BlockSpec((sp_rows, LANES), lambda p, i: (0, p))
    btile = pl.BlockSpec((2, CHUNK, BAND), lambda p, i: (p, 0, 0))
    pad_shape = jax.ShapeDtypeStruct((sp_rows, n_pairs * LANES), F32)
    return pl.pallas_call(
        body, grid=(n_pairs, s // t),
        in_specs=[pl.BlockSpec((t, LANES), lambda p, i: (i + pb, p)), full_in, full_in, btile,
                  pl.BlockSpec((t, LANES), lambda p, i: (i, col0 + p))],
        out_specs=[pl.BlockSpec((t, LANES), lambda p, i: (i, p)), full_in, full_in, btile],
        out_shape=[jax.ShapeDtypeStruct((s, n_pairs * LANES), F32), pad_shape, pad_shape,
                   jax.ShapeDtypeStruct(bias_t.shape, F32)],
        compiler_params=_params("parallel", "arbitrary"), name=name)(qn, knp, vp, bias_t, dmixed)


def _bias_tile(rb):
    n_h = rb.shape[0]
    far = jnp.broadcast_to(rb[:, 2 * REL_CLIP:], (n_h, PAD + CHUNK - REL_CLIP))
    near = rb[:, REL_CLIP - (CHUNK - 1):2 * REL_CLIP][:, ::-1]
    ext = jnp.concatenate([far, near], axis=1)
    return jnp.stack([ext[:, CHUNK - 1 - i:CHUNK - 1 - i + BAND] for i in range(CHUNK)], axis=1)


def _rel_bias_grad(name, db_t):
    _, n_h, _ = db_t.shape
    n_out = 3 * LANES
    assert CHUNK == 64 and REL_CLIP == 128 and LEFT_CHUNKS == 8

    def body(z_ref, o_ref):
        a_idx = lax.broadcasted_iota(jnp.int32, (LANES, n_out), 0)
        r_idx = lax.broadcasted_iota(jnp.int32, (LANES, n_out), 1)
        out = jnp.zeros((n_h, n_out), F32)
        for mm, base in ((0, 191), (1, 255), (2, 319)):
            b = LEFT_CHUNKS - mm
            acc = jnp.zeros((n_h, LANES), F32)
            for i in range(CHUNK):
                row = z_ref[i, :, b * CHUNK:(b + 1) * CHUNK]
                rowp = jnp.concatenate([row, jnp.zeros((n_h, LANES - CHUNK), F32)], axis=1)
                acc = acc + pltpu.roll(rowp, CHUNK - 1 - i, axis=1)
            place = (r_idx == jnp.minimum(base - a_idx, 2 * REL_CLIP)).astype(BF16)
            out = out + _split_dot(acc, place, 3)
        far = jnp.zeros((n_h, 1), F32)
        for i in range(CHUNK):
            far = far + jnp.sum(z_ref[i, :, 0:(LEFT_CHUNKS - 2) * CHUNK], axis=1, keepdims=True)
        lane = lax.broadcasted_iota(jnp.int32, (n_h, n_out), 1)
        o_ref[...] = out + jnp.where(lane == 2 * REL_CLIP, far, 0.0)

    return pl.pallas_call(
        body, out_shape=jax.ShapeDtypeStruct((n_h, n_out), F32),
        in_specs=[pl.BlockSpec(memory_space=pltpu.VMEM)], out_specs=pl.BlockSpec(memory_space=pltpu.VMEM),
        name=name)(db_t)


def _mesh_pos():
    return lax.axis_index("x"), lax.axis_index("y"), lax.axis_index("c")


def _other_chips(x, y):
    return [(1 - x, y), (x, 1 - y), (1 - x, 1 - y)]


def _all_gather_small(name, blk):
    m_per, n = blk.shape

    def body(x_ref, out_ref, send_sems, recv_sems, local_sem):
        x, y, c = _mesh_pos()
        me, sibling = (x, y, c), (x, y, 1 - c)
        chips = _other_chips(x, y)

        def rows(px, py, pc):
            return out_ref.at[pl.ds((4 * px + 2 * py + pc) * m_per, m_per), :]

        def copy(k, block, to, src=None):
            return pltpu.make_async_remote_copy(
                src_ref=rows(*block) if src is None else src, dst_ref=rows(*block),
                send_sem=send_sems.at[k], recv_sem=recv_sems.at[k], device_id=to, device_id_type=MESH)

        mine = pltpu.make_async_copy(x_ref, rows(*me), local_sem)
        mine.start()
        first = [copy(0, me, sibling, src=x_ref)]
        first += [copy(1 + j, me, (*chip, c), src=x_ref) for j, chip in enumerate(chips)]
        for cp in first:
            cp.start()
        passed = [copy(4 + j, (*chip, c), sibling) for j, chip in enumerate(chips)]
        for j, chip in enumerate(chips):
            copy(1 + j, (*chip, c), me).wait_recv()
            passed[j].start()
        copy(0, sibling, me).wait_recv()
        for j, chip in enumerate(chips):
            copy(4 + j, (*chip, 1 - c), me).wait_recv()
        for cp in first + passed:
            cp.wait_send()
        mine.wait()

    return pl.pallas_call(
        body, out_shape=jax.ShapeDtypeStruct((N_DEV * m_per, n), blk.dtype),
        in_specs=[pl.BlockSpec(memory_space=pltpu.VMEM)], out_specs=pl.BlockSpec(memory_space=pltpu.VMEM),
        scratch_shapes=[pltpu.SemaphoreType.DMA((7,)), pltpu.SemaphoreType.DMA((7,)), pltpu.SemaphoreType.DMA],
        compiler_params=pltpu.CompilerParams(vmem_limit_bytes=VMEM_LIMIT), name=name)(blk)


HBM_SPEC = pl.BlockSpec(memory_space=pltpu.HBM)


def _all_gather_weights(name, shards):
    n = len(shards)

    def body(*refs):
        srcs, outs = refs[:n], refs[n:2 * n]
        send_sems, recv_sems, local_sems = refs[2 * n:]
        x, y, c = _mesh_pos()
        me, sibling = (x, y, c), (x, y, 1 - c)
        chips = _other_chips(x, y)

        def win(t, px, py, pc):
            rh = shards[t].shape[1] // 2
            return outs[t].at[:, 2 * px + py, pl.ds(pc * rh, rh), :]

        def own(t):
            rh = shards[t].shape[1] // 2
            return srcs[t].at[:, pl.ds(c * rh, rh), :]

        def copy(t, k, block, to, src=None):
            return pltpu.make_async_remote_copy(
                src_ref=win(t, *block) if src is None else src, dst_ref=win(t, *block),
                send_sem=send_sems.at[t, k], recv_sem=recv_sems.at[t, k], device_id=to, device_id_type=MESH)

        mine = [pltpu.make_async_copy(own(t), win(t, *me), local_sems.at[t]) for t in range(n)]
        for cp in mine:
            cp.start()
        first = []
        for t in range(n):
            first.append(copy(t, 0, me, sibling, src=own(t)))
            first += [copy(t, 1 + j, me, (*chip, c), src=own(t)) for j, chip in enumerate(chips)]
        for cp in first:
            cp.start()
        passed = []
        for j, chip in enumerate(chips):
            for t in range(n):
                copy(t, 1 + j, (*chip, c), me).wait_recv()
                fwd = copy(t, 4 + j, (*chip, c), sibling)
                fwd.start()
                passed.append(fwd)
        for t in range(n):
            copy(t, 0, sibling, me).wait_recv()
            for j, chip in enumerate(chips):
                copy(t, 4 + j, (*chip, 1 - c), me).wait_recv()
        for cp in first + passed:
            cp.wait_send()
        for cp in mine:
            cp.wait()

    out_shape = [jax.ShapeDtypeStruct((w.shape[0], N_CHIPS, w.shape[1], w.shape[2]), w.dtype) for w in shards]
    return pl.pallas_call(
        body, out_shape=out_shape, in_specs=[HBM_SPEC] * n, out_specs=[HBM_SPEC] * n,
        scratch_shapes=[pltpu.SemaphoreType.DMA((n, 7)), pltpu.SemaphoreType.DMA((n, 7)),
                        pltpu.SemaphoreType.DMA((n,))],
        name=name)(*shards)


def _exchange_sibling_halves(name, grads):
    n = len(grads)

    def body(*refs):
        srcs, outs = refs[:n], refs[n:2 * n]
        send_sems, recv_sems = refs[2 * n:]
        x, y, c = _mesh_pos()
        cps = []
        for t in range(n):
            rh = grads[t].shape[1] // 2
            cps.append(pltpu.make_async_remote_copy(
                src_ref=srcs[t].at[:, pl.ds((1 - c) * rh, rh), :], dst_ref=outs[t],
                send_sem=send_sems.at[t], recv_sem=recv_sems.at[t], device_id=(x, y, 1 - c), device_id_type=MESH))
        for cp in cps:
            cp.start()
        for cp in cps:
            cp.wait()

    out_shape = [jax.ShapeDtypeStruct((g.shape[0], g.shape[1] // 2, g.shape[2]), g.dtype) for g in grads]
    return pl.pallas_call(
        body, out_shape=out_shape, in_specs=[HBM_SPEC] * n, out_specs=[HBM_SPEC] * n,
        scratch_shapes=[pltpu.SemaphoreType.DMA((n,)), pltpu.SemaphoreType.DMA((n,))],
        name=name)(*grads)


def _add_halves(name, g, recv, c_idx):
    nb, r, c = g.shape
    rh = r // 2
    tr = rh
    for cand in (512, 256, 128, 64):
        if rh % cand == 0:
            tr = cand
            break
    g4 = g.reshape(nb, 2, rh, c)

    def body(c_ref, g_ref, r_ref, o_ref):
        o_ref[...] = (g_ref[...] + r_ref[...]).astype(BF16)

    grid_spec = pltpu.PrefetchScalarGridSpec(
        num_scalar_prefetch=1, grid=(nb, rh // tr),
        in_specs=[pl.BlockSpec((None, None, tr, c), lambda j, i, cr: (j, cr[0], i, 0)),
                  pl.BlockSpec((None, tr, c), lambda j, i, cr: (j, i, 0))],
        out_specs=pl.BlockSpec((None, tr, c), lambda j, i, cr: (j, i, 0)))
    return pl.pallas_call(
        body, grid_spec=grid_spec, out_shape=jax.ShapeDtypeStruct((nb, rh, c), BF16),
        compiler_params=_params("parallel", "parallel"), name=name)(c_idx, g4, recv)


def _exchange_chips(name, parts, n_layers):
    n = len(parts)
    n_w = n // n_layers

    def body(*refs):
        srcs, outs = refs[:n], refs[n:n + n_w]
        send_sems, recv_sems = refs[n + n_w:]
        x, y, c = _mesh_pos()
        cps = []
        for t in range(n):
            w_i, l_i = divmod(t, n_layers)
            for k, (px, py) in enumerate(_other_chips(x, y)):
                cps.append(pltpu.make_async_remote_copy(
                    src_ref=srcs[t].at[2 * px + py], dst_ref=outs[w_i].at[l_i, k],
                    send_sem=send_sems.at[t, k], recv_sem=recv_sems.at[t, k],
                    device_id=(px, py, c), device_id_type=MESH))
        for cp in cps:
            cp.start()
        for cp in cps:
            cp.wait()

    out_shape = [jax.ShapeDtypeStruct((n_layers, 3) + parts[w_i * n_layers].shape[1:], BF16) for w_i in range(n_w)]
    return pl.pallas_call(
        body, out_shape=out_shape, in_specs=[HBM_SPEC] * n, out_specs=[HBM_SPEC] * n_w,
        scratch_shapes=[pltpu.SemaphoreType.DMA((n, 3)), pltpu.SemaphoreType.DMA((n, 3))],
        name=name)(*parts)


def _sum_chips(name, own_parts, recv, pos_idx):
    n_layers = len(own_parts)
    _, _, rh, c = recv.shape
    tr = rh
    for cand in (512, 256, 128, 64):
        if rh % cand == 0:
            tr = cand
            break
    nb = rh // tr

    def body(pos_ref, *refs):
        own_refs, r_ref, o_ref = refs[:n_layers], refs[n_layers], refs[n_layers + 1]
        for l_i in range(n_layers):
            @pl.when(pl.program_id(0) == l_i)
            def _():
                acc = own_refs[l_i][...].astype(F32)
                for k in range(3):
                    acc = acc + r_ref[k].astype(F32)
                o_ref[...] = acc

    grid_spec = pltpu.PrefetchScalarGridSpec(
        num_scalar_prefetch=1, grid=(n_layers, nb),
        in_specs=[pl.BlockSpec((None, tr, c), lambda l, i, pr: (pr[0], i, 0)) for _ in range(n_layers)]
        + [pl.BlockSpec((None, 3, tr, c), lambda l, i, pr: (l, 0, i, 0))],
        out_specs=pl.BlockSpec((None, tr, c), lambda l, i, pr: (l, pr[1] * nb + i, 0)))
    return pl.pallas_call(
        body, grid_spec=grid_spec, out_shape=jax.ShapeDtypeStruct((n_layers, 2 * rh, c), F32),
        compiler_params=_params("parallel", "parallel"), name=name)(pos_idx, *own_parts, recv)


def _share_with_sibling(name, grads):
    n = len(grads)

    def body(*refs):
        outs = refs[n:2 * n]
        send_sems, recv_sems = refs[2 * n:]
        x, y, c = _mesh_pos()
        cps, landed = [], []
        for t in range(n):
            rh = grads[t].shape[1] // 2
            mine = outs[t].at[:, pl.ds(c * rh, rh), :]
            theirs = outs[t].at[:, pl.ds((1 - c) * rh, rh), :]
            cps.append(pltpu.make_async_remote_copy(
                src_ref=mine, dst_ref=mine, send_sem=send_sems.at[t], recv_sem=recv_sems.at[t],
                device_id=(x, y, 1 - c), device_id_type=MESH))
            landed.append(pltpu.make_async_remote_copy(
                src_ref=mine, dst_ref=theirs, send_sem=send_sems.at[t], recv_sem=recv_sems.at[t],
                device_id=(x, y, 1 - c), device_id_type=MESH))
        for cp in cps:
            cp.start()
        for cp, ld in zip(cps, landed):
            cp.wait_send()
            ld.wait_recv()

    out_shape = [jax.ShapeDtypeStruct(g.shape, g.dtype) for g in grads]
    return pl.pallas_call(
        body, out_shape=out_shape, in_specs=[HBM_SPEC] * n, out_specs=[HBM_SPEC] * n,
        input_output_aliases={t: t for t in range(n)},
        scratch_shapes=[pltpu.SemaphoreType.DMA((n,)), pltpu.SemaphoreType.DMA((n,))],
        name=name)(*grads)


def _row(v):
    return v.reshape(1, -1)


def kernel(x, c, g_norm1, w_in, g_q, g_k, rel_bias, w_o, g_norm2, w1, w2, w_ada, b_ada, loss_target, m_g_norm1, m_w_in, m_g_q, m_g_k, m_rel_bias, m_w_o, m_g_norm2, m_w1, m_w2, m_w_ada, m_b_ada, v_g_norm1, v_w_in, v_g_q, v_g_k, v_rel_bias, v_w_o, v_g_norm2, v_w1, v_w2, v_w_ada, v_b_ada):
    _, s, d = x.shape
    n_layers = g_norm1.shape[0]
    n_pairs = (d // 2) // LANES
    n_ca_heads = rel_bias.shape[1]
    n_rel = rel_bias.shape[2]
    assert n_rel == 2 * REL_CLIP + 1 and g_q.shape[1] == HEAD_DIM and n_ca_heads == 2 * n_pairs
    ada_c = w_ada.shape[2]

    ax, ay, ac = _mesh_pos()
    chip = 2 * ax + ay
    dev = 4 * ax + 2 * ay + ac
    c_idx = jnp.reshape(ac, (1,)).astype(jnp.int32)

    c_act_all = _all_gather_small("ag_c", _silu_rows("silu_c", c)).reshape(N_DEV, 8, d)[:, 0, :]
    mod_parts = []
    for l in range(n_layers):
        bias = lax.dynamic_slice_in_dim(b_ada[l], chip * ada_c, ada_c).reshape(1, ada_c)
        mod_parts.append(_ada_mod(f"ada_mod{l}", c_act_all, w_ada[l], bias))
    mod_all = _all_gather_small("ag_mod", jnp.concatenate(mod_parts, axis=1))
    mod_all = mod_all.reshape(N_CHIPS, 2, N_DEV, n_layers, ada_c)[:, 0]
    mod_all = jnp.transpose(mod_all, (1, 2, 0, 3)).reshape(N_DEV, n_layers, N_CHIPS * ada_c)
    mod = lax.dynamic_index_in_dim(mod_all, dev, axis=0, keepdims=False)

    wg_in, wg_o, wg_1, wg_2 = _all_gather_weights(
        "ag_weights", [w_in.astype(BF16), w_o.astype(BF16), w1.astype(BF16), w2.astype(BF16)])

    xs = x[0]
    saved = []
    for l in range(n_layers):
        sh1, sc1, gt1, sh2, sc2, gt2 = [_row(mod[l, i * d:(i + 1) * d]) for i in range(6)]
        gq2 = _row(jnp.tile(g_q[l], 2))
        gk2 = _row(jnp.tile(g_k[l], 2))
        bias_t = _bias_tile(rel_bias[l])
        h1 = _norm_mod(f"norm1_{l}", xs, _row(g_norm1[l]), sc1, sh1)
        proj = _mm_nn_w(f"proj_{l}", h1, wg_in, l, True, [F32])[0]
        o_sb = _sb_fwd(f"sb_fwd_{l}", proj, n_pairs)
        qn, knp, vp = _qk_norm(f"qk_norm_{l}", proj, gq2, gk2, n_pairs)
        o_ca = _ca_fwd(f"ca_fwd_{l}", qn, knp, vp, bias_t, n_pairs)
        mixed = jnp.concatenate([o_sb, o_ca], axis=1)

        def res_epi(acc, res, gt):
            return res + gt * acc, acc

        def res_specs(tm, tn):
            return (pl.BlockSpec((tm, tn), lambda i, j, kk: (i, j)), pl.BlockSpec((1, tn), lambda i, j, kk: (0, j)))

        x1, m1 = _mm_nn_w(f"attn_out_{l}", mixed, wg_o, l, False, [F32, BF16], epi=res_epi,
                          extras=(xs, gt1), extra_specs_fn=res_specs)
        h2 = _norm_mod(f"norm2_{l}", x1, _row(g_norm2[l]), sc2, sh2)

        def act_epi(acc):
            r = jnp.maximum(acc, 0.0)
            return acc, r * r

        u, a = _mm_nn_w(f"mlp_up_{l}", h2, wg_1, l, True, [BF16, BF16], epi=act_epi)
        x2, m2 = _mm_nn_w(f"mlp_down_{l}", a, wg_2, l, False, [F32, BF16], epi=res_epi,
                          extras=(x1, gt2), extra_specs_fn=res_specs)
        saved.append(dict(x0=xs, h1=h1, proj=proj, o_sb=o_sb, qn=qn, knp=knp, vp=vp, bias_t=bias_t, mixed=mixed,
                          m1=m1, x1=x1, h2=h2, u=u, a=a, m2=m2, gq2=gq2, gk2=gk2,
                          sc1=sc1, gt1=gt1, sc2=sc2, gt2=gt2))
        xs = x2

    dx, sq = _loss_grad("loss", xs, loss_target[0])
    loss = lax.psum(0.5 * jnp.sum(sq) / d, ("x", "y", "c"))

    g_in, g_o, g_1, g_2 = [None] * n_layers, [None] * n_layers, [None] * n_layers, [None] * n_layers
    dmod, dg1, dg2, dgq, dgk, drel = [], [], [], [], [], []
    for l in reversed(range(n_layers)):
        sv = saved[l]
        dm2, dgt2 = _gate_bwd(f"gate2_bwd_{l}", dx, sv["m2"], sv["gt2"])

        def act_bwd_epi(acc, u_t):
            return (acc * (2.0 * jnp.maximum(u_t.astype(F32), 0.0)),)

        def tile_specs(tm, tn):
            return (pl.BlockSpec((tm, tn), lambda i, j, kk: (i, j)),)

        du = _mm_nt_w(f"mlp_down_bwd_{l}", dm2, wg_2, l, False, BF16, epi=act_bwd_epi, extras=(sv["u"],),
                      extra_specs_fn=tile_specs)
        g_2[l] = _mm_tn(f"w2_grad_{l}", sv["a"], dm2, False)
        dh2 = _mm_nt_w(f"mlp_up_bwd_{l}", du, wg_1, l, True, F32)
        g_1[l] = _mm_tn(f"w1_grad_{l}", sv["h2"], du, True)
        dx1, dsh2, dsc2, dgn2 = _norm_mod_bwd(f"norm2_bwd_{l}", dh2, sv["x1"], _row(g_norm2[l]), sv["sc2"], dx)
        dm1, dgt1 = _gate_bwd(f"gate1_bwd_{l}", dx1, sv["m1"], sv["gt1"])
        dmixed = _mm_nt_w(f"attn_out_bwd_{l}", dm1, wg_o, l, False, F32)
        g_o[l] = _mm_tn(f"wo_grad_{l}", sv["mixed"], dm1, False)
        dq_sb, dk_sb, dv_sb = _sb_bwd(f"sb_bwd_{l}", sv["proj"], sv["o_sb"], dmixed, n_pairs)
        dqn, dknp, dvp, dbias_t = _ca_bwd(f"ca_bwd_{l}", sv["qn"], sv["knp"], sv["vp"], sv["bias_t"], dmixed,
                                          n_pairs, n_pairs)
        dq_ca, dk_ca, dgq_l, dgk_l = _qk_norm_bwd(f"qk_norm_bwd_{l}", dqn, dknp, sv["proj"], sv["gq2"], sv["gk2"],
                                                  n_pairs)
        drel_l = _rel_bias_grad(f"rel_bias_grad_{l}", jnp.transpose(dbias_t, (1, 0, 2)))[:, :n_rel]
        dproj = jnp.concatenate([dq_sb.astype(BF16), dk_sb.astype(BF16), dv_sb.astype(BF16), dq_ca, dk_ca,
                                 dvp[PAD:].astype(BF16)], axis=1)
        g_in[l] = _mm_tn(f"win_grad_{l}", sv["h1"], dproj, True)
        dh1 = _mm_nt_w(f"proj_bwd_{l}", dproj, wg_in, l, True, F32)
        dx, dsh1, dsc1, dgn1 = _norm_mod_bwd(f"norm1_bwd_{l}", dh1, sv["x0"], _row(g_norm1[l]), sv["sc1"], dx1)
        dmod.insert(0, jnp.concatenate([dsh1, dsc1, dgt1, dsh2, dsc2, dgt2], axis=1)[0])
        dg1.insert(0, dgn1[0])
        dg2.insert(0, dgn2[0])
        dgq.insert(0, dgq_l[0, :HEAD_DIM])
        dgk.insert(0, dgk_l[0, :HEAD_DIM])
        drel.insert(0, drel_l.reshape(-1))
    grad_x = dx[None]

    small = jnp.concatenate([jnp.concatenate(dmod), jnp.concatenate(dg1), jnp.concatenate(dg2),
                             jnp.concatenate(dgq), jnp.concatenate(dgk), jnp.concatenate(drel)])
    n_small = small.shape[0]
    n_pack = -(-n_small // (8 * LANES)) * (8 * LANES)
    small = jnp.pad(small, (0, n_pack - n_small)).reshape(8, n_pack // 8)
    small_all = _all_gather_small("ag_small", small).reshape(N_DEV, 8, n_pack // 8)
    small_sum = _sum_rows8("sum_small", small_all).reshape(-1)
    dmod_all = small_all.reshape(N_DEV, n_pack)[:, :n_layers * 6 * d].reshape(N_DEV, n_layers, 6 * d)

    off = 0

    def take(n, shape):
        nonlocal off
        out = small_sum[off:off + n].reshape(shape)
        off += n
        return out

    grad_b_ada = take(n_layers * 6 * d, (n_layers, 6 * d))
    grad_g_norm1 = take(n_layers * d, (n_layers, d))
    grad_g_norm2 = take(n_layers * d, (n_layers, d))
    grad_g_q = take(n_layers * HEAD_DIM, (n_layers, HEAD_DIM))
    grad_g_k = take(n_layers * HEAD_DIM, (n_layers, HEAD_DIM))
    grad_rel_bias = take(n_layers * n_ca_heads * n_rel, (n_layers, n_ca_heads, n_rel))

    c_act_t = jnp.transpose(c_act_all)
    grad_w_ada = jnp.stack([
        _outer_sum(f"wada_grad_{l}", c_act_t,
                   lax.dynamic_slice_in_dim(dmod_all[:, l, :], chip * ada_c, ada_c, axis=1))
        for l in range(n_layers)])

    partial = [g for per_w in (g_in, g_o, g_1, g_2) for g in per_w]
    from_sibling = _exchange_sibling_halves("rs_sibling", partial)
    chip_sums = [_add_halves(f"rs_add_{i}", g, r, c_idx) for i, (g, r) in enumerate(zip(partial, from_sibling))]
    from_chips = _exchange_chips("rs_chips", chip_sums, n_layers)
    pos_idx = jnp.stack([chip, ac]).astype(jnp.int32)
    reduced = [_sum_chips(f"rs_sum_{w_i}", chip_sums[w_i * n_layers:(w_i + 1) * n_layers], from_chips[w_i], pos_idx)
               for w_i in range(4)]
    grad_w_in, grad_w_o, grad_w1, grad_w2 = _share_with_sibling("rs_share", reduced)

    def adam_big(name, w, g, m, v):
        shp = w.shape
        outs = _adamw(name, w.reshape(-1, shp[-1]), g.reshape(-1, shp[-1]), m.reshape(-1, shp[-1]),
                      v.reshape(-1, shp[-1]))
        return [o.reshape(shp) for o in outs]

    def pack(arrs):
        flat = jnp.concatenate([a.reshape(-1) for a in arrs])
        n = flat.shape[0]
        n_p = -(-n // (8 * LANES)) * (8 * LANES)
        return jnp.pad(flat, (0, n_p - n), constant_values=1.0).reshape(8, n_p // 8)

    small_w = [g_norm1, g_q, g_k, rel_bias, g_norm2, b_ada]
    small_g = [grad_g_norm1, grad_g_q, grad_g_k, grad_rel_bias, grad_g_norm2, grad_b_ada]
    small_m = [m_g_norm1, m_g_q, m_g_k, m_rel_bias, m_g_norm2, m_b_ada]
    small_v = [v_g_norm1, v_g_q, v_g_k, v_rel_bias, v_g_norm2, v_b_ada]
    packed = _adamw("adamw_small", pack(small_w), pack(small_g), pack(small_m), pack(small_v))

    def unpack(p):
        flat = p.reshape(-1)
        res, o = [], 0
        for a in small_w:
            res.append(flat[o:o + a.size].reshape(a.shape))
            o += a.size
        return res

    sd, sm, sv_ = unpack(packed[0]), unpack(packed[1]), unpack(packed[2])
    big = {
        "w_in": adam_big("adamw_w_in", w_in, grad_w_in, m_w_in, v_w_in),
        "w_o": adam_big("adamw_w_o", w_o, grad_w_o, m_w_o, v_w_o),
        "w1": adam_big("adamw_w1", w1, grad_w1, m_w1, v_w1),
        "w2": adam_big("adamw_w2", w2, grad_w2, m_w2, v_w2),
        "w_ada": adam_big("adamw_w_ada", w_ada, grad_w_ada, m_w_ada, v_w_ada),
    }

    def ordered(kind):
        sm_list = (sd, sm, sv_)[kind]
        return [sm_list[0], big["w_in"][kind], sm_list[1], sm_list[2], sm_list[3], big["w_o"][kind], sm_list[4],
                big["w1"][kind], big["w2"][kind], big["w_ada"][kind], sm_list[5]]

    grads = [grad_g_norm1, grad_w_in, grad_g_q, grad_g_k, grad_rel_bias, grad_w_o, grad_g_norm2, grad_w1, grad_w2,
             grad_w_ada, grad_b_ada]
    return (loss, grad_x, *grads, *ordered(0), *ordered(1), *ordered(2))
```

```python
import functools

import jax
import jax.numpy as jnp
from jax import lax
from jax.experimental import pallas as pl
from jax.experimental.pallas import tpu as pltpu

F32 = jnp.float32
BF16 = jnp.bfloat16
MESH = pl.DeviceIdType.MESH

EPS = 1e-6
NEG_INF = -1e30
HEAD_DIM = 64
CHUNK = 64
LEFT_CHUNKS = 8
PAD = LEFT_CHUNKS * CHUNK
BAND = PAD + CHUNK
REL_CLIP = 128
SCALE = HEAD_DIM ** -0.5
LANES = 128
N_CHIPS = 4
N_DEV = 8

ADAM_LR = 0.001
ADAM_B1 = 0.9
ADAM_B2 = 0.999
ADAM_EPS = 1e-08
ADAM_WD = 0.01
ADAM_STEP = 10

VMEM_LIMIT = 48 * 1024 * 1024
ROW_TILE = 256
MM_TILE = 1024
SB_TILE = 256
CA_TILE = 256

NT = (((1,), (1,)), ((), ()))
TN = (((0,), (0,)), ((), ()))
NN = (((1,), (0,)), ((), ()))


def _tile(n, pref):
    best = None
    t = LANES
    while t <= min(n, pref):
        if n % t == 0:
            best = t
        t += LANES
    return best if best is not None else n


def _params(*sem):
    return pltpu.CompilerParams(dimension_semantics=sem, vmem_limit_bytes=VMEM_LIMIT)


def _split_dot(x, t, passes):
    acc = None
    r = x
    for i in range(passes):
        hi = r.astype(BF16)
        d = jnp.dot(hi, t, preferred_element_type=F32)
        acc = d if acc is None else acc + d
        if i + 1 < passes:
            r = r - hi.astype(F32)
    return acc


def _matmul(name, a, b, *, dims, grid, a_spec, b_spec, out_shapes, out_specs, acc_shape,
            epi=None, extras=(), extra_specs=()):
    nk = grid[2]
    n_ex = len(extras)
    n_out = len(out_shapes)

    def body(*refs):
        a_ref, b_ref = refs[0], refs[1]
        ex = refs[2:2 + n_ex]
        outs = refs[2 + n_ex:2 + n_ex + n_out]
        acc = refs[-1]
        k = pl.program_id(2)

        @pl.when(k == 0)
        def _():
            acc[...] = jnp.zeros_like(acc)

        acc[...] += lax.dot_general(a_ref[...].astype(BF16), b_ref[...].astype(BF16), dims,
                                    preferred_element_type=F32)

        @pl.when(k == nk - 1)
        def _():
            res = epi(acc[...], *[e[...] for e in ex]) if epi is not None else (acc[...],)
            for o, r in zip(outs, res):
                o[...] = r.astype(o.dtype)

    return pl.pallas_call(
        body, grid=grid, in_specs=[a_spec, b_spec, *extra_specs], out_specs=out_specs,
        out_shape=out_shapes, scratch_shapes=[pltpu.VMEM(acc_shape, F32)],
        compiler_params=_params("parallel", "parallel", "arbitrary"), name=name,
    )(a, b, *extras)


def _mm_nn_w(name, a, wg, layer, colshard, out_dtypes, epi=None, extras=(), extra_specs_fn=None):
    m = a.shape[0]
    _, _, r, c = wg.shape
    tm = min(MM_TILE, m)
    if colshard:
        k, n = r, N_CHIPS * c
        tk, tn = k, _tile(c, MM_TILE)
        npc = c // tn
        b_spec = pl.BlockSpec((None, None, tk, tn), lambda i, j, kk: (layer, j // npc, kk, j % npc))
    else:
        k, n = N_CHIPS * r, c
        tk, tn = _tile(r, MM_TILE), _tile(c, MM_TILE)
        kpc = r // tk
        b_spec = pl.BlockSpec((None, None, tk, tn), lambda i, j, kk: (layer, kk // kpc, kk % kpc, j))
    assert a.shape[1] == k
    grid = (m // tm, n // tn, k // tk)
    out_spec = pl.BlockSpec((tm, tn), lambda i, j, kk: (i, j))
    ex_specs = extra_specs_fn(tm, tn) if extra_specs_fn is not None else ()
    return _matmul(name, a, wg, dims=NN, grid=grid,
                   a_spec=pl.BlockSpec((tm, tk), lambda i, j, kk: (i, kk)), b_spec=b_spec,
                   out_shapes=[jax.ShapeDtypeStruct((m, n), d) for d in out_dtypes],
                   out_specs=[out_spec for _ in out_dtypes], acc_shape=(tm, tn),
                   epi=epi, extras=extras, extra_specs=ex_specs)


def _mm_nt_w(name, a, wg, layer, colshard, out_dtype, epi=None, extras=(), extra_specs_fn=None):
    m = a.shape[0]
    _, _, r, c = wg.shape
    tm = min(MM_TILE, m)
    if colshard:
        kdim, n = N_CHIPS * c, r
        tk, tn = _tile(c, MM_TILE), _tile(r, MM_TILE)
        kpc = c // tk
        b_spec = pl.BlockSpec((None, None, tn, tk), lambda i, j, kk: (layer, kk // kpc, j, kk % kpc))
    else:
        kdim, n = c, N_CHIPS * r
        tk, tn = c, _tile(r, MM_TILE)
        npc = r // tn
        b_spec = pl.BlockSpec((None, None, tn, tk), lambda i, j, kk: (layer, j // npc, j % npc, kk))
    assert a.shape[1] == kdim
    grid = (m // tm, n // tn, kdim // tk)
    ex_specs = extra_specs_fn(tm, tn) if extra_specs_fn is not None else ()
    return _matmul(name, a, wg, dims=NT, grid=grid,
                   a_spec=pl.BlockSpec((tm, tk), lambda i, j, kk: (i, kk)), b_spec=b_spec,
                   out_shapes=[jax.ShapeDtypeStruct((m, n), out_dtype)],
                   out_specs=[pl.BlockSpec((tm, tn), lambda i, j, kk: (i, j))], acc_shape=(tm, tn),
                   epi=epi, extras=extras, extra_specs=ex_specs)[0]


def _mm_tn(name, a, b, colshard):
    s, m = a.shape
    n = b.shape[1]
    tk = min(2 * MM_TILE, s)
    if colshard:
        c = n // N_CHIPS
        tm, tn = _tile(m, MM_TILE // 2), _tile(c, MM_TILE)
        npc = c // tn
        out_shape = (N_CHIPS, m, c)
        out_spec = pl.BlockSpec((None, tm, tn), lambda i, j, kk: (j // npc, i, j % npc))
    else:
        r = m // N_CHIPS
        tm, tn = _tile(r, MM_TILE // 2), _tile(n, MM_TILE)
        mpc = r // tm
        out_shape = (N_CHIPS, r, n)
        out_spec = pl.BlockSpec((None, tm, tn), lambda i, j, kk: (i // mpc, i % mpc, j))
    grid = (m // tm, n // tn, s // tk)
    return _matmul(name, a, b, dims=TN, grid=grid,
                   a_spec=pl.BlockSpec((tk, tm), lambda i, j, kk: (kk, i)),
                   b_spec=pl.BlockSpec((tk, tn), lambda i, j, kk: (kk, j)),
                   out_shapes=[jax.ShapeDtypeStruct(out_shape, F32)], out_specs=[out_spec],
                   acc_shape=(tm, tn))[0]


def _row_spec(ts, d):
    return pl.BlockSpec((ts, d), lambda i: (i, 0))


def _vec_spec(d):
    return pl.BlockSpec((1, d), lambda i: (0, 0))


def _norm_mod(name, x, g, sc, sh):
    s, d = x.shape
    ts = min(ROW_TILE, s)

    def body(x_ref, g_ref, sc_ref, sh_ref, h_ref):
        xv = x_ref[...]
        r = lax.rsqrt(jnp.mean(xv * xv, axis=-1, keepdims=True) + EPS)
        h_ref[...] = (xv * r * g_ref[...] * (1.0 + sc_ref[...]) + sh_ref[...]).astype(BF16)

    return pl.pallas_call(
        body, grid=(s // ts,), in_specs=[_row_spec(ts, d), _vec_spec(d), _vec_spec(d), _vec_spec(d)],
        out_specs=_row_spec(ts, d), out_shape=jax.ShapeDtypeStruct((s, d), BF16),
        compiler_params=_params("parallel"), name=name)(x, g, sc, sh)


def _norm_mod_bwd(name, dh, x, g, sc, dres):
    s, d = x.shape
    ts = min(ROW_TILE, s)

    def body(dh_ref, x_ref, g_ref, sc_ref, dres_ref, dx_ref, dsh_ref, dsc_ref, dg_ref):
        @pl.when(pl.program_id(0) == 0)
        def _():
            dsh_ref[...] = jnp.zeros_like(dsh_ref)
            dsc_ref[...] = jnp.zeros_like(dsc_ref)
            dg_ref[...] = jnp.zeros_like(dg_ref)

        xv = x_ref[...]
        r = lax.rsqrt(jnp.mean(xv * xv, axis=-1, keepdims=True) + EPS)
        xhat = xv * r
        dhv = dh_ref[...]
        gv = g_ref[...]
        opsc = 1.0 + sc_ref[...]
        dxhat = dhv * (gv * opsc)
        mdot = jnp.mean(dxhat * xhat, axis=-1, keepdims=True)
        dx_ref[...] = dres_ref[...] + r * (dxhat - xhat * mdot)
        dhx = dhv * xhat
        dsh_ref[...] += jnp.sum(dhv, axis=0, keepdims=True)
        dsc_ref[...] += jnp.sum(dhx * gv, axis=0, keepdims=True)
        dg_ref[...] += jnp.sum(dhx * opsc, axis=0, keepdims=True)

    vec = jax.ShapeDtypeStruct((1, d), F32)
    return pl.pallas_call(
        body, grid=(s // ts,),
        in_specs=[_row_spec(ts, d), _row_spec(ts, d), _vec_spec(d), _vec_spec(d), _row_spec(ts, d)],
        out_specs=[_row_spec(ts, d), _vec_spec(d), _vec_spec(d), _vec_spec(d)],
        out_shape=[jax.ShapeDtypeStruct((s, d), F32), vec, vec, vec],
        compiler_params=_params("arbitrary"), name=name)(dh, x, g, sc, dres)


def _gate_bwd(name, dx, m, gt):
    s, d = dx.shape
    ts = min(ROW_TILE, s)

    def body(dx_ref, m_ref, gt_ref, dm_ref, dgt_ref):
        @pl.when(pl.program_id(0) == 0)
        def _():
            dgt_ref[...] = jnp.zeros_like(dgt_ref)

        dxv = dx_ref[...]
        dm_ref[...] = (dxv * gt_ref[...]).astype(BF16)
        dgt_ref[...] += jnp.sum(dxv * m_ref[...].astype(F32), axis=0, keepdims=True)

    return pl.pallas_call(
        body, grid=(s // ts,), in_specs=[_row_spec(ts, d), _row_spec(ts, d), _vec_spec(d)],
        out_specs=[_row_spec(ts, d), _vec_spec(d)],
        out_shape=[jax.ShapeDtypeStruct((s, d), BF16), jax.ShapeDtypeStruct((1, d), F32)],
        compiler_params=_params("arbitrary"), name=name)(dx, m, gt)


def _loss_grad(name, y, target):
    s, d = y.shape
    ts = min(ROW_TILE, s)

    def body(y_ref, t_ref, dy_ref, sq_ref):
        @pl.when(pl.program_id(0) == 0)
        def _():
            sq_ref[...] = jnp.zeros_like(sq_ref)

        e = y_ref[...] - t_ref[...]
        dy_ref[...] = e * (1.0 / d)
        sq_ref[...] += jnp.sum(e * e, axis=0, keepdims=True)

    return pl.pallas_call(
        body, grid=(s // ts,), in_specs=[_row_spec(ts, d), _row_spec(ts, d)],
        out_specs=[_row_spec(ts, d), _vec_spec(d)],
        out_shape=[jax.ShapeDtypeStruct((s, d), F32), jax.ShapeDtypeStruct((1, d), F32)],
        compiler_params=_params("arbitrary"), name=name)(y, target)


def _adamw(name, w, g, m, v):
    r, c = w.shape
    tr = r
    for cand in (512, 256, 128, 64, 32, 16, 8):
        if r % cand == 0 and cand * c * 4 <= 2 * 1024 * 1024:
            tr = cand
            break

    def body(w_ref, g_ref, m_ref, v_ref, d_ref, mo_ref, vo_ref):
        gv = g_ref[...]
        m2 = ADAM_B1 * m_ref[...] + (1.0 - ADAM_B1) * gv
        v2 = ADAM_B2 * v_ref[...] + (1.0 - ADAM_B2) * (gv * gv)
        m_hat = m2 / (1.0 - ADAM_B1 ** ADAM_STEP)
        v_hat = v2 / (1.0 - ADAM_B2 ** ADAM_STEP)
        d_ref[...] = -ADAM_LR * (m_hat / (jnp.sqrt(v_hat) + ADAM_EPS) + ADAM_WD * w_ref[...])
        mo_ref[...] = m2
        vo_ref[...] = v2

    spec = pl.BlockSpec((tr, c), lambda i: (i, 0))
    shp = jax.ShapeDtypeStruct((r, c), F32)
    return pl.pallas_call(
        body, grid=(r // tr,), in_specs=[spec] * 4, out_specs=[spec] * 3, out_shape=[shp] * 3,
        compiler_params=_params("parallel"), name=name)(w, g, m, v)


def _sum_rows8(name, x):
    n, r, c = x.shape

    def body(x_ref, o_ref):
        acc = x_ref[0]
        for i in range(1, n):
            acc = acc + x_ref[i]
        o_ref[...] = acc

    return pl.pallas_call(
        body, out_shape=jax.ShapeDtypeStruct((r, c), F32),
        in_specs=[pl.BlockSpec(memory_space=pltpu.VMEM)], out_specs=pl.BlockSpec(memory_space=pltpu.VMEM),
        name=name)(x)


def _outer_sum(name, ct, dm):
    d, n_seq = ct.shape
    n = dm.shape[1]
    tr = min(256, d)

    def body(ct_ref, dm_ref, o_ref):
        acc = ct_ref[:, 0:1] * dm_ref[0:1, :]
        for s_i in range(1, n_seq):
            acc = acc + ct_ref[:, s_i:s_i + 1] * dm_ref[s_i:s_i + 1, :]
        o_ref[...] = acc

    return pl.pallas_call(
        body, grid=(d // tr,),
        in_specs=[pl.BlockSpec((tr, n_seq), lambda i: (i, 0)), pl.BlockSpec((n_seq, n), lambda i: (0, 0))],
        out_specs=pl.BlockSpec((tr, n), lambda i: (i, 0)), out_shape=jax.ShapeDtypeStruct((d, n), F32),
        compiler_params=_params("parallel"), name=name)(ct, dm)


def _ada_mod(name, c_act, w_ada_l, bias):
    n_seq, d = c_act.shape
    n = w_ada_l.shape[1]
    tn = _tile(n, 768)

    def epi(acc, b):
        return (acc + b,)

    return _matmul(name, c_act, w_ada_l, dims=NN, grid=(1, n // tn, 1),
                   a_spec=pl.BlockSpec((n_seq, d), lambda i, j, kk: (0, 0)),
                   b_spec=pl.BlockSpec((d, tn), lambda i, j, kk: (0, j)),
                   out_shapes=[jax.ShapeDtypeStruct((n_seq, n), F32)],
                   out_specs=[pl.BlockSpec((n_seq, tn), lambda i, j, kk: (0, j))], acc_shape=(n_seq, tn),
                   epi=epi, extras=(bias,), extra_specs=(pl.BlockSpec((1, tn), lambda i, j, kk: (0, j)),))[0]


def _silu_rows(name, c_row):
    d = c_row.shape[1]

    def body(c_ref, o_ref):
        cv = c_ref[...]
        o_ref[...] = jnp.broadcast_to(cv / (1.0 + jnp.exp(-cv)), (8, d))

    return pl.pallas_call(
        body, out_shape=jax.ShapeDtypeStruct((8, d), F32),
        in_specs=[pl.BlockSpec(memory_space=pltpu.VMEM)], out_specs=pl.BlockSpec(memory_space=pltpu.VMEM),
        name=name)(c_row)


def _sb_masks(t):
    row = lax.broadcasted_iota(jnp.int32, (t, t), 0)
    col = lax.broadcasted_iota(jnp.int32, (t, t), 1)
    strict = col < row
    tri = (row > col).astype(BF16)
    return strict, tri


def _col_minus_row(t):
    return lax.broadcasted_iota(jnp.int32, (t, t), 1) - lax.broadcasted_iota(jnp.int32, (t, t), 0)


def _log_sigmoids(z):
    sp = jnp.log(1.0 + jnp.exp(-jnp.abs(z)))
    return jnp.minimum(z, 0.0) - sp, jnp.minimum(-z, 0.0) - sp, sp


def _sb_fwd(name, proj, n_pairs):
    s = proj.shape[0]
    t = min(SB_TILE, s)
    nq = s // t

    heads = [slice(h * HEAD_DIM, (h + 1) * HEAD_DIM) for h in range(LANES // HEAD_DIM)]

    assert nq % 2 == 0

    def body(q_ref, k_ref, v_ref, o_ref):
        qi = pl.program_id(1)
        _, tri = _sb_masks(t)
        diff = _col_minus_row(t)
        qs = [(q_ref[:, sl] * SCALE).astype(BF16) for sl in heads]
        jd = qi // 2
        chains = [(sb, h) for sb in (1, 0) for h in range(len(heads))]

        def step(j, carry, masked):
            ks = [pl.multiple_of((2 * j + sb) * t, t) for sb, _ in chains]
            k_ = [k_ref[pl.ds(ks[i], t), heads[h]].astype(BF16) for i, (_, h) in enumerate(chains)]
            v_ = [v_ref[pl.ds(ks[i], t), heads[h]].astype(BF16) for i, (_, h) in enumerate(chains)]
            zs = [lax.dot_general(qs[h], k_[i], NT, preferred_element_type=F32) for i, (_, h) in enumerate(chains)]
            lbs, lms, keeps = [], [], []
            for (sb, _), z in zip(chains, zs):
                lb, lm, _ = _log_sigmoids(z)
                keep = None
                if masked:
                    keep = diff < (qi - 2 * j - sb) * t
                    lm = jnp.where(keep, lm, 0.0)
                lbs.append(lb)
                lms.append(lm)
                keeps.append(keep)
            css = [_split_dot(lm, tri, 2) for lm in lms]
            sums = [jnp.sum(lm, axis=1, keepdims=True) for lm in lms]
            ws = []
            for i, (sb, h) in enumerate(chains):
                run = carry[h][1] if sb == 1 else carry[h][1] + sums[h]
                w = jnp.exp(lbs[i] + css[i] + run)
                if masked:
                    w = jnp.where(keeps[i], w, 0.0)
                ws.append(w.astype(BF16))
            pvs = [jnp.dot(w, v, preferred_element_type=F32) for w, v in zip(ws, v_)]
            nh = len(heads)
            return tuple((carry[h][0] + pvs[h] + pvs[nh + h], carry[h][1] + sums[h] + sums[nh + h]) for h in range(nh))

        init = tuple((jnp.zeros((t, HEAD_DIM), F32), jnp.zeros((t, 1), F32)) for _ in heads)
        carry = step(jd, init, True)
        carry = lax.fori_loop(0, jd, lambda i, c: step(jd - 1 - i, c, False), carry)
        for (acc, _), sl in zip(carry, heads):
            o_ref[:, sl] = acc

    return pl.pallas_call(
        body, grid=(n_pairs, nq),
        in_specs=[pl.BlockSpec((t, LANES), lambda p, i: (i, p)),
                  pl.BlockSpec((s, LANES), lambda p, i: (0, n_pairs + p)),
                  pl.BlockSpec((s, LANES), lambda p, i: (0, 2 * n_pairs + p))],
        out_specs=pl.BlockSpec((t, LANES), lambda p, i: (i, p)),
        out_shape=jax.ShapeDtypeStruct((s, n_pairs * LANES), F32),
        compiler_params=_params("parallel", "arbitrary"), name=name)(proj, proj, proj)


def _sb_bwd(name, proj, o_sb, dmixed, n_pairs):
    s = proj.shape[0]
    t = min(SB_TILE, s)
    nq = s // t
    heads = [slice(h * HEAD_DIM, (h + 1) * HEAD_DIM) for h in range(LANES // HEAD_DIM)]

    def body(q_ref, k_ref, v_ref, o_ref, do_ref, dq_ref, dk_ref, dv_ref):
        qi = pl.program_id(1)

        @pl.when(qi == 0)
        def _():
            dk_ref[...] = jnp.zeros_like(dk_ref)
            dv_ref[...] = jnp.zeros_like(dv_ref)

        _, tri = _sb_masks(t)
        diff = _col_minus_row(t)
        qs = [(q_ref[:, sl] * SCALE).astype(BF16) for sl in heads]
        dos = [do_ref[:, sl].astype(BF16) for sl in heads]
        dsums = [jnp.sum(do.astype(F32) * o_ref[:, sl], axis=1, keepdims=True) for do, sl in zip(dos, heads)]
        jd = qi // 2
        nh = len(heads)
        chains = [(sb, h) for sb in (1, 0) for h in range(nh)]

        def step(j, carry, masked):
            ks = [pl.multiple_of((2 * j + sb) * t, t) for sb, _ in chains]
            k_ = [k_ref[pl.ds(ks[i], t), heads[h]].astype(BF16) for i, (_, h) in enumerate(chains)]
            v_ = [v_ref[pl.ds(ks[i], t), heads[h]].astype(BF16) for i, (_, h) in enumerate(chains)]
            zs = [lax.dot_general(qs[h], k_[i], NT, preferred_element_type=F32) for i, (_, h) in enumerate(chains)]
            dws = [lax.dot_general(dos[h], v_[i], NT, preferred_element_type=F32) for i, (_, h) in enumerate(chains)]
            lbs, lms, betas, ombs, keeps = [], [], [], [], []
            for (sb, _), z in zip(chains, zs):
                ez = jnp.exp(-jnp.abs(z))
                lb = jnp.minimum(z, 0.0) - jnp.log(1.0 + ez)
                lm = lb - z
                betas.append(jnp.exp(lb))
                ombs.append(jnp.exp(lm))
                keep = None
                if masked:
                    keep = diff < (qi - 2 * j - sb) * t
                    lm = jnp.where(keep, lm, 0.0)
                lbs.append(lb)
                lms.append(lm)
                keeps.append(keep)
            css = [_split_dot(lm, tri, 2) for lm in lms]
            sums_f = [jnp.sum(lm, axis=1, keepdims=True) for lm in lms]
            wbs, es = [], []
            for i, (sb, h) in enumerate(chains):
                run_f = carry[h][1] if sb == 1 else carry[h][1] + sums_f[h]
                w = jnp.exp(lbs[i] + css[i] + run_f)
                if masked:
                    w = jnp.where(keeps[i], w, 0.0)
                wb = w.astype(BF16)
                wbs.append(wb)
                es.append(wb.astype(F32) * dws[i])
            for i, (_, h) in enumerate(chains):
                dv_ref[pl.ds(ks[i], t), heads[h]] += lax.dot_general(wbs[i], dos[h], TN, preferred_element_type=F32)
            ecs = [_split_dot(e, tri, 2) for e in es]
            sums_e = [jnp.sum(e, axis=1, keepdims=True) for e in es]
            dzbs = []
            for i, (sb, h) in enumerate(chains):
                run_e = carry[h][2] if sb == 1 else carry[h][2] + sums_e[h]
                left = dsums[h] - (ecs[i] + es[i] + run_e)
                dz = es[i] * ombs[i] - left * betas[i]
                if masked:
                    dz = jnp.where(keeps[i], dz, 0.0)
                dzbs.append(dz.astype(BF16))
            dqs = []
            for i, (_, h) in enumerate(chains):
                dk_ref[pl.ds(ks[i], t), heads[h]] += lax.dot_general(dzbs[i], qs[h], TN, preferred_element_type=F32)
                dqs.append(jnp.dot(dzbs[i], k_[i], preferred_element_type=F32))
            return tuple((carry[h][0] + dqs[h] + dqs[nh + h], carry[h][1] + sums_f[h] + sums_f[nh + h],
                          carry[h][2] + sums_e[h] + sums_e[nh + h]) for h in range(nh))

        zero = jnp.zeros((t, 1), F32)
        init = tuple((jnp.zeros((t, HEAD_DIM), F32), zero, zero) for _ in heads)
        carry = step(jd, init, True)
        carry = lax.fori_loop(0, jd, lambda i, c: step(jd - 1 - i, c, False), carry)
        for (dq, _, _), sl in zip(carry, heads):
            dq_ref[:, sl] = dq * SCALE

    blk = pl.BlockSpec((t, LANES), lambda p, i: (i, p))
    full = pl.BlockSpec((s, LANES), lambda p, i: (0, p))
    shp = jax.ShapeDtypeStruct((s, n_pairs * LANES), F32)
    return pl.pallas_call(
        body, grid=(n_pairs, nq),
        in_specs=[blk,
                  pl.BlockSpec((s, LANES), lambda p, i: (0, n_pairs + p)),
                  pl.BlockSpec((s, LANES), lambda p, i: (0, 2 * n_pairs + p)),
                  blk, blk],
        out_specs=[blk, full, full], out_shape=[shp, shp, shp],
        compiler_params=_params("parallel", "arbitrary"), name=name)(proj, proj, proj, o_sb, dmixed)


def _group_mean_matrix():
    row = lax.broadcasted_iota(jnp.int32, (LANES, LANES), 0)
    col = lax.broadcasted_iota(jnp.int32, (LANES, LANES), 1)
    same_head = (row < HEAD_DIM) == (col < HEAD_DIM)
    return jnp.where(same_head, 1.0 / HEAD_DIM, 0.0).astype(BF16)


def _qk_norm(name, proj, gq2, gk2, n_pairs):
    s = proj.shape[0]
    ts = min(ROW_TILE, s)
    pb = PAD // ts
    c0 = 3 * n_pairs

    def body(q_ref, k_ref, v_ref, gq_ref, gk_ref, qn_ref, kn_ref, vp_ref):
        r_i = pl.program_id(1)

        @pl.when(r_i < pb)
        def _():
            qn_ref[...] = jnp.zeros_like(qn_ref)
            kn_ref[...] = jnp.zeros_like(kn_ref)
            vp_ref[...] = jnp.zeros_like(vp_ref)

        @pl.when(r_i >= pb)
        def _():
            gm = _group_mean_matrix()
            for src, g_ref, dst in ((q_ref, gq_ref, qn_ref), (k_ref, gk_ref, kn_ref)):
                xv = src[...]
                r = lax.rsqrt(_split_dot(xv * xv, gm, 3) + EPS)
                dst[...] = (xv * r * g_ref[...]).astype(BF16)
            vp_ref[...] = v_ref[...].astype(BF16)

    def in_spec(off):
        return pl.BlockSpec((ts, LANES), lambda p, i: (jnp.maximum(i - pb, 0), c0 + off * n_pairs + p))

    gspec = pl.BlockSpec((1, LANES), lambda p, i: (0, 0))
    ospec = pl.BlockSpec((ts, LANES), lambda p, i: (i, p))
    shp = jax.ShapeDtypeStruct((s + PAD, n_pairs * LANES), BF16)
    return pl.pallas_call(
        body, grid=(n_pairs, (s + PAD) // ts),
        in_specs=[in_spec(0), in_spec(1), in_spec(2), gspec, gspec],
        out_specs=[ospec, ospec, ospec], out_shape=[shp, shp, shp],
        compiler_params=_params("parallel", "arbitrary"), name=name)(proj, proj, proj, gq2, gk2)


def _qk_norm_bwd(name, dqn, dkn, proj, gq2, gk2, n_pairs):
    s = proj.shape[0]
    ts = min(ROW_TILE, s)
    pb = PAD // ts
    c0 = 3 * n_pairs
    n_r = s // ts

    def body(dqn_ref, dkn_ref, q_ref, k_ref, gq_ref, gk_ref, dq_ref, dk_ref, dgq_ref, dgk_ref):
        first = jnp.logical_and(pl.program_id(0) == 0, pl.program_id(1) == 0)
        last = jnp.logical_and(pl.program_id(0) == n_pairs - 1, pl.program_id(1) == n_r - 1)

        @pl.when(first)
        def _():
            dgq_ref[...] = jnp.zeros_like(dgq_ref)
            dgk_ref[...] = jnp.zeros_like(dgk_ref)

        gm = _group_mean_matrix()
        for dn_ref, x_ref, g_ref, dx_ref, dg_ref in ((dqn_ref, q_ref, gq_ref, dq_ref, dgq_ref),
                                                     (dkn_ref, k_ref, gk_ref, dk_ref, dgk_ref)):
            xv = x_ref[...]
            r = lax.rsqrt(_split_dot(xv * xv, gm, 3) + EPS)
            xhat = xv * r
            dn = dn_ref[...]
            tg = dn * g_ref[...]
            md = _split_dot(tg * xhat, gm, 3)
            dx_ref[...] = (r * (tg - xhat * md)).astype(BF16)
            dg_ref[...] += jnp.sum(dn * xhat, axis=0, keepdims=True)

        @pl.when(last)
        def _():
            for dg_ref in (dgq_ref, dgk_ref):
                gv = dg_ref[...]
                dg_ref[...] = gv + pltpu.roll(gv, HEAD_DIM, axis=1)

    def x_spec(off):
        return pl.BlockSpec((ts, LANES), lambda p, i: (i, c0 + off * n_pairs + p))

    dn_q = pl.BlockSpec((ts, LANES), lambda p, i: (i, p))
    dn_k = pl.BlockSpec((ts, LANES), lambda p, i: (i + pb, p))
    gspec = pl.BlockSpec((1, LANES), lambda p, i: (0, 0))
    ospec = pl.BlockSpec((ts, LANES), lambda p, i: (i, p))
    shp = jax.ShapeDtypeStruct((s, n_pairs * LANES), BF16)
    vec = jax.ShapeDtypeStruct((1, LANES), F32)
    return pl.pallas_call(
        body, grid=(n_pairs, n_r),
        in_specs=[dn_q, dn_k, x_spec(0), x_spec(1), gspec, gspec],
        out_specs=[ospec, ospec, gspec, gspec], out_shape=[shp, shp, vec, vec],
        compiler_params=_params("arbitrary", "arbitrary"), name=name)(dqn, dkn, proj, proj, gq2, gk2)


def _ca_scores(qc, kw, bias, n):
    sc = lax.dot_general(qc, kw, NT, preferred_element_type=F32) * SCALE + bias
    jpos = lax.broadcasted_iota(jnp.int32, (CHUNK, BAND), 1)
    return jnp.where(n * CHUNK + jpos >= PAD, sc, NEG_INF)


def _ca_fwd(name, qn, knp, vp, bias_t, n_pairs):
    sp_rows = knp.shape[0]
    s = sp_rows - PAD
    t = min(CA_TILE, s)
    cpb = t // CHUNK
    pb = PAD // t

    def body(q_ref, k_ref, v_ref, b_ref, o_ref):
        qi = pl.program_id(1)
        work = [(ci, h) for ci in range(cpb) for h in range(LANES // HEAD_DIM)]
        scs, vws = [], []
        for ci, h in work:
            n = qi * cpb + ci
            ws = pl.multiple_of(n * CHUNK, CHUNK)
            sl = slice(h * HEAD_DIM, (h + 1) * HEAD_DIM)
            scs.append(_ca_scores(q_ref[ci * CHUNK:(ci + 1) * CHUNK, sl], k_ref[pl.ds(ws, BAND), sl], b_ref[h], n))
            vws.append(v_ref[pl.ds(ws, BAND), sl])
        ps, dens = [], []
        for sc in scs:
            p = jnp.exp(sc - jnp.max(sc, axis=1, keepdims=True))
            dens.append(jnp.sum(p, axis=1, keepdims=True))
            ps.append(p.astype(BF16))
        for (ci, h), p, den, vw in zip(work, ps, dens, vws):
            sl = slice(h * HEAD_DIM, (h + 1) * HEAD_DIM)
            o_ref[ci * CHUNK:(ci + 1) * CHUNK, sl] = jnp.dot(p, vw, preferred_element_type=F32) / den

    full = pl.BlockSpec((sp_rows, LANES), lambda p, i: (0, p))
    return pl.pallas_call(
        body, grid=(n_pairs, s // t),
        in_specs=[pl.BlockSpec((t, LANES), lambda p, i: (i + pb, p)), full, full,
                  pl.BlockSpec((2, CHUNK, BAND), lambda p, i: (p, 0, 0))],
        out_specs=pl.BlockSpec((t, LANES), lambda p, i: (i, p)),
        out_shape=jax.ShapeDtypeStruct((s, n_pairs * LANES), F32),
        compiler_params=_params("parallel", "arbitrary"), name=name)(qn, knp, vp, bias_t)


def _ca_bwd(name, qn, knp, vp, bias_t, dmixed, n_pairs, col0):
    sp_rows = knp.shape[0]
    s = sp_rows - PAD
    t = min(CA_TILE, s)
    cpb = t // CHUNK
    pb = PAD // t

    def body(q_ref, k_ref, v_ref, b_ref, do_ref, dq_ref, dk_ref, dv_ref, db_ref):
        qi = pl.program_id(1)

        @pl.when(qi == 0)
        def _():
            dk_ref[...] = jnp.zeros_like(dk_ref)
            dv_ref[...] = jnp.zeros_like(dv_ref)
            db_ref[...] = jnp.zeros_like(db_ref)

        work = [(ci, h) for ci in range(cpb) for h in range(LANES // HEAD_DIM)]
        qcs, kws, dos, scs, dps, wss = [], [], [], [], [], []
        for ci, h in work:
            n = qi * cpb + ci
            ws = pl.multiple_of(n * CHUNK, CHUNK)
            sl = slice(h * HEAD_DIM, (h + 1) * HEAD_DIM)
            qc = q_ref[ci * CHUNK:(ci + 1) * CHUNK, sl]
            kw = k_ref[pl.ds(ws, BAND), sl]
            do = do_ref[ci * CHUNK:(ci + 1) * CHUNK, sl].astype(BF16)
            scs.append(_ca_scores(qc, kw, b_ref[h], n))
            dps.append(lax.dot_general(do, v_ref[pl.ds(ws, BAND), sl], NT, preferred_element_type=F32))
            qcs.append(qc)
            kws.append(kw)
            dos.append(do)
            wss.append(ws)
        pbs, dss = [], []
        for sc, dp in zip(scs, dps):
            p = jnp.exp(sc - jnp.max(sc, axis=1, keepdims=True))
            p = p / jnp.sum(p, axis=1, keepdims=True)
            dss.append(p * (dp - jnp.sum(p * dp, axis=1, keepdims=True)))
            pbs.append(p.astype(BF16))
        for i, (ci, h) in enumerate(work):
            sl = slice(h * HEAD_DIM, (h + 1) * HEAD_DIM)
            dsb = dss[i].astype(BF16)
            dv_ref[pl.ds(wss[i], BAND), sl] += lax.dot_general(pbs[i], dos[i], TN, preferred_element_type=F32)
            dk_ref[pl.ds(wss[i], BAND), sl] += lax.dot_general(dsb, qcs[i], TN, preferred_element_type=F32) * SCALE
            dq_ref[ci * CHUNK:(ci + 1) * CHUNK, sl] = jnp.dot(dsb, kws[i], preferred_element_type=F32) * SCALE
        for h in range(LANES // HEAD_DIM):
            total = dss[h]
            for ci in range(1, cpb):
                total = total + dss[ci * (LANES // HEAD_DIM) + h]
            db_ref[h] += total

    full_in = pl.BlockSpec((sp_rows, LANES), lambda p, i: (0, p))
    btile = pl.BlockSpec((2, CHUNK, BAND), lambda p, i: (p, 0, 0))
    pad_shape = jax.ShapeDtypeStruct((sp_rows, n_pairs * LANES), F32)
    return pl.pallas_call(
        body, grid=(n_pairs, s // t),
        in_specs=[pl.BlockSpec((t, LANES), lambda p, i: (i + pb, p)), full_in, full_in, btile,
                  pl.BlockSpec((t, LANES), lambda p, i: (i, col0 + p))],
        out_specs=[pl.BlockSpec((t, LANES), lambda p, i: (i, p)), full_in, full_in, btile],
        out_shape=[jax.ShapeDtypeStruct((s, n_pairs * LANES), F32), pad_shape, pad_shape,
                   jax.ShapeDtypeStruct(bias_t.shape, F32)],
        compiler_params=_params("parallel", "arbitrary"), name=name)(qn, knp, vp, bias_t, dmixed)


def _bias_tile(rb):
    n_h = rb.shape[0]
    far = jnp.broadcast_to(rb[:, 2 * REL_CLIP:], (n_h, PAD + CHUNK - REL_CLIP))
    near = rb[:, REL_CLIP - (CHUNK - 1):2 * REL_CLIP][:, ::-1]
    ext = jnp.concatenate([far, near], axis=1)
    return jnp.stack([ext[:, CHUNK - 1 - i:CHUNK - 1 - i + BAND] for i in range(CHUNK)], axis=1)


def _rel_bias_grad(name, db_t):
    _, n_h, _ = db_t.shape
    n_out = 3 * LANES
    assert CHUNK == 64 and REL_CLIP == 128 and LEFT_CHUNKS == 8

    def body(z_ref, o_ref):
        a_idx = lax.broadcasted_iota(jnp.int32, (LANES, n_out), 0)
        r_idx = lax.broadcasted_iota(jnp.int32, (LANES, n_out), 1)
        out = jnp.zeros((n_h, n_out), F32)
        for mm, base in ((0, 191), (1, 255), (2, 319)):
            b = LEFT_CHUNKS - mm
            acc = jnp.zeros((n_h, LANES), F32)
            for i in range(CHUNK):
                row = z_ref[i, :, b * CHUNK:(b + 1) * CHUNK]
                rowp = jnp.concatenate([row, jnp.zeros((n_h, LANES - CHUNK), F32)], axis=1)
                acc = acc + pltpu.roll(rowp, CHUNK - 1 - i, axis=1)
            place = (r_idx == jnp.minimum(base - a_idx, 2 * REL_CLIP)).astype(BF16)
            out = out + _split_dot(acc, place, 3)
        far = jnp.zeros((n_h, 1), F32)
        for i in range(CHUNK):
            far = far + jnp.sum(z_ref[i, :, 0:(LEFT_CHUNKS - 2) * CHUNK], axis=1, keepdims=True)
        lane = lax.broadcasted_iota(jnp.int32, (n_h, n_out), 1)
        o_ref[...] = out + jnp.where(lane == 2 * REL_CLIP, far, 0.0)

    return pl.pallas_call(
        body, out_shape=jax.ShapeDtypeStruct((n_h, n_out), F32),
        in_specs=[pl.BlockSpec(memory_space=pltpu.VMEM)], out_specs=pl.BlockSpec(memory_space=pltpu.VMEM),
        name=name)(db_t)


def _mesh_pos():
    return lax.axis_index("x"), lax.axis_index("y"), lax.axis_index("c")


def _other_chips(x, y):
    return [(1 - x, y), (x, 1 - y), (1 - x, 1 - y)]


def _all_gather_small(name, blk):
    m_per, n = blk.shape

    def body(x_ref, out_ref, send_sems, recv_sems, local_sem):
        x, y, c = _mesh_pos()
        me, sibling = (x, y, c), (x, y, 1 - c)
        chips = _other_chips(x, y)

        def rows(px, py, pc):
            return out_ref.at[pl.ds((4 * px + 2 * py + pc) * m_per, m_per), :]

        def copy(k, block, to, src=None):
            return pltpu.make_async_remote_copy(
                src_ref=rows(*block) if src is None else src, dst_ref=rows(*block),
                send_sem=send_sems.at[k], recv_sem=recv_sems.at[k], device_id=to, device_id_type=MESH)

        mine = pltpu.make_async_copy(x_ref, rows(*me), local_sem)
        mine.start()
        first = [copy(0, me, sibling, src=x_ref)]
        first += [copy(1 + j, me, (*chip, c), src=x_ref) for j, chip in enumerate(chips)]
        for cp in first:
            cp.start()
        passed = [copy(4 + j, (*chip, c), sibling) for j, chip in enumerate(chips)]
        for j, chip in enumerate(chips):
            copy(1 + j, (*chip, c), me).wait_recv()
            passed[j].start()
        copy(0, sibling, me).wait_recv()
        for j, chip in enumerate(chips):
            copy(4 + j, (*chip, 1 - c), me).wait_recv()
        for cp in first + passed:
            cp.wait_send()
        mine.wait()

    return pl.pallas_call(
        body, out_shape=jax.ShapeDtypeStruct((N_DEV * m_per, n), blk.dtype),
        in_specs=[pl.BlockSpec(memory_space=pltpu.VMEM)], out_specs=pl.BlockSpec(memory_space=pltpu.VMEM),
        scratch_shapes=[pltpu.SemaphoreType.DMA((7,)), pltpu.SemaphoreType.DMA((7,)), pltpu.SemaphoreType.DMA],
        compiler_params=pltpu.CompilerParams(vmem_limit_bytes=VMEM_LIMIT), name=name)(blk)


HBM_SPEC = pl.BlockSpec(memory_space=pltpu.HBM)


def _all_gather_weights(name, shards):
    n = len(shards)

    def body(*refs):
        srcs, outs = refs[:n], refs[n:2 * n]
        send_sems, recv_sems, local_sems = refs[2 * n:]
        x, y, c = _mesh_pos()
        me, sibling = (x, y, c), (x, y, 1 - c)
        chips = _other_chips(x, y)

        def win(t, px, py, pc):
            rh = shards[t].shape[1] // 2
            return outs[t].at[:, 2 * px + py, pl.ds(pc * rh, rh), :]

        def own(t):
            rh = shards[t].shape[1] // 2
            return srcs[t].at[:, pl.ds(c * rh, rh), :]

        def copy(t, k, block, to, src=None):
            return pltpu.make_async_remote_copy(
                src_ref=win(t, *block) if src is None else src, dst_ref=win(t, *block),
                send_sem=send_sems.at[t, k], recv_sem=recv_sems.at[t, k], device_id=to, device_id_type=MESH)

        mine = [pltpu.make_async_copy(own(t), win(t, *me), local_sems.at[t]) for t in range(n)]
        for cp in mine:
            cp.start()
        first = []
        for t in range(n):
            first.append(copy(t, 0, me, sibling, src=own(t)))
            first += [copy(t, 1 + j, me, (*chip, c), src=own(t)) for j, chip in enumerate(chips)]
        for cp in first:
            cp.start()
        passed = []
        for j, chip in enumerate(chips):
            for t in range(n):
                copy(t, 1 + j, (*chip, c), me).wait_recv()
                fwd = copy(t, 4 + j, (*chip, c), sibling)
                fwd.start()
                passed.append(fwd)
        for t in range(n):
            copy(t, 0, sibling, me).wait_recv()
            for j, chip in enumerate(chips):
                copy(t, 4 + j, (*chip, 1 - c), me).wait_recv()
        for cp in first + passed:
            cp.wait_send()
        for cp in mine:
            cp.wait()

    out_shape = [jax.ShapeDtypeStruct((w.shape[0], N_CHIPS, w.shape[1], w.shape[2]), w.dtype) for w in shards]
    return pl.pallas_call(
        body, out_shape=out_shape, in_specs=[HBM_SPEC] * n, out_specs=[HBM_SPEC] * n,
        scratch_shapes=[pltpu.SemaphoreType.DMA((n, 7)), pltpu.SemaphoreType.DMA((n, 7)),
                        pltpu.SemaphoreType.DMA((n,))],
        name=name)(*shards)


def _exchange_sibling_halves(name, grads):
    n = len(grads)

    def body(*refs):
        srcs, outs = refs[:n], refs[n:2 * n]
        send_sems, recv_sems = refs[2 * n:]
        x, y, c = _mesh_pos()
        cps = []
        for t in range(n):
            rh = grads[t].shape[1] // 2
            cps.append(pltpu.make_async_remote_copy(
                src_ref=srcs[t].at[:, pl.ds((1 - c) * rh, rh), :], dst_ref=outs[t],
                send_sem=send_sems.at[t], recv_sem=recv_sems.at[t], device_id=(x, y, 1 - c), device_id_type=MESH))
        for cp in cps:
            cp.start()
        for cp in cps:
            cp.wait()

    out_shape = [jax.ShapeDtypeStruct((g.shape[0], g.shape[1] // 2, g.shape[2]), g.dtype) for g in grads]
    return pl.pallas_call(
        body, out_shape=out_shape, in_specs=[HBM_SPEC] * n, out_specs=[HBM_SPEC] * n,
        scratch_shapes=[pltpu.SemaphoreType.DMA((n,)), pltpu.SemaphoreType.DMA((n,))],
        name=name)(*grads)


def _add_halves(name, g, recv, c_idx):
    nb, r, c = g.shape
    rh = r // 2
    tr = rh
    for cand in (512, 256, 128, 64):
        if rh % cand == 0:
            tr = cand
            break
    g4 = g.reshape(nb, 2, rh, c)

    def body(c_ref, g_ref, r_ref, o_ref):
        o_ref[...] = (g_ref[...] + r_ref[...]).astype(BF16)

    grid_spec = pltpu.PrefetchScalarGridSpec(
        num_scalar_prefetch=1, grid=(nb, rh // tr),
        in_specs=[pl.BlockSpec((None, None, tr, c), lambda j, i, cr: (j, cr[0], i, 0)),
                  pl.BlockSpec((None, tr, c), lambda j, i, cr: (j, i, 0))],
        out_specs=pl.BlockSpec((None, tr, c), lambda j, i, cr: (j, i, 0)))
    return pl.pallas_call(
        body, grid_spec=grid_spec, out_shape=jax.ShapeDtypeStruct((nb, rh, c), BF16),
        compiler_params=_params("parallel", "parallel"), name=name)(c_idx, g4, recv)


def _exchange_chips(name, parts, n_layers):
    n = len(parts)
    n_w = n // n_layers

    def body(*refs):
        srcs, outs = refs[:n], refs[n:n + n_w]
        send_sems, recv_sems = refs[n + n_w:]
        x, y, c = _mesh_pos()
        cps = []
        for t in range(n):
            w_i, l_i = divmod(t, n_layers)
            for k, (px, py) in enumerate(_other_chips(x, y)):
                cps.append(pltpu.make_async_remote_copy(
                    src_ref=srcs[t].at[2 * px + py], dst_ref=outs[w_i].at[l_i, k],
                    send_sem=send_sems.at[t, k], recv_sem=recv_sems.at[t, k],
                    device_id=(px, py, c), device_id_type=MESH))
        for cp in cps:
            cp.start()
        for cp in cps:
            cp.wait()

    out_shape = [jax.ShapeDtypeStruct((n_layers, 3) + parts[w_i * n_layers].shape[1:], BF16) for w_i in range(n_w)]
    return pl.pallas_call(
        body, out_shape=out_shape, in_specs=[HBM_SPEC] * n, out_specs=[HBM_SPEC] * n_w,
        scratch_shapes=[pltpu.SemaphoreType.DMA((n, 3)), pltpu.SemaphoreType.DMA((n, 3))],
        name=name)(*parts)


def _sum_chips(name, own_parts, recv, pos_idx):
    n_layers = len(own_parts)
    _, _, rh, c = recv.shape
    tr = rh
    for cand in (512, 256, 128, 64):
        if rh % cand == 0:
            tr = cand
            break
    nb = rh // tr

    def body(pos_ref, *refs):
        own_refs, r_ref, o_ref = refs[:n_layers], refs[n_layers], refs[n_layers + 1]
        for l_i in range(n_layers):
            @pl.when(pl.program_id(0) == l_i)
            def _():
                acc = own_refs[l_i][...].astype(F32)
                for k in range(3):
                    acc = acc + r_ref[k].astype(F32)
                o_ref[...] = acc

    grid_spec = pltpu.PrefetchScalarGridSpec(
        num_scalar_prefetch=1, grid=(n_layers, nb),
        in_specs=[pl.BlockSpec((None, tr, c), lambda l, i, pr: (pr[0], i, 0)) for _ in range(n_layers)]
        + [pl.BlockSpec((None, 3, tr, c), lambda l, i, pr: (l, 0, i, 0))],
        out_specs=pl.BlockSpec((None, tr, c), lambda l, i, pr: (l, pr[1] * nb + i, 0)))
    return pl.pallas_call(
        body, grid_spec=grid_spec, out_shape=jax.ShapeDtypeStruct((n_layers, 2 * rh, c), F32),
        compiler_params=_params("parallel", "parallel"), name=name)(pos_idx, *own_parts, recv)


def _share_with_sibling(name, grads):
    n = len(grads)

    def body(*refs):
        outs = refs[n:2 * n]
        send_sems, recv_sems = refs[2 * n:]
        x, y, c = _mesh_pos()
        cps, landed = [], []
        for t in range(n):
            rh = grads[t].shape[1] // 2
            mine = outs[t].at[:, pl.ds(c * rh, rh), :]
            theirs = outs[t].at[:, pl.ds((1 - c) * rh, rh), :]
            cps.append(pltpu.make_async_remote_copy(
                src_ref=mine, dst_ref=mine, send_sem=send_sems.at[t], recv_sem=recv_sems.at[t],
                device_id=(x, y, 1 - c), device_id_type=MESH))
            landed.append(pltpu.make_async_remote_copy(
                src_ref=mine, dst_ref=theirs, send_sem=send_sems.at[t], recv_sem=recv_sems.at[t],
                device_id=(x, y, 1 - c), device_id_type=MESH))
        for cp in cps:
            cp.start()
        for cp, ld in zip(cps, landed):
            cp.wait_send()
            ld.wait_recv()

    out_shape = [jax.ShapeDtypeStruct(g.shape, g.dtype) for g in grads]
    return pl.pallas_call(
        body, out_shape=out_shape, in_specs=[HBM_SPEC] * n, out_specs=[HBM_SPEC] * n,
        input_output_aliases={t: t for t in range(n)},
        scratch_shapes=[pltpu.SemaphoreType.DMA((n,)), pltpu.SemaphoreType.DMA((n,))],
        name=name)(*grads)


def _row(v):
    return v.reshape(1, -1)


def kernel(x, c, g_norm1, w_in, g_q, g_k, rel_bias, w_o, g_norm2, w1, w2, w_ada, b_ada, loss_target, m_g_norm1, m_w_in, m_g_q, m_g_k, m_rel_bias, m_w_o, m_g_norm2, m_w1, m_w2, m_w_ada, m_b_ada, v_g_norm1, v_w_in, v_g_q, v_g_k, v_rel_bias, v_w_o, v_g_norm2, v_w1, v_w2, v_w_ada, v_b_ada):
    _, s, d = x.shape
    n_layers = g_norm1.shape[0]
    n_pairs = (d // 2) // LANES
    n_ca_heads = rel_bias.shape[1]
    n_rel = rel_bias.shape[2]
    assert n_rel == 2 * REL_CLIP + 1 and g_q.shape[1] == HEAD_DIM and n_ca_heads == 2 * n_pairs
    ada_c = w_ada.shape[2]

    ax, ay, ac = _mesh_pos()
    chip = 2 * ax + ay
    dev = 4 * ax + 2 * ay + ac
    c_idx = jnp.reshape(ac, (1,)).astype(jnp.int32)

    c_act_all = _all_gather_small("ag_c", _silu_rows("silu_c", c)).reshape(N_DEV, 8, d)[:, 0, :]
    mod_parts = []
    for l in range(n_layers):
        bias = lax.dynamic_slice_in_dim(b_ada[l], chip * ada_c, ada_c).reshape(1, ada_c)
        mod_parts.append(_ada_mod(f"ada_mod{l}", c_act_all, w_ada[l], bias))
    mod_all = _all_gather_small("ag_mod", jnp.concatenate(mod_parts, axis=1))
    mod_all = mod_all.reshape(N_CHIPS, 2, N_DEV, n_layers, ada_c)[:, 0]
    mod_all = jnp.transpose(mod_all, (1, 2, 0, 3)).reshape(N_DEV, n_layers, N_CHIPS * ada_c)
    mod = lax.dynamic_index_in_dim(mod_all, dev, axis=0, keepdims=False)

    wg_in, wg_o, wg_1, wg_2 = _all_gather_weights(
        "ag_weights", [w_in.astype(BF16), w_o.astype(BF16), w1.astype(BF16), w2.astype(BF16)])

    xs = x[0]
    saved = []
    for l in range(n_layers):
        sh1, sc1, gt1, sh2, sc2, gt2 = [_row(mod[l, i * d:(i + 1) * d]) for i in range(6)]
        gq2 = _row(jnp.tile(g_q[l], 2))
        gk2 = _row(jnp.tile(g_k[l], 2))
        bias_t = _bias_tile(rel_bias[l])
        h1 = _norm_mod(f"norm1_{l}", xs, _row(g_norm1[l]), sc1, sh1)
        proj = _mm_nn_w(f"proj_{l}", h1, wg_in, l, True, [F32])[0]
        o_sb = _sb_fwd(f"sb_fwd_{l}", proj, n_pairs)
        qn, knp, vp = _qk_norm(f"qk_norm_{l}", proj, gq2, gk2, n_pairs)
        o_ca = _ca_fwd(f"ca_fwd_{l}", qn, knp, vp, bias_t, n_pairs)
        mixed = jnp.concatenate([o_sb, o_ca], axis=1)

        def res_epi(acc, res, gt):
            return res + gt * acc, acc

        def res_specs(tm, tn):
            return (pl.BlockSpec((tm, tn), lambda i, j, kk: (i, j)), pl.BlockSpec((1, tn), lambda i, j, kk: (0, j)))

        x1, m1 = _mm_nn_w(f"attn_out_{l}", mixed, wg_o, l, False, [F32, BF16], epi=res_epi,
                          extras=(xs, gt1), extra_specs_fn=res_specs)
        h2 = _norm_mod(f"norm2_{l}", x1, _row(g_norm2[l]), sc2, sh2)

        def act_epi(acc):
            r = jnp.maximum(acc, 0.0)
            return acc, r * r

        u, a = _mm_nn_w(f"mlp_up_{l}", h2, wg_1, l, True, [BF16, BF16], epi=act_epi)
        x2, m2 = _mm_nn_w(f"mlp_down_{l}", a, wg_2, l, False, [F32, BF16], epi=res_epi,
                          extras=(x1, gt2), extra_specs_fn=res_specs)
        saved.append(dict(x0=xs, h1=h1, proj=proj, o_sb=o_sb, qn=qn, knp=knp, vp=vp, bias_t=bias_t, mixed=mixed,
                          m1=m1, x1=x1, h2=h2, u=u, a=a, m2=m2, gq2=gq2, gk2=gk2,
                          sc1=sc1, gt1=gt1, sc2=sc2, gt2=gt2))
        xs = x2

    dx, sq = _loss_grad("loss", xs, loss_target[0])
    loss = lax.psum(0.5 * jnp.sum(sq) / d, ("x", "y", "c"))

    g_in, g_o, g_1, g_2 = [None] * n_layers, [None] * n_layers, [None] * n_layers, [None] * n_layers
    dmod, dg1, dg2, dgq, dgk, drel = [], [], [], [], [], []
    for l in reversed(range(n_layers)):
        sv = saved[l]
        dm2, dgt2 = _gate_bwd(f"gate2_bwd_{l}", dx, sv["m2"], sv["gt2"])

        def act_bwd_epi(acc, u_t):
            return (acc * (2.0 * jnp.maximum(u_t.astype(F32), 0.0)),)

        def tile_specs(tm, tn):
            return (pl.BlockSpec((tm, tn), lambda i, j, kk: (i, j)),)

        du = _mm_nt_w(f"mlp_down_bwd_{l}", dm2, wg_2, l, False, BF16, epi=act_bwd_epi, extras=(sv["u"],),
                      extra_specs_fn=tile_specs)
        g_2[l] = _mm_tn(f"w2_grad_{l}", sv["a"], dm2, False)
        dh2 = _mm_nt_w(f"mlp_up_bwd_{l}", du, wg_1, l, True, F32)
        g_1[l] = _mm_tn(f"w1_grad_{l}", sv["h2"], du, True)
        dx1, dsh2, dsc2, dgn2 = _norm_mod_bwd(f"norm2_bwd_{l}", dh2, sv["x1"], _row(g_norm2[l]), sv["sc2"], dx)
        dm1, dgt1 = _gate_bwd(f"gate1_bwd_{l}", dx1, sv["m1"], sv["gt1"])
        dmixed = _mm_nt_w(f"attn_out_bwd_{l}", dm1, wg_o, l, False, F32)
        g_o[l] = _mm_tn(f"wo_grad_{l}", sv["mixed"], dm1, False)
        dq_sb, dk_sb, dv_sb = _sb_bwd(f"sb_bwd_{l}", sv["proj"], sv["o_sb"], dmixed, n_pairs)
        dqn, dknp, dvp, dbias_t = _ca_bwd(f"ca_bwd_{l}", sv["qn"], sv["knp"], sv["vp"], sv["bias_t"], dmixed,
                                          n_pairs, n_pairs)
        dq_ca, dk_ca, dgq_l, dgk_l = _qk_norm_bwd(f"qk_norm_bwd_{l}", dqn, dknp, sv["proj"], sv["gq2"], sv["gk2"],
                                                  n_pairs)
        drel_l = _rel_bias_grad(f"rel_bias_grad_{l}", jnp.transpose(dbias_t, (1, 0, 2)))[:, :n_rel]
        dproj = jnp.concatenate([dq_sb.astype(BF16), dk_sb.astype(BF16), dv_sb.astype(BF16), dq_ca, dk_ca,
                                 dvp[PAD:].astype(BF16)], axis=1)
        g_in[l] = _mm_tn(f"win_grad_{l}", sv["h1"], dproj, True)
        dh1 = _mm_nt_w(f"proj_bwd_{l}", dproj, wg_in, l, True, F32)
        dx, dsh1, dsc1, dgn1 = _norm_mod_bwd(f"norm1_bwd_{l}", dh1, sv["x0"], _row(g_norm1[l]), sv["sc1"], dx1)
        dmod.insert(0, jnp.concatenate([dsh1, dsc1, dgt1, dsh2, dsc2, dgt2], axis=1)[0])
        dg1.insert(0, dgn1[0])
        dg2.insert(0, dgn2[0])
        dgq.insert(0, dgq_l[0, :HEAD_DIM])
        dgk.insert(0, dgk_l[0, :HEAD_DIM])
        drel.insert(0, drel_l.reshape(-1))
    grad_x = dx[None]

    small = jnp.concatenate([jnp.concatenate(dmod), jnp.concatenate(dg1), jnp.concatenate(dg2),
                             jnp.concatenate(dgq), jnp.concatenate(dgk), jnp.concatenate(drel)])
    n_small = small.shape[0]
    n_pack = -(-n_small // (8 * LANES)) * (8 * LANES)
    small = jnp.pad(small, (0, n_pack - n_small)).reshape(8, n_pack // 8)
    small_all = _all_gather_small("ag_small", small).reshape(N_DEV, 8, n_pack // 8)
    small_sum = _sum_rows8("sum_small", small_all).reshape(-1)
    dmod_all = small_all.reshape(N_DEV, n_pack)[:, :n_layers * 6 * d].reshape(N_DEV, n_layers, 6 * d)

    off = 0

    def take(n, shape):
        nonlocal off
        out = small_sum[off:off + n].reshape(shape)
        off += n
        return out

    grad_b_ada = take(n_layers * 6 * d, (n_layers, 6 * d))
    grad_g_norm1 = take(n_layers * d, (n_layers, d))
    grad_g_norm2 = take(n_layers * d, (n_layers, d))
    grad_g_q = take(n_layers * HEAD_DIM, (n_layers, HEAD_DIM))
    grad_g_k = take(n_layers * HEAD_DIM, (n_layers, HEAD_DIM))
    grad_rel_bias = take(n_layers * n_ca_heads * n_rel, (n_layers, n_ca_heads, n_rel))

    c_act_t = jnp.transpose(c_act_all)
    grad_w_ada = jnp.stack([
        _outer_sum(f"wada_grad_{l}", c_act_t,
                   lax.dynamic_slice_in_dim(dmod_all[:, l, :], chip * ada_c, ada_c, axis=1))
        for l in range(n_layers)])

    partial = [g for per_w in (g_in, g_o, g_1, g_2) for g in per_w]
    from_sibling = _exchange_sibling_halves("rs_sibling", partial)
    chip_sums = [_add_halves(f"rs_add_{i}", g, r, c_idx) for i, (g, r) in enumerate(zip(partial, from_sibling))]
    from_chips = _exchange_chips("rs_chips", chip_sums, n_layers)
    pos_idx = jnp.stack([chip, ac]).astype(jnp.int32)
    reduced = [_sum_chips(f"rs_sum_{w_i}", chip_sums[w_i * n_layers:(w_i + 1) * n_layers], from_chips[w_i], pos_idx)
               for w_i in range(4)]
    grad_w_in, grad_w_o, grad_w1, grad_w2 = _share_with_sibling("rs_share", reduced)

    def adam_big(name, w, g, m, v):
        shp = w.shape
        outs = _adamw(name, w.reshape(-1, shp[-1]), g.reshape(-1, shp[-1]), m.reshape(-1, shp[-1]),
                      v.reshape(-1, shp[-1]))
        return [o.reshape(shp) for o in outs]

    def pack(arrs):
        flat = jnp.concatenate([a.reshape(-1) for a in arrs])
        n = flat.shape[0]
        n_p = -(-n // (8 * LANES)) * (8 * LANES)
        return jnp.pad(flat, (0, n_p - n), constant_values=1.0).reshape(8, n_p // 8)

    small_w = [g_norm1, g_q, g_k, rel_bias, g_norm2, b_ada]
    small_g = [grad_g_norm1, grad_g_q, grad_g_k, grad_rel_bias, grad_g_norm2, grad_b_ada]
    small_m = [m_g_norm1, m_g_q, m_g_k, m_rel_bias, m_g_norm2, m_b_ada]
    small_v = [v_g_norm1, v_g_q, v_g_k, v_rel_bias, v_g_norm2, v_b_ada]
    packed = _adamw("adamw_small", pack(small_w), pack(small_g), pack(small_m), pack(small_v))

    def unpack(p):
        flat = p.reshape(-1)
        res, o = [], 0
        for a in small_w:
            res.append(flat[o:o + a.size].reshape(a.shape))
            o += a.size
        return res

    sd, sm, sv_ = unpack(packed[0]), unpack(packed[1]), unpack(packed[2])
    big = {
        "w_in": adam_big("adamw_w_in", w_in, grad_w_in, m_w_in, v_w_in),
        "w_o": adam_big("adamw_w_o", w_o, grad_w_o, m_w_o, v_w_o),
        "w1": adam_big("adamw_w1", w1, grad_w1, m_w1, v_w1),
        "w2": adam_big("adamw_w2", w2, grad_w2, m_w2, v_w2),
        "w_ada": adam_big("adamw_w_ada", w_ada, grad_w_ada, m_w_ada, v_w_ada),
    }

    def ordered(kind):
        sm_list = (sd, sm, sv_)[kind]
        return [sm_list[0], big["w_in"][kind], sm_list[1], sm_list[2], sm_list[3], big["w_o"][kind], sm_list[4],
                big["w1"][kind], big["w2"][kind], big["w_ada"][kind], sm_list[5]]

    grads = [grad_g_norm1, grad_w_in, grad_g_q, grad_g_k, grad_rel_bias, grad_w_o, grad_g_norm2, grad_w1, grad_w2,
             grad_w_ada, grad_b_ada]
    return (loss, grad_x, *grads, *ordered(0), *ordered(1), *ordered(2))
```

```python
import functools

import jax
import jax.numpy as jnp
from jax import lax
from jax.experimental import pallas as pl
from jax.experimental.pallas import tpu as pltpu

F32 = jnp.float32
BF16 = jnp.bfloat16
MESH = pl.DeviceIdType.MESH

EPS = 1e-6
NEG_INF = -1e30
HEAD_DIM = 64
CHUNK = 64
LEFT_CHUNKS = 8
PAD = LEFT_CHUNKS * CHUNK
BAND = PAD + CHUNK
REL_CLIP = 128
SCALE = HEAD_DIM ** -0.5
DEAD_LOG = -88.0
LANES = 128
N_CHIPS = 4
N_DEV = 8

ADAM_LR = 0.001
ADAM_B1 = 0.9
ADAM_B2 = 0.999
ADAM_EPS = 1e-08
ADAM_WD = 0.01
ADAM_STEP = 10

VMEM_LIMIT = 48 * 1024 * 1024
ROW_TILE = 256
MM_TILE = 1024
SB_TILE = 256
CA_TILE = 256

NT = (((1,), (1,)), ((), ()))
TN = (((0,), (0,)), ((), ()))
NN = (((1,), (0,)), ((), ()))


def _tile(n, pref):
    best = None
    t = LANES
    while t <= min(n, pref):
        if n % t == 0:
            best = t
        t += LANES
    return best if best is not None else n


def _params(*sem):
    return pltpu.CompilerParams(dimension_semantics=sem, vmem_limit_bytes=VMEM_LIMIT)


def _split_dot(x, t, passes):
    acc = None
    r = x
    for i in range(passes):
        hi = r.astype(BF16)
        d = jnp.dot(hi, t, preferred_element_type=F32)
        acc = d if acc is None else acc + d
        if i + 1 < passes:
            r = r - hi.astype(F32)
    return acc


def _matmul(name, a, b, *, dims, grid, a_spec, b_spec, out_shapes, out_specs, acc_shape,
            epi=None, extras=(), extra_specs=()):
    nk = grid[2]
    n_ex = len(extras)
    n_out = len(out_shapes)

    def body(*refs):
        a_ref, b_ref = refs[0], refs[1]
        ex = refs[2:2 + n_ex]
        outs = refs[2 + n_ex:2 + n_ex + n_out]
        acc = refs[-1]
        k = pl.program_id(2)

        @pl.when(k == 0)
        def _():
            acc[...] = jnp.zeros_like(acc)

        acc[...] += lax.dot_general(a_ref[...].astype(BF16), b_ref[...].astype(BF16), dims,
                                    preferred_element_type=F32)

        @pl.when(k == nk - 1)
        def _():
            res = epi(acc[...], *[e[...] for e in ex]) if epi is not None else (acc[...],)
            for o, r in zip(outs, res):
                o[...] = r.astype(o.dtype)

    return pl.pallas_call(
        body, grid=grid, in_specs=[a_spec, b_spec, *extra_specs], out_specs=out_specs,
        out_shape=out_shapes, scratch_shapes=[pltpu.VMEM(acc_shape, F32)],
        compiler_params=_params("parallel", "parallel", "arbitrary"), name=name,
    )(a, b, *extras)


def _mm_nn_w(name, a, wg, layer, colshard, out_dtypes, epi=None, extras=(), extra_specs_fn=None):
    m = a.shape[0]
    _, _, r, c = wg.shape
    tm = min(MM_TILE, m)
    if colshard:
        k, n = r, N_CHIPS * c
        tk, tn = k, _tile(c, MM_TILE)
        npc = c // tn
        b_spec = pl.BlockSpec((None, None, tk, tn), lambda i, j, kk: (layer, j // npc, kk, j % npc))
    else:
        k, n = N_CHIPS * r, c
        tk, tn = _tile(r, MM_TILE), _tile(c, MM_TILE)
        kpc = r // tk
        b_spec = pl.BlockSpec((None, None, tk, tn), lambda i, j, kk: (layer, kk // kpc, kk % kpc, j))
    assert a.shape[1] == k
    grid = (m // tm, n // tn, k // tk)
    out_spec = pl.BlockSpec((tm, tn), lambda i, j, kk: (i, j))
    ex_specs = extra_specs_fn(tm, tn) if extra_specs_fn is not None else ()
    return _matmul(name, a, wg, dims=NN, grid=grid,
                   a_spec=pl.BlockSpec((tm, tk), lambda i, j, kk: (i, kk)), b_spec=b_spec,
                   out_shapes=[jax.ShapeDtypeStruct((m, n), d) for d in out_dtypes],
                   out_specs=[out_spec for _ in out_dtypes], acc_shape=(tm, tn),
                   epi=epi, extras=extras, extra_specs=ex_specs)


def _mm_nt_w(name, a, wg, layer, colshard, out_dtype, epi=None, extras=(), extra_specs_fn=None):
    m = a.shape[0]
    _, _, r, c = wg.shape
    tm = min(MM_TILE, m)
    if colshard:
        kdim, n = N_CHIPS * c, r
        tk, tn = _tile(c, MM_TILE), _tile(r, MM_TILE)
        kpc = c // tk
        b_spec = pl.BlockSpec((None, None, tn, tk), lambda i, j, kk: (layer, kk // kpc, j, kk % kpc))
    else:
        kdim, n = c, N_CHIPS * r
        tk, tn = c, _tile(r, MM_TILE)
        npc = r // tn
        b_spec = pl.BlockSpec((None, None, tn, tk), lambda i, j, kk: (layer, j // npc, j % npc, kk))
    assert a.shape[1] == kdim
    grid = (m // tm, n // tn, kdim // tk)
    ex_specs = extra_specs_fn(tm, tn) if extra_specs_fn is not None else ()
    return _matmul(name, a, wg, dims=NT, grid=grid,
                   a_spec=pl.BlockSpec((tm, tk), lambda i, j, kk: (i, kk)), b_spec=b_spec,
                   out_shapes=[jax.ShapeDtypeStruct((m, n), out_dtype)],
                   out_specs=[pl.BlockSpec((tm, tn), lambda i, j, kk: (i, j))], acc_shape=(tm, tn),
                   epi=epi, extras=extras, extra_specs=ex_specs)[0]


def _mm_tn(name, a, b, colshard):
    s, m = a.shape
    n = b.shape[1]
    tk = min(2 * MM_TILE, s)
    if colshard:
        c = n // N_CHIPS
        tm, tn = _tile(m, MM_TILE // 2), _tile(c, MM_TILE)
        npc = c // tn
        out_shape = (N_CHIPS, m, c)
        out_spec = pl.BlockSpec((None, tm, tn), lambda i, j, kk: (j // npc, i, j % npc))
    else:
        r = m // N_CHIPS
        tm, tn = _tile(r, MM_TILE // 2), _tile(n, MM_TILE)
        mpc = r // tm
        out_shape = (N_CHIPS, r, n)
        out_spec = pl.BlockSpec((None, tm, tn), lambda i, j, kk: (i // mpc, i % mpc, j))
    grid = (m // tm, n // tn, s // tk)
    return _matmul(name, a, b, dims=TN, grid=grid,
                   a_spec=pl.BlockSpec((tk, tm), lambda i, j, kk: (kk, i)),
                   b_spec=pl.BlockSpec((tk, tn), lambda i, j, kk: (kk, j)),
                   out_shapes=[jax.ShapeDtypeStruct(out_shape, F32)], out_specs=[out_spec],
                   acc_shape=(tm, tn))[0]


def _row_spec(ts, d):
    return pl.BlockSpec((ts, d), lambda i: (i, 0))


def _vec_spec(d):
    return pl.BlockSpec((1, d), lambda i: (0, 0))


def _norm_mod(name, x, g, sc, sh):
    s, d = x.shape
    ts = min(ROW_TILE, s)

    def body(x_ref, g_ref, sc_ref, sh_ref, h_ref):
        xv = x_ref[...]
        r = lax.rsqrt(jnp.mean(xv * xv, axis=-1, keepdims=True) + EPS)
        h_ref[...] = (xv * r * g_ref[...] * (1.0 + sc_ref[...]) + sh_ref[...]).astype(BF16)

    return pl.pallas_call(
        body, grid=(s // ts,), in_specs=[_row_spec(ts, d), _vec_spec(d), _vec_spec(d), _vec_spec(d)],
        out_specs=_row_spec(ts, d), out_shape=jax.ShapeDtypeStruct((s, d), BF16),
        compiler_params=_params("parallel"), name=name)(x, g, sc, sh)


def _norm_mod_bwd(name, dh, x, g, sc, dres):
    s, d = x.shape
    ts = min(ROW_TILE, s)

    def body(dh_ref, x_ref, g_ref, sc_ref, dres_ref, dx_ref, dsh_ref, dsc_ref, dg_ref):
        @pl.when(pl.program_id(0) == 0)
        def _():
            dsh_ref[...] = jnp.zeros_like(dsh_ref)
            dsc_ref[...] = jnp.zeros_like(dsc_ref)
            dg_ref[...] = jnp.zeros_like(dg_ref)

        xv = x_ref[...]
        r = lax.rsqrt(jnp.mean(xv * xv, axis=-1, keepdims=True) + EPS)
        xhat = xv * r
        dhv = dh_ref[...]
        gv = g_ref[...]
        opsc = 1.0 + sc_ref[...]
        dxhat = dhv * (gv * opsc)
        mdot = jnp.mean(dxhat * xhat, axis=-1, keepdims=True)
        dx_ref[...] = dres_ref[...] + r * (dxhat - xhat * mdot)
        dhx = dhv * xhat
        dsh_ref[...] += jnp.sum(dhv, axis=0, keepdims=True)
        dsc_ref[...] += jnp.sum(dhx * gv, axis=0, keepdims=True)
        dg_ref[...] += jnp.sum(dhx * opsc, axis=0, keepdims=True)

    vec = jax.ShapeDtypeStruct((1, d), F32)
    return pl.pallas_call(
        body, grid=(s // ts,),
        in_specs=[_row_spec(ts, d), _row_spec(ts, d), _vec_spec(d), _vec_spec(d), _row_spec(ts, d)],
        out_specs=[_row_spec(ts, d), _vec_spec(d), _vec_spec(d), _vec_spec(d)],
        out_shape=[jax.ShapeDtypeStruct((s, d), F32), vec, vec, vec],
        compiler_params=_params("arbitrary"), name=name)(dh, x, g, sc, dres)


def _gate_bwd(name, dx, m, gt):
    s, d = dx.shape
    ts = min(ROW_TILE, s)

    def body(dx_ref, m_ref, gt_ref, dm_ref, dgt_ref):
        @pl.when(pl.program_id(0) == 0)
        def _():
            dgt_ref[...] = jnp.zeros_like(dgt_ref)

        dxv = dx_ref[...]
        dm_ref[...] = (dxv * gt_ref[...]).astype(BF16)
        dgt_ref[...] += jnp.sum(dxv * m_ref[...].astype(F32), axis=0, keepdims=True)

    return pl.pallas_call(
        body, grid=(s // ts,), in_specs=[_row_spec(ts, d), _row_spec(ts, d), _vec_spec(d)],
        out_specs=[_row_spec(ts, d), _vec_spec(d)],
        out_shape=[jax.ShapeDtypeStruct((s, d), BF16), jax.ShapeDtypeStruct((1, d), F32)],
        compiler_params=_params("arbitrary"), name=name)(dx, m, gt)


def _loss_grad(name, y, target):
    s, d = y.shape
    ts = min(ROW_TILE, s)

    def body(y_ref, t_ref, dy_ref, sq_ref):
        @pl.when(pl.program_id(0) == 0)
        def _():
            sq_ref[...] = jnp.zeros_like(sq_ref)

        e = y_ref[...] - t_ref[...]
        dy_ref[...] = e * (1.0 / d)
        sq_ref[...] += jnp.sum(e * e, axis=0, keepdims=True)

    return pl.pallas_call(
        body, grid=(s // ts,), in_specs=[_row_spec(ts, d), _row_spec(ts, d)],
        out_specs=[_row_spec(ts, d), _vec_spec(d)],
        out_shape=[jax.ShapeDtypeStruct((s, d), F32), jax.ShapeDtypeStruct((1, d), F32)],
        compiler_params=_params("arbitrary"), name=name)(y, target)


def _adamw(name, w, g, m, v):
    r, c = w.shape
    tr = r
    for cand in (512, 256, 128, 64, 32, 16, 8):
        if r % cand == 0 and cand * c * 4 <= 2 * 1024 * 1024:
            tr = cand
            break

    def body(w_ref, g_ref, m_ref, v_ref, d_ref, mo_ref, vo_ref):
        gv = g_ref[...]
        m2 = ADAM_B1 * m_ref[...] + (1.0 - ADAM_B1) * gv
        v2 = ADAM_B2 * v_ref[...] + (1.0 - ADAM_B2) * (gv * gv)
        m_hat = m2 / (1.0 - ADAM_B1 ** ADAM_STEP)
        v_hat = v2 / (1.0 - ADAM_B2 ** ADAM_STEP)
        d_ref[...] = -ADAM_LR * (m_hat / (jnp.sqrt(v_hat) + ADAM_EPS) + ADAM_WD * w_ref[...])
        mo_ref[...] = m2
        vo_ref[...] = v2

    spec = pl.BlockSpec((tr, c), lambda i: (i, 0))
    shp = jax.ShapeDtypeStruct((r, c), F32)
    return pl.pallas_call(
        body, grid=(r // tr,), in_specs=[spec] * 4, out_specs=[spec] * 3, out_shape=[shp] * 3,
        compiler_params=_params("parallel"), name=name)(w, g, m, v)


def _sum_rows8(name, x):
    n, r, c = x.shape

    def body(x_ref, o_ref):
        acc = x_ref[0]
        for i in range(1, n):
            acc = acc + x_ref[i]
        o_ref[...] = acc

    return pl.pallas_call(
        body, out_shape=jax.ShapeDtypeStruct((r, c), F32),
        in_specs=[pl.BlockSpec(memory_space=pltpu.VMEM)], out_specs=pl.BlockSpec(memory_space=pltpu.VMEM),
        name=name)(x)


def _outer_sum(name, ct, dm):
    d, n_seq = ct.shape
    n = dm.shape[1]
    tr = min(256, d)

    def body(ct_ref, dm_ref, o_ref):
        acc = ct_ref[:, 0:1] * dm_ref[0:1, :]
        for s_i in range(1, n_seq):
            acc = acc + ct_ref[:, s_i:s_i + 1] * dm_ref[s_i:s_i + 1, :]
        o_ref[...] = acc

    return pl.pallas_call(
        body, grid=(d // tr,),
        in_specs=[pl.BlockSpec((tr, n_seq), lambda i: (i, 0)), pl.BlockSpec((n_seq, n), lambda i: (0, 0))],
        out_specs=pl.BlockSpec((tr, n), lambda i: (i, 0)), out_shape=jax.ShapeDtypeStruct((d, n), F32),
        compiler_params=_params("parallel"), name=name)(ct, dm)


def _ada_mod(name, c_act, w_ada_l, bias):
    n_seq, d = c_act.shape
    n = w_ada_l.shape[1]
    tn = _tile(n, 768)

    def epi(acc, b):
        return (acc + b,)

    return _matmul(name, c_act, w_ada_l, dims=NN, grid=(1, n // tn, 1),
                   a_spec=pl.BlockSpec((n_seq, d), lambda i, j, kk: (0, 0)),
                   b_spec=pl.BlockSpec((d, tn), lambda i, j, kk: (0, j)),
                   out_shapes=[jax.ShapeDtypeStruct((n_seq, n), F32)],
                   out_specs=[pl.BlockSpec((n_seq, tn), lambda i, j, kk: (0, j))], acc_shape=(n_seq, tn),
                   epi=epi, extras=(bias,), extra_specs=(pl.BlockSpec((1, tn), lambda i, j, kk: (0, j)),))[0]


def _silu_rows(name, c_row):
    d = c_row.shape[1]

    def body(c_ref, o_ref):
        cv = c_ref[...]
        o_ref[...] = jnp.broadcast_to(cv / (1.0 + jnp.exp(-cv)), (8, d))

    return pl.pallas_call(
        body, out_shape=jax.ShapeDtypeStruct((8, d), F32),
        in_specs=[pl.BlockSpec(memory_space=pltpu.VMEM)], out_specs=pl.BlockSpec(memory_space=pltpu.VMEM),
        name=name)(c_row)


def _sb_masks(t):
    row = lax.broadcasted_iota(jnp.int32, (t, t), 0)
    col = lax.broadcasted_iota(jnp.int32, (t, t), 1)
    strict = col < row
    tri = (row > col).astype(BF16)
    return strict, tri


def _col_minus_row(t):
    return lax.broadcasted_iota(jnp.int32, (t, t), 1) - lax.broadcasted_iota(jnp.int32, (t, t), 0)


def _sweep_left(step, jd, carry, runs_of):
    def alive(c):
        runs = runs_of(c)
        top = runs[0]
        for r in runs[1:]:
            top = jnp.maximum(top, r)
        return jnp.max(top)

    def cond(state):
        j, _, top = state
        return jnp.logical_and(j >= 0, top > DEAD_LOG)

    def body(state):
        j, c, _ = state
        c = step(j, c, False)
        return j - 1, c, alive(c)

    return lax.while_loop(cond, body, (jd - 1, carry, alive(carry)))[1]


def _log_sigmoids(z):
    sp = jnp.log(1.0 + jnp.exp(-jnp.abs(z)))
    return jnp.minimum(z, 0.0) - sp, jnp.minimum(-z, 0.0) - sp, sp


def _sb_fwd(name, proj, n_pairs):
    s = proj.shape[0]
    t = min(SB_TILE, s)
    nq = s // t

    heads = [slice(h * HEAD_DIM, (h + 1) * HEAD_DIM) for h in range(LANES // HEAD_DIM)]

    assert nq % 2 == 0

    def body(q_ref, k_ref, v_ref, o_ref):
        qi = pl.program_id(1)
        _, tri = _sb_masks(t)
        diff = _col_minus_row(t)
        qs = [(q_ref[:, sl] * SCALE).astype(BF16) for sl in heads]
        jd = qi // 2
        chains = [(sb, h) for sb in (1, 0) for h in range(len(heads))]

        def step(j, carry, masked):
            ks = [pl.multiple_of((2 * j + sb) * t, t) for sb, _ in chains]
            k_ = [k_ref[pl.ds(ks[i], t), heads[h]].astype(BF16) for i, (_, h) in enumerate(chains)]
            v_ = [v_ref[pl.ds(ks[i], t), heads[h]].astype(BF16) for i, (_, h) in enumerate(chains)]
            zs = [lax.dot_general(qs[h], k_[i], NT, preferred_element_type=F32) for i, (_, h) in enumerate(chains)]
            lbs, lms, keeps = [], [], []
            for (sb, _), z in zip(chains, zs):
                lb, lm, _ = _log_sigmoids(z)
                keep = None
                if masked:
                    keep = diff < (qi - 2 * j - sb) * t
                    lm = jnp.where(keep, lm, 0.0)
                lbs.append(lb)
                lms.append(lm)
                keeps.append(keep)
            css = [_split_dot(lm, tri, 2) for lm in lms]
            sums = [jnp.sum(lm, axis=1, keepdims=True) for lm in lms]
            ws = []
            for i, (sb, h) in enumerate(chains):
                run = carry[h][1] if sb == 1 else carry[h][1] + sums[h]
                w = jnp.exp(lbs[i] + css[i] + run)
                if masked:
                    w = jnp.where(keeps[i], w, 0.0)
                ws.append(w.astype(BF16))
            pvs = [jnp.dot(w, v, preferred_element_type=F32) for w, v in zip(ws, v_)]
            nh = len(heads)
            return tuple((carry[h][0] + pvs[h] + pvs[nh + h], carry[h][1] + sums[h] + sums[nh + h]) for h in range(nh))

        init = tuple((jnp.zeros((t, HEAD_DIM), F32), jnp.zeros((t, 1), F32)) for _ in heads)
        carry = step(jd, init, True)
        carry = _sweep_left(step, jd, carry, lambda c: [run for _, run in c])
        for (acc, _), sl in zip(carry, heads):
            o_ref[:, sl] = acc

    return pl.pallas_call(
        body, grid=(n_pairs, nq),
        in_specs=[pl.BlockSpec((t, LANES), lambda p, i: (i, p)),
                  pl.BlockSpec((s, LANES), lambda p, i: (0, n_pairs + p)),
                  pl.BlockSpec((s, LANES), lambda p, i: (0, 2 * n_pairs + p))],
        out_specs=pl.BlockSpec((t, LANES), lambda p, i: (i, p)),
        out_shape=jax.ShapeDtypeStruct((s, n_pairs * LANES), F32),
        compiler_params=_params("parallel", "arbitrary"), name=name)(proj, proj, proj)


def _sb_bwd(name, proj, o_sb, dmixed, n_pairs):
    s = proj.shape[0]
    t = min(SB_TILE, s)
    nq = s // t
    heads = [slice(h * HEAD_DIM, (h + 1) * HEAD_DIM) for h in range(LANES // HEAD_DIM)]

    def body(q_ref, k_ref, v_ref, o_ref, do_ref, dq_ref, dk_ref, dv_ref):
        qi = pl.program_id(1)

        @pl.when(qi == 0)
        def _():
            dk_ref[...] = jnp.zeros_like(dk_ref)
            dv_ref[...] = jnp.zeros_like(dv_ref)

        _, tri = _sb_masks(t)
        diff = _col_minus_row(t)
        qs = [(q_ref[:, sl] * SCALE).astype(BF16) for sl in heads]
        dos = [do_ref[:, sl].astype(BF16) for sl in heads]
        dsums = [jnp.sum(do.astype(F32) * o_ref[:, sl], axis=1, keepdims=True) for do, sl in zip(dos, heads)]
        jd = qi // 2
        nh = len(heads)
        chains = [(sb, h) for sb in (1, 0) for h in range(nh)]

        def step(j, carry, masked):
            ks = [pl.multiple_of((2 * j + sb) * t, t) for sb, _ in chains]
            k_ = [k_ref[pl.ds(ks[i], t), heads[h]].astype(BF16) for i, (_, h) in enumerate(chains)]
            v_ = [v_ref[pl.ds(ks[i], t), heads[h]].astype(BF16) for i, (_, h) in enumerate(chains)]
            zs = [lax.dot_general(qs[h], k_[i], NT, preferred_element_type=F32) for i, (_, h) in enumerate(chains)]
            dws = [lax.dot_general(dos[h], v_[i], NT, preferred_element_type=F32) for i, (_, h) in enumerate(chains)]
            lbs, lms, betas, ombs, keeps = [], [], [], [], []
            for (sb, _), z in zip(chains, zs):
                ez = jnp.exp(-jnp.abs(z))
                lb = jnp.minimum(z, 0.0) - jnp.log(1.0 + ez)
                lm = lb - z
                betas.append(jnp.exp(lb))
                ombs.append(jnp.exp(lm))
                keep = None
                if masked:
                    keep = diff < (qi - 2 * j - sb) * t
                    lm = jnp.where(keep, lm, 0.0)
                lbs.append(lb)
                lms.append(lm)
                keeps.append(keep)
            css = [_split_dot(lm, tri, 2) for lm in lms]
            sums_f = [jnp.sum(lm, axis=1, keepdims=True) for lm in lms]
            wbs, es = [], []
            for i, (sb, h) in enumerate(chains):
                run_f = carry[h][1] if sb == 1 else carry[h][1] + sums_f[h]
                w = jnp.exp(lbs[i] + css[i] + run_f)
                if masked:
                    w = jnp.where(keeps[i], w, 0.0)
                wb = w.astype(BF16)
                wbs.append(wb)
                es.append(wb.astype(F32) * dws[i])
            for i, (_, h) in enumerate(chains):
                dv_ref[pl.ds(ks[i], t), heads[h]] += lax.dot_general(wbs[i], dos[h], TN, preferred_element_type=F32)
            ecs = [_split_dot(e, tri, 2) for e in es]
            sums_e = [jnp.sum(e, axis=1, keepdims=True) for e in es]
            dzbs = []
            for i, (sb, h) in enumerate(chains):
                run_e = carry[h][2] if sb == 1 else carry[h][2] + sums_e[h]
                left = dsums[h] - (ecs[i] + es[i] + run_e)
                dz = es[i] * ombs[i] - left * betas[i]
                if masked:
                    dz = jnp.where(keeps[i], dz, 0.0)
                dzbs.append(dz.astype(BF16))
            dqs = []
            for i, (_, h) in enumerate(chains):
                dk_ref[pl.ds(ks[i], t), heads[h]] += lax.dot_general(dzbs[i], qs[h], TN, preferred_element_type=F32)
                dqs.append(jnp.dot(dzbs[i], k_[i], preferred_element_type=F32))
            return tuple((carry[h][0] + dqs[h] + dqs[nh + h], carry[h][1] + sums_f[h] + sums_f[nh + h],
                          carry[h][2] + sums_e[h] + sums_e[nh + h]) for h in range(nh))

        zero = jnp.zeros((t, 1), F32)
        init = tuple((jnp.zeros((t, HEAD_DIM), F32), zero, zero) for _ in heads)
        carry = step(jd, init, True)
        carry = _sweep_left(step, jd, carry, lambda c: [run_f for _, run_f, _ in c])
        for (dq, _, _), sl in zip(carry, heads):
            dq_ref[:, sl] = dq * SCALE

    blk = pl.BlockSpec((t, LANES), lambda p, i: (i, p))
    full = pl.BlockSpec((s, LANES), lambda p, i: (0, p))
    shp = jax.ShapeDtypeStruct((s, n_pairs * LANES), F32)
    return pl.pallas_call(
        body, grid=(n_pairs, nq),
        in_specs=[blk,
                  pl.BlockSpec((s, LANES), lambda p, i: (0, n_pairs + p)),
                  pl.BlockSpec((s, LANES), lambda p, i: (0, 2 * n_pairs + p)),
                  blk, blk],
        out_specs=[blk, full, full], out_shape=[shp, shp, shp],
        compiler_params=_params("parallel", "arbitrary"), name=name)(proj, proj, proj, o_sb, dmixed)


def _group_mean_matrix():
    row = lax.broadcasted_iota(jnp.int32, (LANES, LANES), 0)
    col = lax.broadcasted_iota(jnp.int32, (LANES, LANES), 1)
    same_head = (row < HEAD_DIM) == (col < HEAD_DIM)
    return jnp.where(same_head, 1.0 / HEAD_DIM, 0.0).astype(BF16)


def _qk_norm(name, proj, gq2, gk2, n_pairs):
    s = proj.shape[0]
    ts = min(ROW_TILE, s)
    pb = PAD // ts
    c0 = 3 * n_pairs

    def body(q_ref, k_ref, v_ref, gq_ref, gk_ref, qn_ref, kn_ref, vp_ref):
        r_i = pl.program_id(1)

        @pl.when(r_i < pb)
        def _():
            qn_ref[...] = jnp.zeros_like(qn_ref)
            kn_ref[...] = jnp.zeros_like(kn_ref)
            vp_ref[...] = jnp.zeros_like(vp_ref)

        @pl.when(r_i >= pb)
        def _():
            gm = _group_mean_matrix()
            for src, g_ref, dst in ((q_ref, gq_ref, qn_ref), (k_ref, gk_ref, kn_ref)):
                xv = src[...]
                r = lax.rsqrt(_split_dot(xv * xv, gm, 3) + EPS)
                dst[...] = (xv * r * g_ref[...]).astype(BF16)
            vp_ref[...] = v_ref[...].astype(BF16)

    def in_spec(off):
        return pl.BlockSpec((ts, LANES), lambda p, i: (jnp.maximum(i - pb, 0), c0 + off * n_pairs + p))

    gspec = pl.BlockSpec((1, LANES), lambda p, i: (0, 0))
    ospec = pl.BlockSpec((ts, LANES), lambda p, i: (i, p))
    shp = jax.ShapeDtypeStruct((s + PAD, n_pairs * LANES), BF16)
    return pl.pallas_call(
        body, grid=(n_pairs, (s + PAD) // ts),
        in_specs=[in_spec(0), in_spec(1), in_spec(2), gspec, gspec],
        out_specs=[ospec, ospec, ospec], out_shape=[shp, shp, shp],
        compiler_params=_params("parallel", "arbitrary"), name=name)(proj, proj, proj, gq2, gk2)


def _qk_norm_bwd(name, dqn, dkn, proj, gq2, gk2, n_pairs):
    s = proj.shape[0]
    ts = min(ROW_TILE, s)
    pb = PAD // ts
    c0 = 3 * n_pairs
    n_r = s // ts

    def body(dqn_ref, dkn_ref, q_ref, k_ref, gq_ref, gk_ref, dq_ref, dk_ref, dgq_ref, dgk_ref):
        first = jnp.logical_and(pl.program_id(0) == 0, pl.program_id(1) == 0)
        last = jnp.logical_and(pl.program_id(0) == n_pairs - 1, pl.program_id(1) == n_r - 1)

        @pl.when(first)
        def _():
            dgq_ref[...] = jnp.zeros_like(dgq_ref)
            dgk_ref[...] = jnp.zeros_like(dgk_ref)

        gm = _group_mean_matrix()
        for dn_ref, x_ref, g_ref, dx_ref, dg_ref in ((dqn_ref, q_ref, gq_ref, dq_ref, dgq_ref),
                                                     (dkn_ref, k_ref, gk_ref, dk_ref, dgk_ref)):
            xv = x_ref[...]
            r = lax.rsqrt(_split_dot(xv * xv, gm, 3) + EPS)
            xhat = xv * r
            dn = dn_ref[...]
            tg = dn * g_ref[...]
            md = _split_dot(tg * xhat, gm, 3)
            dx_ref[...] = (r * (tg - xhat * md)).astype(BF16)
            dg_ref[...] += jnp.sum(dn * xhat, axis=0, keepdims=True)

        @pl.when(last)
        def _():
            for dg_ref in (dgq_ref, dgk_ref):
                gv = dg_ref[...]
                dg_ref[...] = gv + pltpu.roll(gv, HEAD_DIM, axis=1)

    def x_spec(off):
        return pl.BlockSpec((ts, LANES), lambda p, i: (i, c0 + off * n_pairs + p))

    dn_q = pl.BlockSpec((ts, LANES), lambda p, i: (i, p))
    dn_k = pl.BlockSpec((ts, LANES), lambda p, i: (i + pb, p))
    gspec = pl.BlockSpec((1, LANES), lambda p, i: (0, 0))
    ospec = pl.BlockSpec((ts, LANES), lambda p, i: (i, p))
    shp = jax.ShapeDtypeStruct((s, n_pairs * LANES), BF16)
    vec = jax.ShapeDtypeStruct((1, LANES), F32)
    return pl.pallas_call(
        body, grid=(n_pairs, n_r),
        in_specs=[dn_q, dn_k, x_spec(0), x_spec(1), gspec, gspec],
        out_specs=[ospec, ospec, gspec, gspec], out_shape=[shp, shp, vec, vec],
        compiler_params=_params("arbitrary", "arbitrary"), name=name)(dqn, dkn, proj, proj, gq2, gk2)


def _ca_scores(qc, kw, bias, n):
    sc = lax.dot_general(qc, kw, NT, preferred_element_type=F32) * SCALE + bias
    jpos = lax.broadcasted_iota(jnp.int32, (CHUNK, BAND), 1)
    return jnp.where(n * CHUNK + jpos >= PAD, sc, NEG_INF)


def _ca_fwd(name, qn, knp, vp, bias_t, n_pairs):
    sp_rows = knp.shape[0]
    s = sp_rows - PAD
    t = min(CA_TILE, s)
    cpb = t // CHUNK
    pb = PAD // t

    def body(q_ref, k_ref, v_ref, b_ref, o_ref):
        qi = pl.program_id(1)
        work = [(ci, h) for ci in range(cpb) for h in range(LANES // HEAD_DIM)]
        scs, vws = [], []
        for ci, h in work:
            n = qi * cpb + ci
            ws = pl.multiple_of(n * CHUNK, CHUNK)
            sl = slice(h * HEAD_DIM, (h + 1) * HEAD_DIM)
            scs.append(_ca_scores(q_ref[ci * CHUNK:(ci + 1) * CHUNK, sl], k_ref[pl.ds(ws, BAND), sl], b_ref[h], n))
            vws.append(v_ref[pl.ds(ws, BAND), sl])
        ps, dens = [], []
        for sc in scs:
            p = jnp.exp(sc - jnp.max(sc, axis=1, keepdims=True))
            dens.append(jnp.sum(p, axis=1, keepdims=True))
            ps.append(p.astype(BF16))
        for (ci, h), p, den, vw in zip(work, ps, dens, vws):
            sl = slice(h * HEAD_DIM, (h + 1) * HEAD_DIM)
            o_ref[ci * CHUNK:(ci + 1) * CHUNK, sl] = jnp.dot(p, vw, preferred_element_type=F32) / den

    full = pl.BlockSpec((sp_rows, LANES), lambda p, i: (0, p))
    return pl.pallas_call(
        body, grid=(n_pairs, s // t),
        in_specs=[pl.BlockSpec((t, LANES), lambda p, i: (i + pb, p)), full, full,
                  pl.BlockSpec((2, CHUNK, BAND), lambda p, i: (p, 0, 0))],
        out_specs=pl.BlockSpec((t, LANES), lambda p, i: (i, p)),
        out_shape=jax.ShapeDtypeStruct((s, n_pairs * LANES), F32),
        compiler_params=_params("parallel", "arbitrary"), name=name)(qn, knp, vp, bias_t)


def _ca_bwd(name, qn, knp, vp, bias_t, dmixed, n_pairs, col0):
    sp_rows = knp.shape[0]
    s = sp_rows - PAD
    t = min(CA_TILE, s)
    cpb = t // CHUNK
    pb = PAD // t

    def body(q_ref, k_ref, v_ref, b_ref, do_ref, dq_ref, dk_ref, dv_ref, db_ref):
        qi = pl.program_id(1)

        @pl.when(qi == 0)
        def _():
            dk_ref[...] = jnp.zeros_like(dk_ref)
            dv_ref[...] = jnp.zeros_like(dv_ref)
            db_ref[...] = jnp.zeros_like(db_ref)

        work = [(ci, h) for ci in range(cpb) for h in range(LANES // HEAD_DIM)]
        qcs, kws, dos, scs, dps, wss = [], [], [], [], [], []
        for ci, h in work:
            n = qi * cpb + ci
            ws = pl.multiple_of(n * CHUNK, CHUNK)
            sl = slice(h * HEAD_DIM, (h + 1) * HEAD_DIM)
            qc = q_ref[ci * CHUNK:(ci + 1) * CHUNK, sl]
            kw = k_ref[pl.ds(ws, BAND), sl]
            do = do_ref[ci * CHUNK:(ci + 1) * CHUNK, sl].astype(BF16)
            scs.append(_ca_scores(qc, kw, b_ref[h], n))
            dps.append(lax.dot_general(do, v_ref[pl.ds(ws, BAND), sl], NT, preferred_element_type=F32))
            qcs.append(qc)
            kws.append(kw)
            dos.append(do)
            wss.append(ws)
        pbs, dss = [], []
        for sc, dp in zip(scs, dps):
            p = jnp.exp(sc - jnp.max(sc, axis=1, keepdims=True))
            p = p / jnp.sum(p, axis=1, keepdims=True)
            dss.append(p * (dp - jnp.sum(p * dp, axis=1, keepdims=True)))
            pbs.append(p.astype(BF16))
        for i, (ci, h) in enumerate(work):
            sl = slice(h * HEAD_DIM, (h + 1) * HEAD_DIM)
            dsb = dss[i].astype(BF16)
            dv_ref[pl.ds(wss[i], BAND), sl] += lax.dot_general(pbs[i], dos[i], TN, preferred_element_type=F32)
            dk_ref[pl.ds(wss[i], BAND), sl] += lax.dot_general(dsb, qcs[i], TN, preferred_element_type=F32) * SCALE
            dq_ref[ci * CHUNK:(ci + 1) * CHUNK, sl] = jnp.dot(dsb, kws[i], preferred_element_type=F32) * SCALE
        for h in range(LANES // HEAD_DIM):
            total = dss[h]
            for ci in range(1, cpb):
                total = total + dss[ci * (LANES // HEAD_DIM) + h]
            db_ref[h] += total

    full_in = pl.BlockSpec((sp_rows, LANES), lambda p, i: (0, p))
    btile = pl.BlockSpec((2, CHUNK, BAND), lambda p, i: (p, 0, 0))
    pad_shape = jax.ShapeDtypeStruct((sp_rows, n_pairs * LANES), F32)
    return pl.pallas_call(
        body, grid=(n_pairs, s // t),
        in_specs=[pl.BlockSpec((t, LANES), lambda p, i: (i + pb, p)), full_in, full_in, btile,
                  pl.BlockSpec((t, LANES), lambda p, i: (i, col0 + p))],
        out_specs=[pl.BlockSpec((t, LANES), lambda p, i: (i, p)), full_in, full_in, btile],
        out_shape=[jax.ShapeDtypeStruct((s, n_pairs * LANES), F32), pad_shape, pad_shape,
                   jax.ShapeDtypeStruct(bias_t.shape, F32)],
        compiler_params=_params("parallel", "arbitrary"), name=name)(qn, knp, vp, bias_t, dmixed)


def _bias_tile(rb):
    n_h = rb.shape[0]
    far = jnp.broadcast_to(rb[:, 2 * REL_CLIP:], (n_h, PAD + CHUNK - REL_CLIP))
    near = rb[:, REL_CLIP - (CHUNK - 1):2 * REL_CLIP][:, ::-1]
    ext = jnp.concatenate([far, near], axis=1)
    return jnp.stack([ext[:, CHUNK - 1 - i:CHUNK - 1 - i + BAND] for i in range(CHUNK)], axis=1)


def _rel_bias_grad(name, db_t):
    _, n_h, _ = db_t.shape
    n_out = 3 * LANES
    assert CHUNK == 64 and REL_CLIP == 128 and LEFT_CHUNKS == 8

    def body(z_ref, o_ref):
        a_idx = lax.broadcasted_iota(jnp.int32, (LANES, n_out), 0)
        r_idx = lax.broadcasted_iota(jnp.int32, (LANES, n_out), 1)
        out = jnp.zeros((n_h, n_out), F32)
        for mm, base in ((0, 191), (1, 255), (2, 319)):
            b = LEFT_CHUNKS - mm
            acc = jnp.zeros((n_h, LANES), F32)
            for i in range(CHUNK):
                row = z_ref[i, :, b * CHUNK:(b + 1) * CHUNK]
                rowp = jnp.concatenate([row, jnp.zeros((n_h, LANES - CHUNK), F32)], axis=1)
                acc = acc + pltpu.roll(rowp, CHUNK - 1 - i, axis=1)
            place = (r_idx == jnp.minimum(base - a_idx, 2 * REL_CLIP)).astype(BF16)
            out = out + _split_dot(acc, place, 3)
        far = jnp.zeros((n_h, 1), F32)
        for i in range(CHUNK):
            far = far + jnp.sum(z_ref[i, :, 0:(LEFT_CHUNKS - 2) * CHUNK], axis=1, keepdims=True)
        lane = lax.broadcasted_iota(jnp.int32, (n_h, n_out), 1)
        o_ref[...] = out + jnp.where(lane == 2 * REL_CLIP, far, 0.0)

    return pl.pallas_call(
        body, out_shape=jax.ShapeDtypeStruct((n_h, n_out), F32),
        in_specs=[pl.BlockSpec(memory_space=pltpu.VMEM)], out_specs=pl.BlockSpec(memory_space=pltpu.VMEM),
        name=name)(db_t)


def _mesh_pos():
    return lax.axis_index("x"), lax.axis_index("y"), lax.axis_index("c")


def _other_chips(x, y):
    return [(1 - x, y), (x, 1 - y), (1 - x, 1 - y)]


def _all_gather_small(name, blk):
    m_per, n = blk.shape

    def body(x_ref, out_ref, send_sems, recv_sems, local_sem):
        x, y, c = _mesh_pos()
        me, sibling = (x, y, c), (x, y, 1 - c)
        chips = _other_chips(x, y)

        def rows(px, py, pc):
            return out_ref.at[pl.ds((4 * px + 2 * py + pc) * m_per, m_per), :]

        def copy(k, block, to, src=None):
            return pltpu.make_async_remote_copy(
                src_ref=rows(*block) if src is None else src, dst_ref=rows(*block),
                send_sem=send_sems.at[k], recv_sem=recv_sems.at[k], device_id=to, device_id_type=MESH)

        mine = pltpu.make_async_copy(x_ref, rows(*me), local_sem)
        mine.start()
        first = [copy(0, me, sibling, src=x_ref)]
        first += [copy(1 + j, me, (*chip, c), src=x_ref) for j, chip in enumerate(chips)]
        for cp in first:
            cp.start()
        passed = [copy(4 + j, (*chip, c), sibling) for j, chip in enumerate(chips)]
        for j, chip in enumerate(chips):
            copy(1 + j, (*chip, c), me).wait_recv()
            passed[j].start()
        copy(0, sibling, me).wait_recv()
        for j, chip in enumerate(chips):
            copy(4 + j, (*chip, 1 - c), me).wait_recv()
        for cp in first + passed:
            cp.wait_send()
        mine.wait()

    return pl.pallas_call(
        body, out_shape=jax.ShapeDtypeStruct((N_DEV * m_per, n), blk.dtype),
        in_specs=[pl.BlockSpec(memory_space=pltpu.VMEM)], out_specs=pl.BlockSpec(memory_space=pltpu.VMEM),
        scratch_shapes=[pltpu.SemaphoreType.DMA((7,)), pltpu.SemaphoreType.DMA((7,)), pltpu.SemaphoreType.DMA],
        compiler_params=pltpu.CompilerParams(vmem_limit_bytes=VMEM_LIMIT), name=name)(blk)


HBM_SPEC = pl.BlockSpec(memory_space=pltpu.HBM)


def _all_gather_weights(name, shards):
    n = len(shards)

    def body(*refs):
        srcs, outs = refs[:n], refs[n:2 * n]
        send_sems, recv_sems, local_sems = refs[2 * n:]
        x, y, c = _mesh_pos()
        me, sibling = (x, y, c), (x, y, 1 - c)
        chips = _other_chips(x, y)

        def win(t, px, py, pc):
            rh = shards[t].shape[1] // 2
            return outs[t].at[:, 2 * px + py, pl.ds(pc * rh, rh), :]

        def own(t):
            rh = shards[t].shape[1] // 2
            return srcs[t].at[:, pl.ds(c * rh, rh), :]

        def copy(t, k, block, to, src=None):
            return pltpu.make_async_remote_copy(
                src_ref=win(t, *block) if src is None else src, dst_ref=win(t, *block),
                send_sem=send_sems.at[t, k], recv_sem=recv_sems.at[t, k], device_id=to, device_id_type=MESH)

        mine = [pltpu.make_async_copy(own(t), win(t, *me), local_sems.at[t]) for t in range(n)]
        for cp in mine:
            cp.start()
        first = []
        for t in range(n):
            first.append(copy(t, 0, me, sibling, src=own(t)))
            first += [copy(t, 1 + j, me, (*chip, c), src=own(t)) for j, chip in enumerate(chips)]
        for cp in first:
            cp.start()
        passed = []
        for j, chip in enumerate(chips):
            for t in range(n):
                copy(t, 1 + j, (*chip, c), me).wait_recv()
                fwd = copy(t, 4 + j, (*chip, c), sibling)
                fwd.start()
                passed.append(fwd)
        for t in range(n):
            copy(t, 0, sibling, me).wait_recv()
            for j, chip in enumerate(chips):
                copy(t, 4 + j, (*chip, 1 - c), me).wait_recv()
        for cp in first + passed:
            cp.wait_send()
        for cp in mine:
            cp.wait()

    out_shape = [jax.ShapeDtypeStruct((w.shape[0], N_CHIPS, w.shape[1], w.shape[2]), w.dtype) for w in shards]
    return pl.pallas_call(
        body, out_shape=out_shape, in_specs=[HBM_SPEC] * n, out_specs=[HBM_SPEC] * n,
        scratch_shapes=[pltpu.SemaphoreType.DMA((n, 7)), pltpu.SemaphoreType.DMA((n, 7)),
                        pltpu.SemaphoreType.DMA((n,))],
        name=name)(*shards)


def _exchange_sibling_halves(name, grads):
    n = len(grads)

    def body(*refs):
        srcs, outs = refs[:n], refs[n:2 * n]
        send_sems, recv_sems = refs[2 * n:]
        x, y, c = _mesh_pos()
        cps = []
        for t in range(n):
            rh = grads[t].shape[1] // 2
            cps.append(pltpu.make_async_remote_copy(
                src_ref=srcs[t].at[:, pl.ds((1 - c) * rh, rh), :], dst_ref=outs[t],
                send_sem=send_sems.at[t], recv_sem=recv_sems.at[t], device_id=(x, y, 1 - c), device_id_type=MESH))
        for cp in cps:
            cp.start()
        for cp in cps:
            cp.wait()

    out_shape = [jax.ShapeDtypeStruct((g.shape[0], g.shape[1] // 2, g.shape[2]), g.dtype) for g in grads]
    return pl.pallas_call(
        body, out_shape=out_shape, in_specs=[HBM_SPEC] * n, out_specs=[HBM_SPEC] * n,
        scratch_shapes=[pltpu.SemaphoreType.DMA((n,)), pltpu.SemaphoreType.DMA((n,))],
        name=name)(*grads)


def _add_halves(name, g, recv, c_idx):
    nb, r, c = g.shape
    rh = r // 2
    tr = rh
    for cand in (512, 256, 128, 64):
        if rh % cand == 0:
            tr = cand
            break
    g4 = g.reshape(nb, 2, rh, c)

    def body(c_ref, g_ref, r_ref, o_ref):
        o_ref[...] = (g_ref[...] + r_ref[...]).astype(BF16)

    grid_spec = pltpu.PrefetchScalarGridSpec(
        num_scalar_prefetch=1, grid=(nb, rh // tr),
        in_specs=[pl.BlockSpec((None, None, tr, c), lambda j, i, cr: (j, cr[0], i, 0)),
                  pl.BlockSpec((None, tr, c), lambda j, i, cr: (j, i, 0))],
        out_specs=pl.BlockSpec((None, tr, c), lambda j, i, cr: (j, i, 0)))
    return pl.pallas_call(
        body, grid_spec=grid_spec, out_shape=jax.ShapeDtypeStruct((nb, rh, c), BF16),
        compiler_params=_params("parallel", "parallel"), name=name)(c_idx, g4, recv)


def _exchange_chips(name, parts, n_layers):
    n = len(parts)
    n_w = n // n_layers

    def body(*refs):
        srcs, outs = refs[:n], refs[n:n + n_w]
        send_sems, recv_sems = refs[n + n_w:]
        x, y, c = _mesh_pos()
        cps = []
        for t in range(n):
            w_i, l_i = divmod(t, n_layers)
            for k, (px, py) in enumerate(_other_chips(x, y)):
                cps.append(pltpu.make_async_remote_copy(
                    src_ref=srcs[t].at[2 * px + py], dst_ref=outs[w_i].at[l_i, k],
                    send_sem=send_sems.at[t, k], recv_sem=recv_sems.at[t, k],
                    device_id=(px, py, c), device_id_type=MESH))
        for cp in cps:
            cp.start()
        for cp in cps:
            cp.wait()

    out_shape = [jax.ShapeDtypeStruct((n_layers, 3) + parts[w_i * n_layers].shape[1:], BF16) for w_i in range(n_w)]
    return pl.pallas_call(
        body, out_shape=out_shape, in_specs=[HBM_SPEC] * n, out_specs=[HBM_SPEC] * n_w,
        scratch_shapes=[pltpu.SemaphoreType.DMA((n, 3)), pltpu.SemaphoreType.DMA((n, 3))],
        name=name)(*parts)


def _sum_chips(name, own_parts, recv, pos_idx):
    n_layers = len(own_parts)
    _, _, rh, c = recv.shape
    tr = rh
    for cand in (512, 256, 128, 64):
        if rh % cand == 0:
            tr = cand
            break
    nb = rh // tr

    def body(pos_ref, *refs):
        own_refs, r_ref, o_ref = refs[:n_layers], refs[n_layers], refs[n_layers + 1]
        for l_i in range(n_layers):
            @pl.when(pl.program_id(0) == l_i)
            def _():
                acc = own_refs[l_i][...].astype(F32)
                for k in range(3):
                    acc = acc + r_ref[k].astype(F32)
                o_ref[...] = acc

    grid_spec = pltpu.PrefetchScalarGridSpec(
        num_scalar_prefetch=1, grid=(n_layers, nb),
        in_specs=[pl.BlockSpec((None, tr, c), lambda l, i, pr: (pr[0], i, 0)) for _ in range(n_layers)]
        + [pl.BlockSpec((None, 3, tr, c), lambda l, i, pr: (l, 0, i, 0))],
        out_specs=pl.BlockSpec((None, tr, c), lambda l, i, pr: (l, pr[1] * nb + i, 0)))
    return pl.pallas_call(
        body, grid_spec=grid_spec, out_shape=jax.ShapeDtypeStruct((n_layers, 2 * rh, c), F32),
        compiler_params=_params("parallel", "parallel"), name=name)(pos_idx, *own_parts, recv)


def _share_with_sibling(name, grads):
    n = len(grads)

    def body(*refs):
        outs = refs[n:2 * n]
        send_sems, recv_sems = refs[2 * n:]
        x, y, c = _mesh_pos()
        cps, landed = [], []
        for t in range(n):
            rh = grads[t].shape[1] // 2
            mine = outs[t].at[:, pl.ds(c * rh, rh), :]
            theirs = outs[t].at[:, pl.ds((1 - c) * rh, rh), :]
            cps.append(pltpu.make_async_remote_copy(
                src_ref=mine, dst_ref=mine, send_sem=send_sems.at[t], recv_sem=recv_sems.at[t],
                device_id=(x, y, 1 - c), device_id_type=MESH))
            landed.append(pltpu.make_async_remote_copy(
                src_ref=mine, dst_ref=theirs, send_sem=send_sems.at[t], recv_sem=recv_sems.at[t],
                device_id=(x, y, 1 - c), device_id_type=MESH))
        for cp in cps:
            cp.start()
        for cp, ld in zip(cps, landed):
            cp.wait_send()
            ld.wait_recv()

    out_shape = [jax.ShapeDtypeStruct(g.shape, g.dtype) for g in grads]
    return pl.pallas_call(
        body, out_shape=out_shape, in_specs=[HBM_SPEC] * n, out_specs=[HBM_SPEC] * n,
        input_output_aliases={t: t for t in range(n)},
        scratch_shapes=[pltpu.SemaphoreType.DMA((n,)), pltpu.SemaphoreType.DMA((n,))],
        name=name)(*grads)


def _row(v):
    return v.reshape(1, -1)


def kernel(x, c, g_norm1, w_in, g_q, g_k, rel_bias, w_o, g_norm2, w1, w2, w_ada, b_ada, loss_target, m_g_norm1, m_w_in, m_g_q, m_g_k, m_rel_bias, m_w_o, m_g_norm2, m_w1, m_w2, m_w_ada, m_b_ada, v_g_norm1, v_w_in, v_g_q, v_g_k, v_rel_bias, v_w_o, v_g_norm2, v_w1, v_w2, v_w_ada, v_b_ada):
    _, s, d = x.shape
    n_layers = g_norm1.shape[0]
    n_pairs = (d // 2) // LANES
    n_ca_heads = rel_bias.shape[1]
    n_rel = rel_bias.shape[2]
    assert n_rel == 2 * REL_CLIP + 1 and g_q.shape[1] == HEAD_DIM and n_ca_heads == 2 * n_pairs
    ada_c = w_ada.shape[2]

    ax, ay, ac = _mesh_pos()
    chip = 2 * ax + ay
    dev = 4 * ax + 2 * ay + ac
    c_idx = jnp.reshape(ac, (1,)).astype(jnp.int32)

    c_act_all = _all_gather_small("ag_c", _silu_rows("silu_c", c)).reshape(N_DEV, 8, d)[:, 0, :]
    mod_parts = []
    for l in range(n_layers):
        bias = lax.dynamic_slice_in_dim(b_ada[l], chip * ada_c, ada_c).reshape(1, ada_c)
        mod_parts.append(_ada_mod(f"ada_mod{l}", c_act_all, w_ada[l], bias))
    mod_all = _all_gather_small("ag_mod", jnp.concatenate(mod_parts, axis=1))
    mod_all = mod_all.reshape(N_CHIPS, 2, N_DEV, n_layers, ada_c)[:, 0]
    mod_all = jnp.transpose(mod_all, (1, 2, 0, 3)).reshape(N_DEV, n_layers, N_CHIPS * ada_c)
    mod = lax.dynamic_index_in_dim(mod_all, dev, axis=0, keepdims=False)

    wg_in, wg_o, wg_1, wg_2 = _all_gather_weights(
        "ag_weights", [w_in.astype(BF16), w_o.astype(BF16), w1.astype(BF16), w2.astype(BF16)])

    xs = x[0]
    saved = []
    for l in range(n_layers):
        sh1, sc1, gt1, sh2, sc2, gt2 = [_row(mod[l, i * d:(i + 1) * d]) for i in range(6)]
        gq2 = _row(jnp.tile(g_q[l], 2))
        gk2 = _row(jnp.tile(g_k[l], 2))
        bias_t = _bias_tile(rel_bias[l])
        h1 = _norm_mod(f"norm1_{l}", xs, _row(g_norm1[l]), sc1, sh1)
        proj = _mm_nn_w(f"proj_{l}", h1, wg_in, l, True, [F32])[0]
        o_sb = _sb_fwd(f"sb_fwd_{l}", proj, n_pairs)
        qn, knp, vp = _qk_norm(f"qk_norm_{l}", proj, gq2, gk2, n_pairs)
        o_ca = _ca_fwd(f"ca_fwd_{l}", qn, knp, vp, bias_t, n_pairs)
        mixed = jnp.concatenate([o_sb, o_ca], axis=1)

        def res_epi(acc, res, gt):
            return res + gt * acc, acc

        def res_specs(tm, tn):
            return (pl.BlockSpec((tm, tn), lambda i, j, kk: (i, j)), pl.BlockSpec((1, tn), lambda i, j, kk: (0, j)))

        x1, m1 = _mm_nn_w(f"attn_out_{l}", mixed, wg_o, l, False, [F32, BF16], epi=res_epi,
                          extras=(xs, gt1), extra_specs_fn=res_specs)
        h2 = _norm_mod(f"norm2_{l}", x1, _row(g_norm2[l]), sc2, sh2)

        def act_epi(acc):
            r = jnp.maximum(acc, 0.0)
            return acc, r * r

        u, a = _mm_nn_w(f"mlp_up_{l}", h2, wg_1, l, True, [BF16, BF16], epi=act_epi)
        x2, m2 = _mm_nn_w(f"mlp_down_{l}", a, wg_2, l, False, [F32, BF16], epi=res_epi,
                          extras=(x1, gt2), extra_specs_fn=res_specs)
        saved.append(dict(x0=xs, h1=h1, proj=proj, o_sb=o_sb, qn=qn, knp=knp, vp=vp, bias_t=bias_t, mixed=mixed,
                          m1=m1, x1=x1, h2=h2, u=u, a=a, m2=m2, gq2=gq2, gk2=gk2,
                          sc1=sc1, gt1=gt1, sc2=sc2, gt2=gt2))
        xs = x2

    dx, sq = _loss_grad("loss", xs, loss_target[0])
    loss = lax.psum(0.5 * jnp.sum(sq) / d, ("x", "y", "c"))

    g_in, g_o, g_1, g_2 = [None] * n_layers, [None] * n_layers, [None] * n_layers, [None] * n_layers
    dmod, dg1, dg2, dgq, dgk, drel = [], [], [], [], [], []
    for l in reversed(range(n_layers)):
        sv = saved[l]
        dm2, dgt2 = _gate_bwd(f"gate2_bwd_{l}", dx, sv["m2"], sv["gt2"])

        def act_bwd_epi(acc, u_t):
            return (acc * (2.0 * jnp.maximum(u_t.astype(F32), 0.0)),)

        def tile_specs(tm, tn):
            return (pl.BlockSpec((tm, tn), lambda i, j, kk: (i, j)),)

        du = _mm_nt_w(f"mlp_down_bwd_{l}", dm2, wg_2, l, False, BF16, epi=act_bwd_epi, extras=(sv["u"],),
                      extra_specs_fn=tile_specs)
        g_2[l] = _mm_tn(f"w2_grad_{l}", sv["a"], dm2, False)
        dh2 = _mm_nt_w(f"mlp_up_bwd_{l}", du, wg_1, l, True, F32)
        g_1[l] = _mm_tn(f"w1_grad_{l}", sv["h2"], du, True)
        dx1, dsh2, dsc2, dgn2 = _norm_mod_bwd(f"norm2_bwd_{l}", dh2, sv["x1"], _row(g_norm2[l]), sv["sc2"], dx)
        dm1, dgt1 = _gate_bwd(f"gate1_bwd_{l}", dx1, sv["m1"], sv["gt1"])
        dmixed = _mm_nt_w(f"attn_out_bwd_{l}", dm1, wg_o, l, False, F32)
        g_o[l] = _mm_tn(f"wo_grad_{l}", sv["mixed"], dm1, False)
        dq_sb, dk_sb, dv_sb = _sb_bwd(f"sb_bwd_{l}", sv["proj"], sv["o_sb"], dmixed, n_pairs)
        dqn, dknp, dvp, dbias_t = _ca_bwd(f"ca_bwd_{l}", sv["qn"], sv["knp"], sv["vp"], sv["bias_t"], dmixed,
                                          n_pairs, n_pairs)
        dq_ca, dk_ca, dgq_l, dgk_l = _qk_norm_bwd(f"qk_norm_bwd_{l}", dqn, dknp, sv["proj"], sv["gq2"], sv["gk2"],
                                                  n_pairs)
        drel_l = _rel_bias_grad(f"rel_bias_grad_{l}", jnp.transpose(dbias_t, (1, 0, 2)))[:, :n_rel]
        dproj = jnp.concatenate([dq_sb.astype(BF16), dk_sb.astype(BF16), dv_sb.astype(BF16), dq_ca, dk_ca,
                                 dvp[PAD:].astype(BF16)], axis=1)
        g_in[l] = _mm_tn(f"win_grad_{l}", sv["h1"], dproj, True)
        dh1 = _mm_nt_w(f"proj_bwd_{l}", dproj, wg_in, l, True, F32)
        dx, dsh1, dsc1, dgn1 = _norm_mod_bwd(f"norm1_bwd_{l}", dh1, sv["x0"], _row(g_norm1[l]), sv["sc1"], dx1)
        dmod.insert(0, jnp.concatenate([dsh1, dsc1, dgt1, dsh2, dsc2, dgt2], axis=1)[0])
        dg1.insert(0, dgn1[0])
        dg2.insert(0, dgn2[0])
        dgq.insert(0, dgq_l[0, :HEAD_DIM])
        dgk.insert(0, dgk_l[0, :HEAD_DIM])
        drel.insert(0, drel_l.reshape(-1))
    grad_x = dx[None]

    small = jnp.concatenate([jnp.concatenate(dmod), jnp.concatenate(dg1), jnp.concatenate(dg2),
                             jnp.concatenate(dgq), jnp.concatenate(dgk), jnp.concatenate(drel)])
    n_small = small.shape[0]
    n_pack = -(-n_small // (8 * LANES)) * (8 * LANES)
    small = jnp.pad(small, (0, n_pack - n_small)).reshape(8, n_pack // 8)
    small_all = _all_gather_small("ag_small", small).reshape(N_DEV, 8, n_pack // 8)
    small_sum = _sum_rows8("sum_small", small_all).reshape(-1)
    dmod_all = small_all.reshape(N_DEV, n_pack)[:, :n_layers * 6 * d].reshape(N_DEV, n_layers, 6 * d)

    off = 0

    def take(n, shape):
        nonlocal off
        out = small_sum[off:off + n].reshape(shape)
        off += n
        return out

    grad_b_ada = take(n_layers * 6 * d, (n_layers, 6 * d))
    grad_g_norm1 = take(n_layers * d, (n_layers, d))
    grad_g_norm2 = take(n_layers * d, (n_layers, d))
    grad_g_q = take(n_layers * HEAD_DIM, (n_layers, HEAD_DIM))
    grad_g_k = take(n_layers * HEAD_DIM, (n_layers, HEAD_DIM))
    grad_rel_bias = take(n_layers * n_ca_heads * n_rel, (n_layers, n_ca_heads, n_rel))

    c_act_t = jnp.transpose(c_act_all)
    grad_w_ada = jnp.stack([
        _outer_sum(f"wada_grad_{l}", c_act_t,
                   lax.dynamic_slice_in_dim(dmod_all[:, l, :], chip * ada_c, ada_c, axis=1))
        for l in range(n_layers)])

    partial = [g for per_w in (g_in, g_o, g_1, g_2) for g in per_w]
    from_sibling = _exchange_sibling_halves("rs_sibling", partial)
    chip_sums = [_add_halves(f"rs_add_{i}", g, r, c_idx) for i, (g, r) in enumerate(zip(partial, from_sibling))]
    from_chips = _exchange_chips("rs_chips", chip_sums, n_layers)
    pos_idx = jnp.stack([chip, ac]).astype(jnp.int32)
    reduced = [_sum_chips(f"rs_sum_{w_i}", chip_sums[w_i * n_layers:(w_i + 1) * n_layers], from_chips[w_i], pos_idx)
               for w_i in range(4)]
    grad_w_in, grad_w_o, grad_w1, grad_w2 = _share_with_sibling("rs_share", reduced)

    def adam_big(name, w, g, m, v):
        shp = w.shape
        outs = _adamw(name, w.reshape(-1, shp[-1]), g.reshape(-1, shp[-1]), m.reshape(-1, shp[-1]),
                      v.reshape(-1, shp[-1]))
        return [o.reshape(shp) for o in outs]

    def pack(arrs):
        flat = jnp.concatenate([a.reshape(-1) for a in arrs])
        n = flat.shape[0]
        n_p = -(-n // (8 * LANES)) * (8 * LANES)
        return jnp.pad(flat, (0, n_p - n), constant_values=1.0).reshape(8, n_p // 8)

    small_w = [g_norm1, g_q, g_k, rel_bias, g_norm2, b_ada]
    small_g = [grad_g_norm1, grad_g_q, grad_g_k, grad_rel_bias, grad_g_norm2, grad_b_ada]
    small_m = [m_g_norm1, m_g_q, m_g_k, m_rel_bias, m_g_norm2, m_b_ada]
    small_v = [v_g_norm1, v_g_q, v_g_k, v_rel_bias, v_g_norm2, v_b_ada]
    packed = _adamw("adamw_small", pack(small_w), pack(small_g), pack(small_m), pack(small_v))

    def unpack(p):
        flat = p.reshape(-1)
        res, o = [], 0
        for a in small_w:
            res.append(flat[o:o + a.size].reshape(a.shape))
            o += a.size
        return res

    sd, sm, sv_ = unpack(packed[0]), unpack(packed[1]), unpack(packed[2])
    big = {
        "w_in": adam_big("adamw_w_in", w_in, grad_w_in, m_w_in, v_w_in),
        "w_o": adam_big("adamw_w_o", w_o, grad_w_o, m_w_o, v_w_o),
        "w1": adam_big("adamw_w1", w1, grad_w1, m_w1, v_w1),
        "w2": adam_big("adamw_w2", w2, grad_w2, m_w2, v_w2),
        "w_ada": adam_big("adamw_w_ada", w_ada, grad_w_ada, m_w_ada, v_w_ada),
    }

    def ordered(kind):
        sm_list = (sd, sm, sv_)[kind]
        return [sm_list[0], big["w_in"][kind], sm_list[1], sm_list[2], sm_list[3], big["w_o"][kind], sm_list[4],
                big["w1"][kind], big["w2"][kind], big["w_ada"][kind], sm_list[5]]

    grads = [grad_g_norm1, grad_w_in, grad_g_q, grad_g_k, grad_rel_bias, grad_w_o, grad_g_norm2, grad_w1, grad_w2,
             grad_w_ada, grad_b_ada]
    return (loss, grad_x, *grads, *ordered(0), *ordered(1), *ordered(2))
```

```python
import functools

import jax
import jax.numpy as jnp
from jax import lax
from jax.experimental import pallas as pl
from jax.experimental.pallas import tpu as pltpu

F32 = jnp.float32
BF16 = jnp.bfloat16
MESH = pl.DeviceIdType.MESH

EPS = 1e-6
NEG_INF = -1e30
HEAD_DIM = 64
CHUNK = 64
LEFT_CHUNKS = 8
PAD = LEFT_CHUNKS * CHUNK
BAND = PAD + CHUNK
REL_CLIP = 128
SCALE = HEAD_DIM ** -0.5
DEAD_LOG = -88.0
LANES = 128
N_CHIPS = 4
N_DEV = 8

ADAM_LR = 0.001
ADAM_B1 = 0.9
ADAM_B2 = 0.999
ADAM_EPS = 1e-08
ADAM_WD = 0.01
ADAM_STEP = 10

VMEM_LIMIT = 48 * 1024 * 1024
ROW_TILE = 256
MM_TILE = 1024
SB_TILE = 256
CA_TILE = 256

NT = (((1,), (1,)), ((), ()))
TN = (((0,), (0,)), ((), ()))
NN = (((1,), (0,)), ((), ()))


def _tile(n, pref):
    best = None
    t = LANES
    while t <= min(n, pref):
        if n % t == 0:
            best = t
        t += LANES
    return best if best is not None else n


def _params(*sem):
    return pltpu.CompilerParams(dimension_semantics=sem, vmem_limit_bytes=VMEM_LIMIT)


def _split_dot(x, t, passes):
    acc = None
    r = x
    for i in range(passes):
        hi = r.astype(BF16)
        d = jnp.dot(hi, t, preferred_element_type=F32)
        acc = d if acc is None else acc + d
        if i + 1 < passes:
            r = r - hi.astype(F32)
    return acc


def _matmul(name, a, b, *, dims, grid, a_spec, b_spec, out_shapes, out_specs, acc_shape,
            epi=None, extras=(), extra_specs=()):
    nk = grid[2]
    n_ex = len(extras)
    n_out = len(out_shapes)

    def body(*refs):
        a_ref, b_ref = refs[0], refs[1]
        ex = refs[2:2 + n_ex]
        outs = refs[2 + n_ex:2 + n_ex + n_out]
        acc = refs[-1]
        k = pl.program_id(2)

        @pl.when(k == 0)
        def _():
            acc[...] = jnp.zeros_like(acc)

        acc[...] += lax.dot_general(a_ref[...].astype(BF16), b_ref[...].astype(BF16), dims,
                                    preferred_element_type=F32)

        @pl.when(k == nk - 1)
        def _():
            res = epi(acc[...], *[e[...] for e in ex]) if epi is not None else (acc[...],)
            for o, r in zip(outs, res):
                o[...] = r.astype(o.dtype)

    return pl.pallas_call(
        body, grid=grid, in_specs=[a_spec, b_spec, *extra_specs], out_specs=out_specs,
        out_shape=out_shapes, scratch_shapes=[pltpu.VMEM(acc_shape, F32)],
        compiler_params=_params("parallel", "parallel", "arbitrary"), name=name,
    )(a, b, *extras)


def _mm_nn_w(name, a, wg, layer, colshard, out_dtypes, epi=None, extras=(), extra_specs_fn=None):
    m = a.shape[0]
    _, _, r, c = wg.shape
    tm = min(MM_TILE, m)
    if colshard:
        k, n = r, N_CHIPS * c
        tk, tn = k, _tile(c, MM_TILE)
        npc = c // tn
        b_spec = pl.BlockSpec((None, None, tk, tn), lambda i, j, kk: (layer, j // npc, kk, j % npc))
    else:
        k, n = N_CHIPS * r, c
        tk, tn = _tile(r, MM_TILE), _tile(c, MM_TILE)
        kpc = r // tk
        b_spec = pl.BlockSpec((None, None, tk, tn), lambda i, j, kk: (layer, kk // kpc, kk % kpc, j))
    assert a.shape[1] == k
    grid = (m // tm, n // tn, k // tk)
    out_spec = pl.BlockSpec((tm, tn), lambda i, j, kk: (i, j))
    ex_specs = extra_specs_fn(tm, tn) if extra_specs_fn is not None else ()
    return _matmul(name, a, wg, dims=NN, grid=grid,
                   a_spec=pl.BlockSpec((tm, tk), lambda i, j, kk: (i, kk)), b_spec=b_spec,
                   out_shapes=[jax.ShapeDtypeStruct((m, n), d) for d in out_dtypes],
                   out_specs=[out_spec for _ in out_dtypes], acc_shape=(tm, tn),
                   epi=epi, extras=extras, extra_specs=ex_specs)


def _mm_nt_w(name, a, wg, layer, colshard, out_dtype, epi=None, extras=(), extra_specs_fn=None):
    m = a.shape[0]
    _, _, r, c = wg.shape
    tm = min(MM_TILE, m)
    if colshard:
        kdim, n = N_CHIPS * c, r
        tk, tn = _tile(c, MM_TILE), _tile(r, MM_TILE)
        kpc = c // tk
        b_spec = pl.BlockSpec((None, None, tn, tk), lambda i, j, kk: (layer, kk // kpc, j, kk % kpc))
    else:
        kdim, n = c, N_CHIPS * r
        tk, tn = c, _tile(r, MM_TILE)
        npc = r // tn
        b_spec = pl.BlockSpec((None, None, tn, tk), lambda i, j, kk: (layer, j // npc, j % npc, kk))
    assert a.shape[1] == kdim
    grid = (m // tm, n // tn, kdim // tk)
    ex_specs = extra_specs_fn(tm, tn) if extra_specs_fn is not None else ()
    return _matmul(name, a, wg, dims=NT, grid=grid,
                   a_spec=pl.BlockSpec((tm, tk), lambda i, j, kk: (i, kk)), b_spec=b_spec,
                   out_shapes=[jax.ShapeDtypeStruct((m, n), out_dtype)],
                   out_specs=[pl.BlockSpec((tm, tn), lambda i, j, kk: (i, j))], acc_shape=(tm, tn),
                   epi=epi, extras=extras, extra_specs=ex_specs)[0]


def _mm_tn(name, a, b, colshard):
    s, m = a.shape
    n = b.shape[1]
    tk = min(2 * MM_TILE, s)
    if colshard:
        c = n // N_CHIPS
        tm, tn = _tile(m, MM_TILE // 2), _tile(c, MM_TILE)
        npc = c // tn
        out_shape = (N_CHIPS, m, c)
        out_spec = pl.BlockSpec((None, tm, tn), lambda i, j, kk: (j // npc, i, j % npc))
    else:
        r = m // N_CHIPS
        tm, tn = _tile(r, MM_TILE // 2), _tile(n, MM_TILE)
        mpc = r // tm
        out_shape = (N_CHIPS, r, n)
        out_spec = pl.BlockSpec((None, tm, tn), lambda i, j, kk: (i // mpc, i % mpc, j))
    grid = (m // tm, n // tn, s // tk)
    return _matmul(name, a, b, dims=TN, grid=grid,
                   a_spec=pl.BlockSpec((tk, tm), lambda i, j, kk: (kk, i)),
                   b_spec=pl.BlockSpec((tk, tn), lambda i, j, kk: (kk, j)),
                   out_shapes=[jax.ShapeDtypeStruct(out_shape, F32)], out_specs=[out_spec],
                   acc_shape=(tm, tn))[0]


def _row_spec(ts, d):
    return pl.BlockSpec((ts, d), lambda i: (i, 0))


def _vec_spec(d):
    return pl.BlockSpec((1, d), lambda i: (0, 0))


def _norm_mod(name, x, g, sc, sh):
    s, d = x.shape
    ts = min(ROW_TILE, s)

    def body(x_ref, g_ref, sc_ref, sh_ref, h_ref):
        xv = x_ref[...]
        r = lax.rsqrt(jnp.mean(xv * xv, axis=-1, keepdims=True) + EPS)
        h_ref[...] = (xv * r * g_ref[...] * (1.0 + sc_ref[...]) + sh_ref[...]).astype(BF16)

    return pl.pallas_call(
        body, grid=(s // ts,), in_specs=[_row_spec(ts, d), _vec_spec(d), _vec_spec(d), _vec_spec(d)],
        out_specs=_row_spec(ts, d), out_shape=jax.ShapeDtypeStruct((s, d), BF16),
        compiler_params=_params("parallel"), name=name)(x, g, sc, sh)


def _norm_mod_bwd(name, dh, x, g, sc, dres):
    s, d = x.shape
    ts = min(ROW_TILE, s)

    def body(dh_ref, x_ref, g_ref, sc_ref, dres_ref, dx_ref, dsh_ref, dsc_ref, dg_ref):
        @pl.when(pl.program_id(0) == 0)
        def _():
            dsh_ref[...] = jnp.zeros_like(dsh_ref)
            dsc_ref[...] = jnp.zeros_like(dsc_ref)
            dg_ref[...] = jnp.zeros_like(dg_ref)

        xv = x_ref[...]
        r = lax.rsqrt(jnp.mean(xv * xv, axis=-1, keepdims=True) + EPS)
        xhat = xv * r
        dhv = dh_ref[...]
        gv = g_ref[...]
        opsc = 1.0 + sc_ref[...]
        dxhat = dhv * (gv * opsc)
        mdot = jnp.mean(dxhat * xhat, axis=-1, keepdims=True)
        dx_ref[...] = dres_ref[...] + r * (dxhat - xhat * mdot)
        dhx = dhv * xhat
        dsh_ref[...] += jnp.sum(dhv, axis=0, keepdims=True)
        dsc_ref[...] += jnp.sum(dhx * gv, axis=0, keepdims=True)
        dg_ref[...] += jnp.sum(dhx * opsc, axis=0, keepdims=True)

    vec = jax.ShapeDtypeStruct((1, d), F32)
    return pl.pallas_call(
        body, grid=(s // ts,),
        in_specs=[_row_spec(ts, d), _row_spec(ts, d), _vec_spec(d), _vec_spec(d), _row_spec(ts, d)],
        out_specs=[_row_spec(ts, d), _vec_spec(d), _vec_spec(d), _vec_spec(d)],
        out_shape=[jax.ShapeDtypeStruct((s, d), F32), vec, vec, vec],
        compiler_params=_params("arbitrary"), name=name)(dh, x, g, sc, dres)


def _gate_bwd(name, dx, m, gt):
    s, d = dx.shape
    ts = min(ROW_TILE, s)

    def body(dx_ref, m_ref, gt_ref, dm_ref, dgt_ref):
        @pl.when(pl.program_id(0) == 0)
        def _():
            dgt_ref[...] = jnp.zeros_like(dgt_ref)

        dxv = dx_ref[...]
        dm_ref[...] = (dxv * gt_ref[...]).astype(BF16)
        dgt_ref[...] += jnp.sum(dxv * m_ref[...].astype(F32), axis=0, keepdims=True)

    return pl.pallas_call(
        body, grid=(s // ts,), in_specs=[_row_spec(ts, d), _row_spec(ts, d), _vec_spec(d)],
        out_specs=[_row_spec(ts, d), _vec_spec(d)],
        out_shape=[jax.ShapeDtypeStruct((s, d), BF16), jax.ShapeDtypeStruct((1, d), F32)],
        compiler_params=_params("arbitrary"), name=name)(dx, m, gt)


def _loss_grad(name, y, target):
    s, d = y.shape
    ts = min(ROW_TILE, s)

    def body(y_ref, t_ref, dy_ref, sq_ref):
        @pl.when(pl.program_id(0) == 0)
        def _():
            sq_ref[...] = jnp.zeros_like(sq_ref)

        e = y_ref[...] - t_ref[...]
        dy_ref[...] = e * (1.0 / d)
        sq_ref[...] += jnp.sum(e * e, axis=0, keepdims=True)

    return pl.pallas_call(
        body, grid=(s // ts,), in_specs=[_row_spec(ts, d), _row_spec(ts, d)],
        out_specs=[_row_spec(ts, d), _vec_spec(d)],
        out_shape=[jax.ShapeDtypeStruct((s, d), F32), jax.ShapeDtypeStruct((1, d), F32)],
        compiler_params=_params("arbitrary"), name=name)(y, target)


def _adamw(name, w, g, m, v):
    r, c = w.shape
    tr = r
    for cand in (512, 256, 128, 64, 32, 16, 8):
        if r % cand == 0 and cand * c * 4 <= 2 * 1024 * 1024:
            tr = cand
            break

    def body(w_ref, g_ref, m_ref, v_ref, d_ref, mo_ref, vo_ref):
        gv = g_ref[...]
        m2 = ADAM_B1 * m_ref[...] + (1.0 - ADAM_B1) * gv
        v2 = ADAM_B2 * v_ref[...] + (1.0 - ADAM_B2) * (gv * gv)
        m_hat = m2 / (1.0 - ADAM_B1 ** ADAM_STEP)
        v_hat = v2 / (1.0 - ADAM_B2 ** ADAM_STEP)
        d_ref[...] = -ADAM_LR * (m_hat / (jnp.sqrt(v_hat) + ADAM_EPS) + ADAM_WD * w_ref[...])
        mo_ref[...] = m2
        vo_ref[...] = v2

    spec = pl.BlockSpec((tr, c), lambda i: (i, 0))
    shp = jax.ShapeDtypeStruct((r, c), F32)
    return pl.pallas_call(
        body, grid=(r // tr,), in_specs=[spec] * 4, out_specs=[spec] * 3, out_shape=[shp] * 3,
        compiler_params=_params("parallel"), name=name)(w, g, m, v)


def _sum_rows8(name, x):
    n, r, c = x.shape

    def body(x_ref, o_ref):
        acc = x_ref[0]
        for i in range(1, n):
            acc = acc + x_ref[i]
        o_ref[...] = acc

    return pl.pallas_call(
        body, out_shape=jax.ShapeDtypeStruct((r, c), F32),
        in_specs=[pl.BlockSpec(memory_space=pltpu.VMEM)], out_specs=pl.BlockSpec(memory_space=pltpu.VMEM),
        name=name)(x)


def _outer_sum(name, ct, dm):
    d, n_seq = ct.shape
    n = dm.shape[1]
    tr = min(256, d)

    def body(ct_ref, dm_ref, o_ref):
        acc = ct_ref[:, 0:1] * dm_ref[0:1, :]
        for s_i in range(1, n_seq):
            acc = acc + ct_ref[:, s_i:s_i + 1] * dm_ref[s_i:s_i + 1, :]
        o_ref[...] = acc

    return pl.pallas_call(
        body, grid=(d // tr,),
        in_specs=[pl.BlockSpec((tr, n_seq), lambda i: (i, 0)), pl.BlockSpec((n_seq, n), lambda i: (0, 0))],
        out_specs=pl.BlockSpec((tr, n), lambda i: (i, 0)), out_shape=jax.ShapeDtypeStruct((d, n), F32),
        compiler_params=_params("parallel"), name=name)(ct, dm)


def _ada_mod(name, c_act, w_ada_l, bias):
    n_seq, d = c_act.shape
    n = w_ada_l.shape[1]
    tn = _tile(n, 768)

    def epi(acc, b):
        return (acc + b,)

    return _matmul(name, c_act, w_ada_l, dims=NN, grid=(1, n // tn, 1),
                   a_spec=pl.BlockSpec((n_seq, d), lambda i, j, kk: (0, 0)),
                   b_spec=pl.BlockSpec((d, tn), lambda i, j, kk: (0, j)),
                   out_shapes=[jax.ShapeDtypeStruct((n_seq, n), F32)],
                   out_specs=[pl.BlockSpec((n_seq, tn), lambda i, j, kk: (0, j))], acc_shape=(n_seq, tn),
                   epi=epi, extras=(bias,), extra_specs=(pl.BlockSpec((1, tn), lambda i, j, kk: (0, j)),))[0]


def _silu_rows(name, c_row):
    d = c_row.shape[1]

    def body(c_ref, o_ref):
        cv = c_ref[...]
        o_ref[...] = jnp.broadcast_to(cv / (1.0 + jnp.exp(-cv)), (8, d))

    return pl.pallas_call(
        body, out_shape=jax.ShapeDtypeStruct((8, d), F32),
        in_specs=[pl.BlockSpec(memory_space=pltpu.VMEM)], out_specs=pl.BlockSpec(memory_space=pltpu.VMEM),
        name=name)(c_row)


def _sb_masks(t):
    row = lax.broadcasted_iota(jnp.int32, (t, t), 0)
    col = lax.broadcasted_iota(jnp.int32, (t, t), 1)
    strict = col < row
    tri = (row > col).astype(BF16)
    return strict, tri


def _col_minus_row(t):
    return lax.broadcasted_iota(jnp.int32, (t, t), 1) - lax.broadcasted_iota(jnp.int32, (t, t), 0)


def _sweep_left(step, jd, carry, runs_of):
    def alive(c):
        runs = runs_of(c)
        top = runs[0]
        for r in runs[1:]:
            top = jnp.maximum(top, r)
        return jnp.max(top)

    def cond(state):
        j, _, top = state
        return jnp.logical_and(j >= 0, top > DEAD_LOG)

    def body(state):
        j, c, _ = state
        c = step(j, c, False)
        return j - 1, c, alive(c)

    return lax.while_loop(cond, body, (jd - 1, carry, alive(carry)))[1]


def _log_sigmoids(z):
    sp = jnp.log(1.0 + jnp.exp(-jnp.abs(z)))
    return jnp.minimum(z, 0.0) - sp, jnp.minimum(-z, 0.0) - sp, sp


def _sb_fwd(name, proj, n_pairs):
    s = proj.shape[0]
    t = min(SB_TILE, s)
    nq = s // t

    heads = [slice(h * HEAD_DIM, (h + 1) * HEAD_DIM) for h in range(LANES // HEAD_DIM)]

    assert nq % 2 == 0

    def body(q_ref, k_ref, v_ref, o_ref):
        qi = pl.program_id(1)
        _, tri = _sb_masks(t)
        diff = _col_minus_row(t)
        qs = [(q_ref[:, sl] * SCALE).astype(BF16) for sl in heads]
        jd = qi // 2
        chains = [(sb, h) for sb in (1, 0) for h in range(len(heads))]

        def step(j, carry, masked):
            ks = [pl.multiple_of((2 * j + sb) * t, t) for sb, _ in chains]
            k_ = [k_ref[pl.ds(ks[i], t), heads[h]].astype(BF16) for i, (_, h) in enumerate(chains)]
            v_ = [v_ref[pl.ds(ks[i], t), heads[h]].astype(BF16) for i, (_, h) in enumerate(chains)]
            zs = [lax.dot_general(qs[h], k_[i], NT, preferred_element_type=F32) for i, (_, h) in enumerate(chains)]
            lbs, lms, keeps = [], [], []
            for (sb, _), z in zip(chains, zs):
                lb, lm, _ = _log_sigmoids(z)
                keep = None
                if masked:
                    keep = diff < (qi - 2 * j - sb) * t
                    lm = jnp.where(keep, lm, 0.0)
                lbs.append(lb)
                lms.append(lm)
                keeps.append(keep)
            css = [_split_dot(lm, tri, 2) for lm in lms]
            sums = [jnp.sum(lm, axis=1, keepdims=True) for lm in lms]
            ws = []
            for i, (sb, h) in enumerate(chains):
                run = carry[h][1] if sb == 1 else carry[h][1] + sums[h]
                w = jnp.exp(lbs[i] + css[i] + run)
                if masked:
                    w = jnp.where(keeps[i], w, 0.0)
                ws.append(w.astype(BF16))
            pvs = [jnp.dot(w, v, preferred_element_type=F32) for w, v in zip(ws, v_)]
            nh = len(heads)
            return tuple((carry[h][0] + pvs[h] + pvs[nh + h], carry[h][1] + sums[h] + sums[nh + h]) for h in range(nh))

        init = tuple((jnp.zeros((t, HEAD_DIM), F32), jnp.zeros((t, 1), F32)) for _ in heads)
        carry = step(jd, init, True)
        carry = _sweep_left(step, jd, carry, lambda c: [run for _, run in c])
        for (acc, _), sl in zip(carry, heads):
            o_ref[:, sl] = acc

    return pl.pallas_call(
        body, grid=(n_pairs, nq),
        in_specs=[pl.BlockSpec((t, LANES), lambda p, i: (i, p)),
                  pl.BlockSpec((s, LANES), lambda p, i: (0, n_pairs + p)),
                  pl.BlockSpec((s, LANES), lambda p, i: (0, 2 * n_pairs + p))],
        out_specs=pl.BlockSpec((t, LANES), lambda p, i: (i, p)),
        out_shape=jax.ShapeDtypeStruct((s, n_pairs * LANES), F32),
        compiler_params=_params("parallel", "arbitrary"), name=name)(proj, proj, proj)


def _sb_bwd(name, proj, o_sb, dmixed, n_pairs):
    s = proj.shape[0]
    t = min(SB_TILE, s)
    nq = s // t
    heads = [slice(h * HEAD_DIM, (h + 1) * HEAD_DIM) for h in range(LANES // HEAD_DIM)]

    def body(q_ref, k_ref, v_ref, o_ref, do_ref, dq_ref, dk_ref, dv_ref):
        qi = pl.program_id(1)

        @pl.when(qi == 0)
        def _():
            dk_ref[...] = jnp.zeros_like(dk_ref)
            dv_ref[...] = jnp.zeros_like(dv_ref)

        _, tri = _sb_masks(t)
        diff = _col_minus_row(t)
        qs = [(q_ref[:, sl] * SCALE).astype(BF16) for sl in heads]
        dos = [do_ref[:, sl].astype(BF16) for sl in heads]
        dsums = [jnp.sum(do.astype(F32) * o_ref[:, sl], axis=1, keepdims=True) for do, sl in zip(dos, heads)]
        jd = qi // 2
        nh = len(heads)
        chains = [(sb, h) for sb in (1, 0) for h in range(nh)]

        def step(j, carry, masked):
            ks = [pl.multiple_of((2 * j + sb) * t, t) for sb, _ in chains]
            k_ = [k_ref[pl.ds(ks[i], t), heads[h]].astype(BF16) for i, (_, h) in enumerate(chains)]
            v_ = [v_ref[pl.ds(ks[i], t), heads[h]].astype(BF16) for i, (_, h) in enumerate(chains)]
            zs = [lax.dot_general(qs[h], k_[i], NT, preferred_element_type=F32) for i, (_, h) in enumerate(chains)]
            dws = [lax.dot_general(dos[h], v_[i], NT, preferred_element_type=F32) for i, (_, h) in enumerate(chains)]
            lbs, lms, betas, ombs, keeps = [], [], [], [], []
            for (sb, _), z in zip(chains, zs):
                ez = jnp.exp(-jnp.abs(z))
                lb = jnp.minimum(z, 0.0) - jnp.log(1.0 + ez)
                lm = lb - z
                betas.append(jnp.exp(lb))
                ombs.append(jnp.exp(lm))
                keep = None
                if masked:
                    keep = diff < (qi - 2 * j - sb) * t
                    lm = jnp.where(keep, lm, 0.0)
                lbs.append(lb)
                lms.append(lm)
                keeps.append(keep)
            css = [_split_dot(lm, tri, 2) for lm in lms]
            sums_f = [jnp.sum(lm, axis=1, keepdims=True) for lm in lms]
            wbs, es = [], []
            for i, (sb, h) in enumerate(chains):
                run_f = carry[h][1] if sb == 1 else carry[h][1] + sums_f[h]
                w = jnp.exp(lbs[i] + css[i] + run_f)
                if masked:
                    w = jnp.where(keeps[i], w, 0.0)
                wb = w.astype(BF16)
                wbs.append(wb)
                es.append(wb.astype(F32) * dws[i])
            for i, (_, h) in enumerate(chains):
                dv_ref[pl.ds(ks[i], t), heads[h]] += lax.dot_general(wbs[i], dos[h], TN, preferred_element_type=F32)
            ecs = [_split_dot(e, tri, 2) for e in es]
            sums_e = [jnp.sum(e, axis=1, keepdims=True) for e in es]
            dzbs = []
            for i, (sb, h) in enumerate(chains):
                run_e = carry[h][2] if sb == 1 else carry[h][2] + sums_e[h]
                left = dsums[h] - (ecs[i] + es[i] + run_e)
                dz = es[i] * ombs[i] - left * betas[i]
                if masked:
                    dz = jnp.where(keeps[i], dz, 0.0)
                dzbs.append(dz.astype(BF16))
            dqs = []
            for i, (_, h) in enumerate(chains):
                dk_ref[pl.ds(ks[i], t), heads[h]] += lax.dot_general(dzbs[i], qs[h], TN, preferred_element_type=F32)
                dqs.append(jnp.dot(dzbs[i], k_[i], preferred_element_type=F32))
            return tuple((carry[h][0] + dqs[h] + dqs[nh + h], carry[h][1] + sums_f[h] + sums_f[nh + h],
                          carry[h][2] + sums_e[h] + sums_e[nh + h]) for h in range(nh))

        zero = jnp.zeros((t, 1), F32)
        init = tuple((jnp.zeros((t, HEAD_DIM), F32), zero, zero) for _ in heads)
        carry = step(jd, init, True)
        carry = _sweep_left(step, jd, carry, lambda c: [run_f for _, run_f, _ in c])
        for (dq, _, _), sl in zip(carry, heads):
            dq_ref[:, sl] = dq * SCALE

    blk = pl.BlockSpec((t, LANES), lambda p, i: (i, p))
    full = pl.BlockSpec((s, LANES), lambda p, i: (0, p))
    shp = jax.ShapeDtypeStruct((s, n_pairs * LANES), F32)
    return pl.pallas_call(
        body, grid=(n_pairs, nq),
        in_specs=[blk,
                  pl.BlockSpec((s, LANES), lambda p, i: (0, n_pairs + p)),
                  pl.BlockSpec((s, LANES), lambda p, i: (0, 2 * n_pairs + p)),
                  blk, blk],
        out_specs=[blk, full, full], out_shape=[shp, shp, shp],
        compiler_params=_params("parallel", "arbitrary"), name=name)(proj, proj, proj, o_sb, dmixed)


def _group_mean_matrix():
    row = lax.broadcasted_iota(jnp.int32, (LANES, LANES), 0)
    col = lax.broadcasted_iota(jnp.int32, (LANES, LANES), 1)
    same_head = (row < HEAD_DIM) == (col < HEAD_DIM)
    return jnp.where(same_head, 1.0 / HEAD_DIM, 0.0).astype(BF16)


def _qk_norm(name, proj, gq2, gk2, n_pairs):
    s = proj.shape[0]
    ts = min(ROW_TILE, s)
    pb = PAD // ts
    c0 = 3 * n_pairs

    def body(q_ref, k_ref, v_ref, gq_ref, gk_ref, qn_ref, kn_ref, vp_ref):
        r_i = pl.program_id(1)

        @pl.when(r_i < pb)
        def _():
            qn_ref[...] = jnp.zeros_like(qn_ref)
            kn_ref[...] = jnp.zeros_like(kn_ref)
            vp_ref[...] = jnp.zeros_like(vp_ref)

        @pl.when(r_i >= pb)
        def _():
            gm = _group_mean_matrix()
            for src, g_ref, dst in ((q_ref, gq_ref, qn_ref), (k_ref, gk_ref, kn_ref)):
                xv = src[...]
                r = lax.rsqrt(_split_dot(xv * xv, gm, 3) + EPS)
                dst[...] = (xv * r * g_ref[...]).astype(BF16)
            vp_ref[...] = v_ref[...].astype(BF16)

    def in_spec(off):
        return pl.BlockSpec((ts, LANES), lambda p, i: (jnp.maximum(i - pb, 0), c0 + off * n_pairs + p))

    gspec = pl.BlockSpec((1, LANES), lambda p, i: (0, 0))
    ospec = pl.BlockSpec((ts, LANES), lambda p, i: (i, p))
    shp = jax.ShapeDtypeStruct((s + PAD, n_pairs * LANES), BF16)
    return pl.pallas_call(
        body, grid=(n_pairs, (s + PAD) // ts),
        in_specs=[in_spec(0), in_spec(1), in_spec(2), gspec, gspec],
        out_specs=[ospec, ospec, ospec], out_shape=[shp, shp, shp],
        compiler_params=_params("parallel", "arbitrary"), name=name)(proj, proj, proj, gq2, gk2)


def _qk_norm_bwd(name, dqn, dkn, proj, gq2, gk2, n_pairs):
    s = proj.shape[0]
    ts = min(ROW_TILE, s)
    pb = PAD // ts
    c0 = 3 * n_pairs
    n_r = s // ts

    def body(dqn_ref, dkn_ref, q_ref, k_ref, gq_ref, gk_ref, dq_ref, dk_ref, dgq_ref, dgk_ref):
        first = jnp.logical_and(pl.program_id(0) == 0, pl.program_id(1) == 0)
        last = jnp.logical_and(pl.program_id(0) == n_pairs - 1, pl.program_id(1) == n_r - 1)

        @pl.when(first)
        def _():
            dgq_ref[...] = jnp.zeros_like(dgq_ref)
            dgk_ref[...] = jnp.zeros_like(dgk_ref)

        gm = _group_mean_matrix()
        for dn_ref, x_ref, g_ref, dx_ref, dg_ref in ((dqn_ref, q_ref, gq_ref, dq_ref, dgq_ref),
                                                     (dkn_ref, k_ref, gk_ref, dk_ref, dgk_ref)):
            xv = x_ref[...]
            r = lax.rsqrt(_split_dot(xv * xv, gm, 3) + EPS)
            xhat = xv * r
            dn = dn_ref[...]
            tg = dn * g_ref[...]
            md = _split_dot(tg * xhat, gm, 3)
            dx_ref[...] = (r * (tg - xhat * md)).astype(BF16)
            dg_ref[...] += jnp.sum(dn * xhat, axis=0, keepdims=True)

        @pl.when(last)
        def _():
            for dg_ref in (dgq_ref, dgk_ref):
                gv = dg_ref[...]
                dg_ref[...] = gv + pltpu.roll(gv, HEAD_DIM, axis=1)

    def x_spec(off):
        return pl.BlockSpec((ts, LANES), lambda p, i: (i, c0 + off * n_pairs + p))

    dn_q = pl.BlockSpec((ts, LANES), lambda p, i: (i, p))
    dn_k = pl.BlockSpec((ts, LANES), lambda p, i: (i + pb, p))
    gspec = pl.BlockSpec((1, LANES), lambda p, i: (0, 0))
    ospec = pl.BlockSpec((ts, LANES), lambda p, i: (i, p))
    shp = jax.ShapeDtypeStruct((s, n_pairs * LANES), BF16)
    vec = jax.ShapeDtypeStruct((1, LANES), F32)
    return pl.pallas_call(
        body, grid=(n_pairs, n_r),
        in_specs=[dn_q, dn_k, x_spec(0), x_spec(1), gspec, gspec],
        out_specs=[ospec, ospec, gspec, gspec], out_shape=[shp, shp, vec, vec],
        compiler_params=_params("arbitrary", "arbitrary"), name=name)(dqn, dkn, proj, proj, gq2, gk2)


def _head_blocks(x2):
    lane = lax.broadcasted_iota(jnp.int32, x2.shape, 1)
    zero = jnp.zeros_like(x2)
    return jnp.concatenate([jnp.where(lane < HEAD_DIM, x2, zero), jnp.where(lane >= HEAD_DIM, x2, zero)], axis=0)


def _head_diag(r):
    lane = lax.broadcasted_iota(jnp.int32, (CHUNK, LANES), 1)
    return jnp.where(lane < HEAD_DIM, r[:CHUNK], r[CHUNK:])


def _ca_probs(kw, qb, bias_t, n):
    sc = lax.dot_general(kw, qb, NT, preferred_element_type=F32) * SCALE + bias_t
    jpos = lax.broadcasted_iota(jnp.int32, (BAND, LANES), 0)
    sc = jnp.where(n * CHUNK + jpos >= PAD, sc, NEG_INF)
    p = jnp.exp(sc - jnp.max(sc, axis=0, keepdims=True))
    return p * (1.0 / jnp.sum(p, axis=0, keepdims=True))


def _ca_fwd(name, qn, knp, vp, bias_tt, n_pairs):
    sp_rows = knp.shape[0]
    s = sp_rows - PAD
    t = min(CA_TILE, s)
    cpb = t // CHUNK
    pb = PAD // t

    def body(q_ref, k_ref, v_ref, b_ref, o_ref):
        qi = pl.program_id(1)
        wss = [pl.multiple_of((qi * cpb + ci) * CHUNK, CHUNK) for ci in range(cpb)]
        ps = [_ca_probs(k_ref[pl.ds(wss[ci], BAND), :], _head_blocks(q_ref[ci * CHUNK:(ci + 1) * CHUNK, :]),
                        b_ref[...], qi * cpb + ci) for ci in range(cpb)]
        for ci in range(cpb):
            r = lax.dot_general(ps[ci].astype(BF16), v_ref[pl.ds(wss[ci], BAND), :], TN, preferred_element_type=F32)
            o_ref[ci * CHUNK:(ci + 1) * CHUNK, :] = _head_diag(r)

    full = pl.BlockSpec((sp_rows, LANES), lambda p, i: (0, p))
    return pl.pallas_call(
        body, grid=(n_pairs, s // t),
        in_specs=[pl.BlockSpec((t, LANES), lambda p, i: (i + pb, p)), full, full,
                  pl.BlockSpec((None, BAND, LANES), lambda p, i: (p, 0, 0))],
        out_specs=pl.BlockSpec((t, LANES), lambda p, i: (i, p)),
        out_shape=jax.ShapeDtypeStruct((s, n_pairs * LANES), F32),
        compiler_params=_params("parallel", "arbitrary"), name=name)(qn, knp, vp, bias_tt)


def _ca_bwd(name, qn, knp, vp, bias_tt, dmixed, n_pairs, col0):
    sp_rows = knp.shape[0]
    s = sp_rows - PAD
    t = min(CA_TILE, s)
    cpb = t // CHUNK
    pb = PAD // t

    def body(q_ref, k_ref, v_ref, b_ref, do_ref, dq_ref, dk_ref, dv_ref, db_ref):
        qi = pl.program_id(1)

        @pl.when(qi == 0)
        def _():
            dk_ref[...] = jnp.zeros_like(dk_ref)
            dv_ref[...] = jnp.zeros_like(dv_ref)
            db_ref[...] = jnp.zeros_like(db_ref)

        wss = [pl.multiple_of((qi * cpb + ci) * CHUNK, CHUNK) for ci in range(cpb)]
        kws = [k_ref[pl.ds(ws, BAND), :] for ws in wss]
        qbs = [_head_blocks(q_ref[ci * CHUNK:(ci + 1) * CHUNK, :]) for ci in range(cpb)]
        dbs = [_head_blocks(do_ref[ci * CHUNK:(ci + 1) * CHUNK, :].astype(BF16)) for ci in range(cpb)]
        ps = [_ca_probs(kws[ci], qbs[ci], b_ref[...], qi * cpb + ci) for ci in range(cpb)]
        dps = [lax.dot_general(v_ref[pl.ds(wss[ci], BAND), :], dbs[ci], NT, preferred_element_type=F32)
               for ci in range(cpb)]
        dss = [p * (dp - jnp.sum(p * dp, axis=0, keepdims=True)) for p, dp in zip(ps, dps)]
        for ci in range(cpb):
            dsb = dss[ci].astype(BF16)
            dv_ref[pl.ds(wss[ci], BAND), :] += jnp.dot(ps[ci].astype(BF16), dbs[ci], preferred_element_type=F32)
            dk_ref[pl.ds(wss[ci], BAND), :] += jnp.dot(dsb, qbs[ci], preferred_element_type=F32) * SCALE
            r = lax.dot_general(dsb, kws[ci], TN, preferred_element_type=F32)
            dq_ref[ci * CHUNK:(ci + 1) * CHUNK, :] = _head_diag(r) * SCALE
        total = dss[0]
        for ci in range(1, cpb):
            total = total + dss[ci]
        db_ref[...] += total

    full_in = pl.BlockSpec((sp_rows, LANES), lambda p, i: (0, p))
    btile = pl.BlockSpec((None, BAND, LANES), lambda p, i: (p, 0, 0))
    pad_shape = jax.ShapeDtypeStruct((sp_rows, n_pairs * LANES), F32)
    return pl.pallas_call(
        body, grid=(n_pairs, s // t),
        in_specs=[pl.BlockSpec((t, LANES), lambda p, i: (i + pb, p)), full_in, full_in, btile,
                  pl.BlockSpec((t, LANES), lambda p, i: (i, col0 + p))],
        out_specs=[pl.BlockSpec((t, LANES), lambda p, i: (i, p)), full_in, full_in, btile],
        out_shape=[jax.ShapeDtypeStruct((s, n_pairs * LANES), F32), pad_shape, pad_shape,
                   jax.ShapeDtypeStruct(bias_tt.shape, F32)],
        compiler_params=_params("parallel", "arbitrary"), name=name)(qn, knp, vp, bias_tt, dmixed)


def _pair_transposed(tile):
    n_h = tile.shape[0]
    return jnp.transpose(tile.reshape(n_h // 2, 2, CHUNK, BAND), (0, 3, 1, 2)).reshape(n_h // 2, BAND, LANES)


def _query_major(tile_t):
    n_p = tile_t.shape[0]
    return jnp.transpose(tile_t.reshape(n_p, BAND, 2, CHUNK), (3, 0, 2, 1)).reshape(CHUNK, 2 * n_p, BAND)


def _bias_tile(rb):
    n_h = rb.shape[0]
    far = jnp.broadcast_to(rb[:, 2 * REL_CLIP:], (n_h, PAD + CHUNK - REL_CLIP))
    near = rb[:, REL_CLIP - (CHUNK - 1):2 * REL_CLIP][:, ::-1]
    ext = jnp.concatenate([far, near], axis=1)
    return jnp.stack([ext[:, CHUNK - 1 - i:CHUNK - 1 - i + BAND] for i in range(CHUNK)], axis=1)


def _rel_bias_grad(name, db_t):
    _, n_h, _ = db_t.shape
    n_out = 3 * LANES
    assert CHUNK == 64 and REL_CLIP == 128 and LEFT_CHUNKS == 8

    def body(z_ref, o_ref):
        a_idx = lax.broadcasted_iota(jnp.int32, (LANES, n_out), 0)
        r_idx = lax.broadcasted_iota(jnp.int32, (LANES, n_out), 1)
        out = jnp.zeros((n_h, n_out), F32)
        for mm, base in ((0, 191), (1, 255), (2, 319)):
            b = LEFT_CHUNKS - mm
            acc = jnp.zeros((n_h, LANES), F32)
            for i in range(CHUNK):
                row = z_ref[i, :, b * CHUNK:(b + 1) * CHUNK]
                rowp = jnp.concatenate([row, jnp.zeros((n_h, LANES - CHUNK), F32)], axis=1)
                acc = acc + pltpu.roll(rowp, CHUNK - 1 - i, axis=1)
            place = (r_idx == jnp.minimum(base - a_idx, 2 * REL_CLIP)).astype(BF16)
            out = out + _split_dot(acc, place, 3)
        far = jnp.zeros((n_h, 1), F32)
        for i in range(CHUNK):
            far = far + jnp.sum(z_ref[i, :, 0:(LEFT_CHUNKS - 2) * CHUNK], axis=1, keepdims=True)
        lane = lax.broadcasted_iota(jnp.int32, (n_h, n_out), 1)
        o_ref[...] = out + jnp.where(lane == 2 * REL_CLIP, far, 0.0)

    return pl.pallas_call(
        body, out_shape=jax.ShapeDtypeStruct((n_h, n_out), F32),
        in_specs=[pl.BlockSpec(memory_space=pltpu.VMEM)], out_specs=pl.BlockSpec(memory_space=pltpu.VMEM),
        name=name)(db_t)


def _mesh_pos():
    return lax.axis_index("x"), lax.axis_index("y"), lax.axis_index("c")


def _other_chips(x, y):
    return [(1 - x, y), (x, 1 - y), (1 - x, 1 - y)]


def _all_gather_small(name, blk):
    m_per, n = blk.shape

    def body(x_ref, out_ref, send_sems, recv_sems, local_sem):
        x, y, c = _mesh_pos()
        me, sibling = (x, y, c), (x, y, 1 - c)
        chips = _other_chips(x, y)

        def rows(px, py, pc):
            return out_ref.at[pl.ds((4 * px + 2 * py + pc) * m_per, m_per), :]

        def copy(k, block, to, src=None):
            return pltpu.make_async_remote_copy(
                src_ref=rows(*block) if src is None else src, dst_ref=rows(*block),
                send_sem=send_sems.at[k], recv_sem=recv_sems.at[k], device_id=to, device_id_type=MESH)

        mine = pltpu.make_async_copy(x_ref, rows(*me), local_sem)
        mine.start()
        first = [copy(0, me, sibling, src=x_ref)]
        first += [copy(1 + j, me, (*chip, c), src=x_ref) for j, chip in enumerate(chips)]
        for cp in first:
            cp.start()
        passed = [copy(4 + j, (*chip, c), sibling) for j, chip in enumerate(chips)]
        for j, chip in enumerate(chips):
            copy(1 + j, (*chip, c), me).wait_recv()
            passed[j].start()
        copy(0, sibling, me).wait_recv()
        for j, chip in enumerate(chips):
            copy(4 + j, (*chip, 1 - c), me).wait_recv()
        for cp in first + passed:
            cp.wait_send()
        mine.wait()

    return pl.pallas_call(
        body, out_shape=jax.ShapeDtypeStruct((N_DEV * m_per, n), blk.dtype),
        in_specs=[pl.BlockSpec(memory_space=pltpu.VMEM)], out_specs=pl.BlockSpec(memory_space=pltpu.VMEM),
        scratch_shapes=[pltpu.SemaphoreType.DMA((7,)), pltpu.SemaphoreType.DMA((7,)), pltpu.SemaphoreType.DMA],
        compiler_params=pltpu.CompilerParams(vmem_limit_bytes=VMEM_LIMIT), name=name)(blk)


HBM_SPEC = pl.BlockSpec(memory_space=pltpu.HBM)


def _all_gather_weights(name, shards):
    n = len(shards)

    def body(*refs):
        srcs, outs = refs[:n], refs[n:2 * n]
        send_sems, recv_sems, local_sems = refs[2 * n:]
        x, y, c = _mesh_pos()
        me, sibling = (x, y, c), (x, y, 1 - c)
        chips = _other_chips(x, y)

        def win(t, px, py, pc):
            rh = shards[t].shape[1] // 2
            return outs[t].at[:, 2 * px + py, pl.ds(pc * rh, rh), :]

        def own(t):
            rh = shards[t].shape[1] // 2
            return srcs[t].at[:, pl.ds(c * rh, rh), :]

        def copy(t, k, block, to, src=None):
            return pltpu.make_async_remote_copy(
                src_ref=win(t, *block) if src is None else src, dst_ref=win(t, *block),
                send_sem=send_sems.at[t, k], recv_sem=recv_sems.at[t, k], device_id=to, device_id_type=MESH)

        mine = [pltpu.make_async_copy(own(t), win(t, *me), local_sems.at[t]) for t in range(n)]
        for cp in mine:
            cp.start()
        first = []
        for t in range(n):
            first.append(copy(t, 0, me, sibling, src=own(t)))
            first += [copy(t, 1 + j, me, (*chip, c), src=own(t)) for j, chip in enumerate(chips)]
        for cp in first:
            cp.start()
        passed = []
        for j, chip in enumerate(chips):
            for t in range(n):
                copy(t, 1 + j, (*chip, c), me).wait_recv()
                fwd = copy(t, 4 + j, (*chip, c), sibling)
                fwd.start()
                passed.append(fwd)
        for t in range(n):
            copy(t, 0, sibling, me).wait_recv()
            for j, chip in enumerate(chips):
                copy(t, 4 + j, (*chip, 1 - c), me).wait_recv()
        for cp in first + passed:
            cp.wait_send()
        for cp in mine:
            cp.wait()

    out_shape = [jax.ShapeDtypeStruct((w.shape[0], N_CHIPS, w.shape[1], w.shape[2]), w.dtype) for w in shards]
    return pl.pallas_call(
        body, out_shape=out_shape, in_specs=[HBM_SPEC] * n, out_specs=[HBM_SPEC] * n,
        scratch_shapes=[pltpu.SemaphoreType.DMA((n, 7)), pltpu.SemaphoreType.DMA((n, 7)),
                        pltpu.SemaphoreType.DMA((n,))],
        name=name)(*shards)


def _exchange_sibling_halves(name, grads):
    n = len(grads)

    def body(*refs):
        srcs, outs = refs[:n], refs[n:2 * n]
        send_sems, recv_sems = refs[2 * n:]
        x, y, c = _mesh_pos()
        cps = []
        for t in range(n):
            rh = grads[t].shape[1] // 2
            cps.append(pltpu.make_async_remote_copy(
                src_ref=srcs[t].at[:, pl.ds((1 - c) * rh, rh), :], dst_ref=outs[t],
                send_sem=send_sems.at[t], recv_sem=recv_sems.at[t], device_id=(x, y, 1 - c), device_id_type=MESH))
        for cp in cps:
            cp.start()
        for cp in cps:
            cp.wait()

    out_shape = [jax.ShapeDtypeStruct((g.shape[0], g.shape[1] // 2, g.shape[2]), g.dtype) for g in grads]
    return pl.pallas_call(
        body, out_shape=out_shape, in_specs=[HBM_SPEC] * n, out_specs=[HBM_SPEC] * n,
        scratch_shapes=[pltpu.SemaphoreType.DMA((n,)), pltpu.SemaphoreType.DMA((n,))],
        name=name)(*grads)


def _add_halves(name, g, recv, c_idx):
    nb, r, c = g.shape
    rh = r // 2
    tr = rh
    for cand in (512, 256, 128, 64):
        if rh % cand == 0:
            tr = cand
            break
    g4 = g.reshape(nb, 2, rh, c)

    def body(c_ref, g_ref, r_ref, o_ref):
        o_ref[...] = (g_ref[...] + r_ref[...]).astype(BF16)

    grid_spec = pltpu.PrefetchScalarGridSpec(
        num_scalar_prefetch=1, grid=(nb, rh // tr),
        in_specs=[pl.BlockSpec((None, None, tr, c), lambda j, i, cr: (j, cr[0], i, 0)),
                  pl.BlockSpec((None, tr, c), lambda j, i, cr: (j, i, 0))],
        out_specs=pl.BlockSpec((None, tr, c), lambda j, i, cr: (j, i, 0)))
    return pl.pallas_call(
        body, grid_spec=grid_spec, out_shape=jax.ShapeDtypeStruct((nb, rh, c), BF16),
        compiler_params=_params("parallel", "parallel"), name=name)(c_idx, g4, recv)


def _exchange_chips(name, parts, n_layers):
    n = len(parts)
    n_w = n // n_layers

    def body(*refs):
        srcs, outs = refs[:n], refs[n:n + n_w]
        send_sems, recv_sems = refs[n + n_w:]
        x, y, c = _mesh_pos()
        cps = []
        for t in range(n):
            w_i, l_i = divmod(t, n_layers)
            for k, (px, py) in enumerate(_other_chips(x, y)):
                cps.append(pltpu.make_async_remote_copy(
                    src_ref=srcs[t].at[2 * px + py], dst_ref=outs[w_i].at[l_i, k],
                    send_sem=send_sems.at[t, k], recv_sem=recv_sems.at[t, k],
                    device_id=(px, py, c), device_id_type=MESH))
        for cp in cps:
            cp.start()
        for cp in cps:
            cp.wait()

    out_shape = [jax.ShapeDtypeStruct((n_layers, 3) + parts[w_i * n_layers].shape[1:], BF16) for w_i in range(n_w)]
    return pl.pallas_call(
        body, out_shape=out_shape, in_specs=[HBM_SPEC] * n, out_specs=[HBM_SPEC] * n_w,
        scratch_shapes=[pltpu.SemaphoreType.DMA((n, 3)), pltpu.SemaphoreType.DMA((n, 3))],
        name=name)(*parts)


def _sum_chips(name, own_parts, recv, pos_idx):
    n_layers = len(own_parts)
    _, _, rh, c = recv.shape
    tr = rh
    for cand in (512, 256, 128, 64):
        if rh % cand == 0:
            tr = cand
            break
    nb = rh // tr

    def body(pos_ref, *refs):
        own_refs, r_ref, o_ref = refs[:n_layers], refs[n_layers], refs[n_layers + 1]
        for l_i in range(n_layers):
            @pl.when(pl.program_id(0) == l_i)
            def _():
                acc = own_refs[l_i][...].astype(F32)
                for k in range(3):
                    acc = acc + r_ref[k].astype(F32)
                o_ref[...] = acc

    grid_spec = pltpu.PrefetchScalarGridSpec(
        num_scalar_prefetch=1, grid=(n_layers, nb),
        in_specs=[pl.BlockSpec((None, tr, c), lambda l, i, pr: (pr[0], i, 0)) for _ in range(n_layers)]
        + [pl.BlockSpec((None, 3, tr, c), lambda l, i, pr: (l, 0, i, 0))],
        out_specs=pl.BlockSpec((None, tr, c), lambda l, i, pr: (l, pr[1] * nb + i, 0)))
    return pl.pallas_call(
        body, grid_spec=grid_spec, out_shape=jax.ShapeDtypeStruct((n_layers, 2 * rh, c), F32),
        compiler_params=_params("parallel", "parallel"), name=name)(pos_idx, *own_parts, recv)


def _share_with_sibling(name, grads):
    n = len(grads)

    def body(*refs):
        outs = refs[n:2 * n]
        send_sems, recv_sems = refs[2 * n:]
        x, y, c = _mesh_pos()
        cps, landed = [], []
        for t in range(n):
            rh = grads[t].shape[1] // 2
            mine = outs[t].at[:, pl.ds(c * rh, rh), :]
            theirs = outs[t].at[:, pl.ds((1 - c) * rh, rh), :]
            cps.append(pltpu.make_async_remote_copy(
                src_ref=mine, dst_ref=mine, send_sem=send_sems.at[t], recv_sem=recv_sems.at[t],
                device_id=(x, y, 1 - c), device_id_type=MESH))
            landed.append(pltpu.make_async_remote_copy(
                src_ref=mine, dst_ref=theirs, send_sem=send_sems.at[t], recv_sem=recv_sems.at[t],
                device_id=(x, y, 1 - c), device_id_type=MESH))
        for cp in cps:
            cp.start()
        for cp, ld in zip(cps, landed):
            cp.wait_send()
            ld.wait_recv()

    out_shape = [jax.ShapeDtypeStruct(g.shape, g.dtype) for g in grads]
    return pl.pallas_call(
        body, out_shape=out_shape, in_specs=[HBM_SPEC] * n, out_specs=[HBM_SPEC] * n,
        input_output_aliases={t: t for t in range(n)},
        scratch_shapes=[pltpu.SemaphoreType.DMA((n,)), pltpu.SemaphoreType.DMA((n,))],
        name=name)(*grads)


def _row(v):
    return v.reshape(1, -1)


def kernel(x, c, g_norm1, w_in, g_q, g_k, rel_bias, w_o, g_norm2, w1, w2, w_ada, b_ada, loss_target, m_g_norm1, m_w_in, m_g_q, m_g_k, m_rel_bias, m_w_o, m_g_norm2, m_w1, m_w2, m_w_ada, m_b_ada, v_g_norm1, v_w_in, v_g_q, v_g_k, v_rel_bias, v_w_o, v_g_norm2, v_w1, v_w2, v_w_ada, v_b_ada):
    _, s, d = x.shape
    n_layers = g_norm1.shape[0]
    n_pairs = (d // 2) // LANES
    n_ca_heads = rel_bias.shape[1]
    n_rel = rel_bias.shape[2]
    assert n_rel == 2 * REL_CLIP + 1 and g_q.shape[1] == HEAD_DIM and n_ca_heads == 2 * n_pairs
    ada_c = w_ada.shape[2]

    ax, ay, ac = _mesh_pos()
    chip = 2 * ax + ay
    dev = 4 * ax + 2 * ay + ac
    c_idx = jnp.reshape(ac, (1,)).astype(jnp.int32)

    c_act_all = _all_gather_small("ag_c", _silu_rows("silu_c", c)).reshape(N_DEV, 8, d)[:, 0, :]
    mod_parts = []
    for l in range(n_layers):
        bias = lax.dynamic_slice_in_dim(b_ada[l], chip * ada_c, ada_c).reshape(1, ada_c)
        mod_parts.append(_ada_mod(f"ada_mod{l}", c_act_all, w_ada[l], bias))
    mod_all = _all_gather_small("ag_mod", jnp.concatenate(mod_parts, axis=1))
    mod_all = mod_all.reshape(N_CHIPS, 2, N_DEV, n_layers, ada_c)[:, 0]
    mod_all = jnp.transpose(mod_all, (1, 2, 0, 3)).reshape(N_DEV, n_layers, N_CHIPS * ada_c)
    mod = lax.dynamic_index_in_dim(mod_all, dev, axis=0, keepdims=False)

    wg_in, wg_o, wg_1, wg_2 = _all_gather_weights(
        "ag_weights", [w_in.astype(BF16), w_o.astype(BF16), w1.astype(BF16), w2.astype(BF16)])

    xs = x[0]
    saved = []
    for l in range(n_layers):
        sh1, sc1, gt1, sh2, sc2, gt2 = [_row(mod[l, i * d:(i + 1) * d]) for i in range(6)]
        gq2 = _row(jnp.tile(g_q[l], 2))
        gk2 = _row(jnp.tile(g_k[l], 2))
        bias_t = _pair_transposed(_bias_tile(rel_bias[l]))
        h1 = _norm_mod(f"norm1_{l}", xs, _row(g_norm1[l]), sc1, sh1)
        proj = _mm_nn_w(f"proj_{l}", h1, wg_in, l, True, [F32])[0]
        o_sb = _sb_fwd(f"sb_fwd_{l}", proj, n_pairs)
        qn, knp, vp = _qk_norm(f"qk_norm_{l}", proj, gq2, gk2, n_pairs)
        o_ca = _ca_fwd(f"ca_fwd_{l}", qn, knp, vp, bias_t, n_pairs)
        mixed = jnp.concatenate([o_sb, o_ca], axis=1)

        def res_epi(acc, res, gt):
            return res + gt * acc, acc

        def res_specs(tm, tn):
            return (pl.BlockSpec((tm, tn), lambda i, j, kk: (i, j)), pl.BlockSpec((1, tn), lambda i, j, kk: (0, j)))

        x1, m1 = _mm_nn_w(f"attn_out_{l}", mixed, wg_o, l, False, [F32, BF16], epi=res_epi,
                          extras=(xs, gt1), extra_specs_fn=res_specs)
        h2 = _norm_mod(f"norm2_{l}", x1, _row(g_norm2[l]), sc2, sh2)

        def act_epi(acc):
            r = jnp.maximum(acc, 0.0)
            return acc, r * r

        u, a = _mm_nn_w(f"mlp_up_{l}", h2, wg_1, l, True, [BF16, BF16], epi=act_epi)
        x2, m2 = _mm_nn_w(f"mlp_down_{l}", a, wg_2, l, False, [F32, BF16], epi=res_epi,
                          extras=(x1, gt2), extra_specs_fn=res_specs)
        saved.append(dict(x0=xs, h1=h1, proj=proj, o_sb=o_sb, qn=qn, knp=knp, vp=vp, bias_t=bias_t, mixed=mixed,
                          m1=m1, x1=x1, h2=h2, u=u, a=a, m2=m2, gq2=gq2, gk2=gk2,
                          sc1=sc1, gt1=gt1, sc2=sc2, gt2=gt2))
        xs = x2

    dx, sq = _loss_grad("loss", xs, loss_target[0])
    loss_part = (0.5 * jnp.sum(sq) / d).reshape(1)

    g_in, g_o, g_1, g_2 = [None] * n_layers, [None] * n_layers, [None] * n_layers, [None] * n_layers
    dmod, dg1, dg2, dgq, dgk, drel = [], [], [], [], [], []
    for l in reversed(range(n_layers)):
        sv = saved[l]
        dm2, dgt2 = _gate_bwd(f"gate2_bwd_{l}", dx, sv["m2"], sv["gt2"])

        def act_bwd_epi(acc, u_t):
            return (acc * (2.0 * jnp.maximum(u_t.astype(F32), 0.0)),)

        def tile_specs(tm, tn):
            return (pl.BlockSpec((tm, tn), lambda i, j, kk: (i, j)),)

        du = _mm_nt_w(f"mlp_down_bwd_{l}", dm2, wg_2, l, False, BF16, epi=act_bwd_epi, extras=(sv["u"],),
                      extra_specs_fn=tile_specs)
        g_2[l] = _mm_tn(f"w2_grad_{l}", sv["a"], dm2, False)
        dh2 = _mm_nt_w(f"mlp_up_bwd_{l}", du, wg_1, l, True, F32)
        g_1[l] = _mm_tn(f"w1_grad_{l}", sv["h2"], du, True)
        dx1, dsh2, dsc2, dgn2 = _norm_mod_bwd(f"norm2_bwd_{l}", dh2, sv["x1"], _row(g_norm2[l]), sv["sc2"], dx)
        dm1, dgt1 = _gate_bwd(f"gate1_bwd_{l}", dx1, sv["m1"], sv["gt1"])
        dmixed = _mm_nt_w(f"attn_out_bwd_{l}", dm1, wg_o, l, False, F32)
        g_o[l] = _mm_tn(f"wo_grad_{l}", sv["mixed"], dm1, False)
        dq_sb, dk_sb, dv_sb = _sb_bwd(f"sb_bwd_{l}", sv["proj"], sv["o_sb"], dmixed, n_pairs)
        dqn, dknp, dvp, dbias_t = _ca_bwd(f"ca_bwd_{l}", sv["qn"], sv["knp"], sv["vp"], sv["bias_t"], dmixed,
                                          n_pairs, n_pairs)
        dq_ca, dk_ca, dgq_l, dgk_l = _qk_norm_bwd(f"qk_norm_bwd_{l}", dqn, dknp, sv["proj"], sv["gq2"], sv["gk2"],
                                                  n_pairs)
        drel_l = _rel_bias_grad(f"rel_bias_grad_{l}", _query_major(dbias_t))[:, :n_rel]
        dproj = jnp.concatenate([dq_sb.astype(BF16), dk_sb.astype(BF16), dv_sb.astype(BF16), dq_ca, dk_ca,
                                 dvp[PAD:].astype(BF16)], axis=1)
        g_in[l] = _mm_tn(f"win_grad_{l}", sv["h1"], dproj, True)
        dh1 = _mm_nt_w(f"proj_bwd_{l}", dproj, wg_in, l, True, F32)
        dx, dsh1, dsc1, dgn1 = _norm_mod_bwd(f"norm1_bwd_{l}", dh1, sv["x0"], _row(g_norm1[l]), sv["sc1"], dx1)
        dmod.insert(0, jnp.concatenate([dsh1, dsc1, dgt1, dsh2, dsc2, dgt2], axis=1)[0])
        dg1.insert(0, dgn1[0])
        dg2.insert(0, dgn2[0])
        dgq.insert(0, dgq_l[0, :HEAD_DIM])
        dgk.insert(0, dgk_l[0, :HEAD_DIM])
        drel.insert(0, drel_l.reshape(-1))
    grad_x = dx[None]

    small = jnp.concatenate([jnp.concatenate(dmod), jnp.concatenate(dg1), jnp.concatenate(dg2),
                             jnp.concatenate(dgq), jnp.concatenate(dgk), jnp.concatenate(drel), loss_part])
    n_small = small.shape[0]
    n_pack = -(-n_small // (8 * LANES)) * (8 * LANES)
    small = jnp.pad(small, (0, n_pack - n_small)).reshape(8, n_pack // 8)
    small_all = _all_gather_small("ag_small", small).reshape(N_DEV, 8, n_pack // 8)
    small_sum = _sum_rows8("sum_small", small_all).reshape(-1)
    dmod_all = small_all.reshape(N_DEV, n_pack)[:, :n_layers * 6 * d].reshape(N_DEV, n_layers, 6 * d)

    off = 0

    def take(n, shape):
        nonlocal off
        out = small_sum[off:off + n].reshape(shape)
        off += n
        return out

    grad_b_ada = take(n_layers * 6 * d, (n_layers, 6 * d))
    grad_g_norm1 = take(n_layers * d, (n_layers, d))
    grad_g_norm2 = take(n_layers * d, (n_layers, d))
    grad_g_q = take(n_layers * HEAD_DIM, (n_layers, HEAD_DIM))
    grad_g_k = take(n_layers * HEAD_DIM, (n_layers, HEAD_DIM))
    grad_rel_bias = take(n_layers * n_ca_heads * n_rel, (n_layers, n_ca_heads, n_rel))
    loss = take(1, ())

    c_act_t = jnp.transpose(c_act_all)
    grad_w_ada = jnp.stack([
        _outer_sum(f"wada_grad_{l}", c_act_t,
                   lax.dynamic_slice_in_dim(dmod_all[:, l, :], chip * ada_c, ada_c, axis=1))
        for l in range(n_layers)])

    partial = [g for per_w in (g_in, g_o, g_1, g_2) for g in per_w]
    from_sibling = _exchange_sibling_halves("rs_sibling", partial)
    chip_sums = [_add_halves(f"rs_add_{i}", g, r, c_idx) for i, (g, r) in enumerate(zip(partial, from_sibling))]
    from_chips = _exchange_chips("rs_chips", chip_sums, n_layers)
    pos_idx = jnp.stack([chip, ac]).astype(jnp.int32)
    reduced = [_sum_chips(f"rs_sum_{w_i}", chip_sums[w_i * n_layers:(w_i + 1) * n_layers], from_chips[w_i], pos_idx)
               for w_i in range(4)]
    grad_w_in, grad_w_o, grad_w1, grad_w2 = _share_with_sibling("rs_share", reduced)

    def adam_big(name, w, g, m, v):
        shp = w.shape
        outs = _adamw(name, w.reshape(-1, shp[-1]), g.reshape(-1, shp[-1]), m.reshape(-1, shp[-1]),
                      v.reshape(-1, shp[-1]))
        return [o.reshape(shp) for o in outs]

    def pack(arrs):
        flat = jnp.concatenate([a.reshape(-1) for a in arrs])
        n = flat.shape[0]
        n_p = -(-n // (8 * LANES)) * (8 * LANES)
        return jnp.pad(flat, (0, n_p - n), constant_values=1.0).reshape(8, n_p // 8)

    small_w = [g_norm1, g_q, g_k, rel_bias, g_norm2, b_ada]
    small_g = [grad_g_norm1, grad_g_q, grad_g_k, grad_rel_bias, grad_g_norm2, grad_b_ada]
    small_m = [m_g_norm1, m_g_q, m_g_k, m_rel_bias, m_g_norm2, m_b_ada]
    small_v = [v_g_norm1, v_g_q, v_g_k, v_rel_bias, v_g_norm2, v_b_ada]
    packed = _adamw("adamw_small", pack(small_w), pack(small_g), pack(small_m), pack(small_v))

    def unpack(p):
        flat = p.reshape(-1)
        res, o = [], 0
        for a in small_w:
            res.append(flat[o:o + a.size].reshape(a.shape))
            o += a.size
        return res

    sd, sm, sv_ = unpack(packed[0]), unpack(packed[1]), unpack(packed[2])
    big = {
        "w_in": adam_big("adamw_w_in", w_in, grad_w_in, m_w_in, v_w_in),
        "w_o": adam_big("adamw_w_o", w_o, grad_w_o, m_w_o, v_w_o),
        "w1": adam_big("adamw_w1", w1, grad_w1, m_w1, v_w1),
        "w2": adam_big("adamw_w2", w2, grad_w2, m_w2, v_w2),
        "w_ada": adam_big("adamw_w_ada", w_ada, grad_w_ada, m_w_ada, v_w_ada),
    }

    def ordered(kind):
        sm_list = (sd, sm, sv_)[kind]
        return [sm_list[0], big["w_in"][kind], sm_list[1], sm_list[2], sm_list[3], big["w_o"][kind], sm_list[4],
                big["w1"][kind], big["w2"][kind], big["w_ada"][kind], sm_list[5]]

    grads = [grad_g_norm1, grad_w_in, grad_g_q, grad_g_k, grad_rel_bias, grad_w_o, grad_g_norm2, grad_w1, grad_w2,
             grad_w_ada, grad_b_ada]
    return (loss, grad_x, *grads, *ordered(0), *ordered(1), *ordered(2))
```

```python
import functools

import jax
import jax.numpy as jnp
from jax import lax
from jax.experimental import pallas as pl
from jax.experimental.pallas import tpu as pltpu

F32 = jnp.float32
BF16 = jnp.bfloat16
MESH = pl.DeviceIdType.MESH

EPS = 1e-6
NEG_INF = -1e30
HEAD_DIM = 64
CHUNK = 64
LEFT_CHUNKS = 8
PAD = LEFT_CHUNKS * CHUNK
BAND = PAD + CHUNK
REL_CLIP = 128
SCALE = HEAD_DIM ** -0.5
DEAD_LOG = -88.0
LANES = 128
N_CHIPS = 4
N_DEV = 8

ADAM_LR = 0.001
ADAM_B1 = 0.9
ADAM_B2 = 0.999
ADAM_EPS = 1e-08
ADAM_WD = 0.01
ADAM_STEP = 10

VMEM_LIMIT = 48 * 1024 * 1024
ROW_TILE = 256
MM_TILE = 1024
SB_TILE = 256
CA_TILE = 256

NT = (((1,), (1,)), ((), ()))
TN = (((0,), (0,)), ((), ()))
NN = (((1,), (0,)), ((), ()))


def _tile(n, pref):
    best = None
    t = LANES
    while t <= min(n, pref):
        if n % t == 0:
            best = t
        t += LANES
    return best if best is not None else n


def _params(*sem):
    return pltpu.CompilerParams(dimension_semantics=sem, vmem_limit_bytes=VMEM_LIMIT)


def _split_dot(x, t, passes):
    acc = None
    r = x
    for i in range(passes):
        hi = r.astype(BF16)
        d = jnp.dot(hi, t, preferred_element_type=F32)
        acc = d if acc is None else acc + d
        if i + 1 < passes:
            r = r - hi.astype(F32)
    return acc


def _matmul(name, a, b, *, dims, grid, a_spec, b_spec, out_shapes, out_specs, acc_shape,
            epi=None, extras=(), extra_specs=()):
    nk = grid[2]
    n_ex = len(extras)
    n_out = len(out_shapes)

    def body(*refs):
        a_ref, b_ref = refs[0], refs[1]
        ex = refs[2:2 + n_ex]
        outs = refs[2 + n_ex:2 + n_ex + n_out]
        acc = refs[-1]
        k = pl.program_id(2)

        @pl.when(k == 0)
        def _():
            acc[...] = jnp.zeros_like(acc)

        acc[...] += lax.dot_general(a_ref[...].astype(BF16), b_ref[...].astype(BF16), dims,
                                    preferred_element_type=F32)

        @pl.when(k == nk - 1)
        def _():
            res = epi(acc[...], *[e[...] for e in ex]) if epi is not None else (acc[...],)
            for o, r in zip(outs, res):
                o[...] = r.astype(o.dtype)

    return pl.pallas_call(
        body, grid=grid, in_specs=[a_spec, b_spec, *extra_specs], out_specs=out_specs,
        out_shape=out_shapes, scratch_shapes=[pltpu.VMEM(acc_shape, F32)],
        compiler_params=_params("parallel", "parallel", "arbitrary"), name=name,
    )(a, b, *extras)


def _mm_nn_w(name, a, wg, layer, colshard, out_dtypes, epi=None, extras=(), extra_specs_fn=None):
    m = a.shape[0]
    _, _, r, c = wg.shape
    tm = min(MM_TILE, m)
    if colshard:
        k, n = r, N_CHIPS * c
        tk, tn = k, _tile(c, MM_TILE)
        npc = c // tn
        b_spec = pl.BlockSpec((None, None, tk, tn), lambda i, j, kk: (layer, j // npc, kk, j % npc))
    else:
        k, n = N_CHIPS * r, c
        tk, tn = _tile(r, MM_TILE), _tile(c, MM_TILE)
        kpc = r // tk
        b_spec = pl.BlockSpec((None, None, tk, tn), lambda i, j, kk: (layer, kk // kpc, kk % kpc, j))
    assert a.shape[1] == k
    grid = (m // tm, n // tn, k // tk)
    out_spec = pl.BlockSpec((tm, tn), lambda i, j, kk: (i, j))
    ex_specs = extra_specs_fn(tm, tn) if extra_specs_fn is not None else ()
    return _matmul(name, a, wg, dims=NN, grid=grid,
                   a_spec=pl.BlockSpec((tm, tk), lambda i, j, kk: (i, kk)), b_spec=b_spec,
                   out_shapes=[jax.ShapeDtypeStruct((m, n), d) for d in out_dtypes],
                   out_specs=[out_spec for _ in out_dtypes], acc_shape=(tm, tn),
                   epi=epi, extras=extras, extra_specs=ex_specs)


def _mm_nt_w(name, a, wg, layer, colshard, out_dtype, epi=None, extras=(), extra_specs_fn=None):
    m = a.shape[0]
    _, _, r, c = wg.shape
    tm = min(MM_TILE, m)
    if colshard:
        kdim, n = N_CHIPS * c, r
        tk, tn = _tile(c, MM_TILE), _tile(r, MM_TILE)
        kpc = c // tk
        b_spec = pl.BlockSpec((None, None, tn, tk), lambda i, j, kk: (layer, kk // kpc, j, kk % kpc))
    else:
        kdim, n = c, N_CHIPS * r
        tk, tn = c, _tile(r, MM_TILE)
        npc = r // tn
        b_spec = pl.BlockSpec((None, None, tn, tk), lambda i, j, kk: (layer, j // npc, j % npc, kk))
    assert a.shape[1] == kdim
    grid = (m // tm, n // tn, kdim // tk)
    ex_specs = extra_specs_fn(tm, tn) if extra_specs_fn is not None else ()
    return _matmul(name, a, wg, dims=NT, grid=grid,
                   a_spec=pl.BlockSpec((tm, tk), lambda i, j, kk: (i, kk)), b_spec=b_spec,
                   out_shapes=[jax.ShapeDtypeStruct((m, n), out_dtype)],
                   out_specs=[pl.BlockSpec((tm, tn), lambda i, j, kk: (i, j))], acc_shape=(tm, tn),
                   epi=epi, extras=extras, extra_specs=ex_specs)[0]


def _mm_tn(name, a, b, colshard):
    s, m = a.shape
    n = b.shape[1]
    tk = min(2 * MM_TILE, s)
    if colshard:
        c = n // N_CHIPS
        tm, tn = _tile(m, MM_TILE // 2), _tile(c, MM_TILE)
        npc = c // tn
        out_shape = (N_CHIPS, m, c)
        out_spec = pl.BlockSpec((None, tm, tn), lambda i, j, kk: (j // npc, i, j % npc))
    else:
        r = m // N_CHIPS
        tm, tn = _tile(r, MM_TILE // 2), _tile(n, MM_TILE)
        mpc = r // tm
        out_shape = (N_CHIPS, r, n)
        out_spec = pl.BlockSpec((None, tm, tn), lambda i, j, kk: (i // mpc, i % mpc, j))
    grid = (m // tm, n // tn, s // tk)
    return _matmul(name, a, b, dims=TN, grid=grid,
                   a_spec=pl.BlockSpec((tk, tm), lambda i, j, kk: (kk, i)),
                   b_spec=pl.BlockSpec((tk, tn), lambda i, j, kk: (kk, j)),
                   out_shapes=[jax.ShapeDtypeStruct(out_shape, F32)], out_specs=[out_spec],
                   acc_shape=(tm, tn))[0]


def _row_spec(ts, d):
    return pl.BlockSpec((ts, d), lambda i: (i, 0))


def _vec_spec(d):
    return pl.BlockSpec((1, d), lambda i: (0, 0))


def _norm_mod(name, x, g, sc, sh):
    s, d = x.shape
    ts = min(ROW_TILE, s)

    def body(x_ref, g_ref, sc_ref, sh_ref, h_ref):
        xv = x_ref[...]
        r = lax.rsqrt(jnp.mean(xv * xv, axis=-1, keepdims=True) + EPS)
        h_ref[...] = (xv * r * g_ref[...] * (1.0 + sc_ref[...]) + sh_ref[...]).astype(BF16)

    return pl.pallas_call(
        body, grid=(s // ts,), in_specs=[_row_spec(ts, d), _vec_spec(d), _vec_spec(d), _vec_spec(d)],
        out_specs=_row_spec(ts, d), out_shape=jax.ShapeDtypeStruct((s, d), BF16),
        compiler_params=_params("parallel"), name=name)(x, g, sc, sh)


def _norm_mod_bwd(name, dh, x, g, sc, dres):
    s, d = x.shape
    ts = min(ROW_TILE, s)

    def body(dh_ref, x_ref, g_ref, sc_ref, dres_ref, dx_ref, dsh_ref, dsc_ref, dg_ref):
        @pl.when(pl.program_id(0) == 0)
        def _():
            dsh_ref[...] = jnp.zeros_like(dsh_ref)
            dsc_ref[...] = jnp.zeros_like(dsc_ref)
            dg_ref[...] = jnp.zeros_like(dg_ref)

        xv = x_ref[...]
        r = lax.rsqrt(jnp.mean(xv * xv, axis=-1, keepdims=True) + EPS)
        xhat = xv * r
        dhv = dh_ref[...]
        gv = g_ref[...]
        opsc = 1.0 + sc_ref[...]
        dxhat = dhv * (gv * opsc)
        mdot = jnp.mean(dxhat * xhat, axis=-1, keepdims=True)
        dx_ref[...] = dres_ref[...] + r * (dxhat - xhat * mdot)
        dhx = dhv * xhat
        dsh_ref[...] += jnp.sum(dhv, axis=0, keepdims=True)
        dsc_ref[...] += jnp.sum(dhx * gv, axis=0, keepdims=True)
        dg_ref[...] += jnp.sum(dhx * opsc, axis=0, keepdims=True)

    vec = jax.ShapeDtypeStruct((1, d), F32)
    return pl.pallas_call(
        body, grid=(s // ts,),
        in_specs=[_row_spec(ts, d), _row_spec(ts, d), _vec_spec(d), _vec_spec(d), _row_spec(ts, d)],
        out_specs=[_row_spec(ts, d), _vec_spec(d), _vec_spec(d), _vec_spec(d)],
        out_shape=[jax.ShapeDtypeStruct((s, d), F32), vec, vec, vec],
        compiler_params=_params("arbitrary"), name=name)(dh, x, g, sc, dres)


def _gate_bwd(name, dx, m, gt):
    s, d = dx.shape
    ts = min(ROW_TILE, s)

    def body(dx_ref, m_ref, gt_ref, dm_ref, dgt_ref):
        @pl.when(pl.program_id(0) == 0)
        def _():
            dgt_ref[...] = jnp.zeros_like(dgt_ref)

        dxv = dx_ref[...]
        dm_ref[...] = (dxv * gt_ref[...]).astype(BF16)
        dgt_ref[...] += jnp.sum(dxv * m_ref[...].astype(F32), axis=0, keepdims=True)

    return pl.pallas_call(
        body, grid=(s // ts,), in_specs=[_row_spec(ts, d), _row_spec(ts, d), _vec_spec(d)],
        out_specs=[_row_spec(ts, d), _vec_spec(d)],
        out_shape=[jax.ShapeDtypeStruct((s, d), BF16), jax.ShapeDtypeStruct((1, d), F32)],
        compiler_params=_params("arbitrary"), name=name)(dx, m, gt)


def _loss_grad(name, y, target):
    s, d = y.shape
    ts = min(ROW_TILE, s)

    def body(y_ref, t_ref, dy_ref, sq_ref):
        @pl.when(pl.program_id(0) == 0)
        def _():
            sq_ref[...] = jnp.zeros_like(sq_ref)

        e = y_ref[...] - t_ref[...]
        dy_ref[...] = e * (1.0 / d)
        sq_ref[...] += jnp.sum(e * e, axis=0, keepdims=True)

    return pl.pallas_call(
        body, grid=(s // ts,), in_specs=[_row_spec(ts, d), _row_spec(ts, d)],
        out_specs=[_row_spec(ts, d), _vec_spec(d)],
        out_shape=[jax.ShapeDtypeStruct((s, d), F32), jax.ShapeDtypeStruct((1, d), F32)],
        compiler_params=_params("arbitrary"), name=name)(y, target)


def _adamw(name, w, g, m, v):
    r, c = w.shape
    tr = r
    for cand in (512, 256, 128, 64, 32, 16, 8):
        if r % cand == 0 and cand * c * 4 <= 2 * 1024 * 1024:
            tr = cand
            break

    def body(w_ref, g_ref, m_ref, v_ref, d_ref, mo_ref, vo_ref):
        gv = g_ref[...]
        m2 = ADAM_B1 * m_ref[...] + (1.0 - ADAM_B1) * gv
        v2 = ADAM_B2 * v_ref[...] + (1.0 - ADAM_B2) * (gv * gv)
        m_hat = m2 / (1.0 - ADAM_B1 ** ADAM_STEP)
        v_hat = v2 / (1.0 - ADAM_B2 ** ADAM_STEP)
        d_ref[...] = -ADAM_LR * (m_hat / (jnp.sqrt(v_hat) + ADAM_EPS) + ADAM_WD * w_ref[...])
        mo_ref[...] = m2
        vo_ref[...] = v2

    spec = pl.BlockSpec((tr, c), lambda i: (i, 0))
    shp = jax.ShapeDtypeStruct((r, c), F32)
    return pl.pallas_call(
        body, grid=(r // tr,), in_specs=[spec] * 4, out_specs=[spec] * 3, out_shape=[shp] * 3,
        compiler_params=_params("parallel"), name=name)(w, g, m, v)


def _sum_rows8(name, x):
    n, r, c = x.shape

    def body(x_ref, o_ref):
        acc = x_ref[0]
        for i in range(1, n):
            acc = acc + x_ref[i]
        o_ref[...] = acc

    return pl.pallas_call(
        body, out_shape=jax.ShapeDtypeStruct((r, c), F32),
        in_specs=[pl.BlockSpec(memory_space=pltpu.VMEM)], out_specs=pl.BlockSpec(memory_space=pltpu.VMEM),
        name=name)(x)


def _outer_sum(name, ct, dm):
    d, n_seq = ct.shape
    n = dm.shape[1]
    tr = min(256, d)

    def body(ct_ref, dm_ref, o_ref):
        acc = ct_ref[:, 0:1] * dm_ref[0:1, :]
        for s_i in range(1, n_seq):
            acc = acc + ct_ref[:, s_i:s_i + 1] * dm_ref[s_i:s_i + 1, :]
        o_ref[...] = acc

    return pl.pallas_call(
        body, grid=(d // tr,),
        in_specs=[pl.BlockSpec((tr, n_seq), lambda i: (i, 0)), pl.BlockSpec((n_seq, n), lambda i: (0, 0))],
        out_specs=pl.BlockSpec((tr, n), lambda i: (i, 0)), out_shape=jax.ShapeDtypeStruct((d, n), F32),
        compiler_params=_params("parallel"), name=name)(ct, dm)


def _ada_mod(name, c_act, w_ada_l, bias):
    n_seq, d = c_act.shape
    n = w_ada_l.shape[1]
    tn = _tile(n, 768)

    def epi(acc, b):
        return (acc + b,)

    return _matmul(name, c_act, w_ada_l, dims=NN, grid=(1, n // tn, 1),
                   a_spec=pl.BlockSpec((n_seq, d), lambda i, j, kk: (0, 0)),
                   b_spec=pl.BlockSpec((d, tn), lambda i, j, kk: (0, j)),
                   out_shapes=[jax.ShapeDtypeStruct((n_seq, n), F32)],
                   out_specs=[pl.BlockSpec((n_seq, tn), lambda i, j, kk: (0, j))], acc_shape=(n_seq, tn),
                   epi=epi, extras=(bias,), extra_specs=(pl.BlockSpec((1, tn), lambda i, j, kk: (0, j)),))[0]


def _silu_rows(name, c_row):
    d = c_row.shape[1]

    def body(c_ref, o_ref):
        cv = c_ref[...]
        o_ref[...] = jnp.broadcast_to(cv / (1.0 + jnp.exp(-cv)), (8, d))

    return pl.pallas_call(
        body, out_shape=jax.ShapeDtypeStruct((8, d), F32),
        in_specs=[pl.BlockSpec(memory_space=pltpu.VMEM)], out_specs=pl.BlockSpec(memory_space=pltpu.VMEM),
        name=name)(c_row)


def _sb_masks(t):
    row = lax.broadcasted_iota(jnp.int32, (t, t), 0)
    col = lax.broadcasted_iota(jnp.int32, (t, t), 1)
    strict = col < row
    tri = (row > col).astype(BF16)
    return strict, tri


def _col_minus_row(t):
    return lax.broadcasted_iota(jnp.int32, (t, t), 1) - lax.broadcasted_iota(jnp.int32, (t, t), 0)


def _sweep_left(step, jd, carry, runs_of):
    def alive(c):
        runs = runs_of(c)
        top = runs[0]
        for r in runs[1:]:
            top = jnp.maximum(top, r)
        return jnp.max(top)

    def cond(state):
        j, _, top = state
        return jnp.logical_and(j >= 0, top > DEAD_LOG)

    def body(state):
        j, c, _ = state
        c = step(j, c, False)
        return j - 1, c, alive(c)

    return lax.while_loop(cond, body, (jd - 1, carry, alive(carry)))[1]


def _log_sigmoids(z):
    sp = jnp.log(1.0 + jnp.exp(-jnp.abs(z)))
    return jnp.minimum(z, 0.0) - sp, jnp.minimum(-z, 0.0) - sp, sp


def _first_step(n0, n1):
    return jnp.logical_and(pl.program_id(0) == 0, pl.program_id(1) == 0)


def _last_step(n0, n1):
    return jnp.logical_and(pl.program_id(0) == n0 - 1, pl.program_id(1) == n1 - 1)


def _sb_fwd(name, proj, n_pairs, gather=()):
    s = proj.shape[0]
    t = min(SB_TILE, s)
    nq = s // t
    n_g = len(gather)

    heads = [slice(h * HEAD_DIM, (h + 1) * HEAD_DIM) for h in range(LANES // HEAD_DIM)]

    assert nq % 2 == 0

    def body(*refs):
        q_ref, k_ref, v_ref = refs[:3]
        o_ref = refs[3 + n_g]
        g_refs, sems = refs[4 + n_g:4 + 2 * n_g], refs[4 + 2 * n_g:]
        if n_g:
            @pl.when(_first_step(n_pairs, nq))
            def _():
                for cp in _gather_copies(g_refs, *sems, 0)[0]:
                    cp.start()

        _sb_fwd_body(q_ref, k_ref, v_ref, o_ref)
        if n_g:
            @pl.when(_last_step(n_pairs, nq))
            def _():
                for cp, ld in zip(*_gather_copies(g_refs, *sems, 0)):
                    cp.wait_send()
                    ld.wait_recv()

    def _sb_fwd_body(q_ref, k_ref, v_ref, o_ref):
        qi = pl.program_id(1)
        _, tri = _sb_masks(t)
        diff = _col_minus_row(t)
        qs = [(q_ref[:, sl] * SCALE).astype(BF16) for sl in heads]
        jd = qi // 2
        chains = [(sb, h) for sb in (1, 0) for h in range(len(heads))]

        def step(j, carry, masked):
            ks = [pl.multiple_of((2 * j + sb) * t, t) for sb, _ in chains]
            k_ = [k_ref[pl.ds(ks[i], t), heads[h]].astype(BF16) for i, (_, h) in enumerate(chains)]
            v_ = [v_ref[pl.ds(ks[i], t), heads[h]].astype(BF16) for i, (_, h) in enumerate(chains)]
            zs = [lax.dot_general(qs[h], k_[i], NT, preferred_element_type=F32) for i, (_, h) in enumerate(chains)]
            lbs, lms, keeps = [], [], []
            for (sb, _), z in zip(chains, zs):
                lb, lm, _ = _log_sigmoids(z)
                keep = None
                if masked:
                    keep = diff < (qi - 2 * j - sb) * t
                    lm = jnp.where(keep, lm, 0.0)
                lbs.append(lb)
                lms.append(lm)
                keeps.append(keep)
            css = [_split_dot(lm, tri, 2) for lm in lms]
            sums = [jnp.sum(lm, axis=1, keepdims=True) for lm in lms]
            ws = []
            for i, (sb, h) in enumerate(chains):
                run = carry[h][1] if sb == 1 else carry[h][1] + sums[h]
                w = jnp.exp(lbs[i] + css[i] + run)
                if masked:
                    w = jnp.where(keeps[i], w, 0.0)
                ws.append(w.astype(BF16))
            pvs = [jnp.dot(w, v, preferred_element_type=F32) for w, v in zip(ws, v_)]
            nh = len(heads)
            return tuple((carry[h][0] + pvs[h] + pvs[nh + h], carry[h][1] + sums[h] + sums[nh + h]) for h in range(nh))

        init = tuple((jnp.zeros((t, HEAD_DIM), F32), jnp.zeros((t, 1), F32)) for _ in heads)
        carry = step(jd, init, True)
        carry = _sweep_left(step, jd, carry, lambda c: [run for _, run in c])
        for (acc, _), sl in zip(carry, heads):
            o_ref[:, sl] = acc

    outs = pl.pallas_call(
        body, grid=(n_pairs, nq),
        in_specs=[pl.BlockSpec((t, LANES), lambda p, i: (i, p)),
                  pl.BlockSpec((s, LANES), lambda p, i: (0, n_pairs + p)),
                  pl.BlockSpec((s, LANES), lambda p, i: (0, 2 * n_pairs + p))] + [HBM_SPEC] * n_g,
        out_specs=[pl.BlockSpec((t, LANES), lambda p, i: (i, p))] + [HBM_SPEC] * n_g,
        out_shape=[jax.ShapeDtypeStruct((s, n_pairs * LANES), F32)]
        + [jax.ShapeDtypeStruct(g.shape, g.dtype) for g in gather],
        input_output_aliases={3 + i: 1 + i for i in range(n_g)},
        scratch_shapes=_gather_sems(n_g) if n_g else [],
        compiler_params=_params("arbitrary", "arbitrary"), name=name)(proj, proj, proj, *gather)
    return outs[0], list(outs[1:])


def _sb_bwd(name, proj, o_sb, dmixed, n_pairs, exchange=()):
    s = proj.shape[0]
    t = min(SB_TILE, s)
    nq = s // t
    n_x = len(exchange)
    heads = [slice(h * HEAD_DIM, (h + 1) * HEAD_DIM) for h in range(LANES // HEAD_DIM)]

    def body(*refs):
        x_in, x_out, sems = refs[5:5 + n_x], refs[8 + n_x:8 + 2 * n_x], refs[8 + 2 * n_x:]
        if n_x:
            @pl.when(_first_step(n_pairs, nq))
            def _():
                for cp in _exchange_copies(x_in, x_out, *sems):
                    cp.start()

        _sb_bwd_body(*refs[:5], *refs[5 + n_x:8 + n_x])
        if n_x:
            @pl.when(_last_step(n_pairs, nq))
            def _():
                for cp in _exchange_copies(x_in, x_out, *sems):
                    cp.wait()

    def _sb_bwd_body(q_ref, k_ref, v_ref, o_ref, do_ref, dq_ref, dk_ref, dv_ref):
        qi = pl.program_id(1)

        @pl.when(qi == 0)
        def _():
            dk_ref[...] = jnp.zeros_like(dk_ref)
            dv_ref[...] = jnp.zeros_like(dv_ref)

        _, tri = _sb_masks(t)
        diff = _col_minus_row(t)
        qs = [(q_ref[:, sl] * SCALE).astype(BF16) for sl in heads]
        dos = [do_ref[:, sl].astype(BF16) for sl in heads]
        dsums = [jnp.sum(do.astype(F32) * o_ref[:, sl], axis=1, keepdims=True) for do, sl in zip(dos, heads)]
        jd = qi // 2
        nh = len(heads)
        chains = [(sb, h) for sb in (1, 0) for h in range(nh)]

        def step(j, carry, masked):
            ks = [pl.multiple_of((2 * j + sb) * t, t) for sb, _ in chains]
            k_ = [k_ref[pl.ds(ks[i], t), heads[h]].astype(BF16) for i, (_, h) in enumerate(chains)]
            v_ = [v_ref[pl.ds(ks[i], t), heads[h]].astype(BF16) for i, (_, h) in enumerate(chains)]
            zs = [lax.dot_general(qs[h], k_[i], NT, preferred_element_type=F32) for i, (_, h) in enumerate(chains)]
            dws = [lax.dot_general(dos[h], v_[i], NT, preferred_element_type=F32) for i, (_, h) in enumerate(chains)]
            lbs, lms, betas, ombs, keeps = [], [], [], [], []
            for (sb, _), z in zip(chains, zs):
                ez = jnp.exp(-jnp.abs(z))
                lb = jnp.minimum(z, 0.0) - jnp.log(1.0 + ez)
                lm = lb - z
                betas.append(jnp.exp(lb))
                ombs.append(jnp.exp(lm))
                keep = None
                if masked:
                    keep = diff < (qi - 2 * j - sb) * t
                    lm = jnp.where(keep, lm, 0.0)
                lbs.append(lb)
                lms.append(lm)
                keeps.append(keep)
            css = [_split_dot(lm, tri, 2) for lm in lms]
            sums_f = [jnp.sum(lm, axis=1, keepdims=True) for lm in lms]
            wbs, es = [], []
            for i, (sb, h) in enumerate(chains):
                run_f = carry[h][1] if sb == 1 else carry[h][1] + sums_f[h]
                w = jnp.exp(lbs[i] + css[i] + run_f)
                if masked:
                    w = jnp.where(keeps[i], w, 0.0)
                wb = w.astype(BF16)
                wbs.append(wb)
                es.append(wb.astype(F32) * dws[i])
            for i, (_, h) in enumerate(chains):
                dv_ref[pl.ds(ks[i], t), heads[h]] += lax.dot_general(wbs[i], dos[h], TN, preferred_element_type=F32)
            ecs = [_split_dot(e, tri, 2) for e in es]
            sums_e = [jnp.sum(e, axis=1, keepdims=True) for e in es]
            dzbs = []
            for i, (sb, h) in enumerate(chains):
                run_e = carry[h][2] if sb == 1 else carry[h][2] + sums_e[h]
                left = dsums[h] - (ecs[i] + es[i] + run_e)
                dz = es[i] * ombs[i] - left * betas[i]
                if masked:
                    dz = jnp.where(keeps[i], dz, 0.0)
                dzbs.append(dz.astype(BF16))
            dqs = []
            for i, (_, h) in enumerate(chains):
                dk_ref[pl.ds(ks[i], t), heads[h]] += lax.dot_general(dzbs[i], qs[h], TN, preferred_element_type=F32)
                dqs.append(jnp.dot(dzbs[i], k_[i], preferred_element_type=F32))
            return tuple((carry[h][0] + dqs[h] + dqs[nh + h], carry[h][1] + sums_f[h] + sums_f[nh + h],
                          carry[h][2] + sums_e[h] + sums_e[nh + h]) for h in range(nh))

        zero = jnp.zeros((t, 1), F32)
        init = tuple((jnp.zeros((t, HEAD_DIM), F32), zero, zero) for _ in heads)
        carry = step(jd, init, True)
        carry = _sweep_left(step, jd, carry, lambda c: [run_f for _, run_f, _ in c])
        for (dq, _, _), sl in zip(carry, heads):
            dq_ref[:, sl] = dq * SCALE

    blk = pl.BlockSpec((t, LANES), lambda p, i: (i, p))
    full = pl.BlockSpec((s, LANES), lambda p, i: (0, p))
    shp = jax.ShapeDtypeStruct((s, n_pairs * LANES), F32)
    outs = pl.pallas_call(
        body, grid=(n_pairs, nq),
        in_specs=[blk,
                  pl.BlockSpec((s, LANES), lambda p, i: (0, n_pairs + p)),
                  pl.BlockSpec((s, LANES), lambda p, i: (0, 2 * n_pairs + p)),
                  blk, blk] + [HBM_SPEC] * n_x,
        out_specs=[blk, full, full] + [HBM_SPEC] * n_x,
        out_shape=[shp, shp, shp] + [jax.ShapeDtypeStruct((3,) + p.shape[1:], p.dtype) for p in exchange],
        scratch_shapes=_exchange_sems(n_x) if n_x else [],
        compiler_params=_params("arbitrary", "arbitrary"), name=name)(proj, proj, proj, o_sb, dmixed, *exchange)
    return outs[0], outs[1], outs[2], list(outs[3:])


def _group_mean_matrix():
    row = lax.broadcasted_iota(jnp.int32, (LANES, LANES), 0)
    col = lax.broadcasted_iota(jnp.int32, (LANES, LANES), 1)
    same_head = (row < HEAD_DIM) == (col < HEAD_DIM)
    return jnp.where(same_head, 1.0 / HEAD_DIM, 0.0).astype(BF16)


def _qk_norm(name, proj, gq2, gk2, n_pairs):
    s = proj.shape[0]
    ts = min(ROW_TILE, s)
    pb = PAD // ts
    c0 = 3 * n_pairs

    def body(q_ref, k_ref, v_ref, gq_ref, gk_ref, qn_ref, kn_ref, vp_ref):
        r_i = pl.program_id(1)

        @pl.when(r_i < pb)
        def _():
            qn_ref[...] = jnp.zeros_like(qn_ref)
            kn_ref[...] = jnp.zeros_like(kn_ref)
            vp_ref[...] = jnp.zeros_like(vp_ref)

        @pl.when(r_i >= pb)
        def _():
            gm = _group_mean_matrix()
            for src, g_ref, dst in ((q_ref, gq_ref, qn_ref), (k_ref, gk_ref, kn_ref)):
                xv = src[...]
                r = lax.rsqrt(_split_dot(xv * xv, gm, 3) + EPS)
                dst[...] = (xv * r * g_ref[...]).astype(BF16)
            vp_ref[...] = v_ref[...].astype(BF16)

    def in_spec(off):
        return pl.BlockSpec((ts, LANES), lambda p, i: (jnp.maximum(i - pb, 0), c0 + off * n_pairs + p))

    gspec = pl.BlockSpec((1, LANES), lambda p, i: (0, 0))
    ospec = pl.BlockSpec((ts, LANES), lambda p, i: (i, p))
    shp = jax.ShapeDtypeStruct((s + PAD, n_pairs * LANES), BF16)
    return pl.pallas_call(
        body, grid=(n_pairs, (s + PAD) // ts),
        in_specs=[in_spec(0), in_spec(1), in_spec(2), gspec, gspec],
        out_specs=[ospec, ospec, ospec], out_shape=[shp, shp, shp],
        compiler_params=_params("parallel", "arbitrary"), name=name)(proj, proj, proj, gq2, gk2)


def _qk_norm_bwd(name, dqn, dkn, proj, gq2, gk2, n_pairs):
    s = proj.shape[0]
    ts = min(ROW_TILE, s)
    pb = PAD // ts
    c0 = 3 * n_pairs
    n_r = s // ts

    def body(dqn_ref, dkn_ref, q_ref, k_ref, gq_ref, gk_ref, dq_ref, dk_ref, dgq_ref, dgk_ref):
        first = jnp.logical_and(pl.program_id(0) == 0, pl.program_id(1) == 0)
        last = jnp.logical_and(pl.program_id(0) == n_pairs - 1, pl.program_id(1) == n_r - 1)

        @pl.when(first)
        def _():
            dgq_ref[...] = jnp.zeros_like(dgq_ref)
            dgk_ref[...] = jnp.zeros_like(dgk_ref)

        gm = _group_mean_matrix()
        for dn_ref, x_ref, g_ref, dx_ref, dg_ref in ((dqn_ref, q_ref, gq_ref, dq_ref, dgq_ref),
                                                     (dkn_ref, k_ref, gk_ref, dk_ref, dgk_ref)):
            xv = x_ref[...]
            r = lax.rsqrt(_split_dot(xv * xv, gm, 3) + EPS)
            xhat = xv * r
            dn = dn_ref[...]
            tg = dn * g_ref[...]
            md = _split_dot(tg * xhat, gm, 3)
            dx_ref[...] = (r * (tg - xhat * md)).astype(BF16)
            dg_ref[...] += jnp.sum(dn * xhat, axis=0, keepdims=True)

        @pl.when(last)
        def _():
            for dg_ref in (dgq_ref, dgk_ref):
                gv = dg_ref[...]
                dg_ref[...] = gv + pltpu.roll(gv, HEAD_DIM, axis=1)

    def x_spec(off):
        return pl.BlockSpec((ts, LANES), lambda p, i: (i, c0 + off * n_pairs + p))

    dn_q = pl.BlockSpec((ts, LANES), lambda p, i: (i, p))
    dn_k = pl.BlockSpec((ts, LANES), lambda p, i: (i + pb, p))
    gspec = pl.BlockSpec((1, LANES), lambda p, i: (0, 0))
    ospec = pl.BlockSpec((ts, LANES), lambda p, i: (i, p))
    shp = jax.ShapeDtypeStruct((s, n_pairs * LANES), BF16)
    vec = jax.ShapeDtypeStruct((1, LANES), F32)
    return pl.pallas_call(
        body, grid=(n_pairs, n_r),
        in_specs=[dn_q, dn_k, x_spec(0), x_spec(1), gspec, gspec],
        out_specs=[ospec, ospec, gspec, gspec], out_shape=[shp, shp, vec, vec],
        compiler_params=_params("arbitrary", "arbitrary"), name=name)(dqn, dkn, proj, proj, gq2, gk2)


def _head_blocks(x2):
    lane = lax.broadcasted_iota(jnp.int32, x2.shape, 1)
    zero = jnp.zeros_like(x2)
    return jnp.concatenate([jnp.where(lane < HEAD_DIM, x2, zero), jnp.where(lane >= HEAD_DIM, x2, zero)], axis=0)


def _head_diag(r):
    lane = lax.broadcasted_iota(jnp.int32, (CHUNK, LANES), 1)
    return jnp.where(lane < HEAD_DIM, r[:CHUNK], r[CHUNK:])


def _ca_probs(kw, qb, bias_t, n):
    sc = lax.dot_general(kw, qb, NT, preferred_element_type=F32) * SCALE + bias_t
    jpos = lax.broadcasted_iota(jnp.int32, (BAND, LANES), 0)
    sc = jnp.where(n * CHUNK + jpos >= PAD, sc, NEG_INF)
    p = jnp.exp(sc - jnp.max(sc, axis=0, keepdims=True))
    return p * (1.0 / jnp.sum(p, axis=0, keepdims=True))


def _ca_fwd(name, qn, knp, vp, bias_tt, n_pairs):
    sp_rows = knp.shape[0]
    s = sp_rows - PAD
    t = min(CA_TILE, s)
    cpb = t // CHUNK
    pb = PAD // t

    def body(q_ref, k_ref, v_ref, b_ref, o_ref):
        qi = pl.program_id(1)
        wss = [pl.multiple_of((qi * cpb + ci) * CHUNK, CHUNK) for ci in range(cpb)]
        ps = [_ca_probs(k_ref[pl.ds(wss[ci], BAND), :], _head_blocks(q_ref[ci * CHUNK:(ci + 1) * CHUNK, :]),
                        b_ref[...], qi * cpb + ci) for ci in range(cpb)]
        for ci in range(cpb):
            r = lax.dot_general(ps[ci].astype(BF16), v_ref[pl.ds(wss[ci], BAND), :], TN, preferred_element_type=F32)
            o_ref[ci * CHUNK:(ci + 1) * CHUNK, :] = _head_diag(r)

    full = pl.BlockSpec((sp_rows, LANES), lambda p, i: (0, p))
    return pl.pallas_call(
        body, grid=(n_pairs, s // t),
        in_specs=[pl.BlockSpec((t, LANES), lambda p, i: (i + pb, p)), full, full,
                  pl.BlockSpec((None, BAND, LANES), lambda p, i: (p, 0, 0))],
        out_specs=pl.BlockSpec((t, LANES), lambda p, i: (i, p)),
        out_shape=jax.ShapeDtypeStruct((s, n_pairs * LANES), F32),
        compiler_params=_params("parallel", "arbitrary"), name=name)(qn, knp, vp, bias_tt)


def _ca_bwd(name, qn, knp, vp, bias_tt, dmixed, n_pairs, col0):
    sp_rows = knp.shape[0]
    s = sp_rows - PAD
    t = min(CA_TILE, s)
    cpb = t // CHUNK
    pb = PAD // t

    def body(q_ref, k_ref, v_ref, b_ref, do_ref, dq_ref, dk_ref, dv_ref, db_ref):
        qi = pl.program_id(1)

        @pl.when(qi == 0)
        def _():
            dk_ref[...] = jnp.zeros_like(dk_ref)
            dv_ref[...] = jnp.zeros_like(dv_ref)
            db_ref[...] = jnp.zeros_like(db_ref)

        wss = [pl.multiple_of((qi * cpb + ci) * CHUNK, CHUNK) for ci in range(cpb)]
        kws = [k_ref[pl.ds(ws, BAND), :] for ws in wss]
        qbs = [_head_blocks(q_ref[ci * CHUNK:(ci + 1) * CHUNK, :]) for ci in range(cpb)]
        dbs = [_head_blocks(do_ref[ci * CHUNK:(ci + 1) * CHUNK, :].astype(BF16)) for ci in range(cpb)]
        ps = [_ca_probs(kws[ci], qbs[ci], b_ref[...], qi * cpb + ci) for ci in range(cpb)]
        dps = [lax.dot_general(v_ref[pl.ds(wss[ci], BAND), :], dbs[ci], NT, preferred_element_type=F32)
               for ci in range(cpb)]
        dss = [p * (dp - jnp.sum(p * dp, axis=0, keepdims=True)) for p, dp in zip(ps, dps)]
        for ci in range(cpb):
            dsb = dss[ci].astype(BF16)
            dv_ref[pl.ds(wss[ci], BAND), :] += jnp.dot(ps[ci].astype(BF16), dbs[ci], preferred_element_type=F32)
            dk_ref[pl.ds(wss[ci], BAND), :] += jnp.dot(dsb, qbs[ci], preferred_element_type=F32) * SCALE
            r = lax.dot_general(dsb, kws[ci], TN, preferred_element_type=F32)
            dq_ref[ci * CHUNK:(ci + 1) * CHUNK, :] = _head_diag(r) * SCALE
        total = dss[0]
        for ci in range(1, cpb):
            total = total + dss[ci]
        db_ref[...] += total

    full_in = pl.BlockSpec((sp_rows, LANES), lambda p, i: (0, p))
    btile = pl.BlockSpec((None, BAND, LANES), lambda p, i: (p, 0, 0))
    pad_shape = jax.ShapeDtypeStruct((sp_rows, n_pairs * LANES), F32)
    return pl.pallas_call(
        body, grid=(n_pairs, s // t),
        in_specs=[pl.BlockSpec((t, LANES), lambda p, i: (i + pb, p)), full_in, full_in, btile,
                  pl.BlockSpec((t, LANES), lambda p, i: (i, col0 + p))],
        out_specs=[pl.BlockSpec((t, LANES), lambda p, i: (i, p)), full_in, full_in, btile],
        out_shape=[jax.ShapeDtypeStruct((s, n_pairs * LANES), F32), pad_shape, pad_shape,
                   jax.ShapeDtypeStruct(bias_tt.shape, F32)],
        compiler_params=_params("parallel", "arbitrary"), name=name)(qn, knp, vp, bias_tt, dmixed)


def _pair_transposed(tile):
    n_h = tile.shape[0]
    return jnp.transpose(tile.reshape(n_h // 2, 2, CHUNK, BAND), (0, 3, 1, 2)).reshape(n_h // 2, BAND, LANES)


def _query_major(tile_t):
    n_p = tile_t.shape[0]
    return jnp.transpose(tile_t.reshape(n_p, BAND, 2, CHUNK), (3, 0, 2, 1)).reshape(CHUNK, 2 * n_p, BAND)


def _bias_tile(rb):
    n_h = rb.shape[0]
    far = jnp.broadcast_to(rb[:, 2 * REL_CLIP:], (n_h, PAD + CHUNK - REL_CLIP))
    near = rb[:, REL_CLIP - (CHUNK - 1):2 * REL_CLIP][:, ::-1]
    ext = jnp.concatenate([far, near], axis=1)
    return jnp.stack([ext[:, CHUNK - 1 - i:CHUNK - 1 - i + BAND] for i in range(CHUNK)], axis=1)


def _rel_bias_grad(name, db_t):
    _, n_h, _ = db_t.shape
    n_out = 3 * LANES
    assert CHUNK == 64 and REL_CLIP == 128 and LEFT_CHUNKS == 8

    def body(z_ref, o_ref):
        a_idx = lax.broadcasted_iota(jnp.int32, (LANES, n_out), 0)
        r_idx = lax.broadcasted_iota(jnp.int32, (LANES, n_out), 1)
        out = jnp.zeros((n_h, n_out), F32)
        for mm, base in ((0, 191), (1, 255), (2, 319)):
            b = LEFT_CHUNKS - mm
            acc = jnp.zeros((n_h, LANES), F32)
            for i in range(CHUNK):
                row = z_ref[i, :, b * CHUNK:(b + 1) * CHUNK]
                rowp = jnp.concatenate([row, jnp.zeros((n_h, LANES - CHUNK), F32)], axis=1)
                acc = acc + pltpu.roll(rowp, CHUNK - 1 - i, axis=1)
            place = (r_idx == jnp.minimum(base - a_idx, 2 * REL_CLIP)).astype(BF16)
            out = out + _split_dot(acc, place, 3)
        far = jnp.zeros((n_h, 1), F32)
        for i in range(CHUNK):
            far = far + jnp.sum(z_ref[i, :, 0:(LEFT_CHUNKS - 2) * CHUNK], axis=1, keepdims=True)
        lane = lax.broadcasted_iota(jnp.int32, (n_h, n_out), 1)
        o_ref[...] = out + jnp.where(lane == 2 * REL_CLIP, far, 0.0)

    return pl.pallas_call(
        body, out_shape=jax.ShapeDtypeStruct((n_h, n_out), F32),
        in_specs=[pl.BlockSpec(memory_space=pltpu.VMEM)], out_specs=pl.BlockSpec(memory_space=pltpu.VMEM),
        name=name)(db_t)


def _mesh_pos():
    return lax.axis_index("x"), lax.axis_index("y"), lax.axis_index("c")


def _other_chips(x, y):
    return [(1 - x, y), (x, 1 - y), (1 - x, 1 - y)]


def _all_gather_small(name, blk):
    m_per, n = blk.shape

    def body(x_ref, out_ref, send_sems, recv_sems, local_sem):
        x, y, c = _mesh_pos()
        me, sibling = (x, y, c), (x, y, 1 - c)
        chips = _other_chips(x, y)

        def rows(px, py, pc):
            return out_ref.at[pl.ds((4 * px + 2 * py + pc) * m_per, m_per), :]

        def copy(k, block, to, src=None):
            return pltpu.make_async_remote_copy(
                src_ref=rows(*block) if src is None else src, dst_ref=rows(*block),
                send_sem=send_sems.at[k], recv_sem=recv_sems.at[k], device_id=to, device_id_type=MESH)

        mine = pltpu.make_async_copy(x_ref, rows(*me), local_sem)
        mine.start()
        first = [copy(0, me, sibling, src=x_ref)]
        first += [copy(1 + j, me, (*chip, c), src=x_ref) for j, chip in enumerate(chips)]
        for cp in first:
            cp.start()
        passed = [copy(4 + j, (*chip, c), sibling) for j, chip in enumerate(chips)]
        for j, chip in enumerate(chips):
            copy(1 + j, (*chip, c), me).wait_recv()
            passed[j].start()
        copy(0, sibling, me).wait_recv()
        for j, chip in enumerate(chips):
            copy(4 + j, (*chip, 1 - c), me).wait_recv()
        for cp in first + passed:
            cp.wait_send()
        mine.wait()

    return pl.pallas_call(
        body, out_shape=jax.ShapeDtypeStruct((N_DEV * m_per, n), blk.dtype),
        in_specs=[pl.BlockSpec(memory_space=pltpu.VMEM)], out_specs=pl.BlockSpec(memory_space=pltpu.VMEM),
        scratch_shapes=[pltpu.SemaphoreType.DMA((7,)), pltpu.SemaphoreType.DMA((7,)), pltpu.SemaphoreType.DMA],
        compiler_params=pltpu.CompilerParams(vmem_limit_bytes=VMEM_LIMIT), name=name)(blk)


HBM_SPEC = pl.BlockSpec(memory_space=pltpu.HBM)


def _gather_copies(bufs, send_sems, recv_sems, phase):
    x, y, c = _mesh_pos()
    my_j = 2 * x + y
    starts, lands = [], []
    for t, buf in enumerate(bufs):
        rh = buf.shape[2] // 2
        for k, (px, py) in enumerate(_other_chips(x, y)):
            their_j = 2 * px + py
            if phase == 0:
                src, to = buf.at[0, my_j, pl.ds(c * rh, rh), :], (px, py, c)
                land = buf.at[0, their_j, pl.ds(c * rh, rh), :]
            else:
                src, to = buf.at[0, their_j, pl.ds(c * rh, rh), :], (x, y, 1 - c)
                land = buf.at[0, their_j, pl.ds((1 - c) * rh, rh), :]
            sems = dict(send_sem=send_sems.at[phase, t, k], recv_sem=recv_sems.at[phase, t, k],
                        device_id=to, device_id_type=MESH)
            starts.append(pltpu.make_async_remote_copy(src_ref=src, dst_ref=src, **sems))
            lands.append(pltpu.make_async_remote_copy(src_ref=src, dst_ref=land, **sems))
    return starts, lands


def _gather_sems(n):
    return [pltpu.SemaphoreType.DMA((2, n, 3)), pltpu.SemaphoreType.DMA((2, n, 3))]


def _all_gather_weights(name, bufs, phases):
    n = len(bufs)

    def body(*refs):
        outs = refs[n:2 * n]
        send_sems, recv_sems = refs[2 * n:]
        for phase in phases:
            starts, lands = _gather_copies(outs, send_sems, recv_sems, phase)
            for cp in starts:
                cp.start()
            for cp, ld in zip(starts, lands):
                cp.wait_send()
                ld.wait_recv()

    return pl.pallas_call(
        body, out_shape=[jax.ShapeDtypeStruct(b.shape, b.dtype) for b in bufs],
        in_specs=[HBM_SPEC] * n, out_specs=[HBM_SPEC] * n, input_output_aliases={t: t for t in range(n)},
        scratch_shapes=_gather_sems(n), name=name)(*bufs)


def _gather_buffer(shard, chip):
    r, c = shard.shape
    return lax.dynamic_update_slice(jnp.zeros((1, N_CHIPS, r, c), BF16), shard.astype(BF16)[None, None],
                                    (0, chip, 0, 0))


def _exchange_sibling_halves(name, grads):
    n = len(grads)

    def body(*refs):
        srcs, outs = refs[:n], refs[n:2 * n]
        send_sems, recv_sems = refs[2 * n:]
        x, y, c = _mesh_pos()
        cps = []
        for t in range(n):
            rh = grads[t].shape[1] // 2
            cps.append(pltpu.make_async_remote_copy(
                src_ref=srcs[t].at[:, pl.ds((1 - c) * rh, rh), :], dst_ref=outs[t],
                send_sem=send_sems.at[t], recv_sem=recv_sems.at[t], device_id=(x, y, 1 - c), device_id_type=MESH))
        for cp in cps:
            cp.start()
        for cp in cps:
            cp.wait()

    out_shape = [jax.ShapeDtypeStruct((g.shape[0], g.shape[1] // 2, g.shape[2]), g.dtype) for g in grads]
    return pl.pallas_call(
        body, out_shape=out_shape, in_specs=[HBM_SPEC] * n, out_specs=[HBM_SPEC] * n,
        scratch_shapes=[pltpu.SemaphoreType.DMA((n,)), pltpu.SemaphoreType.DMA((n,))],
        name=name)(*grads)


def _add_halves(name, g, recv, c_idx):
    nb, r, c = g.shape
    rh = r // 2
    tr = rh
    for cand in (512, 256, 128, 64):
        if rh % cand == 0:
            tr = cand
            break
    g4 = g.reshape(nb, 2, rh, c)

    def body(c_ref, g_ref, r_ref, o_ref):
        o_ref[...] = (g_ref[...] + r_ref[...]).astype(BF16)

    grid_spec = pltpu.PrefetchScalarGridSpec(
        num_scalar_prefetch=1, grid=(nb, rh // tr),
        in_specs=[pl.BlockSpec((None, None, tr, c), lambda j, i, cr: (j, cr[0], i, 0)),
                  pl.BlockSpec((None, tr, c), lambda j, i, cr: (j, i, 0))],
        out_specs=pl.BlockSpec((None, tr, c), lambda j, i, cr: (j, i, 0)))
    return pl.pallas_call(
        body, grid_spec=grid_spec, out_shape=jax.ShapeDtypeStruct((nb, rh, c), BF16),
        compiler_params=_params("parallel", "parallel"), name=name)(c_idx, g4, recv)


def _exchange_copies(srcs, outs, send_sems, recv_sems):
    x, y, c = _mesh_pos()
    return [pltpu.make_async_remote_copy(
        src_ref=srcs[t].at[2 * px + py], dst_ref=outs[t].at[k], send_sem=send_sems.at[t, k],
        recv_sem=recv_sems.at[t, k], device_id=(px, py, c), device_id_type=MESH)
        for t in range(len(srcs)) for k, (px, py) in enumerate(_other_chips(x, y))]


def _exchange_sems(n):
    return [pltpu.SemaphoreType.DMA((n, 3)), pltpu.SemaphoreType.DMA((n, 3))]


def _exchange_chips(name, parts):
    n = len(parts)

    def body(*refs):
        cps = _exchange_copies(refs[:n], refs[n:2 * n], *refs[2 * n:])
        for cp in cps:
            cp.start()
        for cp in cps:
            cp.wait()

    return pl.pallas_call(
        body, out_shape=[jax.ShapeDtypeStruct((3,) + p.shape[1:], p.dtype) for p in parts],
        in_specs=[HBM_SPEC] * n, out_specs=[HBM_SPEC] * n, scratch_shapes=_exchange_sems(n), name=name)(*parts)


def _sum_chips(name, own_parts, recvs, pos_idx):
    n_layers = len(own_parts)
    _, rh, c = recvs[0].shape
    tr = rh
    for cand in (512, 256, 128, 64):
        if rh % cand == 0:
            tr = cand
            break
    nb = rh // tr

    def body(pos_ref, *refs):
        own_refs, r_refs, o_ref = refs[:n_layers], refs[n_layers:2 * n_layers], refs[2 * n_layers]
        for l_i in range(n_layers):
            @pl.when(pl.program_id(0) == l_i)
            def _():
                acc = own_refs[l_i][...].astype(F32)
                for k in range(3):
                    acc = acc + r_refs[l_i][k].astype(F32)
                o_ref[...] = acc

    grid_spec = pltpu.PrefetchScalarGridSpec(
        num_scalar_prefetch=1, grid=(n_layers, nb),
        in_specs=[pl.BlockSpec((None, tr, c), lambda l, i, pr: (pr[0], i, 0)) for _ in range(n_layers)]
        + [pl.BlockSpec((3, tr, c), lambda l, i, pr: (0, i, 0)) for _ in range(n_layers)],
        out_specs=pl.BlockSpec((None, tr, c), lambda l, i, pr: (l, pr[1] * nb + i, 0)))
    return pl.pallas_call(
        body, grid_spec=grid_spec, out_shape=jax.ShapeDtypeStruct((n_layers, 2 * rh, c), F32),
        compiler_params=_params("parallel", "parallel"), name=name)(pos_idx, *own_parts, *recvs)


def _share_with_sibling(name, grads):
    n = len(grads)

    def body(*refs):
        outs = refs[n:2 * n]
        send_sems, recv_sems = refs[2 * n:]
        x, y, c = _mesh_pos()
        cps, landed = [], []
        for t in range(n):
            rh = grads[t].shape[1] // 2
            mine = outs[t].at[:, pl.ds(c * rh, rh), :]
            theirs = outs[t].at[:, pl.ds((1 - c) * rh, rh), :]
            cps.append(pltpu.make_async_remote_copy(
                src_ref=mine, dst_ref=mine, send_sem=send_sems.at[t], recv_sem=recv_sems.at[t],
                device_id=(x, y, 1 - c), device_id_type=MESH))
            landed.append(pltpu.make_async_remote_copy(
                src_ref=mine, dst_ref=theirs, send_sem=send_sems.at[t], recv_sem=recv_sems.at[t],
                device_id=(x, y, 1 - c), device_id_type=MESH))
        for cp in cps:
            cp.start()
        for cp, ld in zip(cps, landed):
            cp.wait_send()
            ld.wait_recv()

    out_shape = [jax.ShapeDtypeStruct(g.shape, g.dtype) for g in grads]
    return pl.pallas_call(
        body, out_shape=out_shape, in_specs=[HBM_SPEC] * n, out_specs=[HBM_SPEC] * n,
        input_output_aliases={t: t for t in range(n)},
        scratch_shapes=[pltpu.SemaphoreType.DMA((n,)), pltpu.SemaphoreType.DMA((n,))],
        name=name)(*grads)


def _row(v):
    return v.reshape(1, -1)


def kernel(x, c, g_norm1, w_in, g_q, g_k, rel_bias, w_o, g_norm2, w1, w2, w_ada, b_ada, loss_target, m_g_norm1, m_w_in, m_g_q, m_g_k, m_rel_bias, m_w_o, m_g_norm2, m_w1, m_w2, m_w_ada, m_b_ada, v_g_norm1, v_w_in, v_g_q, v_g_k, v_rel_bias, v_w_o, v_g_norm2, v_w1, v_w2, v_w_ada, v_b_ada):
    _, s, d = x.shape
    n_layers = g_norm1.shape[0]
    n_pairs = (d // 2) // LANES
    n_ca_heads = rel_bias.shape[1]
    n_rel = rel_bias.shape[2]
    assert n_rel == 2 * REL_CLIP + 1 and g_q.shape[1] == HEAD_DIM and n_ca_heads == 2 * n_pairs
    ada_c = w_ada.shape[2]

    ax, ay, ac = _mesh_pos()
    chip = 2 * ax + ay
    dev = 4 * ax + 2 * ay + ac
    c_idx = jnp.reshape(ac, (1,)).astype(jnp.int32)

    c_act_all = _all_gather_small("ag_c", _silu_rows("silu_c", c)).reshape(N_DEV, 8, d)[:, 0, :]
    mod_parts = []
    for l in range(n_layers):
        bias = lax.dynamic_slice_in_dim(b_ada[l], chip * ada_c, ada_c).reshape(1, ada_c)
        mod_parts.append(_ada_mod(f"ada_mod{l}", c_act_all, w_ada[l], bias))
    mod_all = _all_gather_small("ag_mod", jnp.concatenate(mod_parts, axis=1))
    mod_all = mod_all.reshape(N_CHIPS, 2, N_DEV, n_layers, ada_c)[:, 0]
    mod_all = jnp.transpose(mod_all, (1, 2, 0, 3)).reshape(N_DEV, n_layers, N_CHIPS * ada_c)
    mod = lax.dynamic_index_in_dim(mod_all, dev, axis=0, keepdims=False)

    gathered = [[_gather_buffer(w[l], chip) for w in (w_in, w_o, w1, w2)] for l in range(n_layers)]
    gathered[0] = _all_gather_weights("ag_weights_0", gathered[0], (0, 1))

    xs = x[0]
    saved = []
    for l in range(n_layers):
        wg_in, wg_o, wg_1, wg_2 = gathered[l]
        sh1, sc1, gt1, sh2, sc2, gt2 = [_row(mod[l, i * d:(i + 1) * d]) for i in range(6)]
        gq2 = _row(jnp.tile(g_q[l], 2))
        gk2 = _row(jnp.tile(g_k[l], 2))
        bias_t = _pair_transposed(_bias_tile(rel_bias[l]))
        h1 = _norm_mod(f"norm1_{l}", xs, _row(g_norm1[l]), sc1, sh1)
        proj = _mm_nn_w(f"proj_{l}", h1, wg_in, 0, True, [F32])[0]
        if l + 1 < n_layers:
            o_sb, landed = _sb_fwd(f"sb_fwd_{l}", proj, n_pairs, gather=gathered[l + 1])
            gathered[l + 1] = _all_gather_weights(f"ag_weights_{l + 1}", landed, (1,))
        else:
            o_sb, _ = _sb_fwd(f"sb_fwd_{l}", proj, n_pairs)
        qn, knp, vp = _qk_norm(f"qk_norm_{l}", proj, gq2, gk2, n_pairs)
        o_ca = _ca_fwd(f"ca_fwd_{l}", qn, knp, vp, bias_t, n_pairs)
        mixed = jnp.concatenate([o_sb, o_ca], axis=1)

        def res_epi(acc, res, gt):
            return res + gt * acc, acc

        def res_specs(tm, tn):
            return (pl.BlockSpec((tm, tn), lambda i, j, kk: (i, j)), pl.BlockSpec((1, tn), lambda i, j, kk: (0, j)))

        x1, m1 = _mm_nn_w(f"attn_out_{l}", mixed, wg_o, 0, False, [F32, BF16], epi=res_epi,
                          extras=(xs, gt1), extra_specs_fn=res_specs)
        h2 = _norm_mod(f"norm2_{l}", x1, _row(g_norm2[l]), sc2, sh2)

        def act_epi(acc):
            r = jnp.maximum(acc, 0.0)
            return acc, r * r

        u, a = _mm_nn_w(f"mlp_up_{l}", h2, wg_1, 0, True, [BF16, BF16], epi=act_epi)
        x2, m2 = _mm_nn_w(f"mlp_down_{l}", a, wg_2, 0, False, [F32, BF16], epi=res_epi,
                          extras=(x1, gt2), extra_specs_fn=res_specs)
        saved.append(dict(x0=xs, h1=h1, proj=proj, o_sb=o_sb, qn=qn, knp=knp, vp=vp, bias_t=bias_t, mixed=mixed,
                          m1=m1, x1=x1, h2=h2, u=u, a=a, m2=m2, gq2=gq2, gk2=gk2,
                          sc1=sc1, gt1=gt1, sc2=sc2, gt2=gt2))
        xs = x2

    dx, sq = _loss_grad("loss", xs, loss_target[0])
    loss_part = (0.5 * jnp.sum(sq) / d).reshape(1)

    g_in, g_o, g_1, g_2 = [None] * n_layers, [None] * n_layers, [None] * n_layers, [None] * n_layers
    dmod, dg1, dg2, dgq, dgk, drel = [], [], [], [], [], []
    partial, chip_sum, from_chip = {}, {}, {}

    def reduce_on_chip(tag, keys):
        grads = [partial[k] for k in keys]
        for k, g, r in zip(keys, grads, _exchange_sibling_halves(f"rs_sibling_{tag}", grads)):
            chip_sum[k] = _add_halves(f"rs_add_{k[0]}_{k[1]}", g, r, c_idx)

    for l in reversed(range(n_layers)):
        sv = saved[l]
        wg_in, wg_o, wg_1, wg_2 = gathered[l]
        dm2, dgt2 = _gate_bwd(f"gate2_bwd_{l}", dx, sv["m2"], sv["gt2"])

        def act_bwd_epi(acc, u_t):
            return (acc * (2.0 * jnp.maximum(u_t.astype(F32), 0.0)),)

        def tile_specs(tm, tn):
            return (pl.BlockSpec((tm, tn), lambda i, j, kk: (i, j)),)

        du = _mm_nt_w(f"mlp_down_bwd_{l}", dm2, wg_2, 0, False, BF16, epi=act_bwd_epi, extras=(sv["u"],),
                      extra_specs_fn=tile_specs)
        g_2[l] = _mm_tn(f"w2_grad_{l}", sv["a"], dm2, False)
        dh2 = _mm_nt_w(f"mlp_up_bwd_{l}", du, wg_1, 0, True, F32)
        g_1[l] = _mm_tn(f"w1_grad_{l}", sv["h2"], du, True)
        dx1, dsh2, dsc2, dgn2 = _norm_mod_bwd(f"norm2_bwd_{l}", dh2, sv["x1"], _row(g_norm2[l]), sv["sc2"], dx)
        dm1, dgt1 = _gate_bwd(f"gate1_bwd_{l}", dx1, sv["m1"], sv["gt1"])
        dmixed = _mm_nt_w(f"attn_out_bwd_{l}", dm1, wg_o, 0, False, F32)
        g_o[l] = _mm_tn(f"wo_grad_{l}", sv["mixed"], dm1, False)
        partial.update({(1, l): g_o[l], (2, l): g_1[l], (3, l): g_2[l]})
        if l == 0:
            early = sorted(partial)
            reduce_on_chip("early", early)
            dq_sb, dk_sb, dv_sb, landed = _sb_bwd(f"sb_bwd_{l}", sv["proj"], sv["o_sb"], dmixed, n_pairs,
                                                  exchange=[chip_sum[k] for k in early])
            from_chip.update(zip(early, landed))
        else:
            dq_sb, dk_sb, dv_sb, _ = _sb_bwd(f"sb_bwd_{l}", sv["proj"], sv["o_sb"], dmixed, n_pairs)
        dqn, dknp, dvp, dbias_t = _ca_bwd(f"ca_bwd_{l}", sv["qn"], sv["knp"], sv["vp"], sv["bias_t"], dmixed,
                                          n_pairs, n_pairs)
        dq_ca, dk_ca, dgq_l, dgk_l = _qk_norm_bwd(f"qk_norm_bwd_{l}", dqn, dknp, sv["proj"], sv["gq2"], sv["gk2"],
                                                  n_pairs)
        drel_l = _rel_bias_grad(f"rel_bias_grad_{l}", _query_major(dbias_t))[:, :n_rel]
        dproj = jnp.concatenate([dq_sb.astype(BF16), dk_sb.astype(BF16), dv_sb.astype(BF16), dq_ca, dk_ca,
                                 dvp[PAD:].astype(BF16)], axis=1)
        g_in[l] = _mm_tn(f"win_grad_{l}", sv["h1"], dproj, True)
        partial[(0, l)] = g_in[l]
        dh1 = _mm_nt_w(f"proj_bwd_{l}", dproj, wg_in, 0, True, F32)
        dx, dsh1, dsc1, dgn1 = _norm_mod_bwd(f"norm1_bwd_{l}", dh1, sv["x0"], _row(g_norm1[l]), sv["sc1"], dx1)
        dmod.insert(0, jnp.concatenate([dsh1, dsc1, dgt1, dsh2, dsc2, dgt2], axis=1)[0])
        dg1.insert(0, dgn1[0])
        dg2.insert(0, dgn2[0])
        dgq.insert(0, dgq_l[0, :HEAD_DIM])
        dgk.insert(0, dgk_l[0, :HEAD_DIM])
        drel.insert(0, drel_l.reshape(-1))
    grad_x = dx[None]

    small = jnp.concatenate([jnp.concatenate(dmod), jnp.concatenate(dg1), jnp.concatenate(dg2),
                             jnp.concatenate(dgq), jnp.concatenate(dgk), jnp.concatenate(drel), loss_part])
    n_small = small.shape[0]
    n_pack = -(-n_small // (8 * LANES)) * (8 * LANES)
    small = jnp.pad(small, (0, n_pack - n_small)).reshape(8, n_pack // 8)
    small_all = _all_gather_small("ag_small", small).reshape(N_DEV, 8, n_pack // 8)
    small_sum = _sum_rows8("sum_small", small_all).reshape(-1)
    dmod_all = small_all.reshape(N_DEV, n_pack)[:, :n_layers * 6 * d].reshape(N_DEV, n_layers, 6 * d)

    off = 0

    def take(n, shape):
        nonlocal off
        out = small_sum[off:off + n].reshape(shape)
        off += n
        return out

    grad_b_ada = take(n_layers * 6 * d, (n_layers, 6 * d))
    grad_g_norm1 = take(n_layers * d, (n_layers, d))
    grad_g_norm2 = take(n_layers * d, (n_layers, d))
    grad_g_q = take(n_layers * HEAD_DIM, (n_layers, HEAD_DIM))
    grad_g_k = take(n_layers * HEAD_DIM, (n_layers, HEAD_DIM))
    grad_rel_bias = take(n_layers * n_ca_heads * n_rel, (n_layers, n_ca_heads, n_rel))
    loss = take(1, ())

    c_act_t = jnp.transpose(c_act_all)
    grad_w_ada = jnp.stack([
        _outer_sum(f"wada_grad_{l}", c_act_t,
                   lax.dynamic_slice_in_dim(dmod_all[:, l, :], chip * ada_c, ada_c, axis=1))
        for l in range(n_layers)])

    late = [k for k in sorted(partial) if k not in from_chip]
    reduce_on_chip("late", late)
    from_chip.update(zip(late, _exchange_chips("rs_chips", [chip_sum[k] for k in late])))
    pos_idx = jnp.stack([chip, ac]).astype(jnp.int32)
    reduced = [_sum_chips(f"rs_sum_{w_i}", [chip_sum[(w_i, l)] for l in range(n_layers)],
                          [from_chip[(w_i, l)] for l in range(n_layers)], pos_idx) for w_i in range(4)]
    grad_w_in, grad_w_o, grad_w1, grad_w2 = _share_with_sibling("rs_share", reduced)

    def adam_big(name, w, g, m, v):
        shp = w.shape
        outs = _adamw(name, w.reshape(-1, shp[-1]), g.reshape(-1, shp[-1]), m.reshape(-1, shp[-1]),
                      v.reshape(-1, shp[-1]))
        return [o.reshape(shp) for o in outs]

    def pack(arrs):
        flat = jnp.concatenate([a.reshape(-1) for a in arrs])
        n = flat.shape[0]
        n_p = -(-n // (8 * LANES)) * (8 * LANES)
        return jnp.pad(flat, (0, n_p - n), constant_values=1.0).reshape(8, n_p // 8)

    small_w = [g_norm1, g_q, g_k, rel_bias, g_norm2, b_ada]
    small_g = [grad_g_norm1, grad_g_q, grad_g_k, grad_rel_bias, grad_g_norm2, grad_b_ada]
    small_m = [m_g_norm1, m_g_q, m_g_k, m_rel_bias, m_g_norm2, m_b_ada]
    small_v = [v_g_norm1, v_g_q, v_g_k, v_rel_bias, v_g_norm2, v_b_ada]
    packed = _adamw("adamw_small", pack(small_w), pack(small_g), pack(small_m), pack(small_v))

    def unpack(p):
        flat = p.reshape(-1)
        res, o = [], 0
        for a in small_w:
            res.append(flat[o:o + a.size].reshape(a.shape))
            o += a.size
        return res

    sd, sm, sv_ = unpack(packed[0]), unpack(packed[1]), unpack(packed[2])
    big = {
        "w_in": adam_big("adamw_w_in", w_in, grad_w_in, m_w_in, v_w_in),
        "w_o": adam_big("adamw_w_o", w_o, grad_w_o, m_w_o, v_w_o),
        "w1": adam_big("adamw_w1", w1, grad_w1, m_w1, v_w1),
        "w2": adam_big("adamw_w2", w2, grad_w2, m_w2, v_w2),
        "w_ada": adam_big("adamw_w_ada", w_ada, grad_w_ada, m_w_ada, v_w_ada),
    }

    def ordered(kind):
        sm_list = (sd, sm, sv_)[kind]
        return [sm_list[0], big["w_in"][kind], sm_list[1], sm_list[2], sm_list[3], big["w_o"][kind], sm_list[4],
                big["w1"][kind], big["w2"][kind], big["w_ada"][kind], sm_list[5]]

    grads = [grad_g_norm1, grad_w_in, grad_g_q, grad_g_k, grad_rel_bias, grad_w_o, grad_g_norm2, grad_w1, grad_w2,
             grad_w_ada, grad_b_ada]
    return (loss, grad_x, *grads, *ordered(0), *ordered(1), *ordered(2))
```

```python
import functools

import jax
import jax.numpy as jnp
from jax import lax
from jax.experimental import pallas as pl
from jax.experimental.pallas import tpu as pltpu

F32 = jnp.float32
BF16 = jnp.bfloat16
MESH = pl.DeviceIdType.MESH

EPS = 1e-6
NEG_INF = -1e30
HEAD_DIM = 64
CHUNK = 64
LEFT_CHUNKS = 8
PAD = LEFT_CHUNKS * CHUNK
BAND = PAD + CHUNK
REL_CLIP = 128
SCALE = HEAD_DIM ** -0.5
DEAD_LOG = -88.0
LANES = 128
N_CHIPS = 4
N_DEV = 8

ADAM_LR = 0.001
ADAM_B1 = 0.9
ADAM_B2 = 0.999
ADAM_EPS = 1e-08
ADAM_WD = 0.01
ADAM_STEP = 10

VMEM_LIMIT = 48 * 1024 * 1024
ROW_TILE = 512
MM_TILE = 1024
SB_TILE = 256
CA_TILE = 256

NT = (((1,), (1,)), ((), ()))
TN = (((0,), (0,)), ((), ()))
NN = (((1,), (0,)), ((), ()))


def _tile(n, pref):
    best = None
    t = LANES
    while t <= min(n, pref):
        if n % t == 0:
            best = t
        t += LANES
    return best if best is not None else n


def _params(*sem):
    return pltpu.CompilerParams(dimension_semantics=sem, vmem_limit_bytes=VMEM_LIMIT)


def _split_dot(x, t, passes):
    acc = None
    r = x
    for i in range(passes):
        hi = r.astype(BF16)
        d = jnp.dot(hi, t, preferred_element_type=F32)
        acc = d if acc is None else acc + d
        if i + 1 < passes:
            r = r - hi.astype(F32)
    return acc


def _matmul(name, a, b, *, dims, grid, a_spec, b_spec, out_shapes, out_specs, acc_shape,
            epi=None, extras=(), extra_specs=()):
    nk = grid[2]
    n_ex = len(extras)
    n_out = len(out_shapes)

    def body(*refs):
        a_ref, b_ref = refs[0], refs[1]
        ex = refs[2:2 + n_ex]
        outs = refs[2 + n_ex:2 + n_ex + n_out]
        acc = refs[-1]
        k = pl.program_id(2)

        @pl.when(k == 0)
        def _():
            acc[...] = jnp.zeros_like(acc)

        acc[...] += lax.dot_general(a_ref[...].astype(BF16), b_ref[...].astype(BF16), dims,
                                    preferred_element_type=F32)

        @pl.when(k == nk - 1)
        def _():
            res = epi(acc[...], *[e[...] for e in ex]) if epi is not None else (acc[...],)
            for o, r in zip(outs, res):
                o[...] = r.astype(o.dtype)

    return pl.pallas_call(
        body, grid=grid, in_specs=[a_spec, b_spec, *extra_specs], out_specs=out_specs,
        out_shape=out_shapes, scratch_shapes=[pltpu.VMEM(acc_shape, F32)],
        compiler_params=_params("parallel", "parallel", "arbitrary"), name=name,
    )(a, b, *extras)


def _mm_nn_w(name, a, wg, layer, colshard, out_dtypes, epi=None, extras=(), extra_specs_fn=None):
    m = a.shape[0]
    _, _, r, c = wg.shape
    tm = min(MM_TILE, m)
    if colshard:
        k, n = r, N_CHIPS * c
        tk, tn = k, _tile(c, MM_TILE)
        npc = c // tn
        b_spec = pl.BlockSpec((None, None, tk, tn), lambda i, j, kk: (layer, j // npc, kk, j % npc))
    else:
        k, n = N_CHIPS * r, c
        tk, tn = _tile(r, MM_TILE), _tile(c, MM_TILE)
        kpc = r // tk
        b_spec = pl.BlockSpec((None, None, tk, tn), lambda i, j, kk: (layer, kk // kpc, kk % kpc, j))
    assert a.shape[1] == k
    grid = (m // tm, n // tn, k // tk)
    out_spec = pl.BlockSpec((tm, tn), lambda i, j, kk: (i, j))
    ex_specs = extra_specs_fn(tm, tn) if extra_specs_fn is not None else ()
    return _matmul(name, a, wg, dims=NN, grid=grid,
                   a_spec=pl.BlockSpec((tm, tk), lambda i, j, kk: (i, kk)), b_spec=b_spec,
                   out_shapes=[jax.ShapeDtypeStruct((m, n), d) for d in out_dtypes],
                   out_specs=[out_spec for _ in out_dtypes], acc_shape=(tm, tn),
                   epi=epi, extras=extras, extra_specs=ex_specs)


def _mm_nt_w(name, a, wg, layer, colshard, out_dtype, epi=None, extras=(), extra_specs_fn=None):
    m = a.shape[0]
    _, _, r, c = wg.shape
    tm = min(MM_TILE, m)
    if colshard:
        kdim, n = N_CHIPS * c, r
        tk, tn = _tile(c, MM_TILE), _tile(r, MM_TILE)
        kpc = c // tk
        b_spec = pl.BlockSpec((None, None, tn, tk), lambda i, j, kk: (layer, kk // kpc, j, kk % kpc))
    else:
        kdim, n = c, N_CHIPS * r
        tk, tn = c, _tile(r, MM_TILE)
        npc = r // tn
        b_spec = pl.BlockSpec((None, None, tn, tk), lambda i, j, kk: (layer, j // npc, j % npc, kk))
    assert a.shape[1] == kdim
    grid = (m // tm, n // tn, kdim // tk)
    ex_specs = extra_specs_fn(tm, tn) if extra_specs_fn is not None else ()
    return _matmul(name, a, wg, dims=NT, grid=grid,
                   a_spec=pl.BlockSpec((tm, tk), lambda i, j, kk: (i, kk)), b_spec=b_spec,
                   out_shapes=[jax.ShapeDtypeStruct((m, n), out_dtype)],
                   out_specs=[pl.BlockSpec((tm, tn), lambda i, j, kk: (i, j))], acc_shape=(tm, tn),
                   epi=epi, extras=extras, extra_specs=ex_specs)[0]


def _mm_tn(name, a, b, colshard):
    s, m = a.shape
    n = b.shape[1]
    tk = min(2 * MM_TILE, s)
    if colshard:
        c = n // N_CHIPS
        tm, tn = _tile(m, MM_TILE // 2), _tile(c, MM_TILE)
        npc = c // tn
        out_shape = (N_CHIPS, m, c)
        out_spec = pl.BlockSpec((None, tm, tn), lambda i, j, kk: (j // npc, i, j % npc))
    else:
        r = m // N_CHIPS
        tm, tn = _tile(r, MM_TILE // 2), _tile(n, MM_TILE)
        mpc = r // tm
        out_shape = (N_CHIPS, r, n)
        out_spec = pl.BlockSpec((None, tm, tn), lambda i, j, kk: (i // mpc, i % mpc, j))
    grid = (m // tm, n // tn, s // tk)
    return _matmul(name, a, b, dims=TN, grid=grid,
                   a_spec=pl.BlockSpec((tk, tm), lambda i, j, kk: (kk, i)),
                   b_spec=pl.BlockSpec((tk, tn), lambda i, j, kk: (kk, j)),
                   out_shapes=[jax.ShapeDtypeStruct(out_shape, F32)], out_specs=[out_spec],
                   acc_shape=(tm, tn))[0]


def _row_spec(ts, d):
    return pl.BlockSpec((ts, d), lambda i: (i, 0))


def _vec_spec(d):
    return pl.BlockSpec((1, d), lambda i: (0, 0))


def _norm_mod(name, x, g, sc, sh):
    s, d = x.shape
    ts = min(ROW_TILE, s)

    def body(x_ref, g_ref, sc_ref, sh_ref, h_ref):
        xv = x_ref[...]
        r = lax.rsqrt(jnp.mean(xv * xv, axis=-1, keepdims=True) + EPS)
        h_ref[...] = (xv * r * g_ref[...] * (1.0 + sc_ref[...]) + sh_ref[...]).astype(BF16)

    return pl.pallas_call(
        body, grid=(s // ts,), in_specs=[_row_spec(ts, d), _vec_spec(d), _vec_spec(d), _vec_spec(d)],
        out_specs=_row_spec(ts, d), out_shape=jax.ShapeDtypeStruct((s, d), BF16),
        compiler_params=_params("parallel"), name=name)(x, g, sc, sh)


def _norm_mod_bwd(name, dh, x, g, sc, dres):
    s, d = x.shape
    ts = min(ROW_TILE, s)

    def body(dh_ref, x_ref, g_ref, sc_ref, dres_ref, dx_ref, dsh_ref, dsc_ref, dg_ref):
        @pl.when(pl.program_id(0) == 0)
        def _():
            dsh_ref[...] = jnp.zeros_like(dsh_ref)
            dsc_ref[...] = jnp.zeros_like(dsc_ref)
            dg_ref[...] = jnp.zeros_like(dg_ref)

        xv = x_ref[...]
        r = lax.rsqrt(jnp.mean(xv * xv, axis=-1, keepdims=True) + EPS)
        xhat = xv * r
        dhv = dh_ref[...]
        gv = g_ref[...]
        opsc = 1.0 + sc_ref[...]
        dxhat = dhv * (gv * opsc)
        mdot = jnp.mean(dxhat * xhat, axis=-1, keepdims=True)
        dx_ref[...] = dres_ref[...] + r * (dxhat - xhat * mdot)
        dhx = dhv * xhat
        dsh_ref[...] += jnp.sum(dhv, axis=0, keepdims=True)
        dsc_ref[...] += jnp.sum(dhx * gv, axis=0, keepdims=True)
        dg_ref[...] += jnp.sum(dhx * opsc, axis=0, keepdims=True)

    vec = jax.ShapeDtypeStruct((1, d), F32)
    return pl.pallas_call(
        body, grid=(s // ts,),
        in_specs=[_row_spec(ts, d), _row_spec(ts, d), _vec_spec(d), _vec_spec(d), _row_spec(ts, d)],
        out_specs=[_row_spec(ts, d), _vec_spec(d), _vec_spec(d), _vec_spec(d)],
        out_shape=[jax.ShapeDtypeStruct((s, d), F32), vec, vec, vec],
        compiler_params=_params("arbitrary"), name=name)(dh, x, g, sc, dres)


def _gate_bwd(name, dx, m, gt):
    s, d = dx.shape
    ts = min(ROW_TILE, s)

    def body(dx_ref, m_ref, gt_ref, dm_ref, dgt_ref):
        @pl.when(pl.program_id(0) == 0)
        def _():
            dgt_ref[...] = jnp.zeros_like(dgt_ref)

        dxv = dx_ref[...]
        dm_ref[...] = (dxv * gt_ref[...]).astype(BF16)
        dgt_ref[...] += jnp.sum(dxv * m_ref[...].astype(F32), axis=0, keepdims=True)

    return pl.pallas_call(
        body, grid=(s // ts,), in_specs=[_row_spec(ts, d), _row_spec(ts, d), _vec_spec(d)],
        out_specs=[_row_spec(ts, d), _vec_spec(d)],
        out_shape=[jax.ShapeDtypeStruct((s, d), BF16), jax.ShapeDtypeStruct((1, d), F32)],
        compiler_params=_params("arbitrary"), name=name)(dx, m, gt)


def _loss_grad(name, y, target):
    s, d = y.shape
    ts = min(ROW_TILE, s)

    def body(y_ref, t_ref, dy_ref, sq_ref):
        @pl.when(pl.program_id(0) == 0)
        def _():
            sq_ref[...] = jnp.zeros_like(sq_ref)

        e = y_ref[...] - t_ref[...]
        dy_ref[...] = e * (1.0 / d)
        sq_ref[...] += jnp.sum(e * e, axis=0, keepdims=True)

    return pl.pallas_call(
        body, grid=(s // ts,), in_specs=[_row_spec(ts, d), _row_spec(ts, d)],
        out_specs=[_row_spec(ts, d), _vec_spec(d)],
        out_shape=[jax.ShapeDtypeStruct((s, d), F32), jax.ShapeDtypeStruct((1, d), F32)],
        compiler_params=_params("arbitrary"), name=name)(y, target)


def _adamw(name, w, g, m, v):
    r, c = w.shape
    tr = r
    for cand in (512, 256, 128, 64, 32, 16, 8):
        if r % cand == 0 and cand * c * 4 <= 2 * 1024 * 1024:
            tr = cand
            break

    def body(w_ref, g_ref, m_ref, v_ref, d_ref, mo_ref, vo_ref):
        gv = g_ref[...]
        m2 = ADAM_B1 * m_ref[...] + (1.0 - ADAM_B1) * gv
        v2 = ADAM_B2 * v_ref[...] + (1.0 - ADAM_B2) * (gv * gv)
        m_hat = m2 / (1.0 - ADAM_B1 ** ADAM_STEP)
        v_hat = v2 / (1.0 - ADAM_B2 ** ADAM_STEP)
        d_ref[...] = -ADAM_LR * (m_hat / (jnp.sqrt(v_hat) + ADAM_EPS) + ADAM_WD * w_ref[...])
        mo_ref[...] = m2
        vo_ref[...] = v2

    spec = pl.BlockSpec((tr, c), lambda i: (i, 0))
    shp = jax.ShapeDtypeStruct((r, c), F32)
    return pl.pallas_call(
        body, grid=(r // tr,), in_specs=[spec] * 4, out_specs=[spec] * 3, out_shape=[shp] * 3,
        compiler_params=_params("parallel"), name=name)(w, g, m, v)


def _sum_rows8(name, x):
    n, r, c = x.shape

    def body(x_ref, o_ref):
        acc = x_ref[0]
        for i in range(1, n):
            acc = acc + x_ref[i]
        o_ref[...] = acc

    return pl.pallas_call(
        body, out_shape=jax.ShapeDtypeStruct((r, c), F32),
        in_specs=[pl.BlockSpec(memory_space=pltpu.VMEM)], out_specs=pl.BlockSpec(memory_space=pltpu.VMEM),
        name=name)(x)


def _outer_sum(name, ct, dm):
    d, n_seq = ct.shape
    n = dm.shape[1]
    tr = min(256, d)

    def body(ct_ref, dm_ref, o_ref):
        acc = ct_ref[:, 0:1] * dm_ref[0:1, :]
        for s_i in range(1, n_seq):
            acc = acc + ct_ref[:, s_i:s_i + 1] * dm_ref[s_i:s_i + 1, :]
        o_ref[...] = acc

    return pl.pallas_call(
        body, grid=(d // tr,),
        in_specs=[pl.BlockSpec((tr, n_seq), lambda i: (i, 0)), pl.BlockSpec((n_seq, n), lambda i: (0, 0))],
        out_specs=pl.BlockSpec((tr, n), lambda i: (i, 0)), out_shape=jax.ShapeDtypeStruct((d, n), F32),
        compiler_params=_params("parallel"), name=name)(ct, dm)


def _ada_mod(name, c_act, w_ada_l, bias):
    n_seq, d = c_act.shape
    n = w_ada_l.shape[1]
    tn = _tile(n, 768)

    def epi(acc, b):
        return (acc + b,)

    return _matmul(name, c_act, w_ada_l, dims=NN, grid=(1, n // tn, 1),
                   a_spec=pl.BlockSpec((n_seq, d), lambda i, j, kk: (0, 0)),
                   b_spec=pl.BlockSpec((d, tn), lambda i, j, kk: (0, j)),
                   out_shapes=[jax.ShapeDtypeStruct((n_seq, n), F32)],
                   out_specs=[pl.BlockSpec((n_seq, tn), lambda i, j, kk: (0, j))], acc_shape=(n_seq, tn),
                   epi=epi, extras=(bias,), extra_specs=(pl.BlockSpec((1, tn), lambda i, j, kk: (0, j)),))[0]


def _silu_rows(name, c_row):
    d = c_row.shape[1]

    def body(c_ref, o_ref):
        cv = c_ref[...]
        o_ref[...] = jnp.broadcast_to(cv / (1.0 + jnp.exp(-cv)), (8, d))

    return pl.pallas_call(
        body, out_shape=jax.ShapeDtypeStruct((8, d), F32),
        in_specs=[pl.BlockSpec(memory_space=pltpu.VMEM)], out_specs=pl.BlockSpec(memory_space=pltpu.VMEM),
        name=name)(c_row)


def _sb_masks(t):
    row = lax.broadcasted_iota(jnp.int32, (t, t), 0)
    col = lax.broadcasted_iota(jnp.int32, (t, t), 1)
    strict = col < row
    tri = (row > col).astype(BF16)
    return strict, tri


def _col_minus_row(t):
    return lax.broadcasted_iota(jnp.int32, (t, t), 1) - lax.broadcasted_iota(jnp.int32, (t, t), 0)


def _sweep_left(step, jd, carry, runs_of):
    def alive(c):
        runs = runs_of(c)
        top = runs[0]
        for r in runs[1:]:
            top = jnp.maximum(top, r)
        return jnp.max(top)

    def cond(state):
        j, _, top = state
        return jnp.logical_and(j >= 0, top > DEAD_LOG)

    def body(state):
        j, c, _ = state
        c = step(j, c, False)
        return j - 1, c, alive(c)

    return lax.while_loop(cond, body, (jd - 1, carry, alive(carry)))[1]


def _log_sigmoids(z):
    sp = jnp.log(1.0 + jnp.exp(-jnp.abs(z)))
    return jnp.minimum(z, 0.0) - sp, jnp.minimum(-z, 0.0) - sp, sp


def _first_step(n0, n1):
    return jnp.logical_and(pl.program_id(0) == 0, pl.program_id(1) == 0)


def _last_step(n0, n1):
    return jnp.logical_and(pl.program_id(0) == n0 - 1, pl.program_id(1) == n1 - 1)


def _sb_fwd(name, proj, n_pairs, gather=()):
    s = proj.shape[0]
    t = min(SB_TILE, s)
    nq = s // t
    n_g = len(gather)

    heads = [slice(h * HEAD_DIM, (h + 1) * HEAD_DIM) for h in range(LANES // HEAD_DIM)]

    assert nq % 2 == 0

    def body(*refs):
        q_ref, k_ref, v_ref = refs[:3]
        o_ref = refs[3 + n_g]
        g_refs, sems = refs[4 + n_g:4 + 2 * n_g], refs[4 + 2 * n_g:]
        if n_g:
            @pl.when(_first_step(n_pairs, nq))
            def _():
                for cp in _gather_copies(g_refs, *sems, 0)[0]:
                    cp.start()

        _sb_fwd_body(q_ref, k_ref, v_ref, o_ref)
        if n_g:
            @pl.when(_last_step(n_pairs, nq))
            def _():
                for cp, ld in zip(*_gather_copies(g_refs, *sems, 0)):
                    cp.wait_send()
                    ld.wait_recv()

    def _sb_fwd_body(q_ref, k_ref, v_ref, o_ref):
        qi = pl.program_id(1)
        _, tri = _sb_masks(t)
        diff = _col_minus_row(t)
        qs = [(q_ref[:, sl] * SCALE).astype(BF16) for sl in heads]
        jd = qi // 2
        chains = [(sb, h) for sb in (1, 0) for h in range(len(heads))]

        def step(j, carry, masked):
            ks = [pl.multiple_of((2 * j + sb) * t, t) for sb, _ in chains]
            k_ = [k_ref[pl.ds(ks[i], t), heads[h]].astype(BF16) for i, (_, h) in enumerate(chains)]
            v_ = [v_ref[pl.ds(ks[i], t), heads[h]].astype(BF16) for i, (_, h) in enumerate(chains)]
            zs = [lax.dot_general(qs[h], k_[i], NT, preferred_element_type=F32) for i, (_, h) in enumerate(chains)]
            lbs, lms, keeps = [], [], []
            for (sb, _), z in zip(chains, zs):
                lb, lm, _ = _log_sigmoids(z)
                keep = None
                if masked:
                    keep = diff < (qi - 2 * j - sb) * t
                    lm = jnp.where(keep, lm, 0.0)
                lbs.append(lb)
                lms.append(lm)
                keeps.append(keep)
            css = [_split_dot(lm, tri, 2) for lm in lms]
            sums = [jnp.sum(lm, axis=1, keepdims=True) for lm in lms]
            ws = []
            for i, (sb, h) in enumerate(chains):
                run = carry[h][1] if sb == 1 else carry[h][1] + sums[h]
                w = jnp.exp(lbs[i] + css[i] + run)
                if masked:
                    w = jnp.where(keeps[i], w, 0.0)
                ws.append(w.astype(BF16))
            pvs = [jnp.dot(w, v, preferred_element_type=F32) for w, v in zip(ws, v_)]
            nh = len(heads)
            return tuple((carry[h][0] + pvs[h] + pvs[nh + h], carry[h][1] + sums[h] + sums[nh + h]) for h in range(nh))

        init = tuple((jnp.zeros((t, HEAD_DIM), F32), jnp.zeros((t, 1), F32)) for _ in heads)
        carry = step(jd, init, True)
        carry = _sweep_left(step, jd, carry, lambda c: [run for _, run in c])
        for (acc, _), sl in zip(carry, heads):
            o_ref[:, sl] = acc

    outs = pl.pallas_call(
        body, grid=(n_pairs, nq),
        in_specs=[pl.BlockSpec((t, LANES), lambda p, i: (i, p)),
                  pl.BlockSpec((s, LANES), lambda p, i: (0, n_pairs + p)),
                  pl.BlockSpec((s, LANES), lambda p, i: (0, 2 * n_pairs + p))] + [HBM_SPEC] * n_g,
        out_specs=[pl.BlockSpec((t, LANES), lambda p, i: (i, p))] + [HBM_SPEC] * n_g,
        out_shape=[jax.ShapeDtypeStruct((s, 2 * n_pairs * LANES), F32)]
        + [jax.ShapeDtypeStruct(g.shape, g.dtype) for g in gather],
        input_output_aliases={3 + i: 1 + i for i in range(n_g)},
        scratch_shapes=_gather_sems(n_g) if n_g else [],
        compiler_params=_params("arbitrary", "arbitrary"), name=name)(proj, proj, proj, *gather)
    return outs[0], list(outs[1:])


def _sb_bwd(name, proj, o_sb, dmixed, n_pairs, exchange=()):
    s = proj.shape[0]
    t = min(SB_TILE, s)
    nq = s // t
    n_x = len(exchange)
    heads = [slice(h * HEAD_DIM, (h + 1) * HEAD_DIM) for h in range(LANES // HEAD_DIM)]

    def body(*refs):
        x_in, x_out, sems = refs[5:5 + n_x], refs[8 + n_x:8 + 2 * n_x], refs[8 + 2 * n_x:]
        if n_x:
            @pl.when(_first_step(n_pairs, nq))
            def _():
                for cp in _exchange_copies(x_in, x_out, *sems):
                    cp.start()

        _sb_bwd_body(*refs[:5], *refs[5 + n_x:8 + n_x])
        if n_x:
            @pl.when(_last_step(n_pairs, nq))
            def _():
                for cp in _exchange_copies(x_in, x_out, *sems):
                    cp.wait()

    def _sb_bwd_body(q_ref, k_ref, v_ref, o_ref, do_ref, dq_ref, dk_ref, dv_ref):
        qi = pl.program_id(1)

        @pl.when(qi == 0)
        def _():
            dk_ref[...] = jnp.zeros_like(dk_ref)
            dv_ref[...] = jnp.zeros_like(dv_ref)

        _, tri = _sb_masks(t)
        diff = _col_minus_row(t)
        qs = [(q_ref[:, sl] * SCALE).astype(BF16) for sl in heads]
        dos = [do_ref[:, sl].astype(BF16) for sl in heads]
        dsums = [jnp.sum(do.astype(F32) * o_ref[:, sl], axis=1, keepdims=True) for do, sl in zip(dos, heads)]
        jd = qi // 2
        nh = len(heads)
        chains = [(sb, h) for sb in (1, 0) for h in range(nh)]

        def step(j, carry, masked):
            ks = [pl.multiple_of((2 * j + sb) * t, t) for sb, _ in chains]
            k_ = [k_ref[pl.ds(ks[i], t), heads[h]].astype(BF16) for i, (_, h) in enumerate(chains)]
            v_ = [v_ref[pl.ds(ks[i], t), heads[h]].astype(BF16) for i, (_, h) in enumerate(chains)]
            zs = [lax.dot_general(qs[h], k_[i], NT, preferred_element_type=F32) for i, (_, h) in enumerate(chains)]
            dws = [lax.dot_general(dos[h], v_[i], NT, preferred_element_type=F32) for i, (_, h) in enumerate(chains)]
            lbs, lms, betas, ombs, keeps = [], [], [], [], []
            for (sb, _), z in zip(chains, zs):
                ez = jnp.exp(-jnp.abs(z))
                lb = jnp.minimum(z, 0.0) - jnp.log(1.0 + ez)
                lm = lb - z
                betas.append(jnp.exp(lb))
                ombs.append(jnp.exp(lm))
                keep = None
                if masked:
                    keep = diff < (qi - 2 * j - sb) * t
                    lm = jnp.where(keep, lm, 0.0)
                lbs.append(lb)
                lms.append(lm)
                keeps.append(keep)
            css = [_split_dot(lm, tri, 2) for lm in lms]
            sums_f = [jnp.sum(lm, axis=1, keepdims=True) for lm in lms]
            wbs, es = [], []
            for i, (sb, h) in enumerate(chains):
                run_f = carry[h][1] if sb == 1 else carry[h][1] + sums_f[h]
                w = jnp.exp(lbs[i] + css[i] + run_f)
                if masked:
                    w = jnp.where(keeps[i], w, 0.0)
                wb = w.astype(BF16)
                wbs.append(wb)
                es.append(wb.astype(F32) * dws[i])
            for i, (_, h) in enumerate(chains):
                dv_ref[pl.ds(ks[i], t), heads[h]] += lax.dot_general(wbs[i], dos[h], TN, preferred_element_type=F32)
            ecs = [_split_dot(e, tri, 2) for e in es]
            sums_e = [jnp.sum(e, axis=1, keepdims=True) for e in es]
            dzbs = []
            for i, (sb, h) in enumerate(chains):
                run_e = carry[h][2] if sb == 1 else carry[h][2] + sums_e[h]
                left = dsums[h] - (ecs[i] + es[i] + run_e)
                dz = es[i] * ombs[i] - left * betas[i]
                if masked:
                    dz = jnp.where(keeps[i], dz, 0.0)
                dzbs.append(dz.astype(BF16))
            dqs = []
            for i, (_, h) in enumerate(chains):
                dk_ref[pl.ds(ks[i], t), heads[h]] += lax.dot_general(dzbs[i], qs[h], TN, preferred_element_type=F32)
                dqs.append(jnp.dot(dzbs[i], k_[i], preferred_element_type=F32))
            return tuple((carry[h][0] + dqs[h] + dqs[nh + h], carry[h][1] + sums_f[h] + sums_f[nh + h],
                          carry[h][2] + sums_e[h] + sums_e[nh + h]) for h in range(nh))

        zero = jnp.zeros((t, 1), F32)
        init = tuple((jnp.zeros((t, HEAD_DIM), F32), zero, zero) for _ in heads)
        carry = step(jd, init, True)
        carry = _sweep_left(step, jd, carry, lambda c: [run_f for _, run_f, _ in c])
        for (dq, _, _), sl in zip(carry, heads):
            dq_ref[:, sl] = dq * SCALE

    blk = pl.BlockSpec((t, LANES), lambda p, i: (i, p))
    full = pl.BlockSpec((s, LANES), lambda p, i: (0, p))
    shp = jax.ShapeDtypeStruct((s, n_pairs * LANES), F32)
    outs = pl.pallas_call(
        body, grid=(n_pairs, nq),
        in_specs=[blk,
                  pl.BlockSpec((s, LANES), lambda p, i: (0, n_pairs + p)),
                  pl.BlockSpec((s, LANES), lambda p, i: (0, 2 * n_pairs + p)),
                  blk, blk] + [HBM_SPEC] * n_x,
        out_specs=[blk, full, full] + [HBM_SPEC] * n_x,
        out_shape=[shp, shp, shp] + [jax.ShapeDtypeStruct((3,) + p.shape[1:], p.dtype) for p in exchange],
        scratch_shapes=_exchange_sems(n_x) if n_x else [],
        compiler_params=_params("arbitrary", "arbitrary"), name=name)(proj, proj, proj, o_sb, dmixed, *exchange)
    return outs[0], outs[1], outs[2], list(outs[3:])


def _group_mean_matrix():
    row = lax.broadcasted_iota(jnp.int32, (LANES, LANES), 0)
    col = lax.broadcasted_iota(jnp.int32, (LANES, LANES), 1)
    same_head = (row < HEAD_DIM) == (col < HEAD_DIM)
    return jnp.where(same_head, 1.0 / HEAD_DIM, 0.0).astype(BF16)


def _qk_norm(name, proj, gq2, gk2, n_pairs):
    s = proj.shape[0]
    ts = min(ROW_TILE, s)
    pb = PAD // ts
    c0 = 3 * n_pairs

    def body(q_ref, k_ref, v_ref, gq_ref, gk_ref, qn_ref, kn_ref, vp_ref):
        r_i = pl.program_id(1)

        @pl.when(r_i < pb)
        def _():
            qn_ref[...] = jnp.zeros_like(qn_ref)
            kn_ref[...] = jnp.zeros_like(kn_ref)
            vp_ref[...] = jnp.zeros_like(vp_ref)

        @pl.when(r_i >= pb)
        def _():
            gm = _group_mean_matrix()
            for src, g_ref, dst in ((q_ref, gq_ref, qn_ref), (k_ref, gk_ref, kn_ref)):
                xv = src[...]
                r = lax.rsqrt(_split_dot(xv * xv, gm, 2) + EPS)
                dst[...] = (xv * r * g_ref[...]).astype(BF16)
            vp_ref[...] = v_ref[...].astype(BF16)

    def in_spec(off):
        return pl.BlockSpec((ts, LANES), lambda p, i: (jnp.maximum(i - pb, 0), c0 + off * n_pairs + p))

    gspec = pl.BlockSpec((1, LANES), lambda p, i: (0, 0))
    ospec = pl.BlockSpec((ts, LANES), lambda p, i: (i, p))
    shp = jax.ShapeDtypeStruct((s + PAD, n_pairs * LANES), BF16)
    return pl.pallas_call(
        body, grid=(n_pairs, (s + PAD) // ts),
        in_specs=[in_spec(0), in_spec(1), in_spec(2), gspec, gspec],
        out_specs=[ospec, ospec, ospec], out_shape=[shp, shp, shp],
        compiler_params=_params("parallel", "arbitrary"), name=name)(proj, proj, proj, gq2, gk2)


def _qk_norm_bwd(name, dqn, dkn, proj, gq2, gk2, n_pairs):
    s = proj.shape[0]
    ts = min(ROW_TILE, s)
    pb = PAD // ts
    c0 = 3 * n_pairs
    n_r = s // ts

    def body(dqn_ref, dkn_ref, q_ref, k_ref, gq_ref, gk_ref, dq_ref, dk_ref, dgq_ref, dgk_ref):
        first = jnp.logical_and(pl.program_id(0) == 0, pl.program_id(1) == 0)
        last = jnp.logical_and(pl.program_id(0) == n_pairs - 1, pl.program_id(1) == n_r - 1)

        @pl.when(first)
        def _():
            dgq_ref[...] = jnp.zeros_like(dgq_ref)
            dgk_ref[...] = jnp.zeros_like(dgk_ref)

        gm = _group_mean_matrix()
        for dn_ref, x_ref, g_ref, dx_ref, dg_ref in ((dqn_ref, q_ref, gq_ref, dq_ref, dgq_ref),
                                                     (dkn_ref, k_ref, gk_ref, dk_ref, dgk_ref)):
            xv = x_ref[...]
            r = lax.rsqrt(_split_dot(xv * xv, gm, 2) + EPS)
            xhat = xv * r
            dn = dn_ref[...]
            tg = dn * g_ref[...]
            md = _split_dot(tg * xhat, gm, 2)
            dx_ref[...] = (r * (tg - xhat * md)).astype(BF16)
            dg_ref[...] += jnp.sum(dn * xhat, axis=0, keepdims=True)

        @pl.when(last)
        def _():
            for dg_ref in (dgq_ref, dgk_ref):
                gv = dg_ref[...]
                dg_ref[...] = gv + pltpu.roll(gv, HEAD_DIM, axis=1)

    def x_spec(off):
        return pl.BlockSpec((ts, LANES), lambda p, i: (i, c0 + off * n_pairs + p))

    dn_q = pl.BlockSpec((ts, LANES), lambda p, i: (i, p))
    dn_k = pl.BlockSpec((ts, LANES), lambda p, i: (i + pb, p))
    gspec = pl.BlockSpec((1, LANES), lambda p, i: (0, 0))
    ospec = pl.BlockSpec((ts, LANES), lambda p, i: (i, p))
    shp = jax.ShapeDtypeStruct((s, n_pairs * LANES), BF16)
    vec = jax.ShapeDtypeStruct((1, LANES), F32)
    return pl.pallas_call(
        body, grid=(n_pairs, n_r),
        in_specs=[dn_q, dn_k, x_spec(0), x_spec(1), gspec, gspec],
        out_specs=[ospec, ospec, gspec, gspec], out_shape=[shp, shp, vec, vec],
        compiler_params=_params("arbitrary", "arbitrary"), name=name)(dqn, dkn, proj, proj, gq2, gk2)


def _head_blocks(x2):
    lane = lax.broadcasted_iota(jnp.int32, x2.shape, 1)
    zero = jnp.zeros_like(x2)
    return jnp.concatenate([jnp.where(lane < HEAD_DIM, x2, zero), jnp.where(lane >= HEAD_DIM, x2, zero)], axis=0)


def _head_diag(r):
    lane = lax.broadcasted_iota(jnp.int32, (CHUNK, LANES), 1)
    return jnp.where(lane < HEAD_DIM, r[:CHUNK], r[CHUNK:])


def _ca_probs(kw, qb, bias_t, n):
    sc = lax.dot_general(kw, qb, NT, preferred_element_type=F32) * SCALE + bias_t
    jpos = lax.broadcasted_iota(jnp.int32, (BAND, LANES), 0)
    sc = jnp.where(n * CHUNK + jpos >= PAD, sc, NEG_INF)
    p = jnp.exp(sc - jnp.max(sc, axis=0, keepdims=True))
    return p * (1.0 / jnp.sum(p, axis=0, keepdims=True))


def _ca_fwd(name, qn, knp, vp, bias_tt, n_pairs, mixed):
    sp_rows = knp.shape[0]
    s = sp_rows - PAD
    t = min(CA_TILE, s)
    cpb = t // CHUNK
    pb = PAD // t

    def body(q_ref, k_ref, v_ref, b_ref, mixed_ref, o_ref):
        qi = pl.program_id(1)
        wss = [pl.multiple_of((qi * cpb + ci) * CHUNK, CHUNK) for ci in range(cpb)]
        ps = [_ca_probs(k_ref[pl.ds(wss[ci], BAND), :], _head_blocks(q_ref[ci * CHUNK:(ci + 1) * CHUNK, :]),
                        b_ref[...], qi * cpb + ci) for ci in range(cpb)]
        for ci in range(cpb):
            r = lax.dot_general(ps[ci].astype(BF16), v_ref[pl.ds(wss[ci], BAND), :], TN, preferred_element_type=F32)
            o_ref[ci * CHUNK:(ci + 1) * CHUNK, :] = _head_diag(r)

    full = pl.BlockSpec((sp_rows, LANES), lambda p, i: (0, p))
    return pl.pallas_call(
        body, grid=(n_pairs, s // t),
        in_specs=[pl.BlockSpec((t, LANES), lambda p, i: (i + pb, p)), full, full,
                  pl.BlockSpec((None, BAND, LANES), lambda p, i: (p, 0, 0)), pl.BlockSpec(memory_space=pl.ANY)],
        out_specs=pl.BlockSpec((t, LANES), lambda p, i: (i, n_pairs + p)),
        out_shape=jax.ShapeDtypeStruct(mixed.shape, mixed.dtype), input_output_aliases={4: 0},
        compiler_params=_params("parallel", "arbitrary"), name=name)(qn, knp, vp, bias_tt, mixed)


def _ca_bwd(name, qn, knp, vp, bias_tt, dmixed, n_pairs, col0):
    sp_rows = knp.shape[0]
    s = sp_rows - PAD
    t = min(CA_TILE, s)
    cpb = t // CHUNK
    pb = PAD // t

    def body(q_ref, k_ref, v_ref, b_ref, do_ref, dq_ref, dk_ref, dv_ref, db_ref):
        qi = pl.program_id(1)

        @pl.when(qi == 0)
        def _():
            dk_ref[...] = jnp.zeros_like(dk_ref)
            dv_ref[...] = jnp.zeros_like(dv_ref)
            db_ref[...] = jnp.zeros_like(db_ref)

        wss = [pl.multiple_of((qi * cpb + ci) * CHUNK, CHUNK) for ci in range(cpb)]
        kws = [k_ref[pl.ds(ws, BAND), :] for ws in wss]
        qbs = [_head_blocks(q_ref[ci * CHUNK:(ci + 1) * CHUNK, :]) for ci in range(cpb)]
        dbs = [_head_blocks(do_ref[ci * CHUNK:(ci + 1) * CHUNK, :].astype(BF16)) for ci in range(cpb)]
        ps = [_ca_probs(kws[ci], qbs[ci], b_ref[...], qi * cpb + ci) for ci in range(cpb)]
        dps = [lax.dot_general(v_ref[pl.ds(wss[ci], BAND), :], dbs[ci], NT, preferred_element_type=F32)
               for ci in range(cpb)]
        dss = [p * (dp - jnp.sum(p * dp, axis=0, keepdims=True)) for p, dp in zip(ps, dps)]
        for ci in range(cpb):
            dsb = dss[ci].astype(BF16)
            dv_ref[pl.ds(wss[ci], BAND), :] += jnp.dot(ps[ci].astype(BF16), dbs[ci], preferred_element_type=F32)
            dk_ref[pl.ds(wss[ci], BAND), :] += jnp.dot(dsb, qbs[ci], preferred_element_type=F32) * SCALE
            r = lax.dot_general(dsb, kws[ci], TN, preferred_element_type=F32)
            dq_ref[ci * CHUNK:(ci + 1) * CHUNK, :] = _head_diag(r) * SCALE
        total = dss[0]
        for ci in range(1, cpb):
            total = total + dss[ci]
        db_ref[...] += total

    full_in = pl.BlockSpec((sp_rows, LANES), lambda p, i: (0, p))
    btile = pl.BlockSpec((None, BAND, LANES), lambda p, i: (p, 0, 0))
    pad_shape = jax.ShapeDtypeStruct((sp_rows, n_pairs * LANES), F32)
    return pl.pallas_call(
        body, grid=(n_pairs, s // t),
        in_specs=[pl.BlockSpec((t, LANES), lambda p, i: (i + pb, p)), full_in, full_in, btile,
                  pl.BlockSpec((t, LANES), lambda p, i: (i, col0 + p))],
        out_specs=[pl.BlockSpec((t, LANES), lambda p, i: (i, p)), full_in, full_in, btile],
        out_shape=[jax.ShapeDtypeStruct((s, n_pairs * LANES), F32), pad_shape, pad_shape,
                   jax.ShapeDtypeStruct(bias_tt.shape, F32)],
        compiler_params=_params("parallel", "arbitrary"), name=name)(qn, knp, vp, bias_tt, dmixed)


def _pair_transposed(tile):
    n_h = tile.shape[0]
    return jnp.transpose(tile.reshape(n_h // 2, 2, CHUNK, BAND), (0, 3, 1, 2)).reshape(n_h // 2, BAND, LANES)


def _query_major(tile_t):
    n_p = tile_t.shape[0]
    return jnp.transpose(tile_t.reshape(n_p, BAND, 2, CHUNK), (3, 0, 2, 1)).reshape(CHUNK, 2 * n_p, BAND)


def _bias_tile(rb):
    n_h = rb.shape[0]
    far = jnp.broadcast_to(rb[:, 2 * REL_CLIP:], (n_h, PAD + CHUNK - REL_CLIP))
    near = rb[:, REL_CLIP - (CHUNK - 1):2 * REL_CLIP][:, ::-1]
    ext = jnp.concatenate([far, near], axis=1)
    return jnp.stack([ext[:, CHUNK - 1 - i:CHUNK - 1 - i + BAND] for i in range(CHUNK)], axis=1)


def _rel_bias_grad(name, db_t):
    _, n_h, _ = db_t.shape
    n_out = 3 * LANES
    assert CHUNK == 64 and REL_CLIP == 128 and LEFT_CHUNKS == 8

    def body(z_ref, o_ref):
        a_idx = lax.broadcasted_iota(jnp.int32, (LANES, n_out), 0)
        r_idx = lax.broadcasted_iota(jnp.int32, (LANES, n_out), 1)
        out = jnp.zeros((n_h, n_out), F32)
        for mm, base in ((0, 191), (1, 255), (2, 319)):
            b = LEFT_CHUNKS - mm
            acc = jnp.zeros((n_h, LANES), F32)
            for i in range(CHUNK):
                row = z_ref[i, :, b * CHUNK:(b + 1) * CHUNK]
                rowp = jnp.concatenate([row, jnp.zeros((n_h, LANES - CHUNK), F32)], axis=1)
                acc = acc + pltpu.roll(rowp, CHUNK - 1 - i, axis=1)
            place = (r_idx == jnp.minimum(base - a_idx, 2 * REL_CLIP)).astype(BF16)
            out = out + _split_dot(acc, place, 3)
        far = jnp.zeros((n_h, 1), F32)
        for i in range(CHUNK):
            far = far + jnp.sum(z_ref[i, :, 0:(LEFT_CHUNKS - 2) * CHUNK], axis=1, keepdims=True)
        lane = lax.broadcasted_iota(jnp.int32, (n_h, n_out), 1)
        o_ref[...] = out + jnp.where(lane == 2 * REL_CLIP, far, 0.0)

    return pl.pallas_call(
        body, out_shape=jax.ShapeDtypeStruct((n_h, n_out), F32),
        in_specs=[pl.BlockSpec(memory_space=pltpu.VMEM)], out_specs=pl.BlockSpec(memory_space=pltpu.VMEM),
        name=name)(db_t)


def _mesh_pos():
    return lax.axis_index("x"), lax.axis_index("y"), lax.axis_index("c")


def _other_chips(x, y):
    return [(1 - x, y), (x, 1 - y), (1 - x, 1 - y)]


def _all_gather_small(name, blk):
    m_per, n = blk.shape

    def body(x_ref, out_ref, send_sems, recv_sems, local_sem):
        x, y, c = _mesh_pos()
        me, sibling = (x, y, c), (x, y, 1 - c)
        chips = _other_chips(x, y)

        def rows(px, py, pc):
            return out_ref.at[pl.ds((4 * px + 2 * py + pc) * m_per, m_per), :]

        def copy(k, block, to, src=None):
            return pltpu.make_async_remote_copy(
                src_ref=rows(*block) if src is None else src, dst_ref=rows(*block),
                send_sem=send_sems.at[k], recv_sem=recv_sems.at[k], device_id=to, device_id_type=MESH)

        mine = pltpu.make_async_copy(x_ref, rows(*me), local_sem)
        mine.start()
        first = [copy(0, me, sibling, src=x_ref)]
        first += [copy(1 + j, me, (*chip, c), src=x_ref) for j, chip in enumerate(chips)]
        for cp in first:
            cp.start()
        passed = [copy(4 + j, (*chip, c), sibling) for j, chip in enumerate(chips)]
        for j, chip in enumerate(chips):
            copy(1 + j, (*chip, c), me).wait_recv()
            passed[j].start()
        copy(0, sibling, me).wait_recv()
        for j, chip in enumerate(chips):
            copy(4 + j, (*chip, 1 - c), me).wait_recv()
        for cp in first + passed:
            cp.wait_send()
        mine.wait()

    return pl.pallas_call(
        body, out_shape=jax.ShapeDtypeStruct((N_DEV * m_per, n), blk.dtype),
        in_specs=[pl.BlockSpec(memory_space=pltpu.VMEM)], out_specs=pl.BlockSpec(memory_space=pltpu.VMEM),
        scratch_shapes=[pltpu.SemaphoreType.DMA((7,)), pltpu.SemaphoreType.DMA((7,)), pltpu.SemaphoreType.DMA],
        compiler_params=pltpu.CompilerParams(vmem_limit_bytes=VMEM_LIMIT), name=name)(blk)


HBM_SPEC = pl.BlockSpec(memory_space=pltpu.HBM)


def _gather_copies(bufs, send_sems, recv_sems, phase):
    x, y, c = _mesh_pos()
    my_j = 2 * x + y
    starts, lands = [], []
    for t, buf in enumerate(bufs):
        rh = buf.shape[2] // 2
        for k, (px, py) in enumerate(_other_chips(x, y)):
            their_j = 2 * px + py
            if phase == 0:
                src, to = buf.at[0, my_j, pl.ds(c * rh, rh), :], (px, py, c)
                land = buf.at[0, their_j, pl.ds(c * rh, rh), :]
            else:
                src, to = buf.at[0, their_j, pl.ds(c * rh, rh), :], (x, y, 1 - c)
                land = buf.at[0, their_j, pl.ds((1 - c) * rh, rh), :]
            sems = dict(send_sem=send_sems.at[phase, t, k], recv_sem=recv_sems.at[phase, t, k],
                        device_id=to, device_id_type=MESH)
            starts.append(pltpu.make_async_remote_copy(src_ref=src, dst_ref=src, **sems))
            lands.append(pltpu.make_async_remote_copy(src_ref=src, dst_ref=land, **sems))
    return starts, lands


def _gather_sems(n):
    return [pltpu.SemaphoreType.DMA((2, n, 3)), pltpu.SemaphoreType.DMA((2, n, 3))]


def _all_gather_weights(name, bufs, phases):
    n = len(bufs)

    def body(*refs):
        outs = refs[n:2 * n]
        send_sems, recv_sems = refs[2 * n:]
        for phase in phases:
            starts, lands = _gather_copies(outs, send_sems, recv_sems, phase)
            for cp in starts:
                cp.start()
            for cp, ld in zip(starts, lands):
                cp.wait_send()
                ld.wait_recv()

    return pl.pallas_call(
        body, out_shape=[jax.ShapeDtypeStruct(b.shape, b.dtype) for b in bufs],
        in_specs=[HBM_SPEC] * n, out_specs=[HBM_SPEC] * n, input_output_aliases={t: t for t in range(n)},
        scratch_shapes=_gather_sems(n), name=name)(*bufs)


def _gather_buffer(shard, chip):
    r, c = shard.shape
    return lax.dynamic_update_slice(jnp.zeros((1, N_CHIPS, r, c), BF16), shard.astype(BF16)[None, None],
                                    (0, chip, 0, 0))


def _exchange_sibling_halves(name, grads):
    n = len(grads)

    def body(*refs):
        srcs, outs = refs[:n], refs[n:2 * n]
        send_sems, recv_sems = refs[2 * n:]
        x, y, c = _mesh_pos()
        cps = []
        for t in range(n):
            rh = grads[t].shape[1] // 2
            cps.append(pltpu.make_async_remote_copy(
                src_ref=srcs[t].at[:, pl.ds((1 - c) * rh, rh), :], dst_ref=outs[t],
                send_sem=send_sems.at[t], recv_sem=recv_sems.at[t], device_id=(x, y, 1 - c), device_id_type=MESH))
        for cp in cps:
            cp.start()
        for cp in cps:
            cp.wait()

    out_shape = [jax.ShapeDtypeStruct((g.shape[0], g.shape[1] // 2, g.shape[2]), g.dtype) for g in grads]
    return pl.pallas_call(
        body, out_shape=out_shape, in_specs=[HBM_SPEC] * n, out_specs=[HBM_SPEC] * n,
        scratch_shapes=[pltpu.SemaphoreType.DMA((n,)), pltpu.SemaphoreType.DMA((n,))],
        name=name)(*grads)


def _add_halves(name, g, recv, c_idx):
    nb, r, c = g.shape
    rh = r // 2
    tr = rh
    for cand in (512, 256, 128, 64):
        if rh % cand == 0:
            tr = cand
            break
    g4 = g.reshape(nb, 2, rh, c)

    def body(c_ref, g_ref, r_ref, o_ref):
        o_ref[...] = (g_ref[...] + r_ref[...]).astype(BF16)

    grid_spec = pltpu.PrefetchScalarGridSpec(
        num_scalar_prefetch=1, grid=(nb, rh // tr),
        in_specs=[pl.BlockSpec((None, None, tr, c), lambda j, i, cr: (j, cr[0], i, 0)),
                  pl.BlockSpec((None, tr, c), lambda j, i, cr: (j, i, 0))],
        out_specs=pl.BlockSpec((None, tr, c), lambda j, i, cr: (j, i, 0)))
    return pl.pallas_call(
        body, grid_spec=grid_spec, out_shape=jax.ShapeDtypeStruct((nb, rh, c), BF16),
        compiler_params=_params("parallel", "parallel"), name=name)(c_idx, g4, recv)


def _exchange_copies(srcs, outs, send_sems, recv_sems):
    x, y, c = _mesh_pos()
    return [pltpu.make_async_remote_copy(
        src_ref=srcs[t].at[2 * px + py], dst_ref=outs[t].at[k], send_sem=send_sems.at[t, k],
        recv_sem=recv_sems.at[t, k], device_id=(px, py, c), device_id_type=MESH)
        for t in range(len(srcs)) for k, (px, py) in enumerate(_other_chips(x, y))]


def _exchange_sems(n):
    return [pltpu.SemaphoreType.DMA((n, 3)), pltpu.SemaphoreType.DMA((n, 3))]


def _exchange_chips(name, parts):
    n = len(parts)

    def body(*refs):
        cps = _exchange_copies(refs[:n], refs[n:2 * n], *refs[2 * n:])
        for cp in cps:
            cp.start()
        for cp in cps:
            cp.wait()

    return pl.pallas_call(
        body, out_shape=[jax.ShapeDtypeStruct((3,) + p.shape[1:], p.dtype) for p in parts],
        in_specs=[HBM_SPEC] * n, out_specs=[HBM_SPEC] * n, scratch_shapes=_exchange_sems(n), name=name)(*parts)


def _sum_chips(name, own_parts, recvs, pos_idx):
    n_layers = len(own_parts)
    _, rh, c = recvs[0].shape
    tr = rh
    for cand in (512, 256, 128, 64):
        if rh % cand == 0:
            tr = cand
            break
    nb = rh // tr

    def body(pos_ref, *refs):
        own_refs, r_refs, o_ref = refs[:n_layers], refs[n_layers:2 * n_layers], refs[2 * n_layers]
        for l_i in range(n_layers):
            @pl.when(pl.program_id(0) == l_i)
            def _():
                acc = own_refs[l_i][...].astype(F32)
                for k in range(3):
                    acc = acc + r_refs[l_i][k].astype(F32)
                o_ref[...] = acc

    grid_spec = pltpu.PrefetchScalarGridSpec(
        num_scalar_prefetch=1, grid=(n_layers, nb),
        in_specs=[pl.BlockSpec((None, tr, c), lambda l, i, pr: (pr[0], i, 0)) for _ in range(n_layers)]
        + [pl.BlockSpec((3, tr, c), lambda l, i, pr: (0, i, 0)) for _ in range(n_layers)],
        out_specs=pl.BlockSpec((None, tr, c), lambda l, i, pr: (l, pr[1] * nb + i, 0)))
    return pl.pallas_call(
        body, grid_spec=grid_spec, out_shape=jax.ShapeDtypeStruct((n_layers, 2 * rh, c), F32),
        compiler_params=_params("parallel", "parallel"), name=name)(pos_idx, *own_parts, *recvs)


def _share_with_sibling(name, grads):
    n = len(grads)

    def body(*refs):
        outs = refs[n:2 * n]
        send_sems, recv_sems = refs[2 * n:]
        x, y, c = _mesh_pos()
        cps, landed = [], []
        for t in range(n):
            rh = grads[t].shape[1] // 2
            mine = outs[t].at[:, pl.ds(c * rh, rh), :]
            theirs = outs[t].at[:, pl.ds((1 - c) * rh, rh), :]
            cps.append(pltpu.make_async_remote_copy(
                src_ref=mine, dst_ref=mine, send_sem=send_sems.at[t], recv_sem=recv_sems.at[t],
                device_id=(x, y, 1 - c), device_id_type=MESH))
            landed.append(pltpu.make_async_remote_copy(
                src_ref=mine, dst_ref=theirs, send_sem=send_sems.at[t], recv_sem=recv_sems.at[t],
                device_id=(x, y, 1 - c), device_id_type=MESH))
        for cp in cps:
            cp.start()
        for cp, ld in zip(cps, landed):
            cp.wait_send()
            ld.wait_recv()

    out_shape = [jax.ShapeDtypeStruct(g.shape, g.dtype) for g in grads]
    return pl.pallas_call(
        body, out_shape=out_shape, in_specs=[HBM_SPEC] * n, out_specs=[HBM_SPEC] * n,
        input_output_aliases={t: t for t in range(n)},
        scratch_shapes=[pltpu.SemaphoreType.DMA((n,)), pltpu.SemaphoreType.DMA((n,))],
        name=name)(*grads)


def _row(v):
    return v.reshape(1, -1)


def kernel(x, c, g_norm1, w_in, g_q, g_k, rel_bias, w_o, g_norm2, w1, w2, w_ada, b_ada, loss_target, m_g_norm1, m_w_in, m_g_q, m_g_k, m_rel_bias, m_w_o, m_g_norm2, m_w1, m_w2, m_w_ada, m_b_ada, v_g_norm1, v_w_in, v_g_q, v_g_k, v_rel_bias, v_w_o, v_g_norm2, v_w1, v_w2, v_w_ada, v_b_ada):
    _, s, d = x.shape
    n_layers = g_norm1.shape[0]
    n_pairs = (d // 2) // LANES
    n_ca_heads = rel_bias.shape[1]
    n_rel = rel_bias.shape[2]
    assert n_rel == 2 * REL_CLIP + 1 and g_q.shape[1] == HEAD_DIM and n_ca_heads == 2 * n_pairs
    ada_c = w_ada.shape[2]

    ax, ay, ac = _mesh_pos()
    chip = 2 * ax + ay
    dev = 4 * ax + 2 * ay + ac
    c_idx = jnp.reshape(ac, (1,)).astype(jnp.int32)

    c_act_all = _all_gather_small("ag_c", _silu_rows("silu_c", c)).reshape(N_DEV, 8, d)[:, 0, :]
    mod_parts = []
    for l in range(n_layers):
        bias = lax.dynamic_slice_in_dim(b_ada[l], chip * ada_c, ada_c).reshape(1, ada_c)
        mod_parts.append(_ada_mod(f"ada_mod{l}", c_act_all, w_ada[l], bias))
    mod_all = _all_gather_small("ag_mod", jnp.concatenate(mod_parts, axis=1))
    mod_all = mod_all.reshape(N_CHIPS, 2, N_DEV, n_layers, ada_c)[:, 0]
    mod_all = jnp.transpose(mod_all, (1, 2, 0, 3)).reshape(N_DEV, n_layers, N_CHIPS * ada_c)
    mod = lax.dynamic_index_in_dim(mod_all, dev, axis=0, keepdims=False)

    gathered = [[_gather_buffer(w[l], chip) for w in (w_in, w_o, w1, w2)] for l in range(n_layers)]
    gathered[0] = _all_gather_weights("ag_weights_0", gathered[0], (0, 1))

    xs = x[0]
    saved = []
    for l in range(n_layers):
        wg_in, wg_o, wg_1, wg_2 = gathered[l]
        sh1, sc1, gt1, sh2, sc2, gt2 = [_row(mod[l, i * d:(i + 1) * d]) for i in range(6)]
        gq2 = _row(jnp.tile(g_q[l], 2))
        gk2 = _row(jnp.tile(g_k[l], 2))
        bias_t = _pair_transposed(_bias_tile(rel_bias[l]))
        h1 = _norm_mod(f"norm1_{l}", xs, _row(g_norm1[l]), sc1, sh1)
        proj = _mm_nn_w(f"proj_{l}", h1, wg_in, 0, True, [F32])[0]
        if l + 1 < n_layers:
            o_sb, landed = _sb_fwd(f"sb_fwd_{l}", proj, n_pairs, gather=gathered[l + 1])
            gathered[l + 1] = _all_gather_weights(f"ag_weights_{l + 1}", landed, (1,))
        else:
            o_sb, _ = _sb_fwd(f"sb_fwd_{l}", proj, n_pairs)
        qn, knp, vp = _qk_norm(f"qk_norm_{l}", proj, gq2, gk2, n_pairs)
        mixed = _ca_fwd(f"ca_fwd_{l}", qn, knp, vp, bias_t, n_pairs, o_sb)

        def res_epi(acc, res, gt):
            return res + gt * acc, acc

        def res_specs(tm, tn):
            return (pl.BlockSpec((tm, tn), lambda i, j, kk: (i, j)), pl.BlockSpec((1, tn), lambda i, j, kk: (0, j)))

        x1, m1 = _mm_nn_w(f"attn_out_{l}", mixed, wg_o, 0, False, [F32, BF16], epi=res_epi,
                          extras=(xs, gt1), extra_specs_fn=res_specs)
        h2 = _norm_mod(f"norm2_{l}", x1, _row(g_norm2[l]), sc2, sh2)

        def act_epi(acc):
            r = jnp.maximum(acc, 0.0)
            return acc, r * r

        u, a = _mm_nn_w(f"mlp_up_{l}", h2, wg_1, 0, True, [BF16, BF16], epi=act_epi)
        x2, m2 = _mm_nn_w(f"mlp_down_{l}", a, wg_2, 0, False, [F32, BF16], epi=res_epi,
                          extras=(x1, gt2), extra_specs_fn=res_specs)
        saved.append(dict(x0=xs, h1=h1, proj=proj, o_sb=mixed, qn=qn, knp=knp, vp=vp, bias_t=bias_t, mixed=mixed,
                          m1=m1, x1=x1, h2=h2, u=u, a=a, m2=m2, gq2=gq2, gk2=gk2,
                          sc1=sc1, gt1=gt1, sc2=sc2, gt2=gt2))
        xs = x2

    dx, sq = _loss_grad("loss", xs, loss_target[0])
    loss_part = (0.5 * jnp.sum(sq) / d).reshape(1)

    g_in, g_o, g_1, g_2 = [None] * n_layers, [None] * n_layers, [None] * n_layers, [None] * n_layers
    dmod, dg1, dg2, dgq, dgk, drel = [], [], [], [], [], []
    partial, chip_sum, from_chip = {}, {}, {}

    def reduce_on_chip(tag, keys):
        grads = [partial[k] for k in keys]
        for k, g, r in zip(keys, grads, _exchange_sibling_halves(f"rs_sibling_{tag}", grads)):
            chip_sum[k] = _add_halves(f"rs_add_{k[0]}_{k[1]}", g, r, c_idx)

    for l in reversed(range(n_layers)):
        sv = saved[l]
        wg_in, wg_o, wg_1, wg_2 = gathered[l]
        dm2, dgt2 = _gate_bwd(f"gate2_bwd_{l}", dx, sv["m2"], sv["gt2"])

        def act_bwd_epi(acc, u_t):
            return (acc * (2.0 * jnp.maximum(u_t.astype(F32), 0.0)),)

        def tile_specs(tm, tn):
            return (pl.BlockSpec((tm, tn), lambda i, j, kk: (i, j)),)

        du = _mm_nt_w(f"mlp_down_bwd_{l}", dm2, wg_2, 0, False, BF16, epi=act_bwd_epi, extras=(sv["u"],),
                      extra_specs_fn=tile_specs)
        g_2[l] = _mm_tn(f"w2_grad_{l}", sv["a"], dm2, False)
        dh2 = _mm_nt_w(f"mlp_up_bwd_{l}", du, wg_1, 0, True, F32)
        g_1[l] = _mm_tn(f"w1_grad_{l}", sv["h2"], du, True)
        dx1, dsh2, dsc2, dgn2 = _norm_mod_bwd(f"norm2_bwd_{l}", dh2, sv["x1"], _row(g_norm2[l]), sv["sc2"], dx)
        dm1, dgt1 = _gate_bwd(f"gate1_bwd_{l}", dx1, sv["m1"], sv["gt1"])
        dmixed = _mm_nt_w(f"attn_out_bwd_{l}", dm1, wg_o, 0, False, F32)
        g_o[l] = _mm_tn(f"wo_grad_{l}", sv["mixed"], dm1, False)
        partial.update({(1, l): g_o[l], (2, l): g_1[l], (3, l): g_2[l]})
        if l == 0:
            early = sorted(partial)
            reduce_on_chip("early", early)
            dq_sb, dk_sb, dv_sb, landed = _sb_bwd(f"sb_bwd_{l}", sv["proj"], sv["o_sb"], dmixed, n_pairs,
                                                  exchange=[chip_sum[k] for k in early])
            from_chip.update(zip(early, landed))
        else:
            dq_sb, dk_sb, dv_sb, _ = _sb_bwd(f"sb_bwd_{l}", sv["proj"], sv["o_sb"], dmixed, n_pairs)
        dqn, dknp, dvp, dbias_t = _ca_bwd(f"ca_bwd_{l}", sv["qn"], sv["knp"], sv["vp"], sv["bias_t"], dmixed,
                                          n_pairs, n_pairs)
        dq_ca, dk_ca, dgq_l, dgk_l = _qk_norm_bwd(f"qk_norm_bwd_{l}", dqn, dknp, sv["proj"], sv["gq2"], sv["gk2"],
                                                  n_pairs)
        drel_l = _rel_bias_grad(f"rel_bias_grad_{l}", _query_major(dbias_t))[:, :n_rel]
        dproj = jnp.concatenate([dq_sb.astype(BF16), dk_sb.astype(BF16), dv_sb.astype(BF16), dq_ca, dk_ca,
                                 dvp[PAD:].astype(BF16)], axis=1)
        g_in[l] = _mm_tn(f"win_grad_{l}", sv["h1"], dproj, True)
        partial[(0, l)] = g_in[l]
        dh1 = _mm_nt_w(f"proj_bwd_{l}", dproj, wg_in, 0, True, F32)
        dx, dsh1, dsc1, dgn1 = _norm_mod_bwd(f"norm1_bwd_{l}", dh1, sv["x0"], _row(g_norm1[l]), sv["sc1"], dx1)
        dmod.insert(0, jnp.concatenate([dsh1, dsc1, dgt1, dsh2, dsc2, dgt2], axis=1)[0])
        dg1.insert(0, dgn1[0])
        dg2.insert(0, dgn2[0])
        dgq.insert(0, dgq_l[0, :HEAD_DIM])
        dgk.insert(0, dgk_l[0, :HEAD_DIM])
        drel.insert(0, drel_l.reshape(-1))
    grad_x = dx[None]

    small = jnp.concatenate([jnp.concatenate(dmod), jnp.concatenate(dg1), jnp.concatenate(dg2),
                             jnp.concatenate(dgq), jnp.concatenate(dgk), jnp.concatenate(drel), loss_part])
    n_small = small.shape[0]
    n_pack = -(-n_small // (8 * LANES)) * (8 * LANES)
    small = jnp.pad(small, (0, n_pack - n_small)).reshape(8, n_pack // 8)
    small_all = _all_gather_small("ag_small", small).reshape(N_DEV, 8, n_pack // 8)
    small_sum = _sum_rows8("sum_small", small_all).reshape(-1)
    dmod_all = small_all.reshape(N_DEV, n_pack)[:, :n_layers * 6 * d].reshape(N_DEV, n_layers, 6 * d)

    off = 0

    def take(n, shape):
        nonlocal off
        out = small_sum[off:off + n].reshape(shape)
        off += n
        return out

    grad_b_ada = take(n_layers * 6 * d, (n_layers, 6 * d))
    grad_g_norm1 = take(n_layers * d, (n_layers, d))
    grad_g_norm2 = take(n_layers * d, (n_layers, d))
    grad_g_q = take(n_layers * HEAD_DIM, (n_layers, HEAD_DIM))
    grad_g_k = take(n_layers * HEAD_DIM, (n_layers, HEAD_DIM))
    grad_rel_bias = take(n_layers * n_ca_heads * n_rel, (n_layers, n_ca_heads, n_rel))
    loss = take(1, ())

    c_act_t = jnp.transpose(c_act_all)
    grad_w_ada = jnp.stack([
        _outer_sum(f"wada_grad_{l}", c_act_t,
                   lax.dynamic_slice_in_dim(dmod_all[:, l, :], chip * ada_c, ada_c, axis=1))
        for l in range(n_layers)])

    late = [k for k in sorted(partial) if k not in from_chip]
    reduce_on_chip("late", late)
    from_chip.update(zip(late, _exchange_chips("rs_chips", [chip_sum[k] for k in late])))
    pos_idx = jnp.stack([chip, ac]).astype(jnp.int32)
    reduced = [_sum_chips(f"rs_sum_{w_i}", [chip_sum[(w_i, l)] for l in range(n_layers)],
                          [from_chip[(w_i, l)] for l in range(n_layers)], pos_idx) for w_i in range(4)]
    grad_w_in, grad_w_o, grad_w1, grad_w2 = _share_with_sibling("rs_share", reduced)

    def adam_big(name, w, g, m, v):
        shp = w.shape
        outs = _adamw(name, w.reshape(-1, shp[-1]), g.reshape(-1, shp[-1]), m.reshape(-1, shp[-1]),
                      v.reshape(-1, shp[-1]))
        return [o.reshape(shp) for o in outs]

    def pack(arrs):
        flat = jnp.concatenate([a.reshape(-1) for a in arrs])
        n = flat.shape[0]
        n_p = -(-n // (8 * LANES)) * (8 * LANES)
        return jnp.pad(flat, (0, n_p - n), constant_values=1.0).reshape(8, n_p // 8)

    small_w = [g_norm1, g_q, g_k, rel_bias, g_norm2, b_ada]
    small_g = [grad_g_norm1, grad_g_q, grad_g_k, grad_rel_bias, grad_g_norm2, grad_b_ada]
    small_m = [m_g_norm1, m_g_q, m_g_k, m_rel_bias, m_g_norm2, m_b_ada]
    small_v = [v_g_norm1, v_g_q, v_g_k, v_rel_bias, v_g_norm2, v_b_ada]
    packed = _adamw("adamw_small", pack(small_w), pack(small_g), pack(small_m), pack(small_v))

    def unpack(p):
        flat = p.reshape(-1)
        res, o = [], 0
        for a in small_w:
            res.append(flat[o:o + a.size].reshape(a.shape))
            o += a.size
        return res

    sd, sm, sv_ = unpack(packed[0]), unpack(packed[1]), unpack(packed[2])
    big = {
        "w_in": adam_big("adamw_w_in", w_in, grad_w_in, m_w_in, v_w_in),
        "w_o": adam_big("adamw_w_o", w_o, grad_w_o, m_w_o, v_w_o),
        "w1": adam_big("adamw_w1", w1, grad_w1, m_w1, v_w1),
        "w2": adam_big("adamw_w2", w2, grad_w2, m_w2, v_w2),
        "w_ada": adam_big("adamw_w_ada", w_ada, grad_w_ada, m_w_ada, v_w_ada),
    }

    def ordered(kind):
        sm_list = (sd, sm, sv_)[kind]
        return [sm_list[0], big["w_in"][kind], sm_list[1], sm_list[2], sm_list[3], big["w_o"][kind], sm_list[4],
                big["w1"][kind], big["w2"][kind], big["w_ada"][kind], sm_list[5]]

    grads = [grad_g_norm1, grad_w_in, grad_g_q, grad_g_k, grad_rel_bias, grad_w_o, grad_g_norm2, grad_w1, grad_w2,
             grad_w_ada, grad_b_ada]
    return (loss, grad_x, *grads, *ordered(0), *ordered(1), *ordered(2))
```

```python
import functools

import jax
import jax.numpy as jnp
from jax import lax
from jax.experimental import pallas as pl
from jax.experimental.pallas import tpu as pltpu

F32 = jnp.float32
BF16 = jnp.bfloat16
MESH = pl.DeviceIdType.MESH

EPS = 1e-6
NEG_INF = -1e30
HEAD_DIM = 64
CHUNK = 64
LEFT_CHUNKS = 8
PAD = LEFT_CHUNKS * CHUNK
BAND = PAD + CHUNK
REL_CLIP = 128
SCALE = HEAD_DIM ** -0.5
DEAD_LOG = -88.0
LANES = 128
N_CHIPS = 4
N_DEV = 8

ADAM_LR = 0.001
ADAM_B1 = 0.9
ADAM_B2 = 0.999
ADAM_EPS = 1e-08
ADAM_WD = 0.01
ADAM_STEP = 10

VMEM_LIMIT = 48 * 1024 * 1024
ROW_TILE = 512
MM_TILE = 1024
SB_TILE = 256
CA_TILE = 256

NT = (((1,), (1,)), ((), ()))
TN = (((0,), (0,)), ((), ()))
NN = (((1,), (0,)), ((), ()))


def _tile(n, pref):
    best = None
    t = LANES
    while t <= min(n, pref):
        if n % t == 0:
            best = t
        t += LANES
    return best if best is not None else n


def _params(*sem):
    return pltpu.CompilerParams(dimension_semantics=sem, vmem_limit_bytes=VMEM_LIMIT)


def _split_dot(x, t, passes):
    acc = None
    r = x
    for i in range(passes):
        hi = r.astype(BF16)
        d = jnp.dot(hi, t, preferred_element_type=F32)
        acc = d if acc is None else acc + d
        if i + 1 < passes:
            r = r - hi.astype(F32)
    return acc


def _grid_edge(grid, last):
    conds = [pl.program_id(i) == (g - 1 if last else 0) for i, g in enumerate(grid)]
    return functools.reduce(jnp.logical_and, conds)


def _ride_gather(grid, g_refs, sems, compute):
    if g_refs:
        @pl.when(_grid_edge(grid, False))
        def _():
            for cp in _gather_copies(g_refs, *sems, 0)[0]:
                cp.start()

    compute()
    if g_refs:
        @pl.when(_grid_edge(grid, True))
        def _():
            for cp, ld in zip(*_gather_copies(g_refs, *sems, 0)):
                cp.wait_send()
                ld.wait_recv()


def _matmul(name, a, b, *, dims, grid, a_spec, b_spec, out_shapes, out_specs, acc_shape,
            epi=None, extras=(), extra_specs=(), gather=()):
    nk = grid[2]
    n_ex = len(extras)
    n_out = len(out_shapes)
    n_g = len(gather)

    def body(*refs):
        a_ref, b_ref = refs[0], refs[1]
        ex = refs[2:2 + n_ex]
        outs = refs[2 + n_ex + n_g:2 + n_ex + n_g + n_out]
        g_refs = refs[2 + n_ex + n_g + n_out:2 + n_ex + 2 * n_g + n_out]
        acc = refs[2 + n_ex + 2 * n_g + n_out]
        sems = refs[3 + n_ex + 2 * n_g + n_out:]

        def compute():
            k = pl.program_id(2)

            @pl.when(k == 0)
            def _():
                acc[...] = jnp.zeros_like(acc)

            acc[...] += lax.dot_general(a_ref[...].astype(BF16), b_ref[...].astype(BF16), dims,
                                        preferred_element_type=F32)

            @pl.when(k == nk - 1)
            def _():
                res = epi(acc[...], *[e[...] for e in ex]) if epi is not None else (acc[...],)
                for o, r in zip(outs, res):
                    o[...] = r.astype(o.dtype)

        _ride_gather(grid, g_refs, sems, compute)

    outs = pl.pallas_call(
        body, grid=grid, in_specs=[a_spec, b_spec, *extra_specs] + [HBM_SPEC] * n_g,
        out_specs=list(out_specs) + [HBM_SPEC] * n_g,
        out_shape=list(out_shapes) + [jax.ShapeDtypeStruct(g.shape, g.dtype) for g in gather],
        input_output_aliases={2 + n_ex + i: n_out + i for i in range(n_g)},
        scratch_shapes=[pltpu.VMEM(acc_shape, F32)] + (_gather_sems(n_g) if n_g else []),
        compiler_params=_params(*(("arbitrary",) * 3 if n_g else ("parallel", "parallel", "arbitrary"))), name=name,
    )(a, b, *extras, *gather)
    return list(outs[:n_out]), list(outs[n_out:])


def _mm_nn_w(name, a, wg, layer, colshard, out_dtypes, epi=None, extras=(), extra_specs_fn=None, gather=()):
    m = a.shape[0]
    _, _, r, c = wg.shape
    tm = min(MM_TILE, m)
    if colshard:
        k, n = r, N_CHIPS * c
        tk, tn = k, _tile(c, MM_TILE)
        npc = c // tn
        b_spec = pl.BlockSpec((None, None, tk, tn), lambda i, j, kk: (layer, j // npc, kk, j % npc))
    else:
        k, n = N_CHIPS * r, c
        tk, tn = _tile(r, MM_TILE), _tile(c, MM_TILE)
        kpc = r // tk
        b_spec = pl.BlockSpec((None, None, tk, tn), lambda i, j, kk: (layer, kk // kpc, kk % kpc, j))
    assert a.shape[1] == k
    grid = (m // tm, n // tn, k // tk)
    out_spec = pl.BlockSpec((tm, tn), lambda i, j, kk: (i, j))
    ex_specs = extra_specs_fn(tm, tn) if extra_specs_fn is not None else ()
    outs, landed = _matmul(name, a, wg, dims=NN, grid=grid,
                           a_spec=pl.BlockSpec((tm, tk), lambda i, j, kk: (i, kk)), b_spec=b_spec,
                           out_shapes=[jax.ShapeDtypeStruct((m, n), d) for d in out_dtypes],
                           out_specs=[out_spec for _ in out_dtypes], acc_shape=(tm, tn),
                           epi=epi, extras=extras, extra_specs=ex_specs, gather=gather)
    return (outs, landed) if gather else outs


def _mm_nt_w(name, a, wg, layer, colshard, out_dtype, epi=None, extras=(), extra_specs_fn=None):
    m = a.shape[0]
    _, _, r, c = wg.shape
    tm = min(MM_TILE, m)
    if colshard:
        kdim, n = N_CHIPS * c, r
        tk, tn = _tile(c, MM_TILE), _tile(r, MM_TILE)
        kpc = c // tk
        b_spec = pl.BlockSpec((None, None, tn, tk), lambda i, j, kk: (layer, kk // kpc, j, kk % kpc))
    else:
        kdim, n = c, N_CHIPS * r
        tk, tn = c, _tile(r, MM_TILE)
        npc = r // tn
        b_spec = pl.BlockSpec((None, None, tn, tk), lambda i, j, kk: (layer, j // npc, j % npc, kk))
    assert a.shape[1] == kdim
    grid = (m // tm, n // tn, kdim // tk)
    ex_specs = extra_specs_fn(tm, tn) if extra_specs_fn is not None else ()
    return _matmul(name, a, wg, dims=NT, grid=grid,
                   a_spec=pl.BlockSpec((tm, tk), lambda i, j, kk: (i, kk)), b_spec=b_spec,
                   out_shapes=[jax.ShapeDtypeStruct((m, n), out_dtype)],
                   out_specs=[pl.BlockSpec((tm, tn), lambda i, j, kk: (i, j))], acc_shape=(tm, tn),
                   epi=epi, extras=extras, extra_specs=ex_specs)[0][0]


def _mm_tn(name, a, b, colshard):
    s, m = a.shape
    n = b.shape[1]
    tk = min(2 * MM_TILE, s)
    if colshard:
        c = n // N_CHIPS
        tm, tn = _tile(m, MM_TILE // 2), _tile(c, MM_TILE)
        npc = c // tn
        out_shape = (N_CHIPS, m, c)
        out_spec = pl.BlockSpec((None, tm, tn), lambda i, j, kk: (j // npc, i, j % npc))
    else:
        r = m // N_CHIPS
        tm, tn = _tile(r, MM_TILE // 2), _tile(n, MM_TILE)
        mpc = r // tm
        out_shape = (N_CHIPS, r, n)
        out_spec = pl.BlockSpec((None, tm, tn), lambda i, j, kk: (i // mpc, i % mpc, j))
    grid = (m // tm, n // tn, s // tk)
    return _matmul(name, a, b, dims=TN, grid=grid,
                   a_spec=pl.BlockSpec((tk, tm), lambda i, j, kk: (kk, i)),
                   b_spec=pl.BlockSpec((tk, tn), lambda i, j, kk: (kk, j)),
                   out_shapes=[jax.ShapeDtypeStruct(out_shape, F32)], out_specs=[out_spec],
                   acc_shape=(tm, tn))[0][0]


def _row_spec(ts, d):
    return pl.BlockSpec((ts, d), lambda i: (i, 0))


def _vec_spec(d):
    return pl.BlockSpec((1, d), lambda i: (0, 0))


def _norm_mod(name, x, g, sc, sh):
    s, d = x.shape
    ts = min(ROW_TILE, s)

    def body(x_ref, g_ref, sc_ref, sh_ref, h_ref):
        xv = x_ref[...]
        r = lax.rsqrt(jnp.mean(xv * xv, axis=-1, keepdims=True) + EPS)
        h_ref[...] = (xv * r * g_ref[...] * (1.0 + sc_ref[...]) + sh_ref[...]).astype(BF16)

    return pl.pallas_call(
        body, grid=(s // ts,), in_specs=[_row_spec(ts, d), _vec_spec(d), _vec_spec(d), _vec_spec(d)],
        out_specs=_row_spec(ts, d), out_shape=jax.ShapeDtypeStruct((s, d), BF16),
        compiler_params=_params("parallel"), name=name)(x, g, sc, sh)


def _norm_mod_bwd(name, dh, x, g, sc, dres):
    s, d = x.shape
    ts = min(ROW_TILE, s)

    def body(dh_ref, x_ref, g_ref, sc_ref, dres_ref, dx_ref, dsh_ref, dsc_ref, dg_ref):
        @pl.when(pl.program_id(0) == 0)
        def _():
            dsh_ref[...] = jnp.zeros_like(dsh_ref)
            dsc_ref[...] = jnp.zeros_like(dsc_ref)
            dg_ref[...] = jnp.zeros_like(dg_ref)

        xv = x_ref[...]
        r = lax.rsqrt(jnp.mean(xv * xv, axis=-1, keepdims=True) + EPS)
        xhat = xv * r
        dhv = dh_ref[...]
        gv = g_ref[...]
        opsc = 1.0 + sc_ref[...]
        dxhat = dhv * (gv * opsc)
        mdot = jnp.mean(dxhat * xhat, axis=-1, keepdims=True)
        dx_ref[...] = dres_ref[...] + r * (dxhat - xhat * mdot)
        dhx = dhv * xhat
        dsh_ref[...] += jnp.sum(dhv, axis=0, keepdims=True)
        dsc_ref[...] += jnp.sum(dhx * gv, axis=0, keepdims=True)
        dg_ref[...] += jnp.sum(dhx * opsc, axis=0, keepdims=True)

    vec = jax.ShapeDtypeStruct((1, d), F32)
    return pl.pallas_call(
        body, grid=(s // ts,),
        in_specs=[_row_spec(ts, d), _row_spec(ts, d), _vec_spec(d), _vec_spec(d), _row_spec(ts, d)],
        out_specs=[_row_spec(ts, d), _vec_spec(d), _vec_spec(d), _vec_spec(d)],
        out_shape=[jax.ShapeDtypeStruct((s, d), F32), vec, vec, vec],
        compiler_params=_params("arbitrary"), name=name)(dh, x, g, sc, dres)


def _gate_bwd(name, dx, m, gt):
    s, d = dx.shape
    ts = min(ROW_TILE, s)

    def body(dx_ref, m_ref, gt_ref, dm_ref, dgt_ref):
        @pl.when(pl.program_id(0) == 0)
        def _():
            dgt_ref[...] = jnp.zeros_like(dgt_ref)

        dxv = dx_ref[...]
        dm_ref[...] = (dxv * gt_ref[...]).astype(BF16)
        dgt_ref[...] += jnp.sum(dxv * m_ref[...].astype(F32), axis=0, keepdims=True)

    return pl.pallas_call(
        body, grid=(s // ts,), in_specs=[_row_spec(ts, d), _row_spec(ts, d), _vec_spec(d)],
        out_specs=[_row_spec(ts, d), _vec_spec(d)],
        out_shape=[jax.ShapeDtypeStruct((s, d), BF16), jax.ShapeDtypeStruct((1, d), F32)],
        compiler_params=_params("arbitrary"), name=name)(dx, m, gt)


def _loss_grad(name, y, target):
    s, d = y.shape
    ts = min(ROW_TILE, s)

    def body(y_ref, t_ref, dy_ref, sq_ref):
        @pl.when(pl.program_id(0) == 0)
        def _():
            sq_ref[...] = jnp.zeros_like(sq_ref)

        e = y_ref[...] - t_ref[...]
        dy_ref[...] = e * (1.0 / d)
        sq_ref[...] += jnp.sum(e * e, axis=0, keepdims=True)

    return pl.pallas_call(
        body, grid=(s // ts,), in_specs=[_row_spec(ts, d), _row_spec(ts, d)],
        out_specs=[_row_spec(ts, d), _vec_spec(d)],
        out_shape=[jax.ShapeDtypeStruct((s, d), F32), jax.ShapeDtypeStruct((1, d), F32)],
        compiler_params=_params("arbitrary"), name=name)(y, target)


def _adamw(name, w, g, m, v):
    r, c = w.shape
    tr = r
    for cand in (512, 256, 128, 64, 32, 16, 8):
        if r % cand == 0 and cand * c * 4 <= 2 * 1024 * 1024:
            tr = cand
            break

    def body(w_ref, g_ref, m_ref, v_ref, d_ref, mo_ref, vo_ref):
        gv = g_ref[...]
        m2 = ADAM_B1 * m_ref[...] + (1.0 - ADAM_B1) * gv
        v2 = ADAM_B2 * v_ref[...] + (1.0 - ADAM_B2) * (gv * gv)
        m_hat = m2 / (1.0 - ADAM_B1 ** ADAM_STEP)
        v_hat = v2 / (1.0 - ADAM_B2 ** ADAM_STEP)
        d_ref[...] = -ADAM_LR * (m_hat / (jnp.sqrt(v_hat) + ADAM_EPS) + ADAM_WD * w_ref[...])
        mo_ref[...] = m2
        vo_ref[...] = v2

    spec = pl.BlockSpec((tr, c), lambda i: (i, 0))
    shp = jax.ShapeDtypeStruct((r, c), F32)
    return pl.pallas_call(
        body, grid=(r // tr,), in_specs=[spec] * 4, out_specs=[spec] * 3, out_shape=[shp] * 3,
        compiler_params=_params("parallel"), name=name)(w, g, m, v)


def _sum_rows8(name, x):
    n, r, c = x.shape

    def body(x_ref, o_ref):
        acc = x_ref[0]
        for i in range(1, n):
            acc = acc + x_ref[i]
        o_ref[...] = acc

    return pl.pallas_call(
        body, out_shape=jax.ShapeDtypeStruct((r, c), F32),
        in_specs=[pl.BlockSpec(memory_space=pltpu.VMEM)], out_specs=pl.BlockSpec(memory_space=pltpu.VMEM),
        name=name)(x)


def _outer_sum(name, ct, dm):
    d, n_seq = ct.shape
    n = dm.shape[1]
    tr = min(256, d)

    def body(ct_ref, dm_ref, o_ref):
        acc = ct_ref[:, 0:1] * dm_ref[0:1, :]
        for s_i in range(1, n_seq):
            acc = acc + ct_ref[:, s_i:s_i + 1] * dm_ref[s_i:s_i + 1, :]
        o_ref[...] = acc

    return pl.pallas_call(
        body, grid=(d // tr,),
        in_specs=[pl.BlockSpec((tr, n_seq), lambda i: (i, 0)), pl.BlockSpec((n_seq, n), lambda i: (0, 0))],
        out_specs=pl.BlockSpec((tr, n), lambda i: (i, 0)), out_shape=jax.ShapeDtypeStruct((d, n), F32),
        compiler_params=_params("parallel"), name=name)(ct, dm)


def _ada_mod(name, c_act, w_ada_l, bias):
    n_seq, d = c_act.shape
    n = w_ada_l.shape[1]
    tn = _tile(n, 768)

    def epi(acc, b):
        return (acc + b,)

    return _matmul(name, c_act, w_ada_l, dims=NN, grid=(1, n // tn, 1),
                   a_spec=pl.BlockSpec((n_seq, d), lambda i, j, kk: (0, 0)),
                   b_spec=pl.BlockSpec((d, tn), lambda i, j, kk: (0, j)),
                   out_shapes=[jax.ShapeDtypeStruct((n_seq, n), F32)],
                   out_specs=[pl.BlockSpec((n_seq, tn), lambda i, j, kk: (0, j))], acc_shape=(n_seq, tn),
                   epi=epi, extras=(bias,), extra_specs=(pl.BlockSpec((1, tn), lambda i, j, kk: (0, j)),))[0][0]


def _silu_rows(name, c_row):
    d = c_row.shape[1]

    def body(c_ref, o_ref):
        cv = c_ref[...]
        o_ref[...] = jnp.broadcast_to(cv / (1.0 + jnp.exp(-cv)), (8, d))

    return pl.pallas_call(
        body, out_shape=jax.ShapeDtypeStruct((8, d), F32),
        in_specs=[pl.BlockSpec(memory_space=pltpu.VMEM)], out_specs=pl.BlockSpec(memory_space=pltpu.VMEM),
        name=name)(c_row)


def _sb_masks(t):
    row = lax.broadcasted_iota(jnp.int32, (t, t), 0)
    col = lax.broadcasted_iota(jnp.int32, (t, t), 1)
    strict = col < row
    tri = (row > col).astype(BF16)
    return strict, tri


def _col_minus_row(t):
    return lax.broadcasted_iota(jnp.int32, (t, t), 1) - lax.broadcasted_iota(jnp.int32, (t, t), 0)


def _sweep_left(step, jd, carry, runs_of):
    def alive(c):
        runs = runs_of(c)
        top = runs[0]
        for r in runs[1:]:
            top = jnp.maximum(top, r)
        return jnp.max(top)

    def cond(state):
        j, _, top = state
        return jnp.logical_and(j >= 0, top > DEAD_LOG)

    def body(state):
        j, c, _ = state
        c = step(j, c, False)
        return j - 1, c, alive(c)

    return lax.while_loop(cond, body, (jd - 1, carry, alive(carry)))[1]


def _log_sigmoids(z):
    sp = jnp.log(1.0 + jnp.exp(-jnp.abs(z)))
    return jnp.minimum(z, 0.0) - sp, jnp.minimum(-z, 0.0) - sp, sp


def _sb_fwd(name, proj, n_pairs, gather=()):
    s = proj.shape[0]
    t = min(SB_TILE, s)
    nq = s // t
    n_g = len(gather)

    heads = [slice(h * HEAD_DIM, (h + 1) * HEAD_DIM) for h in range(LANES // HEAD_DIM)]

    assert nq % 2 == 0

    def body(*refs):
        q_ref, k_ref, v_ref = refs[:3]
        o_ref = refs[3 + n_g]
        g_refs, sems = refs[4 + n_g:4 + 2 * n_g], refs[4 + 2 * n_g:]
        _ride_gather((n_pairs, nq), g_refs, sems, lambda: _sb_fwd_body(q_ref, k_ref, v_ref, o_ref))

    def _sb_fwd_body(q_ref, k_ref, v_ref, o_ref):
        qi = pl.program_id(1)
        _, tri = _sb_masks(t)
        diff = _col_minus_row(t)
        qs = [(q_ref[:, sl] * SCALE).astype(BF16) for sl in heads]
        jd = qi // 2
        chains = [(sb, h) for sb in (1, 0) for h in range(len(heads))]

        def step(j, carry, masked):
            ks = [pl.multiple_of((2 * j + sb) * t, t) for sb, _ in chains]
            k_ = [k_ref[pl.ds(ks[i], t), heads[h]].astype(BF16) for i, (_, h) in enumerate(chains)]
            v_ = [v_ref[pl.ds(ks[i], t), heads[h]].astype(BF16) for i, (_, h) in enumerate(chains)]
            zs = [lax.dot_general(qs[h], k_[i], NT, preferred_element_type=F32) for i, (_, h) in enumerate(chains)]
            lbs, lms, keeps = [], [], []
            for (sb, _), z in zip(chains, zs):
                lb, lm, _ = _log_sigmoids(z)
                keep = None
                if masked:
                    keep = diff < (qi - 2 * j - sb) * t
                    lm = jnp.where(keep, lm, 0.0)
                lbs.append(lb)
                lms.append(lm)
                keeps.append(keep)
            css = [_split_dot(lm, tri, 2) for lm in lms]
            sums = [jnp.sum(lm, axis=1, keepdims=True) for lm in lms]
            ws = []
            for i, (sb, h) in enumerate(chains):
                run = carry[h][1] if sb == 1 else carry[h][1] + sums[h]
                w = jnp.exp(lbs[i] + css[i] + run)
                if masked:
                    w = jnp.where(keeps[i], w, 0.0)
                ws.append(w.astype(BF16))
            pvs = [jnp.dot(w, v, preferred_element_type=F32) for w, v in zip(ws, v_)]
            nh = len(heads)
            return tuple((carry[h][0] + pvs[h] + pvs[nh + h], carry[h][1] + sums[h] + sums[nh + h]) for h in range(nh))

        init = tuple((jnp.zeros((t, HEAD_DIM), F32), jnp.zeros((t, 1), F32)) for _ in heads)
        carry = step(jd, init, True)
        carry = _sweep_left(step, jd, carry, lambda c: [run for _, run in c])
        for (acc, _), sl in zip(carry, heads):
            o_ref[:, sl] = acc

    outs = pl.pallas_call(
        body, grid=(n_pairs, nq),
        in_specs=[pl.BlockSpec((t, LANES), lambda p, i: (i, p)),
                  pl.BlockSpec((s, LANES), lambda p, i: (0, n_pairs + p)),
                  pl.BlockSpec((s, LANES), lambda p, i: (0, 2 * n_pairs + p))] + [HBM_SPEC] * n_g,
        out_specs=[pl.BlockSpec((t, LANES), lambda p, i: (i, p))] + [HBM_SPEC] * n_g,
        out_shape=[jax.ShapeDtypeStruct((s, 2 * n_pairs * LANES), F32)]
        + [jax.ShapeDtypeStruct(g.shape, g.dtype) for g in gather],
        input_output_aliases={3 + i: 1 + i for i in range(n_g)},
        scratch_shapes=_gather_sems(n_g) if n_g else [],
        compiler_params=_params("arbitrary", "arbitrary"), name=name)(proj, proj, proj, *gather)
    return outs[0], list(outs[1:])


def _sb_bwd(name, proj, o_sb, dmixed, n_pairs, exchange=()):
    s = proj.shape[0]
    t = min(SB_TILE, s)
    nq = s // t
    n_x = len(exchange)
    heads = [slice(h * HEAD_DIM, (h + 1) * HEAD_DIM) for h in range(LANES // HEAD_DIM)]

    def body(*refs):
        x_in, x_out, sems = refs[5:5 + n_x], refs[8 + n_x:8 + 2 * n_x], refs[8 + 2 * n_x:]
        if n_x:
            @pl.when(_grid_edge((n_pairs, nq), False))
            def _():
                for cp in _exchange_copies(x_in, x_out, *sems):
                    cp.start()

        _sb_bwd_body(*refs[:5], *refs[5 + n_x:8 + n_x])
        if n_x:
            @pl.when(_grid_edge((n_pairs, nq), True))
            def _():
                for cp in _exchange_copies(x_in, x_out, *sems):
                    cp.wait()

    def _sb_bwd_body(q_ref, k_ref, v_ref, o_ref, do_ref, dq_ref, dk_ref, dv_ref):
        qi = pl.program_id(1)

        @pl.when(qi == 0)
        def _():
            dk_ref[...] = jnp.zeros_like(dk_ref)
            dv_ref[...] = jnp.zeros_like(dv_ref)

        _, tri = _sb_masks(t)
        diff = _col_minus_row(t)
        qs = [(q_ref[:, sl] * SCALE).astype(BF16) for sl in heads]
        dos = [do_ref[:, sl].astype(BF16) for sl in heads]
        dsums = [jnp.sum(do.astype(F32) * o_ref[:, sl], axis=1, keepdims=True) for do, sl in zip(dos, heads)]
        jd = qi // 2
        nh = len(heads)
        chains = [(sb, h) for sb in (1, 0) for h in range(nh)]

        def step(j, carry, masked):
            ks = [pl.multiple_of((2 * j + sb) * t, t) for sb, _ in chains]
            k_ = [k_ref[pl.ds(ks[i], t), heads[h]].astype(BF16) for i, (_, h) in enumerate(chains)]
            v_ = [v_ref[pl.ds(ks[i], t), heads[h]].astype(BF16) for i, (_, h) in enumerate(chains)]
            zs = [lax.dot_general(qs[h], k_[i], NT, preferred_element_type=F32) for i, (_, h) in enumerate(chains)]
            dws = [lax.dot_general(dos[h], v_[i], NT, preferred_element_type=F32) for i, (_, h) in enumerate(chains)]
            lbs, lms, betas, ombs, keeps = [], [], [], [], []
            for (sb, _), z in zip(chains, zs):
                ez = jnp.exp(-jnp.abs(z))
                lb = jnp.minimum(z, 0.0) - jnp.log(1.0 + ez)
                lm = lb - z
                betas.append(jnp.exp(lb))
                ombs.append(jnp.exp(lm))
                keep = None
                if masked:
                    keep = diff < (qi - 2 * j - sb) * t
                    lm = jnp.where(keep, lm, 0.0)
                lbs.append(lb)
                lms.append(lm)
                keeps.append(keep)
            css = [_split_dot(lm, tri, 2) for lm in lms]
            sums_f = [jnp.sum(lm, axis=1, keepdims=True) for lm in lms]
            wbs, es = [], []
            for i, (sb, h) in enumerate(chains):
                run_f = carry[h][1] if sb == 1 else carry[h][1] + sums_f[h]
                w = jnp.exp(lbs[i] + css[i] + run_f)
                if masked:
                    w = jnp.where(keeps[i], w, 0.0)
                wb = w.astype(BF16)
                wbs.append(wb)
                es.append(wb.astype(F32) * dws[i])
            for i, (_, h) in enumerate(chains):
                dv_ref[pl.ds(ks[i], t), heads[h]] += lax.dot_general(wbs[i], dos[h], TN, preferred_element_type=F32)
            ecs = [_split_dot(e, tri, 2) for e in es]
            sums_e = [jnp.sum(e, axis=1, keepdims=True) for e in es]
            dzbs = []
            for i, (sb, h) in enumerate(chains):
                run_e = carry[h][2] if sb == 1 else carry[h][2] + sums_e[h]
                left = dsums[h] - (ecs[i] + es[i] + run_e)
                dz = es[i] * ombs[i] - left * betas[i]
                if masked:
                    dz = jnp.where(keeps[i], dz, 0.0)
                dzbs.append(dz.astype(BF16))
            dqs = []
            for i, (_, h) in enumerate(chains):
                dk_ref[pl.ds(ks[i], t), heads[h]] += lax.dot_general(dzbs[i], qs[h], TN, preferred_element_type=F32)
                dqs.append(jnp.dot(dzbs[i], k_[i], preferred_element_type=F32))
            return tuple((carry[h][0] + dqs[h] + dqs[nh + h], carry[h][1] + sums_f[h] + sums_f[nh + h],
                          carry[h][2] + sums_e[h] + sums_e[nh + h]) for h in range(nh))

        zero = jnp.zeros((t, 1), F32)
        init = tuple((jnp.zeros((t, HEAD_DIM), F32), zero, zero) for _ in heads)
        carry = step(jd, init, True)
        carry = _sweep_left(step, jd, carry, lambda c: [run_f for _, run_f, _ in c])
        for (dq, _, _), sl in zip(carry, heads):
            dq_ref[:, sl] = dq * SCALE

    blk = pl.BlockSpec((t, LANES), lambda p, i: (i, p))
    full = pl.BlockSpec((s, LANES), lambda p, i: (0, p))
    shp = jax.ShapeDtypeStruct((s, n_pairs * LANES), F32)
    outs = pl.pallas_call(
        body, grid=(n_pairs, nq),
        in_specs=[blk,
                  pl.BlockSpec((s, LANES), lambda p, i: (0, n_pairs + p)),
                  pl.BlockSpec((s, LANES), lambda p, i: (0, 2 * n_pairs + p)),
                  blk, blk] + [HBM_SPEC] * n_x,
        out_specs=[blk, full, full] + [HBM_SPEC] * n_x,
        out_shape=[shp, shp, shp] + [jax.ShapeDtypeStruct((3,) + p.shape[1:], p.dtype) for p in exchange],
        scratch_shapes=_exchange_sems(n_x) if n_x else [],
        compiler_params=_params("arbitrary", "arbitrary"), name=name)(proj, proj, proj, o_sb, dmixed, *exchange)
    return outs[0], outs[1], outs[2], list(outs[3:])


def _group_mean_matrix():
    row = lax.broadcasted_iota(jnp.int32, (LANES, LANES), 0)
    col = lax.broadcasted_iota(jnp.int32, (LANES, LANES), 1)
    same_head = (row < HEAD_DIM) == (col < HEAD_DIM)
    return jnp.where(same_head, 1.0 / HEAD_DIM, 0.0).astype(BF16)


def _qk_norm(name, proj, gq2, gk2, n_pairs):
    s = proj.shape[0]
    ts = min(ROW_TILE, s)
    pb = PAD // ts
    c0 = 3 * n_pairs

    def body(q_ref, k_ref, v_ref, gq_ref, gk_ref, qn_ref, kn_ref, vp_ref):
        r_i = pl.program_id(1)

        @pl.when(r_i < pb)
        def _():
            qn_ref[...] = jnp.zeros_like(qn_ref)
            kn_ref[...] = jnp.zeros_like(kn_ref)
            vp_ref[...] = jnp.zeros_like(vp_ref)

        @pl.when(r_i >= pb)
        def _():
            gm = _group_mean_matrix()
            for src, g_ref, dst in ((q_ref, gq_ref, qn_ref), (k_ref, gk_ref, kn_ref)):
                xv = src[...]
                r = lax.rsqrt(_split_dot(xv * xv, gm, 2) + EPS)
                dst[...] = (xv * r * g_ref[...]).astype(BF16)
            vp_ref[...] = v_ref[...].astype(BF16)

    def in_spec(off):
        return pl.BlockSpec((ts, LANES), lambda p, i: (jnp.maximum(i - pb, 0), c0 + off * n_pairs + p))

    gspec = pl.BlockSpec((1, LANES), lambda p, i: (0, 0))
    ospec = pl.BlockSpec((ts, LANES), lambda p, i: (i, p))
    shp = jax.ShapeDtypeStruct((s + PAD, n_pairs * LANES), BF16)
    return pl.pallas_call(
        body, grid=(n_pairs, (s + PAD) // ts),
        in_specs=[in_spec(0), in_spec(1), in_spec(2), gspec, gspec],
        out_specs=[ospec, ospec, ospec], out_shape=[shp, shp, shp],
        compiler_params=_params("parallel", "arbitrary"), name=name)(proj, proj, proj, gq2, gk2)


def _qk_norm_bwd(name, dqn, dkn, proj, gq2, gk2, n_pairs):
    s = proj.shape[0]
    ts = min(ROW_TILE, s)
    pb = PAD // ts
    c0 = 3 * n_pairs
    n_r = s // ts

    def body(dqn_ref, dkn_ref, q_ref, k_ref, gq_ref, gk_ref, dq_ref, dk_ref, dgq_ref, dgk_ref):
        first = jnp.logical_and(pl.program_id(0) == 0, pl.program_id(1) == 0)
        last = jnp.logical_and(pl.program_id(0) == n_pairs - 1, pl.program_id(1) == n_r - 1)

        @pl.when(first)
        def _():
            dgq_ref[...] = jnp.zeros_like(dgq_ref)
            dgk_ref[...] = jnp.zeros_like(dgk_ref)

        gm = _group_mean_matrix()
        for dn_ref, x_ref, g_ref, dx_ref, dg_ref in ((dqn_ref, q_ref, gq_ref, dq_ref, dgq_ref),
                                                     (dkn_ref, k_ref, gk_ref, dk_ref, dgk_ref)):
            xv = x_ref[...]
            r = lax.rsqrt(_split_dot(xv * xv, gm, 2) + EPS)
            xhat = xv * r
            dn = dn_ref[...]
            tg = dn * g_ref[...]
            md = _split_dot(tg * xhat, gm, 2)
            dx_ref[...] = (r * (tg - xhat * md)).astype(BF16)
            dg_ref[...] += jnp.sum(dn * xhat, axis=0, keepdims=True)

        @pl.when(last)
        def _():
            for dg_ref in (dgq_ref, dgk_ref):
                gv = dg_ref[...]
                dg_ref[...] = gv + pltpu.roll(gv, HEAD_DIM, axis=1)

    def x_spec(off):
        return pl.BlockSpec((ts, LANES), lambda p, i: (i, c0 + off * n_pairs + p))

    dn_q = pl.BlockSpec((ts, LANES), lambda p, i: (i, p))
    dn_k = pl.BlockSpec((ts, LANES), lambda p, i: (i + pb, p))
    gspec = pl.BlockSpec((1, LANES), lambda p, i: (0, 0))
    ospec = pl.BlockSpec((ts, LANES), lambda p, i: (i, p))
    shp = jax.ShapeDtypeStruct((s, n_pairs * LANES), BF16)
    vec = jax.ShapeDtypeStruct((1, LANES), F32)
    return pl.pallas_call(
        body, grid=(n_pairs, n_r),
        in_specs=[dn_q, dn_k, x_spec(0), x_spec(1), gspec, gspec],
        out_specs=[ospec, ospec, gspec, gspec], out_shape=[shp, shp, vec, vec],
        compiler_params=_params("arbitrary", "arbitrary"), name=name)(dqn, dkn, proj, proj, gq2, gk2)


def _head_blocks(x2):
    lane = lax.broadcasted_iota(jnp.int32, x2.shape, 1)
    zero = jnp.zeros_like(x2)
    return jnp.concatenate([jnp.where(lane < HEAD_DIM, x2, zero), jnp.where(lane >= HEAD_DIM, x2, zero)], axis=0)


def _head_diag(r):
    lane = lax.broadcasted_iota(jnp.int32, (CHUNK, LANES), 1)
    return jnp.where(lane < HEAD_DIM, r[:CHUNK], r[CHUNK:])


def _ca_probs(kw, qb, bias_t, n):
    sc = lax.dot_general(kw, qb, NT, preferred_element_type=F32) * SCALE + bias_t
    jpos = lax.broadcasted_iota(jnp.int32, (BAND, LANES), 0)
    sc = jnp.where(n * CHUNK + jpos >= PAD, sc, NEG_INF)
    p = jnp.exp(sc - jnp.max(sc, axis=0, keepdims=True))
    return p * (1.0 / jnp.sum(p, axis=0, keepdims=True))


def _ca_fwd(name, qn, knp, vp, bias_tt, n_pairs, mixed, gather=()):
    sp_rows = knp.shape[0]
    s = sp_rows - PAD
    t = min(CA_TILE, s)
    cpb = t // CHUNK
    pb = PAD // t

    n_g = len(gather)
    grid = (n_pairs, s // t)

    def body(*refs):
        q_ref, k_ref, v_ref, b_ref = refs[:4]
        o_ref = refs[5 + n_g]
        g_refs, sems = refs[6 + n_g:6 + 2 * n_g], refs[6 + 2 * n_g:]

        def compute():
            qi = pl.program_id(1)
            wss = [pl.multiple_of((qi * cpb + ci) * CHUNK, CHUNK) for ci in range(cpb)]
            ps = [_ca_probs(k_ref[pl.ds(wss[ci], BAND), :], _head_blocks(q_ref[ci * CHUNK:(ci + 1) * CHUNK, :]),
                            b_ref[...], qi * cpb + ci) for ci in range(cpb)]
            for ci in range(cpb):
                r = lax.dot_general(ps[ci].astype(BF16), v_ref[pl.ds(wss[ci], BAND), :], TN,
                                    preferred_element_type=F32)
                o_ref[ci * CHUNK:(ci + 1) * CHUNK, :] = _head_diag(r)

        _ride_gather(grid, g_refs, sems, compute)

    full = pl.BlockSpec((sp_rows, LANES), lambda p, i: (0, p))
    outs = pl.pallas_call(
        body, grid=grid,
        in_specs=[pl.BlockSpec((t, LANES), lambda p, i: (i + pb, p)), full, full,
                  pl.BlockSpec((None, BAND, LANES), lambda p, i: (p, 0, 0)), pl.BlockSpec(memory_space=pl.ANY)]
        + [HBM_SPEC] * n_g,
        out_specs=[pl.BlockSpec((t, LANES), lambda p, i: (i, n_pairs + p))] + [HBM_SPEC] * n_g,
        out_shape=[jax.ShapeDtypeStruct(mixed.shape, mixed.dtype)]
        + [jax.ShapeDtypeStruct(g.shape, g.dtype) for g in gather],
        input_output_aliases={4: 0, **{5 + i: 1 + i for i in range(n_g)}},
        scratch_shapes=_gather_sems(n_g) if n_g else [],
        compiler_params=_params("arbitrary", "arbitrary"), name=name)(qn, knp, vp, bias_tt, mixed, *gather)
    return outs[0], list(outs[1:])


def _ca_bwd(name, qn, knp, vp, bias_tt, dmixed, n_pairs, col0):
    sp_rows = knp.shape[0]
    s = sp_rows - PAD
    t = min(CA_TILE, s)
    cpb = t // CHUNK
    pb = PAD // t

    def body(q_ref, k_ref, v_ref, b_ref, do_ref, dq_ref, dk_ref, dv_ref, db_ref):
        qi = pl.program_id(1)

        @pl.when(qi == 0)
        def _():
            dk_ref[...] = jnp.zeros_like(dk_ref)
            dv_ref[...] = jnp.zeros_like(dv_ref)
            db_ref[...] = jnp.zeros_like(db_ref)

        wss = [pl.multiple_of((qi * cpb + ci) * CHUNK, CHUNK) for ci in range(cpb)]
        kws = [k_ref[pl.ds(ws, BAND), :] for ws in wss]
        qbs = [_head_blocks(q_ref[ci * CHUNK:(ci + 1) * CHUNK, :]) for ci in range(cpb)]
        dbs = [_head_blocks(do_ref[ci * CHUNK:(ci + 1) * CHUNK, :].astype(BF16)) for ci in range(cpb)]
        ps = [_ca_probs(kws[ci], qbs[ci], b_ref[...], qi * cpb + ci) for ci in range(cpb)]
        dps = [lax.dot_general(v_ref[pl.ds(wss[ci], BAND), :], dbs[ci], NT, preferred_element_type=F32)
               for ci in range(cpb)]
        dss = [p * (dp - jnp.sum(p * dp, axis=0, keepdims=True)) for p, dp in zip(ps, dps)]
        for ci in range(cpb):
            dsb = dss[ci].astype(BF16)
            dv_ref[pl.ds(wss[ci], BAND), :] += jnp.dot(ps[ci].astype(BF16), dbs[ci], preferred_element_type=F32)
            dk_ref[pl.ds(wss[ci], BAND), :] += jnp.dot(dsb, qbs[ci], preferred_element_type=F32) * SCALE
            r = lax.dot_general(dsb, kws[ci], TN, preferred_element_type=F32)
            dq_ref[ci * CHUNK:(ci + 1) * CHUNK, :] = _head_diag(r) * SCALE
        total = dss[0]
        for ci in range(1, cpb):
            total = total + dss[ci]
        db_ref[...] += total

    full_in = pl.BlockSpec((sp_rows, LANES), lambda p, i: (0, p))
    btile = pl.BlockSpec((None, BAND, LANES), lambda p, i: (p, 0, 0))
    pad_shape = jax.ShapeDtypeStruct((sp_rows, n_pairs * LANES), F32)
    return pl.pallas_call(
        body, grid=(n_pairs, s // t),
        in_specs=[pl.BlockSpec((t, LANES), lambda p, i: (i + pb, p)), full_in, full_in, btile,
                  pl.BlockSpec((t, LANES), lambda p, i: (i, col0 + p))],
        out_specs=[pl.BlockSpec((t, LANES), lambda p, i: (i, p)), full_in, full_in, btile],
        out_shape=[jax.ShapeDtypeStruct((s, n_pairs * LANES), F32), pad_shape, pad_shape,
                   jax.ShapeDtypeStruct(bias_tt.shape, F32)],
        compiler_params=_params("parallel", "arbitrary"), name=name)(qn, knp, vp, bias_tt, dmixed)


def _pair_transposed(tile):
    n_h = tile.shape[0]
    return jnp.transpose(tile.reshape(n_h // 2, 2, CHUNK, BAND), (0, 3, 1, 2)).reshape(n_h // 2, BAND, LANES)


def _query_major(tile_t):
    n_p = tile_t.shape[0]
    return jnp.transpose(tile_t.reshape(n_p, BAND, 2, CHUNK), (3, 0, 2, 1)).reshape(CHUNK, 2 * n_p, BAND)


def _bias_tile(rb):
    n_h = rb.shape[0]
    far = jnp.broadcast_to(rb[:, 2 * REL_CLIP:], (n_h, PAD + CHUNK - REL_CLIP))
    near = rb[:, REL_CLIP - (CHUNK - 1):2 * REL_CLIP][:, ::-1]
    ext = jnp.concatenate([far, near], axis=1)
    return jnp.stack([ext[:, CHUNK - 1 - i:CHUNK - 1 - i + BAND] for i in range(CHUNK)], axis=1)


def _rel_bias_grad(name, db_t):
    _, n_h, _ = db_t.shape
    n_out = 3 * LANES
    assert CHUNK == 64 and REL_CLIP == 128 and LEFT_CHUNKS == 8

    def body(z_ref, o_ref):
        a_idx = lax.broadcasted_iota(jnp.int32, (LANES, n_out), 0)
        r_idx = lax.broadcasted_iota(jnp.int32, (LANES, n_out), 1)
        out = jnp.zeros((n_h, n_out), F32)
        for mm, base in ((0, 191), (1, 255), (2, 319)):
            b = LEFT_CHUNKS - mm
            acc = jnp.zeros((n_h, LANES), F32)
            for i in range(CHUNK):
                row = z_ref[i, :, b * CHUNK:(b + 1) * CHUNK]
                rowp = jnp.concatenate([row, jnp.zeros((n_h, LANES - CHUNK), F32)], axis=1)
                acc = acc + pltpu.roll(rowp, CHUNK - 1 - i, axis=1)
            place = (r_idx == jnp.minimum(base - a_idx, 2 * REL_CLIP)).astype(BF16)
            out = out + _split_dot(acc, place, 3)
        far = jnp.zeros((n_h, 1), F32)
        for i in range(CHUNK):
            far = far + jnp.sum(z_ref[i, :, 0:(LEFT_CHUNKS - 2) * CHUNK], axis=1, keepdims=True)
        lane = lax.broadcasted_iota(jnp.int32, (n_h, n_out), 1)
        o_ref[...] = out + jnp.where(lane == 2 * REL_CLIP, far, 0.0)

    return pl.pallas_call(
        body, out_shape=jax.ShapeDtypeStruct((n_h, n_out), F32),
        in_specs=[pl.BlockSpec(memory_space=pltpu.VMEM)], out_specs=pl.BlockSpec(memory_space=pltpu.VMEM),
        name=name)(db_t)


def _mesh_pos():
    return lax.axis_index("x"), lax.axis_index("y"), lax.axis_index("c")


def _other_chips(x, y):
    return [(1 - x, y), (x, 1 - y), (1 - x, 1 - y)]


def _all_gather_small(name, blk):
    m_per, n = blk.shape

    def body(x_ref, out_ref, send_sems, recv_sems, local_sem):
        x, y, c = _mesh_pos()
        me, sibling = (x, y, c), (x, y, 1 - c)
        chips = _other_chips(x, y)

        def rows(px, py, pc):
            return out_ref.at[pl.ds((4 * px + 2 * py + pc) * m_per, m_per), :]

        def copy(k, block, to, src=None):
            return pltpu.make_async_remote_copy(
                src_ref=rows(*block) if src is None else src, dst_ref=rows(*block),
                send_sem=send_sems.at[k], recv_sem=recv_sems.at[k], device_id=to, device_id_type=MESH)

        mine = pltpu.make_async_copy(x_ref, rows(*me), local_sem)
        mine.start()
        first = [copy(0, me, sibling, src=x_ref)]
        first += [copy(1 + j, me, (*chip, c), src=x_ref) for j, chip in enumerate(chips)]
        for cp in first:
            cp.start()
        passed = [copy(4 + j, (*chip, c), sibling) for j, chip in enumerate(chips)]
        for j, chip in enumerate(chips):
            copy(1 + j, (*chip, c), me).wait_recv()
            passed[j].start()
        copy(0, sibling, me).wait_recv()
        for j, chip in enumerate(chips):
            copy(4 + j, (*chip, 1 - c), me).wait_recv()
        for cp in first + passed:
            cp.wait_send()
        mine.wait()

    return pl.pallas_call(
        body, out_shape=jax.ShapeDtypeStruct((N_DEV * m_per, n), blk.dtype),
        in_specs=[pl.BlockSpec(memory_space=pltpu.VMEM)], out_specs=pl.BlockSpec(memory_space=pltpu.VMEM),
        scratch_shapes=[pltpu.SemaphoreType.DMA((7,)), pltpu.SemaphoreType.DMA((7,)), pltpu.SemaphoreType.DMA],
        compiler_params=pltpu.CompilerParams(vmem_limit_bytes=VMEM_LIMIT), name=name)(blk)


HBM_SPEC = pl.BlockSpec(memory_space=pltpu.HBM)


def _gather_copies(bufs, send_sems, recv_sems, phase):
    x, y, c = _mesh_pos()
    my_j = 2 * x + y
    starts, lands = [], []
    for t, buf in enumerate(bufs):
        rh = buf.shape[2] // 2
        for k, (px, py) in enumerate(_other_chips(x, y)):
            their_j = 2 * px + py
            if phase == 0:
                src, to = buf.at[0, my_j, pl.ds(c * rh, rh), :], (px, py, c)
                land = buf.at[0, their_j, pl.ds(c * rh, rh), :]
            else:
                src, to = buf.at[0, their_j, pl.ds(c * rh, rh), :], (x, y, 1 - c)
                land = buf.at[0, their_j, pl.ds((1 - c) * rh, rh), :]
            sems = dict(send_sem=send_sems.at[phase, t, k], recv_sem=recv_sems.at[phase, t, k],
                        device_id=to, device_id_type=MESH)
            starts.append(pltpu.make_async_remote_copy(src_ref=src, dst_ref=src, **sems))
            lands.append(pltpu.make_async_remote_copy(src_ref=src, dst_ref=land, **sems))
    return starts, lands


def _gather_sems(n):
    return [pltpu.SemaphoreType.DMA((2, n, 3)), pltpu.SemaphoreType.DMA((2, n, 3))]


def _all_gather_weights(name, bufs, phases):
    n = len(bufs)

    def body(*refs):
        outs = refs[n:2 * n]
        send_sems, recv_sems = refs[2 * n:]
        for phase in phases:
            starts, lands = _gather_copies(outs, send_sems, recv_sems, phase)
            for cp in starts:
                cp.start()
            for cp, ld in zip(starts, lands):
                cp.wait_send()
                ld.wait_recv()

    return pl.pallas_call(
        body, out_shape=[jax.ShapeDtypeStruct(b.shape, b.dtype) for b in bufs],
        in_specs=[HBM_SPEC] * n, out_specs=[HBM_SPEC] * n, input_output_aliases={t: t for t in range(n)},
        scratch_shapes=_gather_sems(n), name=name)(*bufs)


def _gather_buffer(shard, chip):
    r, c = shard.shape
    return lax.dynamic_update_slice(jnp.zeros((1, N_CHIPS, r, c), BF16), shard.astype(BF16)[None, None],
                                    (0, chip, 0, 0))


def _exchange_sibling_halves(name, grads):
    n = len(grads)

    def body(*refs):
        srcs, outs = refs[:n], refs[n:2 * n]
        send_sems, recv_sems = refs[2 * n:]
        x, y, c = _mesh_pos()
        cps = []
        for t in range(n):
            rh = grads[t].shape[1] // 2
            cps.append(pltpu.make_async_remote_copy(
                src_ref=srcs[t].at[:, pl.ds((1 - c) * rh, rh), :], dst_ref=outs[t],
                send_sem=send_sems.at[t], recv_sem=recv_sems.at[t], device_id=(x, y, 1 - c), device_id_type=MESH))
        for cp in cps:
            cp.start()
        for cp in cps:
            cp.wait()

    out_shape = [jax.ShapeDtypeStruct((g.shape[0], g.shape[1] // 2, g.shape[2]), g.dtype) for g in grads]
    return pl.pallas_call(
        body, out_shape=out_shape, in_specs=[HBM_SPEC] * n, out_specs=[HBM_SPEC] * n,
        scratch_shapes=[pltpu.SemaphoreType.DMA((n,)), pltpu.SemaphoreType.DMA((n,))],
        name=name)(*grads)


def _add_halves(name, g, recv, c_idx):
    nb, r, c = g.shape
    rh = r // 2
    tr = rh
    for cand in (512, 256, 128, 64):
        if rh % cand == 0:
            tr = cand
            break
    g4 = g.reshape(nb, 2, rh, c)

    def body(c_ref, g_ref, r_ref, o_ref):
        o_ref[...] = (g_ref[...] + r_ref[...]).astype(BF16)

    grid_spec = pltpu.PrefetchScalarGridSpec(
        num_scalar_prefetch=1, grid=(nb, rh // tr),
        in_specs=[pl.BlockSpec((None, None, tr, c), lambda j, i, cr: (j, cr[0], i, 0)),
                  pl.BlockSpec((None, tr, c), lambda j, i, cr: (j, i, 0))],
        out_specs=pl.BlockSpec((None, tr, c), lambda j, i, cr: (j, i, 0)))
    return pl.pallas_call(
        body, grid_spec=grid_spec, out_shape=jax.ShapeDtypeStruct((nb, rh, c), BF16),
        compiler_params=_params("parallel", "parallel"), name=name)(c_idx, g4, recv)


def _exchange_copies(srcs, outs, send_sems, recv_sems):
    x, y, c = _mesh_pos()
    return [pltpu.make_async_remote_copy(
        src_ref=srcs[t].at[2 * px + py], dst_ref=outs[t].at[k], send_sem=send_sems.at[t, k],
        recv_sem=recv_sems.at[t, k], device_id=(px, py, c), device_id_type=MESH)
        for t in range(len(srcs)) for k, (px, py) in enumerate(_other_chips(x, y))]


def _exchange_sems(n):
    return [pltpu.SemaphoreType.DMA((n, 3)), pltpu.SemaphoreType.DMA((n, 3))]


def _exchange_chips(name, parts):
    n = len(parts)

    def body(*refs):
        cps = _exchange_copies(refs[:n], refs[n:2 * n], *refs[2 * n:])
        for cp in cps:
            cp.start()
        for cp in cps:
            cp.wait()

    return pl.pallas_call(
        body, out_shape=[jax.ShapeDtypeStruct((3,) + p.shape[1:], p.dtype) for p in parts],
        in_specs=[HBM_SPEC] * n, out_specs=[HBM_SPEC] * n, scratch_shapes=_exchange_sems(n), name=name)(*parts)


def _sum_chips(name, own_parts, recvs, pos_idx):
    n_layers = len(own_parts)
    _, rh, c = recvs[0].shape
    tr = rh
    for cand in (512, 256, 128, 64):
        if rh % cand == 0:
            tr = cand
            break
    nb = rh // tr

    def body(pos_ref, *refs):
        own_refs, r_refs, o_ref = refs[:n_layers], refs[n_layers:2 * n_layers], refs[2 * n_layers]
        for l_i in range(n_layers):
            @pl.when(pl.program_id(0) == l_i)
            def _():
                acc = own_refs[l_i][...].astype(F32)
                for k in range(3):
                    acc = acc + r_refs[l_i][k].astype(F32)
                o_ref[...] = acc

    grid_spec = pltpu.PrefetchScalarGridSpec(
        num_scalar_prefetch=1, grid=(n_layers, nb),
        in_specs=[pl.BlockSpec((None, tr, c), lambda l, i, pr: (pr[0], i, 0)) for _ in range(n_layers)]
        + [pl.BlockSpec((3, tr, c), lambda l, i, pr: (0, i, 0)) for _ in range(n_layers)],
        out_specs=pl.BlockSpec((None, tr, c), lambda l, i, pr: (l, pr[1] * nb + i, 0)))
    return pl.pallas_call(
        body, grid_spec=grid_spec, out_shape=jax.ShapeDtypeStruct((n_layers, 2 * rh, c), F32),
        compiler_params=_params("parallel", "parallel"), name=name)(pos_idx, *own_parts, *recvs)


def _share_with_sibling(name, grads):
    n = len(grads)

    def body(*refs):
        outs = refs[n:2 * n]
        send_sems, recv_sems = refs[2 * n:]
        x, y, c = _mesh_pos()
        cps, landed = [], []
        for t in range(n):
            rh = grads[t].shape[1] // 2
            mine = outs[t].at[:, pl.ds(c * rh, rh), :]
            theirs = outs[t].at[:, pl.ds((1 - c) * rh, rh), :]
            cps.append(pltpu.make_async_remote_copy(
                src_ref=mine, dst_ref=mine, send_sem=send_sems.at[t], recv_sem=recv_sems.at[t],
                device_id=(x, y, 1 - c), device_id_type=MESH))
            landed.append(pltpu.make_async_remote_copy(
                src_ref=mine, dst_ref=theirs, send_sem=send_sems.at[t], recv_sem=recv_sems.at[t],
                device_id=(x, y, 1 - c), device_id_type=MESH))
        for cp in cps:
            cp.start()
        for cp, ld in zip(cps, landed):
            cp.wait_send()
            ld.wait_recv()

    out_shape = [jax.ShapeDtypeStruct(g.shape, g.dtype) for g in grads]
    return pl.pallas_call(
        body, out_shape=out_shape, in_specs=[HBM_SPEC] * n, out_specs=[HBM_SPEC] * n,
        input_output_aliases={t: t for t in range(n)},
        scratch_shapes=[pltpu.SemaphoreType.DMA((n,)), pltpu.SemaphoreType.DMA((n,))],
        name=name)(*grads)


def _row(v):
    return v.reshape(1, -1)


def kernel(x, c, g_norm1, w_in, g_q, g_k, rel_bias, w_o, g_norm2, w1, w2, w_ada, b_ada, loss_target, m_g_norm1, m_w_in, m_g_q, m_g_k, m_rel_bias, m_w_o, m_g_norm2, m_w1, m_w2, m_w_ada, m_b_ada, v_g_norm1, v_w_in, v_g_q, v_g_k, v_rel_bias, v_w_o, v_g_norm2, v_w1, v_w2, v_w_ada, v_b_ada):
    _, s, d = x.shape
    n_layers = g_norm1.shape[0]
    n_pairs = (d // 2) // LANES
    n_ca_heads = rel_bias.shape[1]
    n_rel = rel_bias.shape[2]
    assert n_rel == 2 * REL_CLIP + 1 and g_q.shape[1] == HEAD_DIM and n_ca_heads == 2 * n_pairs
    ada_c = w_ada.shape[2]

    ax, ay, ac = _mesh_pos()
    chip = 2 * ax + ay
    dev = 4 * ax + 2 * ay + ac
    c_idx = jnp.reshape(ac, (1,)).astype(jnp.int32)

    c_act_all = _all_gather_small("ag_c", _silu_rows("silu_c", c)).reshape(N_DEV, 8, d)[:, 0, :]
    mod_parts = []
    for l in range(n_layers):
        bias = lax.dynamic_slice_in_dim(b_ada[l], chip * ada_c, ada_c).reshape(1, ada_c)
        mod_parts.append(_ada_mod(f"ada_mod{l}", c_act_all, w_ada[l], bias))
    mod_all = _all_gather_small("ag_mod", jnp.concatenate(mod_parts, axis=1))
    mod_all = mod_all.reshape(N_CHIPS, 2, N_DEV, n_layers, ada_c)[:, 0]
    mod_all = jnp.transpose(mod_all, (1, 2, 0, 3)).reshape(N_DEV, n_layers, N_CHIPS * ada_c)
    mod = lax.dynamic_index_in_dim(mod_all, dev, axis=0, keepdims=False)

    gathered = [[_gather_buffer(w[l], chip) for w in (w_in, w_o, w1, w2)] for l in range(n_layers)]
    gathered[0][:1] = _all_gather_weights("ag_w_in_0", gathered[0][:1], (0, 1))

    xs = x[0]
    saved = []
    for l in range(n_layers):
        sh1, sc1, gt1, sh2, sc2, gt2 = [_row(mod[l, i * d:(i + 1) * d]) for i in range(6)]
        gq2 = _row(jnp.tile(g_q[l], 2))
        gk2 = _row(jnp.tile(g_k[l], 2))
        bias_t = _pair_transposed(_bias_tile(rel_bias[l]))
        h1 = _norm_mod(f"norm1_{l}", xs, _row(g_norm1[l]), sc1, sh1)
        proj = _mm_nn_w(f"proj_{l}", h1, gathered[l][0], 0, True, [F32])[0]
        o_sb, landed = _sb_fwd(f"sb_fwd_{l}", proj, n_pairs, gather=gathered[l][1:] if l == 0 else ())
        if landed:
            gathered[l][1:] = _all_gather_weights(f"ag_rest_{l}", landed, (1,))
        wg_in, wg_o, wg_1, wg_2 = gathered[l]
        nxt = gathered[l + 1] if l + 1 < n_layers else []
        qn, knp, vp = _qk_norm(f"qk_norm_{l}", proj, gq2, gk2, n_pairs)
        mixed, landed = _ca_fwd(f"ca_fwd_{l}", qn, knp, vp, bias_t, n_pairs, o_sb, gather=nxt[:2])

        def res_epi(acc, res, gt):
            return res + gt * acc, acc

        def res_specs(tm, tn):
            return (pl.BlockSpec((tm, tn), lambda i, j, kk: (i, j)), pl.BlockSpec((1, tn), lambda i, j, kk: (0, j)))

        x1, m1 = _mm_nn_w(f"attn_out_{l}", mixed, wg_o, 0, False, [F32, BF16], epi=res_epi,
                          extras=(xs, gt1), extra_specs_fn=res_specs)
        h2 = _norm_mod(f"norm2_{l}", x1, _row(g_norm2[l]), sc2, sh2)

        def act_epi(acc):
            r = jnp.maximum(acc, 0.0)
            return acc, r * r

        if l + 1 < n_layers:
            (u, a), landed_1 = _mm_nn_w(f"mlp_up_{l}", h2, wg_1, 0, True, [BF16, BF16], epi=act_epi, gather=nxt[2:3])
            (x2, m2), landed_2 = _mm_nn_w(f"mlp_down_{l}", a, wg_2, 0, False, [F32, BF16], epi=res_epi,
                                          extras=(x1, gt2), extra_specs_fn=res_specs, gather=nxt[3:])
            gathered[l + 1] = _all_gather_weights(f"ag_weights_{l + 1}", landed + landed_1 + landed_2, (1,))
        else:
            u, a = _mm_nn_w(f"mlp_up_{l}", h2, wg_1, 0, True, [BF16, BF16], epi=act_epi)
            x2, m2 = _mm_nn_w(f"mlp_down_{l}", a, wg_2, 0, False, [F32, BF16], epi=res_epi,
                              extras=(x1, gt2), extra_specs_fn=res_specs)
        saved.append(dict(x0=xs, h1=h1, proj=proj, o_sb=mixed, qn=qn, knp=knp, vp=vp, bias_t=bias_t, mixed=mixed,
                          m1=m1, x1=x1, h2=h2, u=u, a=a, m2=m2, gq2=gq2, gk2=gk2,
                          sc1=sc1, gt1=gt1, sc2=sc2, gt2=gt2))
        xs = x2

    dx, sq = _loss_grad("loss", xs, loss_target[0])
    loss_part = (0.5 * jnp.sum(sq) / d).reshape(1)

    g_in, g_o, g_1, g_2 = [None] * n_layers, [None] * n_layers, [None] * n_layers, [None] * n_layers
    dmod, dg1, dg2, dgq, dgk, drel = [], [], [], [], [], []
    partial, chip_sum, from_chip = {}, {}, {}

    def reduce_on_chip(tag, keys):
        grads = [partial[k] for k in keys]
        for k, g, r in zip(keys, grads, _exchange_sibling_halves(f"rs_sibling_{tag}", grads)):
            chip_sum[k] = _add_halves(f"rs_add_{k[0]}_{k[1]}", g, r, c_idx)

    for l in reversed(range(n_layers)):
        sv = saved[l]
        wg_in, wg_o, wg_1, wg_2 = gathered[l]
        dm2, dgt2 = _gate_bwd(f"gate2_bwd_{l}", dx, sv["m2"], sv["gt2"])

        def act_bwd_epi(acc, u_t):
            return (acc * (2.0 * jnp.maximum(u_t.astype(F32), 0.0)),)

        def tile_specs(tm, tn):
            return (pl.BlockSpec((tm, tn), lambda i, j, kk: (i, j)),)

        du = _mm_nt_w(f"mlp_down_bwd_{l}", dm2, wg_2, 0, False, BF16, epi=act_bwd_epi, extras=(sv["u"],),
                      extra_specs_fn=tile_specs)
        g_2[l] = _mm_tn(f"w2_grad_{l}", sv["a"], dm2, False)
        dh2 = _mm_nt_w(f"mlp_up_bwd_{l}", du, wg_1, 0, True, F32)
        g_1[l] = _mm_tn(f"w1_grad_{l}", sv["h2"], du, True)
        dx1, dsh2, dsc2, dgn2 = _norm_mod_bwd(f"norm2_bwd_{l}", dh2, sv["x1"], _row(g_norm2[l]), sv["sc2"], dx)
        dm1, dgt1 = _gate_bwd(f"gate1_bwd_{l}", dx1, sv["m1"], sv["gt1"])
        dmixed = _mm_nt_w(f"attn_out_bwd_{l}", dm1, wg_o, 0, False, F32)
        g_o[l] = _mm_tn(f"wo_grad_{l}", sv["mixed"], dm1, False)
        partial.update({(1, l): g_o[l], (2, l): g_1[l], (3, l): g_2[l]})
        if l == 0:
            early = sorted(partial)
            reduce_on_chip("early", early)
            dq_sb, dk_sb, dv_sb, landed = _sb_bwd(f"sb_bwd_{l}", sv["proj"], sv["o_sb"], dmixed, n_pairs,
                                                  exchange=[chip_sum[k] for k in early])
            from_chip.update(zip(early, landed))
        else:
            dq_sb, dk_sb, dv_sb, _ = _sb_bwd(f"sb_bwd_{l}", sv["proj"], sv["o_sb"], dmixed, n_pairs)
        dqn, dknp, dvp, dbias_t = _ca_bwd(f"ca_bwd_{l}", sv["qn"], sv["knp"], sv["vp"], sv["bias_t"], dmixed,
                                          n_pairs, n_pairs)
        dq_ca, dk_ca, dgq_l, dgk_l = _qk_norm_bwd(f"qk_norm_bwd_{l}", dqn, dknp, sv["proj"], sv["gq2"], sv["gk2"],
                                                  n_pairs)
        drel_l = _rel_bias_grad(f"rel_bias_grad_{l}", _query_major(dbias_t))[:, :n_rel]
        dproj = jnp.concatenate([dq_sb.astype(BF16), dk_sb.astype(BF16), dv_sb.astype(BF16), dq_ca, dk_ca,
                                 dvp[PAD:].astype(BF16)], axis=1)
        g_in[l] = _mm_tn(f"win_grad_{l}", sv["h1"], dproj, True)
        partial[(0, l)] = g_in[l]
        dh1 = _mm_nt_w(f"proj_bwd_{l}", dproj, wg_in, 0, True, F32)
        dx, dsh1, dsc1, dgn1 = _norm_mod_bwd(f"norm1_bwd_{l}", dh1, sv["x0"], _row(g_norm1[l]), sv["sc1"], dx1)
        dmod.insert(0, jnp.concatenate([dsh1, dsc1, dgt1, dsh2, dsc2, dgt2], axis=1)[0])
        dg1.insert(0, dgn1[0])
        dg2.insert(0, dgn2[0])
        dgq.insert(0, dgq_l[0, :HEAD_DIM])
        dgk.insert(0, dgk_l[0, :HEAD_DIM])
        drel.insert(0, drel_l.reshape(-1))
    grad_x = dx[None]

    small = jnp.concatenate([jnp.concatenate(dmod), jnp.concatenate(dg1), jnp.concatenate(dg2),
                             jnp.concatenate(dgq), jnp.concatenate(dgk), jnp.concatenate(drel), loss_part])
    n_small = small.shape[0]
    n_pack = -(-n_small // (8 * LANES)) * (8 * LANES)
    small = jnp.pad(small, (0, n_pack - n_small)).reshape(8, n_pack // 8)
    small_all = _all_gather_small("ag_small", small).reshape(N_DEV, 8, n_pack // 8)
    small_sum = _sum_rows8("sum_small", small_all).reshape(-1)
    dmod_all = small_all.reshape(N_DEV, n_pack)[:, :n_layers * 6 * d].reshape(N_DEV, n_layers, 6 * d)

    off = 0

    def take(n, shape):
        nonlocal off
        out = small_sum[off:off + n].reshape(shape)
        off += n
        return out

    grad_b_ada = take(n_layers * 6 * d, (n_layers, 6 * d))
    grad_g_norm1 = take(n_layers * d, (n_layers, d))
    grad_g_norm2 = take(n_layers * d, (n_layers, d))
    grad_g_q = take(n_layers * HEAD_DIM, (n_layers, HEAD_DIM))
    grad_g_k = take(n_layers * HEAD_DIM, (n_layers, HEAD_DIM))
    grad_rel_bias = take(n_layers * n_ca_heads * n_rel, (n_layers, n_ca_heads, n_rel))
    loss = take(1, ())

    c_act_t = jnp.transpose(c_act_all)
    grad_w_ada = jnp.stack([
        _outer_sum(f"wada_grad_{l}", c_act_t,
                   lax.dynamic_slice_in_dim(dmod_all[:, l, :], chip * ada_c, ada_c, axis=1))
        for l in range(n_layers)])

    late = [k for k in sorted(partial) if k not in from_chip]
    reduce_on_chip("late", late)
    from_chip.update(zip(late, _exchange_chips("rs_chips", [chip_sum[k] for k in late])))
    pos_idx = jnp.stack([chip, ac]).astype(jnp.int32)
    reduced = [_sum_chips(f"rs_sum_{w_i}", [chip_sum[(w_i, l)] for l in range(n_layers)],
                          [from_chip[(w_i, l)] for l in range(n_layers)], pos_idx) for w_i in range(4)]
    grad_w_in, grad_w_o, grad_w1, grad_w2 = _share_with_sibling("rs_share", reduced)

    def adam_big(name, w, g, m, v):
        shp = w.shape
        outs = _adamw(name, w.reshape(-1, shp[-1]), g.reshape(-1, shp[-1]), m.reshape(-1, shp[-1]),
                      v.reshape(-1, shp[-1]))
        return [o.reshape(shp) for o in outs]

    def pack(arrs):
        flat = jnp.concatenate([a.reshape(-1) for a in arrs])
        n = flat.shape[0]
        n_p = -(-n // (8 * LANES)) * (8 * LANES)
        return jnp.pad(flat, (0, n_p - n), constant_values=1.0).reshape(8, n_p // 8)

    small_w = [g_norm1, g_q, g_k, rel_bias, g_norm2, b_ada]
    small_g = [grad_g_norm1, grad_g_q, grad_g_k, grad_rel_bias, grad_g_norm2, grad_b_ada]
    small_m = [m_g_norm1, m_g_q, m_g_k, m_rel_bias, m_g_norm2, m_b_ada]
    small_v = [v_g_norm1, v_g_q, v_g_k, v_rel_bias, v_g_norm2, v_b_ada]
    packed = _adamw("adamw_small", pack(small_w), pack(small_g), pack(small_m), pack(small_v))

    def unpack(p):
        flat = p.reshape(-1)
        res, o = [], 0
        for a in small_w:
            res.append(flat[o:o + a.size].reshape(a.shape))
            o += a.size
        return res

    sd, sm, sv_ = unpack(packed[0]), unpack(packed[1]), unpack(packed[2])
    big = {
        "w_in": adam_big("adamw_w_in", w_in, grad_w_in, m_w_in, v_w_in),
        "w_o": adam_big("adamw_w_o", w_o, grad_w_o, m_w_o, v_w_o),
        "w1": adam_big("adamw_w1", w1, grad_w1, m_w1, v_w1),
        "w2": adam_big("adamw_w2", w2, grad_w2, m_w2, v_w2),
        "w_ada": adam_big("adamw_w_ada", w_ada, grad_w_ada, m_w_ada, v_w_ada),
    }

    def ordered(kind):
        sm_list = (sd, sm, sv_)[kind]
        return [sm_list[0], big["w_in"][kind], sm_list[1], sm_list[2], sm_list[3], big["w_o"][kind], sm_list[4],
                big["w1"][kind], big["w2"][kind], big["w_ada"][kind], sm_list[5]]

    grads = [grad_g_norm1, grad_w_in, grad_g_q, grad_g_k, grad_rel_bias, grad_w_o, grad_g_norm2, grad_w1, grad_w2,
             grad_w_ada, grad_b_ada]
    return (loss, grad_x, *grads, *ordered(0), *ordered(1), *ordered(2))
```

```python
import functools

import jax
import jax.numpy as jnp
from jax import lax
from jax.experimental import pallas as pl
from jax.experimental.pallas import tpu as pltpu

F32 = jnp.float32
BF16 = jnp.bfloat16
MESH = pl.DeviceIdType.MESH

EPS = 1e-6
NEG_INF = -1e30
HEAD_DIM = 64
CHUNK = 64
LEFT_CHUNKS = 8
PAD = LEFT_CHUNKS * CHUNK
BAND = PAD + CHUNK
REL_CLIP = 128
SCALE = HEAD_DIM ** -0.5
DEAD_LOG = -88.0
LANES = 128
N_CHIPS = 4
N_DEV = 8

ADAM_LR = 0.001
ADAM_B1 = 0.9
ADAM_B2 = 0.999
ADAM_EPS = 1e-08
ADAM_WD = 0.01
ADAM_STEP = 10

VMEM_LIMIT = 48 * 1024 * 1024
ROW_TILE = 512
MM_TILE = 1024
SB_TILE = 256
CA_TILE = 256

NT = (((1,), (1,)), ((), ()))
TN = (((0,), (0,)), ((), ()))
NN = (((1,), (0,)), ((), ()))


def _tile(n, pref):
    best = None
    t = LANES
    while t <= min(n, pref):
        if n % t == 0:
            best = t
        t += LANES
    return best if best is not None else n


def _params(*sem):
    return pltpu.CompilerParams(dimension_semantics=sem, vmem_limit_bytes=VMEM_LIMIT)


def _split_dot(x, t, passes):
    acc = None
    r = x
    for i in range(passes):
        hi = r.astype(BF16)
        d = jnp.dot(hi, t, preferred_element_type=F32)
        acc = d if acc is None else acc + d
        if i + 1 < passes:
            r = r - hi.astype(F32)
    return acc


def _grid_edge(grid, last):
    conds = [pl.program_id(i) == (g - 1 if last else 0) for i, g in enumerate(grid)]
    return functools.reduce(jnp.logical_and, conds)


def _ride_gather(grid, g_refs, sems, compute):
    if g_refs:
        @pl.when(_grid_edge(grid, False))
        def _():
            for cp in _gather_copies(g_refs, *sems, 0)[0]:
                cp.start()

    compute()
    if g_refs:
        @pl.when(_grid_edge(grid, True))
        def _():
            for cp, ld in zip(*_gather_copies(g_refs, *sems, 0)):
                cp.wait_send()
                ld.wait_recv()


def _matmul(name, a, b, *, dims, grid, a_spec, b_spec, out_shapes, out_specs, acc_shape,
            epi=None, extras=(), extra_specs=(), gather=()):
    nk = grid[2]
    n_ex = len(extras)
    n_out = len(out_shapes)
    n_g = len(gather)

    def body(*refs):
        a_ref, b_ref = refs[0], refs[1]
        ex = refs[2:2 + n_ex]
        outs = refs[2 + n_ex + n_g:2 + n_ex + n_g + n_out]
        g_refs = refs[2 + n_ex + n_g + n_out:2 + n_ex + 2 * n_g + n_out]
        acc = refs[2 + n_ex + 2 * n_g + n_out]
        sems = refs[3 + n_ex + 2 * n_g + n_out:]

        def compute():
            k = pl.program_id(2)

            @pl.when(k == 0)
            def _():
                acc[...] = jnp.zeros_like(acc)

            acc[...] += lax.dot_general(a_ref[...].astype(BF16), b_ref[...].astype(BF16), dims,
                                        preferred_element_type=F32)

            @pl.when(k == nk - 1)
            def _():
                res = epi(acc[...], *[e[...] for e in ex]) if epi is not None else (acc[...],)
                for o, r in zip(outs, res):
                    o[...] = r.astype(o.dtype)

        _ride_gather(grid, g_refs, sems, compute)

    outs = pl.pallas_call(
        body, grid=grid, in_specs=[a_spec, b_spec, *extra_specs] + [HBM_SPEC] * n_g,
        out_specs=list(out_specs) + [HBM_SPEC] * n_g,
        out_shape=list(out_shapes) + [jax.ShapeDtypeStruct(g.shape, g.dtype) for g in gather],
        input_output_aliases={2 + n_ex + i: n_out + i for i in range(n_g)},
        scratch_shapes=[pltpu.VMEM(acc_shape, F32)] + (_gather_sems(n_g) if n_g else []),
        compiler_params=_params(*(("arbitrary",) * 3 if n_g else ("parallel", "parallel", "arbitrary"))), name=name,
    )(a, b, *extras, *gather)
    return list(outs[:n_out]), list(outs[n_out:])


def _mm_nn_w(name, a, wg, layer, colshard, out_dtypes, epi=None, extras=(), extra_specs_fn=None, gather=()):
    m = a.shape[0]
    _, _, r, c = wg.shape
    tm = min(MM_TILE, m)
    if colshard:
        k, n = r, N_CHIPS * c
        tk, tn = k, _tile(c, MM_TILE)
        npc = c // tn
        b_spec = pl.BlockSpec((None, None, tk, tn), lambda i, j, kk: (layer, j // npc, kk, j % npc))
    else:
        k, n = N_CHIPS * r, c
        tk, tn = _tile(r, MM_TILE), _tile(c, MM_TILE)
        kpc = r // tk
        b_spec = pl.BlockSpec((None, None, tk, tn), lambda i, j, kk: (layer, kk // kpc, kk % kpc, j))
    assert a.shape[1] == k
    grid = (m // tm, n // tn, k // tk)
    out_spec = pl.BlockSpec((tm, tn), lambda i, j, kk: (i, j))
    ex_specs = extra_specs_fn(tm, tn) if extra_specs_fn is not None else ()
    outs, landed = _matmul(name, a, wg, dims=NN, grid=grid,
                           a_spec=pl.BlockSpec((tm, tk), lambda i, j, kk: (i, kk)), b_spec=b_spec,
                           out_shapes=[jax.ShapeDtypeStruct((m, n), d) for d in out_dtypes],
                           out_specs=[out_spec for _ in out_dtypes], acc_shape=(tm, tn),
                           epi=epi, extras=extras, extra_specs=ex_specs, gather=gather)
    return (outs, landed) if gather else outs


def _mm_nt_w(name, a, wg, layer, colshard, out_dtype, epi=None, extras=(), extra_specs_fn=None):
    m = a.shape[0]
    _, _, r, c = wg.shape
    tm = min(MM_TILE, m)
    if colshard:
        kdim, n = N_CHIPS * c, r
        tk, tn = _tile(c, MM_TILE), _tile(r, MM_TILE)
        kpc = c // tk
        b_spec = pl.BlockSpec((None, None, tn, tk), lambda i, j, kk: (layer, kk // kpc, j, kk % kpc))
    else:
        kdim, n = c, N_CHIPS * r
        tk, tn = c, _tile(r, MM_TILE)
        npc = r // tn
        b_spec = pl.BlockSpec((None, None, tn, tk), lambda i, j, kk: (layer, j // npc, j % npc, kk))
    assert a.shape[1] == kdim
    grid = (m // tm, n // tn, kdim // tk)
    ex_specs = extra_specs_fn(tm, tn) if extra_specs_fn is not None else ()
    return _matmul(name, a, wg, dims=NT, grid=grid,
                   a_spec=pl.BlockSpec((tm, tk), lambda i, j, kk: (i, kk)), b_spec=b_spec,
                   out_shapes=[jax.ShapeDtypeStruct((m, n), out_dtype)],
                   out_specs=[pl.BlockSpec((tm, tn), lambda i, j, kk: (i, j))], acc_shape=(tm, tn),
                   epi=epi, extras=extras, extra_specs=ex_specs)[0][0]


def _mm_tn(name, a, b, colshard):
    s, m = a.shape
    n = b.shape[1]
    tk = min(2 * MM_TILE, s)
    if colshard:
        c = n // N_CHIPS
        tm, tn = _tile(m, MM_TILE // 2), _tile(c, MM_TILE)
        npc = c // tn
        out_shape = (N_CHIPS, m, c)
        out_spec = pl.BlockSpec((None, tm, tn), lambda i, j, kk: (j // npc, i, j % npc))
    else:
        r = m // N_CHIPS
        tm, tn = _tile(r, MM_TILE // 2), _tile(n, MM_TILE)
        mpc = r // tm
        out_shape = (N_CHIPS, r, n)
        out_spec = pl.BlockSpec((None, tm, tn), lambda i, j, kk: (i // mpc, i % mpc, j))
    grid = (m // tm, n // tn, s // tk)
    return _matmul(name, a, b, dims=TN, grid=grid,
                   a_spec=pl.BlockSpec((tk, tm), lambda i, j, kk: (kk, i)),
                   b_spec=pl.BlockSpec((tk, tn), lambda i, j, kk: (kk, j)),
                   out_shapes=[jax.ShapeDtypeStruct(out_shape, F32)], out_specs=[out_spec],
                   acc_shape=(tm, tn))[0][0]


def _row_spec(ts, d):
    return pl.BlockSpec((ts, d), lambda i: (i, 0))


def _vec_spec(d):
    return pl.BlockSpec((1, d), lambda i: (0, 0))


def _norm_mod(name, x, g, sc, sh):
    s, d = x.shape
    ts = min(ROW_TILE, s)

    def body(x_ref, g_ref, sc_ref, sh_ref, h_ref):
        xv = x_ref[...]
        r = lax.rsqrt(jnp.mean(xv * xv, axis=-1, keepdims=True) + EPS)
        h_ref[...] = (xv * r * g_ref[...] * (1.0 + sc_ref[...]) + sh_ref[...]).astype(BF16)

    return pl.pallas_call(
        body, grid=(s // ts,), in_specs=[_row_spec(ts, d), _vec_spec(d), _vec_spec(d), _vec_spec(d)],
        out_specs=_row_spec(ts, d), out_shape=jax.ShapeDtypeStruct((s, d), BF16),
        compiler_params=_params("parallel"), name=name)(x, g, sc, sh)


def _norm_mod_bwd(name, dh, x, g, sc, dres):
    s, d = x.shape
    ts = min(ROW_TILE, s)

    def body(dh_ref, x_ref, g_ref, sc_ref, dres_ref, dx_ref, dsh_ref, dsc_ref, dg_ref):
        @pl.when(pl.program_id(0) == 0)
        def _():
            dsh_ref[...] = jnp.zeros_like(dsh_ref)
            dsc_ref[...] = jnp.zeros_like(dsc_ref)
            dg_ref[...] = jnp.zeros_like(dg_ref)

        xv = x_ref[...]
        r = lax.rsqrt(jnp.mean(xv * xv, axis=-1, keepdims=True) + EPS)
        xhat = xv * r
        dhv = dh_ref[...]
        gv = g_ref[...]
        opsc = 1.0 + sc_ref[...]
        dxhat = dhv * (gv * opsc)
        mdot = jnp.mean(dxhat * xhat, axis=-1, keepdims=True)
        dx_ref[...] = dres_ref[...] + r * (dxhat - xhat * mdot)
        dhx = dhv * xhat
        dsh_ref[...] += jnp.sum(dhv, axis=0, keepdims=True)
        dsc_ref[...] += jnp.sum(dhx * gv, axis=0, keepdims=True)
        dg_ref[...] += jnp.sum(dhx * opsc, axis=0, keepdims=True)

    vec = jax.ShapeDtypeStruct((1, d), F32)
    return pl.pallas_call(
        body, grid=(s // ts,),
        in_specs=[_row_spec(ts, d), _row_spec(ts, d), _vec_spec(d), _vec_spec(d), _row_spec(ts, d)],
        out_specs=[_row_spec(ts, d), _vec_spec(d), _vec_spec(d), _vec_spec(d)],
        out_shape=[jax.ShapeDtypeStruct((s, d), F32), vec, vec, vec],
        compiler_params=_params("arbitrary"), name=name)(dh, x, g, sc, dres)


def _gate_bwd(name, dx, m, gt):
    s, d = dx.shape
    ts = min(ROW_TILE, s)

    def body(dx_ref, m_ref, gt_ref, dm_ref, dgt_ref):
        @pl.when(pl.program_id(0) == 0)
        def _():
            dgt_ref[...] = jnp.zeros_like(dgt_ref)

        dxv = dx_ref[...]
        dm_ref[...] = (dxv * gt_ref[...]).astype(BF16)
        dgt_ref[...] += jnp.sum(dxv * m_ref[...].astype(F32), axis=0, keepdims=True)

    return pl.pallas_call(
        body, grid=(s // ts,), in_specs=[_row_spec(ts, d), _row_spec(ts, d), _vec_spec(d)],
        out_specs=[_row_spec(ts, d), _vec_spec(d)],
        out_shape=[jax.ShapeDtypeStruct((s, d), BF16), jax.ShapeDtypeStruct((1, d), F32)],
        compiler_params=_params("arbitrary"), name=name)(dx, m, gt)


def _loss_grad(name, y, target):
    s, d = y.shape
    ts = min(ROW_TILE, s)

    def body(y_ref, t_ref, dy_ref, sq_ref):
        @pl.when(pl.program_id(0) == 0)
        def _():
            sq_ref[...] = jnp.zeros_like(sq_ref)

        e = y_ref[...] - t_ref[...]
        dy_ref[...] = e * (1.0 / d)
        sq_ref[...] += jnp.sum(e * e, axis=0, keepdims=True)

    return pl.pallas_call(
        body, grid=(s // ts,), in_specs=[_row_spec(ts, d), _row_spec(ts, d)],
        out_specs=[_row_spec(ts, d), _vec_spec(d)],
        out_shape=[jax.ShapeDtypeStruct((s, d), F32), jax.ShapeDtypeStruct((1, d), F32)],
        compiler_params=_params("arbitrary"), name=name)(y, target)


def _adamw(name, w, g, m, v):
    r, c = w.shape
    tr = r
    for cand in (512, 256, 128, 64, 32, 16, 8):
        if r % cand == 0 and cand * c * 4 <= 2 * 1024 * 1024:
            tr = cand
            break

    def body(w_ref, g_ref, m_ref, v_ref, d_ref, mo_ref, vo_ref):
        gv = g_ref[...]
        m2 = ADAM_B1 * m_ref[...] + (1.0 - ADAM_B1) * gv
        v2 = ADAM_B2 * v_ref[...] + (1.0 - ADAM_B2) * (gv * gv)
        m_hat = m2 / (1.0 - ADAM_B1 ** ADAM_STEP)
        v_hat = v2 / (1.0 - ADAM_B2 ** ADAM_STEP)
        d_ref[...] = -ADAM_LR * (m_hat / (jnp.sqrt(v_hat) + ADAM_EPS) + ADAM_WD * w_ref[...])
        mo_ref[...] = m2
        vo_ref[...] = v2

    spec = pl.BlockSpec((tr, c), lambda i: (i, 0))
    shp = jax.ShapeDtypeStruct((r, c), F32)
    return pl.pallas_call(
        body, grid=(r // tr,), in_specs=[spec] * 4, out_specs=[spec] * 3, out_shape=[shp] * 3,
        compiler_params=_params("parallel"), name=name)(w, g, m, v)


def _sum_rows8(name, x):
    n, r, c = x.shape

    def body(x_ref, o_ref):
        acc = x_ref[0]
        for i in range(1, n):
            acc = acc + x_ref[i]
        o_ref[...] = acc

    return pl.pallas_call(
        body, out_shape=jax.ShapeDtypeStruct((r, c), F32),
        in_specs=[pl.BlockSpec(memory_space=pltpu.VMEM)], out_specs=pl.BlockSpec(memory_space=pltpu.VMEM),
        name=name)(x)


def _outer_sum(name, ct, dm):
    d, n_seq = ct.shape
    n = dm.shape[1]
    tr = min(256, d)

    def body(ct_ref, dm_ref, o_ref):
        acc = ct_ref[:, 0:1] * dm_ref[0:1, :]
        for s_i in range(1, n_seq):
            acc = acc + ct_ref[:, s_i:s_i + 1] * dm_ref[s_i:s_i + 1, :]
        o_ref[...] = acc

    return pl.pallas_call(
        body, grid=(d // tr,),
        in_specs=[pl.BlockSpec((tr, n_seq), lambda i: (i, 0)), pl.BlockSpec((n_seq, n), lambda i: (0, 0))],
        out_specs=pl.BlockSpec((tr, n), lambda i: (i, 0)), out_shape=jax.ShapeDtypeStruct((d, n), F32),
        compiler_params=_params("parallel"), name=name)(ct, dm)


def _ada_mod(name, c_act, w_ada_l, bias):
    n_seq, d = c_act.shape
    n = w_ada_l.shape[1]
    tn = _tile(n, 768)

    def epi(acc, b):
        return (acc + b,)

    return _matmul(name, c_act, w_ada_l, dims=NN, grid=(1, n // tn, 1),
                   a_spec=pl.BlockSpec((n_seq, d), lambda i, j, kk: (0, 0)),
                   b_spec=pl.BlockSpec((d, tn), lambda i, j, kk: (0, j)),
                   out_shapes=[jax.ShapeDtypeStruct((n_seq, n), F32)],
                   out_specs=[pl.BlockSpec((n_seq, tn), lambda i, j, kk: (0, j))], acc_shape=(n_seq, tn),
                   epi=epi, extras=(bias,), extra_specs=(pl.BlockSpec((1, tn), lambda i, j, kk: (0, j)),))[0][0]


def _silu_rows(name, c_row):
    d = c_row.shape[1]

    def body(c_ref, o_ref):
        cv = c_ref[...]
        o_ref[...] = jnp.broadcast_to(cv / (1.0 + jnp.exp(-cv)), (8, d))

    return pl.pallas_call(
        body, out_shape=jax.ShapeDtypeStruct((8, d), F32),
        in_specs=[pl.BlockSpec(memory_space=pltpu.VMEM)], out_specs=pl.BlockSpec(memory_space=pltpu.VMEM),
        name=name)(c_row)


def _sb_masks(t):
    row = lax.broadcasted_iota(jnp.int32, (t, t), 0)
    col = lax.broadcasted_iota(jnp.int32, (t, t), 1)
    strict = col < row
    tri = (row > col).astype(BF16)
    return strict, tri


def _col_minus_row(t):
    return lax.broadcasted_iota(jnp.int32, (t, t), 1) - lax.broadcasted_iota(jnp.int32, (t, t), 0)


def _sb_block(qi, i, sb):
    return qi - 2 * i - (1 - sb)


def _sb_keep(diff, qi, kb, t):
    return diff < jnp.where(kb >= 0, (qi - kb) * t, -t)


def _sweep_left(step, qi, carry, runs_of):
    def alive(c):
        runs = runs_of(c)
        top = runs[0]
        for r in runs[1:]:
            top = jnp.maximum(top, r)
        return jnp.max(top)

    n_plain = jnp.maximum(qi - 1, 0) // 2

    def cond(state):
        i, _, top = state
        return jnp.logical_and(i <= n_plain, top > DEAD_LOG)

    def body(state):
        i, c, _ = state
        c = step(i, c, False)
        return i + 1, c, alive(c)

    _, carry, top = lax.while_loop(cond, body, (1, carry, alive(carry)))
    lone = jnp.logical_and(jnp.logical_and(qi >= 2, qi % 2 == 0), top > DEAD_LOG)
    return lax.cond(lone, lambda c: step(qi // 2, c, True), lambda c: c, carry)


def _log_sigmoids(z):
    sp = jnp.log(1.0 + jnp.exp(-jnp.abs(z)))
    return jnp.minimum(z, 0.0) - sp, jnp.minimum(-z, 0.0) - sp, sp


def _sb_fwd(name, proj, n_pairs, gather=()):
    s = proj.shape[0]
    t = min(SB_TILE, s)
    nq = s // t
    n_g = len(gather)

    heads = [slice(h * HEAD_DIM, (h + 1) * HEAD_DIM) for h in range(LANES // HEAD_DIM)]

    assert nq % 2 == 0

    def body(*refs):
        q_ref, k_ref, v_ref = refs[:3]
        o_ref = refs[3 + n_g]
        g_refs, sems = refs[4 + n_g:4 + 2 * n_g], refs[4 + 2 * n_g:]
        _ride_gather((n_pairs, nq), g_refs, sems, lambda: _sb_fwd_body(q_ref, k_ref, v_ref, o_ref))

    def _sb_fwd_body(q_ref, k_ref, v_ref, o_ref):
        qi = pl.program_id(1)
        _, tri = _sb_masks(t)
        diff = _col_minus_row(t)
        qs = [(q_ref[:, sl] * SCALE).astype(BF16) for sl in heads]
        chains = [(sb, h) for sb in (1, 0) for h in range(len(heads))]

        def step(j, carry, masked):
            kbs = [_sb_block(qi, j, sb) for sb, _ in chains]
            ks = [pl.multiple_of(jnp.maximum(kb, 0) * t, t) for kb in kbs]
            k_ = [k_ref[pl.ds(ks[i], t), heads[h]].astype(BF16) for i, (_, h) in enumerate(chains)]
            v_ = [v_ref[pl.ds(ks[i], t), heads[h]].astype(BF16) for i, (_, h) in enumerate(chains)]
            zs = [lax.dot_general(qs[h], k_[i], NT, preferred_element_type=F32) for i, (_, h) in enumerate(chains)]
            lbs, lms, keeps = [], [], []
            for kb, z in zip(kbs, zs):
                lb, lm, _ = _log_sigmoids(z)
                keep = None
                if masked:
                    keep = _sb_keep(diff, qi, kb, t)
                    lm = jnp.where(keep, lm, 0.0)
                lbs.append(lb)
                lms.append(lm)
                keeps.append(keep)
            css = [_split_dot(lm, tri, 2) for lm in lms]
            sums = [jnp.sum(lm, axis=1, keepdims=True) for lm in lms]
            ws = []
            for i, (sb, h) in enumerate(chains):
                run = carry[h][1] if sb == 1 else carry[h][1] + sums[h]
                w = jnp.exp(lbs[i] + css[i] + run)
                if masked:
                    w = jnp.where(keeps[i], w, 0.0)
                ws.append(w.astype(BF16))
            pvs = [jnp.dot(w, v, preferred_element_type=F32) for w, v in zip(ws, v_)]
            nh = len(heads)
            return tuple((carry[h][0] + pvs[h] + pvs[nh + h], carry[h][1] + sums[h] + sums[nh + h]) for h in range(nh))

        init = tuple((jnp.zeros((t, HEAD_DIM), F32), jnp.zeros((t, 1), F32)) for _ in heads)
        carry = step(0, init, True)
        carry = _sweep_left(step, qi, carry, lambda c: [run for _, run in c])
        for (acc, _), sl in zip(carry, heads):
            o_ref[:, sl] = acc

    outs = pl.pallas_call(
        body, grid=(n_pairs, nq),
        in_specs=[pl.BlockSpec((t, LANES), lambda p, i: (i, p)),
                  pl.BlockSpec((s, LANES), lambda p, i: (0, n_pairs + p)),
                  pl.BlockSpec((s, LANES), lambda p, i: (0, 2 * n_pairs + p))] + [HBM_SPEC] * n_g,
        out_specs=[pl.BlockSpec((t, LANES), lambda p, i: (i, p))] + [HBM_SPEC] * n_g,
        out_shape=[jax.ShapeDtypeStruct((s, 2 * n_pairs * LANES), F32)]
        + [jax.ShapeDtypeStruct(g.shape, g.dtype) for g in gather],
        input_output_aliases={3 + i: 1 + i for i in range(n_g)},
        scratch_shapes=_gather_sems(n_g) if n_g else [],
        compiler_params=_params("arbitrary", "arbitrary"), name=name)(proj, proj, proj, *gather)
    return outs[0], list(outs[1:])


def _sb_bwd(name, proj, o_sb, dmixed, n_pairs, exchange=()):
    s = proj.shape[0]
    t = min(SB_TILE, s)
    nq = s // t
    n_x = len(exchange)
    heads = [slice(h * HEAD_DIM, (h + 1) * HEAD_DIM) for h in range(LANES // HEAD_DIM)]

    def body(*refs):
        x_in, x_out, sems = refs[5:5 + n_x], refs[8 + n_x:8 + 2 * n_x], refs[8 + 2 * n_x:]
        if n_x:
            @pl.when(_grid_edge((n_pairs, nq), False))
            def _():
                for cp in _exchange_copies(x_in, x_out, *sems):
                    cp.start()

        _sb_bwd_body(*refs[:5], *refs[5 + n_x:8 + n_x])
        if n_x:
            @pl.when(_grid_edge((n_pairs, nq), True))
            def _():
                for cp in _exchange_copies(x_in, x_out, *sems):
                    cp.wait()

    def _sb_bwd_body(q_ref, k_ref, v_ref, o_ref, do_ref, dq_ref, dk_ref, dv_ref):
        qi = pl.program_id(1)

        @pl.when(qi == 0)
        def _():
            dk_ref[...] = jnp.zeros_like(dk_ref)
            dv_ref[...] = jnp.zeros_like(dv_ref)

        _, tri = _sb_masks(t)
        diff = _col_minus_row(t)
        qs = [(q_ref[:, sl] * SCALE).astype(BF16) for sl in heads]
        dos = [do_ref[:, sl].astype(BF16) for sl in heads]
        dsums = [jnp.sum(do.astype(F32) * o_ref[:, sl], axis=1, keepdims=True) for do, sl in zip(dos, heads)]
        nh = len(heads)
        chains = [(sb, h) for sb in (1, 0) for h in range(nh)]

        def step(j, carry, masked):
            kbs = [_sb_block(qi, j, sb) for sb, _ in chains]
            ks = [pl.multiple_of(jnp.maximum(kb, 0) * t, t) for kb in kbs]
            k_ = [k_ref[pl.ds(ks[i], t), heads[h]].astype(BF16) for i, (_, h) in enumerate(chains)]
            v_ = [v_ref[pl.ds(ks[i], t), heads[h]].astype(BF16) for i, (_, h) in enumerate(chains)]
            zs = [lax.dot_general(qs[h], k_[i], NT, preferred_element_type=F32) for i, (_, h) in enumerate(chains)]
            dws = [lax.dot_general(dos[h], v_[i], NT, preferred_element_type=F32) for i, (_, h) in enumerate(chains)]
            lbs, lms, betas, ombs, keeps = [], [], [], [], []
            for kb, z in zip(kbs, zs):
                ez = jnp.exp(-jnp.abs(z))
                lb = jnp.minimum(z, 0.0) - jnp.log(1.0 + ez)
                lm = lb - z
                betas.append(jnp.exp(lb))
                ombs.append(jnp.exp(lm))
                keep = None
                if masked:
                    keep = _sb_keep(diff, qi, kb, t)
                    lm = jnp.where(keep, lm, 0.0)
                lbs.append(lb)
                lms.append(lm)
                keeps.append(keep)
            css = [_split_dot(lm, tri, 2) for lm in lms]
            sums_f = [jnp.sum(lm, axis=1, keepdims=True) for lm in lms]
            wbs, es = [], []
            for i, (sb, h) in enumerate(chains):
                run_f = carry[h][1] if sb == 1 else carry[h][1] + sums_f[h]
                w = jnp.exp(lbs[i] + css[i] + run_f)
                if masked:
                    w = jnp.where(keeps[i], w, 0.0)
                wb = w.astype(BF16)
                wbs.append(wb)
                es.append(wb.astype(F32) * dws[i])
            for i, (_, h) in enumerate(chains):
                dv_ref[pl.ds(ks[i], t), heads[h]] += lax.dot_general(wbs[i], dos[h], TN, preferred_element_type=F32)
            ecs = [_split_dot(e, tri, 2) for e in es]
            sums_e = [jnp.sum(e, axis=1, keepdims=True) for e in es]
            dzbs = []
            for i, (sb, h) in enumerate(chains):
                run_e = carry[h][2] if sb == 1 else carry[h][2] + sums_e[h]
                left = dsums[h] - (ecs[i] + es[i] + run_e)
                dz = es[i] * ombs[i] - left * betas[i]
                if masked:
                    dz = jnp.where(keeps[i], dz, 0.0)
                dzbs.append(dz.astype(BF16))
            dqs = []
            for i, (_, h) in enumerate(chains):
                dk_ref[pl.ds(ks[i], t), heads[h]] += lax.dot_general(dzbs[i], qs[h], TN, preferred_element_type=F32)
                dqs.append(jnp.dot(dzbs[i], k_[i], preferred_element_type=F32))
            return tuple((carry[h][0] + dqs[h] + dqs[nh + h], carry[h][1] + sums_f[h] + sums_f[nh + h],
                          carry[h][2] + sums_e[h] + sums_e[nh + h]) for h in range(nh))

        zero = jnp.zeros((t, 1), F32)
        init = tuple((jnp.zeros((t, HEAD_DIM), F32), zero, zero) for _ in heads)
        carry = step(0, init, True)
        carry = _sweep_left(step, qi, carry, lambda c: [run_f for _, run_f, _ in c])
        for (dq, _, _), sl in zip(carry, heads):
            dq_ref[:, sl] = dq * SCALE

    blk = pl.BlockSpec((t, LANES), lambda p, i: (i, p))
    full = pl.BlockSpec((s, LANES), lambda p, i: (0, p))
    shp = jax.ShapeDtypeStruct((s, n_pairs * LANES), F32)
    outs = pl.pallas_call(
        body, grid=(n_pairs, nq),
        in_specs=[blk,
                  pl.BlockSpec((s, LANES), lambda p, i: (0, n_pairs + p)),
                  pl.BlockSpec((s, LANES), lambda p, i: (0, 2 * n_pairs + p)),
                  blk, blk] + [HBM_SPEC] * n_x,
        out_specs=[blk, full, full] + [HBM_SPEC] * n_x,
        out_shape=[shp, shp, shp] + [jax.ShapeDtypeStruct((3,) + p.shape[1:], p.dtype) for p in exchange],
        scratch_shapes=_exchange_sems(n_x) if n_x else [],
        compiler_params=_params("arbitrary", "arbitrary"), name=name)(proj, proj, proj, o_sb, dmixed, *exchange)
    return outs[0], outs[1], outs[2], list(outs[3:])


def _group_mean_matrix():
    row = lax.broadcasted_iota(jnp.int32, (LANES, LANES), 0)
    col = lax.broadcasted_iota(jnp.int32, (LANES, LANES), 1)
    same_head = (row < HEAD_DIM) == (col < HEAD_DIM)
    return jnp.where(same_head, 1.0 / HEAD_DIM, 0.0).astype(BF16)


def _qk_norm(name, proj, gq2, gk2, n_pairs):
    s = proj.shape[0]
    ts = min(ROW_TILE, s)
    pb = PAD // ts
    c0 = 3 * n_pairs

    def body(q_ref, k_ref, v_ref, gq_ref, gk_ref, qn_ref, kn_ref, vp_ref):
        r_i = pl.program_id(1)

        @pl.when(r_i < pb)
        def _():
            qn_ref[...] = jnp.zeros_like(qn_ref)
            kn_ref[...] = jnp.zeros_like(kn_ref)
            vp_ref[...] = jnp.zeros_like(vp_ref)

        @pl.when(r_i >= pb)
        def _():
            gm = _group_mean_matrix()
            for src, g_ref, dst in ((q_ref, gq_ref, qn_ref), (k_ref, gk_ref, kn_ref)):
                xv = src[...]
                r = lax.rsqrt(_split_dot(xv * xv, gm, 2) + EPS)
                dst[...] = (xv * r * g_ref[...]).astype(BF16)
            vp_ref[...] = v_ref[...].astype(BF16)

    def in_spec(off):
        return pl.BlockSpec((ts, LANES), lambda p, i: (jnp.maximum(i - pb, 0), c0 + off * n_pairs + p))

    gspec = pl.BlockSpec((1, LANES), lambda p, i: (0, 0))
    ospec = pl.BlockSpec((ts, LANES), lambda p, i: (i, p))
    shp = jax.ShapeDtypeStruct((s + PAD, n_pairs * LANES), BF16)
    return pl.pallas_call(
        body, grid=(n_pairs, (s + PAD) // ts),
        in_specs=[in_spec(0), in_spec(1), in_spec(2), gspec, gspec],
        out_specs=[ospec, ospec, ospec], out_shape=[shp, shp, shp],
        compiler_params=_params("parallel", "arbitrary"), name=name)(proj, proj, proj, gq2, gk2)


def _qk_norm_bwd(name, dqn, dkn, proj, gq2, gk2, n_pairs):
    s = proj.shape[0]
    ts = min(ROW_TILE, s)
    pb = PAD // ts
    c0 = 3 * n_pairs
    n_r = s // ts

    def body(dqn_ref, dkn_ref, q_ref, k_ref, gq_ref, gk_ref, dq_ref, dk_ref, dgq_ref, dgk_ref):
        first = jnp.logical_and(pl.program_id(0) == 0, pl.program_id(1) == 0)
        last = jnp.logical_and(pl.program_id(0) == n_pairs - 1, pl.program_id(1) == n_r - 1)

        @pl.when(first)
        def _():
            dgq_ref[...] = jnp.zeros_like(dgq_ref)
            dgk_ref[...] = jnp.zeros_like(dgk_ref)

        gm = _group_mean_matrix()
        for dn_ref, x_ref, g_ref, dx_ref, dg_ref in ((dqn_ref, q_ref, gq_ref, dq_ref, dgq_ref),
                                                     (dkn_ref, k_ref, gk_ref, dk_ref, dgk_ref)):
            xv = x_ref[...]
            r = lax.rsqrt(_split_dot(xv * xv, gm, 2) + EPS)
            xhat = xv * r
            dn = dn_ref[...]
            tg = dn * g_ref[...]
            md = _split_dot(tg * xhat, gm, 2)
            dx_ref[...] = (r * (tg - xhat * md)).astype(BF16)
            dg_ref[...] += jnp.sum(dn * xhat, axis=0, keepdims=True)

        @pl.when(last)
        def _():
            for dg_ref in (dgq_ref, dgk_ref):
                gv = dg_ref[...]
                dg_ref[...] = gv + pltpu.roll(gv, HEAD_DIM, axis=1)

    def x_spec(off):
        return pl.BlockSpec((ts, LANES), lambda p, i: (i, c0 + off * n_pairs + p))

    dn_q = pl.BlockSpec((ts, LANES), lambda p, i: (i, p))
    dn_k = pl.BlockSpec((ts, LANES), lambda p, i: (i + pb, p))
    gspec = pl.BlockSpec((1, LANES), lambda p, i: (0, 0))
    ospec = pl.BlockSpec((ts, LANES), lambda p, i: (i, p))
    shp = jax.ShapeDtypeStruct((s, n_pairs * LANES), BF16)
    vec = jax.ShapeDtypeStruct((1, LANES), F32)
    return pl.pallas_call(
        body, grid=(n_pairs, n_r),
        in_specs=[dn_q, dn_k, x_spec(0), x_spec(1), gspec, gspec],
        out_specs=[ospec, ospec, gspec, gspec], out_shape=[shp, shp, vec, vec],
        compiler_params=_params("arbitrary", "arbitrary"), name=name)(dqn, dkn, proj, proj, gq2, gk2)


def _head_blocks(x2):
    lane = lax.broadcasted_iota(jnp.int32, x2.shape, 1)
    zero = jnp.zeros_like(x2)
    return jnp.concatenate([jnp.where(lane < HEAD_DIM, x2, zero), jnp.where(lane >= HEAD_DIM, x2, zero)], axis=0)


def _head_diag(r):
    lane = lax.broadcasted_iota(jnp.int32, (CHUNK, LANES), 1)
    return jnp.where(lane < HEAD_DIM, r[:CHUNK], r[CHUNK:])


def _ca_probs(kw, qb, bias_t, n):
    sc = lax.dot_general(kw, qb, NT, preferred_element_type=F32) * SCALE + bias_t
    jpos = lax.broadcasted_iota(jnp.int32, (BAND, LANES), 0)
    sc = jnp.where(n * CHUNK + jpos >= PAD, sc, NEG_INF)
    p = jnp.exp(sc - jnp.max(sc, axis=0, keepdims=True))
    return p * (1.0 / jnp.sum(p, axis=0, keepdims=True))


def _ca_fwd(name, qn, knp, vp, bias_tt, n_pairs, mixed, gather=()):
    sp_rows = knp.shape[0]
    s = sp_rows - PAD
    t = min(CA_TILE, s)
    cpb = t // CHUNK
    pb = PAD // t

    n_g = len(gather)
    grid = (n_pairs, s // t)

    def body(*refs):
        q_ref, k_ref, v_ref, b_ref = refs[:4]
        o_ref = refs[5 + n_g]
        g_refs, sems = refs[6 + n_g:6 + 2 * n_g], refs[6 + 2 * n_g:]

        def compute():
            qi = pl.program_id(1)
            wss = [pl.multiple_of((qi * cpb + ci) * CHUNK, CHUNK) for ci in range(cpb)]
            ps = [_ca_probs(k_ref[pl.ds(wss[ci], BAND), :], _head_blocks(q_ref[ci * CHUNK:(ci + 1) * CHUNK, :]),
                            b_ref[...], qi * cpb + ci) for ci in range(cpb)]
            for ci in range(cpb):
                r = lax.dot_general(ps[ci].astype(BF16), v_ref[pl.ds(wss[ci], BAND), :], TN,
                                    preferred_element_type=F32)
                o_ref[ci * CHUNK:(ci + 1) * CHUNK, :] = _head_diag(r)

        _ride_gather(grid, g_refs, sems, compute)

    full = pl.BlockSpec((sp_rows, LANES), lambda p, i: (0, p))
    outs = pl.pallas_call(
        body, grid=grid,
        in_specs=[pl.BlockSpec((t, LANES), lambda p, i: (i + pb, p)), full, full,
                  pl.BlockSpec((None, BAND, LANES), lambda p, i: (p, 0, 0)), pl.BlockSpec(memory_space=pl.ANY)]
        + [HBM_SPEC] * n_g,
        out_specs=[pl.BlockSpec((t, LANES), lambda p, i: (i, n_pairs + p))] + [HBM_SPEC] * n_g,
        out_shape=[jax.ShapeDtypeStruct(mixed.shape, mixed.dtype)]
        + [jax.ShapeDtypeStruct(g.shape, g.dtype) for g in gather],
        input_output_aliases={4: 0, **{5 + i: 1 + i for i in range(n_g)}},
        scratch_shapes=_gather_sems(n_g) if n_g else [],
        compiler_params=_params("arbitrary", "arbitrary"), name=name)(qn, knp, vp, bias_tt, mixed, *gather)
    return outs[0], list(outs[1:])


def _ca_bwd(name, qn, knp, vp, bias_tt, dmixed, n_pairs, col0):
    sp_rows = knp.shape[0]
    s = sp_rows - PAD
    t = min(CA_TILE, s)
    cpb = t // CHUNK
    pb = PAD // t

    def body(q_ref, k_ref, v_ref, b_ref, do_ref, dq_ref, dk_ref, dv_ref, db_ref):
        qi = pl.program_id(1)

        @pl.when(qi == 0)
        def _():
            dk_ref[...] = jnp.zeros_like(dk_ref)
            dv_ref[...] = jnp.zeros_like(dv_ref)
            db_ref[...] = jnp.zeros_like(db_ref)

        wss = [pl.multiple_of((qi * cpb + ci) * CHUNK, CHUNK) for ci in range(cpb)]
        kws = [k_ref[pl.ds(ws, BAND), :] for ws in wss]
        qbs = [_head_blocks(q_ref[ci * CHUNK:(ci + 1) * CHUNK, :]) for ci in range(cpb)]
        dbs = [_head_blocks(do_ref[ci * CHUNK:(ci + 1) * CHUNK, :].astype(BF16)) for ci in range(cpb)]
        ps = [_ca_probs(kws[ci], qbs[ci], b_ref[...], qi * cpb + ci) for ci in range(cpb)]
        dps = [lax.dot_general(v_ref[pl.ds(wss[ci], BAND), :], dbs[ci], NT, preferred_element_type=F32)
               for ci in range(cpb)]
        dss = [p * (dp - jnp.sum(p * dp, axis=0, keepdims=True)) for p, dp in zip(ps, dps)]
        for ci in range(cpb):
            dsb = dss[ci].astype(BF16)
            dv_ref[pl.ds(wss[ci], BAND), :] += jnp.dot(ps[ci].astype(BF16), dbs[ci], preferred_element_type=F32)
            dk_ref[pl.ds(wss[ci], BAND), :] += jnp.dot(dsb, qbs[ci], preferred_element_type=F32) * SCALE
            r = lax.dot_general(dsb, kws[ci], TN, preferred_element_type=F32)
            dq_ref[ci * CHUNK:(ci + 1) * CHUNK, :] = _head_diag(r) * SCALE
        total = dss[0]
        for ci in range(1, cpb):
            total = total + dss[ci]
        db_ref[...] += total

    full_in = pl.BlockSpec((sp_rows, LANES), lambda p, i: (0, p))
    btile = pl.BlockSpec((None, BAND, LANES), lambda p, i: (p, 0, 0))
    pad_shape = jax.ShapeDtypeStruct((sp_rows, n_pairs * LANES), F32)
    return pl.pallas_call(
        body, grid=(n_pairs, s // t),
        in_specs=[pl.BlockSpec((t, LANES), lambda p, i: (i + pb, p)), full_in, full_in, btile,
                  pl.BlockSpec((t, LANES), lambda p, i: (i, col0 + p))],
        out_specs=[pl.BlockSpec((t, LANES), lambda p, i: (i, p)), full_in, full_in, btile],
        out_shape=[jax.ShapeDtypeStruct((s, n_pairs * LANES), F32), pad_shape, pad_shape,
                   jax.ShapeDtypeStruct(bias_tt.shape, F32)],
        compiler_params=_params("parallel", "arbitrary"), name=name)(qn, knp, vp, bias_tt, dmixed)


def _pair_transposed(tile):
    n_h = tile.shape[0]
    return jnp.transpose(tile.reshape(n_h // 2, 2, CHUNK, BAND), (0, 3, 1, 2)).reshape(n_h // 2, BAND, LANES)


def _query_major(tile_t):
    n_p = tile_t.shape[0]
    return jnp.transpose(tile_t.reshape(n_p, BAND, 2, CHUNK), (3, 0, 2, 1)).reshape(CHUNK, 2 * n_p, BAND)


def _bias_tile(rb):
    n_h = rb.shape[0]
    far = jnp.broadcast_to(rb[:, 2 * REL_CLIP:], (n_h, PAD + CHUNK - REL_CLIP))
    near = rb[:, REL_CLIP - (CHUNK - 1):2 * REL_CLIP][:, ::-1]
    ext = jnp.concatenate([far, near], axis=1)
    return jnp.stack([ext[:, CHUNK - 1 - i:CHUNK - 1 - i + BAND] for i in range(CHUNK)], axis=1)


def _rel_bias_grad(name, db_t):
    _, n_h, _ = db_t.shape
    n_out = 3 * LANES
    assert CHUNK == 64 and REL_CLIP == 128 and LEFT_CHUNKS == 8

    def body(z_ref, o_ref):
        a_idx = lax.broadcasted_iota(jnp.int32, (LANES, n_out), 0)
        r_idx = lax.broadcasted_iota(jnp.int32, (LANES, n_out), 1)
        out = jnp.zeros((n_h, n_out), F32)
        for mm, base in ((0, 191), (1, 255), (2, 319)):
            b = LEFT_CHUNKS - mm
            acc = jnp.zeros((n_h, LANES), F32)
            for i in range(CHUNK):
                row = z_ref[i, :, b * CHUNK:(b + 1) * CHUNK]
                rowp = jnp.concatenate([row, jnp.zeros((n_h, LANES - CHUNK), F32)], axis=1)
                acc = acc + pltpu.roll(rowp, CHUNK - 1 - i, axis=1)
            place = (r_idx == jnp.minimum(base - a_idx, 2 * REL_CLIP)).astype(BF16)
            out = out + _split_dot(acc, place, 3)
        far = jnp.zeros((n_h, 1), F32)
        for i in range(CHUNK):
            far = far + jnp.sum(z_ref[i, :, 0:(LEFT_CHUNKS - 2) * CHUNK], axis=1, keepdims=True)
        lane = lax.broadcasted_iota(jnp.int32, (n_h, n_out), 1)
        o_ref[...] = out + jnp.where(lane == 2 * REL_CLIP, far, 0.0)

    return pl.pallas_call(
        body, out_shape=jax.ShapeDtypeStruct((n_h, n_out), F32),
        in_specs=[pl.BlockSpec(memory_space=pltpu.VMEM)], out_specs=pl.BlockSpec(memory_space=pltpu.VMEM),
        name=name)(db_t)


def _mesh_pos():
    return lax.axis_index("x"), lax.axis_index("y"), lax.axis_index("c")


def _other_chips(x, y):
    return [(1 - x, y), (x, 1 - y), (1 - x, 1 - y)]


def _all_gather_small(name, blk):
    m_per, n = blk.shape

    def body(x_ref, out_ref, send_sems, recv_sems, local_sem):
        x, y, c = _mesh_pos()
        me, sibling = (x, y, c), (x, y, 1 - c)
        chips = _other_chips(x, y)

        def rows(px, py, pc):
            return out_ref.at[pl.ds((4 * px + 2 * py + pc) * m_per, m_per), :]

        def copy(k, block, to, src=None):
            return pltpu.make_async_remote_copy(
                src_ref=rows(*block) if src is None else src, dst_ref=rows(*block),
                send_sem=send_sems.at[k], recv_sem=recv_sems.at[k], device_id=to, device_id_type=MESH)

        mine = pltpu.make_async_copy(x_ref, rows(*me), local_sem)
        mine.start()
        first = [copy(0, me, sibling, src=x_ref)]
        first += [copy(1 + j, me, (*chip, c), src=x_ref) for j, chip in enumerate(chips)]
        for cp in first:
            cp.start()
        passed = [copy(4 + j, (*chip, c), sibling) for j, chip in enumerate(chips)]
        for j, chip in enumerate(chips):
            copy(1 + j, (*chip, c), me).wait_recv()
            passed[j].start()
        copy(0, sibling, me).wait_recv()
        for j, chip in enumerate(chips):
            copy(4 + j, (*chip, 1 - c), me).wait_recv()
        for cp in first + passed:
            cp.wait_send()
        mine.wait()

    return pl.pallas_call(
        body, out_shape=jax.ShapeDtypeStruct((N_DEV * m_per, n), blk.dtype),
        in_specs=[pl.BlockSpec(memory_space=pltpu.VMEM)], out_specs=pl.BlockSpec(memory_space=pltpu.VMEM),
        scratch_shapes=[pltpu.SemaphoreType.DMA((7,)), pltpu.SemaphoreType.DMA((7,)), pltpu.SemaphoreType.DMA],
        compiler_params=pltpu.CompilerParams(vmem_limit_bytes=VMEM_LIMIT), name=name)(blk)


HBM_SPEC = pl.BlockSpec(memory_space=pltpu.HBM)


def _gather_copies(bufs, send_sems, recv_sems, phase):
    x, y, c = _mesh_pos()
    my_j = 2 * x + y
    starts, lands = [], []
    for t, buf in enumerate(bufs):
        rh = buf.shape[2] // 2
        for k, (px, py) in enumerate(_other_chips(x, y)):
            their_j = 2 * px + py
            if phase == 0:
                src, to = buf.at[0, my_j, pl.ds(c * rh, rh), :], (px, py, c)
                land = buf.at[0, their_j, pl.ds(c * rh, rh), :]
            else:
                src, to = buf.at[0, their_j, pl.ds(c * rh, rh), :], (x, y, 1 - c)
                land = buf.at[0, their_j, pl.ds((1 - c) * rh, rh), :]
            sems = dict(send_sem=send_sems.at[phase, t, k], recv_sem=recv_sems.at[phase, t, k],
                        device_id=to, device_id_type=MESH)
            starts.append(pltpu.make_async_remote_copy(src_ref=src, dst_ref=src, **sems))
            lands.append(pltpu.make_async_remote_copy(src_ref=src, dst_ref=land, **sems))
    return starts, lands


def _gather_sems(n):
    return [pltpu.SemaphoreType.DMA((2, n, 3)), pltpu.SemaphoreType.DMA((2, n, 3))]


def _all_gather_weights(name, bufs, phases):
    n = len(bufs)

    def body(*refs):
        outs = refs[n:2 * n]
        send_sems, recv_sems = refs[2 * n:]
        for phase in phases:
            starts, lands = _gather_copies(outs, send_sems, recv_sems, phase)
            for cp in starts:
                cp.start()
            for cp, ld in zip(starts, lands):
                cp.wait_send()
                ld.wait_recv()

    return pl.pallas_call(
        body, out_shape=[jax.ShapeDtypeStruct(b.shape, b.dtype) for b in bufs],
        in_specs=[HBM_SPEC] * n, out_specs=[HBM_SPEC] * n, input_output_aliases={t: t for t in range(n)},
        scratch_shapes=_gather_sems(n), name=name)(*bufs)


def _gather_buffer(shard, chip):
    r, c = shard.shape
    return lax.dynamic_update_slice(jnp.zeros((1, N_CHIPS, r, c), BF16), shard.astype(BF16)[None, None],
                                    (0, chip, 0, 0))


def _exchange_sibling_halves(name, grads):
    n = len(grads)

    def body(*refs):
        srcs, outs = refs[:n], refs[n:2 * n]
        send_sems, recv_sems = refs[2 * n:]
        x, y, c = _mesh_pos()
        cps = []
        for t in range(n):
            rh = grads[t].shape[1] // 2
            cps.append(pltpu.make_async_remote_copy(
                src_ref=srcs[t].at[:, pl.ds((1 - c) * rh, rh), :], dst_ref=outs[t],
                send_sem=send_sems.at[t], recv_sem=recv_sems.at[t], device_id=(x, y, 1 - c), device_id_type=MESH))
        for cp in cps:
            cp.start()
        for cp in cps:
            cp.wait()

    out_shape = [jax.ShapeDtypeStruct((g.shape[0], g.shape[1] // 2, g.shape[2]), g.dtype) for g in grads]
    return pl.pallas_call(
        body, out_shape=out_shape, in_specs=[HBM_SPEC] * n, out_specs=[HBM_SPEC] * n,
        scratch_shapes=[pltpu.SemaphoreType.DMA((n,)), pltpu.SemaphoreType.DMA((n,))],
        name=name)(*grads)


def _add_halves(name, g, recv, c_idx):
    nb, r, c = g.shape
    rh = r // 2
    tr = rh
    for cand in (512, 256, 128, 64):
        if rh % cand == 0:
            tr = cand
            break
    g4 = g.reshape(nb, 2, rh, c)

    def body(c_ref, g_ref, r_ref, o_ref):
        o_ref[...] = (g_ref[...] + r_ref[...]).astype(BF16)

    grid_spec = pltpu.PrefetchScalarGridSpec(
        num_scalar_prefetch=1, grid=(nb, rh // tr),
        in_specs=[pl.BlockSpec((None, None, tr, c), lambda j, i, cr: (j, cr[0], i, 0)),
                  pl.BlockSpec((None, tr, c), lambda j, i, cr: (j, i, 0))],
        out_specs=pl.BlockSpec((None, tr, c), lambda j, i, cr: (j, i, 0)))
    return pl.pallas_call(
        body, grid_spec=grid_spec, out_shape=jax.ShapeDtypeStruct((nb, rh, c), BF16),
        compiler_params=_params("parallel", "parallel"), name=name)(c_idx, g4, recv)


def _exchange_copies(srcs, outs, send_sems, recv_sems):
    x, y, c = _mesh_pos()
    return [pltpu.make_async_remote_copy(
        src_ref=srcs[t].at[2 * px + py], dst_ref=outs[t].at[k], send_sem=send_sems.at[t, k],
        recv_sem=recv_sems.at[t, k], device_id=(px, py, c), device_id_type=MESH)
        for t in range(len(srcs)) for k, (px, py) in enumerate(_other_chips(x, y))]


def _exchange_sems(n):
    return [pltpu.SemaphoreType.DMA((n, 3)), pltpu.SemaphoreType.DMA((n, 3))]


def _exchange_chips(name, parts):
    n = len(parts)

    def body(*refs):
        cps = _exchange_copies(refs[:n], refs[n:2 * n], *refs[2 * n:])
        for cp in cps:
            cp.start()
        for cp in cps:
            cp.wait()

    return pl.pallas_call(
        body, out_shape=[jax.ShapeDtypeStruct((3,) + p.shape[1:], p.dtype) for p in parts],
        in_specs=[HBM_SPEC] * n, out_specs=[HBM_SPEC] * n, scratch_shapes=_exchange_sems(n), name=name)(*parts)


def _sum_chips(name, own_parts, recvs, pos_idx):
    n_layers = len(own_parts)
    _, rh, c = recvs[0].shape
    tr = rh
    for cand in (512, 256, 128, 64):
        if rh % cand == 0:
            tr = cand
            break
    nb = rh // tr

    def body(pos_ref, *refs):
        own_refs, r_refs, o_ref = refs[:n_layers], refs[n_layers:2 * n_layers], refs[2 * n_layers]
        for l_i in range(n_layers):
            @pl.when(pl.program_id(0) == l_i)
            def _():
                acc = own_refs[l_i][...].astype(F32)
                for k in range(3):
                    acc = acc + r_refs[l_i][k].astype(F32)
                o_ref[...] = acc

    grid_spec = pltpu.PrefetchScalarGridSpec(
        num_scalar_prefetch=1, grid=(n_layers, nb),
        in_specs=[pl.BlockSpec((None, tr, c), lambda l, i, pr: (pr[0], i, 0)) for _ in range(n_layers)]
        + [pl.BlockSpec((3, tr, c), lambda l, i, pr: (0, i, 0)) for _ in range(n_layers)],
        out_specs=pl.BlockSpec((None, tr, c), lambda l, i, pr: (l, pr[1] * nb + i, 0)))
    return pl.pallas_call(
        body, grid_spec=grid_spec, out_shape=jax.ShapeDtypeStruct((n_layers, 2 * rh, c), F32),
        compiler_params=_params("parallel", "parallel"), name=name)(pos_idx, *own_parts, *recvs)


def _share_with_sibling(name, grads):
    n = len(grads)

    def body(*refs):
        outs = refs[n:2 * n]
        send_sems, recv_sems = refs[2 * n:]
        x, y, c = _mesh_pos()
        cps, landed = [], []
        for t in range(n):
            rh = grads[t].shape[1] // 2
            mine = outs[t].at[:, pl.ds(c * rh, rh), :]
            theirs = outs[t].at[:, pl.ds((1 - c) * rh, rh), :]
            cps.append(pltpu.make_async_remote_copy(
                src_ref=mine, dst_ref=mine, send_sem=send_sems.at[t], recv_sem=recv_sems.at[t],
                device_id=(x, y, 1 - c), device_id_type=MESH))
            landed.append(pltpu.make_async_remote_copy(
                src_ref=mine, dst_ref=theirs, send_sem=send_sems.at[t], recv_sem=recv_sems.at[t],
                device_id=(x, y, 1 - c), device_id_type=MESH))
        for cp in cps:
            cp.start()
        for cp, ld in zip(cps, landed):
            cp.wait_send()
            ld.wait_recv()

    out_shape = [jax.ShapeDtypeStruct(g.shape, g.dtype) for g in grads]
    return pl.pallas_call(
        body, out_shape=out_shape, in_specs=[HBM_SPEC] * n, out_specs=[HBM_SPEC] * n,
        input_output_aliases={t: t for t in range(n)},
        scratch_shapes=[pltpu.SemaphoreType.DMA((n,)), pltpu.SemaphoreType.DMA((n,))],
        name=name)(*grads)


def _row(v):
    return v.reshape(1, -1)


def kernel(x, c, g_norm1, w_in, g_q, g_k, rel_bias, w_o, g_norm2, w1, w2, w_ada, b_ada, loss_target, m_g_norm1, m_w_in, m_g_q, m_g_k, m_rel_bias, m_w_o, m_g_norm2, m_w1, m_w2, m_w_ada, m_b_ada, v_g_norm1, v_w_in, v_g_q, v_g_k, v_rel_bias, v_w_o, v_g_norm2, v_w1, v_w2, v_w_ada, v_b_ada):
    _, s, d = x.shape
    n_layers = g_norm1.shape[0]
    n_pairs = (d // 2) // LANES
    n_ca_heads = rel_bias.shape[1]
    n_rel = rel_bias.shape[2]
    assert n_rel == 2 * REL_CLIP + 1 and g_q.shape[1] == HEAD_DIM and n_ca_heads == 2 * n_pairs
    ada_c = w_ada.shape[2]

    ax, ay, ac = _mesh_pos()
    chip = 2 * ax + ay
    dev = 4 * ax + 2 * ay + ac
    c_idx = jnp.reshape(ac, (1,)).astype(jnp.int32)

    c_act_all = _all_gather_small("ag_c", _silu_rows("silu_c", c)).reshape(N_DEV, 8, d)[:, 0, :]
    mod_parts = []
    for l in range(n_layers):
        bias = lax.dynamic_slice_in_dim(b_ada[l], chip * ada_c, ada_c).reshape(1, ada_c)
        mod_parts.append(_ada_mod(f"ada_mod{l}", c_act_all, w_ada[l], bias))
    mod_all = _all_gather_small("ag_mod", jnp.concatenate(mod_parts, axis=1))
    mod_all = mod_all.reshape(N_CHIPS, 2, N_DEV, n_layers, ada_c)[:, 0]
    mod_all = jnp.transpose(mod_all, (1, 2, 0, 3)).reshape(N_DEV, n_layers, N_CHIPS * ada_c)
    mod = lax.dynamic_index_in_dim(mod_all, dev, axis=0, keepdims=False)

    gathered = [[_gather_buffer(w[l], chip) for w in (w_in, w_o, w1, w2)] for l in range(n_layers)]
    gathered[0][:1] = _all_gather_weights("ag_w_in_0", gathered[0][:1], (0, 1))

    xs = x[0]
    saved = []
    for l in range(n_layers):
        sh1, sc1, gt1, sh2, sc2, gt2 = [_row(mod[l, i * d:(i + 1) * d]) for i in range(6)]
        gq2 = _row(jnp.tile(g_q[l], 2))
        gk2 = _row(jnp.tile(g_k[l], 2))
        bias_t = _pair_transposed(_bias_tile(rel_bias[l]))
        h1 = _norm_mod(f"norm1_{l}", xs, _row(g_norm1[l]), sc1, sh1)
        proj = _mm_nn_w(f"proj_{l}", h1, gathered[l][0], 0, True, [F32])[0]
        o_sb, landed = _sb_fwd(f"sb_fwd_{l}", proj, n_pairs, gather=gathered[l][1:] if l == 0 else ())
        if landed:
            gathered[l][1:] = _all_gather_weights(f"ag_rest_{l}", landed, (1,))
        wg_in, wg_o, wg_1, wg_2 = gathered[l]
        nxt = gathered[l + 1] if l + 1 < n_layers else []
        qn, knp, vp = _qk_norm(f"qk_norm_{l}", proj, gq2, gk2, n_pairs)
        mixed, landed = _ca_fwd(f"ca_fwd_{l}", qn, knp, vp, bias_t, n_pairs, o_sb, gather=nxt[:2])

        def res_epi(acc, res, gt):
            return res + gt * acc, acc

        def res_specs(tm, tn):
            return (pl.BlockSpec((tm, tn), lambda i, j, kk: (i, j)), pl.BlockSpec((1, tn), lambda i, j, kk: (0, j)))

        x1, m1 = _mm_nn_w(f"attn_out_{l}", mixed, wg_o, 0, False, [F32, BF16], epi=res_epi,
                          extras=(xs, gt1), extra_specs_fn=res_specs)
        h2 = _norm_mod(f"norm2_{l}", x1, _row(g_norm2[l]), sc2, sh2)

        def act_epi(acc):
            r = jnp.maximum(acc, 0.0)
            return acc, r * r

        if l + 1 < n_layers:
            (u, a), landed_1 = _mm_nn_w(f"mlp_up_{l}", h2, wg_1, 0, True, [BF16, BF16], epi=act_epi, gather=nxt[2:3])
            (x2, m2), landed_2 = _mm_nn_w(f"mlp_down_{l}", a, wg_2, 0, False, [F32, BF16], epi=res_epi,
                                          extras=(x1, gt2), extra_specs_fn=res_specs, gather=nxt[3:])
            gathered[l + 1] = _all_gather_weights(f"ag_weights_{l + 1}", landed + landed_1 + landed_2, (1,))
        else:
            u, a = _mm_nn_w(f"mlp_up_{l}", h2, wg_1, 0, True, [BF16, BF16], epi=act_epi)
            x2, m2 = _mm_nn_w(f"mlp_down_{l}", a, wg_2, 0, False, [F32, BF16], epi=res_epi,
                              extras=(x1, gt2), extra_specs_fn=res_specs)
        saved.append(dict(x0=xs, h1=h1, proj=proj, o_sb=mixed, qn=qn, knp=knp, vp=vp, bias_t=bias_t, mixed=mixed,
                          m1=m1, x1=x1, h2=h2, u=u, a=a, m2=m2, gq2=gq2, gk2=gk2,
                          sc1=sc1, gt1=gt1, sc2=sc2, gt2=gt2))
        xs = x2

    dx, sq = _loss_grad("loss", xs, loss_target[0])
    loss_part = (0.5 * jnp.sum(sq) / d).reshape(1)

    g_in, g_o, g_1, g_2 = [None] * n_layers, [None] * n_layers, [None] * n_layers, [None] * n_layers
    dmod, dg1, dg2, dgq, dgk, drel = [], [], [], [], [], []
    partial, chip_sum, from_chip = {}, {}, {}

    def reduce_on_chip(tag, keys):
        grads = [partial[k] for k in keys]
        for k, g, r in zip(keys, grads, _exchange_sibling_halves(f"rs_sibling_{tag}", grads)):
            chip_sum[k] = _add_halves(f"rs_add_{k[0]}_{k[1]}", g, r, c_idx)

    for l in reversed(range(n_layers)):
        sv = saved[l]
        wg_in, wg_o, wg_1, wg_2 = gathered[l]
        dm2, dgt2 = _gate_bwd(f"gate2_bwd_{l}", dx, sv["m2"], sv["gt2"])

        def act_bwd_epi(acc, u_t):
            return (acc * (2.0 * jnp.maximum(u_t.astype(F32), 0.0)),)

        def tile_specs(tm, tn):
            return (pl.BlockSpec((tm, tn), lambda i, j, kk: (i, j)),)

        du = _mm_nt_w(f"mlp_down_bwd_{l}", dm2, wg_2, 0, False, BF16, epi=act_bwd_epi, extras=(sv["u"],),
                      extra_specs_fn=tile_specs)
        g_2[l] = _mm_tn(f"w2_grad_{l}", sv["a"], dm2, False)
        dh2 = _mm_nt_w(f"mlp_up_bwd_{l}", du, wg_1, 0, True, F32)
        g_1[l] = _mm_tn(f"w1_grad_{l}", sv["h2"], du, True)
        dx1, dsh2, dsc2, dgn2 = _norm_mod_bwd(f"norm2_bwd_{l}", dh2, sv["x1"], _row(g_norm2[l]), sv["sc2"], dx)
        dm1, dgt1 = _gate_bwd(f"gate1_bwd_{l}", dx1, sv["m1"], sv["gt1"])
        dmixed = _mm_nt_w(f"attn_out_bwd_{l}", dm1, wg_o, 0, False, F32)
        g_o[l] = _mm_tn(f"wo_grad_{l}", sv["mixed"], dm1, False)
        partial.update({(1, l): g_o[l], (2, l): g_1[l], (3, l): g_2[l]})
        if l == 0:
            early = sorted(partial)
            reduce_on_chip("early", early)
            dq_sb, dk_sb, dv_sb, landed = _sb_bwd(f"sb_bwd_{l}", sv["proj"], sv["o_sb"], dmixed, n_pairs,
                                                  exchange=[chip_sum[k] for k in early])
            from_chip.update(zip(early, landed))
        else:
            dq_sb, dk_sb, dv_sb, _ = _sb_bwd(f"sb_bwd_{l}", sv["proj"], sv["o_sb"], dmixed, n_pairs)
        dqn, dknp, dvp, dbias_t = _ca_bwd(f"ca_bwd_{l}", sv["qn"], sv["knp"], sv["vp"], sv["bias_t"], dmixed,
                                          n_pairs, n_pairs)
        dq_ca, dk_ca, dgq_l, dgk_l = _qk_norm_bwd(f"qk_norm_bwd_{l}", dqn, dknp, sv["proj"], sv["gq2"], sv["gk2"],
                                                  n_pairs)
        drel_l = _rel_bias_grad(f"rel_bias_grad_{l}", _query_major(dbias_t))[:, :n_rel]
        dproj = jnp.concatenate([dq_sb.astype(BF16), dk_sb.astype(BF16), dv_sb.astype(BF16), dq_ca, dk_ca,
                                 dvp[PAD:].astype(BF16)], axis=1)
        g_in[l] = _mm_tn(f"win_grad_{l}", sv["h1"], dproj, True)
        partial[(0, l)] = g_in[l]
        dh1 = _mm_nt_w(f"proj_bwd_{l}", dproj, wg_in, 0, True, F32)
        dx, dsh1, dsc1, dgn1 = _norm_mod_bwd(f"norm1_bwd_{l}", dh1, sv["x0"], _row(g_norm1[l]), sv["sc1"], dx1)
        dmod.insert(0, jnp.concatenate([dsh1, dsc1, dgt1, dsh2, dsc2, dgt2], axis=1)[0])
        dg1.insert(0, dgn1[0])
        dg2.insert(0, dgn2[0])
        dgq.insert(0, dgq_l[0, :HEAD_DIM])
        dgk.insert(0, dgk_l[0, :HEAD_DIM])
        drel.insert(0, drel_l.reshape(-1))
    grad_x = dx[None]

    small = jnp.concatenate([jnp.concatenate(dmod), jnp.concatenate(dg1), jnp.concatenate(dg2),
                             jnp.concatenate(dgq), jnp.concatenate(dgk), jnp.concatenate(drel), loss_part])
    n_small = small.shape[0]
    n_pack = -(-n_small // (8 * LANES)) * (8 * LANES)
    small = jnp.pad(small, (0, n_pack - n_small)).reshape(8, n_pack // 8)
    small_all = _all_gather_small("ag_small", small).reshape(N_DEV, 8, n_pack // 8)
    small_sum = _sum_rows8("sum_small", small_all).reshape(-1)
    dmod_all = small_all.reshape(N_DEV, n_pack)[:, :n_layers * 6 * d].reshape(N_DEV, n_layers, 6 * d)

    off = 0

    def take(n, shape):
        nonlocal off
        out = small_sum[off:off + n].reshape(shape)
        off += n
        return out

    grad_b_ada = take(n_layers * 6 * d, (n_layers, 6 * d))
    grad_g_norm1 = take(n_layers * d, (n_layers, d))
    grad_g_norm2 = take(n_layers * d, (n_layers, d))
    grad_g_q = take(n_layers * HEAD_DIM, (n_layers, HEAD_DIM))
    grad_g_k = take(n_layers * HEAD_DIM, (n_layers, HEAD_DIM))
    grad_rel_bias = take(n_layers * n_ca_heads * n_rel, (n_layers, n_ca_heads, n_rel))
    loss = take(1, ())

    c_act_t = jnp.transpose(c_act_all)
    grad_w_ada = jnp.stack([
        _outer_sum(f"wada_grad_{l}", c_act_t,
                   lax.dynamic_slice_in_dim(dmod_all[:, l, :], chip * ada_c, ada_c, axis=1))
        for l in range(n_layers)])

    late = [k for k in sorted(partial) if k not in from_chip]
    reduce_on_chip("late", late)
    from_chip.update(zip(late, _exchange_chips("rs_chips", [chip_sum[k] for k in late])))
    pos_idx = jnp.stack([chip, ac]).astype(jnp.int32)
    reduced = [_sum_chips(f"rs_sum_{w_i}", [chip_sum[(w_i, l)] for l in range(n_layers)],
                          [from_chip[(w_i, l)] for l in range(n_layers)], pos_idx) for w_i in range(4)]
    grad_w_in, grad_w_o, grad_w1, grad_w2 = _share_with_sibling("rs_share", reduced)

    def adam_big(name, w, g, m, v):
        shp = w.shape
        outs = _adamw(name, w.reshape(-1, shp[-1]), g.reshape(-1, shp[-1]), m.reshape(-1, shp[-1]),
                      v.reshape(-1, shp[-1]))
        return [o.reshape(shp) for o in outs]

    def pack(arrs):
        flat = jnp.concatenate([a.reshape(-1) for a in arrs])
        n = flat.shape[0]
        n_p = -(-n // (8 * LANES)) * (8 * LANES)
        return jnp.pad(flat, (0, n_p - n), constant_values=1.0).reshape(8, n_p // 8)

    small_w = [g_norm1, g_q, g_k, rel_bias, g_norm2, b_ada]
    small_g = [grad_g_norm1, grad_g_q, grad_g_k, grad_rel_bias, grad_g_norm2, grad_b_ada]
    small_m = [m_g_norm1, m_g_q, m_g_k, m_rel_bias, m_g_norm2, m_b_ada]
    small_v = [v_g_norm1, v_g_q, v_g_k, v_rel_bias, v_g_norm2, v_b_ada]
    packed = _adamw("adamw_small", pack(small_w), pack(small_g), pack(small_m), pack(small_v))

    def unpack(p):
        flat = p.reshape(-1)
        res, o = [], 0
        for a in small_w:
            res.append(flat[o:o + a.size].reshape(a.shape))
            o += a.size
        return res

    sd, sm, sv_ = unpack(packed[0]), unpack(packed[1]), unpack(packed[2])
    big = {
        "w_in": adam_big("adamw_w_in", w_in, grad_w_in, m_w_in, v_w_in),
        "w_o": adam_big("adamw_w_o", w_o, grad_w_o, m_w_o, v_w_o),
        "w1": adam_big("adamw_w1", w1, grad_w1, m_w1, v_w1),
        "w2": adam_big("adamw_w2", w2, grad_w2, m_w2, v_w2),
        "w_ada": adam_big("adamw_w_ada", w_ada, grad_w_ada, m_w_ada, v_w_ada),
    }

    def ordered(kind):
        sm_list = (sd, sm, sv_)[kind]
        return [sm_list[0], big["w_in"][kind], sm_list[1], sm_list[2], sm_list[3], big["w_o"][kind], sm_list[4],
                big["w1"][kind], big["w2"][kind], big["w_ada"][kind], sm_list[5]]

    grads = [grad_g_norm1, grad_w_in, grad_g_q, grad_g_k, grad_rel_bias, grad_w_o, grad_g_norm2, grad_w1, grad_w2,
             grad_w_ada, grad_b_ada]
    return (loss, grad_x, *grads, *ordered(0), *ordered(1), *ordered(2))
```

```python
import functools

import jax
import jax.numpy as jnp
from jax import lax
from jax.experimental import pallas as pl
from jax.experimental.pallas import tpu as pltpu

F32 = jnp.float32
BF16 = jnp.bfloat16
MESH = pl.DeviceIdType.MESH

EPS = 1e-6
NEG_INF = -1e30
HEAD_DIM = 64
CHUNK = 64
LEFT_CHUNKS = 8
PAD = LEFT_CHUNKS * CHUNK
BAND = PAD + CHUNK
REL_CLIP = 128
SCALE = HEAD_DIM ** -0.5
DEAD_LOG = -88.0
LANES = 128
N_CHIPS = 4
N_DEV = 8

ADAM_LR = 0.001
ADAM_B1 = 0.9
ADAM_B2 = 0.999
ADAM_EPS = 1e-08
ADAM_WD = 0.01
ADAM_STEP = 10

VMEM_LIMIT = 48 * 1024 * 1024
ROW_TILE = 512
MM_TILE = 1024
SB_TILE = 256
CA_TILE = 256

NT = (((1,), (1,)), ((), ()))
TN = (((0,), (0,)), ((), ()))
NN = (((1,), (0,)), ((), ()))


def _tile(n, pref):
    best = None
    t = LANES
    while t <= min(n, pref):
        if n % t == 0:
            best = t
        t += LANES
    return best if best is not None else n


def _params(*sem):
    return pltpu.CompilerParams(dimension_semantics=sem, vmem_limit_bytes=VMEM_LIMIT)


def _split_dot(x, t, passes):
    acc = None
    r = x
    for i in range(passes):
        hi = r.astype(BF16)
        d = jnp.dot(hi, t, preferred_element_type=F32)
        acc = d if acc is None else acc + d
        if i + 1 < passes:
            r = r - hi.astype(F32)
    return acc


def _grid_edge(grid, last):
    conds = [pl.program_id(i) == (g - 1 if last else 0) for i, g in enumerate(grid)]
    return functools.reduce(jnp.logical_and, conds)


def _ride_gather(grid, g_refs, sems, compute):
    if g_refs:
        @pl.when(_grid_edge(grid, False))
        def _():
            for cp in _gather_copies(g_refs, *sems, 0)[0]:
                cp.start()

    compute()
    if g_refs:
        @pl.when(_grid_edge(grid, True))
        def _():
            for cp, ld in zip(*_gather_copies(g_refs, *sems, 0)):
                cp.wait_send()
                ld.wait_recv()


def _matmul(name, a, b, *, dims, grid, a_spec, b_spec, out_shapes, out_specs, acc_shape,
            epi=None, extras=(), extra_specs=(), gather=()):
    nk = grid[2]
    n_ex = len(extras)
    n_out = len(out_shapes)
    n_g = len(gather)

    def body(*refs):
        a_ref, b_ref = refs[0], refs[1]
        ex = refs[2:2 + n_ex]
        outs = refs[2 + n_ex + n_g:2 + n_ex + n_g + n_out]
        g_refs = refs[2 + n_ex + n_g + n_out:2 + n_ex + 2 * n_g + n_out]
        acc = refs[2 + n_ex + 2 * n_g + n_out]
        sems = refs[3 + n_ex + 2 * n_g + n_out:]

        def compute():
            k = pl.program_id(2)

            @pl.when(k == 0)
            def _():
                acc[...] = jnp.zeros_like(acc)

            acc[...] += lax.dot_general(a_ref[...].astype(BF16), b_ref[...].astype(BF16), dims,
                                        preferred_element_type=F32)

            @pl.when(k == nk - 1)
            def _():
                res = epi(acc[...], *[e[...] for e in ex]) if epi is not None else (acc[...],)
                for o, r in zip(outs, res):
                    o[...] = r.astype(o.dtype)

        _ride_gather(grid, g_refs, sems, compute)

    outs = pl.pallas_call(
        body, grid=grid, in_specs=[a_spec, b_spec, *extra_specs] + [HBM_SPEC] * n_g,
        out_specs=list(out_specs) + [HBM_SPEC] * n_g,
        out_shape=list(out_shapes) + [jax.ShapeDtypeStruct(g.shape, g.dtype) for g in gather],
        input_output_aliases={2 + n_ex + i: n_out + i for i in range(n_g)},
        scratch_shapes=[pltpu.VMEM(acc_shape, F32)] + (_gather_sems(n_g) if n_g else []),
        compiler_params=_params(*(("arbitrary",) * 3 if n_g else ("parallel", "parallel", "arbitrary"))), name=name,
    )(a, b, *extras, *gather)
    return list(outs[:n_out]), list(outs[n_out:])


def _mm_nn_w(name, a, wg, layer, colshard, out_dtypes, epi=None, extras=(), extra_specs_fn=None, gather=()):
    m = a.shape[0]
    _, _, r, c = wg.shape
    tm = min(MM_TILE, m)
    if colshard:
        k, n = r, N_CHIPS * c
        tk, tn = k, _tile(c, MM_TILE)
        npc = c // tn
        b_spec = pl.BlockSpec((None, None, tk, tn), lambda i, j, kk: (layer, j // npc, kk, j % npc))
    else:
        k, n = N_CHIPS * r, c
        tk, tn = _tile(r, MM_TILE), _tile(c, MM_TILE)
        kpc = r // tk
        b_spec = pl.BlockSpec((None, None, tk, tn), lambda i, j, kk: (layer, kk // kpc, kk % kpc, j))
    assert a.shape[1] == k
    grid = (m // tm, n // tn, k // tk)
    out_spec = pl.BlockSpec((tm, tn), lambda i, j, kk: (i, j))
    ex_specs = extra_specs_fn(tm, tn) if extra_specs_fn is not None else ()
    outs, landed = _matmul(name, a, wg, dims=NN, grid=grid,
                           a_spec=pl.BlockSpec((tm, tk), lambda i, j, kk: (i, kk)), b_spec=b_spec,
                           out_shapes=[jax.ShapeDtypeStruct((m, n), d) for d in out_dtypes],
                           out_specs=[out_spec for _ in out_dtypes], acc_shape=(tm, tn),
                           epi=epi, extras=extras, extra_specs=ex_specs, gather=gather)
    return (outs, landed) if gather else outs


def _mm_nt_w(name, a, wg, layer, colshard, out_dtype, epi=None, extras=(), extra_specs_fn=None):
    m = a.shape[0]
    _, _, r, c = wg.shape
    tm = min(MM_TILE, m)
    if colshard:
        kdim, n = N_CHIPS * c, r
        tk, tn = _tile(c, MM_TILE), _tile(r, MM_TILE)
        kpc = c // tk
        b_spec = pl.BlockSpec((None, None, tn, tk), lambda i, j, kk: (layer, kk // kpc, j, kk % kpc))
    else:
        kdim, n = c, N_CHIPS * r
        tk, tn = c, _tile(r, MM_TILE)
        npc = r // tn
        b_spec = pl.BlockSpec((None, None, tn, tk), lambda i, j, kk: (layer, j // npc, j % npc, kk))
    assert a.shape[1] == kdim
    grid = (m // tm, n // tn, kdim // tk)
    ex_specs = extra_specs_fn(tm, tn) if extra_specs_fn is not None else ()
    return _matmul(name, a, wg, dims=NT, grid=grid,
                   a_spec=pl.BlockSpec((tm, tk), lambda i, j, kk: (i, kk)), b_spec=b_spec,
                   out_shapes=[jax.ShapeDtypeStruct((m, n), out_dtype)],
                   out_specs=[pl.BlockSpec((tm, tn), lambda i, j, kk: (i, j))], acc_shape=(tm, tn),
                   epi=epi, extras=extras, extra_specs=ex_specs)[0][0]


def _mm_tn(name, a, b, colshard):
    s, m = a.shape
    n = b.shape[1]
    tk = min(2 * MM_TILE, s)
    if colshard:
        c = n // N_CHIPS
        tm, tn = _tile(m, MM_TILE // 2), _tile(c, MM_TILE)
        npc = c // tn
        out_shape = (N_CHIPS, m, c)
        out_spec = pl.BlockSpec((None, tm, tn), lambda i, j, kk: (j // npc, i, j % npc))
    else:
        r = m // N_CHIPS
        tm, tn = _tile(r, MM_TILE // 2), _tile(n, MM_TILE)
        mpc = r // tm
        out_shape = (N_CHIPS, r, n)
        out_spec = pl.BlockSpec((None, tm, tn), lambda i, j, kk: (i // mpc, i % mpc, j))
    grid = (m // tm, n // tn, s // tk)
    return _matmul(name, a, b, dims=TN, grid=grid,
                   a_spec=pl.BlockSpec((tk, tm), lambda i, j, kk: (kk, i)),
                   b_spec=pl.BlockSpec((tk, tn), lambda i, j, kk: (kk, j)),
                   out_shapes=[jax.ShapeDtypeStruct(out_shape, F32)], out_specs=[out_spec],
                   acc_shape=(tm, tn))[0][0]


def _row_spec(ts, d):
    return pl.BlockSpec((ts, d), lambda i: (i, 0))


def _vec_spec(d):
    return pl.BlockSpec((1, d), lambda i: (0, 0))


def _norm_mod(name, x, g, sc, sh):
    s, d = x.shape
    ts = min(ROW_TILE, s)

    def body(x_ref, g_ref, sc_ref, sh_ref, h_ref):
        xv = x_ref[...]
        r = lax.rsqrt(jnp.mean(xv * xv, axis=-1, keepdims=True) + EPS)
        h_ref[...] = (xv * r * g_ref[...] * (1.0 + sc_ref[...]) + sh_ref[...]).astype(BF16)

    return pl.pallas_call(
        body, grid=(s // ts,), in_specs=[_row_spec(ts, d), _vec_spec(d), _vec_spec(d), _vec_spec(d)],
        out_specs=_row_spec(ts, d), out_shape=jax.ShapeDtypeStruct((s, d), BF16),
        compiler_params=_params("parallel"), name=name)(x, g, sc, sh)


def _norm_mod_bwd(name, dh, x, g, sc, dres):
    s, d = x.shape
    ts = min(ROW_TILE, s)

    def body(dh_ref, x_ref, g_ref, sc_ref, dres_ref, dx_ref, dsh_ref, dsc_ref, dg_ref):
        @pl.when(pl.program_id(0) == 0)
        def _():
            dsh_ref[...] = jnp.zeros_like(dsh_ref)
            dsc_ref[...] = jnp.zeros_like(dsc_ref)
            dg_ref[...] = jnp.zeros_like(dg_ref)

        xv = x_ref[...]
        r = lax.rsqrt(jnp.mean(xv * xv, axis=-1, keepdims=True) + EPS)
        xhat = xv * r
        dhv = dh_ref[...]
        gv = g_ref[...]
        opsc = 1.0 + sc_ref[...]
        dxhat = dhv * (gv * opsc)
        mdot = jnp.mean(dxhat * xhat, axis=-1, keepdims=True)
        dx_ref[...] = dres_ref[...] + r * (dxhat - xhat * mdot)
        dhx = dhv * xhat
        dsh_ref[...] += jnp.sum(dhv, axis=0, keepdims=True)
        dsc_ref[...] += jnp.sum(dhx * gv, axis=0, keepdims=True)
        dg_ref[...] += jnp.sum(dhx * opsc, axis=0, keepdims=True)

    vec = jax.ShapeDtypeStruct((1, d), F32)
    return pl.pallas_call(
        body, grid=(s // ts,),
        in_specs=[_row_spec(ts, d), _row_spec(ts, d), _vec_spec(d), _vec_spec(d), _row_spec(ts, d)],
        out_specs=[_row_spec(ts, d), _vec_spec(d), _vec_spec(d), _vec_spec(d)],
        out_shape=[jax.ShapeDtypeStruct((s, d), F32), vec, vec, vec],
        compiler_params=_params("arbitrary"), name=name)(dh, x, g, sc, dres)


def _gate_bwd(name, dx, m, gt):
    s, d = dx.shape
    ts = min(ROW_TILE, s)

    def body(dx_ref, m_ref, gt_ref, dm_ref, dgt_ref):
        @pl.when(pl.program_id(0) == 0)
        def _():
            dgt_ref[...] = jnp.zeros_like(dgt_ref)

        dxv = dx_ref[...]
        dm_ref[...] = (dxv * gt_ref[...]).astype(BF16)
        dgt_ref[...] += jnp.sum(dxv * m_ref[...].astype(F32), axis=0, keepdims=True)

    return pl.pallas_call(
        body, grid=(s // ts,), in_specs=[_row_spec(ts, d), _row_spec(ts, d), _vec_spec(d)],
        out_specs=[_row_spec(ts, d), _vec_spec(d)],
        out_shape=[jax.ShapeDtypeStruct((s, d), BF16), jax.ShapeDtypeStruct((1, d), F32)],
        compiler_params=_params("arbitrary"), name=name)(dx, m, gt)


def _loss_grad(name, y, target):
    s, d = y.shape
    ts = min(ROW_TILE, s)

    def body(y_ref, t_ref, dy_ref, sq_ref):
        @pl.when(pl.program_id(0) == 0)
        def _():
            sq_ref[...] = jnp.zeros_like(sq_ref)

        e = y_ref[...] - t_ref[...]
        dy_ref[...] = e * (1.0 / d)
        sq_ref[...] += jnp.sum(e * e, axis=0, keepdims=True)

    return pl.pallas_call(
        body, grid=(s // ts,), in_specs=[_row_spec(ts, d), _row_spec(ts, d)],
        out_specs=[_row_spec(ts, d), _vec_spec(d)],
        out_shape=[jax.ShapeDtypeStruct((s, d), F32), jax.ShapeDtypeStruct((1, d), F32)],
        compiler_params=_params("arbitrary"), name=name)(y, target)


def _adamw(name, w, g, m, v):
    r, c = w.shape
    tr = r
    for cand in (512, 256, 128, 64, 32, 16, 8):
        if r % cand == 0 and cand * c * 4 <= 2 * 1024 * 1024:
            tr = cand
            break

    def body(w_ref, g_ref, m_ref, v_ref, d_ref, mo_ref, vo_ref):
        gv = g_ref[...]
        m2 = ADAM_B1 * m_ref[...] + (1.0 - ADAM_B1) * gv
        v2 = ADAM_B2 * v_ref[...] + (1.0 - ADAM_B2) * (gv * gv)
        m_hat = m2 / (1.0 - ADAM_B1 ** ADAM_STEP)
        v_hat = v2 / (1.0 - ADAM_B2 ** ADAM_STEP)
        d_ref[...] = -ADAM_LR * (m_hat / (jnp.sqrt(v_hat) + ADAM_EPS) + ADAM_WD * w_ref[...])
        mo_ref[...] = m2
        vo_ref[...] = v2

    spec = pl.BlockSpec((tr, c), lambda i: (i, 0))
    shp = jax.ShapeDtypeStruct((r, c), F32)
    return pl.pallas_call(
        body, grid=(r // tr,), in_specs=[spec] * 4, out_specs=[spec] * 3, out_shape=[shp] * 3,
        compiler_params=_params("parallel"), name=name)(w, g, m, v)


def _sum_rows8(name, x):
    n, r, c = x.shape

    def body(x_ref, o_ref):
        acc = x_ref[0]
        for i in range(1, n):
            acc = acc + x_ref[i]
        o_ref[...] = acc

    return pl.pallas_call(
        body, out_shape=jax.ShapeDtypeStruct((r, c), F32),
        in_specs=[pl.BlockSpec(memory_space=pltpu.VMEM)], out_specs=pl.BlockSpec(memory_space=pltpu.VMEM),
        name=name)(x)


def _outer_sum(name, ct, dm):
    d, n_seq = ct.shape
    n = dm.shape[1]
    tr = min(256, d)

    def body(ct_ref, dm_ref, o_ref):
        acc = ct_ref[:, 0:1] * dm_ref[0:1, :]
        for s_i in range(1, n_seq):
            acc = acc + ct_ref[:, s_i:s_i + 1] * dm_ref[s_i:s_i + 1, :]
        o_ref[...] = acc

    return pl.pallas_call(
        body, grid=(d // tr,),
        in_specs=[pl.BlockSpec((tr, n_seq), lambda i: (i, 0)), pl.BlockSpec((n_seq, n), lambda i: (0, 0))],
        out_specs=pl.BlockSpec((tr, n), lambda i: (i, 0)), out_shape=jax.ShapeDtypeStruct((d, n), F32),
        compiler_params=_params("parallel"), name=name)(ct, dm)


def _ada_mod(name, c_act, w_ada_l, bias):
    n_seq, d = c_act.shape
    n = w_ada_l.shape[1]
    tn = _tile(n, 768)

    def epi(acc, b):
        return (acc + b,)

    return _matmul(name, c_act, w_ada_l, dims=NN, grid=(1, n // tn, 1),
                   a_spec=pl.BlockSpec((n_seq, d), lambda i, j, kk: (0, 0)),
                   b_spec=pl.BlockSpec((d, tn), lambda i, j, kk: (0, j)),
                   out_shapes=[jax.ShapeDtypeStruct((n_seq, n), F32)],
                   out_specs=[pl.BlockSpec((n_seq, tn), lambda i, j, kk: (0, j))], acc_shape=(n_seq, tn),
                   epi=epi, extras=(bias,), extra_specs=(pl.BlockSpec((1, tn), lambda i, j, kk: (0, j)),))[0][0]


def _silu_rows(name, c_row):
    d = c_row.shape[1]

    def body(c_ref, o_ref):
        cv = c_ref[...]
        o_ref[...] = jnp.broadcast_to(cv / (1.0 + jnp.exp(-cv)), (8, d))

    return pl.pallas_call(
        body, out_shape=jax.ShapeDtypeStruct((8, d), F32),
        in_specs=[pl.BlockSpec(memory_space=pltpu.VMEM)], out_specs=pl.BlockSpec(memory_space=pltpu.VMEM),
        name=name)(c_row)


def _sb_masks(t):
    row = lax.broadcasted_iota(jnp.int32, (t, t), 0)
    col = lax.broadcasted_iota(jnp.int32, (t, t), 1)
    strict = col < row
    tri = (row > col).astype(BF16)
    return strict, tri


def _col_minus_row(t):
    return lax.broadcasted_iota(jnp.int32, (t, t), 1) - lax.broadcasted_iota(jnp.int32, (t, t), 0)


def _sb_block(qi, i, sb):
    return qi - 2 * i - (1 - sb)


def _sb_keep(diff, qi, kb, t):
    return diff < jnp.where(kb >= 0, (qi - kb) * t, -t)


def _sweep_left(step, qi, carry, runs_of):
    def alive(c):
        runs = runs_of(c)
        top = runs[0]
        for r in runs[1:]:
            top = jnp.maximum(top, r)
        return jnp.max(top)

    n_plain = jnp.maximum(qi - 1, 0) // 2

    def cond(state):
        i, _, top = state
        return jnp.logical_and(i <= n_plain, top > DEAD_LOG)

    def body(state):
        i, c, _ = state
        c = step(i, c, False)
        return i + 1, c, alive(c)

    _, carry, top = lax.while_loop(cond, body, (1, carry, alive(carry)))
    lone = jnp.logical_and(jnp.logical_and(qi >= 2, qi % 2 == 0), top > DEAD_LOG)
    return lax.cond(lone, lambda c: step(qi // 2, c, True), lambda c: c, carry)


def _log_sigmoids(z):
    sp = jnp.log(1.0 + jnp.exp(-jnp.abs(z)))
    return jnp.minimum(z, 0.0) - sp, jnp.minimum(-z, 0.0) - sp, sp


def _sb_fwd(name, proj, n_pairs, gather=()):
    s = proj.shape[0]
    t = min(SB_TILE, s)
    nq = s // t
    n_g = len(gather)

    heads = [slice(h * HEAD_DIM, (h + 1) * HEAD_DIM) for h in range(LANES // HEAD_DIM)]

    assert nq % 2 == 0

    def body(*refs):
        q_ref, k_ref, v_ref = refs[:3]
        o_ref = refs[3 + n_g]
        g_refs, sems = refs[4 + n_g:4 + 2 * n_g], refs[4 + 2 * n_g:]
        _ride_gather((n_pairs, nq), g_refs, sems, lambda: _sb_fwd_body(q_ref, k_ref, v_ref, o_ref))

    def _sb_fwd_body(q_ref, k_ref, v_ref, o_ref):
        qi = pl.program_id(1)
        _, tri = _sb_masks(t)
        diff = _col_minus_row(t)
        qs = [(q_ref[:, sl] * SCALE).astype(BF16) for sl in heads]
        chains = [(sb, h) for sb in (1, 0) for h in range(len(heads))]

        def step(j, carry, masked):
            kbs = [_sb_block(qi, j, sb) for sb, _ in chains]
            ks = [pl.multiple_of(jnp.maximum(kb, 0) * t, t) for kb in kbs]
            k_ = [k_ref[pl.ds(ks[i], t), heads[h]].astype(BF16) for i, (_, h) in enumerate(chains)]
            v_ = [v_ref[pl.ds(ks[i], t), heads[h]].astype(BF16) for i, (_, h) in enumerate(chains)]
            zs = [lax.dot_general(qs[h], k_[i], NT, preferred_element_type=F32) for i, (_, h) in enumerate(chains)]
            lbs, lms, keeps = [], [], []
            for kb, z in zip(kbs, zs):
                lb, lm, _ = _log_sigmoids(z)
                keep = None
                if masked:
                    keep = _sb_keep(diff, qi, kb, t)
                    lm = jnp.where(keep, lm, 0.0)
                lbs.append(lb)
                lms.append(lm)
                keeps.append(keep)
            css = [_split_dot(lm, tri, 2) for lm in lms]
            sums = [jnp.sum(lm, axis=1, keepdims=True) for lm in lms]
            ws = []
            for i, (sb, h) in enumerate(chains):
                run = carry[h][1] if sb == 1 else carry[h][1] + sums[h]
                w = jnp.exp(lbs[i] + css[i] + run)
                if masked:
                    w = jnp.where(keeps[i], w, 0.0)
                ws.append(w.astype(BF16))
            pvs = [jnp.dot(w, v, preferred_element_type=F32) for w, v in zip(ws, v_)]
            nh = len(heads)
            return tuple((carry[h][0] + pvs[h] + pvs[nh + h], carry[h][1] + sums[h] + sums[nh + h]) for h in range(nh))

        init = tuple((jnp.zeros((t, HEAD_DIM), F32), jnp.zeros((t, 1), F32)) for _ in heads)
        carry = step(0, init, True)
        carry = _sweep_left(step, qi, carry, lambda c: [run for _, run in c])
        for (acc, _), sl in zip(carry, heads):
            o_ref[:, sl] = acc

    outs = pl.pallas_call(
        body, grid=(n_pairs, nq),
        in_specs=[pl.BlockSpec((t, LANES), lambda p, i: (i, p)),
                  pl.BlockSpec((s, LANES), lambda p, i: (0, n_pairs + p)),
                  pl.BlockSpec((s, LANES), lambda p, i: (0, 2 * n_pairs + p))] + [HBM_SPEC] * n_g,
        out_specs=[pl.BlockSpec((t, LANES), lambda p, i: (i, p))] + [HBM_SPEC] * n_g,
        out_shape=[jax.ShapeDtypeStruct((s, 2 * n_pairs * LANES), F32)]
        + [jax.ShapeDtypeStruct(g.shape, g.dtype) for g in gather],
        input_output_aliases={3 + i: 1 + i for i in range(n_g)},
        scratch_shapes=_gather_sems(n_g) if n_g else [],
        compiler_params=_params("arbitrary", "arbitrary"), name=name)(proj, proj, proj, *gather)
    return outs[0], list(outs[1:])


def _sb_bwd(name, proj, o_sb, dmixed, n_pairs, exchange=()):
    s = proj.shape[0]
    t = min(SB_TILE, s)
    nq = s // t
    n_x = len(exchange)
    heads = [slice(h * HEAD_DIM, (h + 1) * HEAD_DIM) for h in range(LANES // HEAD_DIM)]

    def body(*refs):
        x_in, x_out, sems = refs[5:5 + n_x], refs[8 + n_x:8 + 2 * n_x], refs[8 + 2 * n_x:]
        if n_x:
            @pl.when(_grid_edge((n_pairs, nq), False))
            def _():
                for cp in _exchange_copies(x_in, x_out, *sems):
                    cp.start()

        _sb_bwd_body(*refs[:5], *refs[5 + n_x:8 + n_x])
        if n_x:
            @pl.when(_grid_edge((n_pairs, nq), True))
            def _():
                for cp in _exchange_copies(x_in, x_out, *sems):
                    cp.wait()

    def _sb_bwd_body(q_ref, k_ref, v_ref, o_ref, do_ref, dq_ref, dk_ref, dv_ref):
        qi = pl.program_id(1)

        @pl.when(qi == 0)
        def _():
            dk_ref[...] = jnp.zeros_like(dk_ref)
            dv_ref[...] = jnp.zeros_like(dv_ref)

        _, tri = _sb_masks(t)
        diff = _col_minus_row(t)
        qs = [(q_ref[:, sl] * SCALE).astype(BF16) for sl in heads]
        dos = [do_ref[:, sl].astype(BF16) for sl in heads]
        dsums = [jnp.sum(do.astype(F32) * o_ref[:, sl], axis=1, keepdims=True) for do, sl in zip(dos, heads)]
        nh = len(heads)
        chains = [(sb, h) for sb in (1, 0) for h in range(nh)]

        def step(j, carry, masked):
            kbs = [_sb_block(qi, j, sb) for sb, _ in chains]
            ks = [pl.multiple_of(jnp.maximum(kb, 0) * t, t) for kb in kbs]
            k_ = [k_ref[pl.ds(ks[i], t), heads[h]].astype(BF16) for i, (_, h) in enumerate(chains)]
            v_ = [v_ref[pl.ds(ks[i], t), heads[h]].astype(BF16) for i, (_, h) in enumerate(chains)]
            zs = [lax.dot_general(qs[h], k_[i], NT, preferred_element_type=F32) for i, (_, h) in enumerate(chains)]
            dws = [lax.dot_general(dos[h], v_[i], NT, preferred_element_type=F32) for i, (_, h) in enumerate(chains)]
            lbs, lms, betas, ombs, keeps = [], [], [], [], []
            for kb, z in zip(kbs, zs):
                ez = jnp.exp(-jnp.abs(z))
                lb = jnp.minimum(z, 0.0) - jnp.log(1.0 + ez)
                lm = lb - z
                betas.append(jnp.exp(lb))
                ombs.append(jnp.exp(lm))
                keep = None
                if masked:
                    keep = _sb_keep(diff, qi, kb, t)
                    lm = jnp.where(keep, lm, 0.0)
                lbs.append(lb)
                lms.append(lm)
                keeps.append(keep)
            css = [_split_dot(lm, tri, 2) for lm in lms]
            sums_f = [jnp.sum(lm, axis=1, keepdims=True) for lm in lms]
            wbs, es = [], []
            for i, (sb, h) in enumerate(chains):
                run_f = carry[h][1] if sb == 1 else carry[h][1] + sums_f[h]
                w = jnp.exp(lbs[i] + css[i] + run_f)
                if masked:
                    w = jnp.where(keeps[i], w, 0.0)
                wb = w.astype(BF16)
                wbs.append(wb)
                es.append(wb.astype(F32) * dws[i])
            for i, (_, h) in enumerate(chains):
                dv_ref[pl.ds(ks[i], t), heads[h]] += lax.dot_general(wbs[i], dos[h], TN, preferred_element_type=F32)
            ecs = [_split_dot(e, tri, 2) for e in es]
            sums_e = [jnp.sum(e, axis=1, keepdims=True) for e in es]
            dzbs = []
            for i, (sb, h) in enumerate(chains):
                run_e = carry[h][2] if sb == 1 else carry[h][2] + sums_e[h]
                left = dsums[h] - (ecs[i] + es[i] + run_e)
                dz = es[i] * ombs[i] - left * betas[i]
                if masked:
                    dz = jnp.where(keeps[i], dz, 0.0)
                dzbs.append(dz.astype(BF16))
            dqs = []
            for i, (_, h) in enumerate(chains):
                dk_ref[pl.ds(ks[i], t), heads[h]] += lax.dot_general(dzbs[i], qs[h], TN, preferred_element_type=F32)
                dqs.append(jnp.dot(dzbs[i], k_[i], preferred_element_type=F32))
            return tuple((carry[h][0] + dqs[h] + dqs[nh + h], carry[h][1] + sums_f[h] + sums_f[nh + h],
                          carry[h][2] + sums_e[h] + sums_e[nh + h]) for h in range(nh))

        zero = jnp.zeros((t, 1), F32)
        init = tuple((jnp.zeros((t, HEAD_DIM), F32), zero, zero) for _ in heads)
        carry = step(0, init, True)
        carry = _sweep_left(step, qi, carry, lambda c: [run_f for _, run_f, _ in c])
        for (dq, _, _), sl in zip(carry, heads):
            dq_ref[:, sl] = dq * SCALE

    blk = pl.BlockSpec((t, LANES), lambda p, i: (i, p))
    full = pl.BlockSpec((s, LANES), lambda p, i: (0, p))
    shp = jax.ShapeDtypeStruct((s, n_pairs * LANES), F32)
    outs = pl.pallas_call(
        body, grid=(n_pairs, nq),
        in_specs=[blk,
                  pl.BlockSpec((s, LANES), lambda p, i: (0, n_pairs + p)),
                  pl.BlockSpec((s, LANES), lambda p, i: (0, 2 * n_pairs + p)),
                  blk, blk] + [HBM_SPEC] * n_x,
        out_specs=[blk, full, full] + [HBM_SPEC] * n_x,
        out_shape=[shp, shp, shp] + [jax.ShapeDtypeStruct((3,) + p.shape[1:], p.dtype) for p in exchange],
        scratch_shapes=_exchange_sems(n_x) if n_x else [],
        compiler_params=_params("arbitrary", "arbitrary"), name=name)(proj, proj, proj, o_sb, dmixed, *exchange)
    return outs[0], outs[1], outs[2], list(outs[3:])


def _group_mean_matrix():
    row = lax.broadcasted_iota(jnp.int32, (LANES, LANES), 0)
    col = lax.broadcasted_iota(jnp.int32, (LANES, LANES), 1)
    same_head = (row < HEAD_DIM) == (col < HEAD_DIM)
    return jnp.where(same_head, 1.0 / HEAD_DIM, 0.0).astype(BF16)


def _qk_norm(name, proj, gq2, gk2, n_pairs):
    s = proj.shape[0]
    ts = min(ROW_TILE, s)
    pb = PAD // ts
    c0 = 3 * n_pairs

    def body(q_ref, k_ref, v_ref, gq_ref, gk_ref, qn_ref, kn_ref, vp_ref):
        r_i = pl.program_id(1)

        @pl.when(r_i < pb)
        def _():
            qn_ref[...] = jnp.zeros_like(qn_ref)
            kn_ref[...] = jnp.zeros_like(kn_ref)
            vp_ref[...] = jnp.zeros_like(vp_ref)

        @pl.when(r_i >= pb)
        def _():
            gm = _group_mean_matrix()
            for src, g_ref, dst in ((q_ref, gq_ref, qn_ref), (k_ref, gk_ref, kn_ref)):
                xv = src[...]
                r = lax.rsqrt(_split_dot(xv * xv, gm, 2) + EPS)
                dst[...] = (xv * r * g_ref[...]).astype(BF16)
            vp_ref[...] = v_ref[...].astype(BF16)

    def in_spec(off):
        return pl.BlockSpec((ts, LANES), lambda p, i: (jnp.maximum(i - pb, 0), c0 + off * n_pairs + p))

    gspec = pl.BlockSpec((1, LANES), lambda p, i: (0, 0))
    ospec = pl.BlockSpec((ts, LANES), lambda p, i: (i, p))
    shp = jax.ShapeDtypeStruct((s + PAD, n_pairs * LANES), BF16)
    return pl.pallas_call(
        body, grid=(n_pairs, (s + PAD) // ts),
        in_specs=[in_spec(0), in_spec(1), in_spec(2), gspec, gspec],
        out_specs=[ospec, ospec, ospec], out_shape=[shp, shp, shp],
        compiler_params=_params("parallel", "arbitrary"), name=name)(proj, proj, proj, gq2, gk2)


def _qk_norm_bwd(name, dqn, dkn, proj, gq2, gk2, n_pairs):
    s = proj.shape[0]
    ts = min(ROW_TILE, s)
    pb = PAD // ts
    c0 = 3 * n_pairs
    n_r = s // ts

    def body(dqn_ref, dkn_ref, q_ref, k_ref, gq_ref, gk_ref, dq_ref, dk_ref, dgq_ref, dgk_ref):
        first = jnp.logical_and(pl.program_id(0) == 0, pl.program_id(1) == 0)
        last = jnp.logical_and(pl.program_id(0) == n_pairs - 1, pl.program_id(1) == n_r - 1)

        @pl.when(first)
        def _():
            dgq_ref[...] = jnp.zeros_like(dgq_ref)
            dgk_ref[...] = jnp.zeros_like(dgk_ref)

        gm = _group_mean_matrix()
        for dn_ref, x_ref, g_ref, dx_ref, dg_ref in ((dqn_ref, q_ref, gq_ref, dq_ref, dgq_ref),
                                                     (dkn_ref, k_ref, gk_ref, dk_ref, dgk_ref)):
            xv = x_ref[...]
            r = lax.rsqrt(_split_dot(xv * xv, gm, 2) + EPS)
            xhat = xv * r
            dn = dn_ref[...]
            tg = dn * g_ref[...]
            md = _split_dot(tg * xhat, gm, 2)
            dx_ref[...] = (r * (tg - xhat * md)).astype(BF16)
            dg_ref[...] += jnp.sum(dn * xhat, axis=0, keepdims=True)

        @pl.when(last)
        def _():
            for dg_ref in (dgq_ref, dgk_ref):
                gv = dg_ref[...]
                dg_ref[...] = gv + pltpu.roll(gv, HEAD_DIM, axis=1)

    def x_spec(off):
        return pl.BlockSpec((ts, LANES), lambda p, i: (i, c0 + off * n_pairs + p))

    dn_q = pl.BlockSpec((ts, LANES), lambda p, i: (i, p))
    dn_k = pl.BlockSpec((ts, LANES), lambda p, i: (i + pb, p))
    gspec = pl.BlockSpec((1, LANES), lambda p, i: (0, 0))
    ospec = pl.BlockSpec((ts, LANES), lambda p, i: (i, p))
    shp = jax.ShapeDtypeStruct((s, n_pairs * LANES), BF16)
    vec = jax.ShapeDtypeStruct((1, LANES), F32)
    return pl.pallas_call(
        body, grid=(n_pairs, n_r),
        in_specs=[dn_q, dn_k, x_spec(0), x_spec(1), gspec, gspec],
        out_specs=[ospec, ospec, gspec, gspec], out_shape=[shp, shp, vec, vec],
        compiler_params=_params("arbitrary", "arbitrary"), name=name)(dqn, dkn, proj, proj, gq2, gk2)


def _head_blocks(x2):
    lane = lax.broadcasted_iota(jnp.int32, x2.shape, 1)
    zero = jnp.zeros_like(x2)
    return jnp.concatenate([jnp.where(lane < HEAD_DIM, x2, zero), jnp.where(lane >= HEAD_DIM, x2, zero)], axis=0)


def _head_diag(r):
    lane = lax.broadcasted_iota(jnp.int32, (CHUNK, LANES), 1)
    return jnp.where(lane < HEAD_DIM, r[:CHUNK], r[CHUNK:])


def _ca_probs(kw, qb, bias_t, n):
    sc = lax.dot_general(kw, qb, NT, preferred_element_type=F32) * SCALE + bias_t
    jpos = lax.broadcasted_iota(jnp.int32, (BAND, LANES), 0)
    sc = jnp.where(n * CHUNK + jpos >= PAD, sc, NEG_INF)
    p = jnp.exp(sc - jnp.max(sc, axis=0, keepdims=True))
    return p * (1.0 / jnp.sum(p, axis=0, keepdims=True))


def _ca_fwd(name, qn, knp, vp, bias_tt, n_pairs, mixed, gather=()):
    sp_rows = knp.shape[0]
    s = sp_rows - PAD
    t = min(CA_TILE, s)
    cpb = t // CHUNK
    pb = PAD // t

    n_g = len(gather)
    grid = (n_pairs, s // t)

    def body(*refs):
        q_ref, k_ref, v_ref, b_ref = refs[:4]
        o_ref = refs[5 + n_g]
        g_refs, sems = refs[6 + n_g:6 + 2 * n_g], refs[6 + 2 * n_g:]

        def compute():
            qi = pl.program_id(1)
            wss = [pl.multiple_of((qi * cpb + ci) * CHUNK, CHUNK) for ci in range(cpb)]
            ps = [_ca_probs(k_ref[pl.ds(wss[ci], BAND), :], _head_blocks(q_ref[ci * CHUNK:(ci + 1) * CHUNK, :]),
                            b_ref[...], qi * cpb + ci) for ci in range(cpb)]
            for ci in range(cpb):
                r = lax.dot_general(ps[ci].astype(BF16), v_ref[pl.ds(wss[ci], BAND), :], TN,
                                    preferred_element_type=F32)
                o_ref[ci * CHUNK:(ci + 1) * CHUNK, :] = _head_diag(r)

        _ride_gather(grid, g_refs, sems, compute)

    full = pl.BlockSpec((sp_rows, LANES), lambda p, i: (0, p))
    outs = pl.pallas_call(
        body, grid=grid,
        in_specs=[pl.BlockSpec((t, LANES), lambda p, i: (i + pb, p)), full, full,
                  pl.BlockSpec((None, BAND, LANES), lambda p, i: (p, 0, 0)), pl.BlockSpec(memory_space=pl.ANY)]
        + [HBM_SPEC] * n_g,
        out_specs=[pl.BlockSpec((t, LANES), lambda p, i: (i, n_pairs + p))] + [HBM_SPEC] * n_g,
        out_shape=[jax.ShapeDtypeStruct(mixed.shape, mixed.dtype)]
        + [jax.ShapeDtypeStruct(g.shape, g.dtype) for g in gather],
        input_output_aliases={4: 0, **{5 + i: 1 + i for i in range(n_g)}},
        scratch_shapes=_gather_sems(n_g) if n_g else [],
        compiler_params=_params("arbitrary", "arbitrary"), name=name)(qn, knp, vp, bias_tt, mixed, *gather)
    return outs[0], list(outs[1:])


def _ca_bwd(name, qn, knp, vp, bias_tt, dmixed, n_pairs, col0, exchange=()):
    sp_rows = knp.shape[0]
    s = sp_rows - PAD
    t = min(CA_TILE, s)
    cpb = t // CHUNK
    pb = PAD // t
    n_x = len(exchange)
    grid = (n_pairs, s // t)

    def body(*refs):
        x_in, x_out, sems = refs[5:5 + n_x], refs[9 + n_x:9 + 2 * n_x], refs[9 + 2 * n_x:]
        if n_x:
            @pl.when(_grid_edge(grid, False))
            def _():
                for cp in _exchange_copies(x_in, x_out, *sems):
                    cp.start()

        _ca_bwd_body(*refs[:5], *refs[5 + n_x:9 + n_x])
        if n_x:
            @pl.when(_grid_edge(grid, True))
            def _():
                for cp in _exchange_copies(x_in, x_out, *sems):
                    cp.wait()

    def _ca_bwd_body(q_ref, k_ref, v_ref, b_ref, do_ref, dq_ref, dk_ref, dv_ref, db_ref):
        qi = pl.program_id(1)

        @pl.when(qi == 0)
        def _():
            dk_ref[...] = jnp.zeros_like(dk_ref)
            dv_ref[...] = jnp.zeros_like(dv_ref)
            db_ref[...] = jnp.zeros_like(db_ref)

        wss = [pl.multiple_of((qi * cpb + ci) * CHUNK, CHUNK) for ci in range(cpb)]
        kws = [k_ref[pl.ds(ws, BAND), :] for ws in wss]
        qbs = [_head_blocks(q_ref[ci * CHUNK:(ci + 1) * CHUNK, :]) for ci in range(cpb)]
        dbs = [_head_blocks(do_ref[ci * CHUNK:(ci + 1) * CHUNK, :].astype(BF16)) for ci in range(cpb)]
        ps = [_ca_probs(kws[ci], qbs[ci], b_ref[...], qi * cpb + ci) for ci in range(cpb)]
        dps = [lax.dot_general(v_ref[pl.ds(wss[ci], BAND), :], dbs[ci], NT, preferred_element_type=F32)
               for ci in range(cpb)]
        dss = [p * (dp - jnp.sum(p * dp, axis=0, keepdims=True)) for p, dp in zip(ps, dps)]
        for ci in range(cpb):
            dsb = dss[ci].astype(BF16)
            dv_ref[pl.ds(wss[ci], BAND), :] += jnp.dot(ps[ci].astype(BF16), dbs[ci], preferred_element_type=F32)
            dk_ref[pl.ds(wss[ci], BAND), :] += jnp.dot(dsb, qbs[ci], preferred_element_type=F32) * SCALE
            r = lax.dot_general(dsb, kws[ci], TN, preferred_element_type=F32)
            dq_ref[ci * CHUNK:(ci + 1) * CHUNK, :] = _head_diag(r) * SCALE
        total = dss[0]
        for ci in range(1, cpb):
            total = total + dss[ci]
        db_ref[...] += total

    full_in = pl.BlockSpec((sp_rows, LANES), lambda p, i: (0, p))
    btile = pl.BlockSpec((None, BAND, LANES), lambda p, i: (p, 0, 0))
    pad_shape = jax.ShapeDtypeStruct((sp_rows, n_pairs * LANES), F32)
    outs = pl.pallas_call(
        body, grid=grid,
        in_specs=[pl.BlockSpec((t, LANES), lambda p, i: (i + pb, p)), full_in, full_in, btile,
                  pl.BlockSpec((t, LANES), lambda p, i: (i, col0 + p))] + [HBM_SPEC] * n_x,
        out_specs=[pl.BlockSpec((t, LANES), lambda p, i: (i, p)), full_in, full_in, btile] + [HBM_SPEC] * n_x,
        out_shape=[jax.ShapeDtypeStruct((s, n_pairs * LANES), F32), pad_shape, pad_shape,
                   jax.ShapeDtypeStruct(bias_tt.shape, F32)]
        + [jax.ShapeDtypeStruct((3,) + p.shape[1:], p.dtype) for p in exchange],
        scratch_shapes=_exchange_sems(n_x) if n_x else [],
        compiler_params=_params("arbitrary", "arbitrary"), name=name)(qn, knp, vp, bias_tt, dmixed, *exchange)
    return outs[0], outs[1], outs[2], outs[3], list(outs[4:])


def _pair_transposed(tile):
    n_h = tile.shape[0]
    return jnp.transpose(tile.reshape(n_h // 2, 2, CHUNK, BAND), (0, 3, 1, 2)).reshape(n_h // 2, BAND, LANES)


def _query_major(tile_t):
    n_p = tile_t.shape[0]
    return jnp.transpose(tile_t.reshape(n_p, BAND, 2, CHUNK), (3, 0, 2, 1)).reshape(CHUNK, 2 * n_p, BAND)


def _bias_tile(rb):
    n_h = rb.shape[0]
    far = jnp.broadcast_to(rb[:, 2 * REL_CLIP:], (n_h, PAD + CHUNK - REL_CLIP))
    near = rb[:, REL_CLIP - (CHUNK - 1):2 * REL_CLIP][:, ::-1]
    ext = jnp.concatenate([far, near], axis=1)
    return jnp.stack([ext[:, CHUNK - 1 - i:CHUNK - 1 - i + BAND] for i in range(CHUNK)], axis=1)


def _rel_bias_grad(name, db_t):
    _, n_h, _ = db_t.shape
    n_out = 3 * LANES
    assert CHUNK == 64 and REL_CLIP == 128 and LEFT_CHUNKS == 8

    def body(z_ref, o_ref):
        a_idx = lax.broadcasted_iota(jnp.int32, (LANES, n_out), 0)
        r_idx = lax.broadcasted_iota(jnp.int32, (LANES, n_out), 1)
        out = jnp.zeros((n_h, n_out), F32)
        for mm, base in ((0, 191), (1, 255), (2, 319)):
            b = LEFT_CHUNKS - mm
            acc = jnp.zeros((n_h, LANES), F32)
            for i in range(CHUNK):
                row = z_ref[i, :, b * CHUNK:(b + 1) * CHUNK]
                rowp = jnp.concatenate([row, jnp.zeros((n_h, LANES - CHUNK), F32)], axis=1)
                acc = acc + pltpu.roll(rowp, CHUNK - 1 - i, axis=1)
            place = (r_idx == jnp.minimum(base - a_idx, 2 * REL_CLIP)).astype(BF16)
            out = out + _split_dot(acc, place, 3)
        far = jnp.zeros((n_h, 1), F32)
        for i in range(CHUNK):
            far = far + jnp.sum(z_ref[i, :, 0:(LEFT_CHUNKS - 2) * CHUNK], axis=1, keepdims=True)
        lane = lax.broadcasted_iota(jnp.int32, (n_h, n_out), 1)
        o_ref[...] = out + jnp.where(lane == 2 * REL_CLIP, far, 0.0)

    return pl.pallas_call(
        body, out_shape=jax.ShapeDtypeStruct((n_h, n_out), F32),
        in_specs=[pl.BlockSpec(memory_space=pltpu.VMEM)], out_specs=pl.BlockSpec(memory_space=pltpu.VMEM),
        name=name)(db_t)


def _mesh_pos():
    return lax.axis_index("x"), lax.axis_index("y"), lax.axis_index("c")


def _other_chips(x, y):
    return [(1 - x, y), (x, 1 - y), (1 - x, 1 - y)]


def _all_gather_small(name, blk):
    m_per, n = blk.shape

    def body(x_ref, out_ref, send_sems, recv_sems, local_sem):
        x, y, c = _mesh_pos()
        me, sibling = (x, y, c), (x, y, 1 - c)
        chips = _other_chips(x, y)

        def rows(px, py, pc):
            return out_ref.at[pl.ds((4 * px + 2 * py + pc) * m_per, m_per), :]

        def copy(k, block, to, src=None):
            return pltpu.make_async_remote_copy(
                src_ref=rows(*block) if src is None else src, dst_ref=rows(*block),
                send_sem=send_sems.at[k], recv_sem=recv_sems.at[k], device_id=to, device_id_type=MESH)

        mine = pltpu.make_async_copy(x_ref, rows(*me), local_sem)
        mine.start()
        first = [copy(0, me, sibling, src=x_ref)]
        first += [copy(1 + j, me, (*chip, c), src=x_ref) for j, chip in enumerate(chips)]
        for cp in first:
            cp.start()
        passed = [copy(4 + j, (*chip, c), sibling) for j, chip in enumerate(chips)]
        for j, chip in enumerate(chips):
            copy(1 + j, (*chip, c), me).wait_recv()
            passed[j].start()
        copy(0, sibling, me).wait_recv()
        for j, chip in enumerate(chips):
            copy(4 + j, (*chip, 1 - c), me).wait_recv()
        for cp in first + passed:
            cp.wait_send()
        mine.wait()

    return pl.pallas_call(
        body, out_shape=jax.ShapeDtypeStruct((N_DEV * m_per, n), blk.dtype),
        in_specs=[pl.BlockSpec(memory_space=pltpu.VMEM)], out_specs=pl.BlockSpec(memory_space=pltpu.VMEM),
        scratch_shapes=[pltpu.SemaphoreType.DMA((7,)), pltpu.SemaphoreType.DMA((7,)), pltpu.SemaphoreType.DMA],
        compiler_params=pltpu.CompilerParams(vmem_limit_bytes=VMEM_LIMIT), name=name)(blk)


HBM_SPEC = pl.BlockSpec(memory_space=pltpu.HBM)


def _gather_copies(bufs, send_sems, recv_sems, phase):
    x, y, c = _mesh_pos()
    my_j = 2 * x + y
    starts, lands = [], []
    for t, buf in enumerate(bufs):
        rh = buf.shape[2] // 2
        for k, (px, py) in enumerate(_other_chips(x, y)):
            their_j = 2 * px + py
            if phase == 0:
                src, to = buf.at[0, my_j, pl.ds(c * rh, rh), :], (px, py, c)
                land = buf.at[0, their_j, pl.ds(c * rh, rh), :]
            else:
                src, to = buf.at[0, their_j, pl.ds(c * rh, rh), :], (x, y, 1 - c)
                land = buf.at[0, their_j, pl.ds((1 - c) * rh, rh), :]
            sems = dict(send_sem=send_sems.at[phase, t, k], recv_sem=recv_sems.at[phase, t, k],
                        device_id=to, device_id_type=MESH)
            starts.append(pltpu.make_async_remote_copy(src_ref=src, dst_ref=src, **sems))
            lands.append(pltpu.make_async_remote_copy(src_ref=src, dst_ref=land, **sems))
    return starts, lands


def _gather_sems(n):
    return [pltpu.SemaphoreType.DMA((2, n, 3)), pltpu.SemaphoreType.DMA((2, n, 3))]


def _all_gather_weights(name, bufs, phases):
    n = len(bufs)

    def body(*refs):
        outs = refs[n:2 * n]
        send_sems, recv_sems = refs[2 * n:]
        for phase in phases:
            starts, lands = _gather_copies(outs, send_sems, recv_sems, phase)
            for cp in starts:
                cp.start()
            for cp, ld in zip(starts, lands):
                cp.wait_send()
                ld.wait_recv()

    return pl.pallas_call(
        body, out_shape=[jax.ShapeDtypeStruct(b.shape, b.dtype) for b in bufs],
        in_specs=[HBM_SPEC] * n, out_specs=[HBM_SPEC] * n, input_output_aliases={t: t for t in range(n)},
        scratch_shapes=_gather_sems(n), name=name)(*bufs)


def _gather_buffer(shard, chip):
    r, c = shard.shape
    return lax.dynamic_update_slice(jnp.zeros((1, N_CHIPS, r, c), BF16), shard.astype(BF16)[None, None],
                                    (0, chip, 0, 0))


def _exchange_sibling_halves(name, grads):
    n = len(grads)

    def body(*refs):
        srcs, outs = refs[:n], refs[n:2 * n]
        send_sems, recv_sems = refs[2 * n:]
        x, y, c = _mesh_pos()
        cps = []
        for t in range(n):
            rh = grads[t].shape[1] // 2
            cps.append(pltpu.make_async_remote_copy(
                src_ref=srcs[t].at[:, pl.ds((1 - c) * rh, rh), :], dst_ref=outs[t],
                send_sem=send_sems.at[t], recv_sem=recv_sems.at[t], device_id=(x, y, 1 - c), device_id_type=MESH))
        for cp in cps:
            cp.start()
        for cp in cps:
            cp.wait()

    out_shape = [jax.ShapeDtypeStruct((g.shape[0], g.shape[1] // 2, g.shape[2]), g.dtype) for g in grads]
    return pl.pallas_call(
        body, out_shape=out_shape, in_specs=[HBM_SPEC] * n, out_specs=[HBM_SPEC] * n,
        scratch_shapes=[pltpu.SemaphoreType.DMA((n,)), pltpu.SemaphoreType.DMA((n,))],
        name=name)(*grads)


def _add_halves(name, g, recv, c_idx):
    nb, r, c = g.shape
    rh = r // 2
    tr = rh
    for cand in (512, 256, 128, 64):
        if rh % cand == 0:
            tr = cand
            break
    g4 = g.reshape(nb, 2, rh, c)

    def body(c_ref, g_ref, r_ref, o_ref):
        o_ref[...] = (g_ref[...] + r_ref[...]).astype(BF16)

    grid_spec = pltpu.PrefetchScalarGridSpec(
        num_scalar_prefetch=1, grid=(nb, rh // tr),
        in_specs=[pl.BlockSpec((None, None, tr, c), lambda j, i, cr: (j, cr[0], i, 0)),
                  pl.BlockSpec((None, tr, c), lambda j, i, cr: (j, i, 0))],
        out_specs=pl.BlockSpec((None, tr, c), lambda j, i, cr: (j, i, 0)))
    return pl.pallas_call(
        body, grid_spec=grid_spec, out_shape=jax.ShapeDtypeStruct((nb, rh, c), BF16),
        compiler_params=_params("parallel", "parallel"), name=name)(c_idx, g4, recv)


def _exchange_copies(srcs, outs, send_sems, recv_sems):
    x, y, c = _mesh_pos()
    return [pltpu.make_async_remote_copy(
        src_ref=srcs[t].at[2 * px + py], dst_ref=outs[t].at[k], send_sem=send_sems.at[t, k],
        recv_sem=recv_sems.at[t, k], device_id=(px, py, c), device_id_type=MESH)
        for t in range(len(srcs)) for k, (px, py) in enumerate(_other_chips(x, y))]


def _exchange_sems(n):
    return [pltpu.SemaphoreType.DMA((n, 3)), pltpu.SemaphoreType.DMA((n, 3))]


def _exchange_chips(name, parts):
    n = len(parts)

    def body(*refs):
        cps = _exchange_copies(refs[:n], refs[n:2 * n], *refs[2 * n:])
        for cp in cps:
            cp.start()
        for cp in cps:
            cp.wait()

    return pl.pallas_call(
        body, out_shape=[jax.ShapeDtypeStruct((3,) + p.shape[1:], p.dtype) for p in parts],
        in_specs=[HBM_SPEC] * n, out_specs=[HBM_SPEC] * n, scratch_shapes=_exchange_sems(n), name=name)(*parts)


def _sum_chips(name, own_parts, recvs, pos_idx):
    n_layers = len(own_parts)
    _, rh, c = recvs[0].shape
    tr = rh
    for cand in (512, 256, 128, 64):
        if rh % cand == 0:
            tr = cand
            break
    nb = rh // tr

    def body(pos_ref, *refs):
        own_refs, r_refs, o_ref = refs[:n_layers], refs[n_layers:2 * n_layers], refs[2 * n_layers]
        for l_i in range(n_layers):
            @pl.when(pl.program_id(0) == l_i)
            def _():
                acc = own_refs[l_i][...].astype(F32)
                for k in range(3):
                    acc = acc + r_refs[l_i][k].astype(F32)
                o_ref[...] = acc

    grid_spec = pltpu.PrefetchScalarGridSpec(
        num_scalar_prefetch=1, grid=(n_layers, nb),
        in_specs=[pl.BlockSpec((None, tr, c), lambda l, i, pr: (pr[0], i, 0)) for _ in range(n_layers)]
        + [pl.BlockSpec((3, tr, c), lambda l, i, pr: (0, i, 0)) for _ in range(n_layers)],
        out_specs=pl.BlockSpec((None, tr, c), lambda l, i, pr: (l, pr[1] * nb + i, 0)))
    return pl.pallas_call(
        body, grid_spec=grid_spec, out_shape=jax.ShapeDtypeStruct((n_layers, 2 * rh, c), F32),
        compiler_params=_params("parallel", "parallel"), name=name)(pos_idx, *own_parts, *recvs)


def _share_with_sibling(name, grads):
    n = len(grads)

    def body(*refs):
        outs = refs[n:2 * n]
        send_sems, recv_sems = refs[2 * n:]
        x, y, c = _mesh_pos()
        cps, landed = [], []
        for t in range(n):
            rh = grads[t].shape[1] // 2
            mine = outs[t].at[:, pl.ds(c * rh, rh), :]
            theirs = outs[t].at[:, pl.ds((1 - c) * rh, rh), :]
            cps.append(pltpu.make_async_remote_copy(
                src_ref=mine, dst_ref=mine, send_sem=send_sems.at[t], recv_sem=recv_sems.at[t],
                device_id=(x, y, 1 - c), device_id_type=MESH))
            landed.append(pltpu.make_async_remote_copy(
                src_ref=mine, dst_ref=theirs, send_sem=send_sems.at[t], recv_sem=recv_sems.at[t],
                device_id=(x, y, 1 - c), device_id_type=MESH))
        for cp in cps:
            cp.start()
        for cp, ld in zip(cps, landed):
            cp.wait_send()
            ld.wait_recv()

    out_shape = [jax.ShapeDtypeStruct(g.shape, g.dtype) for g in grads]
    return pl.pallas_call(
        body, out_shape=out_shape, in_specs=[HBM_SPEC] * n, out_specs=[HBM_SPEC] * n,
        input_output_aliases={t: t for t in range(n)},
        scratch_shapes=[pltpu.SemaphoreType.DMA((n,)), pltpu.SemaphoreType.DMA((n,))],
        name=name)(*grads)


def _row(v):
    return v.reshape(1, -1)


def kernel(x, c, g_norm1, w_in, g_q, g_k, rel_bias, w_o, g_norm2, w1, w2, w_ada, b_ada, loss_target, m_g_norm1, m_w_in, m_g_q, m_g_k, m_rel_bias, m_w_o, m_g_norm2, m_w1, m_w2, m_w_ada, m_b_ada, v_g_norm1, v_w_in, v_g_q, v_g_k, v_rel_bias, v_w_o, v_g_norm2, v_w1, v_w2, v_w_ada, v_b_ada):
    _, s, d = x.shape
    n_layers = g_norm1.shape[0]
    n_pairs = (d // 2) // LANES
    n_ca_heads = rel_bias.shape[1]
    n_rel = rel_bias.shape[2]
    assert n_rel == 2 * REL_CLIP + 1 and g_q.shape[1] == HEAD_DIM and n_ca_heads == 2 * n_pairs
    ada_c = w_ada.shape[2]

    ax, ay, ac = _mesh_pos()
    chip = 2 * ax + ay
    dev = 4 * ax + 2 * ay + ac
    c_idx = jnp.reshape(ac, (1,)).astype(jnp.int32)

    c_act_all = _all_gather_small("ag_c", _silu_rows("silu_c", c)).reshape(N_DEV, 8, d)[:, 0, :]
    mod_parts = []
    for l in range(n_layers):
        bias = lax.dynamic_slice_in_dim(b_ada[l], chip * ada_c, ada_c).reshape(1, ada_c)
        mod_parts.append(_ada_mod(f"ada_mod{l}", c_act_all, w_ada[l], bias))
    mod_all = _all_gather_small("ag_mod", jnp.concatenate(mod_parts, axis=1))
    mod_all = mod_all.reshape(N_CHIPS, 2, N_DEV, n_layers, ada_c)[:, 0]
    mod_all = jnp.transpose(mod_all, (1, 2, 0, 3)).reshape(N_DEV, n_layers, N_CHIPS * ada_c)
    mod = lax.dynamic_index_in_dim(mod_all, dev, axis=0, keepdims=False)

    gathered = [[_gather_buffer(w[l], chip) for w in (w_in, w_o, w1, w2)] for l in range(n_layers)]
    gathered[0][:1] = _all_gather_weights("ag_w_in_0", gathered[0][:1], (0, 1))

    xs = x[0]
    saved = []
    for l in range(n_layers):
        sh1, sc1, gt1, sh2, sc2, gt2 = [_row(mod[l, i * d:(i + 1) * d]) for i in range(6)]
        gq2 = _row(jnp.tile(g_q[l], 2))
        gk2 = _row(jnp.tile(g_k[l], 2))
        bias_t = _pair_transposed(_bias_tile(rel_bias[l]))
        h1 = _norm_mod(f"norm1_{l}", xs, _row(g_norm1[l]), sc1, sh1)
        proj = _mm_nn_w(f"proj_{l}", h1, gathered[l][0], 0, True, [F32])[0]
        o_sb, landed = _sb_fwd(f"sb_fwd_{l}", proj, n_pairs, gather=gathered[l][1:] if l == 0 else ())
        if landed:
            gathered[l][1:] = _all_gather_weights(f"ag_rest_{l}", landed, (1,))
        wg_in, wg_o, wg_1, wg_2 = gathered[l]
        nxt = gathered[l + 1] if l + 1 < n_layers else []
        qn, knp, vp = _qk_norm(f"qk_norm_{l}", proj, gq2, gk2, n_pairs)
        mixed, landed = _ca_fwd(f"ca_fwd_{l}", qn, knp, vp, bias_t, n_pairs, o_sb, gather=nxt[:2])

        def res_epi(acc, res, gt):
            return res + gt * acc, acc

        def res_specs(tm, tn):
            return (pl.BlockSpec((tm, tn), lambda i, j, kk: (i, j)), pl.BlockSpec((1, tn), lambda i, j, kk: (0, j)))

        x1, m1 = _mm_nn_w(f"attn_out_{l}", mixed, wg_o, 0, False, [F32, BF16], epi=res_epi,
                          extras=(xs, gt1), extra_specs_fn=res_specs)
        h2 = _norm_mod(f"norm2_{l}", x1, _row(g_norm2[l]), sc2, sh2)

        def act_epi(acc):
            r = jnp.maximum(acc, 0.0)
            return acc, r * r

        if l + 1 < n_layers:
            (u, a), landed_1 = _mm_nn_w(f"mlp_up_{l}", h2, wg_1, 0, True, [BF16, BF16], epi=act_epi, gather=nxt[2:3])
            (x2, m2), landed_2 = _mm_nn_w(f"mlp_down_{l}", a, wg_2, 0, False, [F32, BF16], epi=res_epi,
                                          extras=(x1, gt2), extra_specs_fn=res_specs, gather=nxt[3:])
            gathered[l + 1] = _all_gather_weights(f"ag_weights_{l + 1}", landed + landed_1 + landed_2, (1,))
        else:
            u, a = _mm_nn_w(f"mlp_up_{l}", h2, wg_1, 0, True, [BF16, BF16], epi=act_epi)
            x2, m2 = _mm_nn_w(f"mlp_down_{l}", a, wg_2, 0, False, [F32, BF16], epi=res_epi,
                              extras=(x1, gt2), extra_specs_fn=res_specs)
        saved.append(dict(x0=xs, h1=h1, proj=proj, o_sb=mixed, qn=qn, knp=knp, vp=vp, bias_t=bias_t, mixed=mixed,
                          m1=m1, x1=x1, h2=h2, u=u, a=a, m2=m2, gq2=gq2, gk2=gk2,
                          sc1=sc1, gt1=gt1, sc2=sc2, gt2=gt2))
        xs = x2

    dx, sq = _loss_grad("loss", xs, loss_target[0])
    loss_part = (0.5 * jnp.sum(sq) / d).reshape(1)

    g_in, g_o, g_1, g_2 = [None] * n_layers, [None] * n_layers, [None] * n_layers, [None] * n_layers
    dmod, dg1, dg2, dgq, dgk, drel = [], [], [], [], [], []
    partial, chip_sum, from_chip = {}, {}, {}

    def reduce_on_chip(tag, keys):
        grads = [partial[k] for k in keys]
        for k, g, r in zip(keys, grads, _exchange_sibling_halves(f"rs_sibling_{tag}", grads)):
            chip_sum[k] = _add_halves(f"rs_add_{k[0]}_{k[1]}", g, r, c_idx)

    for l in reversed(range(n_layers)):
        sv = saved[l]
        wg_in, wg_o, wg_1, wg_2 = gathered[l]
        dm2, dgt2 = _gate_bwd(f"gate2_bwd_{l}", dx, sv["m2"], sv["gt2"])

        def act_bwd_epi(acc, u_t):
            return (acc * (2.0 * jnp.maximum(u_t.astype(F32), 0.0)),)

        def tile_specs(tm, tn):
            return (pl.BlockSpec((tm, tn), lambda i, j, kk: (i, j)),)

        du = _mm_nt_w(f"mlp_down_bwd_{l}", dm2, wg_2, 0, False, BF16, epi=act_bwd_epi, extras=(sv["u"],),
                      extra_specs_fn=tile_specs)
        g_2[l] = _mm_tn(f"w2_grad_{l}", sv["a"], dm2, False)
        dh2 = _mm_nt_w(f"mlp_up_bwd_{l}", du, wg_1, 0, True, F32)
        g_1[l] = _mm_tn(f"w1_grad_{l}", sv["h2"], du, True)
        dx1, dsh2, dsc2, dgn2 = _norm_mod_bwd(f"norm2_bwd_{l}", dh2, sv["x1"], _row(g_norm2[l]), sv["sc2"], dx)
        dm1, dgt1 = _gate_bwd(f"gate1_bwd_{l}", dx1, sv["m1"], sv["gt1"])
        dmixed = _mm_nt_w(f"attn_out_bwd_{l}", dm1, wg_o, 0, False, F32)
        g_o[l] = _mm_tn(f"wo_grad_{l}", sv["mixed"], dm1, False)
        partial.update({(1, l): g_o[l], (2, l): g_1[l], (3, l): g_2[l]})
        ride_sb, ride_ca = [], []
        if l == 0:
            early = sorted(partial)
            reduce_on_chip("early", early)
            ride_ca = [k for k in early if k[0] == 0 or k == (3, 0)]
            ride_sb = [k for k in early if k not in ride_ca]
        dq_sb, dk_sb, dv_sb, landed = _sb_bwd(f"sb_bwd_{l}", sv["proj"], sv["o_sb"], dmixed, n_pairs,
                                              exchange=[chip_sum[k] for k in ride_sb])
        from_chip.update(zip(ride_sb, landed))
        dqn, dknp, dvp, dbias_t, landed = _ca_bwd(f"ca_bwd_{l}", sv["qn"], sv["knp"], sv["vp"], sv["bias_t"], dmixed,
                                                  n_pairs, n_pairs, exchange=[chip_sum[k] for k in ride_ca])
        from_chip.update(zip(ride_ca, landed))
        dq_ca, dk_ca, dgq_l, dgk_l = _qk_norm_bwd(f"qk_norm_bwd_{l}", dqn, dknp, sv["proj"], sv["gq2"], sv["gk2"],
                                                  n_pairs)
        drel_l = _rel_bias_grad(f"rel_bias_grad_{l}", _query_major(dbias_t))[:, :n_rel]
        dproj = jnp.concatenate([dq_sb.astype(BF16), dk_sb.astype(BF16), dv_sb.astype(BF16), dq_ca, dk_ca,
                                 dvp[PAD:].astype(BF16)], axis=1)
        g_in[l] = _mm_tn(f"win_grad_{l}", sv["h1"], dproj, True)
        partial[(0, l)] = g_in[l]
        dh1 = _mm_nt_w(f"proj_bwd_{l}", dproj, wg_in, 0, True, F32)
        dx, dsh1, dsc1, dgn1 = _norm_mod_bwd(f"norm1_bwd_{l}", dh1, sv["x0"], _row(g_norm1[l]), sv["sc1"], dx1)
        dmod.insert(0, jnp.concatenate([dsh1, dsc1, dgt1, dsh2, dsc2, dgt2], axis=1)[0])
        dg1.insert(0, dgn1[0])
        dg2.insert(0, dgn2[0])
        dgq.insert(0, dgq_l[0, :HEAD_DIM])
        dgk.insert(0, dgk_l[0, :HEAD_DIM])
        drel.insert(0, drel_l.reshape(-1))
    grad_x = dx[None]

    small = jnp.concatenate([jnp.concatenate(dmod), jnp.concatenate(dg1), jnp.concatenate(dg2),
                             jnp.concatenate(dgq), jnp.concatenate(dgk), jnp.concatenate(drel), loss_part])
    n_small = small.shape[0]
    n_pack = -(-n_small // (8 * LANES)) * (8 * LANES)
    small = jnp.pad(small, (0, n_pack - n_small)).reshape(8, n_pack // 8)
    small_all = _all_gather_small("ag_small", small).reshape(N_DEV, 8, n_pack // 8)
    small_sum = _sum_rows8("sum_small", small_all).reshape(-1)
    dmod_all = small_all.reshape(N_DEV, n_pack)[:, :n_layers * 6 * d].reshape(N_DEV, n_layers, 6 * d)

    off = 0

    def take(n, shape):
        nonlocal off
        out = small_sum[off:off + n].reshape(shape)
        off += n
        return out

    grad_b_ada = take(n_layers * 6 * d, (n_layers, 6 * d))
    grad_g_norm1 = take(n_layers * d, (n_layers, d))
    grad_g_norm2 = take(n_layers * d, (n_layers, d))
    grad_g_q = take(n_layers * HEAD_DIM, (n_layers, HEAD_DIM))
    grad_g_k = take(n_layers * HEAD_DIM, (n_layers, HEAD_DIM))
    grad_rel_bias = take(n_layers * n_ca_heads * n_rel, (n_layers, n_ca_heads, n_rel))
    loss = take(1, ())

    c_act_t = jnp.transpose(c_act_all)
    grad_w_ada = jnp.stack([
        _outer_sum(f"wada_grad_{l}", c_act_t,
                   lax.dynamic_slice_in_dim(dmod_all[:, l, :], chip * ada_c, ada_c, axis=1))
        for l in range(n_layers)])

    late = [k for k in sorted(partial) if k not in from_chip]
    reduce_on_chip("late", late)
    from_chip.update(zip(late, _exchange_chips("rs_chips", [chip_sum[k] for k in late])))
    pos_idx = jnp.stack([chip, ac]).astype(jnp.int32)
    reduced = [_sum_chips(f"rs_sum_{w_i}", [chip_sum[(w_i, l)] for l in range(n_layers)],
                          [from_chip[(w_i, l)] for l in range(n_layers)], pos_idx) for w_i in range(4)]
    grad_w_in, grad_w_o, grad_w1, grad_w2 = _share_with_sibling("rs_share", reduced)

    def adam_big(name, w, g, m, v):
        shp = w.shape
        outs = _adamw(name, w.reshape(-1, shp[-1]), g.reshape(-1, shp[-1]), m.reshape(-1, shp[-1]),
                      v.reshape(-1, shp[-1]))
        return [o.reshape(shp) for o in outs]

    def pack(arrs):
        flat = jnp.concatenate([a.reshape(-1) for a in arrs])
        n = flat.shape[0]
        n_p = -(-n // (8 * LANES)) * (8 * LANES)
        return jnp.pad(flat, (0, n_p - n), constant_values=1.0).reshape(8, n_p // 8)

    small_w = [g_norm1, g_q, g_k, rel_bias, g_norm2, b_ada]
    small_g = [grad_g_norm1, grad_g_q, grad_g_k, grad_rel_bias, grad_g_norm2, grad_b_ada]
    small_m = [m_g_norm1, m_g_q, m_g_k, m_rel_bias, m_g_norm2, m_b_ada]
    small_v = [v_g_norm1, v_g_q, v_g_k, v_rel_bias, v_g_norm2, v_b_ada]
    packed = _adamw("adamw_small", pack(small_w), pack(small_g), pack(small_m), pack(small_v))

    def unpack(p):
        flat = p.reshape(-1)
        res, o = [], 0
        for a in small_w:
            res.append(flat[o:o + a.size].reshape(a.shape))
            o += a.size
        return res

    sd, sm, sv_ = unpack(packed[0]), unpack(packed[1]), unpack(packed[2])
    big = {
        "w_in": adam_big("adamw_w_in", w_in, grad_w_in, m_w_in, v_w_in),
        "w_o": adam_big("adamw_w_o", w_o, grad_w_o, m_w_o, v_w_o),
        "w1": adam_big("adamw_w1", w1, grad_w1, m_w1, v_w1),
        "w2": adam_big("adamw_w2", w2, grad_w2, m_w2, v_w2),
        "w_ada": adam_big("adamw_w_ada", w_ada, grad_w_ada, m_w_ada, v_w_ada),
    }

    def ordered(kind):
        sm_list = (sd, sm, sv_)[kind]
        return [sm_list[0], big["w_in"][kind], sm_list[1], sm_list[2], sm_list[3], big["w_o"][kind], sm_list[4],
                big["w1"][kind], big["w2"][kind], big["w_ada"][kind], sm_list[5]]

    grads = [grad_g_norm1, grad_w_in, grad_g_q, grad_g_k, grad_rel_bias, grad_w_o, grad_g_norm2, grad_w1, grad_w2,
             grad_w_ada, grad_b_ada]
    return (loss, grad_x, *grads, *ordered(0), *ordered(1), *ordered(2))
```

```python
import functools

import jax
import jax.numpy as jnp
from jax import lax
from jax.experimental import pallas as pl
from jax.experimental.pallas import tpu as pltpu

F32 = jnp.float32
BF16 = jnp.bfloat16
MESH = pl.DeviceIdType.MESH

EPS = 1e-6
NEG_INF = -1e30
HEAD_DIM = 64
CHUNK = 64
LEFT_CHUNKS = 8
PAD = LEFT_CHUNKS * CHUNK
BAND = PAD + CHUNK
REL_CLIP = 128
SCALE = HEAD_DIM ** -0.5
DEAD_LOG = -88.0
LANES = 128
N_CHIPS = 4
N_DEV = 8

ADAM_LR = 0.001
ADAM_B1 = 0.9
ADAM_B2 = 0.999
ADAM_EPS = 1e-08
ADAM_WD = 0.01
ADAM_STEP = 10

VMEM_LIMIT = 48 * 1024 * 1024
ROW_TILE = 512
MM_TILE = 1024
SB_TILE = 256
CA_TILE = 256

NT = (((1,), (1,)), ((), ()))
TN = (((0,), (0,)), ((), ()))
NN = (((1,), (0,)), ((), ()))


def _tile(n, pref):
    best = None
    t = LANES
    while t <= min(n, pref):
        if n % t == 0:
            best = t
        t += LANES
    return best if best is not None else n


def _params(*sem):
    return pltpu.CompilerParams(dimension_semantics=sem, vmem_limit_bytes=VMEM_LIMIT)


def _split_dot(x, t, passes):
    acc = None
    r = x
    for i in range(passes):
        hi = r.astype(BF16)
        d = jnp.dot(hi, t, preferred_element_type=F32)
        acc = d if acc is None else acc + d
        if i + 1 < passes:
            r = r - hi.astype(F32)
    return acc


def _grid_edge(grid, last):
    conds = [pl.program_id(i) == (g - 1 if last else 0) for i, g in enumerate(grid)]
    return functools.reduce(jnp.logical_and, conds)


def _ride_gather(grid, g_refs, sems, compute):
    if g_refs:
        @pl.when(_grid_edge(grid, False))
        def _():
            for cp in _gather_copies(g_refs, *sems, 0)[0]:
                cp.start()

    compute()
    if g_refs:
        @pl.when(_grid_edge(grid, True))
        def _():
            for cp, ld in zip(*_gather_copies(g_refs, *sems, 0)):
                cp.wait_send()
                ld.wait_recv()


def _rider(ins, out_shapes, sems, copies, aliased):
    return dict(ins=list(ins), outs=list(out_shapes), sems=list(sems), copies=copies, aliased=aliased)


def _gather_rider(bufs):
    def copies(_, outs, sems):
        def both(cp, ld):
            def wait():
                cp.wait_send()
                ld.wait_recv()
            return wait
        return [(cp, both(cp, ld)) for cp, ld in zip(*_gather_copies(outs, *sems, 0))]

    return _rider(bufs, [jax.ShapeDtypeStruct(b.shape, b.dtype) for b in bufs], _gather_sems(len(bufs)), copies, True)


def _exchange_rider(parts):
    def copies(ins, outs, sems):
        return [(cp, cp.wait) for cp in _exchange_copies(ins, outs, *sems)]

    return _rider(parts, [jax.ShapeDtypeStruct((3,) + p.shape[1:], p.dtype) for p in parts],
                  _exchange_sems(len(parts)), copies, False)


def _sibling_rider(grads):
    def copies(ins, outs, sems):
        return [(cp, cp.wait) for cp in _sibling_copies(ins, outs, *sems)]

    n = len(grads)
    return _rider(grads, [jax.ShapeDtypeStruct((g.shape[0], g.shape[1] // 2, g.shape[2]), g.dtype) for g in grads],
                  [pltpu.SemaphoreType.DMA((n,)), pltpu.SemaphoreType.DMA((n,))], copies, False)


def _matmul(name, a, b, *, dims, grid, a_spec, b_spec, out_shapes, out_specs, acc_shape,
            epi=None, extras=(), extra_specs=(), riders=()):
    nk = grid[2]
    n_ex = len(extras)
    n_out = len(out_shapes)
    n_in = [len(r["ins"]) for r in riders]
    n_ro = [len(r["outs"]) for r in riders]

    def body(*refs):
        a_ref, b_ref = refs[0], refs[1]
        ex = refs[2:2 + n_ex]
        pos = 2 + n_ex
        r_in = []
        for n in n_in:
            r_in.append(refs[pos:pos + n])
            pos += n
        outs = refs[pos:pos + n_out]
        pos += n_out
        r_out = []
        for n in n_ro:
            r_out.append(refs[pos:pos + n])
            pos += n
        acc = refs[pos]
        pos += 1
        r_sems = []
        for r in riders:
            r_sems.append(refs[pos:pos + len(r["sems"])])
            pos += len(r["sems"])

        def all_copies():
            return [c for r, i, o, sm in zip(riders, r_in, r_out, r_sems) for c in r["copies"](i, o, sm)]

        if riders:
            @pl.when(_grid_edge(grid, False))
            def _():
                for cp, _ in all_copies():
                    cp.start()

        k = pl.program_id(2)

        @pl.when(k == 0)
        def _():
            acc[...] = jnp.zeros_like(acc)

        acc[...] += lax.dot_general(a_ref[...].astype(BF16), b_ref[...].astype(BF16), dims,
                                    preferred_element_type=F32)

        @pl.when(k == nk - 1)
        def _():
            res = epi(acc[...], *[e[...] for e in ex]) if epi is not None else (acc[...],)
            for o, r in zip(outs, res):
                o[...] = r.astype(o.dtype)

        if riders:
            @pl.when(_grid_edge(grid, True))
            def _():
                for _, wait in all_copies():
                    wait()

    r_ins = [x for r in riders for x in r["ins"]]
    r_outs = [x for r in riders for x in r["outs"]]
    aliases, i_pos, o_pos = {}, 2 + n_ex, n_out
    for r in riders:
        if r["aliased"]:
            aliases.update({i_pos + i: o_pos + i for i in range(len(r["ins"]))})
        i_pos += len(r["ins"])
        o_pos += len(r["outs"])
    outs = pl.pallas_call(
        body, grid=grid, in_specs=[a_spec, b_spec, *extra_specs] + [HBM_SPEC] * len(r_ins),
        out_specs=list(out_specs) + [HBM_SPEC] * len(r_outs), out_shape=list(out_shapes) + r_outs,
        input_output_aliases=aliases,
        scratch_shapes=[pltpu.VMEM(acc_shape, F32)] + [sm for r in riders for sm in r["sems"]],
        compiler_params=_params(*(("arbitrary",) * 3 if riders else ("parallel", "parallel", "arbitrary"))), name=name,
    )(a, b, *extras, *r_ins)
    landed, pos = [], n_out
    for n in n_ro:
        landed.append(list(outs[pos:pos + n]))
        pos += n
    return list(outs[:n_out]), landed


def _mm_nn_w(name, a, wg, layer, colshard, out_dtypes, epi=None, extras=(), extra_specs_fn=None, gather=()):
    m = a.shape[0]
    _, _, r, c = wg.shape
    tm = min(MM_TILE, m)
    if colshard:
        k, n = r, N_CHIPS * c
        tk, tn = k, _tile(c, MM_TILE)
        npc = c // tn
        b_spec = pl.BlockSpec((None, None, tk, tn), lambda i, j, kk: (layer, j // npc, kk, j % npc))
    else:
        k, n = N_CHIPS * r, c
        tk, tn = _tile(r, MM_TILE), _tile(c, MM_TILE)
        kpc = r // tk
        b_spec = pl.BlockSpec((None, None, tk, tn), lambda i, j, kk: (layer, kk // kpc, kk % kpc, j))
    assert a.shape[1] == k
    grid = (m // tm, n // tn, k // tk)
    out_spec = pl.BlockSpec((tm, tn), lambda i, j, kk: (i, j))
    ex_specs = extra_specs_fn(tm, tn) if extra_specs_fn is not None else ()
    outs, landed = _matmul(name, a, wg, dims=NN, grid=grid,
                           a_spec=pl.BlockSpec((tm, tk), lambda i, j, kk: (i, kk)), b_spec=b_spec,
                           out_shapes=[jax.ShapeDtypeStruct((m, n), d) for d in out_dtypes],
                           out_specs=[out_spec for _ in out_dtypes], acc_shape=(tm, tn),
                           epi=epi, extras=extras, extra_specs=ex_specs,
                           riders=[_gather_rider(gather)] if gather else ())
    return (outs, landed[0]) if gather else outs


def _mm_nt_w(name, a, wg, layer, colshard, out_dtype, epi=None, extras=(), extra_specs_fn=None, riders=()):
    m = a.shape[0]
    _, _, r, c = wg.shape
    tm = min(MM_TILE, m)
    if colshard:
        kdim, n = N_CHIPS * c, r
        tk, tn = _tile(c, MM_TILE), _tile(r, MM_TILE)
        kpc = c // tk
        b_spec = pl.BlockSpec((None, None, tn, tk), lambda i, j, kk: (layer, kk // kpc, j, kk % kpc))
    else:
        kdim, n = c, N_CHIPS * r
        tk, tn = c, _tile(r, MM_TILE)
        npc = r // tn
        b_spec = pl.BlockSpec((None, None, tn, tk), lambda i, j, kk: (layer, j // npc, j % npc, kk))
    assert a.shape[1] == kdim
    grid = (m // tm, n // tn, kdim // tk)
    ex_specs = extra_specs_fn(tm, tn) if extra_specs_fn is not None else ()
    outs, landed = _matmul(name, a, wg, dims=NT, grid=grid,
                   a_spec=pl.BlockSpec((tm, tk), lambda i, j, kk: (i, kk)), b_spec=b_spec,
                   out_shapes=[jax.ShapeDtypeStruct((m, n), out_dtype)],
                   out_specs=[pl.BlockSpec((tm, tn), lambda i, j, kk: (i, j))], acc_shape=(tm, tn),
                   epi=epi, extras=extras, extra_specs=ex_specs, riders=riders)
    return (outs[0], landed) if riders else outs[0]


def _mm_tn(name, a, b, colshard):
    s, m = a.shape
    n = b.shape[1]
    tk = min(2 * MM_TILE, s)
    if colshard:
        c = n // N_CHIPS
        tm, tn = _tile(m, MM_TILE // 2), _tile(c, MM_TILE)
        npc = c // tn
        out_shape = (N_CHIPS, m, c)
        out_spec = pl.BlockSpec((None, tm, tn), lambda i, j, kk: (j // npc, i, j % npc))
    else:
        r = m // N_CHIPS
        tm, tn = _tile(r, MM_TILE // 2), _tile(n, MM_TILE)
        mpc = r // tm
        out_shape = (N_CHIPS, r, n)
        out_spec = pl.BlockSpec((None, tm, tn), lambda i, j, kk: (i // mpc, i % mpc, j))
    grid = (m // tm, n // tn, s // tk)
    return _matmul(name, a, b, dims=TN, grid=grid,
                   a_spec=pl.BlockSpec((tk, tm), lambda i, j, kk: (kk, i)),
                   b_spec=pl.BlockSpec((tk, tn), lambda i, j, kk: (kk, j)),
                   out_shapes=[jax.ShapeDtypeStruct(out_shape, F32)], out_specs=[out_spec],
                   acc_shape=(tm, tn))[0][0]


def _row_spec(ts, d):
    return pl.BlockSpec((ts, d), lambda i: (i, 0))


def _vec_spec(d):
    return pl.BlockSpec((1, d), lambda i: (0, 0))


def _norm_mod(name, x, g, sc, sh):
    s, d = x.shape
    ts = min(ROW_TILE, s)

    def body(x_ref, g_ref, sc_ref, sh_ref, h_ref):
        xv = x_ref[...]
        r = lax.rsqrt(jnp.mean(xv * xv, axis=-1, keepdims=True) + EPS)
        h_ref[...] = (xv * r * g_ref[...] * (1.0 + sc_ref[...]) + sh_ref[...]).astype(BF16)

    return pl.pallas_call(
        body, grid=(s // ts,), in_specs=[_row_spec(ts, d), _vec_spec(d), _vec_spec(d), _vec_spec(d)],
        out_specs=_row_spec(ts, d), out_shape=jax.ShapeDtypeStruct((s, d), BF16),
        compiler_params=_params("parallel"), name=name)(x, g, sc, sh)


def _norm_mod_bwd(name, dh, x, g, sc, dres):
    s, d = x.shape
    ts = min(ROW_TILE, s)

    def body(dh_ref, x_ref, g_ref, sc_ref, dres_ref, dx_ref, dsh_ref, dsc_ref, dg_ref):
        @pl.when(pl.program_id(0) == 0)
        def _():
            dsh_ref[...] = jnp.zeros_like(dsh_ref)
            dsc_ref[...] = jnp.zeros_like(dsc_ref)
            dg_ref[...] = jnp.zeros_like(dg_ref)

        xv = x_ref[...]
        r = lax.rsqrt(jnp.mean(xv * xv, axis=-1, keepdims=True) + EPS)
        xhat = xv * r
        dhv = dh_ref[...]
        gv = g_ref[...]
        opsc = 1.0 + sc_ref[...]
        dxhat = dhv * (gv * opsc)
        mdot = jnp.mean(dxhat * xhat, axis=-1, keepdims=True)
        dx_ref[...] = dres_ref[...] + r * (dxhat - xhat * mdot)
        dhx = dhv * xhat
        dsh_ref[...] += jnp.sum(dhv, axis=0, keepdims=True)
        dsc_ref[...] += jnp.sum(dhx * gv, axis=0, keepdims=True)
        dg_ref[...] += jnp.sum(dhx * opsc, axis=0, keepdims=True)

    vec = jax.ShapeDtypeStruct((1, d), F32)
    return pl.pallas_call(
        body, grid=(s // ts,),
        in_specs=[_row_spec(ts, d), _row_spec(ts, d), _vec_spec(d), _vec_spec(d), _row_spec(ts, d)],
        out_specs=[_row_spec(ts, d), _vec_spec(d), _vec_spec(d), _vec_spec(d)],
        out_shape=[jax.ShapeDtypeStruct((s, d), F32), vec, vec, vec],
        compiler_params=_params("arbitrary"), name=name)(dh, x, g, sc, dres)


def _gate_bwd(name, dx, m, gt):
    s, d = dx.shape
    ts = min(ROW_TILE, s)

    def body(dx_ref, m_ref, gt_ref, dm_ref, dgt_ref):
        @pl.when(pl.program_id(0) == 0)
        def _():
            dgt_ref[...] = jnp.zeros_like(dgt_ref)

        dxv = dx_ref[...]
        dm_ref[...] = (dxv * gt_ref[...]).astype(BF16)
        dgt_ref[...] += jnp.sum(dxv * m_ref[...].astype(F32), axis=0, keepdims=True)

    return pl.pallas_call(
        body, grid=(s // ts,), in_specs=[_row_spec(ts, d), _row_spec(ts, d), _vec_spec(d)],
        out_specs=[_row_spec(ts, d), _vec_spec(d)],
        out_shape=[jax.ShapeDtypeStruct((s, d), BF16), jax.ShapeDtypeStruct((1, d), F32)],
        compiler_params=_params("arbitrary"), name=name)(dx, m, gt)


def _loss_grad(name, y, target):
    s, d = y.shape
    ts = min(ROW_TILE, s)

    def body(y_ref, t_ref, dy_ref, sq_ref):
        @pl.when(pl.program_id(0) == 0)
        def _():
            sq_ref[...] = jnp.zeros_like(sq_ref)

        e = y_ref[...] - t_ref[...]
        dy_ref[...] = e * (1.0 / d)
        sq_ref[...] += jnp.sum(e * e, axis=0, keepdims=True)

    return pl.pallas_call(
        body, grid=(s // ts,), in_specs=[_row_spec(ts, d), _row_spec(ts, d)],
        out_specs=[_row_spec(ts, d), _vec_spec(d)],
        out_shape=[jax.ShapeDtypeStruct((s, d), F32), jax.ShapeDtypeStruct((1, d), F32)],
        compiler_params=_params("arbitrary"), name=name)(y, target)


def _adamw(name, w, g, m, v):
    r, c = w.shape
    tr = r
    for cand in (512, 256, 128, 64, 32, 16, 8):
        if r % cand == 0 and cand * c * 4 <= 2 * 1024 * 1024:
            tr = cand
            break

    def body(w_ref, g_ref, m_ref, v_ref, d_ref, mo_ref, vo_ref):
        gv = g_ref[...]
        m2 = ADAM_B1 * m_ref[...] + (1.0 - ADAM_B1) * gv
        v2 = ADAM_B2 * v_ref[...] + (1.0 - ADAM_B2) * (gv * gv)
        m_hat = m2 / (1.0 - ADAM_B1 ** ADAM_STEP)
        v_hat = v2 / (1.0 - ADAM_B2 ** ADAM_STEP)
        d_ref[...] = -ADAM_LR * (m_hat / (jnp.sqrt(v_hat) + ADAM_EPS) + ADAM_WD * w_ref[...])
        mo_ref[...] = m2
        vo_ref[...] = v2

    spec = pl.BlockSpec((tr, c), lambda i: (i, 0))
    shp = jax.ShapeDtypeStruct((r, c), F32)
    return pl.pallas_call(
        body, grid=(r // tr,), in_specs=[spec] * 4, out_specs=[spec] * 3, out_shape=[shp] * 3,
        compiler_params=_params("parallel"), name=name)(w, g, m, v)


def _sum_rows8(name, x):
    n, r, c = x.shape

    def body(x_ref, o_ref):
        acc = x_ref[0]
        for i in range(1, n):
            acc = acc + x_ref[i]
        o_ref[...] = acc

    return pl.pallas_call(
        body, out_shape=jax.ShapeDtypeStruct((r, c), F32),
        in_specs=[pl.BlockSpec(memory_space=pltpu.VMEM)], out_specs=pl.BlockSpec(memory_space=pltpu.VMEM),
        name=name)(x)


def _outer_sum(name, ct, dm):
    d, n_seq = ct.shape
    n = dm.shape[1]
    tr = min(256, d)

    def body(ct_ref, dm_ref, o_ref):
        acc = ct_ref[:, 0:1] * dm_ref[0:1, :]
        for s_i in range(1, n_seq):
            acc = acc + ct_ref[:, s_i:s_i + 1] * dm_ref[s_i:s_i + 1, :]
        o_ref[...] = acc

    return pl.pallas_call(
        body, grid=(d // tr,),
        in_specs=[pl.BlockSpec((tr, n_seq), lambda i: (i, 0)), pl.BlockSpec((n_seq, n), lambda i: (0, 0))],
        out_specs=pl.BlockSpec((tr, n), lambda i: (i, 0)), out_shape=jax.ShapeDtypeStruct((d, n), F32),
        compiler_params=_params("parallel"), name=name)(ct, dm)


def _ada_mod(name, c_act, w_ada_l, bias):
    n_seq, d = c_act.shape
    n = w_ada_l.shape[1]
    tn = _tile(n, 768)

    def epi(acc, b):
        return (acc + b,)

    return _matmul(name, c_act, w_ada_l, dims=NN, grid=(1, n // tn, 1),
                   a_spec=pl.BlockSpec((n_seq, d), lambda i, j, kk: (0, 0)),
                   b_spec=pl.BlockSpec((d, tn), lambda i, j, kk: (0, j)),
                   out_shapes=[jax.ShapeDtypeStruct((n_seq, n), F32)],
                   out_specs=[pl.BlockSpec((n_seq, tn), lambda i, j, kk: (0, j))], acc_shape=(n_seq, tn),
                   epi=epi, extras=(bias,), extra_specs=(pl.BlockSpec((1, tn), lambda i, j, kk: (0, j)),))[0][0]


def _silu_rows(name, c_row):
    d = c_row.shape[1]

    def body(c_ref, o_ref):
        cv = c_ref[...]
        o_ref[...] = jnp.broadcast_to(cv / (1.0 + jnp.exp(-cv)), (8, d))

    return pl.pallas_call(
        body, out_shape=jax.ShapeDtypeStruct((8, d), F32),
        in_specs=[pl.BlockSpec(memory_space=pltpu.VMEM)], out_specs=pl.BlockSpec(memory_space=pltpu.VMEM),
        name=name)(c_row)


def _sb_masks(t):
    row = lax.broadcasted_iota(jnp.int32, (t, t), 0)
    col = lax.broadcasted_iota(jnp.int32, (t, t), 1)
    strict = col < row
    tri = (row > col).astype(BF16)
    return strict, tri


def _col_minus_row(t):
    return lax.broadcasted_iota(jnp.int32, (t, t), 1) - lax.broadcasted_iota(jnp.int32, (t, t), 0)


def _sb_block(qi, i, sb):
    return qi - 2 * i - (1 - sb)


def _sb_keep(diff, qi, kb, t):
    return diff < jnp.where(kb >= 0, (qi - kb) * t, -t)


def _sweep_left(step, qi, carry, runs_of):
    def alive(c):
        runs = runs_of(c)
        top = runs[0]
        for r in runs[1:]:
            top = jnp.maximum(top, r)
        return jnp.max(top)

    n_plain = jnp.maximum(qi - 1, 0) // 2

    def cond(state):
        i, _, top = state
        return jnp.logical_and(i <= n_plain, top > DEAD_LOG)

    def body(state):
        i, c, _ = state
        c = step(i, c, False)
        return i + 1, c, alive(c)

    _, carry, top = lax.while_loop(cond, body, (1, carry, alive(carry)))
    lone = jnp.logical_and(jnp.logical_and(qi >= 2, qi % 2 == 0), top > DEAD_LOG)
    return lax.cond(lone, lambda c: step(qi // 2, c, True), lambda c: c, carry)


def _log_sigmoids(z):
    sp = jnp.log(1.0 + jnp.exp(-jnp.abs(z)))
    return jnp.minimum(z, 0.0) - sp, jnp.minimum(-z, 0.0) - sp, sp


def _sb_fwd(name, proj, n_pairs, gather=()):
    s = proj.shape[0]
    t = min(SB_TILE, s)
    nq = s // t
    n_g = len(gather)

    heads = [slice(h * HEAD_DIM, (h + 1) * HEAD_DIM) for h in range(LANES // HEAD_DIM)]

    assert nq % 2 == 0

    def body(*refs):
        q_ref, k_ref, v_ref = refs[:3]
        o_ref = refs[3 + n_g]
        g_refs, sems = refs[4 + n_g:4 + 2 * n_g], refs[4 + 2 * n_g:]
        _ride_gather((n_pairs, nq), g_refs, sems, lambda: _sb_fwd_body(q_ref, k_ref, v_ref, o_ref))

    def _sb_fwd_body(q_ref, k_ref, v_ref, o_ref):
        qi = pl.program_id(1)
        _, tri = _sb_masks(t)
        diff = _col_minus_row(t)
        qs = [(q_ref[:, sl] * SCALE).astype(BF16) for sl in heads]
        chains = [(sb, h) for sb in (1, 0) for h in range(len(heads))]

        def step(j, carry, masked):
            kbs = [_sb_block(qi, j, sb) for sb, _ in chains]
            ks = [pl.multiple_of(jnp.maximum(kb, 0) * t, t) for kb in kbs]
            k_ = [k_ref[pl.ds(ks[i], t), heads[h]].astype(BF16) for i, (_, h) in enumerate(chains)]
            v_ = [v_ref[pl.ds(ks[i], t), heads[h]].astype(BF16) for i, (_, h) in enumerate(chains)]
            zs = [lax.dot_general(qs[h], k_[i], NT, preferred_element_type=F32) for i, (_, h) in enumerate(chains)]
            lbs, lms, keeps = [], [], []
            for kb, z in zip(kbs, zs):
                lb, lm, _ = _log_sigmoids(z)
                keep = None
                if masked:
                    keep = _sb_keep(diff, qi, kb, t)
                    lm = jnp.where(keep, lm, 0.0)
                lbs.append(lb)
                lms.append(lm)
                keeps.append(keep)
            css = [_split_dot(lm, tri, 2) for lm in lms]
            sums = [jnp.sum(lm, axis=1, keepdims=True) for lm in lms]
            ws = []
            for i, (sb, h) in enumerate(chains):
                run = carry[h][1] if sb == 1 else carry[h][1] + sums[h]
                w = jnp.exp(lbs[i] + css[i] + run)
                if masked:
                    w = jnp.where(keeps[i], w, 0.0)
                ws.append(w.astype(BF16))
            pvs = [jnp.dot(w, v, preferred_element_type=F32) for w, v in zip(ws, v_)]
            nh = len(heads)
            return tuple((carry[h][0] + pvs[h] + pvs[nh + h], carry[h][1] + sums[h] + sums[nh + h]) for h in range(nh))

        init = tuple((jnp.zeros((t, HEAD_DIM), F32), jnp.zeros((t, 1), F32)) for _ in heads)
        carry = step(0, init, True)
        carry = _sweep_left(step, qi, carry, lambda c: [run for _, run in c])
        for (acc, _), sl in zip(carry, heads):
            o_ref[:, sl] = acc

    outs = pl.pallas_call(
        body, grid=(n_pairs, nq),
        in_specs=[pl.BlockSpec((t, LANES), lambda p, i: (i, p)),
                  pl.BlockSpec((s, LANES), lambda p, i: (0, n_pairs + p)),
                  pl.BlockSpec((s, LANES), lambda p, i: (0, 2 * n_pairs + p))] + [HBM_SPEC] * n_g,
        out_specs=[pl.BlockSpec((t, LANES), lambda p, i: (i, p))] + [HBM_SPEC] * n_g,
        out_shape=[jax.ShapeDtypeStruct((s, 2 * n_pairs * LANES), F32)]
        + [jax.ShapeDtypeStruct(g.shape, g.dtype) for g in gather],
        input_output_aliases={3 + i: 1 + i for i in range(n_g)},
        scratch_shapes=_gather_sems(n_g) if n_g else [],
        compiler_params=_params("arbitrary", "arbitrary"), name=name)(proj, proj, proj, *gather)
    return outs[0], list(outs[1:])


def _sb_bwd(name, proj, o_sb, dmixed, n_pairs, exchange=()):
    s = proj.shape[0]
    t = min(SB_TILE, s)
    nq = s // t
    n_x = len(exchange)
    heads = [slice(h * HEAD_DIM, (h + 1) * HEAD_DIM) for h in range(LANES // HEAD_DIM)]

    def body(*refs):
        x_in, x_out, sems = refs[5:5 + n_x], refs[8 + n_x:8 + 2 * n_x], refs[8 + 2 * n_x:]
        if n_x:
            @pl.when(_grid_edge((n_pairs, nq), False))
            def _():
                for cp in _exchange_copies(x_in, x_out, *sems):
                    cp.start()

        _sb_bwd_body(*refs[:5], *refs[5 + n_x:8 + n_x])
        if n_x:
            @pl.when(_grid_edge((n_pairs, nq), True))
            def _():
                for cp in _exchange_copies(x_in, x_out, *sems):
                    cp.wait()

    def _sb_bwd_body(q_ref, k_ref, v_ref, o_ref, do_ref, dq_ref, dk_ref, dv_ref):
        qi = pl.program_id(1)

        @pl.when(qi == 0)
        def _():
            dk_ref[...] = jnp.zeros_like(dk_ref)
            dv_ref[...] = jnp.zeros_like(dv_ref)

        _, tri = _sb_masks(t)
        diff = _col_minus_row(t)
        qs = [(q_ref[:, sl] * SCALE).astype(BF16) for sl in heads]
        dos = [do_ref[:, sl].astype(BF16) for sl in heads]
        dsums = [jnp.sum(do.astype(F32) * o_ref[:, sl], axis=1, keepdims=True) for do, sl in zip(dos, heads)]
        nh = len(heads)
        chains = [(sb, h) for sb in (1, 0) for h in range(nh)]

        def step(j, carry, masked):
            kbs = [_sb_block(qi, j, sb) for sb, _ in chains]
            ks = [pl.multiple_of(jnp.maximum(kb, 0) * t, t) for kb in kbs]
            k_ = [k_ref[pl.ds(ks[i], t), heads[h]].astype(BF16) for i, (_, h) in enumerate(chains)]
            v_ = [v_ref[pl.ds(ks[i], t), heads[h]].astype(BF16) for i, (_, h) in enumerate(chains)]
            zs = [lax.dot_general(qs[h], k_[i], NT, preferred_element_type=F32) for i, (_, h) in enumerate(chains)]
            dws = [lax.dot_general(dos[h], v_[i], NT, preferred_element_type=F32) for i, (_, h) in enumerate(chains)]
            lbs, lms, betas, ombs, keeps = [], [], [], [], []
            for kb, z in zip(kbs, zs):
                ez = jnp.exp(-jnp.abs(z))
                lb = jnp.minimum(z, 0.0) - jnp.log(1.0 + ez)
                lm = lb - z
                betas.append(jnp.exp(lb))
                ombs.append(jnp.exp(lm))
                keep = None
                if masked:
                    keep = _sb_keep(diff, qi, kb, t)
                    lm = jnp.where(keep, lm, 0.0)
                lbs.append(lb)
                lms.append(lm)
                keeps.append(keep)
            css = [_split_dot(lm, tri, 2) for lm in lms]
            sums_f = [jnp.sum(lm, axis=1, keepdims=True) for lm in lms]
            wbs, es = [], []
            for i, (sb, h) in enumerate(chains):
                run_f = carry[h][1] if sb == 1 else carry[h][1] + sums_f[h]
                w = jnp.exp(lbs[i] + css[i] + run_f)
                if masked:
                    w = jnp.where(keeps[i], w, 0.0)
                wb = w.astype(BF16)
                wbs.append(wb)
                es.append(wb.astype(F32) * dws[i])
            for i, (_, h) in enumerate(chains):
                dv_ref[pl.ds(ks[i], t), heads[h]] += lax.dot_general(wbs[i], dos[h], TN, preferred_element_type=F32)
            ecs = [_split_dot(e, tri, 2) for e in es]
            sums_e = [jnp.sum(e, axis=1, keepdims=True) for e in es]
            dzbs = []
            for i, (sb, h) in enumerate(chains):
                run_e = carry[h][2] if sb == 1 else carry[h][2] + sums_e[h]
                left = dsums[h] - (ecs[i] + es[i] + run_e)
                dz = es[i] * ombs[i] - left * betas[i]
                if masked:
                    dz = jnp.where(keeps[i], dz, 0.0)
                dzbs.append(dz.astype(BF16))
            dqs = []
            for i, (_, h) in enumerate(chains):
                dk_ref[pl.ds(ks[i], t), heads[h]] += lax.dot_general(dzbs[i], qs[h], TN, preferred_element_type=F32)
                dqs.append(jnp.dot(dzbs[i], k_[i], preferred_element_type=F32))
            return tuple((carry[h][0] + dqs[h] + dqs[nh + h], carry[h][1] + sums_f[h] + sums_f[nh + h],
                          carry[h][2] + sums_e[h] + sums_e[nh + h]) for h in range(nh))

        zero = jnp.zeros((t, 1), F32)
        init = tuple((jnp.zeros((t, HEAD_DIM), F32), zero, zero) for _ in heads)
        carry = step(0, init, True)
        carry = _sweep_left(step, qi, carry, lambda c: [run_f for _, run_f, _ in c])
        for (dq, _, _), sl in zip(carry, heads):
            dq_ref[:, sl] = dq * SCALE

    blk = pl.BlockSpec((t, LANES), lambda p, i: (i, p))
    full = pl.BlockSpec((s, LANES), lambda p, i: (0, p))
    shp = jax.ShapeDtypeStruct((s, n_pairs * LANES), F32)
    outs = pl.pallas_call(
        body, grid=(n_pairs, nq),
        in_specs=[blk,
                  pl.BlockSpec((s, LANES), lambda p, i: (0, n_pairs + p)),
                  pl.BlockSpec((s, LANES), lambda p, i: (0, 2 * n_pairs + p)),
                  blk, blk] + [HBM_SPEC] * n_x,
        out_specs=[blk, full, full] + [HBM_SPEC] * n_x,
        out_shape=[shp, shp, shp] + [jax.ShapeDtypeStruct((3,) + p.shape[1:], p.dtype) for p in exchange],
        scratch_shapes=_exchange_sems(n_x) if n_x else [],
        compiler_params=_params("arbitrary", "arbitrary"), name=name)(proj, proj, proj, o_sb, dmixed, *exchange)
    return outs[0], outs[1], outs[2], list(outs[3:])


def _group_mean_matrix():
    row = lax.broadcasted_iota(jnp.int32, (LANES, LANES), 0)
    col = lax.broadcasted_iota(jnp.int32, (LANES, LANES), 1)
    same_head = (row < HEAD_DIM) == (col < HEAD_DIM)
    return jnp.where(same_head, 1.0 / HEAD_DIM, 0.0).astype(BF16)


def _qk_norm(name, proj, gq2, gk2, n_pairs):
    s = proj.shape[0]
    ts = min(ROW_TILE, s)
    pb = PAD // ts
    c0 = 3 * n_pairs

    def body(q_ref, k_ref, v_ref, gq_ref, gk_ref, qn_ref, kn_ref, vp_ref):
        r_i = pl.program_id(1)

        @pl.when(r_i < pb)
        def _():
            qn_ref[...] = jnp.zeros_like(qn_ref)
            kn_ref[...] = jnp.zeros_like(kn_ref)
            vp_ref[...] = jnp.zeros_like(vp_ref)

        @pl.when(r_i >= pb)
        def _():
            gm = _group_mean_matrix()
            for src, g_ref, dst in ((q_ref, gq_ref, qn_ref), (k_ref, gk_ref, kn_ref)):
                xv = src[...]
                r = lax.rsqrt(_split_dot(xv * xv, gm, 2) + EPS)
                dst[...] = (xv * r * g_ref[...]).astype(BF16)
            vp_ref[...] = v_ref[...].astype(BF16)

    def in_spec(off):
        return pl.BlockSpec((ts, LANES), lambda p, i: (jnp.maximum(i - pb, 0), c0 + off * n_pairs + p))

    gspec = pl.BlockSpec((1, LANES), lambda p, i: (0, 0))
    ospec = pl.BlockSpec((ts, LANES), lambda p, i: (i, p))
    shp = jax.ShapeDtypeStruct((s + PAD, n_pairs * LANES), BF16)
    return pl.pallas_call(
        body, grid=(n_pairs, (s + PAD) // ts),
        in_specs=[in_spec(0), in_spec(1), in_spec(2), gspec, gspec],
        out_specs=[ospec, ospec, ospec], out_shape=[shp, shp, shp],
        compiler_params=_params("parallel", "arbitrary"), name=name)(proj, proj, proj, gq2, gk2)


def _qk_norm_bwd(name, dqn, dkn, proj, gq2, gk2, n_pairs):
    s = proj.shape[0]
    ts = min(ROW_TILE, s)
    pb = PAD // ts
    c0 = 3 * n_pairs
    n_r = s // ts

    def body(dqn_ref, dkn_ref, q_ref, k_ref, gq_ref, gk_ref, dq_ref, dk_ref, dgq_ref, dgk_ref):
        first = jnp.logical_and(pl.program_id(0) == 0, pl.program_id(1) == 0)
        last = jnp.logical_and(pl.program_id(0) == n_pairs - 1, pl.program_id(1) == n_r - 1)

        @pl.when(first)
        def _():
            dgq_ref[...] = jnp.zeros_like(dgq_ref)
            dgk_ref[...] = jnp.zeros_like(dgk_ref)

        gm = _group_mean_matrix()
        for dn_ref, x_ref, g_ref, dx_ref, dg_ref in ((dqn_ref, q_ref, gq_ref, dq_ref, dgq_ref),
                                                     (dkn_ref, k_ref, gk_ref, dk_ref, dgk_ref)):
            xv = x_ref[...]
            r = lax.rsqrt(_split_dot(xv * xv, gm, 2) + EPS)
            xhat = xv * r
            dn = dn_ref[...]
            tg = dn * g_ref[...]
            md = _split_dot(tg * xhat, gm, 2)
            dx_ref[...] = (r * (tg - xhat * md)).astype(BF16)
            dg_ref[...] += jnp.sum(dn * xhat, axis=0, keepdims=True)

        @pl.when(last)
        def _():
            for dg_ref in (dgq_ref, dgk_ref):
                gv = dg_ref[...]
                dg_ref[...] = gv + pltpu.roll(gv, HEAD_DIM, axis=1)

    def x_spec(off):
        return pl.BlockSpec((ts, LANES), lambda p, i: (i, c0 + off * n_pairs + p))

    dn_q = pl.BlockSpec((ts, LANES), lambda p, i: (i, p))
    dn_k = pl.BlockSpec((ts, LANES), lambda p, i: (i + pb, p))
    gspec = pl.BlockSpec((1, LANES), lambda p, i: (0, 0))
    ospec = pl.BlockSpec((ts, LANES), lambda p, i: (i, p))
    shp = jax.ShapeDtypeStruct((s, n_pairs * LANES), BF16)
    vec = jax.ShapeDtypeStruct((1, LANES), F32)
    return pl.pallas_call(
        body, grid=(n_pairs, n_r),
        in_specs=[dn_q, dn_k, x_spec(0), x_spec(1), gspec, gspec],
        out_specs=[ospec, ospec, gspec, gspec], out_shape=[shp, shp, vec, vec],
        compiler_params=_params("arbitrary", "arbitrary"), name=name)(dqn, dkn, proj, proj, gq2, gk2)


def _head_blocks(x2):
    lane = lax.broadcasted_iota(jnp.int32, x2.shape, 1)
    zero = jnp.zeros_like(x2)
    return jnp.concatenate([jnp.where(lane < HEAD_DIM, x2, zero), jnp.where(lane >= HEAD_DIM, x2, zero)], axis=0)


def _head_diag(r):
    lane = lax.broadcasted_iota(jnp.int32, (CHUNK, LANES), 1)
    return jnp.where(lane < HEAD_DIM, r[:CHUNK], r[CHUNK:])


def _ca_probs(kw, qb, bias_t, n):
    sc = lax.dot_general(kw, qb, NT, preferred_element_type=F32) * SCALE + bias_t
    jpos = lax.broadcasted_iota(jnp.int32, (BAND, LANES), 0)
    sc = jnp.where(n * CHUNK + jpos >= PAD, sc, NEG_INF)
    p = jnp.exp(sc - jnp.max(sc, axis=0, keepdims=True))
    return p * (1.0 / jnp.sum(p, axis=0, keepdims=True))


def _ca_fwd(name, qn, knp, vp, bias_tt, n_pairs, mixed, gather=()):
    sp_rows = knp.shape[0]
    s = sp_rows - PAD
    t = min(CA_TILE, s)
    cpb = t // CHUNK
    pb = PAD // t

    n_g = len(gather)
    grid = (n_pairs, s // t)

    def body(*refs):
        q_ref, k_ref, v_ref, b_ref = refs[:4]
        o_ref = refs[5 + n_g]
        g_refs, sems = refs[6 + n_g:6 + 2 * n_g], refs[6 + 2 * n_g:]

        def compute():
            qi = pl.program_id(1)
            wss = [pl.multiple_of((qi * cpb + ci) * CHUNK, CHUNK) for ci in range(cpb)]
            ps = [_ca_probs(k_ref[pl.ds(wss[ci], BAND), :], _head_blocks(q_ref[ci * CHUNK:(ci + 1) * CHUNK, :]),
                            b_ref[...], qi * cpb + ci) for ci in range(cpb)]
            for ci in range(cpb):
                r = lax.dot_general(ps[ci].astype(BF16), v_ref[pl.ds(wss[ci], BAND), :], TN,
                                    preferred_element_type=F32)
                o_ref[ci * CHUNK:(ci + 1) * CHUNK, :] = _head_diag(r)

        _ride_gather(grid, g_refs, sems, compute)

    full = pl.BlockSpec((sp_rows, LANES), lambda p, i: (0, p))
    outs = pl.pallas_call(
        body, grid=grid,
        in_specs=[pl.BlockSpec((t, LANES), lambda p, i: (i + pb, p)), full, full,
                  pl.BlockSpec((None, BAND, LANES), lambda p, i: (p, 0, 0)), pl.BlockSpec(memory_space=pl.ANY)]
        + [HBM_SPEC] * n_g,
        out_specs=[pl.BlockSpec((t, LANES), lambda p, i: (i, n_pairs + p))] + [HBM_SPEC] * n_g,
        out_shape=[jax.ShapeDtypeStruct(mixed.shape, mixed.dtype)]
        + [jax.ShapeDtypeStruct(g.shape, g.dtype) for g in gather],
        input_output_aliases={4: 0, **{5 + i: 1 + i for i in range(n_g)}},
        scratch_shapes=_gather_sems(n_g) if n_g else [],
        compiler_params=_params("arbitrary", "arbitrary"), name=name)(qn, knp, vp, bias_tt, mixed, *gather)
    return outs[0], list(outs[1:])


def _ca_bwd(name, qn, knp, vp, bias_tt, dmixed, n_pairs, col0, exchange=()):
    sp_rows = knp.shape[0]
    s = sp_rows - PAD
    t = min(CA_TILE, s)
    cpb = t // CHUNK
    pb = PAD // t
    n_x = len(exchange)
    grid = (n_pairs, s // t)

    def body(*refs):
        x_in, x_out, sems = refs[5:5 + n_x], refs[9 + n_x:9 + 2 * n_x], refs[9 + 2 * n_x:]
        if n_x:
            @pl.when(_grid_edge(grid, False))
            def _():
                for cp in _exchange_copies(x_in, x_out, *sems):
                    cp.start()

        _ca_bwd_body(*refs[:5], *refs[5 + n_x:9 + n_x])
        if n_x:
            @pl.when(_grid_edge(grid, True))
            def _():
                for cp in _exchange_copies(x_in, x_out, *sems):
                    cp.wait()

    def _ca_bwd_body(q_ref, k_ref, v_ref, b_ref, do_ref, dq_ref, dk_ref, dv_ref, db_ref):
        qi = pl.program_id(1)

        @pl.when(qi == 0)
        def _():
            dk_ref[...] = jnp.zeros_like(dk_ref)
            dv_ref[...] = jnp.zeros_like(dv_ref)
            db_ref[...] = jnp.zeros_like(db_ref)

        wss = [pl.multiple_of((qi * cpb + ci) * CHUNK, CHUNK) for ci in range(cpb)]
        kws = [k_ref[pl.ds(ws, BAND), :] for ws in wss]
        qbs = [_head_blocks(q_ref[ci * CHUNK:(ci + 1) * CHUNK, :]) for ci in range(cpb)]
        dbs = [_head_blocks(do_ref[ci * CHUNK:(ci + 1) * CHUNK, :].astype(BF16)) for ci in range(cpb)]
        ps = [_ca_probs(kws[ci], qbs[ci], b_ref[...], qi * cpb + ci) for ci in range(cpb)]
        dps = [lax.dot_general(v_ref[pl.ds(wss[ci], BAND), :], dbs[ci], NT, preferred_element_type=F32)
               for ci in range(cpb)]
        dss = [p * (dp - jnp.sum(p * dp, axis=0, keepdims=True)) for p, dp in zip(ps, dps)]
        for ci in range(cpb):
            dsb = dss[ci].astype(BF16)
            dv_ref[pl.ds(wss[ci], BAND), :] += jnp.dot(ps[ci].astype(BF16), dbs[ci], preferred_element_type=F32)
            dk_ref[pl.ds(wss[ci], BAND), :] += jnp.dot(dsb, qbs[ci], preferred_element_type=F32) * SCALE
            r = lax.dot_general(dsb, kws[ci], TN, preferred_element_type=F32)
            dq_ref[ci * CHUNK:(ci + 1) * CHUNK, :] = _head_diag(r) * SCALE
        total = dss[0]
        for ci in range(1, cpb):
            total = total + dss[ci]
        db_ref[...] += total

    full_in = pl.BlockSpec((sp_rows, LANES), lambda p, i: (0, p))
    btile = pl.BlockSpec((None, BAND, LANES), lambda p, i: (p, 0, 0))
    pad_shape = jax.ShapeDtypeStruct((sp_rows, n_pairs * LANES), F32)
    outs = pl.pallas_call(
        body, grid=grid,
        in_specs=[pl.BlockSpec((t, LANES), lambda p, i: (i + pb, p)), full_in, full_in, btile,
                  pl.BlockSpec((t, LANES), lambda p, i: (i, col0 + p))] + [HBM_SPEC] * n_x,
        out_specs=[pl.BlockSpec((t, LANES), lambda p, i: (i, p)), full_in, full_in, btile] + [HBM_SPEC] * n_x,
        out_shape=[jax.ShapeDtypeStruct((s, n_pairs * LANES), F32), pad_shape, pad_shape,
                   jax.ShapeDtypeStruct(bias_tt.shape, F32)]
        + [jax.ShapeDtypeStruct((3,) + p.shape[1:], p.dtype) for p in exchange],
        scratch_shapes=_exchange_sems(n_x) if n_x else [],
        compiler_params=_params("arbitrary", "arbitrary"), name=name)(qn, knp, vp, bias_tt, dmixed, *exchange)
    return outs[0], outs[1], outs[2], outs[3], list(outs[4:])


def _pair_transposed(tile):
    n_h = tile.shape[0]
    return jnp.transpose(tile.reshape(n_h // 2, 2, CHUNK, BAND), (0, 3, 1, 2)).reshape(n_h // 2, BAND, LANES)


def _query_major(tile_t):
    n_p = tile_t.shape[0]
    return jnp.transpose(tile_t.reshape(n_p, BAND, 2, CHUNK), (3, 0, 2, 1)).reshape(CHUNK, 2 * n_p, BAND)


def _bias_tile(rb):
    n_h = rb.shape[0]
    far = jnp.broadcast_to(rb[:, 2 * REL_CLIP:], (n_h, PAD + CHUNK - REL_CLIP))
    near = rb[:, REL_CLIP - (CHUNK - 1):2 * REL_CLIP][:, ::-1]
    ext = jnp.concatenate([far, near], axis=1)
    return jnp.stack([ext[:, CHUNK - 1 - i:CHUNK - 1 - i + BAND] for i in range(CHUNK)], axis=1)


def _rel_bias_grad(name, db_t):
    _, n_h, _ = db_t.shape
    n_out = 3 * LANES
    assert CHUNK == 64 and REL_CLIP == 128 and LEFT_CHUNKS == 8

    def body(z_ref, o_ref):
        a_idx = lax.broadcasted_iota(jnp.int32, (LANES, n_out), 0)
        r_idx = lax.broadcasted_iota(jnp.int32, (LANES, n_out), 1)
        out = jnp.zeros((n_h, n_out), F32)
        for mm, base in ((0, 191), (1, 255), (2, 319)):
            b = LEFT_CHUNKS - mm
            acc = jnp.zeros((n_h, LANES), F32)
            for i in range(CHUNK):
                row = z_ref[i, :, b * CHUNK:(b + 1) * CHUNK]
                rowp = jnp.concatenate([row, jnp.zeros((n_h, LANES - CHUNK), F32)], axis=1)
                acc = acc + pltpu.roll(rowp, CHUNK - 1 - i, axis=1)
            place = (r_idx == jnp.minimum(base - a_idx, 2 * REL_CLIP)).astype(BF16)
            out = out + _split_dot(acc, place, 3)
        far = jnp.zeros((n_h, 1), F32)
        for i in range(CHUNK):
            far = far + jnp.sum(z_ref[i, :, 0:(LEFT_CHUNKS - 2) * CHUNK], axis=1, keepdims=True)
        lane = lax.broadcasted_iota(jnp.int32, (n_h, n_out), 1)
        o_ref[...] = out + jnp.where(lane == 2 * REL_CLIP, far, 0.0)

    return pl.pallas_call(
        body, out_shape=jax.ShapeDtypeStruct((n_h, n_out), F32),
        in_specs=[pl.BlockSpec(memory_space=pltpu.VMEM)], out_specs=pl.BlockSpec(memory_space=pltpu.VMEM),
        name=name)(db_t)


def _mesh_pos():
    return lax.axis_index("x"), lax.axis_index("y"), lax.axis_index("c")


def _other_chips(x, y):
    return [(1 - x, y), (x, 1 - y), (1 - x, 1 - y)]


def _all_gather_small(name, blk):
    m_per, n = blk.shape

    def body(x_ref, out_ref, send_sems, recv_sems, local_sem):
        x, y, c = _mesh_pos()
        me, sibling = (x, y, c), (x, y, 1 - c)
        chips = _other_chips(x, y)

        def rows(px, py, pc):
            return out_ref.at[pl.ds((4 * px + 2 * py + pc) * m_per, m_per), :]

        def copy(k, block, to, src=None):
            return pltpu.make_async_remote_copy(
                src_ref=rows(*block) if src is None else src, dst_ref=rows(*block),
                send_sem=send_sems.at[k], recv_sem=recv_sems.at[k], device_id=to, device_id_type=MESH)

        mine = pltpu.make_async_copy(x_ref, rows(*me), local_sem)
        mine.start()
        first = [copy(0, me, sibling, src=x_ref)]
        first += [copy(1 + j, me, (*chip, c), src=x_ref) for j, chip in enumerate(chips)]
        for cp in first:
            cp.start()
        passed = [copy(4 + j, (*chip, c), sibling) for j, chip in enumerate(chips)]
        for j, chip in enumerate(chips):
            copy(1 + j, (*chip, c), me).wait_recv()
            passed[j].start()
        copy(0, sibling, me).wait_recv()
        for j, chip in enumerate(chips):
            copy(4 + j, (*chip, 1 - c), me).wait_recv()
        for cp in first + passed:
            cp.wait_send()
        mine.wait()

    return pl.pallas_call(
        body, out_shape=jax.ShapeDtypeStruct((N_DEV * m_per, n), blk.dtype),
        in_specs=[pl.BlockSpec(memory_space=pltpu.VMEM)], out_specs=pl.BlockSpec(memory_space=pltpu.VMEM),
        scratch_shapes=[pltpu.SemaphoreType.DMA((7,)), pltpu.SemaphoreType.DMA((7,)), pltpu.SemaphoreType.DMA],
        compiler_params=pltpu.CompilerParams(vmem_limit_bytes=VMEM_LIMIT), name=name)(blk)


HBM_SPEC = pl.BlockSpec(memory_space=pltpu.HBM)


def _gather_copies(bufs, send_sems, recv_sems, phase):
    x, y, c = _mesh_pos()
    my_j = 2 * x + y
    starts, lands = [], []
    for t, buf in enumerate(bufs):
        rh = buf.shape[2] // 2
        for k, (px, py) in enumerate(_other_chips(x, y)):
            their_j = 2 * px + py
            if phase == 0:
                src, to = buf.at[0, my_j, pl.ds(c * rh, rh), :], (px, py, c)
                land = buf.at[0, their_j, pl.ds(c * rh, rh), :]
            else:
                src, to = buf.at[0, their_j, pl.ds(c * rh, rh), :], (x, y, 1 - c)
                land = buf.at[0, their_j, pl.ds((1 - c) * rh, rh), :]
            sems = dict(send_sem=send_sems.at[phase, t, k], recv_sem=recv_sems.at[phase, t, k],
                        device_id=to, device_id_type=MESH)
            starts.append(pltpu.make_async_remote_copy(src_ref=src, dst_ref=src, **sems))
            lands.append(pltpu.make_async_remote_copy(src_ref=src, dst_ref=land, **sems))
    return starts, lands


def _gather_sems(n):
    return [pltpu.SemaphoreType.DMA((2, n, 3)), pltpu.SemaphoreType.DMA((2, n, 3))]


def _all_gather_weights(name, bufs, phases):
    n = len(bufs)

    def body(*refs):
        outs = refs[n:2 * n]
        send_sems, recv_sems = refs[2 * n:]
        for phase in phases:
            starts, lands = _gather_copies(outs, send_sems, recv_sems, phase)
            for cp in starts:
                cp.start()
            for cp, ld in zip(starts, lands):
                cp.wait_send()
                ld.wait_recv()

    return pl.pallas_call(
        body, out_shape=[jax.ShapeDtypeStruct(b.shape, b.dtype) for b in bufs],
        in_specs=[HBM_SPEC] * n, out_specs=[HBM_SPEC] * n, input_output_aliases={t: t for t in range(n)},
        scratch_shapes=_gather_sems(n), name=name)(*bufs)


def _gather_buffer(shard, chip):
    r, c = shard.shape
    return lax.dynamic_update_slice(jnp.zeros((1, N_CHIPS, r, c), BF16), shard.astype(BF16)[None, None],
                                    (0, chip, 0, 0))


def _sibling_copies(srcs, outs, send_sems, recv_sems):
    x, y, c = _mesh_pos()
    cps = []
    for t, (src, out) in enumerate(zip(srcs, outs)):
        rh = src.shape[1] // 2
        cps.append(pltpu.make_async_remote_copy(
            src_ref=src.at[:, pl.ds((1 - c) * rh, rh), :], dst_ref=out,
            send_sem=send_sems.at[t], recv_sem=recv_sems.at[t], device_id=(x, y, 1 - c), device_id_type=MESH))
    return cps


def _exchange_sibling_halves(name, grads):
    n = len(grads)

    def body(*refs):
        cps = _sibling_copies(refs[:n], refs[n:2 * n], *refs[2 * n:])
        for cp in cps:
            cp.start()
        for cp in cps:
            cp.wait()

    out_shape = [jax.ShapeDtypeStruct((g.shape[0], g.shape[1] // 2, g.shape[2]), g.dtype) for g in grads]
    return pl.pallas_call(
        body, out_shape=out_shape, in_specs=[HBM_SPEC] * n, out_specs=[HBM_SPEC] * n,
        scratch_shapes=[pltpu.SemaphoreType.DMA((n,)), pltpu.SemaphoreType.DMA((n,))],
        name=name)(*grads)


def _add_halves(name, g, recv, c_idx):
    nb, r, c = g.shape
    rh = r // 2
    tr = rh
    for cand in (512, 256, 128, 64):
        if rh % cand == 0:
            tr = cand
            break
    g4 = g.reshape(nb, 2, rh, c)

    def body(c_ref, g_ref, r_ref, o_ref):
        o_ref[...] = (g_ref[...] + r_ref[...]).astype(BF16)

    grid_spec = pltpu.PrefetchScalarGridSpec(
        num_scalar_prefetch=1, grid=(nb, rh // tr),
        in_specs=[pl.BlockSpec((None, None, tr, c), lambda j, i, cr: (j, cr[0], i, 0)),
                  pl.BlockSpec((None, tr, c), lambda j, i, cr: (j, i, 0))],
        out_specs=pl.BlockSpec((None, tr, c), lambda j, i, cr: (j, i, 0)))
    return pl.pallas_call(
        body, grid_spec=grid_spec, out_shape=jax.ShapeDtypeStruct((nb, rh, c), BF16),
        compiler_params=_params("parallel", "parallel"), name=name)(c_idx, g4, recv)


def _exchange_copies(srcs, outs, send_sems, recv_sems):
    x, y, c = _mesh_pos()
    return [pltpu.make_async_remote_copy(
        src_ref=srcs[t].at[2 * px + py], dst_ref=outs[t].at[k], send_sem=send_sems.at[t, k],
        recv_sem=recv_sems.at[t, k], device_id=(px, py, c), device_id_type=MESH)
        for t in range(len(srcs)) for k, (px, py) in enumerate(_other_chips(x, y))]


def _exchange_sems(n):
    return [pltpu.SemaphoreType.DMA((n, 3)), pltpu.SemaphoreType.DMA((n, 3))]


def _sum_chips(name, own_parts, recvs, pos_idx):
    n_layers = len(own_parts)
    _, rh, c = recvs[0].shape
    tr = rh
    for cand in (512, 256, 128, 64):
        if rh % cand == 0:
            tr = cand
            break
    nb = rh // tr

    def body(pos_ref, *refs):
        own_refs, r_refs, o_ref = refs[:n_layers], refs[n_layers:2 * n_layers], refs[2 * n_layers]
        for l_i in range(n_layers):
            @pl.when(pl.program_id(0) == l_i)
            def _():
                acc = own_refs[l_i][...].astype(F32)
                for k in range(3):
                    acc = acc + r_refs[l_i][k].astype(F32)
                o_ref[...] = acc

    grid_spec = pltpu.PrefetchScalarGridSpec(
        num_scalar_prefetch=1, grid=(n_layers, nb),
        in_specs=[pl.BlockSpec((None, tr, c), lambda l, i, pr: (pr[0], i, 0)) for _ in range(n_layers)]
        + [pl.BlockSpec((3, tr, c), lambda l, i, pr: (0, i, 0)) for _ in range(n_layers)],
        out_specs=pl.BlockSpec((None, tr, c), lambda l, i, pr: (l, pr[1] * nb + i, 0)))
    return pl.pallas_call(
        body, grid_spec=grid_spec, out_shape=jax.ShapeDtypeStruct((n_layers, 2 * rh, c), F32),
        compiler_params=_params("parallel", "parallel"), name=name)(pos_idx, *own_parts, *recvs)


def _share_with_sibling(name, grads):
    n = len(grads)

    def body(*refs):
        outs = refs[n:2 * n]
        send_sems, recv_sems = refs[2 * n:]
        x, y, c = _mesh_pos()
        cps, landed = [], []
        for t in range(n):
            rh = grads[t].shape[1] // 2
            mine = outs[t].at[:, pl.ds(c * rh, rh), :]
            theirs = outs[t].at[:, pl.ds((1 - c) * rh, rh), :]
            cps.append(pltpu.make_async_remote_copy(
                src_ref=mine, dst_ref=mine, send_sem=send_sems.at[t], recv_sem=recv_sems.at[t],
                device_id=(x, y, 1 - c), device_id_type=MESH))
            landed.append(pltpu.make_async_remote_copy(
                src_ref=mine, dst_ref=theirs, send_sem=send_sems.at[t], recv_sem=recv_sems.at[t],
                device_id=(x, y, 1 - c), device_id_type=MESH))
        for cp in cps:
            cp.start()
        for cp, ld in zip(cps, landed):
            cp.wait_send()
            ld.wait_recv()

    out_shape = [jax.ShapeDtypeStruct(g.shape, g.dtype) for g in grads]
    return pl.pallas_call(
        body, out_shape=out_shape, in_specs=[HBM_SPEC] * n, out_specs=[HBM_SPEC] * n,
        input_output_aliases={t: t for t in range(n)},
        scratch_shapes=[pltpu.SemaphoreType.DMA((n,)), pltpu.SemaphoreType.DMA((n,))],
        name=name)(*grads)


def _row(v):
    return v.reshape(1, -1)


def kernel(x, c, g_norm1, w_in, g_q, g_k, rel_bias, w_o, g_norm2, w1, w2, w_ada, b_ada, loss_target, m_g_norm1, m_w_in, m_g_q, m_g_k, m_rel_bias, m_w_o, m_g_norm2, m_w1, m_w2, m_w_ada, m_b_ada, v_g_norm1, v_w_in, v_g_q, v_g_k, v_rel_bias, v_w_o, v_g_norm2, v_w1, v_w2, v_w_ada, v_b_ada):
    _, s, d = x.shape
    n_layers = g_norm1.shape[0]
    n_pairs = (d // 2) // LANES
    n_ca_heads = rel_bias.shape[1]
    n_rel = rel_bias.shape[2]
    assert n_rel == 2 * REL_CLIP + 1 and g_q.shape[1] == HEAD_DIM and n_ca_heads == 2 * n_pairs
    ada_c = w_ada.shape[2]

    ax, ay, ac = _mesh_pos()
    chip = 2 * ax + ay
    dev = 4 * ax + 2 * ay + ac
    c_idx = jnp.reshape(ac, (1,)).astype(jnp.int32)

    c_act_all = _all_gather_small("ag_c", _silu_rows("silu_c", c)).reshape(N_DEV, 8, d)[:, 0, :]
    mod_parts = []
    for l in range(n_layers):
        bias = lax.dynamic_slice_in_dim(b_ada[l], chip * ada_c, ada_c).reshape(1, ada_c)
        mod_parts.append(_ada_mod(f"ada_mod{l}", c_act_all, w_ada[l], bias))
    mod_all = _all_gather_small("ag_mod", jnp.concatenate(mod_parts, axis=1))
    mod_all = mod_all.reshape(N_CHIPS, 2, N_DEV, n_layers, ada_c)[:, 0]
    mod_all = jnp.transpose(mod_all, (1, 2, 0, 3)).reshape(N_DEV, n_layers, N_CHIPS * ada_c)
    mod = lax.dynamic_index_in_dim(mod_all, dev, axis=0, keepdims=False)

    gathered = [[_gather_buffer(w[l], chip) for w in (w_in, w_o, w1, w2)] for l in range(n_layers)]
    gathered[0][:1] = _all_gather_weights("ag_w_in_0", gathered[0][:1], (0, 1))

    xs = x[0]
    saved = []
    for l in range(n_layers):
        sh1, sc1, gt1, sh2, sc2, gt2 = [_row(mod[l, i * d:(i + 1) * d]) for i in range(6)]
        gq2 = _row(jnp.tile(g_q[l], 2))
        gk2 = _row(jnp.tile(g_k[l], 2))
        bias_t = _pair_transposed(_bias_tile(rel_bias[l]))
        h1 = _norm_mod(f"norm1_{l}", xs, _row(g_norm1[l]), sc1, sh1)
        proj = _mm_nn_w(f"proj_{l}", h1, gathered[l][0], 0, True, [F32])[0]
        o_sb, landed = _sb_fwd(f"sb_fwd_{l}", proj, n_pairs, gather=gathered[l][1:] if l == 0 else ())
        if landed:
            gathered[l][1:] = _all_gather_weights(f"ag_rest_{l}", landed, (1,))
        wg_in, wg_o, wg_1, wg_2 = gathered[l]
        nxt = gathered[l + 1] if l + 1 < n_layers else []
        qn, knp, vp = _qk_norm(f"qk_norm_{l}", proj, gq2, gk2, n_pairs)
        mixed, landed = _ca_fwd(f"ca_fwd_{l}", qn, knp, vp, bias_t, n_pairs, o_sb, gather=nxt[:2])

        def res_epi(acc, res, gt):
            return res + gt * acc, acc

        def res_specs(tm, tn):
            return (pl.BlockSpec((tm, tn), lambda i, j, kk: (i, j)), pl.BlockSpec((1, tn), lambda i, j, kk: (0, j)))

        x1, m1 = _mm_nn_w(f"attn_out_{l}", mixed, wg_o, 0, False, [F32, BF16], epi=res_epi,
                          extras=(xs, gt1), extra_specs_fn=res_specs)
        h2 = _norm_mod(f"norm2_{l}", x1, _row(g_norm2[l]), sc2, sh2)

        def act_epi(acc):
            r = jnp.maximum(acc, 0.0)
            return acc, r * r

        if l + 1 < n_layers:
            (u, a), landed_1 = _mm_nn_w(f"mlp_up_{l}", h2, wg_1, 0, True, [BF16, BF16], epi=act_epi, gather=nxt[2:3])
            (x2, m2), landed_2 = _mm_nn_w(f"mlp_down_{l}", a, wg_2, 0, False, [F32, BF16], epi=res_epi,
                                          extras=(x1, gt2), extra_specs_fn=res_specs, gather=nxt[3:])
            gathered[l + 1] = _all_gather_weights(f"ag_weights_{l + 1}", landed + landed_1 + landed_2, (1,))
        else:
            u, a = _mm_nn_w(f"mlp_up_{l}", h2, wg_1, 0, True, [BF16, BF16], epi=act_epi)
            x2, m2 = _mm_nn_w(f"mlp_down_{l}", a, wg_2, 0, False, [F32, BF16], epi=res_epi,
                              extras=(x1, gt2), extra_specs_fn=res_specs)
        saved.append(dict(x0=xs, h1=h1, proj=proj, o_sb=mixed, qn=qn, knp=knp, vp=vp, bias_t=bias_t, mixed=mixed,
                          m1=m1, x1=x1, h2=h2, u=u, a=a, m2=m2, gq2=gq2, gk2=gk2,
                          sc1=sc1, gt1=gt1, sc2=sc2, gt2=gt2))
        xs = x2

    dx, sq = _loss_grad("loss", xs, loss_target[0])
    loss_part = (0.5 * jnp.sum(sq) / d).reshape(1)

    g_in, g_o, g_1, g_2 = [None] * n_layers, [None] * n_layers, [None] * n_layers, [None] * n_layers
    dmod, dg1, dg2, dgq, dgk, drel = [], [], [], [], [], []
    partial, chip_sum, from_chip = {}, {}, {}

    def reduce_on_chip(tag, keys):
        grads = [partial[k] for k in keys]
        for k, g, r in zip(keys, grads, _exchange_sibling_halves(f"rs_sibling_{tag}", grads)):
            chip_sum[k] = _add_halves(f"rs_add_{k[0]}_{k[1]}", g, r, c_idx)

    for l in reversed(range(n_layers)):
        sv = saved[l]
        wg_in, wg_o, wg_1, wg_2 = gathered[l]
        dm2, dgt2 = _gate_bwd(f"gate2_bwd_{l}", dx, sv["m2"], sv["gt2"])

        def act_bwd_epi(acc, u_t):
            return (acc * (2.0 * jnp.maximum(u_t.astype(F32), 0.0)),)

        def tile_specs(tm, tn):
            return (pl.BlockSpec((tm, tn), lambda i, j, kk: (i, j)),)

        above = sorted(k for k in partial if k not in chip_sum) if l == 0 else []
        du = _mm_nt_w(f"mlp_down_bwd_{l}", dm2, wg_2, 0, False, BF16, epi=act_bwd_epi, extras=(sv["u"],),
                      extra_specs_fn=tile_specs, riders=[_sibling_rider([partial[k] for k in above])] if above else ())
        if above:
            du, (from_sibling,) = du
            for k, r in zip(above, from_sibling):
                chip_sum[k] = _add_halves(f"rs_add_{k[0]}_{k[1]}", partial[k], r, c_idx)
        g_2[l] = _mm_tn(f"w2_grad_{l}", sv["a"], dm2, False)
        dh2 = _mm_nt_w(f"mlp_up_bwd_{l}", du, wg_1, 0, True, F32)
        g_1[l] = _mm_tn(f"w1_grad_{l}", sv["h2"], du, True)
        dx1, dsh2, dsc2, dgn2 = _norm_mod_bwd(f"norm2_bwd_{l}", dh2, sv["x1"], _row(g_norm2[l]), sv["sc2"], dx)
        dm1, dgt1 = _gate_bwd(f"gate1_bwd_{l}", dx1, sv["m1"], sv["gt1"])
        dmixed = _mm_nt_w(f"attn_out_bwd_{l}", dm1, wg_o, 0, False, F32)
        g_o[l] = _mm_tn(f"wo_grad_{l}", sv["mixed"], dm1, False)
        partial.update({(1, l): g_o[l], (2, l): g_1[l], (3, l): g_2[l]})
        ride_sb, ride_ca = [], []
        if l == 0:
            early = sorted(partial)
            reduce_on_chip("early", [k for k in early if k not in chip_sum])
            ride_ca = [k for k in early if k[0] == 0 or k == (3, 0)]
            ride_sb = [k for k in early if k not in ride_ca]
        dq_sb, dk_sb, dv_sb, landed = _sb_bwd(f"sb_bwd_{l}", sv["proj"], sv["o_sb"], dmixed, n_pairs,
                                              exchange=[chip_sum[k] for k in ride_sb])
        from_chip.update(zip(ride_sb, landed))
        dqn, dknp, dvp, dbias_t, landed = _ca_bwd(f"ca_bwd_{l}", sv["qn"], sv["knp"], sv["vp"], sv["bias_t"], dmixed,
                                                  n_pairs, n_pairs, exchange=[chip_sum[k] for k in ride_ca])
        from_chip.update(zip(ride_ca, landed))
        dq_ca, dk_ca, dgq_l, dgk_l = _qk_norm_bwd(f"qk_norm_bwd_{l}", dqn, dknp, sv["proj"], sv["gq2"], sv["gk2"],
                                                  n_pairs)
        drel_l = _rel_bias_grad(f"rel_bias_grad_{l}", _query_major(dbias_t))[:, :n_rel]
        dproj = jnp.concatenate([dq_sb.astype(BF16), dk_sb.astype(BF16), dv_sb.astype(BF16), dq_ca, dk_ca,
                                 dvp[PAD:].astype(BF16)], axis=1)
        g_in[l] = _mm_tn(f"win_grad_{l}", sv["h1"], dproj, True)
        partial[(0, l)] = g_in[l]
        if l == 0:
            reduce_on_chip("late", [(0, l)])
            dh1, (landed,) = _mm_nt_w(f"proj_bwd_{l}", dproj, wg_in, 0, True, F32,
                                      riders=[_exchange_rider([chip_sum[(0, l)]])])
            from_chip[(0, l)] = landed[0]
        else:
            dh1 = _mm_nt_w(f"proj_bwd_{l}", dproj, wg_in, 0, True, F32)
        dx, dsh1, dsc1, dgn1 = _norm_mod_bwd(f"norm1_bwd_{l}", dh1, sv["x0"], _row(g_norm1[l]), sv["sc1"], dx1)
        dmod.insert(0, jnp.concatenate([dsh1, dsc1, dgt1, dsh2, dsc2, dgt2], axis=1)[0])
        dg1.insert(0, dgn1[0])
        dg2.insert(0, dgn2[0])
        dgq.insert(0, dgq_l[0, :HEAD_DIM])
        dgk.insert(0, dgk_l[0, :HEAD_DIM])
        drel.insert(0, drel_l.reshape(-1))
    grad_x = dx[None]

    small = jnp.concatenate([jnp.concatenate(dmod), jnp.concatenate(dg1), jnp.concatenate(dg2),
                             jnp.concatenate(dgq), jnp.concatenate(dgk), jnp.concatenate(drel), loss_part])
    n_small = small.shape[0]
    n_pack = -(-n_small // (8 * LANES)) * (8 * LANES)
    small = jnp.pad(small, (0, n_pack - n_small)).reshape(8, n_pack // 8)
    small_all = _all_gather_small("ag_small", small).reshape(N_DEV, 8, n_pack // 8)
    small_sum = _sum_rows8("sum_small", small_all).reshape(-1)
    dmod_all = small_all.reshape(N_DEV, n_pack)[:, :n_layers * 6 * d].reshape(N_DEV, n_layers, 6 * d)

    off = 0

    def take(n, shape):
        nonlocal off
        out = small_sum[off:off + n].reshape(shape)
        off += n
        return out

    grad_b_ada = take(n_layers * 6 * d, (n_layers, 6 * d))
    grad_g_norm1 = take(n_layers * d, (n_layers, d))
    grad_g_norm2 = take(n_layers * d, (n_layers, d))
    grad_g_q = take(n_layers * HEAD_DIM, (n_layers, HEAD_DIM))
    grad_g_k = take(n_layers * HEAD_DIM, (n_layers, HEAD_DIM))
    grad_rel_bias = take(n_layers * n_ca_heads * n_rel, (n_layers, n_ca_heads, n_rel))
    loss = take(1, ())

    c_act_t = jnp.transpose(c_act_all)
    grad_w_ada = jnp.stack([
        _outer_sum(f"wada_grad_{l}", c_act_t,
                   lax.dynamic_slice_in_dim(dmod_all[:, l, :], chip * ada_c, ada_c, axis=1))
        for l in range(n_layers)])

    assert all(k in from_chip for k in partial)
    pos_idx = jnp.stack([chip, ac]).astype(jnp.int32)
    reduced = [_sum_chips(f"rs_sum_{w_i}", [chip_sum[(w_i, l)] for l in range(n_layers)],
                          [from_chip[(w_i, l)] for l in range(n_layers)], pos_idx) for w_i in range(4)]
    grad_w_in, grad_w_o, grad_w1, grad_w2 = _share_with_sibling("rs_share", reduced)

    def adam_big(name, w, g, m, v):
        shp = w.shape
        outs = _adamw(name, w.reshape(-1, shp[-1]), g.reshape(-1, shp[-1]), m.reshape(-1, shp[-1]),
                      v.reshape(-1, shp[-1]))
        return [o.reshape(shp) for o in outs]

    def pack(arrs):
        flat = jnp.concatenate([a.reshape(-1) for a in arrs])
        n = flat.shape[0]
        n_p = -(-n // (8 * LANES)) * (8 * LANES)
        return jnp.pad(flat, (0, n_p - n), constant_values=1.0).reshape(8, n_p // 8)

    small_w = [g_norm1, g_q, g_k, rel_bias, g_norm2, b_ada]
    small_g = [grad_g_norm1, grad_g_q, grad_g_k, grad_rel_bias, grad_g_norm2, grad_b_ada]
    small_m = [m_g_norm1, m_g_q, m_g_k, m_rel_bias, m_g_norm2, m_b_ada]
    small_v = [v_g_norm1, v_g_q, v_g_k, v_rel_bias, v_g_norm2, v_b_ada]
    packed = _adamw("adamw_small", pack(small_w), pack(small_g), pack(small_m), pack(small_v))

    def unpack(p):
        flat = p.reshape(-1)
        res, o = [], 0
        for a in small_w:
            res.append(flat[o:o + a.size].reshape(a.shape))
            o += a.size
        return res

    sd, sm, sv_ = unpack(packed[0]), unpack(packed[1]), unpack(packed[2])
    big = {
        "w_in": adam_big("adamw_w_in", w_in, grad_w_in, m_w_in, v_w_in),
        "w_o": adam_big("adamw_w_o", w_o, grad_w_o, m_w_o, v_w_o),
        "w1": adam_big("adamw_w1", w1, grad_w1, m_w1, v_w1),
        "w2": adam_big("adamw_w2", w2, grad_w2, m_w2, v_w2),
        "w_ada": adam_big("adamw_w_ada", w_ada, grad_w_ada, m_w_ada, v_w_ada),
    }

    def ordered(kind):
        sm_list = (sd, sm, sv_)[kind]
        return [sm_list[0], big["w_in"][kind], sm_list[1], sm_list[2], sm_list[3], big["w_o"][kind], sm_list[4],
                big["w1"][kind], big["w2"][kind], big["w_ada"][kind], sm_list[5]]

    grads = [grad_g_norm1, grad_w_in, grad_g_q, grad_g_k, grad_rel_bias, grad_w_o, grad_g_norm2, grad_w1, grad_w2,
             grad_w_ada, grad_b_ada]
    return (loss, grad_x, *grads, *ordered(0), *ordered(1), *ordered(2))
```

```python
import functools

import jax
import jax.numpy as jnp
from jax import lax
from jax.experimental import pallas as pl
from jax.experimental.pallas import tpu as pltpu

F32 = jnp.float32
BF16 = jnp.bfloat16
MESH = pl.DeviceIdType.MESH

EPS = 1e-6
NEG_INF = -1e30
HEAD_DIM = 64
CHUNK = 64
LEFT_CHUNKS = 8
PAD = LEFT_CHUNKS * CHUNK
BAND = PAD + CHUNK
REL_CLIP = 128
SCALE = HEAD_DIM ** -0.5
DEAD_LOG = -88.0
LANES = 128
N_CHIPS = 4
N_DEV = 8

ADAM_LR = 0.001
ADAM_B1 = 0.9
ADAM_B2 = 0.999
ADAM_EPS = 1e-08
ADAM_WD = 0.01
ADAM_STEP = 10

VMEM_LIMIT = 48 * 1024 * 1024
ROW_TILE = 512
MM_TILE = 1024
SB_TILE = 256
CA_TILE = 256

NT = (((1,), (1,)), ((), ()))
TN = (((0,), (0,)), ((), ()))
NN = (((1,), (0,)), ((), ()))


def _tile(n, pref):
    best = None
    t = LANES
    while t <= min(n, pref):
        if n % t == 0:
            best = t
        t += LANES
    return best if best is not None else n


def _params(*sem):
    return pltpu.CompilerParams(dimension_semantics=sem, vmem_limit_bytes=VMEM_LIMIT)


def _split_dot(x, t, passes):
    acc = None
    r = x
    for i in range(passes):
        hi = r.astype(BF16)
        d = jnp.dot(hi, t, preferred_element_type=F32)
        acc = d if acc is None else acc + d
        if i + 1 < passes:
            r = r - hi.astype(F32)
    return acc


def _grid_edge(grid, last):
    conds = [pl.program_id(i) == (g - 1 if last else 0) for i, g in enumerate(grid)]
    return functools.reduce(jnp.logical_and, conds)


def _ride_gather(grid, g_refs, sems, compute):
    if g_refs:
        @pl.when(_grid_edge(grid, False))
        def _():
            for cp in _gather_copies(g_refs, *sems, 0)[0]:
                cp.start()

    compute()
    if g_refs:
        @pl.when(_grid_edge(grid, True))
        def _():
            for cp, ld in zip(*_gather_copies(g_refs, *sems, 0)):
                cp.wait_send()
                ld.wait_recv()


def _rider(ins, out_shapes, sems, copies, aliased):
    return dict(ins=list(ins), outs=list(out_shapes), sems=list(sems), copies=copies, aliased=aliased)


def _gather_rider(bufs):
    def copies(_, outs, sems):
        def both(cp, ld):
            def wait():
                cp.wait_send()
                ld.wait_recv()
            return wait
        return [(cp, both(cp, ld)) for cp, ld in zip(*_gather_copies(outs, *sems, 0))]

    return _rider(bufs, [jax.ShapeDtypeStruct(b.shape, b.dtype) for b in bufs], _gather_sems(len(bufs)), copies, True)


def _exchange_rider(parts):
    def copies(ins, outs, sems):
        return [(cp, cp.wait) for cp in _exchange_copies(ins, outs, *sems)]

    return _rider(parts, [jax.ShapeDtypeStruct((3,) + p.shape[1:], p.dtype) for p in parts],
                  _exchange_sems(len(parts)), copies, False)


def _sibling_rider(grads):
    def copies(ins, outs, sems):
        return [(cp, cp.wait) for cp in _sibling_copies(ins, outs, *sems)]

    n = len(grads)
    return _rider(grads, [jax.ShapeDtypeStruct((g.shape[0], g.shape[1] // 2, g.shape[2]), g.dtype) for g in grads],
                  [pltpu.SemaphoreType.DMA((n,)), pltpu.SemaphoreType.DMA((n,))], copies, False)


def _matmul(name, a, b, *, dims, grid, a_spec, b_spec, out_shapes, out_specs, acc_shape,
            epi=None, extras=(), extra_specs=(), riders=()):
    nk = grid[2]
    n_ex = len(extras)
    n_out = len(out_shapes)
    n_in = [len(r["ins"]) for r in riders]
    n_ro = [len(r["outs"]) for r in riders]

    def body(*refs):
        a_ref, b_ref = refs[0], refs[1]
        ex = refs[2:2 + n_ex]
        pos = 2 + n_ex
        r_in = []
        for n in n_in:
            r_in.append(refs[pos:pos + n])
            pos += n
        outs = refs[pos:pos + n_out]
        pos += n_out
        r_out = []
        for n in n_ro:
            r_out.append(refs[pos:pos + n])
            pos += n
        acc = refs[pos]
        pos += 1
        r_sems = []
        for r in riders:
            r_sems.append(refs[pos:pos + len(r["sems"])])
            pos += len(r["sems"])

        def all_copies():
            return [c for r, i, o, sm in zip(riders, r_in, r_out, r_sems) for c in r["copies"](i, o, sm)]

        if riders:
            @pl.when(_grid_edge(grid, False))
            def _():
                for cp, _ in all_copies():
                    cp.start()

        k = pl.program_id(2)

        @pl.when(k == 0)
        def _():
            acc[...] = jnp.zeros_like(acc)

        acc[...] += lax.dot_general(a_ref[...].astype(BF16), b_ref[...].astype(BF16), dims,
                                    preferred_element_type=F32)

        @pl.when(k == nk - 1)
        def _():
            res = epi(acc[...], *[e[...] for e in ex]) if epi is not None else (acc[...],)
            for o, r in zip(outs, res):
                o[...] = r.astype(o.dtype)

        if riders:
            @pl.when(_grid_edge(grid, True))
            def _():
                for _, wait in all_copies():
                    wait()

    r_ins = [x for r in riders for x in r["ins"]]
    r_outs = [x for r in riders for x in r["outs"]]
    aliases, i_pos, o_pos = {}, 2 + n_ex, n_out
    for r in riders:
        if r["aliased"]:
            aliases.update({i_pos + i: o_pos + i for i in range(len(r["ins"]))})
        i_pos += len(r["ins"])
        o_pos += len(r["outs"])
    outs = pl.pallas_call(
        body, grid=grid, in_specs=[a_spec, b_spec, *extra_specs] + [HBM_SPEC] * len(r_ins),
        out_specs=list(out_specs) + [HBM_SPEC] * len(r_outs), out_shape=list(out_shapes) + r_outs,
        input_output_aliases=aliases,
        scratch_shapes=[pltpu.VMEM(acc_shape, F32)] + [sm for r in riders for sm in r["sems"]],
        compiler_params=_params(*(("arbitrary",) * 3 if riders else ("parallel", "parallel", "arbitrary"))), name=name,
    )(a, b, *extras, *r_ins)
    landed, pos = [], n_out
    for n in n_ro:
        landed.append(list(outs[pos:pos + n]))
        pos += n
    return list(outs[:n_out]), landed


def _mm_nn_w(name, a, wg, layer, colshard, out_dtypes, epi=None, extras=(), extra_specs_fn=None, gather=()):
    m = a.shape[0]
    _, _, r, c = wg.shape
    tm = min(MM_TILE, m)
    if colshard:
        k, n = r, N_CHIPS * c
        tk, tn = k, _tile(c, MM_TILE)
        npc = c // tn
        b_spec = pl.BlockSpec((None, None, tk, tn), lambda i, j, kk: (layer, j // npc, kk, j % npc))
    else:
        k, n = N_CHIPS * r, c
        tk, tn = _tile(r, MM_TILE), _tile(c, MM_TILE)
        kpc = r // tk
        b_spec = pl.BlockSpec((None, None, tk, tn), lambda i, j, kk: (layer, kk // kpc, kk % kpc, j))
    assert a.shape[1] == k
    grid = (m // tm, n // tn, k // tk)
    out_spec = pl.BlockSpec((tm, tn), lambda i, j, kk: (i, j))
    ex_specs = extra_specs_fn(tm, tn) if extra_specs_fn is not None else ()
    outs, landed = _matmul(name, a, wg, dims=NN, grid=grid,
                           a_spec=pl.BlockSpec((tm, tk), lambda i, j, kk: (i, kk)), b_spec=b_spec,
                           out_shapes=[jax.ShapeDtypeStruct((m, n), d) for d in out_dtypes],
                           out_specs=[out_spec for _ in out_dtypes], acc_shape=(tm, tn),
                           epi=epi, extras=extras, extra_specs=ex_specs,
                           riders=[_gather_rider(gather)] if gather else ())
    return (outs, landed[0]) if gather else outs


def _mm_nt_w(name, a, wg, layer, colshard, out_dtype, epi=None, extras=(), extra_specs_fn=None, riders=()):
    m = a.shape[0]
    _, _, r, c = wg.shape
    tm = min(MM_TILE, m)
    if colshard:
        kdim, n = N_CHIPS * c, r
        tk, tn = _tile(c, MM_TILE), _tile(r, MM_TILE)
        kpc = c // tk
        b_spec = pl.BlockSpec((None, None, tn, tk), lambda i, j, kk: (layer, kk // kpc, j, kk % kpc))
    else:
        kdim, n = c, N_CHIPS * r
        tk, tn = c, _tile(r, MM_TILE)
        npc = r // tn
        b_spec = pl.BlockSpec((None, None, tn, tk), lambda i, j, kk: (layer, j // npc, j % npc, kk))
    assert a.shape[1] == kdim
    grid = (m // tm, n // tn, kdim // tk)
    ex_specs = extra_specs_fn(tm, tn) if extra_specs_fn is not None else ()
    outs, landed = _matmul(name, a, wg, dims=NT, grid=grid,
                   a_spec=pl.BlockSpec((tm, tk), lambda i, j, kk: (i, kk)), b_spec=b_spec,
                   out_shapes=[jax.ShapeDtypeStruct((m, n), out_dtype)],
                   out_specs=[pl.BlockSpec((tm, tn), lambda i, j, kk: (i, j))], acc_shape=(tm, tn),
                   epi=epi, extras=extras, extra_specs=ex_specs, riders=riders)
    return (outs[0], landed) if riders else outs[0]


def _mm_tn(name, a, b, colshard):
    s, m = a.shape
    n = b.shape[1]
    tk = min(2 * MM_TILE, s)
    if colshard:
        c = n // N_CHIPS
        tm, tn = _tile(m, MM_TILE // 2), _tile(c, MM_TILE)
        npc = c // tn
        out_shape = (N_CHIPS, m, c)
        out_spec = pl.BlockSpec((None, tm, tn), lambda i, j, kk: (j // npc, i, j % npc))
    else:
        r = m // N_CHIPS
        tm, tn = _tile(r, MM_TILE // 2), _tile(n, MM_TILE)
        mpc = r // tm
        out_shape = (N_CHIPS, r, n)
        out_spec = pl.BlockSpec((None, tm, tn), lambda i, j, kk: (i // mpc, i % mpc, j))
    grid = (m // tm, n // tn, s // tk)
    return _matmul(name, a, b, dims=TN, grid=grid,
                   a_spec=pl.BlockSpec((tk, tm), lambda i, j, kk: (kk, i)),
                   b_spec=pl.BlockSpec((tk, tn), lambda i, j, kk: (kk, j)),
                   out_shapes=[jax.ShapeDtypeStruct(out_shape, F32)], out_specs=[out_spec],
                   acc_shape=(tm, tn))[0][0]


def _row_spec(ts, d):
    return pl.BlockSpec((ts, d), lambda i: (i, 0))


def _vec_spec(d):
    return pl.BlockSpec((1, d), lambda i: (0, 0))


def _norm_mod(name, x, g, sc, sh):
    s, d = x.shape
    ts = min(ROW_TILE, s)

    def body(x_ref, g_ref, sc_ref, sh_ref, h_ref):
        xv = x_ref[...]
        r = lax.rsqrt(jnp.mean(xv * xv, axis=-1, keepdims=True) + EPS)
        h_ref[...] = (xv * r * g_ref[...] * (1.0 + sc_ref[...]) + sh_ref[...]).astype(BF16)

    return pl.pallas_call(
        body, grid=(s // ts,), in_specs=[_row_spec(ts, d), _vec_spec(d), _vec_spec(d), _vec_spec(d)],
        out_specs=_row_spec(ts, d), out_shape=jax.ShapeDtypeStruct((s, d), BF16),
        compiler_params=_params("parallel"), name=name)(x, g, sc, sh)


def _norm_mod_bwd(name, dh, x, g, sc, dres):
    s, d = x.shape
    ts = min(ROW_TILE, s)

    def body(dh_ref, x_ref, g_ref, sc_ref, dres_ref, dx_ref, dsh_ref, dsc_ref, dg_ref):
        @pl.when(pl.program_id(0) == 0)
        def _():
            dsh_ref[...] = jnp.zeros_like(dsh_ref)
            dsc_ref[...] = jnp.zeros_like(dsc_ref)
            dg_ref[...] = jnp.zeros_like(dg_ref)

        xv = x_ref[...]
        r = lax.rsqrt(jnp.mean(xv * xv, axis=-1, keepdims=True) + EPS)
        xhat = xv * r
        dhv = dh_ref[...]
        gv = g_ref[...]
        opsc = 1.0 + sc_ref[...]
        dxhat = dhv * (gv * opsc)
        mdot = jnp.mean(dxhat * xhat, axis=-1, keepdims=True)
        dx_ref[...] = dres_ref[...] + r * (dxhat - xhat * mdot)
        dhx = dhv * xhat
        dsh_ref[...] += jnp.sum(dhv, axis=0, keepdims=True)
        dsc_ref[...] += jnp.sum(dhx * gv, axis=0, keepdims=True)
        dg_ref[...] += jnp.sum(dhx * opsc, axis=0, keepdims=True)

    vec = jax.ShapeDtypeStruct((1, d), F32)
    return pl.pallas_call(
        body, grid=(s // ts,),
        in_specs=[_row_spec(ts, d), _row_spec(ts, d), _vec_spec(d), _vec_spec(d), _row_spec(ts, d)],
        out_specs=[_row_spec(ts, d), _vec_spec(d), _vec_spec(d), _vec_spec(d)],
        out_shape=[jax.ShapeDtypeStruct((s, d), F32), vec, vec, vec],
        compiler_params=_params("arbitrary"), name=name)(dh, x, g, sc, dres)


def _gate_bwd(name, dx, m, gt):
    s, d = dx.shape
    ts = min(ROW_TILE, s)

    def body(dx_ref, m_ref, gt_ref, dm_ref, dgt_ref):
        @pl.when(pl.program_id(0) == 0)
        def _():
            dgt_ref[...] = jnp.zeros_like(dgt_ref)

        dxv = dx_ref[...]
        dm_ref[...] = (dxv * gt_ref[...]).astype(BF16)
        dgt_ref[...] += jnp.sum(dxv * m_ref[...].astype(F32), axis=0, keepdims=True)

    return pl.pallas_call(
        body, grid=(s // ts,), in_specs=[_row_spec(ts, d), _row_spec(ts, d), _vec_spec(d)],
        out_specs=[_row_spec(ts, d), _vec_spec(d)],
        out_shape=[jax.ShapeDtypeStruct((s, d), BF16), jax.ShapeDtypeStruct((1, d), F32)],
        compiler_params=_params("arbitrary"), name=name)(dx, m, gt)


def _loss_grad(name, y, target):
    s, d = y.shape
    ts = min(ROW_TILE, s)

    def body(y_ref, t_ref, dy_ref, sq_ref):
        @pl.when(pl.program_id(0) == 0)
        def _():
            sq_ref[...] = jnp.zeros_like(sq_ref)

        e = y_ref[...] - t_ref[...]
        dy_ref[...] = e * (1.0 / d)
        sq_ref[...] += jnp.sum(e * e, axis=0, keepdims=True)

    return pl.pallas_call(
        body, grid=(s // ts,), in_specs=[_row_spec(ts, d), _row_spec(ts, d)],
        out_specs=[_row_spec(ts, d), _vec_spec(d)],
        out_shape=[jax.ShapeDtypeStruct((s, d), F32), jax.ShapeDtypeStruct((1, d), F32)],
        compiler_params=_params("arbitrary"), name=name)(y, target)


def _adamw(name, w, g, m, v):
    r, c = w.shape
    tr = r
    for cand in (512, 256, 128, 64, 32, 16, 8):
        if r % cand == 0 and cand * c * 4 <= 2 * 1024 * 1024:
            tr = cand
            break

    def body(w_ref, g_ref, m_ref, v_ref, d_ref, mo_ref, vo_ref):
        gv = g_ref[...]
        m2 = ADAM_B1 * m_ref[...] + (1.0 - ADAM_B1) * gv
        v2 = ADAM_B2 * v_ref[...] + (1.0 - ADAM_B2) * (gv * gv)
        m_hat = m2 / (1.0 - ADAM_B1 ** ADAM_STEP)
        v_hat = v2 / (1.0 - ADAM_B2 ** ADAM_STEP)
        d_ref[...] = -ADAM_LR * (m_hat / (jnp.sqrt(v_hat) + ADAM_EPS) + ADAM_WD * w_ref[...])
        mo_ref[...] = m2
        vo_ref[...] = v2

    spec = pl.BlockSpec((tr, c), lambda i: (i, 0))
    shp = jax.ShapeDtypeStruct((r, c), F32)
    return pl.pallas_call(
        body, grid=(r // tr,), in_specs=[spec] * 4, out_specs=[spec] * 3, out_shape=[shp] * 3,
        compiler_params=_params("parallel"), name=name)(w, g, m, v)


def _sum_rows8(name, x):
    n, r, c = x.shape

    def body(x_ref, o_ref):
        acc = x_ref[0]
        for i in range(1, n):
            acc = acc + x_ref[i]
        o_ref[...] = acc

    return pl.pallas_call(
        body, out_shape=jax.ShapeDtypeStruct((r, c), F32),
        in_specs=[pl.BlockSpec(memory_space=pltpu.VMEM)], out_specs=pl.BlockSpec(memory_space=pltpu.VMEM),
        name=name)(x)


def _outer_sum(name, ct, dm):
    d, n_seq = ct.shape
    n = dm.shape[1]
    tr = min(256, d)

    def body(ct_ref, dm_ref, o_ref):
        acc = ct_ref[:, 0:1] * dm_ref[0:1, :]
        for s_i in range(1, n_seq):
            acc = acc + ct_ref[:, s_i:s_i + 1] * dm_ref[s_i:s_i + 1, :]
        o_ref[...] = acc

    return pl.pallas_call(
        body, grid=(d // tr,),
        in_specs=[pl.BlockSpec((tr, n_seq), lambda i: (i, 0)), pl.BlockSpec((n_seq, n), lambda i: (0, 0))],
        out_specs=pl.BlockSpec((tr, n), lambda i: (i, 0)), out_shape=jax.ShapeDtypeStruct((d, n), F32),
        compiler_params=_params("parallel"), name=name)(ct, dm)


def _ada_mod(name, c_act, w_ada_l, bias):
    n_seq, d = c_act.shape
    n = w_ada_l.shape[1]
    tn = _tile(n, 768)

    def epi(acc, b):
        return (acc + b,)

    return _matmul(name, c_act, w_ada_l, dims=NN, grid=(1, n // tn, 1),
                   a_spec=pl.BlockSpec((n_seq, d), lambda i, j, kk: (0, 0)),
                   b_spec=pl.BlockSpec((d, tn), lambda i, j, kk: (0, j)),
                   out_shapes=[jax.ShapeDtypeStruct((n_seq, n), F32)],
                   out_specs=[pl.BlockSpec((n_seq, tn), lambda i, j, kk: (0, j))], acc_shape=(n_seq, tn),
                   epi=epi, extras=(bias,), extra_specs=(pl.BlockSpec((1, tn), lambda i, j, kk: (0, j)),))[0][0]


def _silu_rows(name, c_row):
    d = c_row.shape[1]

    def body(c_ref, o_ref):
        cv = c_ref[...]
        o_ref[...] = jnp.broadcast_to(cv / (1.0 + jnp.exp(-cv)), (8, d))

    return pl.pallas_call(
        body, out_shape=jax.ShapeDtypeStruct((8, d), F32),
        in_specs=[pl.BlockSpec(memory_space=pltpu.VMEM)], out_specs=pl.BlockSpec(memory_space=pltpu.VMEM),
        name=name)(c_row)


def _sb_masks(t):
    row = lax.broadcasted_iota(jnp.int32, (t, t), 0)
    col = lax.broadcasted_iota(jnp.int32, (t, t), 1)
    strict = col < row
    tri = (row > col).astype(BF16)
    return strict, tri


def _col_minus_row(t):
    return lax.broadcasted_iota(jnp.int32, (t, t), 1) - lax.broadcasted_iota(jnp.int32, (t, t), 0)


def _sb_block(qi, i, sb):
    return qi - 2 * i - (1 - sb)


def _sb_keep(diff, qi, kb, t):
    return diff < jnp.where(kb >= 0, (qi - kb) * t, -t)


def _sweep_left(step, qi, carry, runs_of):
    def alive(c):
        runs = runs_of(c)
        top = runs[0]
        for r in runs[1:]:
            top = jnp.maximum(top, r)
        return jnp.max(top)

    n_plain = jnp.maximum(qi - 1, 0) // 2

    def cond(state):
        i, _, top = state
        return jnp.logical_and(i <= n_plain, top > DEAD_LOG)

    def body(state):
        i, c, _ = state
        c = step(i, c, False)
        return i + 1, c, alive(c)

    _, carry, top = lax.while_loop(cond, body, (1, carry, alive(carry)))
    lone = jnp.logical_and(jnp.logical_and(qi >= 2, qi % 2 == 0), top > DEAD_LOG)
    return lax.cond(lone, lambda c: step(qi // 2, c, True), lambda c: c, carry)


def _log_sigmoids(z):
    sp = jnp.log(1.0 + jnp.exp(-jnp.abs(z)))
    return jnp.minimum(z, 0.0) - sp, jnp.minimum(-z, 0.0) - sp, sp


def _sb_fwd(name, proj, n_pairs, gather=()):
    s = proj.shape[0]
    t = min(SB_TILE, s)
    nq = s // t
    n_g = len(gather)

    heads = [slice(h * HEAD_DIM, (h + 1) * HEAD_DIM) for h in range(LANES // HEAD_DIM)]

    assert nq % 2 == 0

    def body(*refs):
        q_ref, k_ref, v_ref = refs[:3]
        o_ref = refs[3 + n_g]
        g_refs, sems = refs[4 + n_g:4 + 2 * n_g], refs[4 + 2 * n_g:]
        _ride_gather((n_pairs, nq), g_refs, sems, lambda: _sb_fwd_body(q_ref, k_ref, v_ref, o_ref))

    def _sb_fwd_body(q_ref, k_ref, v_ref, o_ref):
        qi = pl.program_id(1)
        _, tri = _sb_masks(t)
        diff = _col_minus_row(t)
        qs = [(q_ref[:, sl] * SCALE).astype(BF16) for sl in heads]
        chains = [(sb, h) for sb in (1, 0) for h in range(len(heads))]

        def step(j, carry, masked):
            kbs = [_sb_block(qi, j, sb) for sb, _ in chains]
            ks = [pl.multiple_of(jnp.maximum(kb, 0) * t, t) for kb in kbs]
            k_ = [k_ref[pl.ds(ks[i], t), heads[h]].astype(BF16) for i, (_, h) in enumerate(chains)]
            v_ = [v_ref[pl.ds(ks[i], t), heads[h]].astype(BF16) for i, (_, h) in enumerate(chains)]
            zs = [lax.dot_general(qs[h], k_[i], NT, preferred_element_type=F32) for i, (_, h) in enumerate(chains)]
            lbs, lms, keeps = [], [], []
            for kb, z in zip(kbs, zs):
                lb, lm, _ = _log_sigmoids(z)
                keep = None
                if masked:
                    keep = _sb_keep(diff, qi, kb, t)
                    lm = jnp.where(keep, lm, 0.0)
                lbs.append(lb)
                lms.append(lm)
                keeps.append(keep)
            css = [_split_dot(lm, tri, 2) for lm in lms]
            sums = [jnp.sum(lm, axis=1, keepdims=True) for lm in lms]
            ws = []
            for i, (sb, h) in enumerate(chains):
                run = carry[h][1] if sb == 1 else carry[h][1] + sums[h]
                w = jnp.exp(lbs[i] + css[i] + run)
                if masked:
                    w = jnp.where(keeps[i], w, 0.0)
                ws.append(w.astype(BF16))
            pvs = [jnp.dot(w, v, preferred_element_type=F32) for w, v in zip(ws, v_)]
            nh = len(heads)
            return tuple((carry[h][0] + pvs[h] + pvs[nh + h], carry[h][1] + sums[h] + sums[nh + h]) for h in range(nh))

        init = tuple((jnp.zeros((t, HEAD_DIM), F32), jnp.zeros((t, 1), F32)) for _ in heads)
        carry = step(0, init, True)
        carry = _sweep_left(step, qi, carry, lambda c: [run for _, run in c])
        for (acc, _), sl in zip(carry, heads):
            o_ref[:, sl] = acc

    outs = pl.pallas_call(
        body, grid=(n_pairs, nq),
        in_specs=[pl.BlockSpec((t, LANES), lambda p, i: (i, p)),
                  pl.BlockSpec((s, LANES), lambda p, i: (0, n_pairs + p)),
                  pl.BlockSpec((s, LANES), lambda p, i: (0, 2 * n_pairs + p))] + [HBM_SPEC] * n_g,
        out_specs=[pl.BlockSpec((t, LANES), lambda p, i: (i, p))] + [HBM_SPEC] * n_g,
        out_shape=[jax.ShapeDtypeStruct((s, 2 * n_pairs * LANES), F32)]
        + [jax.ShapeDtypeStruct(g.shape, g.dtype) for g in gather],
        input_output_aliases={3 + i: 1 + i for i in range(n_g)},
        scratch_shapes=_gather_sems(n_g) if n_g else [],
        compiler_params=_params("arbitrary", "arbitrary"), name=name)(proj, proj, proj, *gather)
    return outs[0], list(outs[1:])


def _sb_bwd(name, proj, o_sb, dmixed, n_pairs, exchange=()):
    s = proj.shape[0]
    t = min(SB_TILE, s)
    nq = s // t
    n_x = len(exchange)
    heads = [slice(h * HEAD_DIM, (h + 1) * HEAD_DIM) for h in range(LANES // HEAD_DIM)]

    def body(*refs):
        x_in, x_out, sems = refs[5:5 + n_x], refs[8 + n_x:8 + 2 * n_x], refs[8 + 2 * n_x:]
        if n_x:
            @pl.when(_grid_edge((n_pairs, nq), False))
            def _():
                for cp in _exchange_copies(x_in, x_out, *sems):
                    cp.start()

        _sb_bwd_body(*refs[:5], *refs[5 + n_x:8 + n_x])
        if n_x:
            @pl.when(_grid_edge((n_pairs, nq), True))
            def _():
                for cp in _exchange_copies(x_in, x_out, *sems):
                    cp.wait()

    def _sb_bwd_body(q_ref, k_ref, v_ref, o_ref, do_ref, dq_ref, dk_ref, dv_ref):
        qi = pl.program_id(1)

        @pl.when(qi == 0)
        def _():
            dk_ref[...] = jnp.zeros_like(dk_ref)
            dv_ref[...] = jnp.zeros_like(dv_ref)

        _, tri = _sb_masks(t)
        diff = _col_minus_row(t)
        qs = [(q_ref[:, sl] * SCALE).astype(BF16) for sl in heads]
        dos = [do_ref[:, sl].astype(BF16) for sl in heads]
        dsums = [jnp.sum(do.astype(F32) * o_ref[:, sl], axis=1, keepdims=True) for do, sl in zip(dos, heads)]
        nh = len(heads)
        chains = [(sb, h) for sb in (1, 0) for h in range(nh)]

        def step(j, carry, masked):
            kbs = [_sb_block(qi, j, sb) for sb, _ in chains]
            ks = [pl.multiple_of(jnp.maximum(kb, 0) * t, t) for kb in kbs]
            k_ = [k_ref[pl.ds(ks[i], t), heads[h]].astype(BF16) for i, (_, h) in enumerate(chains)]
            v_ = [v_ref[pl.ds(ks[i], t), heads[h]].astype(BF16) for i, (_, h) in enumerate(chains)]
            zs = [lax.dot_general(qs[h], k_[i], NT, preferred_element_type=F32) for i, (_, h) in enumerate(chains)]
            dws = [lax.dot_general(dos[h], v_[i], NT, preferred_element_type=F32) for i, (_, h) in enumerate(chains)]
            lbs, lms, betas, ombs, keeps = [], [], [], [], []
            for kb, z in zip(kbs, zs):
                ez = jnp.exp(-jnp.abs(z))
                lb = jnp.minimum(z, 0.0) - jnp.log(1.0 + ez)
                lm = lb - z
                betas.append(jnp.exp(lb))
                ombs.append(jnp.exp(lm))
                keep = None
                if masked:
                    keep = _sb_keep(diff, qi, kb, t)
                    lm = jnp.where(keep, lm, 0.0)
                lbs.append(lb)
                lms.append(lm)
                keeps.append(keep)
            css = [_split_dot(lm, tri, 2) for lm in lms]
            sums_f = [jnp.sum(lm, axis=1, keepdims=True) for lm in lms]
            wbs, es = [], []
            for i, (sb, h) in enumerate(chains):
                run_f = carry[h][1] if sb == 1 else carry[h][1] + sums_f[h]
                w = jnp.exp(lbs[i] + css[i] + run_f)
                if masked:
                    w = jnp.where(keeps[i], w, 0.0)
                wb = w.astype(BF16)
                wbs.append(wb)
                es.append(wb.astype(F32) * dws[i])
            for i, (_, h) in enumerate(chains):
                dv_ref[pl.ds(ks[i], t), heads[h]] += lax.dot_general(wbs[i], dos[h], TN, preferred_element_type=F32)
            ecs = [_split_dot(e, tri, 2) for e in es]
            sums_e = [jnp.sum(e, axis=1, keepdims=True) for e in es]
            dzbs = []
            for i, (sb, h) in enumerate(chains):
                run_e = carry[h][2] if sb == 1 else carry[h][2] + sums_e[h]
                left = dsums[h] - (ecs[i] + es[i] + run_e)
                dz = es[i] * ombs[i] - left * betas[i]
                if masked:
                    dz = jnp.where(keeps[i], dz, 0.0)
                dzbs.append(dz.astype(BF16))
            dqs = []
            for i, (_, h) in enumerate(chains):
                dk_ref[pl.ds(ks[i], t), heads[h]] += lax.dot_general(dzbs[i], qs[h], TN, preferred_element_type=F32)
                dqs.append(jnp.dot(dzbs[i], k_[i], preferred_element_type=F32))
            return tuple((carry[h][0] + dqs[h] + dqs[nh + h], carry[h][1] + sums_f[h] + sums_f[nh + h],
                          carry[h][2] + sums_e[h] + sums_e[nh + h]) for h in range(nh))

        zero = jnp.zeros((t, 1), F32)
        init = tuple((jnp.zeros((t, HEAD_DIM), F32), zero, zero) for _ in heads)
        carry = step(0, init, True)
        carry = _sweep_left(step, qi, carry, lambda c: [run_f for _, run_f, _ in c])
        for (dq, _, _), sl in zip(carry, heads):
            dq_ref[:, sl] = dq * SCALE

    blk = pl.BlockSpec((t, LANES), lambda p, i: (i, p))
    full = pl.BlockSpec((s, LANES), lambda p, i: (0, p))
    shp = jax.ShapeDtypeStruct((s, n_pairs * LANES), F32)
    outs = pl.pallas_call(
        body, grid=(n_pairs, nq),
        in_specs=[blk,
                  pl.BlockSpec((s, LANES), lambda p, i: (0, n_pairs + p)),
                  pl.BlockSpec((s, LANES), lambda p, i: (0, 2 * n_pairs + p)),
                  blk, blk] + [HBM_SPEC] * n_x,
        out_specs=[blk, full, full] + [HBM_SPEC] * n_x,
        out_shape=[shp, shp, shp] + [jax.ShapeDtypeStruct((3,) + p.shape[1:], p.dtype) for p in exchange],
        scratch_shapes=_exchange_sems(n_x) if n_x else [],
        compiler_params=_params("arbitrary", "arbitrary"), name=name)(proj, proj, proj, o_sb, dmixed, *exchange)
    return outs[0], outs[1], outs[2], list(outs[3:])


def _group_mean_matrix():
    row = lax.broadcasted_iota(jnp.int32, (LANES, LANES), 0)
    col = lax.broadcasted_iota(jnp.int32, (LANES, LANES), 1)
    same_head = (row < HEAD_DIM) == (col < HEAD_DIM)
    return jnp.where(same_head, 1.0 / HEAD_DIM, 0.0).astype(BF16)


def _qk_norm(name, proj, gq2, gk2, n_pairs):
    s = proj.shape[0]
    ts = min(ROW_TILE, s)
    pb = PAD // ts
    c0 = 3 * n_pairs

    def body(q_ref, k_ref, v_ref, gq_ref, gk_ref, qn_ref, kn_ref, vp_ref):
        r_i = pl.program_id(1)

        @pl.when(r_i < pb)
        def _():
            qn_ref[...] = jnp.zeros_like(qn_ref)
            kn_ref[...] = jnp.zeros_like(kn_ref)
            vp_ref[...] = jnp.zeros_like(vp_ref)

        @pl.when(r_i >= pb)
        def _():
            gm = _group_mean_matrix()
            for src, g_ref, dst in ((q_ref, gq_ref, qn_ref), (k_ref, gk_ref, kn_ref)):
                xv = src[...]
                r = lax.rsqrt(_split_dot(xv * xv, gm, 2) + EPS)
                dst[...] = (xv * r * g_ref[...]).astype(BF16)
            vp_ref[...] = v_ref[...].astype(BF16)

    def in_spec(off):
        return pl.BlockSpec((ts, LANES), lambda p, i: (jnp.maximum(i - pb, 0), c0 + off * n_pairs + p))

    gspec = pl.BlockSpec((1, LANES), lambda p, i: (0, 0))
    ospec = pl.BlockSpec((ts, LANES), lambda p, i: (i, p))
    shp = jax.ShapeDtypeStruct((s + PAD, n_pairs * LANES), BF16)
    return pl.pallas_call(
        body, grid=(n_pairs, (s + PAD) // ts),
        in_specs=[in_spec(0), in_spec(1), in_spec(2), gspec, gspec],
        out_specs=[ospec, ospec, ospec], out_shape=[shp, shp, shp],
        compiler_params=_params("parallel", "arbitrary"), name=name)(proj, proj, proj, gq2, gk2)


def _qk_norm_bwd(name, dqn, dkn, proj, gq2, gk2, n_pairs):
    s = proj.shape[0]
    ts = min(ROW_TILE, s)
    pb = PAD // ts
    c0 = 3 * n_pairs
    n_r = s // ts

    def body(dqn_ref, dkn_ref, q_ref, k_ref, gq_ref, gk_ref, dq_ref, dk_ref, dgq_ref, dgk_ref):
        first = jnp.logical_and(pl.program_id(0) == 0, pl.program_id(1) == 0)
        last = jnp.logical_and(pl.program_id(0) == n_pairs - 1, pl.program_id(1) == n_r - 1)

        @pl.when(first)
        def _():
            dgq_ref[...] = jnp.zeros_like(dgq_ref)
            dgk_ref[...] = jnp.zeros_like(dgk_ref)

        gm = _group_mean_matrix()
        for dn_ref, x_ref, g_ref, dx_ref, dg_ref in ((dqn_ref, q_ref, gq_ref, dq_ref, dgq_ref),
                                                     (dkn_ref, k_ref, gk_ref, dk_ref, dgk_ref)):
            xv = x_ref[...]
            r = lax.rsqrt(_split_dot(xv * xv, gm, 2) + EPS)
            xhat = xv * r
            dn = dn_ref[...]
            tg = dn * g_ref[...]
            md = _split_dot(tg * xhat, gm, 2)
            dx_ref[...] = (r * (tg - xhat * md)).astype(BF16)
            dg_ref[...] += jnp.sum(dn * xhat, axis=0, keepdims=True)

        @pl.when(last)
        def _():
            for dg_ref in (dgq_ref, dgk_ref):
                gv = dg_ref[...]
                dg_ref[...] = gv + pltpu.roll(gv, HEAD_DIM, axis=1)

    def x_spec(off):
        return pl.BlockSpec((ts, LANES), lambda p, i: (i, c0 + off * n_pairs + p))

    dn_q = pl.BlockSpec((ts, LANES), lambda p, i: (i, p))
    dn_k = pl.BlockSpec((ts, LANES), lambda p, i: (i + pb, p))
    gspec = pl.BlockSpec((1, LANES), lambda p, i: (0, 0))
    ospec = pl.BlockSpec((ts, LANES), lambda p, i: (i, p))
    shp = jax.ShapeDtypeStruct((s, n_pairs * LANES), BF16)
    vec = jax.ShapeDtypeStruct((1, LANES), F32)
    return pl.pallas_call(
        body, grid=(n_pairs, n_r),
        in_specs=[dn_q, dn_k, x_spec(0), x_spec(1), gspec, gspec],
        out_specs=[ospec, ospec, gspec, gspec], out_shape=[shp, shp, vec, vec],
        compiler_params=_params("arbitrary", "arbitrary"), name=name)(dqn, dkn, proj, proj, gq2, gk2)


def _head_blocks(x2):
    lane = lax.broadcasted_iota(jnp.int32, x2.shape, 1)
    zero = jnp.zeros_like(x2)
    return jnp.concatenate([jnp.where(lane < HEAD_DIM, x2, zero), jnp.where(lane >= HEAD_DIM, x2, zero)], axis=0)


def _head_diag(r):
    lane = lax.broadcasted_iota(jnp.int32, (CHUNK, LANES), 1)
    return jnp.where(lane < HEAD_DIM, r[:CHUNK], r[CHUNK:])


def _ca_probs(kw, qb, bias_t, n):
    sc = lax.dot_general(kw, qb, NT, preferred_element_type=F32) * SCALE + bias_t
    jpos = lax.broadcasted_iota(jnp.int32, (BAND, LANES), 0)
    sc = jnp.where(n * CHUNK + jpos >= PAD, sc, NEG_INF)
    p = jnp.exp(sc - jnp.max(sc, axis=0, keepdims=True))
    return p * (1.0 / jnp.sum(p, axis=0, keepdims=True))


def _ca_fwd(name, qn, knp, vp, bias_tt, n_pairs, mixed, gather=()):
    sp_rows = knp.shape[0]
    s = sp_rows - PAD
    t = min(CA_TILE, s)
    cpb = t // CHUNK
    pb = PAD // t

    n_g = len(gather)
    grid = (n_pairs, s // t)

    def body(*refs):
        q_ref, k_ref, v_ref, b_ref = refs[:4]
        o_ref = refs[5 + n_g]
        g_refs, sems = refs[6 + n_g:6 + 2 * n_g], refs[6 + 2 * n_g:]

        def compute():
            qi = pl.program_id(1)
            wss = [pl.multiple_of((qi * cpb + ci) * CHUNK, CHUNK) for ci in range(cpb)]
            ps = [_ca_probs(k_ref[pl.ds(wss[ci], BAND), :], _head_blocks(q_ref[ci * CHUNK:(ci + 1) * CHUNK, :]),
                            b_ref[...], qi * cpb + ci) for ci in range(cpb)]
            for ci in range(cpb):
                r = lax.dot_general(ps[ci].astype(BF16), v_ref[pl.ds(wss[ci], BAND), :], TN,
                                    preferred_element_type=F32)
                o_ref[ci * CHUNK:(ci + 1) * CHUNK, :] = _head_diag(r)

        _ride_gather(grid, g_refs, sems, compute)

    full = pl.BlockSpec((sp_rows, LANES), lambda p, i: (0, p))
    outs = pl.pallas_call(
        body, grid=grid,
        in_specs=[pl.BlockSpec((t, LANES), lambda p, i: (i + pb, p)), full, full,
                  pl.BlockSpec((None, BAND, LANES), lambda p, i: (p, 0, 0)), pl.BlockSpec(memory_space=pl.ANY)]
        + [HBM_SPEC] * n_g,
        out_specs=[pl.BlockSpec((t, LANES), lambda p, i: (i, n_pairs + p))] + [HBM_SPEC] * n_g,
        out_shape=[jax.ShapeDtypeStruct(mixed.shape, mixed.dtype)]
        + [jax.ShapeDtypeStruct(g.shape, g.dtype) for g in gather],
        input_output_aliases={4: 0, **{5 + i: 1 + i for i in range(n_g)}},
        scratch_shapes=_gather_sems(n_g) if n_g else [],
        compiler_params=_params("arbitrary", "arbitrary"), name=name)(qn, knp, vp, bias_tt, mixed, *gather)
    return outs[0], list(outs[1:])


def _ca_bwd(name, qn, knp, vp, bias_tt, dmixed, n_pairs, col0, exchange=()):
    sp_rows = knp.shape[0]
    s = sp_rows - PAD
    t = min(CA_TILE, s)
    cpb = t // CHUNK
    pb = PAD // t
    n_x = len(exchange)
    grid = (n_pairs, s // t)

    def body(*refs):
        x_in, x_out, sems = refs[5:5 + n_x], refs[9 + n_x:9 + 2 * n_x], refs[9 + 2 * n_x:]
        if n_x:
            @pl.when(_grid_edge(grid, False))
            def _():
                for cp in _exchange_copies(x_in, x_out, *sems):
                    cp.start()

        _ca_bwd_body(*refs[:5], *refs[5 + n_x:9 + n_x])
        if n_x:
            @pl.when(_grid_edge(grid, True))
            def _():
                for cp in _exchange_copies(x_in, x_out, *sems):
                    cp.wait()

    def _ca_bwd_body(q_ref, k_ref, v_ref, b_ref, do_ref, dq_ref, dk_ref, dv_ref, db_ref):
        qi = pl.program_id(1)

        @pl.when(qi == 0)
        def _():
            dk_ref[...] = jnp.zeros_like(dk_ref)
            dv_ref[...] = jnp.zeros_like(dv_ref)
            db_ref[...] = jnp.zeros_like(db_ref)

        wss = [pl.multiple_of((qi * cpb + ci) * CHUNK, CHUNK) for ci in range(cpb)]
        kws = [k_ref[pl.ds(ws, BAND), :] for ws in wss]
        qbs = [_head_blocks(q_ref[ci * CHUNK:(ci + 1) * CHUNK, :]) for ci in range(cpb)]
        dbs = [_head_blocks(do_ref[ci * CHUNK:(ci + 1) * CHUNK, :].astype(BF16)) for ci in range(cpb)]
        ps = [_ca_probs(kws[ci], qbs[ci], b_ref[...], qi * cpb + ci) for ci in range(cpb)]
        dps = [lax.dot_general(v_ref[pl.ds(wss[ci], BAND), :], dbs[ci], NT, preferred_element_type=F32)
               for ci in range(cpb)]
        dss = [p * (dp - jnp.sum(p * dp, axis=0, keepdims=True)) for p, dp in zip(ps, dps)]
        for ci in range(cpb):
            dsb = dss[ci].astype(BF16)
            dv_ref[pl.ds(wss[ci], BAND), :] += jnp.dot(ps[ci].astype(BF16), dbs[ci], preferred_element_type=F32)
            dk_ref[pl.ds(wss[ci], BAND), :] += jnp.dot(dsb, qbs[ci], preferred_element_type=F32) * SCALE
            r = lax.dot_general(dsb, kws[ci], TN, preferred_element_type=F32)
            dq_ref[ci * CHUNK:(ci + 1) * CHUNK, :] = _head_diag(r) * SCALE
        total = dss[0]
        for ci in range(1, cpb):
            total = total + dss[ci]
        db_ref[...] += total

    full_in = pl.BlockSpec((sp_rows, LANES), lambda p, i: (0, p))
    btile = pl.BlockSpec((None, BAND, LANES), lambda p, i: (p, 0, 0))
    pad_shape = jax.ShapeDtypeStruct((sp_rows, n_pairs * LANES), F32)
    outs = pl.pallas_call(
        body, grid=grid,
        in_specs=[pl.BlockSpec((t, LANES), lambda p, i: (i + pb, p)), full_in, full_in, btile,
                  pl.BlockSpec((t, LANES), lambda p, i: (i, col0 + p))] + [HBM_SPEC] * n_x,
        out_specs=[pl.BlockSpec((t, LANES), lambda p, i: (i, p)), full_in, full_in, btile] + [HBM_SPEC] * n_x,
        out_shape=[jax.ShapeDtypeStruct((s, n_pairs * LANES), F32), pad_shape, pad_shape,
                   jax.ShapeDtypeStruct(bias_tt.shape, F32)]
        + [jax.ShapeDtypeStruct((3,) + p.shape[1:], p.dtype) for p in exchange],
        scratch_shapes=_exchange_sems(n_x) if n_x else [],
        compiler_params=_params("arbitrary", "arbitrary"), name=name)(qn, knp, vp, bias_tt, dmixed, *exchange)
    return outs[0], outs[1], outs[2], outs[3], list(outs[4:])


def _pair_transposed(tile):
    n_h = tile.shape[0]
    return jnp.transpose(tile.reshape(n_h // 2, 2, CHUNK, BAND), (0, 3, 1, 2)).reshape(n_h // 2, BAND, LANES)


def _query_major(tile_t):
    n_p = tile_t.shape[0]
    return jnp.transpose(tile_t.reshape(n_p, BAND, 2, CHUNK), (3, 0, 2, 1)).reshape(CHUNK, 2 * n_p, BAND)


def _bias_tile(rb):
    n_h = rb.shape[0]
    far = jnp.broadcast_to(rb[:, 2 * REL_CLIP:], (n_h, PAD + CHUNK - REL_CLIP))
    near = rb[:, REL_CLIP - (CHUNK - 1):2 * REL_CLIP][:, ::-1]
    ext = jnp.concatenate([far, near], axis=1)
    return jnp.stack([ext[:, CHUNK - 1 - i:CHUNK - 1 - i + BAND] for i in range(CHUNK)], axis=1)


def _rel_bias_grad(name, db_t):
    _, n_h, _ = db_t.shape
    n_out = 3 * LANES
    assert CHUNK == 64 and REL_CLIP == 128 and LEFT_CHUNKS == 8

    def body(z_ref, o_ref):
        a_idx = lax.broadcasted_iota(jnp.int32, (LANES, n_out), 0)
        r_idx = lax.broadcasted_iota(jnp.int32, (LANES, n_out), 1)
        out = jnp.zeros((n_h, n_out), F32)
        for mm, base in ((0, 191), (1, 255), (2, 319)):
            b = LEFT_CHUNKS - mm
            acc = jnp.zeros((n_h, LANES), F32)
            for i in range(CHUNK):
                row = z_ref[i, :, b * CHUNK:(b + 1) * CHUNK]
                rowp = jnp.concatenate([row, jnp.zeros((n_h, LANES - CHUNK), F32)], axis=1)
                acc = acc + pltpu.roll(rowp, CHUNK - 1 - i, axis=1)
            place = (r_idx == jnp.minimum(base - a_idx, 2 * REL_CLIP)).astype(BF16)
            out = out + _split_dot(acc, place, 3)
        far = jnp.zeros((n_h, 1), F32)
        for i in range(CHUNK):
            far = far + jnp.sum(z_ref[i, :, 0:(LEFT_CHUNKS - 2) * CHUNK], axis=1, keepdims=True)
        lane = lax.broadcasted_iota(jnp.int32, (n_h, n_out), 1)
        o_ref[...] = out + jnp.where(lane == 2 * REL_CLIP, far, 0.0)

    return pl.pallas_call(
        body, out_shape=jax.ShapeDtypeStruct((n_h, n_out), F32),
        in_specs=[pl.BlockSpec(memory_space=pltpu.VMEM)], out_specs=pl.BlockSpec(memory_space=pltpu.VMEM),
        name=name)(db_t)


def _mesh_pos():
    return lax.axis_index("x"), lax.axis_index("y"), lax.axis_index("c")


def _other_chips(x, y):
    return [(1 - x, y), (x, 1 - y), (1 - x, 1 - y)]


def _all_gather_small(name, blk):
    m_per, n = blk.shape

    def body(x_ref, out_ref, send_sems, recv_sems, local_sem):
        x, y, c = _mesh_pos()
        me, sibling = (x, y, c), (x, y, 1 - c)
        chips = _other_chips(x, y)

        def rows(px, py, pc):
            return out_ref.at[pl.ds((4 * px + 2 * py + pc) * m_per, m_per), :]

        def copy(k, block, to, src=None):
            return pltpu.make_async_remote_copy(
                src_ref=rows(*block) if src is None else src, dst_ref=rows(*block),
                send_sem=send_sems.at[k], recv_sem=recv_sems.at[k], device_id=to, device_id_type=MESH)

        mine = pltpu.make_async_copy(x_ref, rows(*me), local_sem)
        mine.start()
        first = [copy(0, me, sibling, src=x_ref)]
        first += [copy(1 + j, me, (*chip, c), src=x_ref) for j, chip in enumerate(chips)]
        for cp in first:
            cp.start()
        passed = [copy(4 + j, (*chip, c), sibling) for j, chip in enumerate(chips)]
        for j, chip in enumerate(chips):
            copy(1 + j, (*chip, c), me).wait_recv()
            passed[j].start()
        copy(0, sibling, me).wait_recv()
        for j, chip in enumerate(chips):
            copy(4 + j, (*chip, 1 - c), me).wait_recv()
        for cp in first + passed:
            cp.wait_send()
        mine.wait()

    return pl.pallas_call(
        body, out_shape=jax.ShapeDtypeStruct((N_DEV * m_per, n), blk.dtype),
        in_specs=[pl.BlockSpec(memory_space=pltpu.VMEM)], out_specs=pl.BlockSpec(memory_space=pltpu.VMEM),
        scratch_shapes=[pltpu.SemaphoreType.DMA((7,)), pltpu.SemaphoreType.DMA((7,)), pltpu.SemaphoreType.DMA],
        compiler_params=pltpu.CompilerParams(vmem_limit_bytes=VMEM_LIMIT), name=name)(blk)


HBM_SPEC = pl.BlockSpec(memory_space=pltpu.HBM)


def _gather_copies(bufs, send_sems, recv_sems, phase):
    x, y, c = _mesh_pos()
    my_j = 2 * x + y
    starts, lands = [], []
    for t, buf in enumerate(bufs):
        rh = buf.shape[2] // 2
        for k, (px, py) in enumerate(_other_chips(x, y)):
            their_j = 2 * px + py
            if phase == 0:
                src, to = buf.at[0, my_j, pl.ds(c * rh, rh), :], (px, py, c)
                land = buf.at[0, their_j, pl.ds(c * rh, rh), :]
            else:
                src, to = buf.at[0, their_j, pl.ds(c * rh, rh), :], (x, y, 1 - c)
                land = buf.at[0, their_j, pl.ds((1 - c) * rh, rh), :]
            sems = dict(send_sem=send_sems.at[phase, t, k], recv_sem=recv_sems.at[phase, t, k],
                        device_id=to, device_id_type=MESH)
            starts.append(pltpu.make_async_remote_copy(src_ref=src, dst_ref=src, **sems))
            lands.append(pltpu.make_async_remote_copy(src_ref=src, dst_ref=land, **sems))
    return starts, lands


def _gather_sems(n):
    return [pltpu.SemaphoreType.DMA((2, n, 3)), pltpu.SemaphoreType.DMA((2, n, 3))]


def _all_gather_weights(name, bufs, phases):
    n = len(bufs)

    def body(*refs):
        outs = refs[n:2 * n]
        send_sems, recv_sems = refs[2 * n:]
        for phase in phases:
            starts, lands = _gather_copies(outs, send_sems, recv_sems, phase)
            for cp in starts:
                cp.start()
            for cp, ld in zip(starts, lands):
                cp.wait_send()
                ld.wait_recv()

    return pl.pallas_call(
        body, out_shape=[jax.ShapeDtypeStruct(b.shape, b.dtype) for b in bufs],
        in_specs=[HBM_SPEC] * n, out_specs=[HBM_SPEC] * n, input_output_aliases={t: t for t in range(n)},
        scratch_shapes=_gather_sems(n), name=name)(*bufs)


def _gather_buffer(shard, chip):
    r, c = shard.shape
    return lax.dynamic_update_slice(jnp.zeros((1, N_CHIPS, r, c), BF16), shard.astype(BF16)[None, None],
                                    (0, chip, 0, 0))


def _sibling_copies(srcs, outs, send_sems, recv_sems):
    x, y, c = _mesh_pos()
    cps = []
    for t, (src, out) in enumerate(zip(srcs, outs)):
        rh = src.shape[1] // 2
        cps.append(pltpu.make_async_remote_copy(
            src_ref=src.at[:, pl.ds((1 - c) * rh, rh), :], dst_ref=out,
            send_sem=send_sems.at[t], recv_sem=recv_sems.at[t], device_id=(x, y, 1 - c), device_id_type=MESH))
    return cps


def _exchange_sibling_halves(name, grads):
    n = len(grads)

    def body(*refs):
        cps = _sibling_copies(refs[:n], refs[n:2 * n], *refs[2 * n:])
        for cp in cps:
            cp.start()
        for cp in cps:
            cp.wait()

    out_shape = [jax.ShapeDtypeStruct((g.shape[0], g.shape[1] // 2, g.shape[2]), g.dtype) for g in grads]
    return pl.pallas_call(
        body, out_shape=out_shape, in_specs=[HBM_SPEC] * n, out_specs=[HBM_SPEC] * n,
        scratch_shapes=[pltpu.SemaphoreType.DMA((n,)), pltpu.SemaphoreType.DMA((n,))],
        name=name)(*grads)


def _add_halves(name, g, recv, c_idx):
    nb, r, c = g.shape
    rh = r // 2
    tr = rh
    for cand in (512, 256, 128, 64):
        if rh % cand == 0:
            tr = cand
            break
    g4 = g.reshape(nb, 2, rh, c)

    def body(c_ref, g_ref, r_ref, o_ref):
        o_ref[...] = (g_ref[...] + r_ref[...]).astype(BF16)

    grid_spec = pltpu.PrefetchScalarGridSpec(
        num_scalar_prefetch=1, grid=(nb, rh // tr),
        in_specs=[pl.BlockSpec((None, None, tr, c), lambda j, i, cr: (j, cr[0], i, 0)),
                  pl.BlockSpec((None, tr, c), lambda j, i, cr: (j, i, 0))],
        out_specs=pl.BlockSpec((None, tr, c), lambda j, i, cr: (j, i, 0)))
    return pl.pallas_call(
        body, grid_spec=grid_spec, out_shape=jax.ShapeDtypeStruct((nb, rh, c), BF16),
        compiler_params=_params("parallel", "parallel"), name=name)(c_idx, g4, recv)


def _exchange_copies(srcs, outs, send_sems, recv_sems):
    x, y, c = _mesh_pos()
    return [pltpu.make_async_remote_copy(
        src_ref=srcs[t].at[2 * px + py], dst_ref=outs[t].at[k], send_sem=send_sems.at[t, k],
        recv_sem=recv_sems.at[t, k], device_id=(px, py, c), device_id_type=MESH)
        for t in range(len(srcs)) for k, (px, py) in enumerate(_other_chips(x, y))]


def _exchange_sems(n):
    return [pltpu.SemaphoreType.DMA((n, 3)), pltpu.SemaphoreType.DMA((n, 3))]


def _sum_chips(name, own_parts, recvs, pos_idx):
    n_layers = len(own_parts)
    _, rh, c = recvs[0].shape
    tr = rh
    for cand in (512, 256, 128, 64):
        if rh % cand == 0:
            tr = cand
            break
    nb = rh // tr

    def body(pos_ref, *refs):
        own_refs, r_refs, o_ref = refs[:n_layers], refs[n_layers:2 * n_layers], refs[2 * n_layers]
        for l_i in range(n_layers):
            @pl.when(pl.program_id(0) == l_i)
            def _():
                acc = own_refs[l_i][...].astype(F32)
                for k in range(3):
                    acc = acc + r_refs[l_i][k].astype(F32)
                o_ref[...] = acc

    grid_spec = pltpu.PrefetchScalarGridSpec(
        num_scalar_prefetch=1, grid=(n_layers, nb),
        in_specs=[pl.BlockSpec((None, tr, c), lambda l, i, pr: (pr[0], i, 0)) for _ in range(n_layers)]
        + [pl.BlockSpec((3, tr, c), lambda l, i, pr: (0, i, 0)) for _ in range(n_layers)],
        out_specs=pl.BlockSpec((None, tr, c), lambda l, i, pr: (l, pr[1] * nb + i, 0)))
    return pl.pallas_call(
        body, grid_spec=grid_spec, out_shape=jax.ShapeDtypeStruct((n_layers, 2 * rh, c), F32),
        compiler_params=_params("parallel", "parallel"), name=name)(pos_idx, *own_parts, *recvs)


def _share_with_sibling(name, grads):
    n = len(grads)

    def body(*refs):
        outs = refs[n:2 * n]
        send_sems, recv_sems = refs[2 * n:]
        x, y, c = _mesh_pos()
        cps, landed = [], []
        for t in range(n):
            rh = grads[t].shape[1] // 2
            mine = outs[t].at[:, pl.ds(c * rh, rh), :]
            theirs = outs[t].at[:, pl.ds((1 - c) * rh, rh), :]
            cps.append(pltpu.make_async_remote_copy(
                src_ref=mine, dst_ref=mine, send_sem=send_sems.at[t], recv_sem=recv_sems.at[t],
                device_id=(x, y, 1 - c), device_id_type=MESH))
            landed.append(pltpu.make_async_remote_copy(
                src_ref=mine, dst_ref=theirs, send_sem=send_sems.at[t], recv_sem=recv_sems.at[t],
                device_id=(x, y, 1 - c), device_id_type=MESH))
        for cp in cps:
            cp.start()
        for cp, ld in zip(cps, landed):
            cp.wait_send()
            ld.wait_recv()

    out_shape = [jax.ShapeDtypeStruct(g.shape, g.dtype) for g in grads]
    return pl.pallas_call(
        body, out_shape=out_shape, in_specs=[HBM_SPEC] * n, out_specs=[HBM_SPEC] * n,
        input_output_aliases={t: t for t in range(n)},
        scratch_shapes=[pltpu.SemaphoreType.DMA((n,)), pltpu.SemaphoreType.DMA((n,))],
        name=name)(*grads)


def _row(v):
    return v.reshape(1, -1)


def kernel(x, c, g_norm1, w_in, g_q, g_k, rel_bias, w_o, g_norm2, w1, w2, w_ada, b_ada, loss_target, m_g_norm1, m_w_in, m_g_q, m_g_k, m_rel_bias, m_w_o, m_g_norm2, m_w1, m_w2, m_w_ada, m_b_ada, v_g_norm1, v_w_in, v_g_q, v_g_k, v_rel_bias, v_w_o, v_g_norm2, v_w1, v_w2, v_w_ada, v_b_ada):
    _, s, d = x.shape
    n_layers = g_norm1.shape[0]
    n_pairs = (d // 2) // LANES
    n_ca_heads = rel_bias.shape[1]
    n_rel = rel_bias.shape[2]
    assert n_rel == 2 * REL_CLIP + 1 and g_q.shape[1] == HEAD_DIM and n_ca_heads == 2 * n_pairs
    ada_c = w_ada.shape[2]

    ax, ay, ac = _mesh_pos()
    chip = 2 * ax + ay
    dev = 4 * ax + 2 * ay + ac
    c_idx = jnp.reshape(ac, (1,)).astype(jnp.int32)

    c_act_all = _all_gather_small("ag_c", _silu_rows("silu_c", c)).reshape(N_DEV, 8, d)[:, 0, :]
    mod_parts = []
    for l in range(n_layers):
        bias = lax.dynamic_slice_in_dim(b_ada[l], chip * ada_c, ada_c).reshape(1, ada_c)
        mod_parts.append(_ada_mod(f"ada_mod{l}", c_act_all, w_ada[l], bias))
    mod_all = _all_gather_small("ag_mod", jnp.concatenate(mod_parts, axis=1))
    mod_all = mod_all.reshape(N_CHIPS, 2, N_DEV, n_layers, ada_c)[:, 0]
    mod_all = jnp.transpose(mod_all, (1, 2, 0, 3)).reshape(N_DEV, n_layers, N_CHIPS * ada_c)
    mod = lax.dynamic_index_in_dim(mod_all, dev, axis=0, keepdims=False)

    gathered = [[_gather_buffer(w[l], chip) for w in (w_in, w_o, w1, w2)] for l in range(n_layers)]
    gathered[0][:1] = _all_gather_weights("ag_w_in_0", gathered[0][:1], (0, 1))

    xs = x[0]
    saved = []
    for l in range(n_layers):
        sh1, sc1, gt1, sh2, sc2, gt2 = [_row(mod[l, i * d:(i + 1) * d]) for i in range(6)]
        gq2 = _row(jnp.tile(g_q[l], 2))
        gk2 = _row(jnp.tile(g_k[l], 2))
        bias_t = _pair_transposed(_bias_tile(rel_bias[l]))
        h1 = _norm_mod(f"norm1_{l}", xs, _row(g_norm1[l]), sc1, sh1)
        proj = _mm_nn_w(f"proj_{l}", h1, gathered[l][0], 0, True, [F32])[0]
        o_sb, landed = _sb_fwd(f"sb_fwd_{l}", proj, n_pairs, gather=gathered[l][1:] if l == 0 else ())
        if landed:
            gathered[l][1:] = _all_gather_weights(f"ag_rest_{l}", landed, (1,))
        wg_in, wg_o, wg_1, wg_2 = gathered[l]
        nxt = gathered[l + 1] if l + 1 < n_layers else []
        qn, knp, vp = _qk_norm(f"qk_norm_{l}", proj, gq2, gk2, n_pairs)
        mixed, landed = _ca_fwd(f"ca_fwd_{l}", qn, knp, vp, bias_t, n_pairs, o_sb, gather=nxt[:2])

        def res_epi(acc, res, gt):
            return res + gt * acc, acc

        def res_specs(tm, tn):
            return (pl.BlockSpec((tm, tn), lambda i, j, kk: (i, j)), pl.BlockSpec((1, tn), lambda i, j, kk: (0, j)))

        x1, m1 = _mm_nn_w(f"attn_out_{l}", mixed, wg_o, 0, False, [F32, BF16], epi=res_epi,
                          extras=(xs, gt1), extra_specs_fn=res_specs)
        h2 = _norm_mod(f"norm2_{l}", x1, _row(g_norm2[l]), sc2, sh2)

        def act_epi(acc):
            r = jnp.maximum(acc, 0.0)
            return acc, r * r

        if l + 1 < n_layers:
            (u, a), landed_1 = _mm_nn_w(f"mlp_up_{l}", h2, wg_1, 0, True, [BF16, BF16], epi=act_epi, gather=nxt[2:3])
            (x2, m2), landed_2 = _mm_nn_w(f"mlp_down_{l}", a, wg_2, 0, False, [F32, BF16], epi=res_epi,
                                          extras=(x1, gt2), extra_specs_fn=res_specs, gather=nxt[3:])
            gathered[l + 1] = _all_gather_weights(f"ag_weights_{l + 1}", landed + landed_1 + landed_2, (1,))
        else:
            u, a = _mm_nn_w(f"mlp_up_{l}", h2, wg_1, 0, True, [BF16, BF16], epi=act_epi)
            x2, m2 = _mm_nn_w(f"mlp_down_{l}", a, wg_2, 0, False, [F32, BF16], epi=res_epi,
                              extras=(x1, gt2), extra_specs_fn=res_specs)
        saved.append(dict(x0=xs, h1=h1, proj=proj, o_sb=mixed, qn=qn, knp=knp, vp=vp, bias_t=bias_t, mixed=mixed,
                          m1=m1, x1=x1, h2=h2, u=u, a=a, m2=m2, gq2=gq2, gk2=gk2,
                          sc1=sc1, gt1=gt1, sc2=sc2, gt2=gt2))
        xs = x2

    dx, sq = _loss_grad("loss", xs, loss_target[0])
    loss_part = (0.5 * jnp.sum(sq) / d).reshape(1)

    g_in, g_o, g_1, g_2 = [None] * n_layers, [None] * n_layers, [None] * n_layers, [None] * n_layers
    dmod, dg1, dg2, dgq, dgk, drel = [], [], [], [], [], []
    partial, chip_sum, from_chip = {}, {}, {}

    def reduce_on_chip(tag, keys):
        grads = [partial[k] for k in keys]
        for k, g, r in zip(keys, grads, _exchange_sibling_halves(f"rs_sibling_{tag}", grads)):
            chip_sum[k] = _add_halves(f"rs_add_{k[0]}_{k[1]}", g, r, c_idx)

    for l in reversed(range(n_layers)):
        sv = saved[l]
        wg_in, wg_o, wg_1, wg_2 = gathered[l]
        dm2, dgt2 = _gate_bwd(f"gate2_bwd_{l}", dx, sv["m2"], sv["gt2"])

        def act_bwd_epi(acc, u_t):
            return (acc * (2.0 * jnp.maximum(u_t.astype(F32), 0.0)),)

        def tile_specs(tm, tn):
            return (pl.BlockSpec((tm, tn), lambda i, j, kk: (i, j)),)

        def nt_with_siblings(keys, name, *args, **kwargs):
            if not keys:
                return _mm_nt_w(name, *args, **kwargs)
            out, (from_sibling,) = _mm_nt_w(name, *args, riders=[_sibling_rider([partial[k] for k in keys])], **kwargs)
            for k, r in zip(keys, from_sibling):
                chip_sum[k] = _add_halves(f"rs_add_{k[0]}_{k[1]}", partial[k], r, c_idx)
            return out

        above = sorted(partial) if l == 0 else []
        du = nt_with_siblings([k for k in above if k[0] in (1, 3)], f"mlp_down_bwd_{l}", dm2, wg_2, 0, False, BF16,
                              epi=act_bwd_epi, extras=(sv["u"],), extra_specs_fn=tile_specs)
        g_2[l] = _mm_tn(f"w2_grad_{l}", sv["a"], dm2, False)
        dh2 = nt_with_siblings([k for k in above if k[0] in (0, 2)], f"mlp_up_bwd_{l}", du, wg_1, 0, True, F32)
        g_1[l] = _mm_tn(f"w1_grad_{l}", sv["h2"], du, True)
        partial.update({(2, l): g_1[l], (3, l): g_2[l]})
        dx1, dsh2, dsc2, dgn2 = _norm_mod_bwd(f"norm2_bwd_{l}", dh2, sv["x1"], _row(g_norm2[l]), sv["sc2"], dx)
        dm1, dgt1 = _gate_bwd(f"gate1_bwd_{l}", dx1, sv["m1"], sv["gt1"])
        dmixed = nt_with_siblings([(2, l), (3, l)] if l == 0 else [], f"attn_out_bwd_{l}", dm1, wg_o, 0, False, F32)
        g_o[l] = _mm_tn(f"wo_grad_{l}", sv["mixed"], dm1, False)
        partial[(1, l)] = g_o[l]
        ride_sb, ride_ca = [], []
        if l == 0:
            early = sorted(partial)
            reduce_on_chip("early", [k for k in early if k not in chip_sum])
            ride_ca = [k for k in early if k[0] == 0 or k == (3, 0)]
            ride_sb = [k for k in early if k not in ride_ca]
        dq_sb, dk_sb, dv_sb, landed = _sb_bwd(f"sb_bwd_{l}", sv["proj"], sv["o_sb"], dmixed, n_pairs,
                                              exchange=[chip_sum[k] for k in ride_sb])
        from_chip.update(zip(ride_sb, landed))
        dqn, dknp, dvp, dbias_t, landed = _ca_bwd(f"ca_bwd_{l}", sv["qn"], sv["knp"], sv["vp"], sv["bias_t"], dmixed,
                                                  n_pairs, n_pairs, exchange=[chip_sum[k] for k in ride_ca])
        from_chip.update(zip(ride_ca, landed))
        dq_ca, dk_ca, dgq_l, dgk_l = _qk_norm_bwd(f"qk_norm_bwd_{l}", dqn, dknp, sv["proj"], sv["gq2"], sv["gk2"],
                                                  n_pairs)
        drel_l = _rel_bias_grad(f"rel_bias_grad_{l}", _query_major(dbias_t))[:, :n_rel]
        dproj = jnp.concatenate([dq_sb.astype(BF16), dk_sb.astype(BF16), dv_sb.astype(BF16), dq_ca, dk_ca,
                                 dvp[PAD:].astype(BF16)], axis=1)
        g_in[l] = _mm_tn(f"win_grad_{l}", sv["h1"], dproj, True)
        partial[(0, l)] = g_in[l]
        if l == 0:
            reduce_on_chip("late", [(0, l)])
            dh1, (landed,) = _mm_nt_w(f"proj_bwd_{l}", dproj, wg_in, 0, True, F32,
                                      riders=[_exchange_rider([chip_sum[(0, l)]])])
            from_chip[(0, l)] = landed[0]
        else:
            dh1 = _mm_nt_w(f"proj_bwd_{l}", dproj, wg_in, 0, True, F32)
        dx, dsh1, dsc1, dgn1 = _norm_mod_bwd(f"norm1_bwd_{l}", dh1, sv["x0"], _row(g_norm1[l]), sv["sc1"], dx1)
        dmod.insert(0, jnp.concatenate([dsh1, dsc1, dgt1, dsh2, dsc2, dgt2], axis=1)[0])
        dg1.insert(0, dgn1[0])
        dg2.insert(0, dgn2[0])
        dgq.insert(0, dgq_l[0, :HEAD_DIM])
        dgk.insert(0, dgk_l[0, :HEAD_DIM])
        drel.insert(0, drel_l.reshape(-1))
    grad_x = dx[None]

    small = jnp.concatenate([jnp.concatenate(dmod), jnp.concatenate(dg1), jnp.concatenate(dg2),
                             jnp.concatenate(dgq), jnp.concatenate(dgk), jnp.concatenate(drel), loss_part])
    n_small = small.shape[0]
    n_pack = -(-n_small // (8 * LANES)) * (8 * LANES)
    small = jnp.pad(small, (0, n_pack - n_small)).reshape(8, n_pack // 8)
    small_all = _all_gather_small("ag_small", small).reshape(N_DEV, 8, n_pack // 8)
    small_sum = _sum_rows8("sum_small", small_all).reshape(-1)
    dmod_all = small_all.reshape(N_DEV, n_pack)[:, :n_layers * 6 * d].reshape(N_DEV, n_layers, 6 * d)

    off = 0

    def take(n, shape):
        nonlocal off
        out = small_sum[off:off + n].reshape(shape)
        off += n
        return out

    grad_b_ada = take(n_layers * 6 * d, (n_layers, 6 * d))
    grad_g_norm1 = take(n_layers * d, (n_layers, d))
    grad_g_norm2 = take(n_layers * d, (n_layers, d))
    grad_g_q = take(n_layers * HEAD_DIM, (n_layers, HEAD_DIM))
    grad_g_k = take(n_layers * HEAD_DIM, (n_layers, HEAD_DIM))
    grad_rel_bias = take(n_layers * n_ca_heads * n_rel, (n_layers, n_ca_heads, n_rel))
    loss = take(1, ())

    c_act_t = jnp.transpose(c_act_all)
    grad_w_ada = jnp.stack([
        _outer_sum(f"wada_grad_{l}", c_act_t,
                   lax.dynamic_slice_in_dim(dmod_all[:, l, :], chip * ada_c, ada_c, axis=1))
        for l in range(n_layers)])

    assert all(k in from_chip for k in partial)
    pos_idx = jnp.stack([chip, ac]).astype(jnp.int32)
    reduced = [_sum_chips(f"rs_sum_{w_i}", [chip_sum[(w_i, l)] for l in range(n_layers)],
                          [from_chip[(w_i, l)] for l in range(n_layers)], pos_idx) for w_i in range(4)]
    grad_w_in, grad_w_o, grad_w1, grad_w2 = _share_with_sibling("rs_share", reduced)

    def adam_big(name, w, g, m, v):
        shp = w.shape
        outs = _adamw(name, w.reshape(-1, shp[-1]), g.reshape(-1, shp[-1]), m.reshape(-1, shp[-1]),
                      v.reshape(-1, shp[-1]))
        return [o.reshape(shp) for o in outs]

    def pack(arrs):
        flat = jnp.concatenate([a.reshape(-1) for a in arrs])
        n = flat.shape[0]
        n_p = -(-n // (8 * LANES)) * (8 * LANES)
        return jnp.pad(flat, (0, n_p - n), constant_values=1.0).reshape(8, n_p // 8)

    small_w = [g_norm1, g_q, g_k, rel_bias, g_norm2, b_ada]
    small_g = [grad_g_norm1, grad_g_q, grad_g_k, grad_rel_bias, grad_g_norm2, grad_b_ada]
    small_m = [m_g_norm1, m_g_q, m_g_k, m_rel_bias, m_g_norm2, m_b_ada]
    small_v = [v_g_norm1, v_g_q, v_g_k, v_rel_bias, v_g_norm2, v_b_ada]
    packed = _adamw("adamw_small", pack(small_w), pack(small_g), pack(small_m), pack(small_v))

    def unpack(p):
        flat = p.reshape(-1)
        res, o = [], 0
        for a in small_w:
            res.append(flat[o:o + a.size].reshape(a.shape))
            o += a.size
        return res

    sd, sm, sv_ = unpack(packed[0]), unpack(packed[1]), unpack(packed[2])
    big = {
        "w_in": adam_big("adamw_w_in", w_in, grad_w_in, m_w_in, v_w_in),
        "w_o": adam_big("adamw_w_o", w_o, grad_w_o, m_w_o, v_w_o),
        "w1": adam_big("adamw_w1", w1, grad_w1, m_w1, v_w1),
        "w2": adam_big("adamw_w2", w2, grad_w2, m_w2, v_w2),
        "w_ada": adam_big("adamw_w_ada", w_ada, grad_w_ada, m_w_ada, v_w_ada),
    }

    def ordered(kind):
        sm_list = (sd, sm, sv_)[kind]
        return [sm_list[0], big["w_in"][kind], sm_list[1], sm_list[2], sm_list[3], big["w_o"][kind], sm_list[4],
                big["w1"][kind], big["w2"][kind], big["w_ada"][kind], sm_list[5]]

    grads = [grad_g_norm1, grad_w_in, grad_g_q, grad_g_k, grad_rel_bias, grad_w_o, grad_g_norm2, grad_w1, grad_w2,
             grad_w_ada, grad_b_ada]
    return (loss, grad_x, *grads, *ordered(0), *ordered(1), *ordered(2))
```

```python
import functools

import jax
import jax.numpy as jnp
from jax import lax
from jax.experimental import pallas as pl
from jax.experimental.pallas import tpu as pltpu

F32 = jnp.float32
BF16 = jnp.bfloat16
MESH = pl.DeviceIdType.MESH

EPS = 1e-6
NEG_INF = -1e30
HEAD_DIM = 64
CHUNK = 64
LEFT_CHUNKS = 8
PAD = LEFT_CHUNKS * CHUNK
BAND = PAD + CHUNK
REL_CLIP = 128
SCALE = HEAD_DIM ** -0.5
DEAD_LOG = -88.0
LANES = 128
N_CHIPS = 4
N_DEV = 8

ADAM_LR = 0.001
ADAM_B1 = 0.9
ADAM_B2 = 0.999
ADAM_EPS = 1e-08
ADAM_WD = 0.01
ADAM_STEP = 10

VMEM_LIMIT = 48 * 1024 * 1024
ROW_TILE = 512
MM_TILE = 1024
SB_TILE = 256
CA_TILE = 512

NT = (((1,), (1,)), ((), ()))
TN = (((0,), (0,)), ((), ()))
NN = (((1,), (0,)), ((), ()))


def _tile(n, pref):
    best = None
    t = LANES
    while t <= min(n, pref):
        if n % t == 0:
            best = t
        t += LANES
    return best if best is not None else n


def _params(*sem):
    return pltpu.CompilerParams(dimension_semantics=sem, vmem_limit_bytes=VMEM_LIMIT)


def _split_dot(x, t, passes):
    acc = None
    r = x
    for i in range(passes):
        hi = r.astype(BF16)
        d = jnp.dot(hi, t, preferred_element_type=F32)
        acc = d if acc is None else acc + d
        if i + 1 < passes:
            r = r - hi.astype(F32)
    return acc


def _grid_edge(grid, last):
    conds = [pl.program_id(i) == (g - 1 if last else 0) for i, g in enumerate(grid)]
    return functools.reduce(jnp.logical_and, conds)


def _ride_gather(grid, g_refs, sems, compute):
    if g_refs:
        @pl.when(_grid_edge(grid, False))
        def _():
            for cp in _gather_copies(g_refs, *sems, 0)[0]:
                cp.start()

    compute()
    if g_refs:
        @pl.when(_grid_edge(grid, True))
        def _():
            for cp, ld in zip(*_gather_copies(g_refs, *sems, 0)):
                cp.wait_send()
                ld.wait_recv()


def _rider(ins, out_shapes, sems, copies, aliased):
    return dict(ins=list(ins), outs=list(out_shapes), sems=list(sems), copies=copies, aliased=aliased)


def _gather_rider(bufs):
    def copies(_, outs, sems):
        def both(cp, ld):
            def wait():
                cp.wait_send()
                ld.wait_recv()
            return wait
        return [(cp, both(cp, ld)) for cp, ld in zip(*_gather_copies(outs, *sems, 0))]

    return _rider(bufs, [jax.ShapeDtypeStruct(b.shape, b.dtype) for b in bufs], _gather_sems(len(bufs)), copies, True)


def _exchange_rider(parts):
    def copies(ins, outs, sems):
        return [(cp, cp.wait) for cp in _exchange_copies(ins, outs, *sems)]

    return _rider(parts, [jax.ShapeDtypeStruct((3,) + p.shape[1:], p.dtype) for p in parts],
                  _exchange_sems(len(parts)), copies, False)


def _sibling_rider(grads):
    def copies(ins, outs, sems):
        return [(cp, cp.wait) for cp in _sibling_copies(ins, outs, *sems)]

    n = len(grads)
    return _rider(grads, [jax.ShapeDtypeStruct((g.shape[0], g.shape[1] // 2, g.shape[2]), g.dtype) for g in grads],
                  [pltpu.SemaphoreType.DMA((n,)), pltpu.SemaphoreType.DMA((n,))], copies, False)


def _matmul(name, a, b, *, dims, grid, a_spec, b_spec, out_shapes, out_specs, acc_shape,
            epi=None, extras=(), extra_specs=(), riders=()):
    nk = grid[2]
    n_ex = len(extras)
    n_out = len(out_shapes)
    n_in = [len(r["ins"]) for r in riders]
    n_ro = [len(r["outs"]) for r in riders]

    def body(*refs):
        a_ref, b_ref = refs[0], refs[1]
        ex = refs[2:2 + n_ex]
        pos = 2 + n_ex
        r_in = []
        for n in n_in:
            r_in.append(refs[pos:pos + n])
            pos += n
        outs = refs[pos:pos + n_out]
        pos += n_out
        r_out = []
        for n in n_ro:
            r_out.append(refs[pos:pos + n])
            pos += n
        acc = refs[pos]
        pos += 1
        r_sems = []
        for r in riders:
            r_sems.append(refs[pos:pos + len(r["sems"])])
            pos += len(r["sems"])

        def all_copies():
            return [c for r, i, o, sm in zip(riders, r_in, r_out, r_sems) for c in r["copies"](i, o, sm)]

        if riders:
            @pl.when(_grid_edge(grid, False))
            def _():
                for cp, _ in all_copies():
                    cp.start()

        k = pl.program_id(2)

        @pl.when(k == 0)
        def _():
            acc[...] = jnp.zeros_like(acc)

        acc[...] += lax.dot_general(a_ref[...].astype(BF16), b_ref[...].astype(BF16), dims,
                                    preferred_element_type=F32)

        @pl.when(k == nk - 1)
        def _():
            res = epi(acc[...], *[e[...] for e in ex]) if epi is not None else (acc[...],)
            for o, r in zip(outs, res):
                o[...] = r.astype(o.dtype)

        if riders:
            @pl.when(_grid_edge(grid, True))
            def _():
                for _, wait in all_copies():
                    wait()

    r_ins = [x for r in riders for x in r["ins"]]
    r_outs = [x for r in riders for x in r["outs"]]
    aliases, i_pos, o_pos = {}, 2 + n_ex, n_out
    for r in riders:
        if r["aliased"]:
            aliases.update({i_pos + i: o_pos + i for i in range(len(r["ins"]))})
        i_pos += len(r["ins"])
        o_pos += len(r["outs"])
    outs = pl.pallas_call(
        body, grid=grid, in_specs=[a_spec, b_spec, *extra_specs] + [HBM_SPEC] * len(r_ins),
        out_specs=list(out_specs) + [HBM_SPEC] * len(r_outs), out_shape=list(out_shapes) + r_outs,
        input_output_aliases=aliases,
        scratch_shapes=[pltpu.VMEM(acc_shape, F32)] + [sm for r in riders for sm in r["sems"]],
        compiler_params=_params(*(("arbitrary",) * 3 if riders else ("parallel", "parallel", "arbitrary"))), name=name,
    )(a, b, *extras, *r_ins)
    landed, pos = [], n_out
    for n in n_ro:
        landed.append(list(outs[pos:pos + n]))
        pos += n
    return list(outs[:n_out]), landed


def _mm_nn_w(name, a, wg, layer, colshard, out_dtypes, epi=None, extras=(), extra_specs_fn=None, gather=()):
    m = a.shape[0]
    _, _, r, c = wg.shape
    tm = min(MM_TILE, m)
    if colshard:
        k, n = r, N_CHIPS * c
        tk, tn = k, _tile(c, MM_TILE)
        npc = c // tn
        b_spec = pl.BlockSpec((None, None, tk, tn), lambda i, j, kk: (layer, j // npc, kk, j % npc))
    else:
        k, n = N_CHIPS * r, c
        tk, tn = _tile(r, MM_TILE), _tile(c, MM_TILE)
        kpc = r // tk
        b_spec = pl.BlockSpec((None, None, tk, tn), lambda i, j, kk: (layer, kk // kpc, kk % kpc, j))
    assert a.shape[1] == k
    grid = (m // tm, n // tn, k // tk)
    out_spec = pl.BlockSpec((tm, tn), lambda i, j, kk: (i, j))
    ex_specs = extra_specs_fn(tm, tn) if extra_specs_fn is not None else ()
    outs, landed = _matmul(name, a, wg, dims=NN, grid=grid,
                           a_spec=pl.BlockSpec((tm, tk), lambda i, j, kk: (i, kk)), b_spec=b_spec,
                           out_shapes=[jax.ShapeDtypeStruct((m, n), d) for d in out_dtypes],
                           out_specs=[out_spec for _ in out_dtypes], acc_shape=(tm, tn),
                           epi=epi, extras=extras, extra_specs=ex_specs,
                           riders=[_gather_rider(gather)] if gather else ())
    return (outs, landed[0]) if gather else outs


def _mm_nt_w(name, a, wg, layer, colshard, out_dtype, epi=None, extras=(), extra_specs_fn=None, riders=()):
    m = a.shape[0]
    _, _, r, c = wg.shape
    tm = min(MM_TILE, m)
    if colshard:
        kdim, n = N_CHIPS * c, r
        tk, tn = _tile(c, MM_TILE), _tile(r, MM_TILE)
        kpc = c // tk
        b_spec = pl.BlockSpec((None, None, tn, tk), lambda i, j, kk: (layer, kk // kpc, j, kk % kpc))
    else:
        kdim, n = c, N_CHIPS * r
        tk, tn = c, _tile(r, MM_TILE)
        npc = r // tn
        b_spec = pl.BlockSpec((None, None, tn, tk), lambda i, j, kk: (layer, j // npc, j % npc, kk))
    assert a.shape[1] == kdim
    grid = (m // tm, n // tn, kdim // tk)
    ex_specs = extra_specs_fn(tm, tn) if extra_specs_fn is not None else ()
    outs, landed = _matmul(name, a, wg, dims=NT, grid=grid,
                   a_spec=pl.BlockSpec((tm, tk), lambda i, j, kk: (i, kk)), b_spec=b_spec,
                   out_shapes=[jax.ShapeDtypeStruct((m, n), out_dtype)],
                   out_specs=[pl.BlockSpec((tm, tn), lambda i, j, kk: (i, j))], acc_shape=(tm, tn),
                   epi=epi, extras=extras, extra_specs=ex_specs, riders=riders)
    return (outs[0], landed) if riders else outs[0]


def _mm_tn(name, a, b, colshard):
    s, m = a.shape
    n = b.shape[1]
    tk = min(2 * MM_TILE, s)
    if colshard:
        c = n // N_CHIPS
        tm, tn = _tile(m, MM_TILE // 2), _tile(c, MM_TILE)
        npc = c // tn
        out_shape = (N_CHIPS, m, c)
        out_spec = pl.BlockSpec((None, tm, tn), lambda i, j, kk: (j // npc, i, j % npc))
    else:
        r = m // N_CHIPS
        tm, tn = _tile(r, MM_TILE // 2), _tile(n, MM_TILE)
        mpc = r // tm
        out_shape = (N_CHIPS, r, n)
        out_spec = pl.BlockSpec((None, tm, tn), lambda i, j, kk: (i // mpc, i % mpc, j))
    grid = (m // tm, n // tn, s // tk)
    return _matmul(name, a, b, dims=TN, grid=grid,
                   a_spec=pl.BlockSpec((tk, tm), lambda i, j, kk: (kk, i)),
                   b_spec=pl.BlockSpec((tk, tn), lambda i, j, kk: (kk, j)),
                   out_shapes=[jax.ShapeDtypeStruct(out_shape, F32)], out_specs=[out_spec],
                   acc_shape=(tm, tn))[0][0]


def _row_spec(ts, d):
    return pl.BlockSpec((ts, d), lambda i: (i, 0))


def _vec_spec(d):
    return pl.BlockSpec((1, d), lambda i: (0, 0))


def _norm_mod(name, x, g, sc, sh):
    s, d = x.shape
    ts = min(ROW_TILE, s)

    def body(x_ref, g_ref, sc_ref, sh_ref, h_ref):
        xv = x_ref[...]
        r = lax.rsqrt(jnp.mean(xv * xv, axis=-1, keepdims=True) + EPS)
        h_ref[...] = (xv * r * g_ref[...] * (1.0 + sc_ref[...]) + sh_ref[...]).astype(BF16)

    return pl.pallas_call(
        body, grid=(s // ts,), in_specs=[_row_spec(ts, d), _vec_spec(d), _vec_spec(d), _vec_spec(d)],
        out_specs=_row_spec(ts, d), out_shape=jax.ShapeDtypeStruct((s, d), BF16),
        compiler_params=_params("parallel"), name=name)(x, g, sc, sh)


def _norm_mod_bwd(name, dh, x, g, sc, dres):
    s, d = x.shape
    ts = min(ROW_TILE, s)

    def body(dh_ref, x_ref, g_ref, sc_ref, dres_ref, dx_ref, dsh_ref, dsc_ref, dg_ref):
        @pl.when(pl.program_id(0) == 0)
        def _():
            dsh_ref[...] = jnp.zeros_like(dsh_ref)
            dsc_ref[...] = jnp.zeros_like(dsc_ref)
            dg_ref[...] = jnp.zeros_like(dg_ref)

        xv = x_ref[...]
        r = lax.rsqrt(jnp.mean(xv * xv, axis=-1, keepdims=True) + EPS)
        xhat = xv * r
        dhv = dh_ref[...]
        gv = g_ref[...]
        opsc = 1.0 + sc_ref[...]
        dxhat = dhv * (gv * opsc)
        mdot = jnp.mean(dxhat * xhat, axis=-1, keepdims=True)
        dx_ref[...] = dres_ref[...] + r * (dxhat - xhat * mdot)
        dhx = dhv * xhat
        dsh_ref[...] += jnp.sum(dhv, axis=0, keepdims=True)
        dsc_ref[...] += jnp.sum(dhx * gv, axis=0, keepdims=True)
        dg_ref[...] += jnp.sum(dhx * opsc, axis=0, keepdims=True)

    vec = jax.ShapeDtypeStruct((1, d), F32)
    return pl.pallas_call(
        body, grid=(s // ts,),
        in_specs=[_row_spec(ts, d), _row_spec(ts, d), _vec_spec(d), _vec_spec(d), _row_spec(ts, d)],
        out_specs=[_row_spec(ts, d), _vec_spec(d), _vec_spec(d), _vec_spec(d)],
        out_shape=[jax.ShapeDtypeStruct((s, d), F32), vec, vec, vec],
        compiler_params=_params("arbitrary"), name=name)(dh, x, g, sc, dres)


def _gate_bwd(name, dx, m, gt):
    s, d = dx.shape
    ts = min(ROW_TILE, s)

    def body(dx_ref, m_ref, gt_ref, dm_ref, dgt_ref):
        @pl.when(pl.program_id(0) == 0)
        def _():
            dgt_ref[...] = jnp.zeros_like(dgt_ref)

        dxv = dx_ref[...]
        dm_ref[...] = (dxv * gt_ref[...]).astype(BF16)
        dgt_ref[...] += jnp.sum(dxv * m_ref[...].astype(F32), axis=0, keepdims=True)

    return pl.pallas_call(
        body, grid=(s // ts,), in_specs=[_row_spec(ts, d), _row_spec(ts, d), _vec_spec(d)],
        out_specs=[_row_spec(ts, d), _vec_spec(d)],
        out_shape=[jax.ShapeDtypeStruct((s, d), BF16), jax.ShapeDtypeStruct((1, d), F32)],
        compiler_params=_params("arbitrary"), name=name)(dx, m, gt)


def _loss_grad(name, y, target):
    s, d = y.shape
    ts = min(ROW_TILE, s)

    def body(y_ref, t_ref, dy_ref, sq_ref):
        @pl.when(pl.program_id(0) == 0)
        def _():
            sq_ref[...] = jnp.zeros_like(sq_ref)

        e = y_ref[...] - t_ref[...]
        dy_ref[...] = e * (1.0 / d)
        sq_ref[...] += jnp.sum(e * e, axis=0, keepdims=True)

    return pl.pallas_call(
        body, grid=(s // ts,), in_specs=[_row_spec(ts, d), _row_spec(ts, d)],
        out_specs=[_row_spec(ts, d), _vec_spec(d)],
        out_shape=[jax.ShapeDtypeStruct((s, d), F32), jax.ShapeDtypeStruct((1, d), F32)],
        compiler_params=_params("arbitrary"), name=name)(y, target)


def _adamw(name, w, g, m, v):
    r, c = w.shape
    tr = r
    for cand in (512, 256, 128, 64, 32, 16, 8):
        if r % cand == 0 and cand * c * 4 <= 2 * 1024 * 1024:
            tr = cand
            break

    def body(w_ref, g_ref, m_ref, v_ref, d_ref, mo_ref, vo_ref):
        gv = g_ref[...]
        m2 = ADAM_B1 * m_ref[...] + (1.0 - ADAM_B1) * gv
        v2 = ADAM_B2 * v_ref[...] + (1.0 - ADAM_B2) * (gv * gv)
        m_hat = m2 / (1.0 - ADAM_B1 ** ADAM_STEP)
        v_hat = v2 / (1.0 - ADAM_B2 ** ADAM_STEP)
        d_ref[...] = -ADAM_LR * (m_hat / (jnp.sqrt(v_hat) + ADAM_EPS) + ADAM_WD * w_ref[...])
        mo_ref[...] = m2
        vo_ref[...] = v2

    spec = pl.BlockSpec((tr, c), lambda i: (i, 0))
    shp = jax.ShapeDtypeStruct((r, c), F32)
    return pl.pallas_call(
        body, grid=(r // tr,), in_specs=[spec] * 4, out_specs=[spec] * 3, out_shape=[shp] * 3,
        compiler_params=_params("parallel"), name=name)(w, g, m, v)


def _sum_rows8(name, x):
    n, r, c = x.shape

    def body(x_ref, o_ref):
        acc = x_ref[0]
        for i in range(1, n):
            acc = acc + x_ref[i]
        o_ref[...] = acc

    return pl.pallas_call(
        body, out_shape=jax.ShapeDtypeStruct((r, c), F32),
        in_specs=[pl.BlockSpec(memory_space=pltpu.VMEM)], out_specs=pl.BlockSpec(memory_space=pltpu.VMEM),
        name=name)(x)


def _outer_sum(name, ct, dm):
    d, n_seq = ct.shape
    n = dm.shape[1]
    tr = min(256, d)

    def body(ct_ref, dm_ref, o_ref):
        acc = ct_ref[:, 0:1] * dm_ref[0:1, :]
        for s_i in range(1, n_seq):
            acc = acc + ct_ref[:, s_i:s_i + 1] * dm_ref[s_i:s_i + 1, :]
        o_ref[...] = acc

    return pl.pallas_call(
        body, grid=(d // tr,),
        in_specs=[pl.BlockSpec((tr, n_seq), lambda i: (i, 0)), pl.BlockSpec((n_seq, n), lambda i: (0, 0))],
        out_specs=pl.BlockSpec((tr, n), lambda i: (i, 0)), out_shape=jax.ShapeDtypeStruct((d, n), F32),
        compiler_params=_params("parallel"), name=name)(ct, dm)


def _ada_mod(name, c_act, w_ada_l, bias):
    n_seq, d = c_act.shape
    n = w_ada_l.shape[1]
    tn = _tile(n, 768)

    def epi(acc, b):
        return (acc + b,)

    return _matmul(name, c_act, w_ada_l, dims=NN, grid=(1, n // tn, 1),
                   a_spec=pl.BlockSpec((n_seq, d), lambda i, j, kk: (0, 0)),
                   b_spec=pl.BlockSpec((d, tn), lambda i, j, kk: (0, j)),
                   out_shapes=[jax.ShapeDtypeStruct((n_seq, n), F32)],
                   out_specs=[pl.BlockSpec((n_seq, tn), lambda i, j, kk: (0, j))], acc_shape=(n_seq, tn),
                   epi=epi, extras=(bias,), extra_specs=(pl.BlockSpec((1, tn), lambda i, j, kk: (0, j)),))[0][0]


def _silu_rows(name, c_row):
    d = c_row.shape[1]

    def body(c_ref, o_ref):
        cv = c_ref[...]
        o_ref[...] = jnp.broadcast_to(cv / (1.0 + jnp.exp(-cv)), (8, d))

    return pl.pallas_call(
        body, out_shape=jax.ShapeDtypeStruct((8, d), F32),
        in_specs=[pl.BlockSpec(memory_space=pltpu.VMEM)], out_specs=pl.BlockSpec(memory_space=pltpu.VMEM),
        name=name)(c_row)


def _sb_masks(t):
    row = lax.broadcasted_iota(jnp.int32, (t, t), 0)
    col = lax.broadcasted_iota(jnp.int32, (t, t), 1)
    strict = col < row
    tri = (row > col).astype(BF16)
    return strict, tri


def _col_minus_row(t):
    return lax.broadcasted_iota(jnp.int32, (t, t), 1) - lax.broadcasted_iota(jnp.int32, (t, t), 0)


def _sb_block(qi, i, sb):
    return qi - 2 * i - (1 - sb)


def _sb_keep(diff, qi, kb, t):
    return diff < jnp.where(kb >= 0, (qi - kb) * t, -t)


def _sweep_left(step, qi, carry, runs_of):
    def alive(c):
        runs = runs_of(c)
        top = runs[0]
        for r in runs[1:]:
            top = jnp.maximum(top, r)
        return jnp.max(top)

    n_plain = jnp.maximum(qi - 1, 0) // 2

    def cond(state):
        i, _, top = state
        return jnp.logical_and(i <= n_plain, top > DEAD_LOG)

    def body(state):
        i, c, _ = state
        c = step(i, c, False)
        return i + 1, c, alive(c)

    _, carry, top = lax.while_loop(cond, body, (1, carry, alive(carry)))
    lone = jnp.logical_and(jnp.logical_and(qi >= 2, qi % 2 == 0), top > DEAD_LOG)
    return lax.cond(lone, lambda c: step(qi // 2, c, True), lambda c: c, carry)


def _log_sigmoids(z):
    sp = jnp.log(1.0 + jnp.exp(-jnp.abs(z)))
    return jnp.minimum(z, 0.0) - sp, jnp.minimum(-z, 0.0) - sp, sp


def _sb_fwd(name, proj, n_pairs, gather=()):
    s = proj.shape[0]
    t = min(SB_TILE, s)
    nq = s // t
    n_g = len(gather)

    heads = [slice(h * HEAD_DIM, (h + 1) * HEAD_DIM) for h in range(LANES // HEAD_DIM)]

    assert nq % 2 == 0

    def body(*refs):
        q_ref, k_ref, v_ref = refs[:3]
        o_ref = refs[3 + n_g]
        g_refs, sems = refs[4 + n_g:4 + 2 * n_g], refs[4 + 2 * n_g:]
        _ride_gather((n_pairs, nq), g_refs, sems, lambda: _sb_fwd_body(q_ref, k_ref, v_ref, o_ref))

    def _sb_fwd_body(q_ref, k_ref, v_ref, o_ref):
        qi = pl.program_id(1)
        _, tri = _sb_masks(t)
        diff = _col_minus_row(t)
        qs = [(q_ref[:, sl] * SCALE).astype(BF16) for sl in heads]
        chains = [(sb, h) for sb in (1, 0) for h in range(len(heads))]

        def step(j, carry, masked):
            kbs = [_sb_block(qi, j, sb) for sb, _ in chains]
            ks = [pl.multiple_of(jnp.maximum(kb, 0) * t, t) for kb in kbs]
            k_ = [k_ref[pl.ds(ks[i], t), heads[h]].astype(BF16) for i, (_, h) in enumerate(chains)]
            v_ = [v_ref[pl.ds(ks[i], t), heads[h]].astype(BF16) for i, (_, h) in enumerate(chains)]
            zs = [lax.dot_general(qs[h], k_[i], NT, preferred_element_type=F32) for i, (_, h) in enumerate(chains)]
            lbs, lms, keeps = [], [], []
            for kb, z in zip(kbs, zs):
                lb, lm, _ = _log_sigmoids(z)
                keep = None
                if masked:
                    keep = _sb_keep(diff, qi, kb, t)
                    lm = jnp.where(keep, lm, 0.0)
                lbs.append(lb)
                lms.append(lm)
                keeps.append(keep)
            css = [_split_dot(lm, tri, 2) for lm in lms]
            sums = [jnp.sum(lm, axis=1, keepdims=True) for lm in lms]
            ws = []
            for i, (sb, h) in enumerate(chains):
                run = carry[h][1] if sb == 1 else carry[h][1] + sums[h]
                w = jnp.exp(lbs[i] + css[i] + run)
                if masked:
                    w = jnp.where(keeps[i], w, 0.0)
                ws.append(w.astype(BF16))
            pvs = [jnp.dot(w, v, preferred_element_type=F32) for w, v in zip(ws, v_)]
            nh = len(heads)
            return tuple((carry[h][0] + pvs[h] + pvs[nh + h], carry[h][1] + sums[h] + sums[nh + h]) for h in range(nh))

        init = tuple((jnp.zeros((t, HEAD_DIM), F32), jnp.zeros((t, 1), F32)) for _ in heads)
        carry = step(0, init, True)
        carry = _sweep_left(step, qi, carry, lambda c: [run for _, run in c])
        for (acc, _), sl in zip(carry, heads):
            o_ref[:, sl] = acc

    outs = pl.pallas_call(
        body, grid=(n_pairs, nq),
        in_specs=[pl.BlockSpec((t, LANES), lambda p, i: (i, p)),
                  pl.BlockSpec((s, LANES), lambda p, i: (0, n_pairs + p)),
                  pl.BlockSpec((s, LANES), lambda p, i: (0, 2 * n_pairs + p))] + [HBM_SPEC] * n_g,
        out_specs=[pl.BlockSpec((t, LANES), lambda p, i: (i, p))] + [HBM_SPEC] * n_g,
        out_shape=[jax.ShapeDtypeStruct((s, 2 * n_pairs * LANES), F32)]
        + [jax.ShapeDtypeStruct(g.shape, g.dtype) for g in gather],
        input_output_aliases={3 + i: 1 + i for i in range(n_g)},
        scratch_shapes=_gather_sems(n_g) if n_g else [],
        compiler_params=_params("arbitrary", "arbitrary"), name=name)(proj, proj, proj, *gather)
    return outs[0], list(outs[1:])


def _sb_bwd(name, proj, o_sb, dmixed, n_pairs, exchange=()):
    s = proj.shape[0]
    t = min(SB_TILE, s)
    nq = s // t
    n_x = len(exchange)
    heads = [slice(h * HEAD_DIM, (h + 1) * HEAD_DIM) for h in range(LANES // HEAD_DIM)]

    def body(*refs):
        x_in, x_out, sems = refs[5:5 + n_x], refs[8 + n_x:8 + 2 * n_x], refs[8 + 2 * n_x:]
        if n_x:
            @pl.when(_grid_edge((n_pairs, nq), False))
            def _():
                for cp in _exchange_copies(x_in, x_out, *sems):
                    cp.start()

        _sb_bwd_body(*refs[:5], *refs[5 + n_x:8 + n_x])
        if n_x:
            @pl.when(_grid_edge((n_pairs, nq), True))
            def _():
                for cp in _exchange_copies(x_in, x_out, *sems):
                    cp.wait()

    def _sb_bwd_body(q_ref, k_ref, v_ref, o_ref, do_ref, dq_ref, dk_ref, dv_ref):
        qi = pl.program_id(1)

        @pl.when(qi == 0)
        def _():
            dk_ref[...] = jnp.zeros_like(dk_ref)
            dv_ref[...] = jnp.zeros_like(dv_ref)

        _, tri = _sb_masks(t)
        diff = _col_minus_row(t)
        qs = [(q_ref[:, sl] * SCALE).astype(BF16) for sl in heads]
        dos = [do_ref[:, sl].astype(BF16) for sl in heads]
        dsums = [jnp.sum(do.astype(F32) * o_ref[:, sl], axis=1, keepdims=True) for do, sl in zip(dos, heads)]
        nh = len(heads)
        chains = [(sb, h) for sb in (1, 0) for h in range(nh)]

        def step(j, carry, masked):
            kbs = [_sb_block(qi, j, sb) for sb, _ in chains]
            ks = [pl.multiple_of(jnp.maximum(kb, 0) * t, t) for kb in kbs]
            k_ = [k_ref[pl.ds(ks[i], t), heads[h]].astype(BF16) for i, (_, h) in enumerate(chains)]
            v_ = [v_ref[pl.ds(ks[i], t), heads[h]].astype(BF16) for i, (_, h) in enumerate(chains)]
            zs = [lax.dot_general(qs[h], k_[i], NT, preferred_element_type=F32) for i, (_, h) in enumerate(chains)]
            dws = [lax.dot_general(dos[h], v_[i], NT, preferred_element_type=F32) for i, (_, h) in enumerate(chains)]
            lbs, lms, betas, ombs, keeps = [], [], [], [], []
            for kb, z in zip(kbs, zs):
                ez = jnp.exp(-jnp.abs(z))
                lb = jnp.minimum(z, 0.0) - jnp.log(1.0 + ez)
                lm = lb - z
                betas.append(jnp.exp(lb))
                ombs.append(jnp.exp(lm))
                keep = None
                if masked:
                    keep = _sb_keep(diff, qi, kb, t)
                    lm = jnp.where(keep, lm, 0.0)
                lbs.append(lb)
                lms.append(lm)
                keeps.append(keep)
            css = [_split_dot(lm, tri, 2) for lm in lms]
            sums_f = [jnp.sum(lm, axis=1, keepdims=True) for lm in lms]
            wbs, es = [], []
            for i, (sb, h) in enumerate(chains):
                run_f = carry[h][1] if sb == 1 else carry[h][1] + sums_f[h]
                w = jnp.exp(lbs[i] + css[i] + run_f)
                if masked:
                    w = jnp.where(keeps[i], w, 0.0)
                wb = w.astype(BF16)
                wbs.append(wb)
                es.append(wb.astype(F32) * dws[i])
            for i, (_, h) in enumerate(chains):
                dv_ref[pl.ds(ks[i], t), heads[h]] += lax.dot_general(wbs[i], dos[h], TN, preferred_element_type=F32)
            ecs = [_split_dot(e, tri, 2) for e in es]
            sums_e = [jnp.sum(e, axis=1, keepdims=True) for e in es]
            dzbs = []
            for i, (sb, h) in enumerate(chains):
                run_e = carry[h][2] if sb == 1 else carry[h][2] + sums_e[h]
                left = dsums[h] - (ecs[i] + es[i] + run_e)
                dz = es[i] * ombs[i] - left * betas[i]
                if masked:
                    dz = jnp.where(keeps[i], dz, 0.0)
                dzbs.append(dz.astype(BF16))
            dqs = []
            for i, (_, h) in enumerate(chains):
                dk_ref[pl.ds(ks[i], t), heads[h]] += lax.dot_general(dzbs[i], qs[h], TN, preferred_element_type=F32)
                dqs.append(jnp.dot(dzbs[i], k_[i], preferred_element_type=F32))
            return tuple((carry[h][0] + dqs[h] + dqs[nh + h], carry[h][1] + sums_f[h] + sums_f[nh + h],
                          carry[h][2] + sums_e[h] + sums_e[nh + h]) for h in range(nh))

        zero = jnp.zeros((t, 1), F32)
        init = tuple((jnp.zeros((t, HEAD_DIM), F32), zero, zero) for _ in heads)
        carry = step(0, init, True)
        carry = _sweep_left(step, qi, carry, lambda c: [run_f for _, run_f, _ in c])
        for (dq, _, _), sl in zip(carry, heads):
            dq_ref[:, sl] = dq * SCALE

    blk = pl.BlockSpec((t, LANES), lambda p, i: (i, p))
    full = pl.BlockSpec((s, LANES), lambda p, i: (0, p))
    shp = jax.ShapeDtypeStruct((s, n_pairs * LANES), F32)
    outs = pl.pallas_call(
        body, grid=(n_pairs, nq),
        in_specs=[blk,
                  pl.BlockSpec((s, LANES), lambda p, i: (0, n_pairs + p)),
                  pl.BlockSpec((s, LANES), lambda p, i: (0, 2 * n_pairs + p)),
                  blk, blk] + [HBM_SPEC] * n_x,
        out_specs=[blk, full, full] + [HBM_SPEC] * n_x,
        out_shape=[shp, shp, shp] + [jax.ShapeDtypeStruct((3,) + p.shape[1:], p.dtype) for p in exchange],
        scratch_shapes=_exchange_sems(n_x) if n_x else [],
        compiler_params=_params("arbitrary", "arbitrary"), name=name)(proj, proj, proj, o_sb, dmixed, *exchange)
    return outs[0], outs[1], outs[2], list(outs[3:])


def _group_mean_matrix():
    row = lax.broadcasted_iota(jnp.int32, (LANES, LANES), 0)
    col = lax.broadcasted_iota(jnp.int32, (LANES, LANES), 1)
    same_head = (row < HEAD_DIM) == (col < HEAD_DIM)
    return jnp.where(same_head, 1.0 / HEAD_DIM, 0.0).astype(BF16)


def _qk_norm(name, proj, gq2, gk2, n_pairs):
    s = proj.shape[0]
    ts = min(ROW_TILE, s)
    pb = PAD // ts
    c0 = 3 * n_pairs

    def body(q_ref, k_ref, v_ref, gq_ref, gk_ref, qn_ref, kn_ref, vp_ref):
        r_i = pl.program_id(1)

        @pl.when(r_i < pb)
        def _():
            qn_ref[...] = jnp.zeros_like(qn_ref)
            kn_ref[...] = jnp.zeros_like(kn_ref)
            vp_ref[...] = jnp.zeros_like(vp_ref)

        @pl.when(r_i >= pb)
        def _():
            gm = _group_mean_matrix()
            for src, g_ref, dst in ((q_ref, gq_ref, qn_ref), (k_ref, gk_ref, kn_ref)):
                xv = src[...]
                r = lax.rsqrt(_split_dot(xv * xv, gm, 2) + EPS)
                dst[...] = (xv * r * g_ref[...]).astype(BF16)
            vp_ref[...] = v_ref[...].astype(BF16)

    def in_spec(off):
        return pl.BlockSpec((ts, LANES), lambda p, i: (jnp.maximum(i - pb, 0), c0 + off * n_pairs + p))

    gspec = pl.BlockSpec((1, LANES), lambda p, i: (0, 0))
    ospec = pl.BlockSpec((ts, LANES), lambda p, i: (i, p))
    shp = jax.ShapeDtypeStruct((s + PAD, n_pairs * LANES), BF16)
    return pl.pallas_call(
        body, grid=(n_pairs, (s + PAD) // ts),
        in_specs=[in_spec(0), in_spec(1), in_spec(2), gspec, gspec],
        out_specs=[ospec, ospec, ospec], out_shape=[shp, shp, shp],
        compiler_params=_params("parallel", "arbitrary"), name=name)(proj, proj, proj, gq2, gk2)


def _qk_norm_bwd(name, dqn, dkn, proj, gq2, gk2, n_pairs):
    s = proj.shape[0]
    ts = min(ROW_TILE, s)
    pb = PAD // ts
    c0 = 3 * n_pairs
    n_r = s // ts

    def body(dqn_ref, dkn_ref, q_ref, k_ref, gq_ref, gk_ref, dq_ref, dk_ref, dgq_ref, dgk_ref):
        first = jnp.logical_and(pl.program_id(0) == 0, pl.program_id(1) == 0)
        last = jnp.logical_and(pl.program_id(0) == n_pairs - 1, pl.program_id(1) == n_r - 1)

        @pl.when(first)
        def _():
            dgq_ref[...] = jnp.zeros_like(dgq_ref)
            dgk_ref[...] = jnp.zeros_like(dgk_ref)

        gm = _group_mean_matrix()
        for dn_ref, x_ref, g_ref, dx_ref, dg_ref in ((dqn_ref, q_ref, gq_ref, dq_ref, dgq_ref),
                                                     (dkn_ref, k_ref, gk_ref, dk_ref, dgk_ref)):
            xv = x_ref[...]
            r = lax.rsqrt(_split_dot(xv * xv, gm, 2) + EPS)
            xhat = xv * r
            dn = dn_ref[...]
            tg = dn * g_ref[...]
            md = _split_dot(tg * xhat, gm, 2)
            dx_ref[...] = (r * (tg - xhat * md)).astype(BF16)
            dg_ref[...] += jnp.sum(dn * xhat, axis=0, keepdims=True)

        @pl.when(last)
        def _():
            for dg_ref in (dgq_ref, dgk_ref):
                gv = dg_ref[...]
                dg_ref[...] = gv + pltpu.roll(gv, HEAD_DIM, axis=1)

    def x_spec(off):
        return pl.BlockSpec((ts, LANES), lambda p, i: (i, c0 + off * n_pairs + p))

    dn_q = pl.BlockSpec((ts, LANES), lambda p, i: (i, p))
    dn_k = pl.BlockSpec((ts, LANES), lambda p, i: (i + pb, p))
    gspec = pl.BlockSpec((1, LANES), lambda p, i: (0, 0))
    ospec = pl.BlockSpec((ts, LANES), lambda p, i: (i, p))
    shp = jax.ShapeDtypeStruct((s, n_pairs * LANES), BF16)
    vec = jax.ShapeDtypeStruct((1, LANES), F32)
    return pl.pallas_call(
        body, grid=(n_pairs, n_r),
        in_specs=[dn_q, dn_k, x_spec(0), x_spec(1), gspec, gspec],
        out_specs=[ospec, ospec, gspec, gspec], out_shape=[shp, shp, vec, vec],
        compiler_params=_params("arbitrary", "arbitrary"), name=name)(dqn, dkn, proj, proj, gq2, gk2)


def _head_blocks(x2):
    lane = lax.broadcasted_iota(jnp.int32, x2.shape, 1)
    zero = jnp.zeros_like(x2)
    return jnp.concatenate([jnp.where(lane < HEAD_DIM, x2, zero), jnp.where(lane >= HEAD_DIM, x2, zero)], axis=0)


def _head_diag(r):
    lane = lax.broadcasted_iota(jnp.int32, (CHUNK, LANES), 1)
    return jnp.where(lane < HEAD_DIM, r[:CHUNK], r[CHUNK:])


def _ca_probs(kw, qb, bias_t, n):
    sc = lax.dot_general(kw, qb, NT, preferred_element_type=F32) * SCALE + bias_t
    jpos = lax.broadcasted_iota(jnp.int32, (BAND, LANES), 0)
    sc = jnp.where(n * CHUNK + jpos >= PAD, sc, NEG_INF)
    p = jnp.exp(sc - jnp.max(sc, axis=0, keepdims=True))
    return p * (1.0 / jnp.sum(p, axis=0, keepdims=True))


def _ca_fwd(name, qn, knp, vp, bias_tt, n_pairs, mixed, gather=()):
    sp_rows = knp.shape[0]
    s = sp_rows - PAD
    t = min(CA_TILE, s)
    cpb = t // CHUNK
    pb = PAD // t

    n_g = len(gather)
    grid = (n_pairs, s // t)

    def body(*refs):
        q_ref, k_ref, v_ref, b_ref = refs[:4]
        o_ref = refs[5 + n_g]
        g_refs, sems = refs[6 + n_g:6 + 2 * n_g], refs[6 + 2 * n_g:]

        def compute():
            qi = pl.program_id(1)
            wss = [pl.multiple_of((qi * cpb + ci) * CHUNK, CHUNK) for ci in range(cpb)]
            ps = [_ca_probs(k_ref[pl.ds(wss[ci], BAND), :], _head_blocks(q_ref[ci * CHUNK:(ci + 1) * CHUNK, :]),
                            b_ref[...], qi * cpb + ci) for ci in range(cpb)]
            for ci in range(cpb):
                r = lax.dot_general(ps[ci].astype(BF16), v_ref[pl.ds(wss[ci], BAND), :], TN,
                                    preferred_element_type=F32)
                o_ref[ci * CHUNK:(ci + 1) * CHUNK, :] = _head_diag(r)

        _ride_gather(grid, g_refs, sems, compute)

    full = pl.BlockSpec((sp_rows, LANES), lambda p, i: (0, p))
    outs = pl.pallas_call(
        body, grid=grid,
        in_specs=[pl.BlockSpec((t, LANES), lambda p, i: (i + pb, p)), full, full,
                  pl.BlockSpec((None, BAND, LANES), lambda p, i: (p, 0, 0)), pl.BlockSpec(memory_space=pl.ANY)]
        + [HBM_SPEC] * n_g,
        out_specs=[pl.BlockSpec((t, LANES), lambda p, i: (i, n_pairs + p))] + [HBM_SPEC] * n_g,
        out_shape=[jax.ShapeDtypeStruct(mixed.shape, mixed.dtype)]
        + [jax.ShapeDtypeStruct(g.shape, g.dtype) for g in gather],
        input_output_aliases={4: 0, **{5 + i: 1 + i for i in range(n_g)}},
        scratch_shapes=_gather_sems(n_g) if n_g else [],
        compiler_params=_params("arbitrary", "arbitrary"), name=name)(qn, knp, vp, bias_tt, mixed, *gather)
    return outs[0], list(outs[1:])


def _ca_bwd(name, qn, knp, vp, bias_tt, dmixed, n_pairs, col0, exchange=()):
    sp_rows = knp.shape[0]
    s = sp_rows - PAD
    t = min(CA_TILE, s)
    cpb = t // CHUNK
    pb = PAD // t
    n_x = len(exchange)
    grid = (n_pairs, s // t)

    def body(*refs):
        x_in, x_out, sems = refs[5:5 + n_x], refs[9 + n_x:9 + 2 * n_x], refs[9 + 2 * n_x:]
        if n_x:
            @pl.when(_grid_edge(grid, False))
            def _():
                for cp in _exchange_copies(x_in, x_out, *sems):
                    cp.start()

        _ca_bwd_body(*refs[:5], *refs[5 + n_x:9 + n_x])
        if n_x:
            @pl.when(_grid_edge(grid, True))
            def _():
                for cp in _exchange_copies(x_in, x_out, *sems):
                    cp.wait()

    def _ca_bwd_body(q_ref, k_ref, v_ref, b_ref, do_ref, dq_ref, dk_ref, dv_ref, db_ref):
        qi = pl.program_id(1)

        @pl.when(qi == 0)
        def _():
            dk_ref[...] = jnp.zeros_like(dk_ref)
            dv_ref[...] = jnp.zeros_like(dv_ref)
            db_ref[...] = jnp.zeros_like(db_ref)

        wss = [pl.multiple_of((qi * cpb + ci) * CHUNK, CHUNK) for ci in range(cpb)]
        kws = [k_ref[pl.ds(ws, BAND), :] for ws in wss]
        qbs = [_head_blocks(q_ref[ci * CHUNK:(ci + 1) * CHUNK, :]) for ci in range(cpb)]
        dbs = [_head_blocks(do_ref[ci * CHUNK:(ci + 1) * CHUNK, :].astype(BF16)) for ci in range(cpb)]
        ps = [_ca_probs(kws[ci], qbs[ci], b_ref[...], qi * cpb + ci) for ci in range(cpb)]
        dps = [lax.dot_general(v_ref[pl.ds(wss[ci], BAND), :], dbs[ci], NT, preferred_element_type=F32)
               for ci in range(cpb)]
        dss = [p * (dp - jnp.sum(p * dp, axis=0, keepdims=True)) for p, dp in zip(ps, dps)]
        for ci in range(cpb):
            dsb = dss[ci].astype(BF16)
            dv_ref[pl.ds(wss[ci], BAND), :] += jnp.dot(ps[ci].astype(BF16), dbs[ci], preferred_element_type=F32)
            dk_ref[pl.ds(wss[ci], BAND), :] += jnp.dot(dsb, qbs[ci], preferred_element_type=F32) * SCALE
            r = lax.dot_general(dsb, kws[ci], TN, preferred_element_type=F32)
            dq_ref[ci * CHUNK:(ci + 1) * CHUNK, :] = _head_diag(r) * SCALE
        total = dss[0]
        for ci in range(1, cpb):
            total = total + dss[ci]
        db_ref[...] += total

    full_in = pl.BlockSpec((sp_rows, LANES), lambda p, i: (0, p))
    btile = pl.BlockSpec((None, BAND, LANES), lambda p, i: (p, 0, 0))
    pad_shape = jax.ShapeDtypeStruct((sp_rows, n_pairs * LANES), F32)
    outs = pl.pallas_call(
        body, grid=grid,
        in_specs=[pl.BlockSpec((t, LANES), lambda p, i: (i + pb, p)), full_in, full_in, btile,
                  pl.BlockSpec((t, LANES), lambda p, i: (i, col0 + p))] + [HBM_SPEC] * n_x,
        out_specs=[pl.BlockSpec((t, LANES), lambda p, i: (i, p)), full_in, full_in, btile] + [HBM_SPEC] * n_x,
        out_shape=[jax.ShapeDtypeStruct((s, n_pairs * LANES), F32), pad_shape, pad_shape,
                   jax.ShapeDtypeStruct(bias_tt.shape, F32)]
        + [jax.ShapeDtypeStruct((3,) + p.shape[1:], p.dtype) for p in exchange],
        scratch_shapes=_exchange_sems(n_x) if n_x else [],
        compiler_params=_params("arbitrary", "arbitrary"), name=name)(qn, knp, vp, bias_tt, dmixed, *exchange)
    return outs[0], outs[1], outs[2], outs[3], list(outs[4:])


def _pair_transposed(tile):
    n_h = tile.shape[0]
    return jnp.transpose(tile.reshape(n_h // 2, 2, CHUNK, BAND), (0, 3, 1, 2)).reshape(n_h // 2, BAND, LANES)


def _query_major(tile_t):
    n_p = tile_t.shape[0]
    return jnp.transpose(tile_t.reshape(n_p, BAND, 2, CHUNK), (3, 0, 2, 1)).reshape(CHUNK, 2 * n_p, BAND)


def _bias_tile(rb):
    n_h = rb.shape[0]
    far = jnp.broadcast_to(rb[:, 2 * REL_CLIP:], (n_h, PAD + CHUNK - REL_CLIP))
    near = rb[:, REL_CLIP - (CHUNK - 1):2 * REL_CLIP][:, ::-1]
    ext = jnp.concatenate([far, near], axis=1)
    return jnp.stack([ext[:, CHUNK - 1 - i:CHUNK - 1 - i + BAND] for i in range(CHUNK)], axis=1)


def _rel_bias_grad(name, db_t):
    _, n_h, _ = db_t.shape
    n_out = 3 * LANES
    assert CHUNK == 64 and REL_CLIP == 128 and LEFT_CHUNKS == 8

    def body(z_ref, o_ref):
        a_idx = lax.broadcasted_iota(jnp.int32, (LANES, n_out), 0)
        r_idx = lax.broadcasted_iota(jnp.int32, (LANES, n_out), 1)
        out = jnp.zeros((n_h, n_out), F32)
        for mm, base in ((0, 191), (1, 255), (2, 319)):
            b = LEFT_CHUNKS - mm
            acc = jnp.zeros((n_h, LANES), F32)
            for i in range(CHUNK):
                row = z_ref[i, :, b * CHUNK:(b + 1) * CHUNK]
                rowp = jnp.concatenate([row, jnp.zeros((n_h, LANES - CHUNK), F32)], axis=1)
                acc = acc + pltpu.roll(rowp, CHUNK - 1 - i, axis=1)
            place = (r_idx == jnp.minimum(base - a_idx, 2 * REL_CLIP)).astype(BF16)
            out = out + _split_dot(acc, place, 3)
        far = jnp.zeros((n_h, 1), F32)
        for i in range(CHUNK):
            far = far + jnp.sum(z_ref[i, :, 0:(LEFT_CHUNKS - 2) * CHUNK], axis=1, keepdims=True)
        lane = lax.broadcasted_iota(jnp.int32, (n_h, n_out), 1)
        o_ref[...] = out + jnp.where(lane == 2 * REL_CLIP, far, 0.0)

    return pl.pallas_call(
        body, out_shape=jax.ShapeDtypeStruct((n_h, n_out), F32),
        in_specs=[pl.BlockSpec(memory_space=pltpu.VMEM)], out_specs=pl.BlockSpec(memory_space=pltpu.VMEM),
        name=name)(db_t)


def _mesh_pos():
    return lax.axis_index("x"), lax.axis_index("y"), lax.axis_index("c")


def _other_chips(x, y):
    return [(1 - x, y), (x, 1 - y), (1 - x, 1 - y)]


def _all_gather_small(name, blk):
    m_per, n = blk.shape

    def body(x_ref, out_ref, send_sems, recv_sems, local_sem):
        x, y, c = _mesh_pos()
        me, sibling = (x, y, c), (x, y, 1 - c)
        chips = _other_chips(x, y)

        def rows(px, py, pc):
            return out_ref.at[pl.ds((4 * px + 2 * py + pc) * m_per, m_per), :]

        def copy(k, block, to, src=None):
            return pltpu.make_async_remote_copy(
                src_ref=rows(*block) if src is None else src, dst_ref=rows(*block),
                send_sem=send_sems.at[k], recv_sem=recv_sems.at[k], device_id=to, device_id_type=MESH)

        mine = pltpu.make_async_copy(x_ref, rows(*me), local_sem)
        mine.start()
        first = [copy(0, me, sibling, src=x_ref)]
        first += [copy(1 + j, me, (*chip, c), src=x_ref) for j, chip in enumerate(chips)]
        for cp in first:
            cp.start()
        passed = [copy(4 + j, (*chip, c), sibling) for j, chip in enumerate(chips)]
        for j, chip in enumerate(chips):
            copy(1 + j, (*chip, c), me).wait_recv()
            passed[j].start()
        copy(0, sibling, me).wait_recv()
        for j, chip in enumerate(chips):
            copy(4 + j, (*chip, 1 - c), me).wait_recv()
        for cp in first + passed:
            cp.wait_send()
        mine.wait()

    return pl.pallas_call(
        body, out_shape=jax.ShapeDtypeStruct((N_DEV * m_per, n), blk.dtype),
        in_specs=[pl.BlockSpec(memory_space=pltpu.VMEM)], out_specs=pl.BlockSpec(memory_space=pltpu.VMEM),
        scratch_shapes=[pltpu.SemaphoreType.DMA((7,)), pltpu.SemaphoreType.DMA((7,)), pltpu.SemaphoreType.DMA],
        compiler_params=pltpu.CompilerParams(vmem_limit_bytes=VMEM_LIMIT), name=name)(blk)


HBM_SPEC = pl.BlockSpec(memory_space=pltpu.HBM)


def _gather_copies(bufs, send_sems, recv_sems, phase):
    x, y, c = _mesh_pos()
    my_j = 2 * x + y
    starts, lands = [], []
    for t, buf in enumerate(bufs):
        rh = buf.shape[2] // 2
        for k, (px, py) in enumerate(_other_chips(x, y)):
            their_j = 2 * px + py
            if phase == 0:
                src, to = buf.at[0, my_j, pl.ds(c * rh, rh), :], (px, py, c)
                land = buf.at[0, their_j, pl.ds(c * rh, rh), :]
            else:
                src, to = buf.at[0, their_j, pl.ds(c * rh, rh), :], (x, y, 1 - c)
                land = buf.at[0, their_j, pl.ds((1 - c) * rh, rh), :]
            sems = dict(send_sem=send_sems.at[phase, t, k], recv_sem=recv_sems.at[phase, t, k],
                        device_id=to, device_id_type=MESH)
            starts.append(pltpu.make_async_remote_copy(src_ref=src, dst_ref=src, **sems))
            lands.append(pltpu.make_async_remote_copy(src_ref=src, dst_ref=land, **sems))
    return starts, lands


def _gather_sems(n):
    return [pltpu.SemaphoreType.DMA((2, n, 3)), pltpu.SemaphoreType.DMA((2, n, 3))]


def _all_gather_weights(name, bufs, phases):
    n = len(bufs)

    def body(*refs):
        outs = refs[n:2 * n]
        send_sems, recv_sems = refs[2 * n:]
        for phase in phases:
            starts, lands = _gather_copies(outs, send_sems, recv_sems, phase)
            for cp in starts:
                cp.start()
            for cp, ld in zip(starts, lands):
                cp.wait_send()
                ld.wait_recv()

    return pl.pallas_call(
        body, out_shape=[jax.ShapeDtypeStruct(b.shape, b.dtype) for b in bufs],
        in_specs=[HBM_SPEC] * n, out_specs=[HBM_SPEC] * n, input_output_aliases={t: t for t in range(n)},
        scratch_shapes=_gather_sems(n), name=name)(*bufs)


def _gather_buffer(shard, chip):
    r, c = shard.shape
    return lax.dynamic_update_slice(jnp.zeros((1, N_CHIPS, r, c), BF16), shard.astype(BF16)[None, None],
                                    (0, chip, 0, 0))


def _sibling_copies(srcs, outs, send_sems, recv_sems):
    x, y, c = _mesh_pos()
    cps = []
    for t, (src, out) in enumerate(zip(srcs, outs)):
        rh = src.shape[1] // 2
        cps.append(pltpu.make_async_remote_copy(
            src_ref=src.at[:, pl.ds((1 - c) * rh, rh), :], dst_ref=out,
            send_sem=send_sems.at[t], recv_sem=recv_sems.at[t], device_id=(x, y, 1 - c), device_id_type=MESH))
    return cps


def _exchange_sibling_halves(name, grads):
    n = len(grads)

    def body(*refs):
        cps = _sibling_copies(refs[:n], refs[n:2 * n], *refs[2 * n:])
        for cp in cps:
            cp.start()
        for cp in cps:
            cp.wait()

    out_shape = [jax.ShapeDtypeStruct((g.shape[0], g.shape[1] // 2, g.shape[2]), g.dtype) for g in grads]
    return pl.pallas_call(
        body, out_shape=out_shape, in_specs=[HBM_SPEC] * n, out_specs=[HBM_SPEC] * n,
        scratch_shapes=[pltpu.SemaphoreType.DMA((n,)), pltpu.SemaphoreType.DMA((n,))],
        name=name)(*grads)


def _add_halves(name, g, recv, c_idx):
    nb, r, c = g.shape
    rh = r // 2
    tr = rh
    for cand in (512, 256, 128, 64):
        if rh % cand == 0:
            tr = cand
            break
    g4 = g.reshape(nb, 2, rh, c)

    def body(c_ref, g_ref, r_ref, o_ref):
        o_ref[...] = (g_ref[...] + r_ref[...]).astype(BF16)

    grid_spec = pltpu.PrefetchScalarGridSpec(
        num_scalar_prefetch=1, grid=(nb, rh // tr),
        in_specs=[pl.BlockSpec((None, None, tr, c), lambda j, i, cr: (j, cr[0], i, 0)),
                  pl.BlockSpec((None, tr, c), lambda j, i, cr: (j, i, 0))],
        out_specs=pl.BlockSpec((None, tr, c), lambda j, i, cr: (j, i, 0)))
    return pl.pallas_call(
        body, grid_spec=grid_spec, out_shape=jax.ShapeDtypeStruct((nb, rh, c), BF16),
        compiler_params=_params("parallel", "parallel"), name=name)(c_idx, g4, recv)


def _exchange_copies(srcs, outs, send_sems, recv_sems):
    x, y, c = _mesh_pos()
    return [pltpu.make_async_remote_copy(
        src_ref=srcs[t].at[2 * px + py], dst_ref=outs[t].at[k], send_sem=send_sems.at[t, k],
        recv_sem=recv_sems.at[t, k], device_id=(px, py, c), device_id_type=MESH)
        for t in range(len(srcs)) for k, (px, py) in enumerate(_other_chips(x, y))]


def _exchange_sems(n):
    return [pltpu.SemaphoreType.DMA((n, 3)), pltpu.SemaphoreType.DMA((n, 3))]


def _sum_chips(name, own_parts, recvs, pos_idx):
    n_layers = len(own_parts)
    _, rh, c = recvs[0].shape
    tr = rh
    for cand in (512, 256, 128, 64):
        if rh % cand == 0:
            tr = cand
            break
    nb = rh // tr

    def body(pos_ref, *refs):
        own_refs, r_refs, o_ref = refs[:n_layers], refs[n_layers:2 * n_layers], refs[2 * n_layers]
        for l_i in range(n_layers):
            @pl.when(pl.program_id(0) == l_i)
            def _():
                acc = own_refs[l_i][...].astype(F32)
                for k in range(3):
                    acc = acc + r_refs[l_i][k].astype(F32)
                o_ref[...] = acc

    grid_spec = pltpu.PrefetchScalarGridSpec(
        num_scalar_prefetch=1, grid=(n_layers, nb),
        in_specs=[pl.BlockSpec((None, tr, c), lambda l, i, pr: (pr[0], i, 0)) for _ in range(n_layers)]
        + [pl.BlockSpec((3, tr, c), lambda l, i, pr: (0, i, 0)) for _ in range(n_layers)],
        out_specs=pl.BlockSpec((None, tr, c), lambda l, i, pr: (l, pr[1] * nb + i, 0)))
    return pl.pallas_call(
        body, grid_spec=grid_spec, out_shape=jax.ShapeDtypeStruct((n_layers, 2 * rh, c), F32),
        compiler_params=_params("parallel", "parallel"), name=name)(pos_idx, *own_parts, *recvs)


def _share_with_sibling(name, grads):
    n = len(grads)

    def body(*refs):
        outs = refs[n:2 * n]
        send_sems, recv_sems = refs[2 * n:]
        x, y, c = _mesh_pos()
        cps, landed = [], []
        for t in range(n):
            rh = grads[t].shape[1] // 2
            mine = outs[t].at[:, pl.ds(c * rh, rh), :]
            theirs = outs[t].at[:, pl.ds((1 - c) * rh, rh), :]
            cps.append(pltpu.make_async_remote_copy(
                src_ref=mine, dst_ref=mine, send_sem=send_sems.at[t], recv_sem=recv_sems.at[t],
                device_id=(x, y, 1 - c), device_id_type=MESH))
            landed.append(pltpu.make_async_remote_copy(
                src_ref=mine, dst_ref=theirs, send_sem=send_sems.at[t], recv_sem=recv_sems.at[t],
                device_id=(x, y, 1 - c), device_id_type=MESH))
        for cp in cps:
            cp.start()
        for cp, ld in zip(cps, landed):
            cp.wait_send()
            ld.wait_recv()

    out_shape = [jax.ShapeDtypeStruct(g.shape, g.dtype) for g in grads]
    return pl.pallas_call(
        body, out_shape=out_shape, in_specs=[HBM_SPEC] * n, out_specs=[HBM_SPEC] * n,
        input_output_aliases={t: t for t in range(n)},
        scratch_shapes=[pltpu.SemaphoreType.DMA((n,)), pltpu.SemaphoreType.DMA((n,))],
        name=name)(*grads)


def _row(v):
    return v.reshape(1, -1)


def kernel(x, c, g_norm1, w_in, g_q, g_k, rel_bias, w_o, g_norm2, w1, w2, w_ada, b_ada, loss_target, m_g_norm1, m_w_in, m_g_q, m_g_k, m_rel_bias, m_w_o, m_g_norm2, m_w1, m_w2, m_w_ada, m_b_ada, v_g_norm1, v_w_in, v_g_q, v_g_k, v_rel_bias, v_w_o, v_g_norm2, v_w1, v_w2, v_w_ada, v_b_ada):
    _, s, d = x.shape
    n_layers = g_norm1.shape[0]
    n_pairs = (d // 2) // LANES
    n_ca_heads = rel_bias.shape[1]
    n_rel = rel_bias.shape[2]
    assert n_rel == 2 * REL_CLIP + 1 and g_q.shape[1] == HEAD_DIM and n_ca_heads == 2 * n_pairs
    ada_c = w_ada.shape[2]

    ax, ay, ac = _mesh_pos()
    chip = 2 * ax + ay
    dev = 4 * ax + 2 * ay + ac
    c_idx = jnp.reshape(ac, (1,)).astype(jnp.int32)

    c_act_all = _all_gather_small("ag_c", _silu_rows("silu_c", c)).reshape(N_DEV, 8, d)[:, 0, :]
    mod_parts = []
    for l in range(n_layers):
        bias = lax.dynamic_slice_in_dim(b_ada[l], chip * ada_c, ada_c).reshape(1, ada_c)
        mod_parts.append(_ada_mod(f"ada_mod{l}", c_act_all, w_ada[l], bias))
    mod_all = _all_gather_small("ag_mod", jnp.concatenate(mod_parts, axis=1))
    mod_all = mod_all.reshape(N_CHIPS, 2, N_DEV, n_layers, ada_c)[:, 0]
    mod_all = jnp.transpose(mod_all, (1, 2, 0, 3)).reshape(N_DEV, n_layers, N_CHIPS * ada_c)
    mod = lax.dynamic_index_in_dim(mod_all, dev, axis=0, keepdims=False)

    gathered = [[_gather_buffer(w[l], chip) for w in (w_in, w_o, w1, w2)] for l in range(n_layers)]
    gathered[0][:1] = _all_gather_weights("ag_w_in_0", gathered[0][:1], (0, 1))

    xs = x[0]
    saved = []
    for l in range(n_layers):
        sh1, sc1, gt1, sh2, sc2, gt2 = [_row(mod[l, i * d:(i + 1) * d]) for i in range(6)]
        gq2 = _row(jnp.tile(g_q[l], 2))
        gk2 = _row(jnp.tile(g_k[l], 2))
        bias_t = _pair_transposed(_bias_tile(rel_bias[l]))
        h1 = _norm_mod(f"norm1_{l}", xs, _row(g_norm1[l]), sc1, sh1)
        proj = _mm_nn_w(f"proj_{l}", h1, gathered[l][0], 0, True, [F32])[0]
        o_sb, landed = _sb_fwd(f"sb_fwd_{l}", proj, n_pairs, gather=gathered[l][1:] if l == 0 else ())
        if landed:
            gathered[l][1:] = _all_gather_weights(f"ag_rest_{l}", landed, (1,))
        wg_in, wg_o, wg_1, wg_2 = gathered[l]
        nxt = gathered[l + 1] if l + 1 < n_layers else []
        qn, knp, vp = _qk_norm(f"qk_norm_{l}", proj, gq2, gk2, n_pairs)
        mixed, landed = _ca_fwd(f"ca_fwd_{l}", qn, knp, vp, bias_t, n_pairs, o_sb, gather=nxt[:2])

        def res_epi(acc, res, gt):
            return res + gt * acc, acc

        def res_specs(tm, tn):
            return (pl.BlockSpec((tm, tn), lambda i, j, kk: (i, j)), pl.BlockSpec((1, tn), lambda i, j, kk: (0, j)))

        x1, m1 = _mm_nn_w(f"attn_out_{l}", mixed, wg_o, 0, False, [F32, BF16], epi=res_epi,
                          extras=(xs, gt1), extra_specs_fn=res_specs)
        h2 = _norm_mod(f"norm2_{l}", x1, _row(g_norm2[l]), sc2, sh2)

        def act_epi(acc):
            r = jnp.maximum(acc, 0.0)
            return acc, r * r

        if l + 1 < n_layers:
            (u, a), landed_1 = _mm_nn_w(f"mlp_up_{l}", h2, wg_1, 0, True, [BF16, BF16], epi=act_epi, gather=nxt[2:3])
            (x2, m2), landed_2 = _mm_nn_w(f"mlp_down_{l}", a, wg_2, 0, False, [F32, BF16], epi=res_epi,
                                          extras=(x1, gt2), extra_specs_fn=res_specs, gather=nxt[3:])
            gathered[l + 1] = _all_gather_weights(f"ag_weights_{l + 1}", landed + landed_1 + landed_2, (1,))
        else:
            u, a = _mm_nn_w(f"mlp_up_{l}", h2, wg_1, 0, True, [BF16, BF16], epi=act_epi)
            x2, m2 = _mm_nn_w(f"mlp_down_{l}", a, wg_2, 0, False, [F32, BF16], epi=res_epi,
                              extras=(x1, gt2), extra_specs_fn=res_specs)
        saved.append(dict(x0=xs, h1=h1, proj=proj, o_sb=mixed, qn=qn, knp=knp, vp=vp, bias_t=bias_t, mixed=mixed,
                          m1=m1, x1=x1, h2=h2, u=u, a=a, m2=m2, gq2=gq2, gk2=gk2,
                          sc1=sc1, gt1=gt1, sc2=sc2, gt2=gt2))
        xs = x2

    dx, sq = _loss_grad("loss", xs, loss_target[0])
    loss_part = (0.5 * jnp.sum(sq) / d).reshape(1)

    g_in, g_o, g_1, g_2 = [None] * n_layers, [None] * n_layers, [None] * n_layers, [None] * n_layers
    dmod, dg1, dg2, dgq, dgk, drel = [], [], [], [], [], []
    partial, chip_sum, from_chip = {}, {}, {}

    def reduce_on_chip(tag, keys):
        grads = [partial[k] for k in keys]
        for k, g, r in zip(keys, grads, _exchange_sibling_halves(f"rs_sibling_{tag}", grads)):
            chip_sum[k] = _add_halves(f"rs_add_{k[0]}_{k[1]}", g, r, c_idx)

    for l in reversed(range(n_layers)):
        sv = saved[l]
        wg_in, wg_o, wg_1, wg_2 = gathered[l]
        dm2, dgt2 = _gate_bwd(f"gate2_bwd_{l}", dx, sv["m2"], sv["gt2"])

        def act_bwd_epi(acc, u_t):
            return (acc * (2.0 * jnp.maximum(u_t.astype(F32), 0.0)),)

        def tile_specs(tm, tn):
            return (pl.BlockSpec((tm, tn), lambda i, j, kk: (i, j)),)

        above = sorted(k for k in partial if k not in chip_sum) if l == 0 else []
        du = _mm_nt_w(f"mlp_down_bwd_{l}", dm2, wg_2, 0, False, BF16, epi=act_bwd_epi, extras=(sv["u"],),
                      extra_specs_fn=tile_specs, riders=[_sibling_rider([partial[k] for k in above])] if above else ())
        if above:
            du, (from_sibling,) = du
            for k, r in zip(above, from_sibling):
                chip_sum[k] = _add_halves(f"rs_add_{k[0]}_{k[1]}", partial[k], r, c_idx)
        g_2[l] = _mm_tn(f"w2_grad_{l}", sv["a"], dm2, False)
        dh2 = _mm_nt_w(f"mlp_up_bwd_{l}", du, wg_1, 0, True, F32)
        g_1[l] = _mm_tn(f"w1_grad_{l}", sv["h2"], du, True)
        dx1, dsh2, dsc2, dgn2 = _norm_mod_bwd(f"norm2_bwd_{l}", dh2, sv["x1"], _row(g_norm2[l]), sv["sc2"], dx)
        dm1, dgt1 = _gate_bwd(f"gate1_bwd_{l}", dx1, sv["m1"], sv["gt1"])
        dmixed = _mm_nt_w(f"attn_out_bwd_{l}", dm1, wg_o, 0, False, F32)
        g_o[l] = _mm_tn(f"wo_grad_{l}", sv["mixed"], dm1, False)
        partial.update({(1, l): g_o[l], (2, l): g_1[l], (3, l): g_2[l]})
        ride_sb, ride_ca = [], []
        if l == 0:
            early = sorted(partial)
            reduce_on_chip("early", [k for k in early if k not in chip_sum])
            ride_ca = [k for k in early if k[0] == 0 or k == (3, 0)]
            ride_sb = [k for k in early if k not in ride_ca]
        dq_sb, dk_sb, dv_sb, landed = _sb_bwd(f"sb_bwd_{l}", sv["proj"], sv["o_sb"], dmixed, n_pairs,
                                              exchange=[chip_sum[k] for k in ride_sb])
        from_chip.update(zip(ride_sb, landed))
        dqn, dknp, dvp, dbias_t, landed = _ca_bwd(f"ca_bwd_{l}", sv["qn"], sv["knp"], sv["vp"], sv["bias_t"], dmixed,
                                                  n_pairs, n_pairs, exchange=[chip_sum[k] for k in ride_ca])
        from_chip.update(zip(ride_ca, landed))
        dq_ca, dk_ca, dgq_l, dgk_l = _qk_norm_bwd(f"qk_norm_bwd_{l}", dqn, dknp, sv["proj"], sv["gq2"], sv["gk2"],
                                                  n_pairs)
        drel_l = _rel_bias_grad(f"rel_bias_grad_{l}", _query_major(dbias_t))[:, :n_rel]
        dproj = jnp.concatenate([dq_sb.astype(BF16), dk_sb.astype(BF16), dv_sb.astype(BF16), dq_ca, dk_ca,
                                 dvp[PAD:].astype(BF16)], axis=1)
        g_in[l] = _mm_tn(f"win_grad_{l}", sv["h1"], dproj, True)
        partial[(0, l)] = g_in[l]
        if l == 0:
            reduce_on_chip("late", [(0, l)])
            dh1, (landed,) = _mm_nt_w(f"proj_bwd_{l}", dproj, wg_in, 0, True, F32,
                                      riders=[_exchange_rider([chip_sum[(0, l)]])])
            from_chip[(0, l)] = landed[0]
        else:
            dh1 = _mm_nt_w(f"proj_bwd_{l}", dproj, wg_in, 0, True, F32)
        dx, dsh1, dsc1, dgn1 = _norm_mod_bwd(f"norm1_bwd_{l}", dh1, sv["x0"], _row(g_norm1[l]), sv["sc1"], dx1)
        dmod.insert(0, jnp.concatenate([dsh1, dsc1, dgt1, dsh2, dsc2, dgt2], axis=1)[0])
        dg1.insert(0, dgn1[0])
        dg2.insert(0, dgn2[0])
        dgq.insert(0, dgq_l[0, :HEAD_DIM])
        dgk.insert(0, dgk_l[0, :HEAD_DIM])
        drel.insert(0, drel_l.reshape(-1))
    grad_x = dx[None]

    small = jnp.concatenate([jnp.concatenate(dmod), jnp.concatenate(dg1), jnp.concatenate(dg2),
                             jnp.concatenate(dgq), jnp.concatenate(dgk), jnp.concatenate(drel), loss_part])
    n_small = small.shape[0]
    n_pack = -(-n_small // (8 * LANES)) * (8 * LANES)
    small = jnp.pad(small, (0, n_pack - n_small)).reshape(8, n_pack // 8)
    small_all = _all_gather_small("ag_small", small).reshape(N_DEV, 8, n_pack // 8)
    small_sum = _sum_rows8("sum_small", small_all).reshape(-1)
    dmod_all = small_all.reshape(N_DEV, n_pack)[:, :n_layers * 6 * d].reshape(N_DEV, n_layers, 6 * d)

    off = 0

    def take(n, shape):
        nonlocal off
        out = small_sum[off:off + n].reshape(shape)
        off += n
        return out

    grad_b_ada = take(n_layers * 6 * d, (n_layers, 6 * d))
    grad_g_norm1 = take(n_layers * d, (n_layers, d))
    grad_g_norm2 = take(n_layers * d, (n_layers, d))
    grad_g_q = take(n_layers * HEAD_DIM, (n_layers, HEAD_DIM))
    grad_g_k = take(n_layers * HEAD_DIM, (n_layers, HEAD_DIM))
    grad_rel_bias = take(n_layers * n_ca_heads * n_rel, (n_layers, n_ca_heads, n_rel))
    loss = take(1, ())

    c_act_t = jnp.transpose(c_act_all)
    grad_w_ada = jnp.stack([
        _outer_sum(f"wada_grad_{l}", c_act_t,
                   lax.dynamic_slice_in_dim(dmod_all[:, l, :], chip * ada_c, ada_c, axis=1))
        for l in range(n_layers)])

    assert all(k in from_chip for k in partial)
    pos_idx = jnp.stack([chip, ac]).astype(jnp.int32)
    reduced = [_sum_chips(f"rs_sum_{w_i}", [chip_sum[(w_i, l)] for l in range(n_layers)],
                          [from_chip[(w_i, l)] for l in range(n_layers)], pos_idx) for w_i in range(4)]
    grad_w_in, grad_w_o, grad_w1, grad_w2 = _share_with_sibling("rs_share", reduced)

    def adam_big(name, w, g, m, v):
        shp = w.shape
        outs = _adamw(name, w.reshape(-1, shp[-1]), g.reshape(-1, shp[-1]), m.reshape(-1, shp[-1]),
                      v.reshape(-1, shp[-1]))
        return [o.reshape(shp) for o in outs]

    def pack(arrs):
        flat = jnp.concatenate([a.reshape(-1) for a in arrs])
        n = flat.shape[0]
        n_p = -(-n // (8 * LANES)) * (8 * LANES)
        return jnp.pad(flat, (0, n_p - n), constant_values=1.0).reshape(8, n_p // 8)

    small_w = [g_norm1, g_q, g_k, rel_bias, g_norm2, b_ada]
    small_g = [grad_g_norm1, grad_g_q, grad_g_k, grad_rel_bias, grad_g_norm2, grad_b_ada]
    small_m = [m_g_norm1, m_g_q, m_g_k, m_rel_bias, m_g_norm2, m_b_ada]
    small_v = [v_g_norm1, v_g_q, v_g_k, v_rel_bias, v_g_norm2, v_b_ada]
    packed = _adamw("adamw_small", pack(small_w), pack(small_g), pack(small_m), pack(small_v))

    def unpack(p):
        flat = p.reshape(-1)
        res, o = [], 0
        for a in small_w:
            res.append(flat[o:o + a.size].reshape(a.shape))
            o += a.size
        return res

    sd, sm, sv_ = unpack(packed[0]), unpack(packed[1]), unpack(packed[2])
    big = {
        "w_in": adam_big("adamw_w_in", w_in, grad_w_in, m_w_in, v_w_in),
        "w_o": adam_big("adamw_w_o", w_o, grad_w_o, m_w_o, v_w_o),
        "w1": adam_big("adamw_w1", w1, grad_w1, m_w1, v_w1),
        "w2": adam_big("adamw_w2", w2, grad_w2, m_w2, v_w2),
        "w_ada": adam_big("adamw_w_ada", w_ada, grad_w_ada, m_w_ada, v_w_ada),
    }

    def ordered(kind):
        sm_list = (sd, sm, sv_)[kind]
        return [sm_list[0], big["w_in"][kind], sm_list[1], sm_list[2], sm_list[3], big["w_o"][kind], sm_list[4],
                big["w1"][kind], big["w2"][kind], big["w_ada"][kind], sm_list[5]]

    grads = [grad_g_norm1, grad_w_in, grad_g_q, grad_g_k, grad_rel_bias, grad_w_o, grad_g_norm2, grad_w1, grad_w2,
             grad_w_ada, grad_b_ada]
    return (loss, grad_x, *grads, *ordered(0), *ordered(1), *ordered(2))
```

```python
import functools

import jax
import jax.numpy as jnp
from jax import lax
from jax.experimental import pallas as pl
from jax.experimental.pallas import tpu as pltpu

F32 = jnp.float32
BF16 = jnp.bfloat16
MESH = pl.DeviceIdType.MESH

EPS = 1e-6
NEG_INF = -1e30
HEAD_DIM = 64
CHUNK = 64
LEFT_CHUNKS = 8
PAD = LEFT_CHUNKS * CHUNK
BAND = PAD + CHUNK
REL_CLIP = 128
SCALE = HEAD_DIM ** -0.5
DEAD_LOG = -88.0
LANES = 128
N_CHIPS = 4
N_DEV = 8

ADAM_LR = 0.001
ADAM_B1 = 0.9
ADAM_B2 = 0.999
ADAM_EPS = 1e-08
ADAM_WD = 0.01
ADAM_STEP = 10

VMEM_LIMIT = 48 * 1024 * 1024
ROW_TILE = 512
MM_TILE = 1024
SB_TILE = 256
CA_TILE = 512

NT = (((1,), (1,)), ((), ()))
TN = (((0,), (0,)), ((), ()))
NN = (((1,), (0,)), ((), ()))


def _tile(n, pref):
    best = None
    t = LANES
    while t <= min(n, pref):
        if n % t == 0:
            best = t
        t += LANES
    return best if best is not None else n


def _params(*sem):
    return pltpu.CompilerParams(dimension_semantics=sem, vmem_limit_bytes=VMEM_LIMIT)


def _split_dot(x, t, passes):
    acc = None
    r = x
    for i in range(passes):
        hi = r.astype(BF16)
        d = jnp.dot(hi, t, preferred_element_type=F32)
        acc = d if acc is None else acc + d
        if i + 1 < passes:
            r = r - hi.astype(F32)
    return acc


def _grid_edge(grid, last):
    conds = [pl.program_id(i) == (g - 1 if last else 0) for i, g in enumerate(grid)]
    return functools.reduce(jnp.logical_and, conds)


def _ride_gather(grid, g_refs, sems, compute):
    if g_refs:
        @pl.when(_grid_edge(grid, False))
        def _():
            for cp in _gather_copies(g_refs, *sems, 0)[0]:
                cp.start()

    compute()
    if g_refs:
        @pl.when(_grid_edge(grid, True))
        def _():
            for cp, ld in zip(*_gather_copies(g_refs, *sems, 0)):
                cp.wait_send()
                ld.wait_recv()


def _rider(ins, out_shapes, sems, copies, aliased):
    return dict(ins=list(ins), outs=list(out_shapes), sems=list(sems), copies=copies, aliased=aliased)


def _gather_rider(bufs):
    def copies(_, outs, sems):
        def both(cp, ld):
            def wait():
                cp.wait_send()
                ld.wait_recv()
            return wait
        return [(cp, both(cp, ld)) for cp, ld in zip(*_gather_copies(outs, *sems, 0))]

    return _rider(bufs, [jax.ShapeDtypeStruct(b.shape, b.dtype) for b in bufs], _gather_sems(len(bufs)), copies, True)


def _exchange_rider(parts):
    def copies(ins, outs, sems):
        return [(cp, cp.wait) for cp in _exchange_copies(ins, outs, *sems)]

    return _rider(parts, [jax.ShapeDtypeStruct((3,) + p.shape[1:], p.dtype) for p in parts],
                  _exchange_sems(len(parts)), copies, False)


def _sibling_rider(grads):
    def copies(ins, outs, sems):
        return [(cp, cp.wait) for cp in _sibling_copies(ins, outs, *sems)]

    n = len(grads)
    return _rider(grads, [jax.ShapeDtypeStruct((g.shape[0], g.shape[1] // 2, g.shape[2]), g.dtype) for g in grads],
                  [pltpu.SemaphoreType.DMA((n,)), pltpu.SemaphoreType.DMA((n,))], copies, False)


def _matmul(name, a, b, *, dims, grid, a_spec, b_spec, out_shapes, out_specs, acc_shape,
            epi=None, extras=(), extra_specs=(), riders=()):
    nk = grid[2]
    n_ex = len(extras)
    n_out = len(out_shapes)
    n_in = [len(r["ins"]) for r in riders]
    n_ro = [len(r["outs"]) for r in riders]

    def body(*refs):
        a_ref, b_ref = refs[0], refs[1]
        ex = refs[2:2 + n_ex]
        pos = 2 + n_ex
        r_in = []
        for n in n_in:
            r_in.append(refs[pos:pos + n])
            pos += n
        outs = refs[pos:pos + n_out]
        pos += n_out
        r_out = []
        for n in n_ro:
            r_out.append(refs[pos:pos + n])
            pos += n
        acc = refs[pos]
        pos += 1
        r_sems = []
        for r in riders:
            r_sems.append(refs[pos:pos + len(r["sems"])])
            pos += len(r["sems"])

        def all_copies():
            return [c for r, i, o, sm in zip(riders, r_in, r_out, r_sems) for c in r["copies"](i, o, sm)]

        if riders:
            @pl.when(_grid_edge(grid, False))
            def _():
                for cp, _ in all_copies():
                    cp.start()

        k = pl.program_id(2)

        @pl.when(k == 0)
        def _():
            acc[...] = jnp.zeros_like(acc)

        acc[...] += lax.dot_general(a_ref[...].astype(BF16), b_ref[...].astype(BF16), dims,
                                    preferred_element_type=F32)

        @pl.when(k == nk - 1)
        def _():
            res = epi(acc[...], *[e[...] for e in ex]) if epi is not None else (acc[...],)
            for o, r in zip(outs, res):
                o[...] = r.astype(o.dtype)

        if riders:
            @pl.when(_grid_edge(grid, True))
            def _():
                for _, wait in all_copies():
                    wait()

    r_ins = [x for r in riders for x in r["ins"]]
    r_outs = [x for r in riders for x in r["outs"]]
    aliases, i_pos, o_pos = {}, 2 + n_ex, n_out
    for r in riders:
        if r["aliased"]:
            aliases.update({i_pos + i: o_pos + i for i in range(len(r["ins"]))})
        i_pos += len(r["ins"])
        o_pos += len(r["outs"])
    outs = pl.pallas_call(
        body, grid=grid, in_specs=[a_spec, b_spec, *extra_specs] + [HBM_SPEC] * len(r_ins),
        out_specs=list(out_specs) + [HBM_SPEC] * len(r_outs), out_shape=list(out_shapes) + r_outs,
        input_output_aliases=aliases,
        scratch_shapes=[pltpu.VMEM(acc_shape, F32)] + [sm for r in riders for sm in r["sems"]],
        compiler_params=_params(*(("arbitrary",) * 3 if riders else ("parallel", "parallel", "arbitrary"))), name=name,
    )(a, b, *extras, *r_ins)
    landed, pos = [], n_out
    for n in n_ro:
        landed.append(list(outs[pos:pos + n]))
        pos += n
    return list(outs[:n_out]), landed


def _mm_nn_w(name, a, wg, layer, colshard, out_dtypes, epi=None, extras=(), extra_specs_fn=None, gather=()):
    m = a.shape[0]
    _, _, r, c = wg.shape
    tm = min(MM_TILE, m)
    if colshard:
        k, n = r, N_CHIPS * c
        tk, tn = k, _tile(c, MM_TILE)
        npc = c // tn
        b_spec = pl.BlockSpec((None, None, tk, tn), lambda i, j, kk: (layer, j // npc, kk, j % npc))
    else:
        k, n = N_CHIPS * r, c
        wg = wg.reshape(wg.shape[0], 1, k, c)
        tk, tn = _tile(k, MM_TILE), _tile(c, MM_TILE)
        b_spec = pl.BlockSpec((None, None, tk, tn), lambda i, j, kk: (layer, 0, kk, j))
    assert a.shape[1] == k
    grid = (m // tm, n // tn, k // tk)
    out_spec = pl.BlockSpec((tm, tn), lambda i, j, kk: (i, j))
    ex_specs = extra_specs_fn(tm, tn) if extra_specs_fn is not None else ()
    outs, landed = _matmul(name, a, wg, dims=NN, grid=grid,
                           a_spec=pl.BlockSpec((tm, tk), lambda i, j, kk: (i, kk)), b_spec=b_spec,
                           out_shapes=[jax.ShapeDtypeStruct((m, n), d) for d in out_dtypes],
                           out_specs=[out_spec for _ in out_dtypes], acc_shape=(tm, tn),
                           epi=epi, extras=extras, extra_specs=ex_specs,
                           riders=[_gather_rider(gather)] if gather else ())
    return (outs, landed[0]) if gather else outs


def _mm_nt_w(name, a, wg, layer, colshard, out_dtype, epi=None, extras=(), extra_specs_fn=None, riders=()):
    m = a.shape[0]
    _, _, r, c = wg.shape
    tm = min(MM_TILE, m)
    if colshard:
        kdim, n = N_CHIPS * c, r
        tk, tn = _tile(c, MM_TILE), _tile(r, MM_TILE)
        kpc = c // tk
        b_spec = pl.BlockSpec((None, None, tn, tk), lambda i, j, kk: (layer, kk // kpc, j, kk % kpc))
    else:
        kdim, n = c, N_CHIPS * r
        wg = wg.reshape(wg.shape[0], 1, n, c)
        tk, tn = c, _tile(n, MM_TILE)
        b_spec = pl.BlockSpec((None, None, tn, tk), lambda i, j, kk: (layer, 0, j, kk))
    assert a.shape[1] == kdim
    grid = (m // tm, n // tn, kdim // tk)
    ex_specs = extra_specs_fn(tm, tn) if extra_specs_fn is not None else ()
    outs, landed = _matmul(name, a, wg, dims=NT, grid=grid,
                   a_spec=pl.BlockSpec((tm, tk), lambda i, j, kk: (i, kk)), b_spec=b_spec,
                   out_shapes=[jax.ShapeDtypeStruct((m, n), out_dtype)],
                   out_specs=[pl.BlockSpec((tm, tn), lambda i, j, kk: (i, j))], acc_shape=(tm, tn),
                   epi=epi, extras=extras, extra_specs=ex_specs, riders=riders)
    return (outs[0], landed) if riders else outs[0]


def _mm_tn(name, a, b, colshard):
    s, m = a.shape
    n = b.shape[1]
    tk = min(2 * MM_TILE, s)
    if colshard:
        c = n // N_CHIPS
        tm, tn = _tile(m, MM_TILE // 2), _tile(c, MM_TILE)
        npc = c // tn
        out_shape = (N_CHIPS, m, c)
        out_spec = pl.BlockSpec((None, tm, tn), lambda i, j, kk: (j // npc, i, j % npc))
    else:
        r = m // N_CHIPS
        tm, tn = _tile(r, MM_TILE // 2), _tile(n, MM_TILE)
        mpc = r // tm
        out_shape = (N_CHIPS, r, n)
        out_spec = pl.BlockSpec((None, tm, tn), lambda i, j, kk: (i // mpc, i % mpc, j))
    grid = (m // tm, n // tn, s // tk)
    return _matmul(name, a, b, dims=TN, grid=grid,
                   a_spec=pl.BlockSpec((tk, tm), lambda i, j, kk: (kk, i)),
                   b_spec=pl.BlockSpec((tk, tn), lambda i, j, kk: (kk, j)),
                   out_shapes=[jax.ShapeDtypeStruct(out_shape, F32)], out_specs=[out_spec],
                   acc_shape=(tm, tn))[0][0]


def _row_spec(ts, d):
    return pl.BlockSpec((ts, d), lambda i: (i, 0))


def _vec_spec(d):
    return pl.BlockSpec((1, d), lambda i: (0, 0))


def _norm_mod(name, x, g, sc, sh):
    s, d = x.shape
    ts = min(ROW_TILE, s)

    def body(x_ref, g_ref, sc_ref, sh_ref, h_ref):
        xv = x_ref[...]
        r = lax.rsqrt(jnp.mean(xv * xv, axis=-1, keepdims=True) + EPS)
        h_ref[...] = (xv * r * g_ref[...] * (1.0 + sc_ref[...]) + sh_ref[...]).astype(BF16)

    return pl.pallas_call(
        body, grid=(s // ts,), in_specs=[_row_spec(ts, d), _vec_spec(d), _vec_spec(d), _vec_spec(d)],
        out_specs=_row_spec(ts, d), out_shape=jax.ShapeDtypeStruct((s, d), BF16),
        compiler_params=_params("parallel"), name=name)(x, g, sc, sh)


def _norm_mod_bwd(name, dh, x, g, sc, dres):
    s, d = x.shape
    ts = min(ROW_TILE, s)

    def body(dh_ref, x_ref, g_ref, sc_ref, dres_ref, dx_ref, dsh_ref, dsc_ref, dg_ref):
        @pl.when(pl.program_id(0) == 0)
        def _():
            dsh_ref[...] = jnp.zeros_like(dsh_ref)
            dsc_ref[...] = jnp.zeros_like(dsc_ref)
            dg_ref[...] = jnp.zeros_like(dg_ref)

        xv = x_ref[...]
        r = lax.rsqrt(jnp.mean(xv * xv, axis=-1, keepdims=True) + EPS)
        xhat = xv * r
        dhv = dh_ref[...]
        gv = g_ref[...]
        opsc = 1.0 + sc_ref[...]
        dxhat = dhv * (gv * opsc)
        mdot = jnp.mean(dxhat * xhat, axis=-1, keepdims=True)
        dx_ref[...] = dres_ref[...] + r * (dxhat - xhat * mdot)
        dhx = dhv * xhat
        dsh_ref[...] += jnp.sum(dhv, axis=0, keepdims=True)
        dsc_ref[...] += jnp.sum(dhx * gv, axis=0, keepdims=True)
        dg_ref[...] += jnp.sum(dhx * opsc, axis=0, keepdims=True)

    vec = jax.ShapeDtypeStruct((1, d), F32)
    return pl.pallas_call(
        body, grid=(s // ts,),
        in_specs=[_row_spec(ts, d), _row_spec(ts, d), _vec_spec(d), _vec_spec(d), _row_spec(ts, d)],
        out_specs=[_row_spec(ts, d), _vec_spec(d), _vec_spec(d), _vec_spec(d)],
        out_shape=[jax.ShapeDtypeStruct((s, d), F32), vec, vec, vec],
        compiler_params=_params("arbitrary"), name=name)(dh, x, g, sc, dres)


def _gate_bwd(name, dx, m, gt):
    s, d = dx.shape
    ts = min(ROW_TILE, s)

    def body(dx_ref, m_ref, gt_ref, dm_ref, dgt_ref):
        @pl.when(pl.program_id(0) == 0)
        def _():
            dgt_ref[...] = jnp.zeros_like(dgt_ref)

        dxv = dx_ref[...]
        dm_ref[...] = (dxv * gt_ref[...]).astype(BF16)
        dgt_ref[...] += jnp.sum(dxv * m_ref[...].astype(F32), axis=0, keepdims=True)

    return pl.pallas_call(
        body, grid=(s // ts,), in_specs=[_row_spec(ts, d), _row_spec(ts, d), _vec_spec(d)],
        out_specs=[_row_spec(ts, d), _vec_spec(d)],
        out_shape=[jax.ShapeDtypeStruct((s, d), BF16), jax.ShapeDtypeStruct((1, d), F32)],
        compiler_params=_params("arbitrary"), name=name)(dx, m, gt)


def _loss_grad(name, y, target):
    s, d = y.shape
    ts = min(ROW_TILE, s)

    def body(y_ref, t_ref, dy_ref, sq_ref):
        @pl.when(pl.program_id(0) == 0)
        def _():
            sq_ref[...] = jnp.zeros_like(sq_ref)

        e = y_ref[...] - t_ref[...]
        dy_ref[...] = e * (1.0 / d)
        sq_ref[...] += jnp.sum(e * e, axis=0, keepdims=True)

    return pl.pallas_call(
        body, grid=(s // ts,), in_specs=[_row_spec(ts, d), _row_spec(ts, d)],
        out_specs=[_row_spec(ts, d), _vec_spec(d)],
        out_shape=[jax.ShapeDtypeStruct((s, d), F32), jax.ShapeDtypeStruct((1, d), F32)],
        compiler_params=_params("arbitrary"), name=name)(y, target)


def _adamw(name, w, g, m, v):
    r, c = w.shape
    tr = r
    for cand in (512, 256, 128, 64, 32, 16, 8):
        if r % cand == 0 and cand * c * 4 <= 2 * 1024 * 1024:
            tr = cand
            break

    def body(w_ref, g_ref, m_ref, v_ref, d_ref, mo_ref, vo_ref):
        gv = g_ref[...]
        m2 = ADAM_B1 * m_ref[...] + (1.0 - ADAM_B1) * gv
        v2 = ADAM_B2 * v_ref[...] + (1.0 - ADAM_B2) * (gv * gv)
        m_hat = m2 / (1.0 - ADAM_B1 ** ADAM_STEP)
        v_hat = v2 / (1.0 - ADAM_B2 ** ADAM_STEP)
        d_ref[...] = -ADAM_LR * (m_hat / (jnp.sqrt(v_hat) + ADAM_EPS) + ADAM_WD * w_ref[...])
        mo_ref[...] = m2
        vo_ref[...] = v2

    spec = pl.BlockSpec((tr, c), lambda i: (i, 0))
    shp = jax.ShapeDtypeStruct((r, c), F32)
    return pl.pallas_call(
        body, grid=(r // tr,), in_specs=[spec] * 4, out_specs=[spec] * 3, out_shape=[shp] * 3,
        compiler_params=_params("parallel"), name=name)(w, g, m, v)


def _sum_rows8(name, x):
    n, r, c = x.shape

    def body(x_ref, o_ref):
        acc = x_ref[0]
        for i in range(1, n):
            acc = acc + x_ref[i]
        o_ref[...] = acc

    return pl.pallas_call(
        body, out_shape=jax.ShapeDtypeStruct((r, c), F32),
        in_specs=[pl.BlockSpec(memory_space=pltpu.VMEM)], out_specs=pl.BlockSpec(memory_space=pltpu.VMEM),
        name=name)(x)


def _outer_sum(name, ct, dm):
    d, n_seq = ct.shape
    n = dm.shape[1]
    tr = min(256, d)

    def body(ct_ref, dm_ref, o_ref):
        acc = ct_ref[:, 0:1] * dm_ref[0:1, :]
        for s_i in range(1, n_seq):
            acc = acc + ct_ref[:, s_i:s_i + 1] * dm_ref[s_i:s_i + 1, :]
        o_ref[...] = acc

    return pl.pallas_call(
        body, grid=(d // tr,),
        in_specs=[pl.BlockSpec((tr, n_seq), lambda i: (i, 0)), pl.BlockSpec((n_seq, n), lambda i: (0, 0))],
        out_specs=pl.BlockSpec((tr, n), lambda i: (i, 0)), out_shape=jax.ShapeDtypeStruct((d, n), F32),
        compiler_params=_params("parallel"), name=name)(ct, dm)


def _ada_mod(name, c_act, w_ada_l, bias):
    n_seq, d = c_act.shape
    n = w_ada_l.shape[1]
    tn = _tile(n, 768)

    def epi(acc, b):
        return (acc + b,)

    return _matmul(name, c_act, w_ada_l, dims=NN, grid=(1, n // tn, 1),
                   a_spec=pl.BlockSpec((n_seq, d), lambda i, j, kk: (0, 0)),
                   b_spec=pl.BlockSpec((d, tn), lambda i, j, kk: (0, j)),
                   out_shapes=[jax.ShapeDtypeStruct((n_seq, n), F32)],
                   out_specs=[pl.BlockSpec((n_seq, tn), lambda i, j, kk: (0, j))], acc_shape=(n_seq, tn),
                   epi=epi, extras=(bias,), extra_specs=(pl.BlockSpec((1, tn), lambda i, j, kk: (0, j)),))[0][0]


def _silu_rows(name, c_row):
    d = c_row.shape[1]

    def body(c_ref, o_ref):
        cv = c_ref[...]
        o_ref[...] = jnp.broadcast_to(cv / (1.0 + jnp.exp(-cv)), (8, d))

    return pl.pallas_call(
        body, out_shape=jax.ShapeDtypeStruct((8, d), F32),
        in_specs=[pl.BlockSpec(memory_space=pltpu.VMEM)], out_specs=pl.BlockSpec(memory_space=pltpu.VMEM),
        name=name)(c_row)


def _sb_masks(t):
    row = lax.broadcasted_iota(jnp.int32, (t, t), 0)
    col = lax.broadcasted_iota(jnp.int32, (t, t), 1)
    strict = col < row
    tri = (row > col).astype(BF16)
    return strict, tri


def _col_minus_row(t):
    return lax.broadcasted_iota(jnp.int32, (t, t), 1) - lax.broadcasted_iota(jnp.int32, (t, t), 0)


def _sb_block(qi, i, sb):
    return qi - 2 * i - (1 - sb)


def _sb_keep(diff, qi, kb, t):
    return diff < jnp.where(kb >= 0, (qi - kb) * t, -t)


def _sweep_left(step, qi, carry, runs_of):
    def alive(c):
        runs = runs_of(c)
        top = runs[0]
        for r in runs[1:]:
            top = jnp.maximum(top, r)
        return jnp.max(top)

    n_plain = jnp.maximum(qi - 1, 0) // 2

    def cond(state):
        i, _, top = state
        return jnp.logical_and(i <= n_plain, top > DEAD_LOG)

    def body(state):
        i, c, _ = state
        c = step(i, c, False)
        return i + 1, c, alive(c)

    _, carry, top = lax.while_loop(cond, body, (1, carry, alive(carry)))
    lone = jnp.logical_and(jnp.logical_and(qi >= 2, qi % 2 == 0), top > DEAD_LOG)
    return lax.cond(lone, lambda c: step(qi // 2, c, True), lambda c: c, carry)


def _log_sigmoids(z):
    sp = jnp.log(1.0 + jnp.exp(-jnp.abs(z)))
    return jnp.minimum(z, 0.0) - sp, jnp.minimum(-z, 0.0) - sp, sp


def _sb_fwd(name, proj, n_pairs, gather=()):
    s = proj.shape[0]
    t = min(SB_TILE, s)
    nq = s // t
    n_g = len(gather)

    heads = [slice(h * HEAD_DIM, (h + 1) * HEAD_DIM) for h in range(LANES // HEAD_DIM)]

    assert nq % 2 == 0

    def body(*refs):
        q_ref, k_ref, v_ref = refs[:3]
        o_ref = refs[3 + n_g]
        g_refs, sems = refs[4 + n_g:4 + 2 * n_g], refs[4 + 2 * n_g:]
        _ride_gather((n_pairs, nq), g_refs, sems, lambda: _sb_fwd_body(q_ref, k_ref, v_ref, o_ref))

    def _sb_fwd_body(q_ref, k_ref, v_ref, o_ref):
        qi = pl.program_id(1)
        _, tri = _sb_masks(t)
        diff = _col_minus_row(t)
        qs = [(q_ref[:, sl] * SCALE).astype(BF16) for sl in heads]
        chains = [(sb, h) for sb in (1, 0) for h in range(len(heads))]

        def step(j, carry, masked):
            kbs = [_sb_block(qi, j, sb) for sb, _ in chains]
            ks = [pl.multiple_of(jnp.maximum(kb, 0) * t, t) for kb in kbs]
            k_ = [k_ref[pl.ds(ks[i], t), heads[h]].astype(BF16) for i, (_, h) in enumerate(chains)]
            v_ = [v_ref[pl.ds(ks[i], t), heads[h]].astype(BF16) for i, (_, h) in enumerate(chains)]
            zs = [lax.dot_general(qs[h], k_[i], NT, preferred_element_type=F32) for i, (_, h) in enumerate(chains)]
            lbs, lms, keeps = [], [], []
            for kb, z in zip(kbs, zs):
                lb, lm, _ = _log_sigmoids(z)
                keep = None
                if masked:
                    keep = _sb_keep(diff, qi, kb, t)
                    lm = jnp.where(keep, lm, 0.0)
                lbs.append(lb)
                lms.append(lm)
                keeps.append(keep)
            css = [_split_dot(lm, tri, 2) for lm in lms]
            sums = [jnp.sum(lm, axis=1, keepdims=True) for lm in lms]
            ws = []
            for i, (sb, h) in enumerate(chains):
                run = carry[h][1] if sb == 1 else carry[h][1] + sums[h]
                w = jnp.exp(lbs[i] + css[i] + run)
                if masked:
                    w = jnp.where(keeps[i], w, 0.0)
                ws.append(w.astype(BF16))
            pvs = [jnp.dot(w, v, preferred_element_type=F32) for w, v in zip(ws, v_)]
            nh = len(heads)
            return tuple((carry[h][0] + pvs[h] + pvs[nh + h], carry[h][1] + sums[h] + sums[nh + h]) for h in range(nh))

        init = tuple((jnp.zeros((t, HEAD_DIM), F32), jnp.zeros((t, 1), F32)) for _ in heads)
        carry = step(0, init, True)
        carry = _sweep_left(step, qi, carry, lambda c: [run for _, run in c])
        for (acc, _), sl in zip(carry, heads):
            o_ref[:, sl] = acc

    outs = pl.pallas_call(
        body, grid=(n_pairs, nq),
        in_specs=[pl.BlockSpec((t, LANES), lambda p, i: (i, p)),
                  pl.BlockSpec((s, LANES), lambda p, i: (0, n_pairs + p)),
                  pl.BlockSpec((s, LANES), lambda p, i: (0, 2 * n_pairs + p))] + [HBM_SPEC] * n_g,
        out_specs=[pl.BlockSpec((t, LANES), lambda p, i: (i, p))] + [HBM_SPEC] * n_g,
        out_shape=[jax.ShapeDtypeStruct((s, 2 * n_pairs * LANES), F32)]
        + [jax.ShapeDtypeStruct(g.shape, g.dtype) for g in gather],
        input_output_aliases={3 + i: 1 + i for i in range(n_g)},
        scratch_shapes=_gather_sems(n_g) if n_g else [],
        compiler_params=_params("arbitrary", "arbitrary"), name=name)(proj, proj, proj, *gather)
    return outs[0], list(outs[1:])


def _sb_bwd(name, proj, o_sb, dmixed, n_pairs, exchange=()):
    s = proj.shape[0]
    t = min(SB_TILE, s)
    nq = s // t
    n_x = len(exchange)
    heads = [slice(h * HEAD_DIM, (h + 1) * HEAD_DIM) for h in range(LANES // HEAD_DIM)]

    def body(*refs):
        x_in, x_out, sems = refs[5:5 + n_x], refs[8 + n_x:8 + 2 * n_x], refs[8 + 2 * n_x:]
        if n_x:
            @pl.when(_grid_edge((n_pairs, nq), False))
            def _():
                for cp in _exchange_copies(x_in, x_out, *sems):
                    cp.start()

        _sb_bwd_body(*refs[:5], *refs[5 + n_x:8 + n_x])
        if n_x:
            @pl.when(_grid_edge((n_pairs, nq), True))
            def _():
                for cp in _exchange_copies(x_in, x_out, *sems):
                    cp.wait()

    def _sb_bwd_body(q_ref, k_ref, v_ref, o_ref, do_ref, dq_ref, dk_ref, dv_ref):
        qi = pl.program_id(1)

        @pl.when(qi == 0)
        def _():
            dk_ref[...] = jnp.zeros_like(dk_ref)
            dv_ref[...] = jnp.zeros_like(dv_ref)

        _, tri = _sb_masks(t)
        diff = _col_minus_row(t)
        qs = [(q_ref[:, sl] * SCALE).astype(BF16) for sl in heads]
        dos = [do_ref[:, sl].astype(BF16) for sl in heads]
        dsums = [jnp.sum(do.astype(F32) * o_ref[:, sl], axis=1, keepdims=True) for do, sl in zip(dos, heads)]
        nh = len(heads)
        chains = [(sb, h) for sb in (1, 0) for h in range(nh)]

        def step(j, carry, masked):
            kbs = [_sb_block(qi, j, sb) for sb, _ in chains]
            ks = [pl.multiple_of(jnp.maximum(kb, 0) * t, t) for kb in kbs]
            k_ = [k_ref[pl.ds(ks[i], t), heads[h]].astype(BF16) for i, (_, h) in enumerate(chains)]
            v_ = [v_ref[pl.ds(ks[i], t), heads[h]].astype(BF16) for i, (_, h) in enumerate(chains)]
            zs = [lax.dot_general(qs[h], k_[i], NT, preferred_element_type=F32) for i, (_, h) in enumerate(chains)]
            dws = [lax.dot_general(dos[h], v_[i], NT, preferred_element_type=F32) for i, (_, h) in enumerate(chains)]
            lbs, lms, betas, ombs, keeps = [], [], [], [], []
            for kb, z in zip(kbs, zs):
                ez = jnp.exp(-jnp.abs(z))
                lb = jnp.minimum(z, 0.0) - jnp.log(1.0 + ez)
                lm = lb - z
                betas.append(jnp.exp(lb))
                ombs.append(jnp.exp(lm))
                keep = None
                if masked:
                    keep = _sb_keep(diff, qi, kb, t)
                    lm = jnp.where(keep, lm, 0.0)
                lbs.append(lb)
                lms.append(lm)
                keeps.append(keep)
            css = [_split_dot(lm, tri, 2) for lm in lms]
            sums_f = [jnp.sum(lm, axis=1, keepdims=True) for lm in lms]
            wbs, es = [], []
            for i, (sb, h) in enumerate(chains):
                run_f = carry[h][1] if sb == 1 else carry[h][1] + sums_f[h]
                w = jnp.exp(lbs[i] + css[i] + run_f)
                if masked:
                    w = jnp.where(keeps[i], w, 0.0)
                wb = w.astype(BF16)
                wbs.append(wb)
                es.append(wb.astype(F32) * dws[i])
            for i, (_, h) in enumerate(chains):
                dv_ref[pl.ds(ks[i], t), heads[h]] += lax.dot_general(wbs[i], dos[h], TN, preferred_element_type=F32)
            ecs = [_split_dot(e, tri, 2) for e in es]
            sums_e = [jnp.sum(e, axis=1, keepdims=True) for e in es]
            dzbs = []
            for i, (sb, h) in enumerate(chains):
                run_e = carry[h][2] if sb == 1 else carry[h][2] + sums_e[h]
                left = dsums[h] - (ecs[i] + es[i] + run_e)
                dz = es[i] * ombs[i] - left * betas[i]
                if masked:
                    dz = jnp.where(keeps[i], dz, 0.0)
                dzbs.append(dz.astype(BF16))
            dqs = []
            for i, (_, h) in enumerate(chains):
                dk_ref[pl.ds(ks[i], t), heads[h]] += lax.dot_general(dzbs[i], qs[h], TN, preferred_element_type=F32)
                dqs.append(jnp.dot(dzbs[i], k_[i], preferred_element_type=F32))
            return tuple((carry[h][0] + dqs[h] + dqs[nh + h], carry[h][1] + sums_f[h] + sums_f[nh + h],
                          carry[h][2] + sums_e[h] + sums_e[nh + h]) for h in range(nh))

        zero = jnp.zeros((t, 1), F32)
        init = tuple((jnp.zeros((t, HEAD_DIM), F32), zero, zero) for _ in heads)
        carry = step(0, init, True)
        carry = _sweep_left(step, qi, carry, lambda c: [run_f for _, run_f, _ in c])
        for (dq, _, _), sl in zip(carry, heads):
            dq_ref[:, sl] = dq * SCALE

    blk = pl.BlockSpec((t, LANES), lambda p, i: (i, p))
    full = pl.BlockSpec((s, LANES), lambda p, i: (0, p))
    shp = jax.ShapeDtypeStruct((s, n_pairs * LANES), F32)
    outs = pl.pallas_call(
        body, grid=(n_pairs, nq),
        in_specs=[blk,
                  pl.BlockSpec((s, LANES), lambda p, i: (0, n_pairs + p)),
                  pl.BlockSpec((s, LANES), lambda p, i: (0, 2 * n_pairs + p)),
                  blk, blk] + [HBM_SPEC] * n_x,
        out_specs=[blk, full, full] + [HBM_SPEC] * n_x,
        out_shape=[shp, shp, shp] + [jax.ShapeDtypeStruct((3,) + p.shape[1:], p.dtype) for p in exchange],
        scratch_shapes=_exchange_sems(n_x) if n_x else [],
        compiler_params=_params("arbitrary", "arbitrary"), name=name)(proj, proj, proj, o_sb, dmixed, *exchange)
    return outs[0], outs[1], outs[2], list(outs[3:])


def _group_mean_matrix():
    row = lax.broadcasted_iota(jnp.int32, (LANES, LANES), 0)
    col = lax.broadcasted_iota(jnp.int32, (LANES, LANES), 1)
    same_head = (row < HEAD_DIM) == (col < HEAD_DIM)
    return jnp.where(same_head, 1.0 / HEAD_DIM, 0.0).astype(BF16)


def _qk_norm(name, proj, gq2, gk2, n_pairs):
    s = proj.shape[0]
    ts = min(ROW_TILE, s)
    pb = PAD // ts
    c0 = 3 * n_pairs

    def body(q_ref, k_ref, v_ref, gq_ref, gk_ref, qn_ref, kn_ref, vp_ref):
        r_i = pl.program_id(1)

        @pl.when(r_i < pb)
        def _():
            qn_ref[...] = jnp.zeros_like(qn_ref)
            kn_ref[...] = jnp.zeros_like(kn_ref)
            vp_ref[...] = jnp.zeros_like(vp_ref)

        @pl.when(r_i >= pb)
        def _():
            gm = _group_mean_matrix()
            for src, g_ref, dst in ((q_ref, gq_ref, qn_ref), (k_ref, gk_ref, kn_ref)):
                xv = src[...]
                r = lax.rsqrt(_split_dot(xv * xv, gm, 2) + EPS)
                dst[...] = (xv * r * g_ref[...]).astype(BF16)
            vp_ref[...] = v_ref[...].astype(BF16)

    def in_spec(off):
        return pl.BlockSpec((ts, LANES), lambda p, i: (jnp.maximum(i - pb, 0), c0 + off * n_pairs + p))

    gspec = pl.BlockSpec((1, LANES), lambda p, i: (0, 0))
    ospec = pl.BlockSpec((ts, LANES), lambda p, i: (i, p))
    shp = jax.ShapeDtypeStruct((s + PAD, n_pairs * LANES), BF16)
    return pl.pallas_call(
        body, grid=(n_pairs, (s + PAD) // ts),
        in_specs=[in_spec(0), in_spec(1), in_spec(2), gspec, gspec],
        out_specs=[ospec, ospec, ospec], out_shape=[shp, shp, shp],
        compiler_params=_params("parallel", "arbitrary"), name=name)(proj, proj, proj, gq2, gk2)


def _qk_norm_bwd(name, dqn, dkn, proj, gq2, gk2, n_pairs):
    s = proj.shape[0]
    ts = min(ROW_TILE, s)
    pb = PAD // ts
    c0 = 3 * n_pairs
    n_r = s // ts

    def body(dqn_ref, dkn_ref, q_ref, k_ref, gq_ref, gk_ref, dq_ref, dk_ref, dgq_ref, dgk_ref):
        first = jnp.logical_and(pl.program_id(0) == 0, pl.program_id(1) == 0)
        last = jnp.logical_and(pl.program_id(0) == n_pairs - 1, pl.program_id(1) == n_r - 1)

        @pl.when(first)
        def _():
            dgq_ref[...] = jnp.zeros_like(dgq_ref)
            dgk_ref[...] = jnp.zeros_like(dgk_ref)

        gm = _group_mean_matrix()
        for dn_ref, x_ref, g_ref, dx_ref, dg_ref in ((dqn_ref, q_ref, gq_ref, dq_ref, dgq_ref),
                                                     (dkn_ref, k_ref, gk_ref, dk_ref, dgk_ref)):
            xv = x_ref[...]
            r = lax.rsqrt(_split_dot(xv * xv, gm, 2) + EPS)
            xhat = xv * r
            dn = dn_ref[...]
            tg = dn * g_ref[...]
            md = _split_dot(tg * xhat, gm, 2)
            dx_ref[...] = (r * (tg - xhat * md)).astype(BF16)
            dg_ref[...] += jnp.sum(dn * xhat, axis=0, keepdims=True)

        @pl.when(last)
        def _():
            for dg_ref in (dgq_ref, dgk_ref):
                gv = dg_ref[...]
                dg_ref[...] = gv + pltpu.roll(gv, HEAD_DIM, axis=1)

    def x_spec(off):
        return pl.BlockSpec((ts, LANES), lambda p, i: (i, c0 + off * n_pairs + p))

    dn_q = pl.BlockSpec((ts, LANES), lambda p, i: (i, p))
    dn_k = pl.BlockSpec((ts, LANES), lambda p, i: (i + pb, p))
    gspec = pl.BlockSpec((1, LANES), lambda p, i: (0, 0))
    ospec = pl.BlockSpec((ts, LANES), lambda p, i: (i, p))
    shp = jax.ShapeDtypeStruct((s, n_pairs * LANES), BF16)
    vec = jax.ShapeDtypeStruct((1, LANES), F32)
    return pl.pallas_call(
        body, grid=(n_pairs, n_r),
        in_specs=[dn_q, dn_k, x_spec(0), x_spec(1), gspec, gspec],
        out_specs=[ospec, ospec, gspec, gspec], out_shape=[shp, shp, vec, vec],
        compiler_params=_params("arbitrary", "arbitrary"), name=name)(dqn, dkn, proj, proj, gq2, gk2)


def _head_blocks(x2):
    lane = lax.broadcasted_iota(jnp.int32, x2.shape, 1)
    zero = jnp.zeros_like(x2)
    return jnp.concatenate([jnp.where(lane < HEAD_DIM, x2, zero), jnp.where(lane >= HEAD_DIM, x2, zero)], axis=0)


def _head_diag(r):
    lane = lax.broadcasted_iota(jnp.int32, (CHUNK, LANES), 1)
    return jnp.where(lane < HEAD_DIM, r[:CHUNK], r[CHUNK:])


def _ca_probs(kw, qb, bias_t, n):
    sc = lax.dot_general(kw, qb, NT, preferred_element_type=F32) * SCALE + bias_t
    jpos = lax.broadcasted_iota(jnp.int32, (BAND, LANES), 0)
    sc = jnp.where(n * CHUNK + jpos >= PAD, sc, NEG_INF)
    p = jnp.exp(sc - jnp.max(sc, axis=0, keepdims=True))
    return p * (1.0 / jnp.sum(p, axis=0, keepdims=True))


def _ca_fwd(name, qn, knp, vp, bias_tt, n_pairs, mixed, gather=()):
    sp_rows = knp.shape[0]
    s = sp_rows - PAD
    t = min(CA_TILE, s)
    cpb = t // CHUNK
    pb = PAD // t

    n_g = len(gather)
    grid = (n_pairs, s // t)

    def body(*refs):
        q_ref, k_ref, v_ref, b_ref = refs[:4]
        o_ref = refs[5 + n_g]
        g_refs, sems = refs[6 + n_g:6 + 2 * n_g], refs[6 + 2 * n_g:]

        def compute():
            qi = pl.program_id(1)
            wss = [pl.multiple_of((qi * cpb + ci) * CHUNK, CHUNK) for ci in range(cpb)]
            ps = [_ca_probs(k_ref[pl.ds(wss[ci], BAND), :], _head_blocks(q_ref[ci * CHUNK:(ci + 1) * CHUNK, :]),
                            b_ref[...], qi * cpb + ci) for ci in range(cpb)]
            for ci in range(cpb):
                r = lax.dot_general(ps[ci].astype(BF16), v_ref[pl.ds(wss[ci], BAND), :], TN,
                                    preferred_element_type=F32)
                o_ref[ci * CHUNK:(ci + 1) * CHUNK, :] = _head_diag(r)

        _ride_gather(grid, g_refs, sems, compute)

    full = pl.BlockSpec((sp_rows, LANES), lambda p, i: (0, p))
    outs = pl.pallas_call(
        body, grid=grid,
        in_specs=[pl.BlockSpec((t, LANES), lambda p, i: (i + pb, p)), full, full,
                  pl.BlockSpec((None, BAND, LANES), lambda p, i: (p, 0, 0)), pl.BlockSpec(memory_space=pl.ANY)]
        + [HBM_SPEC] * n_g,
        out_specs=[pl.BlockSpec((t, LANES), lambda p, i: (i, n_pairs + p))] + [HBM_SPEC] * n_g,
        out_shape=[jax.ShapeDtypeStruct(mixed.shape, mixed.dtype)]
        + [jax.ShapeDtypeStruct(g.shape, g.dtype) for g in gather],
        input_output_aliases={4: 0, **{5 + i: 1 + i for i in range(n_g)}},
        scratch_shapes=_gather_sems(n_g) if n_g else [],
        compiler_params=_params("arbitrary", "arbitrary"), name=name)(qn, knp, vp, bias_tt, mixed, *gather)
    return outs[0], list(outs[1:])


def _ca_bwd(name, qn, knp, vp, bias_tt, dmixed, n_pairs, col0, exchange=()):
    sp_rows = knp.shape[0]
    s = sp_rows - PAD
    t = min(CA_TILE, s)
    cpb = t // CHUNK
    pb = PAD // t
    n_x = len(exchange)
    grid = (n_pairs, s // t)

    def body(*refs):
        x_in, x_out, sems = refs[5:5 + n_x], refs[9 + n_x:9 + 2 * n_x], refs[9 + 2 * n_x:]
        if n_x:
            @pl.when(_grid_edge(grid, False))
            def _():
                for cp in _exchange_copies(x_in, x_out, *sems):
                    cp.start()

        _ca_bwd_body(*refs[:5], *refs[5 + n_x:9 + n_x])
        if n_x:
            @pl.when(_grid_edge(grid, True))
            def _():
                for cp in _exchange_copies(x_in, x_out, *sems):
                    cp.wait()

    def _ca_bwd_body(q_ref, k_ref, v_ref, b_ref, do_ref, dq_ref, dk_ref, dv_ref, db_ref):
        qi = pl.program_id(1)

        @pl.when(qi == 0)
        def _():
            dk_ref[...] = jnp.zeros_like(dk_ref)
            dv_ref[...] = jnp.zeros_like(dv_ref)
            db_ref[...] = jnp.zeros_like(db_ref)

        wss = [pl.multiple_of((qi * cpb + ci) * CHUNK, CHUNK) for ci in range(cpb)]
        kws = [k_ref[pl.ds(ws, BAND), :] for ws in wss]
        qbs = [_head_blocks(q_ref[ci * CHUNK:(ci + 1) * CHUNK, :]) for ci in range(cpb)]
        dbs = [_head_blocks(do_ref[ci * CHUNK:(ci + 1) * CHUNK, :].astype(BF16)) for ci in range(cpb)]
        ps = [_ca_probs(kws[ci], qbs[ci], b_ref[...], qi * cpb + ci) for ci in range(cpb)]
        dps = [lax.dot_general(v_ref[pl.ds(wss[ci], BAND), :], dbs[ci], NT, preferred_element_type=F32)
               for ci in range(cpb)]
        dss = [p * (dp - jnp.sum(p * dp, axis=0, keepdims=True)) for p, dp in zip(ps, dps)]
        for ci in range(cpb):
            dsb = dss[ci].astype(BF16)
            dv_ref[pl.ds(wss[ci], BAND), :] += jnp.dot(ps[ci].astype(BF16), dbs[ci], preferred_element_type=F32)
            dk_ref[pl.ds(wss[ci], BAND), :] += jnp.dot(dsb, qbs[ci], preferred_element_type=F32) * SCALE
            r = lax.dot_general(dsb, kws[ci], TN, preferred_element_type=F32)
            dq_ref[ci * CHUNK:(ci + 1) * CHUNK, :] = _head_diag(r) * SCALE
        total = dss[0]
        for ci in range(1, cpb):
            total = total + dss[ci]
        db_ref[...] += total

    full_in = pl.BlockSpec((sp_rows, LANES), lambda p, i: (0, p))
    btile = pl.BlockSpec((None, BAND, LANES), lambda p, i: (p, 0, 0))
    pad_shape = jax.ShapeDtypeStruct((sp_rows, n_pairs * LANES), F32)
    outs = pl.pallas_call(
        body, grid=grid,
        in_specs=[pl.BlockSpec((t, LANES), lambda p, i: (i + pb, p)), full_in, full_in, btile,
                  pl.BlockSpec((t, LANES), lambda p, i: (i, col0 + p))] + [HBM_SPEC] * n_x,
        out_specs=[pl.BlockSpec((t, LANES), lambda p, i: (i, p)), full_in, full_in, btile] + [HBM_SPEC] * n_x,
        out_shape=[jax.ShapeDtypeStruct((s, n_pairs * LANES), F32), pad_shape, pad_shape,
                   jax.ShapeDtypeStruct(bias_tt.shape, F32)]
        + [jax.ShapeDtypeStruct((3,) + p.shape[1:], p.dtype) for p in exchange],
        scratch_shapes=_exchange_sems(n_x) if n_x else [],
        compiler_params=_params("arbitrary", "arbitrary"), name=name)(qn, knp, vp, bias_tt, dmixed, *exchange)
    return outs[0], outs[1], outs[2], outs[3], list(outs[4:])


def _pair_transposed(tile):
    n_h = tile.shape[0]
    return jnp.transpose(tile.reshape(n_h // 2, 2, CHUNK, BAND), (0, 3, 1, 2)).reshape(n_h // 2, BAND, LANES)


def _query_major(tile_t):
    n_p = tile_t.shape[0]
    return jnp.transpose(tile_t.reshape(n_p, BAND, 2, CHUNK), (3, 0, 2, 1)).reshape(CHUNK, 2 * n_p, BAND)


def _bias_tile(rb):
    n_h = rb.shape[0]
    far = jnp.broadcast_to(rb[:, 2 * REL_CLIP:], (n_h, PAD + CHUNK - REL_CLIP))
    near = rb[:, REL_CLIP - (CHUNK - 1):2 * REL_CLIP][:, ::-1]
    ext = jnp.concatenate([far, near], axis=1)
    return jnp.stack([ext[:, CHUNK - 1 - i:CHUNK - 1 - i + BAND] for i in range(CHUNK)], axis=1)


def _rel_bias_grad(name, db_t):
    _, n_h, _ = db_t.shape
    n_out = 3 * LANES
    assert CHUNK == 64 and REL_CLIP == 128 and LEFT_CHUNKS == 8

    def body(z_ref, o_ref):
        a_idx = lax.broadcasted_iota(jnp.int32, (LANES, n_out), 0)
        r_idx = lax.broadcasted_iota(jnp.int32, (LANES, n_out), 1)
        out = jnp.zeros((n_h, n_out), F32)
        for mm, base in ((0, 191), (1, 255), (2, 319)):
            b = LEFT_CHUNKS - mm
            acc = jnp.zeros((n_h, LANES), F32)
            for i in range(CHUNK):
                row = z_ref[i, :, b * CHUNK:(b + 1) * CHUNK]
                rowp = jnp.concatenate([row, jnp.zeros((n_h, LANES - CHUNK), F32)], axis=1)
                acc = acc + pltpu.roll(rowp, CHUNK - 1 - i, axis=1)
            place = (r_idx == jnp.minimum(base - a_idx, 2 * REL_CLIP)).astype(BF16)
            out = out + _split_dot(acc, place, 3)
        far = jnp.zeros((n_h, 1), F32)
        for i in range(CHUNK):
            far = far + jnp.sum(z_ref[i, :, 0:(LEFT_CHUNKS - 2) * CHUNK], axis=1, keepdims=True)
        lane = lax.broadcasted_iota(jnp.int32, (n_h, n_out), 1)
        o_ref[...] = out + jnp.where(lane == 2 * REL_CLIP, far, 0.0)

    return pl.pallas_call(
        body, out_shape=jax.ShapeDtypeStruct((n_h, n_out), F32),
        in_specs=[pl.BlockSpec(memory_space=pltpu.VMEM)], out_specs=pl.BlockSpec(memory_space=pltpu.VMEM),
        name=name)(db_t)


def _mesh_pos():
    return lax.axis_index("x"), lax.axis_index("y"), lax.axis_index("c")


def _other_chips(x, y):
    return [(1 - x, y), (x, 1 - y), (1 - x, 1 - y)]


def _all_gather_small(name, blk):
    m_per, n = blk.shape

    def body(x_ref, out_ref, send_sems, recv_sems, local_sem):
        x, y, c = _mesh_pos()
        me, sibling = (x, y, c), (x, y, 1 - c)
        chips = _other_chips(x, y)

        def rows(px, py, pc):
            return out_ref.at[pl.ds((4 * px + 2 * py + pc) * m_per, m_per), :]

        def copy(k, block, to, src=None):
            return pltpu.make_async_remote_copy(
                src_ref=rows(*block) if src is None else src, dst_ref=rows(*block),
                send_sem=send_sems.at[k], recv_sem=recv_sems.at[k], device_id=to, device_id_type=MESH)

        mine = pltpu.make_async_copy(x_ref, rows(*me), local_sem)
        mine.start()
        first = [copy(0, me, sibling, src=x_ref)]
        first += [copy(1 + j, me, (*chip, c), src=x_ref) for j, chip in enumerate(chips)]
        for cp in first:
            cp.start()
        passed = [copy(4 + j, (*chip, c), sibling) for j, chip in enumerate(chips)]
        for j, chip in enumerate(chips):
            copy(1 + j, (*chip, c), me).wait_recv()
            passed[j].start()
        copy(0, sibling, me).wait_recv()
        for j, chip in enumerate(chips):
            copy(4 + j, (*chip, 1 - c), me).wait_recv()
        for cp in first + passed:
            cp.wait_send()
        mine.wait()

    return pl.pallas_call(
        body, out_shape=jax.ShapeDtypeStruct((N_DEV * m_per, n), blk.dtype),
        in_specs=[pl.BlockSpec(memory_space=pltpu.VMEM)], out_specs=pl.BlockSpec(memory_space=pltpu.VMEM),
        scratch_shapes=[pltpu.SemaphoreType.DMA((7,)), pltpu.SemaphoreType.DMA((7,)), pltpu.SemaphoreType.DMA],
        compiler_params=pltpu.CompilerParams(vmem_limit_bytes=VMEM_LIMIT), name=name)(blk)


HBM_SPEC = pl.BlockSpec(memory_space=pltpu.HBM)


def _gather_copies(bufs, send_sems, recv_sems, phase):
    x, y, c = _mesh_pos()
    my_j = 2 * x + y
    starts, lands = [], []
    for t, buf in enumerate(bufs):
        rh = buf.shape[2] // 2
        for k, (px, py) in enumerate(_other_chips(x, y)):
            their_j = 2 * px + py
            if phase == 0:
                src, to = buf.at[0, my_j, pl.ds(c * rh, rh), :], (px, py, c)
                land = buf.at[0, their_j, pl.ds(c * rh, rh), :]
            else:
                src, to = buf.at[0, their_j, pl.ds(c * rh, rh), :], (x, y, 1 - c)
                land = buf.at[0, their_j, pl.ds((1 - c) * rh, rh), :]
            sems = dict(send_sem=send_sems.at[phase, t, k], recv_sem=recv_sems.at[phase, t, k],
                        device_id=to, device_id_type=MESH)
            starts.append(pltpu.make_async_remote_copy(src_ref=src, dst_ref=src, **sems))
            lands.append(pltpu.make_async_remote_copy(src_ref=src, dst_ref=land, **sems))
    return starts, lands


def _gather_sems(n):
    return [pltpu.SemaphoreType.DMA((2, n, 3)), pltpu.SemaphoreType.DMA((2, n, 3))]


def _all_gather_weights(name, bufs, phases):
    n = len(bufs)

    def body(*refs):
        outs = refs[n:2 * n]
        send_sems, recv_sems = refs[2 * n:]
        for phase in phases:
            starts, lands = _gather_copies(outs, send_sems, recv_sems, phase)
            for cp in starts:
                cp.start()
            for cp, ld in zip(starts, lands):
                cp.wait_send()
                ld.wait_recv()

    return pl.pallas_call(
        body, out_shape=[jax.ShapeDtypeStruct(b.shape, b.dtype) for b in bufs],
        in_specs=[HBM_SPEC] * n, out_specs=[HBM_SPEC] * n, input_output_aliases={t: t for t in range(n)},
        scratch_shapes=_gather_sems(n), name=name)(*bufs)


def _gather_buffer(shard, chip):
    r, c = shard.shape
    return lax.dynamic_update_slice(jnp.zeros((1, N_CHIPS, r, c), BF16), shard.astype(BF16)[None, None],
                                    (0, chip, 0, 0))


def _sibling_copies(srcs, outs, send_sems, recv_sems):
    x, y, c = _mesh_pos()
    cps = []
    for t, (src, out) in enumerate(zip(srcs, outs)):
        rh = src.shape[1] // 2
        cps.append(pltpu.make_async_remote_copy(
            src_ref=src.at[:, pl.ds((1 - c) * rh, rh), :], dst_ref=out,
            send_sem=send_sems.at[t], recv_sem=recv_sems.at[t], device_id=(x, y, 1 - c), device_id_type=MESH))
    return cps


def _exchange_sibling_halves(name, grads):
    n = len(grads)

    def body(*refs):
        cps = _sibling_copies(refs[:n], refs[n:2 * n], *refs[2 * n:])
        for cp in cps:
            cp.start()
        for cp in cps:
            cp.wait()

    out_shape = [jax.ShapeDtypeStruct((g.shape[0], g.shape[1] // 2, g.shape[2]), g.dtype) for g in grads]
    return pl.pallas_call(
        body, out_shape=out_shape, in_specs=[HBM_SPEC] * n, out_specs=[HBM_SPEC] * n,
        scratch_shapes=[pltpu.SemaphoreType.DMA((n,)), pltpu.SemaphoreType.DMA((n,))],
        name=name)(*grads)


def _add_halves(name, g, recv, c_idx):
    nb, r, c = g.shape
    rh = r // 2
    tr = rh
    for cand in (512, 256, 128, 64):
        if rh % cand == 0:
            tr = cand
            break
    g4 = g.reshape(nb, 2, rh, c)

    def body(c_ref, g_ref, r_ref, o_ref):
        o_ref[...] = (g_ref[...] + r_ref[...]).astype(BF16)

    grid_spec = pltpu.PrefetchScalarGridSpec(
        num_scalar_prefetch=1, grid=(nb, rh // tr),
        in_specs=[pl.BlockSpec((None, None, tr, c), lambda j, i, cr: (j, cr[0], i, 0)),
                  pl.BlockSpec((None, tr, c), lambda j, i, cr: (j, i, 0))],
        out_specs=pl.BlockSpec((None, tr, c), lambda j, i, cr: (j, i, 0)))
    return pl.pallas_call(
        body, grid_spec=grid_spec, out_shape=jax.ShapeDtypeStruct((nb, rh, c), BF16),
        compiler_params=_params("parallel", "parallel"), name=name)(c_idx, g4, recv)


def _exchange_copies(srcs, outs, send_sems, recv_sems):
    x, y, c = _mesh_pos()
    return [pltpu.make_async_remote_copy(
        src_ref=srcs[t].at[2 * px + py], dst_ref=outs[t].at[k], send_sem=send_sems.at[t, k],
        recv_sem=recv_sems.at[t, k], device_id=(px, py, c), device_id_type=MESH)
        for t in range(len(srcs)) for k, (px, py) in enumerate(_other_chips(x, y))]


def _exchange_sems(n):
    return [pltpu.SemaphoreType.DMA((n, 3)), pltpu.SemaphoreType.DMA((n, 3))]


def _sum_chips(name, own_parts, recvs, pos_idx):
    n_layers = len(own_parts)
    _, rh, c = recvs[0].shape
    tr = rh
    for cand in (512, 256, 128, 64):
        if rh % cand == 0:
            tr = cand
            break
    nb = rh // tr

    def body(pos_ref, *refs):
        own_refs, r_refs, o_ref = refs[:n_layers], refs[n_layers:2 * n_layers], refs[2 * n_layers]
        for l_i in range(n_layers):
            @pl.when(pl.program_id(0) == l_i)
            def _():
                acc = own_refs[l_i][...].astype(F32)
                for k in range(3):
                    acc = acc + r_refs[l_i][k].astype(F32)
                o_ref[...] = acc

    grid_spec = pltpu.PrefetchScalarGridSpec(
        num_scalar_prefetch=1, grid=(n_layers, nb),
        in_specs=[pl.BlockSpec((None, tr, c), lambda l, i, pr: (pr[0], i, 0)) for _ in range(n_layers)]
        + [pl.BlockSpec((3, tr, c), lambda l, i, pr: (0, i, 0)) for _ in range(n_layers)],
        out_specs=pl.BlockSpec((None, tr, c), lambda l, i, pr: (l, pr[1] * nb + i, 0)))
    return pl.pallas_call(
        body, grid_spec=grid_spec, out_shape=jax.ShapeDtypeStruct((n_layers, 2 * rh, c), F32),
        compiler_params=_params("parallel", "parallel"), name=name)(pos_idx, *own_parts, *recvs)


def _share_with_sibling(name, grads):
    n = len(grads)

    def body(*refs):
        outs = refs[n:2 * n]
        send_sems, recv_sems = refs[2 * n:]
        x, y, c = _mesh_pos()
        cps, landed = [], []
        for t in range(n):
            rh = grads[t].shape[1] // 2
            mine = outs[t].at[:, pl.ds(c * rh, rh), :]
            theirs = outs[t].at[:, pl.ds((1 - c) * rh, rh), :]
            cps.append(pltpu.make_async_remote_copy(
                src_ref=mine, dst_ref=mine, send_sem=send_sems.at[t], recv_sem=recv_sems.at[t],
                device_id=(x, y, 1 - c), device_id_type=MESH))
            landed.append(pltpu.make_async_remote_copy(
                src_ref=mine, dst_ref=theirs, send_sem=send_sems.at[t], recv_sem=recv_sems.at[t],
                device_id=(x, y, 1 - c), device_id_type=MESH))
        for cp in cps:
            cp.start()
        for cp, ld in zip(cps, landed):
            cp.wait_send()
            ld.wait_recv()

    out_shape = [jax.ShapeDtypeStruct(g.shape, g.dtype) for g in grads]
    return pl.pallas_call(
        body, out_shape=out_shape, in_specs=[HBM_SPEC] * n, out_specs=[HBM_SPEC] * n,
        input_output_aliases={t: t for t in range(n)},
        scratch_shapes=[pltpu.SemaphoreType.DMA((n,)), pltpu.SemaphoreType.DMA((n,))],
        name=name)(*grads)


def _row(v):
    return v.reshape(1, -1)


def kernel(x, c, g_norm1, w_in, g_q, g_k, rel_bias, w_o, g_norm2, w1, w2, w_ada, b_ada, loss_target, m_g_norm1, m_w_in, m_g_q, m_g_k, m_rel_bias, m_w_o, m_g_norm2, m_w1, m_w2, m_w_ada, m_b_ada, v_g_norm1, v_w_in, v_g_q, v_g_k, v_rel_bias, v_w_o, v_g_norm2, v_w1, v_w2, v_w_ada, v_b_ada):
    _, s, d = x.shape
    n_layers = g_norm1.shape[0]
    n_pairs = (d // 2) // LANES
    n_ca_heads = rel_bias.shape[1]
    n_rel = rel_bias.shape[2]
    assert n_rel == 2 * REL_CLIP + 1 and g_q.shape[1] == HEAD_DIM and n_ca_heads == 2 * n_pairs
    ada_c = w_ada.shape[2]

    ax, ay, ac = _mesh_pos()
    chip = 2 * ax + ay
    dev = 4 * ax + 2 * ay + ac
    c_idx = jnp.reshape(ac, (1,)).astype(jnp.int32)

    c_act_all = _all_gather_small("ag_c", _silu_rows("silu_c", c)).reshape(N_DEV, 8, d)[:, 0, :]
    mod_parts = []
    for l in range(n_layers):
        bias = lax.dynamic_slice_in_dim(b_ada[l], chip * ada_c, ada_c).reshape(1, ada_c)
        mod_parts.append(_ada_mod(f"ada_mod{l}", c_act_all, w_ada[l], bias))
    mod_all = _all_gather_small("ag_mod", jnp.concatenate(mod_parts, axis=1))
    mod_all = mod_all.reshape(N_CHIPS, 2, N_DEV, n_layers, ada_c)[:, 0]
    mod_all = jnp.transpose(mod_all, (1, 2, 0, 3)).reshape(N_DEV, n_layers, N_CHIPS * ada_c)
    mod = lax.dynamic_index_in_dim(mod_all, dev, axis=0, keepdims=False)

    gathered = [[_gather_buffer(w[l], chip) for w in (w_in, w_o, w1, w2)] for l in range(n_layers)]
    gathered[0][:1] = _all_gather_weights("ag_w_in_0", gathered[0][:1], (0, 1))

    xs = x[0]
    saved = []
    for l in range(n_layers):
        sh1, sc1, gt1, sh2, sc2, gt2 = [_row(mod[l, i * d:(i + 1) * d]) for i in range(6)]
        gq2 = _row(jnp.tile(g_q[l], 2))
        gk2 = _row(jnp.tile(g_k[l], 2))
        bias_t = _pair_transposed(_bias_tile(rel_bias[l]))
        h1 = _norm_mod(f"norm1_{l}", xs, _row(g_norm1[l]), sc1, sh1)
        proj = _mm_nn_w(f"proj_{l}", h1, gathered[l][0], 0, True, [F32])[0]
        o_sb, landed = _sb_fwd(f"sb_fwd_{l}", proj, n_pairs, gather=gathered[l][1:] if l == 0 else ())
        if landed:
            gathered[l][1:] = _all_gather_weights(f"ag_rest_{l}", landed, (1,))
        wg_in, wg_o, wg_1, wg_2 = gathered[l]
        nxt = gathered[l + 1] if l + 1 < n_layers else []
        qn, knp, vp = _qk_norm(f"qk_norm_{l}", proj, gq2, gk2, n_pairs)
        mixed, landed = _ca_fwd(f"ca_fwd_{l}", qn, knp, vp, bias_t, n_pairs, o_sb, gather=nxt[:2])

        def res_epi(acc, res, gt):
            return res + gt * acc, acc

        def res_specs(tm, tn):
            return (pl.BlockSpec((tm, tn), lambda i, j, kk: (i, j)), pl.BlockSpec((1, tn), lambda i, j, kk: (0, j)))

        x1, m1 = _mm_nn_w(f"attn_out_{l}", mixed, wg_o, 0, False, [F32, BF16], epi=res_epi,
                          extras=(xs, gt1), extra_specs_fn=res_specs)
        h2 = _norm_mod(f"norm2_{l}", x1, _row(g_norm2[l]), sc2, sh2)

        def act_epi(acc):
            r = jnp.maximum(acc, 0.0)
            return acc, r * r

        if l + 1 < n_layers:
            (u, a), landed_1 = _mm_nn_w(f"mlp_up_{l}", h2, wg_1, 0, True, [BF16, BF16], epi=act_epi, gather=nxt[2:3])
            (x2, m2), landed_2 = _mm_nn_w(f"mlp_down_{l}", a, wg_2, 0, False, [F32, BF16], epi=res_epi,
                                          extras=(x1, gt2), extra_specs_fn=res_specs, gather=nxt[3:])
            gathered[l + 1] = _all_gather_weights(f"ag_weights_{l + 1}", landed + landed_1 + landed_2, (1,))
        else:
            u, a = _mm_nn_w(f"mlp_up_{l}", h2, wg_1, 0, True, [BF16, BF16], epi=act_epi)
            x2, m2 = _mm_nn_w(f"mlp_down_{l}", a, wg_2, 0, False, [F32, BF16], epi=res_epi,
                              extras=(x1, gt2), extra_specs_fn=res_specs)
        saved.append(dict(x0=xs, h1=h1, proj=proj, o_sb=mixed, qn=qn, knp=knp, vp=vp, bias_t=bias_t, mixed=mixed,
                          m1=m1, x1=x1, h2=h2, u=u, a=a, m2=m2, gq2=gq2, gk2=gk2,
                          sc1=sc1, gt1=gt1, sc2=sc2, gt2=gt2))
        xs = x2

    dx, sq = _loss_grad("loss", xs, loss_target[0])
    loss_part = (0.5 * jnp.sum(sq) / d).reshape(1)

    g_in, g_o, g_1, g_2 = [None] * n_layers, [None] * n_layers, [None] * n_layers, [None] * n_layers
    dmod, dg1, dg2, dgq, dgk, drel = [], [], [], [], [], []
    partial, chip_sum, from_chip = {}, {}, {}

    def reduce_on_chip(tag, keys):
        grads = [partial[k] for k in keys]
        for k, g, r in zip(keys, grads, _exchange_sibling_halves(f"rs_sibling_{tag}", grads)):
            chip_sum[k] = _add_halves(f"rs_add_{k[0]}_{k[1]}", g, r, c_idx)

    for l in reversed(range(n_layers)):
        sv = saved[l]
        wg_in, wg_o, wg_1, wg_2 = gathered[l]
        dm2, dgt2 = _gate_bwd(f"gate2_bwd_{l}", dx, sv["m2"], sv["gt2"])

        def act_bwd_epi(acc, u_t):
            return (acc * (2.0 * jnp.maximum(u_t.astype(F32), 0.0)),)

        def tile_specs(tm, tn):
            return (pl.BlockSpec((tm, tn), lambda i, j, kk: (i, j)),)

        above = sorted(k for k in partial if k not in chip_sum) if l == 0 else []
        du = _mm_nt_w(f"mlp_down_bwd_{l}", dm2, wg_2, 0, False, BF16, epi=act_bwd_epi, extras=(sv["u"],),
                      extra_specs_fn=tile_specs, riders=[_sibling_rider([partial[k] for k in above])] if above else ())
        if above:
            du, (from_sibling,) = du
            for k, r in zip(above, from_sibling):
                chip_sum[k] = _add_halves(f"rs_add_{k[0]}_{k[1]}", partial[k], r, c_idx)
        g_2[l] = _mm_tn(f"w2_grad_{l}", sv["a"], dm2, False)
        dh2 = _mm_nt_w(f"mlp_up_bwd_{l}", du, wg_1, 0, True, F32)
        g_1[l] = _mm_tn(f"w1_grad_{l}", sv["h2"], du, True)
        dx1, dsh2, dsc2, dgn2 = _norm_mod_bwd(f"norm2_bwd_{l}", dh2, sv["x1"], _row(g_norm2[l]), sv["sc2"], dx)
        dm1, dgt1 = _gate_bwd(f"gate1_bwd_{l}", dx1, sv["m1"], sv["gt1"])
        dmixed = _mm_nt_w(f"attn_out_bwd_{l}", dm1, wg_o, 0, False, F32)
        g_o[l] = _mm_tn(f"wo_grad_{l}", sv["mixed"], dm1, False)
        partial.update({(1, l): g_o[l], (2, l): g_1[l], (3, l): g_2[l]})
        ride_sb, ride_ca = [], []
        if l == 0:
            early = sorted(partial)
            reduce_on_chip("early", [k for k in early if k not in chip_sum])
            ride_ca = [k for k in early if k[0] == 0 or k == (3, 0)]
            ride_sb = [k for k in early if k not in ride_ca]
        dq_sb, dk_sb, dv_sb, landed = _sb_bwd(f"sb_bwd_{l}", sv["proj"], sv["o_sb"], dmixed, n_pairs,
                                              exchange=[chip_sum[k] for k in ride_sb])
        from_chip.update(zip(ride_sb, landed))
        dqn, dknp, dvp, dbias_t, landed = _ca_bwd(f"ca_bwd_{l}", sv["qn"], sv["knp"], sv["vp"], sv["bias_t"], dmixed,
                                                  n_pairs, n_pairs, exchange=[chip_sum[k] for k in ride_ca])
        from_chip.update(zip(ride_ca, landed))
        dq_ca, dk_ca, dgq_l, dgk_l = _qk_norm_bwd(f"qk_norm_bwd_{l}", dqn, dknp, sv["proj"], sv["gq2"], sv["gk2"],
                                                  n_pairs)
        drel_l = _rel_bias_grad(f"rel_bias_grad_{l}", _query_major(dbias_t))[:, :n_rel]
        dproj = jnp.concatenate([dq_sb.astype(BF16), dk_sb.astype(BF16), dv_sb.astype(BF16), dq_ca, dk_ca,
                                 dvp[PAD:].astype(BF16)], axis=1)
        g_in[l] = _mm_tn(f"win_grad_{l}", sv["h1"], dproj, True)
        partial[(0, l)] = g_in[l]
        if l == 0:
            reduce_on_chip("late", [(0, l)])
            dh1, (landed,) = _mm_nt_w(f"proj_bwd_{l}", dproj, wg_in, 0, True, F32,
                                      riders=[_exchange_rider([chip_sum[(0, l)]])])
            from_chip[(0, l)] = landed[0]
        else:
            dh1 = _mm_nt_w(f"proj_bwd_{l}", dproj, wg_in, 0, True, F32)
        dx, dsh1, dsc1, dgn1 = _norm_mod_bwd(f"norm1_bwd_{l}", dh1, sv["x0"], _row(g_norm1[l]), sv["sc1"], dx1)
        dmod.insert(0, jnp.concatenate([dsh1, dsc1, dgt1, dsh2, dsc2, dgt2], axis=1)[0])
        dg1.insert(0, dgn1[0])
        dg2.insert(0, dgn2[0])
        dgq.insert(0, dgq_l[0, :HEAD_DIM])
        dgk.insert(0, dgk_l[0, :HEAD_DIM])
        drel.insert(0, drel_l.reshape(-1))
    grad_x = dx[None]

    small = jnp.concatenate([jnp.concatenate(dmod), jnp.concatenate(dg1), jnp.concatenate(dg2),
                             jnp.concatenate(dgq), jnp.concatenate(dgk), jnp.concatenate(drel), loss_part])
    n_small = small.shape[0]
    n_pack = -(-n_small // (8 * LANES)) * (8 * LANES)
    small = jnp.pad(small, (0, n_pack - n_small)).reshape(8, n_pack // 8)
    small_all = _all_gather_small("ag_small", small).reshape(N_DEV, 8, n_pack // 8)
    small_sum = _sum_rows8("sum_small", small_all).reshape(-1)
    dmod_all = small_all.reshape(N_DEV, n_pack)[:, :n_layers * 6 * d].reshape(N_DEV, n_layers, 6 * d)

    off = 0

    def take(n, shape):
        nonlocal off
        out = small_sum[off:off + n].reshape(shape)
        off += n
        return out

    grad_b_ada = take(n_layers * 6 * d, (n_layers, 6 * d))
    grad_g_norm1 = take(n_layers * d, (n_layers, d))
    grad_g_norm2 = take(n_layers * d, (n_layers, d))
    grad_g_q = take(n_layers * HEAD_DIM, (n_layers, HEAD_DIM))
    grad_g_k = take(n_layers * HEAD_DIM, (n_layers, HEAD_DIM))
    grad_rel_bias = take(n_layers * n_ca_heads * n_rel, (n_layers, n_ca_heads, n_rel))
    loss = take(1, ())

    c_act_t = jnp.transpose(c_act_all)
    grad_w_ada = jnp.stack([
        _outer_sum(f"wada_grad_{l}", c_act_t,
                   lax.dynamic_slice_in_dim(dmod_all[:, l, :], chip * ada_c, ada_c, axis=1))
        for l in range(n_layers)])

    assert all(k in from_chip for k in partial)
    pos_idx = jnp.stack([chip, ac]).astype(jnp.int32)
    reduced = [_sum_chips(f"rs_sum_{w_i}", [chip_sum[(w_i, l)] for l in range(n_layers)],
                          [from_chip[(w_i, l)] for l in range(n_layers)], pos_idx) for w_i in range(4)]
    grad_w_in, grad_w_o, grad_w1, grad_w2 = _share_with_sibling("rs_share", reduced)

    def adam_big(name, w, g, m, v):
        shp = w.shape
        outs = _adamw(name, w.reshape(-1, shp[-1]), g.reshape(-1, shp[-1]), m.reshape(-1, shp[-1]),
                      v.reshape(-1, shp[-1]))
        return [o.reshape(shp) for o in outs]

    def pack(arrs):
        flat = jnp.concatenate([a.reshape(-1) for a in arrs])
        n = flat.shape[0]
        n_p = -(-n // (8 * LANES)) * (8 * LANES)
        return jnp.pad(flat, (0, n_p - n), constant_values=1.0).reshape(8, n_p // 8)

    small_w = [g_norm1, g_q, g_k, rel_bias, g_norm2, b_ada]
    small_g = [grad_g_norm1, grad_g_q, grad_g_k, grad_rel_bias, grad_g_norm2, grad_b_ada]
    small_m = [m_g_norm1, m_g_q, m_g_k, m_rel_bias, m_g_norm2, m_b_ada]
    small_v = [v_g_norm1, v_g_q, v_g_k, v_rel_bias, v_g_norm2, v_b_ada]
    packed = _adamw("adamw_small", pack(small_w), pack(small_g), pack(small_m), pack(small_v))

    def unpack(p):
        flat = p.reshape(-1)
        res, o = [], 0
        for a in small_w:
            res.append(flat[o:o + a.size].reshape(a.shape))
            o += a.size
        return res

    sd, sm, sv_ = unpack(packed[0]), unpack(packed[1]), unpack(packed[2])
    big = {
        "w_in": adam_big("adamw_w_in", w_in, grad_w_in, m_w_in, v_w_in),
        "w_o": adam_big("adamw_w_o", w_o, grad_w_o, m_w_o, v_w_o),
        "w1": adam_big("adamw_w1", w1, grad_w1, m_w1, v_w1),
        "w2": adam_big("adamw_w2", w2, grad_w2, m_w2, v_w2),
        "w_ada": adam_big("adamw_w_ada", w_ada, grad_w_ada, m_w_ada, v_w_ada),
    }

    def ordered(kind):
        sm_list = (sd, sm, sv_)[kind]
        return [sm_list[0], big["w_in"][kind], sm_list[1], sm_list[2], sm_list[3], big["w_o"][kind], sm_list[4],
                big["w1"][kind], big["w2"][kind], big["w_ada"][kind], sm_list[5]]

    grads = [grad_g_norm1, grad_w_in, grad_g_q, grad_g_k, grad_rel_bias, grad_w_o, grad_g_norm2, grad_w1, grad_w2,
             grad_w_ada, grad_b_ada]
    return (loss, grad_x, *grads, *ordered(0), *ordered(1), *ordered(2))
```

```python
import functools

import jax
import jax.numpy as jnp
from jax import lax
from jax.experimental import pallas as pl
from jax.experimental.pallas import tpu as pltpu

F32 = jnp.float32
BF16 = jnp.bfloat16
MESH = pl.DeviceIdType.MESH

EPS = 1e-6
NEG_INF = -1e30
HEAD_DIM = 64
CHUNK = 64
LEFT_CHUNKS = 8
PAD = LEFT_CHUNKS * CHUNK
BAND = PAD + CHUNK
REL_CLIP = 128
SCALE = HEAD_DIM ** -0.5
DEAD_LOG = -88.0
LANES = 128
N_CHIPS = 4
N_DEV = 8

ADAM_LR = 0.001
ADAM_B1 = 0.9
ADAM_B2 = 0.999
ADAM_EPS = 1e-08
ADAM_WD = 0.01
ADAM_STEP = 10

VMEM_LIMIT = 48 * 1024 * 1024
ROW_TILE = 512
MM_TILE = 1024
SB_TILE = 256
CA_TILE = 512

NT = (((1,), (1,)), ((), ()))
TN = (((0,), (0,)), ((), ()))
NN = (((1,), (0,)), ((), ()))


def _tile(n, pref):
    best = None
    t = LANES
    while t <= min(n, pref):
        if n % t == 0:
            best = t
        t += LANES
    return best if best is not None else n


def _params(*sem):
    return pltpu.CompilerParams(dimension_semantics=sem, vmem_limit_bytes=VMEM_LIMIT)


def _split_dot(x, t, passes):
    acc = None
    r = x
    for i in range(passes):
        hi = r.astype(BF16)
        d = jnp.dot(hi, t, preferred_element_type=F32)
        acc = d if acc is None else acc + d
        if i + 1 < passes:
            r = r - hi.astype(F32)
    return acc


def _grid_edge(grid, last):
    conds = [pl.program_id(i) == (g - 1 if last else 0) for i, g in enumerate(grid)]
    return functools.reduce(jnp.logical_and, conds)


def _ride_gather(grid, g_refs, sems, compute):
    if g_refs:
        @pl.when(_grid_edge(grid, False))
        def _():
            for cp in _gather_copies(g_refs, *sems, 0)[0]:
                cp.start()

    compute()
    if g_refs:
        @pl.when(_grid_edge(grid, True))
        def _():
            for cp, ld in zip(*_gather_copies(g_refs, *sems, 0)):
                cp.wait_send()
                ld.wait_recv()


def _rider(ins, out_shapes, sems, copies, aliased):
    return dict(ins=list(ins), outs=list(out_shapes), sems=list(sems), copies=copies, aliased=aliased)


def _gather_rider(bufs):
    def copies(_, outs, sems):
        def both(cp, ld):
            def wait():
                cp.wait_send()
                ld.wait_recv()
            return wait
        return [(cp, both(cp, ld)) for cp, ld in zip(*_gather_copies(outs, *sems, 0))]

    return _rider(bufs, [jax.ShapeDtypeStruct(b.shape, b.dtype) for b in bufs], _gather_sems(len(bufs)), copies, True)


def _exchange_rider(parts):
    def copies(ins, outs, sems):
        return [(cp, cp.wait) for cp in _exchange_copies(ins, outs, *sems)]

    return _rider(parts, [jax.ShapeDtypeStruct((3,) + p.shape[1:], p.dtype) for p in parts],
                  _exchange_sems(len(parts)), copies, False)


def _sibling_rider(grads):
    def copies(ins, outs, sems):
        return [(cp, cp.wait) for cp in _sibling_copies(ins, outs, *sems)]

    n = len(grads)
    return _rider(grads, [jax.ShapeDtypeStruct((g.shape[0], g.shape[1] // 2, g.shape[2]), g.dtype) for g in grads],
                  [pltpu.SemaphoreType.DMA((n,)), pltpu.SemaphoreType.DMA((n,))], copies, False)


def _matmul(name, a, b, *, dims, grid, a_spec, b_spec, out_shapes, out_specs, acc_shape,
            epi=None, extras=(), extra_specs=(), riders=()):
    nk = grid[2]
    n_ex = len(extras)
    n_out = len(out_shapes)
    n_in = [len(r["ins"]) for r in riders]
    n_ro = [len(r["outs"]) for r in riders]

    def body(*refs):
        a_ref, b_ref = refs[0], refs[1]
        ex = refs[2:2 + n_ex]
        pos = 2 + n_ex
        r_in = []
        for n in n_in:
            r_in.append(refs[pos:pos + n])
            pos += n
        outs = refs[pos:pos + n_out]
        pos += n_out
        r_out = []
        for n in n_ro:
            r_out.append(refs[pos:pos + n])
            pos += n
        acc = refs[pos]
        pos += 1
        r_sems = []
        for r in riders:
            r_sems.append(refs[pos:pos + len(r["sems"])])
            pos += len(r["sems"])

        def all_copies():
            return [c for r, i, o, sm in zip(riders, r_in, r_out, r_sems) for c in r["copies"](i, o, sm)]

        if riders:
            @pl.when(_grid_edge(grid, False))
            def _():
                for cp, _ in all_copies():
                    cp.start()

        k = pl.program_id(2)

        @pl.when(k == 0)
        def _():
            acc[...] = jnp.zeros_like(acc)

        acc[...] += lax.dot_general(a_ref[...].astype(BF16), b_ref[...].astype(BF16), dims,
                                    preferred_element_type=F32)

        @pl.when(k == nk - 1)
        def _():
            res = epi(acc[...], *[e[...] for e in ex]) if epi is not None else (acc[...],)
            for o, r in zip(outs, res):
                o[...] = r.astype(o.dtype)

        if riders:
            @pl.when(_grid_edge(grid, True))
            def _():
                for _, wait in all_copies():
                    wait()

    r_ins = [x for r in riders for x in r["ins"]]
    r_outs = [x for r in riders for x in r["outs"]]
    aliases, i_pos, o_pos = {}, 2 + n_ex, n_out
    for r in riders:
        if r["aliased"]:
            aliases.update({i_pos + i: o_pos + i for i in range(len(r["ins"]))})
        i_pos += len(r["ins"])
        o_pos += len(r["outs"])
    outs = pl.pallas_call(
        body, grid=grid, in_specs=[a_spec, b_spec, *extra_specs] + [HBM_SPEC] * len(r_ins),
        out_specs=list(out_specs) + [HBM_SPEC] * len(r_outs), out_shape=list(out_shapes) + r_outs,
        input_output_aliases=aliases,
        scratch_shapes=[pltpu.VMEM(acc_shape, F32)] + [sm for r in riders for sm in r["sems"]],
        compiler_params=_params(*(("arbitrary",) * 3 if riders else ("parallel", "parallel", "arbitrary"))), name=name,
    )(a, b, *extras, *r_ins)
    landed, pos = [], n_out
    for n in n_ro:
        landed.append(list(outs[pos:pos + n]))
        pos += n
    return list(outs[:n_out]), landed


def _mm_nn_w(name, a, wg, layer, colshard, out_dtypes, epi=None, extras=(), extra_specs_fn=None, gather=()):
    m = a.shape[0]
    _, _, r, c = wg.shape
    tm = min(MM_TILE, m)
    if colshard:
        k, n = r, N_CHIPS * c
        tk, tn = k, _tile(c, MM_TILE)
        npc = c // tn
        b_spec = pl.BlockSpec((None, None, tk, tn), lambda i, j, kk: (layer, j // npc, kk, j % npc))
    else:
        k, n = N_CHIPS * r, c
        wg = wg.reshape(wg.shape[0], 1, k, c)
        tk, tn = _tile(k, MM_TILE), _tile(c, MM_TILE)
        b_spec = pl.BlockSpec((None, None, tk, tn), lambda i, j, kk: (layer, 0, kk, j))
    assert a.shape[1] == k
    grid = (m // tm, n // tn, k // tk)
    out_spec = pl.BlockSpec((tm, tn), lambda i, j, kk: (i, j))
    ex_specs = extra_specs_fn(tm, tn) if extra_specs_fn is not None else ()
    outs, landed = _matmul(name, a, wg, dims=NN, grid=grid,
                           a_spec=pl.BlockSpec((tm, tk), lambda i, j, kk: (i, kk)), b_spec=b_spec,
                           out_shapes=[jax.ShapeDtypeStruct((m, n), d) for d in out_dtypes],
                           out_specs=[out_spec for _ in out_dtypes], acc_shape=(tm, tn),
                           epi=epi, extras=extras, extra_specs=ex_specs,
                           riders=[_gather_rider(gather)] if gather else ())
    return (outs, landed[0]) if gather else outs


def _mm_nt_w(name, a, wg, layer, colshard, out_dtype, epi=None, extras=(), extra_specs_fn=None, riders=()):
    m = a.shape[0]
    _, _, r, c = wg.shape
    tm = min(MM_TILE, m)
    if colshard:
        kdim, n = N_CHIPS * c, r
        tk, tn = _tile(c, MM_TILE), _tile(r, MM_TILE)
        kpc = c // tk
        b_spec = pl.BlockSpec((None, None, tn, tk), lambda i, j, kk: (layer, kk // kpc, j, kk % kpc))
    else:
        kdim, n = c, N_CHIPS * r
        wg = wg.reshape(wg.shape[0], 1, n, c)
        tk, tn = c, _tile(n, MM_TILE)
        b_spec = pl.BlockSpec((None, None, tn, tk), lambda i, j, kk: (layer, 0, j, kk))
    assert a.shape[1] == kdim
    grid = (m // tm, n // tn, kdim // tk)
    ex_specs = extra_specs_fn(tm, tn) if extra_specs_fn is not None else ()
    outs, landed = _matmul(name, a, wg, dims=NT, grid=grid,
                   a_spec=pl.BlockSpec((tm, tk), lambda i, j, kk: (i, kk)), b_spec=b_spec,
                   out_shapes=[jax.ShapeDtypeStruct((m, n), out_dtype)],
                   out_specs=[pl.BlockSpec((tm, tn), lambda i, j, kk: (i, j))], acc_shape=(tm, tn),
                   epi=epi, extras=extras, extra_specs=ex_specs, riders=riders)
    return (outs[0], landed) if riders else outs[0]


def _mm_tn(name, a, b, colshard):
    s, m = a.shape
    n = b.shape[1]
    tk = min(2 * MM_TILE, s)
    if colshard:
        c = n // N_CHIPS
        tm, tn = _tile(m, MM_TILE), _tile(c, MM_TILE)
        npc = c // tn
        out_shape = (N_CHIPS, m, c)
        out_spec = pl.BlockSpec((None, tm, tn), lambda i, j, kk: (j // npc, i, j % npc))
    else:
        r = m // N_CHIPS
        tm, tn = _tile(r, MM_TILE), _tile(n, MM_TILE)
        mpc = r // tm
        out_shape = (N_CHIPS, r, n)
        out_spec = pl.BlockSpec((None, tm, tn), lambda i, j, kk: (i // mpc, i % mpc, j))
    grid = (m // tm, n // tn, s // tk)
    return _matmul(name, a, b, dims=TN, grid=grid,
                   a_spec=pl.BlockSpec((tk, tm), lambda i, j, kk: (kk, i)),
                   b_spec=pl.BlockSpec((tk, tn), lambda i, j, kk: (kk, j)),
                   out_shapes=[jax.ShapeDtypeStruct(out_shape, F32)], out_specs=[out_spec],
                   acc_shape=(tm, tn))[0][0]


def _row_spec(ts, d):
    return pl.BlockSpec((ts, d), lambda i: (i, 0))


def _vec_spec(d):
    return pl.BlockSpec((1, d), lambda i: (0, 0))


def _norm_mod(name, x, g, sc, sh):
    s, d = x.shape
    ts = min(ROW_TILE, s)

    def body(x_ref, g_ref, sc_ref, sh_ref, h_ref):
        xv = x_ref[...]
        r = lax.rsqrt(jnp.mean(xv * xv, axis=-1, keepdims=True) + EPS)
        h_ref[...] = (xv * r * g_ref[...] * (1.0 + sc_ref[...]) + sh_ref[...]).astype(BF16)

    return pl.pallas_call(
        body, grid=(s // ts,), in_specs=[_row_spec(ts, d), _vec_spec(d), _vec_spec(d), _vec_spec(d)],
        out_specs=_row_spec(ts, d), out_shape=jax.ShapeDtypeStruct((s, d), BF16),
        compiler_params=_params("parallel"), name=name)(x, g, sc, sh)


def _norm_mod_bwd(name, dh, x, g, sc, dres):
    s, d = x.shape
    ts = min(ROW_TILE, s)

    def body(dh_ref, x_ref, g_ref, sc_ref, dres_ref, dx_ref, dsh_ref, dsc_ref, dg_ref):
        @pl.when(pl.program_id(0) == 0)
        def _():
            dsh_ref[...] = jnp.zeros_like(dsh_ref)
            dsc_ref[...] = jnp.zeros_like(dsc_ref)
            dg_ref[...] = jnp.zeros_like(dg_ref)

        xv = x_ref[...]
        r = lax.rsqrt(jnp.mean(xv * xv, axis=-1, keepdims=True) + EPS)
        xhat = xv * r
        dhv = dh_ref[...]
        gv = g_ref[...]
        opsc = 1.0 + sc_ref[...]
        dxhat = dhv * (gv * opsc)
        mdot = jnp.mean(dxhat * xhat, axis=-1, keepdims=True)
        dx_ref[...] = dres_ref[...] + r * (dxhat - xhat * mdot)
        dhx = dhv * xhat
        dsh_ref[...] += jnp.sum(dhv, axis=0, keepdims=True)
        dsc_ref[...] += jnp.sum(dhx * gv, axis=0, keepdims=True)
        dg_ref[...] += jnp.sum(dhx * opsc, axis=0, keepdims=True)

    vec = jax.ShapeDtypeStruct((1, d), F32)
    return pl.pallas_call(
        body, grid=(s // ts,),
        in_specs=[_row_spec(ts, d), _row_spec(ts, d), _vec_spec(d), _vec_spec(d), _row_spec(ts, d)],
        out_specs=[_row_spec(ts, d), _vec_spec(d), _vec_spec(d), _vec_spec(d)],
        out_shape=[jax.ShapeDtypeStruct((s, d), F32), vec, vec, vec],
        compiler_params=_params("arbitrary"), name=name)(dh, x, g, sc, dres)


def _gate_bwd(name, dx, m, gt):
    s, d = dx.shape
    ts = min(ROW_TILE, s)

    def body(dx_ref, m_ref, gt_ref, dm_ref, dgt_ref):
        @pl.when(pl.program_id(0) == 0)
        def _():
            dgt_ref[...] = jnp.zeros_like(dgt_ref)

        dxv = dx_ref[...]
        dm_ref[...] = (dxv * gt_ref[...]).astype(BF16)
        dgt_ref[...] += jnp.sum(dxv * m_ref[...].astype(F32), axis=0, keepdims=True)

    return pl.pallas_call(
        body, grid=(s // ts,), in_specs=[_row_spec(ts, d), _row_spec(ts, d), _vec_spec(d)],
        out_specs=[_row_spec(ts, d), _vec_spec(d)],
        out_shape=[jax.ShapeDtypeStruct((s, d), BF16), jax.ShapeDtypeStruct((1, d), F32)],
        compiler_params=_params("arbitrary"), name=name)(dx, m, gt)


def _loss_grad(name, y, target):
    s, d = y.shape
    ts = min(ROW_TILE, s)

    def body(y_ref, t_ref, dy_ref, sq_ref):
        @pl.when(pl.program_id(0) == 0)
        def _():
            sq_ref[...] = jnp.zeros_like(sq_ref)

        e = y_ref[...] - t_ref[...]
        dy_ref[...] = e * (1.0 / d)
        sq_ref[...] += jnp.sum(e * e, axis=0, keepdims=True)

    return pl.pallas_call(
        body, grid=(s // ts,), in_specs=[_row_spec(ts, d), _row_spec(ts, d)],
        out_specs=[_row_spec(ts, d), _vec_spec(d)],
        out_shape=[jax.ShapeDtypeStruct((s, d), F32), jax.ShapeDtypeStruct((1, d), F32)],
        compiler_params=_params("arbitrary"), name=name)(y, target)


def _adamw(name, w, g, m, v):
    r, c = w.shape
    tr = r
    for cand in (512, 256, 128, 64, 32, 16, 8):
        if r % cand == 0 and cand * c * 4 <= 2 * 1024 * 1024:
            tr = cand
            break

    def body(w_ref, g_ref, m_ref, v_ref, d_ref, mo_ref, vo_ref):
        gv = g_ref[...]
        m2 = ADAM_B1 * m_ref[...] + (1.0 - ADAM_B1) * gv
        v2 = ADAM_B2 * v_ref[...] + (1.0 - ADAM_B2) * (gv * gv)
        m_hat = m2 / (1.0 - ADAM_B1 ** ADAM_STEP)
        v_hat = v2 / (1.0 - ADAM_B2 ** ADAM_STEP)
        d_ref[...] = -ADAM_LR * (m_hat / (jnp.sqrt(v_hat) + ADAM_EPS) + ADAM_WD * w_ref[...])
        mo_ref[...] = m2
        vo_ref[...] = v2

    spec = pl.BlockSpec((tr, c), lambda i: (i, 0))
    shp = jax.ShapeDtypeStruct((r, c), F32)
    return pl.pallas_call(
        body, grid=(r // tr,), in_specs=[spec] * 4, out_specs=[spec] * 3, out_shape=[shp] * 3,
        compiler_params=_params("parallel"), name=name)(w, g, m, v)


def _sum_rows8(name, x):
    n, r, c = x.shape

    def body(x_ref, o_ref):
        acc = x_ref[0]
        for i in range(1, n):
            acc = acc + x_ref[i]
        o_ref[...] = acc

    return pl.pallas_call(
        body, out_shape=jax.ShapeDtypeStruct((r, c), F32),
        in_specs=[pl.BlockSpec(memory_space=pltpu.VMEM)], out_specs=pl.BlockSpec(memory_space=pltpu.VMEM),
        name=name)(x)


def _outer_sum(name, ct, dm):
    d, n_seq = ct.shape
    n = dm.shape[1]
    tr = min(256, d)

    def body(ct_ref, dm_ref, o_ref):
        acc = ct_ref[:, 0:1] * dm_ref[0:1, :]
        for s_i in range(1, n_seq):
            acc = acc + ct_ref[:, s_i:s_i + 1] * dm_ref[s_i:s_i + 1, :]
        o_ref[...] = acc

    return pl.pallas_call(
        body, grid=(d // tr,),
        in_specs=[pl.BlockSpec((tr, n_seq), lambda i: (i, 0)), pl.BlockSpec((n_seq, n), lambda i: (0, 0))],
        out_specs=pl.BlockSpec((tr, n), lambda i: (i, 0)), out_shape=jax.ShapeDtypeStruct((d, n), F32),
        compiler_params=_params("parallel"), name=name)(ct, dm)


def _ada_mod(name, c_act, w_ada_l, bias):
    n_seq, d = c_act.shape
    n = w_ada_l.shape[1]
    tn = _tile(n, 768)

    def epi(acc, b):
        return (acc + b,)

    return _matmul(name, c_act, w_ada_l, dims=NN, grid=(1, n // tn, 1),
                   a_spec=pl.BlockSpec((n_seq, d), lambda i, j, kk: (0, 0)),
                   b_spec=pl.BlockSpec((d, tn), lambda i, j, kk: (0, j)),
                   out_shapes=[jax.ShapeDtypeStruct((n_seq, n), F32)],
                   out_specs=[pl.BlockSpec((n_seq, tn), lambda i, j, kk: (0, j))], acc_shape=(n_seq, tn),
                   epi=epi, extras=(bias,), extra_specs=(pl.BlockSpec((1, tn), lambda i, j, kk: (0, j)),))[0][0]


def _silu_rows(name, c_row):
    d = c_row.shape[1]

    def body(c_ref, o_ref):
        cv = c_ref[...]
        o_ref[...] = jnp.broadcast_to(cv / (1.0 + jnp.exp(-cv)), (8, d))

    return pl.pallas_call(
        body, out_shape=jax.ShapeDtypeStruct((8, d), F32),
        in_specs=[pl.BlockSpec(memory_space=pltpu.VMEM)], out_specs=pl.BlockSpec(memory_space=pltpu.VMEM),
        name=name)(c_row)


def _sb_masks(t):
    row = lax.broadcasted_iota(jnp.int32, (t, t), 0)
    col = lax.broadcasted_iota(jnp.int32, (t, t), 1)
    strict = col < row
    tri = (row > col).astype(BF16)
    return strict, tri


def _col_minus_row(t):
    return lax.broadcasted_iota(jnp.int32, (t, t), 1) - lax.broadcasted_iota(jnp.int32, (t, t), 0)


def _sb_block(qi, i, sb):
    return qi - 2 * i - (1 - sb)


def _sb_keep(diff, qi, kb, t):
    return diff < jnp.where(kb >= 0, (qi - kb) * t, -t)


def _sweep_left(step, qi, carry, runs_of):
    def alive(c):
        runs = runs_of(c)
        top = runs[0]
        for r in runs[1:]:
            top = jnp.maximum(top, r)
        return jnp.max(top)

    n_plain = jnp.maximum(qi - 1, 0) // 2

    def cond(state):
        i, _, top = state
        return jnp.logical_and(i <= n_plain, top > DEAD_LOG)

    def body(state):
        i, c, _ = state
        c = step(i, c, False)
        return i + 1, c, alive(c)

    _, carry, top = lax.while_loop(cond, body, (1, carry, alive(carry)))
    lone = jnp.logical_and(jnp.logical_and(qi >= 2, qi % 2 == 0), top > DEAD_LOG)
    return lax.cond(lone, lambda c: step(qi // 2, c, True), lambda c: c, carry)


def _log_sigmoids(z):
    sp = jnp.log(1.0 + jnp.exp(-jnp.abs(z)))
    return jnp.minimum(z, 0.0) - sp, jnp.minimum(-z, 0.0) - sp, sp


def _sb_fwd(name, proj, n_pairs, gather=()):
    s = proj.shape[0]
    t = min(SB_TILE, s)
    nq = s // t
    n_g = len(gather)

    heads = [slice(h * HEAD_DIM, (h + 1) * HEAD_DIM) for h in range(LANES // HEAD_DIM)]

    assert nq % 2 == 0

    def body(*refs):
        q_ref, k_ref, v_ref = refs[:3]
        o_ref = refs[3 + n_g]
        g_refs, sems = refs[4 + n_g:4 + 2 * n_g], refs[4 + 2 * n_g:]
        _ride_gather((n_pairs, nq), g_refs, sems, lambda: _sb_fwd_body(q_ref, k_ref, v_ref, o_ref))

    def _sb_fwd_body(q_ref, k_ref, v_ref, o_ref):
        qi = pl.program_id(1)
        _, tri = _sb_masks(t)
        diff = _col_minus_row(t)
        qs = [(q_ref[:, sl] * SCALE).astype(BF16) for sl in heads]
        chains = [(sb, h) for sb in (1, 0) for h in range(len(heads))]

        def step(j, carry, masked):
            kbs = [_sb_block(qi, j, sb) for sb, _ in chains]
            ks = [pl.multiple_of(jnp.maximum(kb, 0) * t, t) for kb in kbs]
            k_ = [k_ref[pl.ds(ks[i], t), heads[h]].astype(BF16) for i, (_, h) in enumerate(chains)]
            v_ = [v_ref[pl.ds(ks[i], t), heads[h]].astype(BF16) for i, (_, h) in enumerate(chains)]
            zs = [lax.dot_general(qs[h], k_[i], NT, preferred_element_type=F32) for i, (_, h) in enumerate(chains)]
            lbs, lms, keeps = [], [], []
            for kb, z in zip(kbs, zs):
                lb, lm, _ = _log_sigmoids(z)
                keep = None
                if masked:
                    keep = _sb_keep(diff, qi, kb, t)
                    lm = jnp.where(keep, lm, 0.0)
                lbs.append(lb)
                lms.append(lm)
                keeps.append(keep)
            css = [_split_dot(lm, tri, 2) for lm in lms]
            sums = [jnp.sum(lm, axis=1, keepdims=True) for lm in lms]
            ws = []
            for i, (sb, h) in enumerate(chains):
                run = carry[h][1] if sb == 1 else carry[h][1] + sums[h]
                w = jnp.exp(lbs[i] + css[i] + run)
                if masked:
                    w = jnp.where(keeps[i], w, 0.0)
                ws.append(w.astype(BF16))
            pvs = [jnp.dot(w, v, preferred_element_type=F32) for w, v in zip(ws, v_)]
            nh = len(heads)
            return tuple((carry[h][0] + pvs[h] + pvs[nh + h], carry[h][1] + sums[h] + sums[nh + h]) for h in range(nh))

        init = tuple((jnp.zeros((t, HEAD_DIM), F32), jnp.zeros((t, 1), F32)) for _ in heads)
        carry = step(0, init, True)
        carry = _sweep_left(step, qi, carry, lambda c: [run for _, run in c])
        for (acc, _), sl in zip(carry, heads):
            o_ref[:, sl] = acc

    outs = pl.pallas_call(
        body, grid=(n_pairs, nq),
        in_specs=[pl.BlockSpec((t, LANES), lambda p, i: (i, p)),
                  pl.BlockSpec((s, LANES), lambda p, i: (0, n_pairs + p)),
                  pl.BlockSpec((s, LANES), lambda p, i: (0, 2 * n_pairs + p))] + [HBM_SPEC] * n_g,
        out_specs=[pl.BlockSpec((t, LANES), lambda p, i: (i, p))] + [HBM_SPEC] * n_g,
        out_shape=[jax.ShapeDtypeStruct((s, 2 * n_pairs * LANES), F32)]
        + [jax.ShapeDtypeStruct(g.shape, g.dtype) for g in gather],
        input_output_aliases={3 + i: 1 + i for i in range(n_g)},
        scratch_shapes=_gather_sems(n_g) if n_g else [],
        compiler_params=_params("arbitrary", "arbitrary"), name=name)(proj, proj, proj, *gather)
    return outs[0], list(outs[1:])


def _sb_bwd(name, proj, o_sb, dmixed, n_pairs, exchange=()):
    s = proj.shape[0]
    t = min(SB_TILE, s)
    nq = s // t
    n_x = len(exchange)
    heads = [slice(h * HEAD_DIM, (h + 1) * HEAD_DIM) for h in range(LANES // HEAD_DIM)]

    def body(*refs):
        x_in, x_out, sems = refs[5:5 + n_x], refs[8 + n_x:8 + 2 * n_x], refs[8 + 2 * n_x:]
        if n_x:
            @pl.when(_grid_edge((n_pairs, nq), False))
            def _():
                for cp in _exchange_copies(x_in, x_out, *sems):
                    cp.start()

        _sb_bwd_body(*refs[:5], *refs[5 + n_x:8 + n_x])
        if n_x:
            @pl.when(_grid_edge((n_pairs, nq), True))
            def _():
                for cp in _exchange_copies(x_in, x_out, *sems):
                    cp.wait()

    def _sb_bwd_body(q_ref, k_ref, v_ref, o_ref, do_ref, dq_ref, dk_ref, dv_ref):
        qi = pl.program_id(1)

        @pl.when(qi == 0)
        def _():
            dk_ref[...] = jnp.zeros_like(dk_ref)
            dv_ref[...] = jnp.zeros_like(dv_ref)

        _, tri = _sb_masks(t)
        diff = _col_minus_row(t)
        qs = [(q_ref[:, sl] * SCALE).astype(BF16) for sl in heads]
        dos = [do_ref[:, sl].astype(BF16) for sl in heads]
        dsums = [jnp.sum(do.astype(F32) * o_ref[:, sl], axis=1, keepdims=True) for do, sl in zip(dos, heads)]
        nh = len(heads)
        chains = [(sb, h) for sb in (1, 0) for h in range(nh)]

        def step(j, carry, masked):
            kbs = [_sb_block(qi, j, sb) for sb, _ in chains]
            ks = [pl.multiple_of(jnp.maximum(kb, 0) * t, t) for kb in kbs]
            k_ = [k_ref[pl.ds(ks[i], t), heads[h]].astype(BF16) for i, (_, h) in enumerate(chains)]
            v_ = [v_ref[pl.ds(ks[i], t), heads[h]].astype(BF16) for i, (_, h) in enumerate(chains)]
            zs = [lax.dot_general(qs[h], k_[i], NT, preferred_element_type=F32) for i, (_, h) in enumerate(chains)]
            dws = [lax.dot_general(dos[h], v_[i], NT, preferred_element_type=F32) for i, (_, h) in enumerate(chains)]
            lbs, lms, betas, ombs, keeps = [], [], [], [], []
            for kb, z in zip(kbs, zs):
                ez = jnp.exp(-jnp.abs(z))
                lb = jnp.minimum(z, 0.0) - jnp.log(1.0 + ez)
                lm = lb - z
                betas.append(jnp.exp(lb))
                ombs.append(jnp.exp(lm))
                keep = None
                if masked:
                    keep = _sb_keep(diff, qi, kb, t)
                    lm = jnp.where(keep, lm, 0.0)
                lbs.append(lb)
                lms.append(lm)
                keeps.append(keep)
            css = [_split_dot(lm, tri, 2) for lm in lms]
            sums_f = [jnp.sum(lm, axis=1, keepdims=True) for lm in lms]
            wbs, es = [], []
            for i, (sb, h) in enumerate(chains):
                run_f = carry[h][1] if sb == 1 else carry[h][1] + sums_f[h]
                w = jnp.exp(lbs[i] + css[i] + run_f)
                if masked:
                    w = jnp.where(keeps[i], w, 0.0)
                wb = w.astype(BF16)
                wbs.append(wb)
                es.append(wb.astype(F32) * dws[i])
            for i, (_, h) in enumerate(chains):
                dv_ref[pl.ds(ks[i], t), heads[h]] += lax.dot_general(wbs[i], dos[h], TN, preferred_element_type=F32)
            ecs = [_split_dot(e, tri, 2) for e in es]
            sums_e = [jnp.sum(e, axis=1, keepdims=True) for e in es]
            dzbs = []
            for i, (sb, h) in enumerate(chains):
                run_e = carry[h][2] if sb == 1 else carry[h][2] + sums_e[h]
                left = dsums[h] - (ecs[i] + es[i] + run_e)
                dz = es[i] * ombs[i] - left * betas[i]
                if masked:
                    dz = jnp.where(keeps[i], dz, 0.0)
                dzbs.append(dz.astype(BF16))
            dqs = []
            for i, (_, h) in enumerate(chains):
                dk_ref[pl.ds(ks[i], t), heads[h]] += lax.dot_general(dzbs[i], qs[h], TN, preferred_element_type=F32)
                dqs.append(jnp.dot(dzbs[i], k_[i], preferred_element_type=F32))
            return tuple((carry[h][0] + dqs[h] + dqs[nh + h], carry[h][1] + sums_f[h] + sums_f[nh + h],
                          carry[h][2] + sums_e[h] + sums_e[nh + h]) for h in range(nh))

        zero = jnp.zeros((t, 1), F32)
        init = tuple((jnp.zeros((t, HEAD_DIM), F32), zero, zero) for _ in heads)
        carry = step(0, init, True)
        carry = _sweep_left(step, qi, carry, lambda c: [run_f for _, run_f, _ in c])
        for (dq, _, _), sl in zip(carry, heads):
            dq_ref[:, sl] = dq * SCALE

    blk = pl.BlockSpec((t, LANES), lambda p, i: (i, p))
    full = pl.BlockSpec((s, LANES), lambda p, i: (0, p))
    shp = jax.ShapeDtypeStruct((s, n_pairs * LANES), F32)
    outs = pl.pallas_call(
        body, grid=(n_pairs, nq),
        in_specs=[blk,
                  pl.BlockSpec((s, LANES), lambda p, i: (0, n_pairs + p)),
                  pl.BlockSpec((s, LANES), lambda p, i: (0, 2 * n_pairs + p)),
                  blk, blk] + [HBM_SPEC] * n_x,
        out_specs=[blk, full, full] + [HBM_SPEC] * n_x,
        out_shape=[shp, shp, shp] + [jax.ShapeDtypeStruct((3,) + p.shape[1:], p.dtype) for p in exchange],
        scratch_shapes=_exchange_sems(n_x) if n_x else [],
        compiler_params=_params("arbitrary", "arbitrary"), name=name)(proj, proj, proj, o_sb, dmixed, *exchange)
    return outs[0], outs[1], outs[2], list(outs[3:])


def _group_mean_matrix():
    row = lax.broadcasted_iota(jnp.int32, (LANES, LANES), 0)
    col = lax.broadcasted_iota(jnp.int32, (LANES, LANES), 1)
    same_head = (row < HEAD_DIM) == (col < HEAD_DIM)
    return jnp.where(same_head, 1.0 / HEAD_DIM, 0.0).astype(BF16)


def _qk_norm(name, proj, gq2, gk2, n_pairs):
    s = proj.shape[0]
    ts = min(ROW_TILE, s)
    pb = PAD // ts
    c0 = 3 * n_pairs

    def body(q_ref, k_ref, v_ref, gq_ref, gk_ref, qn_ref, kn_ref, vp_ref):
        r_i = pl.program_id(1)

        @pl.when(r_i < pb)
        def _():
            qn_ref[...] = jnp.zeros_like(qn_ref)
            kn_ref[...] = jnp.zeros_like(kn_ref)
            vp_ref[...] = jnp.zeros_like(vp_ref)

        @pl.when(r_i >= pb)
        def _():
            gm = _group_mean_matrix()
            for src, g_ref, dst in ((q_ref, gq_ref, qn_ref), (k_ref, gk_ref, kn_ref)):
                xv = src[...]
                r = lax.rsqrt(_split_dot(xv * xv, gm, 2) + EPS)
                dst[...] = (xv * r * g_ref[...]).astype(BF16)
            vp_ref[...] = v_ref[...].astype(BF16)

    def in_spec(off):
        return pl.BlockSpec((ts, LANES), lambda p, i: (jnp.maximum(i - pb, 0), c0 + off * n_pairs + p))

    gspec = pl.BlockSpec((1, LANES), lambda p, i: (0, 0))
    ospec = pl.BlockSpec((ts, LANES), lambda p, i: (i, p))
    shp = jax.ShapeDtypeStruct((s + PAD, n_pairs * LANES), BF16)
    return pl.pallas_call(
        body, grid=(n_pairs, (s + PAD) // ts),
        in_specs=[in_spec(0), in_spec(1), in_spec(2), gspec, gspec],
        out_specs=[ospec, ospec, ospec], out_shape=[shp, shp, shp],
        compiler_params=_params("parallel", "arbitrary"), name=name)(proj, proj, proj, gq2, gk2)


def _qk_norm_bwd(name, dqn, dkn, proj, gq2, gk2, n_pairs):
    s = proj.shape[0]
    ts = min(ROW_TILE, s)
    pb = PAD // ts
    c0 = 3 * n_pairs
    n_r = s // ts

    def body(dqn_ref, dkn_ref, q_ref, k_ref, gq_ref, gk_ref, dq_ref, dk_ref, dgq_ref, dgk_ref):
        first = jnp.logical_and(pl.program_id(0) == 0, pl.program_id(1) == 0)
        last = jnp.logical_and(pl.program_id(0) == n_pairs - 1, pl.program_id(1) == n_r - 1)

        @pl.when(first)
        def _():
            dgq_ref[...] = jnp.zeros_like(dgq_ref)
            dgk_ref[...] = jnp.zeros_like(dgk_ref)

        gm = _group_mean_matrix()
        for dn_ref, x_ref, g_ref, dx_ref, dg_ref in ((dqn_ref, q_ref, gq_ref, dq_ref, dgq_ref),
                                                     (dkn_ref, k_ref, gk_ref, dk_ref, dgk_ref)):
            xv = x_ref[...]
            r = lax.rsqrt(_split_dot(xv * xv, gm, 2) + EPS)
            xhat = xv * r
            dn = dn_ref[...]
            tg = dn * g_ref[...]
            md = _split_dot(tg * xhat, gm, 2)
            dx_ref[...] = (r * (tg - xhat * md)).astype(BF16)
            dg_ref[...] += jnp.sum(dn * xhat, axis=0, keepdims=True)

        @pl.when(last)
        def _():
            for dg_ref in (dgq_ref, dgk_ref):
                gv = dg_ref[...]
                dg_ref[...] = gv + pltpu.roll(gv, HEAD_DIM, axis=1)

    def x_spec(off):
        return pl.BlockSpec((ts, LANES), lambda p, i: (i, c0 + off * n_pairs + p))

    dn_q = pl.BlockSpec((ts, LANES), lambda p, i: (i, p))
    dn_k = pl.BlockSpec((ts, LANES), lambda p, i: (i + pb, p))
    gspec = pl.BlockSpec((1, LANES), lambda p, i: (0, 0))
    ospec = pl.BlockSpec((ts, LANES), lambda p, i: (i, p))
    shp = jax.ShapeDtypeStruct((s, n_pairs * LANES), BF16)
    vec = jax.ShapeDtypeStruct((1, LANES), F32)
    return pl.pallas_call(
        body, grid=(n_pairs, n_r),
        in_specs=[dn_q, dn_k, x_spec(0), x_spec(1), gspec, gspec],
        out_specs=[ospec, ospec, gspec, gspec], out_shape=[shp, shp, vec, vec],
        compiler_params=_params("arbitrary", "arbitrary"), name=name)(dqn, dkn, proj, proj, gq2, gk2)


def _head_blocks(x2):
    lane = lax.broadcasted_iota(jnp.int32, x2.shape, 1)
    zero = jnp.zeros_like(x2)
    return jnp.concatenate([jnp.where(lane < HEAD_DIM, x2, zero), jnp.where(lane >= HEAD_DIM, x2, zero)], axis=0)


def _head_diag(r):
    lane = lax.broadcasted_iota(jnp.int32, (CHUNK, LANES), 1)
    return jnp.where(lane < HEAD_DIM, r[:CHUNK], r[CHUNK:])


def _ca_probs(kw, qb, bias_t, n):
    sc = lax.dot_general(kw, qb, NT, preferred_element_type=F32) * SCALE + bias_t
    jpos = lax.broadcasted_iota(jnp.int32, (BAND, LANES), 0)
    sc = jnp.where(n * CHUNK + jpos >= PAD, sc, NEG_INF)
    p = jnp.exp(sc - jnp.max(sc, axis=0, keepdims=True))
    return p * (1.0 / jnp.sum(p, axis=0, keepdims=True))


def _ca_fwd(name, qn, knp, vp, bias_tt, n_pairs, mixed, gather=()):
    sp_rows = knp.shape[0]
    s = sp_rows - PAD
    t = min(CA_TILE, s)
    cpb = t // CHUNK
    pb = PAD // t

    n_g = len(gather)
    grid = (n_pairs, s // t)

    def body(*refs):
        q_ref, k_ref, v_ref, b_ref = refs[:4]
        o_ref = refs[5 + n_g]
        g_refs, sems = refs[6 + n_g:6 + 2 * n_g], refs[6 + 2 * n_g:]

        def compute():
            qi = pl.program_id(1)
            wss = [pl.multiple_of((qi * cpb + ci) * CHUNK, CHUNK) for ci in range(cpb)]
            ps = [_ca_probs(k_ref[pl.ds(wss[ci], BAND), :], _head_blocks(q_ref[ci * CHUNK:(ci + 1) * CHUNK, :]),
                            b_ref[...], qi * cpb + ci) for ci in range(cpb)]
            for ci in range(cpb):
                r = lax.dot_general(ps[ci].astype(BF16), v_ref[pl.ds(wss[ci], BAND), :], TN,
                                    preferred_element_type=F32)
                o_ref[ci * CHUNK:(ci + 1) * CHUNK, :] = _head_diag(r)

        _ride_gather(grid, g_refs, sems, compute)

    full = pl.BlockSpec((sp_rows, LANES), lambda p, i: (0, p))
    outs = pl.pallas_call(
        body, grid=grid,
        in_specs=[pl.BlockSpec((t, LANES), lambda p, i: (i + pb, p)), full, full,
                  pl.BlockSpec((None, BAND, LANES), lambda p, i: (p, 0, 0)), pl.BlockSpec(memory_space=pl.ANY)]
        + [HBM_SPEC] * n_g,
        out_specs=[pl.BlockSpec((t, LANES), lambda p, i: (i, n_pairs + p))] + [HBM_SPEC] * n_g,
        out_shape=[jax.ShapeDtypeStruct(mixed.shape, mixed.dtype)]
        + [jax.ShapeDtypeStruct(g.shape, g.dtype) for g in gather],
        input_output_aliases={4: 0, **{5 + i: 1 + i for i in range(n_g)}},
        scratch_shapes=_gather_sems(n_g) if n_g else [],
        compiler_params=_params("arbitrary", "arbitrary"), name=name)(qn, knp, vp, bias_tt, mixed, *gather)
    return outs[0], list(outs[1:])


def _ca_bwd(name, qn, knp, vp, bias_tt, dmixed, n_pairs, col0, exchange=()):
    sp_rows = knp.shape[0]
    s = sp_rows - PAD
    t = min(CA_TILE, s)
    cpb = t // CHUNK
    pb = PAD // t
    n_x = len(exchange)
    grid = (n_pairs, s // t)

    def body(*refs):
        x_in, x_out, sems = refs[5:5 + n_x], refs[9 + n_x:9 + 2 * n_x], refs[9 + 2 * n_x:]
        if n_x:
            @pl.when(_grid_edge(grid, False))
            def _():
                for cp in _exchange_copies(x_in, x_out, *sems):
                    cp.start()

        _ca_bwd_body(*refs[:5], *refs[5 + n_x:9 + n_x])
        if n_x:
            @pl.when(_grid_edge(grid, True))
            def _():
                for cp in _exchange_copies(x_in, x_out, *sems):
                    cp.wait()

    def _ca_bwd_body(q_ref, k_ref, v_ref, b_ref, do_ref, dq_ref, dk_ref, dv_ref, db_ref):
        qi = pl.program_id(1)

        @pl.when(qi == 0)
        def _():
            dk_ref[...] = jnp.zeros_like(dk_ref)
            dv_ref[...] = jnp.zeros_like(dv_ref)
            db_ref[...] = jnp.zeros_like(db_ref)

        wss = [pl.multiple_of((qi * cpb + ci) * CHUNK, CHUNK) for ci in range(cpb)]
        kws = [k_ref[pl.ds(ws, BAND), :] for ws in wss]
        qbs = [_head_blocks(q_ref[ci * CHUNK:(ci + 1) * CHUNK, :]) for ci in range(cpb)]
        dbs = [_head_blocks(do_ref[ci * CHUNK:(ci + 1) * CHUNK, :].astype(BF16)) for ci in range(cpb)]
        ps = [_ca_probs(kws[ci], qbs[ci], b_ref[...], qi * cpb + ci) for ci in range(cpb)]
        dps = [lax.dot_general(v_ref[pl.ds(wss[ci], BAND), :], dbs[ci], NT, preferred_element_type=F32)
               for ci in range(cpb)]
        dss = [p * (dp - jnp.sum(p * dp, axis=0, keepdims=True)) for p, dp in zip(ps, dps)]
        for ci in range(cpb):
            dsb = dss[ci].astype(BF16)
            dv_ref[pl.ds(wss[ci], BAND), :] += jnp.dot(ps[ci].astype(BF16), dbs[ci], preferred_element_type=F32)
            dk_ref[pl.ds(wss[ci], BAND), :] += jnp.dot(dsb, qbs[ci], preferred_element_type=F32) * SCALE
            r = lax.dot_general(dsb, kws[ci], TN, preferred_element_type=F32)
            dq_ref[ci * CHUNK:(ci + 1) * CHUNK, :] = _head_diag(r) * SCALE
        total = dss[0]
        for ci in range(1, cpb):
            total = total + dss[ci]
        db_ref[...] += total

    full_in = pl.BlockSpec((sp_rows, LANES), lambda p, i: (0, p))
    btile = pl.BlockSpec((None, BAND, LANES), lambda p, i: (p, 0, 0))
    pad_shape = jax.ShapeDtypeStruct((sp_rows, n_pairs * LANES), F32)
    outs = pl.pallas_call(
        body, grid=grid,
        in_specs=[pl.BlockSpec((t, LANES), lambda p, i: (i + pb, p)), full_in, full_in, btile,
                  pl.BlockSpec((t, LANES), lambda p, i: (i, col0 + p))] + [HBM_SPEC] * n_x,
        out_specs=[pl.BlockSpec((t, LANES), lambda p, i: (i, p)), full_in, full_in, btile] + [HBM_SPEC] * n_x,
        out_shape=[jax.ShapeDtypeStruct((s, n_pairs * LANES), F32), pad_shape, pad_shape,
                   jax.ShapeDtypeStruct(bias_tt.shape, F32)]
        + [jax.ShapeDtypeStruct((3,) + p.shape[1:], p.dtype) for p in exchange],
        scratch_shapes=_exchange_sems(n_x) if n_x else [],
        compiler_params=_params("arbitrary", "arbitrary"), name=name)(qn, knp, vp, bias_tt, dmixed, *exchange)
    return outs[0], outs[1], outs[2], outs[3], list(outs[4:])


def _pair_transposed(tile):
    n_h = tile.shape[0]
    return jnp.transpose(tile.reshape(n_h // 2, 2, CHUNK, BAND), (0, 3, 1, 2)).reshape(n_h // 2, BAND, LANES)


def _query_major(tile_t):
    n_p = tile_t.shape[0]
    return jnp.transpose(tile_t.reshape(n_p, BAND, 2, CHUNK), (3, 0, 2, 1)).reshape(CHUNK, 2 * n_p, BAND)


def _bias_tile(rb):
    n_h = rb.shape[0]
    far = jnp.broadcast_to(rb[:, 2 * REL_CLIP:], (n_h, PAD + CHUNK - REL_CLIP))
    near = rb[:, REL_CLIP - (CHUNK - 1):2 * REL_CLIP][:, ::-1]
    ext = jnp.concatenate([far, near], axis=1)
    return jnp.stack([ext[:, CHUNK - 1 - i:CHUNK - 1 - i + BAND] for i in range(CHUNK)], axis=1)


def _rel_bias_grad(name, db_t):
    _, n_h, _ = db_t.shape
    n_out = 3 * LANES
    assert CHUNK == 64 and REL_CLIP == 128 and LEFT_CHUNKS == 8

    def body(z_ref, o_ref):
        a_idx = lax.broadcasted_iota(jnp.int32, (LANES, n_out), 0)
        r_idx = lax.broadcasted_iota(jnp.int32, (LANES, n_out), 1)
        out = jnp.zeros((n_h, n_out), F32)
        for mm, base in ((0, 191), (1, 255), (2, 319)):
            b = LEFT_CHUNKS - mm
            acc = jnp.zeros((n_h, LANES), F32)
            for i in range(CHUNK):
                row = z_ref[i, :, b * CHUNK:(b + 1) * CHUNK]
                rowp = jnp.concatenate([row, jnp.zeros((n_h, LANES - CHUNK), F32)], axis=1)
                acc = acc + pltpu.roll(rowp, CHUNK - 1 - i, axis=1)
            place = (r_idx == jnp.minimum(base - a_idx, 2 * REL_CLIP)).astype(BF16)
            out = out + _split_dot(acc, place, 3)
        far = jnp.zeros((n_h, 1), F32)
        for i in range(CHUNK):
            far = far + jnp.sum(z_ref[i, :, 0:(LEFT_CHUNKS - 2) * CHUNK], axis=1, keepdims=True)
        lane = lax.broadcasted_iota(jnp.int32, (n_h, n_out), 1)
        o_ref[...] = out + jnp.where(lane == 2 * REL_CLIP, far, 0.0)

    return pl.pallas_call(
        body, out_shape=jax.ShapeDtypeStruct((n_h, n_out), F32),
        in_specs=[pl.BlockSpec(memory_space=pltpu.VMEM)], out_specs=pl.BlockSpec(memory_space=pltpu.VMEM),
        name=name)(db_t)


def _mesh_pos():
    return lax.axis_index("x"), lax.axis_index("y"), lax.axis_index("c")


def _other_chips(x, y):
    return [(1 - x, y), (x, 1 - y), (1 - x, 1 - y)]


def _all_gather_small(name, blk):
    m_per, n = blk.shape

    def body(x_ref, out_ref, send_sems, recv_sems, local_sem):
        x, y, c = _mesh_pos()
        me, sibling = (x, y, c), (x, y, 1 - c)
        chips = _other_chips(x, y)

        def rows(px, py, pc):
            return out_ref.at[pl.ds((4 * px + 2 * py + pc) * m_per, m_per), :]

        def copy(k, block, to, src=None):
            return pltpu.make_async_remote_copy(
                src_ref=rows(*block) if src is None else src, dst_ref=rows(*block),
                send_sem=send_sems.at[k], recv_sem=recv_sems.at[k], device_id=to, device_id_type=MESH)

        mine = pltpu.make_async_copy(x_ref, rows(*me), local_sem)
        mine.start()
        first = [copy(0, me, sibling, src=x_ref)]
        first += [copy(1 + j, me, (*chip, c), src=x_ref) for j, chip in enumerate(chips)]
        for cp in first:
            cp.start()
        passed = [copy(4 + j, (*chip, c), sibling) for j, chip in enumerate(chips)]
        for j, chip in enumerate(chips):
            copy(1 + j, (*chip, c), me).wait_recv()
            passed[j].start()
        copy(0, sibling, me).wait_recv()
        for j, chip in enumerate(chips):
            copy(4 + j, (*chip, 1 - c), me).wait_recv()
        for cp in first + passed:
            cp.wait_send()
        mine.wait()

    return pl.pallas_call(
        body, out_shape=jax.ShapeDtypeStruct((N_DEV * m_per, n), blk.dtype),
        in_specs=[pl.BlockSpec(memory_space=pltpu.VMEM)], out_specs=pl.BlockSpec(memory_space=pltpu.VMEM),
        scratch_shapes=[pltpu.SemaphoreType.DMA((7,)), pltpu.SemaphoreType.DMA((7,)), pltpu.SemaphoreType.DMA],
        compiler_params=pltpu.CompilerParams(vmem_limit_bytes=VMEM_LIMIT), name=name)(blk)


HBM_SPEC = pl.BlockSpec(memory_space=pltpu.HBM)


def _gather_copies(bufs, send_sems, recv_sems, phase):
    x, y, c = _mesh_pos()
    my_j = 2 * x + y
    starts, lands = [], []
    for t, buf in enumerate(bufs):
        rh = buf.shape[2] // 2
        for k, (px, py) in enumerate(_other_chips(x, y)):
            their_j = 2 * px + py
            if phase == 0:
                src, to = buf.at[0, my_j, pl.ds(c * rh, rh), :], (px, py, c)
                land = buf.at[0, their_j, pl.ds(c * rh, rh), :]
            else:
                src, to = buf.at[0, their_j, pl.ds(c * rh, rh), :], (x, y, 1 - c)
                land = buf.at[0, their_j, pl.ds((1 - c) * rh, rh), :]
            sems = dict(send_sem=send_sems.at[phase, t, k], recv_sem=recv_sems.at[phase, t, k],
                        device_id=to, device_id_type=MESH)
            starts.append(pltpu.make_async_remote_copy(src_ref=src, dst_ref=src, **sems))
            lands.append(pltpu.make_async_remote_copy(src_ref=src, dst_ref=land, **sems))
    return starts, lands


def _gather_sems(n):
    return [pltpu.SemaphoreType.DMA((2, n, 3)), pltpu.SemaphoreType.DMA((2, n, 3))]


def _all_gather_weights(name, bufs, phases):
    n = len(bufs)

    def body(*refs):
        outs = refs[n:2 * n]
        send_sems, recv_sems = refs[2 * n:]
        for phase in phases:
            starts, lands = _gather_copies(outs, send_sems, recv_sems, phase)
            for cp in starts:
                cp.start()
            for cp, ld in zip(starts, lands):
                cp.wait_send()
                ld.wait_recv()

    return pl.pallas_call(
        body, out_shape=[jax.ShapeDtypeStruct(b.shape, b.dtype) for b in bufs],
        in_specs=[HBM_SPEC] * n, out_specs=[HBM_SPEC] * n, input_output_aliases={t: t for t in range(n)},
        scratch_shapes=_gather_sems(n), name=name)(*bufs)


def _gather_buffer(shard, chip):
    r, c = shard.shape
    return lax.dynamic_update_slice(jnp.zeros((1, N_CHIPS, r, c), BF16), shard.astype(BF16)[None, None],
                                    (0, chip, 0, 0))


def _sibling_copies(srcs, outs, send_sems, recv_sems):
    x, y, c = _mesh_pos()
    cps = []
    for t, (src, out) in enumerate(zip(srcs, outs)):
        rh = src.shape[1] // 2
        cps.append(pltpu.make_async_remote_copy(
            src_ref=src.at[:, pl.ds((1 - c) * rh, rh), :], dst_ref=out,
            send_sem=send_sems.at[t], recv_sem=recv_sems.at[t], device_id=(x, y, 1 - c), device_id_type=MESH))
    return cps


def _exchange_sibling_halves(name, grads):
    n = len(grads)

    def body(*refs):
        cps = _sibling_copies(refs[:n], refs[n:2 * n], *refs[2 * n:])
        for cp in cps:
            cp.start()
        for cp in cps:
            cp.wait()

    out_shape = [jax.ShapeDtypeStruct((g.shape[0], g.shape[1] // 2, g.shape[2]), g.dtype) for g in grads]
    return pl.pallas_call(
        body, out_shape=out_shape, in_specs=[HBM_SPEC] * n, out_specs=[HBM_SPEC] * n,
        scratch_shapes=[pltpu.SemaphoreType.DMA((n,)), pltpu.SemaphoreType.DMA((n,))],
        name=name)(*grads)


def _add_halves(name, g, recv, c_idx):
    nb, r, c = g.shape
    rh = r // 2
    tr = rh
    for cand in (512, 256, 128, 64):
        if rh % cand == 0:
            tr = cand
            break
    g4 = g.reshape(nb, 2, rh, c)

    def body(c_ref, g_ref, r_ref, o_ref):
        o_ref[...] = (g_ref[...] + r_ref[...]).astype(BF16)

    grid_spec = pltpu.PrefetchScalarGridSpec(
        num_scalar_prefetch=1, grid=(nb, rh // tr),
        in_specs=[pl.BlockSpec((None, None, tr, c), lambda j, i, cr: (j, cr[0], i, 0)),
                  pl.BlockSpec((None, tr, c), lambda j, i, cr: (j, i, 0))],
        out_specs=pl.BlockSpec((None, tr, c), lambda j, i, cr: (j, i, 0)))
    return pl.pallas_call(
        body, grid_spec=grid_spec, out_shape=jax.ShapeDtypeStruct((nb, rh, c), BF16),
        compiler_params=_params("parallel", "parallel"), name=name)(c_idx, g4, recv)


def _exchange_copies(srcs, outs, send_sems, recv_sems):
    x, y, c = _mesh_pos()
    return [pltpu.make_async_remote_copy(
        src_ref=srcs[t].at[2 * px + py], dst_ref=outs[t].at[k], send_sem=send_sems.at[t, k],
        recv_sem=recv_sems.at[t, k], device_id=(px, py, c), device_id_type=MESH)
        for t in range(len(srcs)) for k, (px, py) in enumerate(_other_chips(x, y))]


def _exchange_sems(n):
    return [pltpu.SemaphoreType.DMA((n, 3)), pltpu.SemaphoreType.DMA((n, 3))]


def _sum_chips(name, own_parts, recvs, pos_idx):
    n_layers = len(own_parts)
    _, rh, c = recvs[0].shape
    tr = rh
    for cand in (512, 256, 128, 64):
        if rh % cand == 0:
            tr = cand
            break
    nb = rh // tr

    def body(pos_ref, *refs):
        own_refs, r_refs, o_ref = refs[:n_layers], refs[n_layers:2 * n_layers], refs[2 * n_layers]
        for l_i in range(n_layers):
            @pl.when(pl.program_id(0) == l_i)
            def _():
                acc = own_refs[l_i][...].astype(F32)
                for k in range(3):
                    acc = acc + r_refs[l_i][k].astype(F32)
                o_ref[...] = acc

    grid_spec = pltpu.PrefetchScalarGridSpec(
        num_scalar_prefetch=1, grid=(n_layers, nb),
        in_specs=[pl.BlockSpec((None, tr, c), lambda l, i, pr: (pr[0], i, 0)) for _ in range(n_layers)]
        + [pl.BlockSpec((3, tr, c), lambda l, i, pr: (0, i, 0)) for _ in range(n_layers)],
        out_specs=pl.BlockSpec((None, tr, c), lambda l, i, pr: (l, pr[1] * nb + i, 0)))
    return pl.pallas_call(
        body, grid_spec=grid_spec, out_shape=jax.ShapeDtypeStruct((n_layers, 2 * rh, c), F32),
        compiler_params=_params("parallel", "parallel"), name=name)(pos_idx, *own_parts, *recvs)


def _share_with_sibling(name, grads):
    n = len(grads)

    def body(*refs):
        outs = refs[n:2 * n]
        send_sems, recv_sems = refs[2 * n:]
        x, y, c = _mesh_pos()
        cps, landed = [], []
        for t in range(n):
            rh = grads[t].shape[1] // 2
            mine = outs[t].at[:, pl.ds(c * rh, rh), :]
            theirs = outs[t].at[:, pl.ds((1 - c) * rh, rh), :]
            cps.append(pltpu.make_async_remote_copy(
                src_ref=mine, dst_ref=mine, send_sem=send_sems.at[t], recv_sem=recv_sems.at[t],
                device_id=(x, y, 1 - c), device_id_type=MESH))
            landed.append(pltpu.make_async_remote_copy(
                src_ref=mine, dst_ref=theirs, send_sem=send_sems.at[t], recv_sem=recv_sems.at[t],
                device_id=(x, y, 1 - c), device_id_type=MESH))
        for cp in cps:
            cp.start()
        for cp, ld in zip(cps, landed):
            cp.wait_send()
            ld.wait_recv()

    out_shape = [jax.ShapeDtypeStruct(g.shape, g.dtype) for g in grads]
    return pl.pallas_call(
        body, out_shape=out_shape, in_specs=[HBM_SPEC] * n, out_specs=[HBM_SPEC] * n,
        input_output_aliases={t: t for t in range(n)},
        scratch_shapes=[pltpu.SemaphoreType.DMA((n,)), pltpu.SemaphoreType.DMA((n,))],
        name=name)(*grads)


def _row(v):
    return v.reshape(1, -1)


def kernel(x, c, g_norm1, w_in, g_q, g_k, rel_bias, w_o, g_norm2, w1, w2, w_ada, b_ada, loss_target, m_g_norm1, m_w_in, m_g_q, m_g_k, m_rel_bias, m_w_o, m_g_norm2, m_w1, m_w2, m_w_ada, m_b_ada, v_g_norm1, v_w_in, v_g_q, v_g_k, v_rel_bias, v_w_o, v_g_norm2, v_w1, v_w2, v_w_ada, v_b_ada):
    _, s, d = x.shape
    n_layers = g_norm1.shape[0]
    n_pairs = (d // 2) // LANES
    n_ca_heads = rel_bias.shape[1]
    n_rel = rel_bias.shape[2]
    assert n_rel == 2 * REL_CLIP + 1 and g_q.shape[1] == HEAD_DIM and n_ca_heads == 2 * n_pairs
    ada_c = w_ada.shape[2]

    ax, ay, ac = _mesh_pos()
    chip = 2 * ax + ay
    dev = 4 * ax + 2 * ay + ac
    c_idx = jnp.reshape(ac, (1,)).astype(jnp.int32)

    c_act_all = _all_gather_small("ag_c", _silu_rows("silu_c", c)).reshape(N_DEV, 8, d)[:, 0, :]
    mod_parts = []
    for l in range(n_layers):
        bias = lax.dynamic_slice_in_dim(b_ada[l], chip * ada_c, ada_c).reshape(1, ada_c)
        mod_parts.append(_ada_mod(f"ada_mod{l}", c_act_all, w_ada[l], bias))
    mod_all = _all_gather_small("ag_mod", jnp.concatenate(mod_parts, axis=1))
    mod_all = mod_all.reshape(N_CHIPS, 2, N_DEV, n_layers, ada_c)[:, 0]
    mod_all = jnp.transpose(mod_all, (1, 2, 0, 3)).reshape(N_DEV, n_layers, N_CHIPS * ada_c)
    mod = lax.dynamic_index_in_dim(mod_all, dev, axis=0, keepdims=False)

    gathered = [[_gather_buffer(w[l], chip) for w in (w_in, w_o, w1, w2)] for l in range(n_layers)]
    gathered[0][:1] = _all_gather_weights("ag_w_in_0", gathered[0][:1], (0, 1))

    xs = x[0]
    saved = []
    for l in range(n_layers):
        sh1, sc1, gt1, sh2, sc2, gt2 = [_row(mod[l, i * d:(i + 1) * d]) for i in range(6)]
        gq2 = _row(jnp.tile(g_q[l], 2))
        gk2 = _row(jnp.tile(g_k[l], 2))
        bias_t = _pair_transposed(_bias_tile(rel_bias[l]))
        h1 = _norm_mod(f"norm1_{l}", xs, _row(g_norm1[l]), sc1, sh1)
        proj = _mm_nn_w(f"proj_{l}", h1, gathered[l][0], 0, True, [F32])[0]
        o_sb, landed = _sb_fwd(f"sb_fwd_{l}", proj, n_pairs, gather=gathered[l][1:] if l == 0 else ())
        if landed:
            gathered[l][1:] = _all_gather_weights(f"ag_rest_{l}", landed, (1,))
        wg_in, wg_o, wg_1, wg_2 = gathered[l]
        nxt = gathered[l + 1] if l + 1 < n_layers else []
        qn, knp, vp = _qk_norm(f"qk_norm_{l}", proj, gq2, gk2, n_pairs)
        mixed, landed = _ca_fwd(f"ca_fwd_{l}", qn, knp, vp, bias_t, n_pairs, o_sb, gather=nxt[:2])

        def res_epi(acc, res, gt):
            return res + gt * acc, acc

        def res_specs(tm, tn):
            return (pl.BlockSpec((tm, tn), lambda i, j, kk: (i, j)), pl.BlockSpec((1, tn), lambda i, j, kk: (0, j)))

        x1, m1 = _mm_nn_w(f"attn_out_{l}", mixed, wg_o, 0, False, [F32, BF16], epi=res_epi,
                          extras=(xs, gt1), extra_specs_fn=res_specs)
        h2 = _norm_mod(f"norm2_{l}", x1, _row(g_norm2[l]), sc2, sh2)

        def act_epi(acc):
            r = jnp.maximum(acc, 0.0)
            return acc, r * r

        if l + 1 < n_layers:
            (u, a), landed_1 = _mm_nn_w(f"mlp_up_{l}", h2, wg_1, 0, True, [BF16, BF16], epi=act_epi, gather=nxt[2:3])
            (x2, m2), landed_2 = _mm_nn_w(f"mlp_down_{l}", a, wg_2, 0, False, [F32, BF16], epi=res_epi,
                                          extras=(x1, gt2), extra_specs_fn=res_specs, gather=nxt[3:])
            gathered[l + 1] = _all_gather_weights(f"ag_weights_{l + 1}", landed + landed_1 + landed_2, (1,))
        else:
            u, a = _mm_nn_w(f"mlp_up_{l}", h2, wg_1, 0, True, [BF16, BF16], epi=act_epi)
            x2, m2 = _mm_nn_w(f"mlp_down_{l}", a, wg_2, 0, False, [F32, BF16], epi=res_epi,
                              extras=(x1, gt2), extra_specs_fn=res_specs)
        saved.append(dict(x0=xs, h1=h1, proj=proj, o_sb=mixed, qn=qn, knp=knp, vp=vp, bias_t=bias_t, mixed=mixed,
                          m1=m1, x1=x1, h2=h2, u=u, a=a, m2=m2, gq2=gq2, gk2=gk2,
                          sc1=sc1, gt1=gt1, sc2=sc2, gt2=gt2))
        xs = x2

    dx, sq = _loss_grad("loss", xs, loss_target[0])
    loss_part = (0.5 * jnp.sum(sq) / d).reshape(1)

    g_in, g_o, g_1, g_2 = [None] * n_layers, [None] * n_layers, [None] * n_layers, [None] * n_layers
    dmod, dg1, dg2, dgq, dgk, drel = [], [], [], [], [], []
    partial, chip_sum, from_chip = {}, {}, {}

    def reduce_on_chip(tag, keys):
        grads = [partial[k] for k in keys]
        for k, g, r in zip(keys, grads, _exchange_sibling_halves(f"rs_sibling_{tag}", grads)):
            chip_sum[k] = _add_halves(f"rs_add_{k[0]}_{k[1]}", g, r, c_idx)

    for l in reversed(range(n_layers)):
        sv = saved[l]
        wg_in, wg_o, wg_1, wg_2 = gathered[l]
        dm2, dgt2 = _gate_bwd(f"gate2_bwd_{l}", dx, sv["m2"], sv["gt2"])

        def act_bwd_epi(acc, u_t):
            return (acc * (2.0 * jnp.maximum(u_t.astype(F32), 0.0)),)

        def tile_specs(tm, tn):
            return (pl.BlockSpec((tm, tn), lambda i, j, kk: (i, j)),)

        above = sorted(k for k in partial if k not in chip_sum) if l == 0 else []
        du = _mm_nt_w(f"mlp_down_bwd_{l}", dm2, wg_2, 0, False, BF16, epi=act_bwd_epi, extras=(sv["u"],),
                      extra_specs_fn=tile_specs, riders=[_sibling_rider([partial[k] for k in above])] if above else ())
        if above:
            du, (from_sibling,) = du
            for k, r in zip(above, from_sibling):
                chip_sum[k] = _add_halves(f"rs_add_{k[0]}_{k[1]}", partial[k], r, c_idx)
        g_2[l] = _mm_tn(f"w2_grad_{l}", sv["a"], dm2, False)
        dh2 = _mm_nt_w(f"mlp_up_bwd_{l}", du, wg_1, 0, True, F32)
        g_1[l] = _mm_tn(f"w1_grad_{l}", sv["h2"], du, True)
        dx1, dsh2, dsc2, dgn2 = _norm_mod_bwd(f"norm2_bwd_{l}", dh2, sv["x1"], _row(g_norm2[l]), sv["sc2"], dx)
        dm1, dgt1 = _gate_bwd(f"gate1_bwd_{l}", dx1, sv["m1"], sv["gt1"])
        dmixed = _mm_nt_w(f"attn_out_bwd_{l}", dm1, wg_o, 0, False, F32)
        g_o[l] = _mm_tn(f"wo_grad_{l}", sv["mixed"], dm1, False)
        partial.update({(1, l): g_o[l], (2, l): g_1[l], (3, l): g_2[l]})
        ride_sb, ride_ca = [], []
        if l == 0:
            early = sorted(partial)
            reduce_on_chip("early", [k for k in early if k not in chip_sum])
            ride_ca = [k for k in early if k[0] == 0 or k == (3, 0)]
            ride_sb = [k for k in early if k not in ride_ca]
        dq_sb, dk_sb, dv_sb, landed = _sb_bwd(f"sb_bwd_{l}", sv["proj"], sv["o_sb"], dmixed, n_pairs,
                                              exchange=[chip_sum[k] for k in ride_sb])
        from_chip.update(zip(ride_sb, landed))
        dqn, dknp, dvp, dbias_t, landed = _ca_bwd(f"ca_bwd_{l}", sv["qn"], sv["knp"], sv["vp"], sv["bias_t"], dmixed,
                                                  n_pairs, n_pairs, exchange=[chip_sum[k] for k in ride_ca])
        from_chip.update(zip(ride_ca, landed))
        dq_ca, dk_ca, dgq_l, dgk_l = _qk_norm_bwd(f"qk_norm_bwd_{l}", dqn, dknp, sv["proj"], sv["gq2"], sv["gk2"],
                                                  n_pairs)
        drel_l = _rel_bias_grad(f"rel_bias_grad_{l}", _query_major(dbias_t))[:, :n_rel]
        dproj = jnp.concatenate([dq_sb.astype(BF16), dk_sb.astype(BF16), dv_sb.astype(BF16), dq_ca, dk_ca,
                                 dvp[PAD:].astype(BF16)], axis=1)
        g_in[l] = _mm_tn(f"win_grad_{l}", sv["h1"], dproj, True)
        partial[(0, l)] = g_in[l]
        if l == 0:
            reduce_on_chip("late", [(0, l)])
            dh1, (landed,) = _mm_nt_w(f"proj_bwd_{l}", dproj, wg_in, 0, True, F32,
                                      riders=[_exchange_rider([chip_sum[(0, l)]])])
            from_chip[(0, l)] = landed[0]
        else:
            dh1 = _mm_nt_w(f"proj_bwd_{l}", dproj, wg_in, 0, True, F32)
        dx, dsh1, dsc1, dgn1 = _norm_mod_bwd(f"norm1_bwd_{l}", dh1, sv["x0"], _row(g_norm1[l]), sv["sc1"], dx1)
        dmod.insert(0, jnp.concatenate([dsh1, dsc1, dgt1, dsh2, dsc2, dgt2], axis=1)[0])
        dg1.insert(0, dgn1[0])
        dg2.insert(0, dgn2[0])
        dgq.insert(0, dgq_l[0, :HEAD_DIM])
        dgk.insert(0, dgk_l[0, :HEAD_DIM])
        drel.insert(0, drel_l.reshape(-1))
    grad_x = dx[None]

    small = jnp.concatenate([jnp.concatenate(dmod), jnp.concatenate(dg1), jnp.concatenate(dg2),
                             jnp.concatenate(dgq), jnp.concatenate(dgk), jnp.concatenate(drel), loss_part])
    n_small = small.shape[0]
    n_pack = -(-n_small // (8 * LANES)) * (8 * LANES)
    small = jnp.pad(small, (0, n_pack - n_small)).reshape(8, n_pack // 8)
    small_all = _all_gather_small("ag_small", small).reshape(N_DEV, 8, n_pack // 8)
    small_sum = _sum_rows8("sum_small", small_all).reshape(-1)
    dmod_all = small_all.reshape(N_DEV, n_pack)[:, :n_layers * 6 * d].reshape(N_DEV, n_layers, 6 * d)

    off = 0

    def take(n, shape):
        nonlocal off
        out = small_sum[off:off + n].reshape(shape)
        off += n
        return out

    grad_b_ada = take(n_layers * 6 * d, (n_layers, 6 * d))
    grad_g_norm1 = take(n_layers * d, (n_layers, d))
    grad_g_norm2 = take(n_layers * d, (n_layers, d))
    grad_g_q = take(n_layers * HEAD_DIM, (n_layers, HEAD_DIM))
    grad_g_k = take(n_layers * HEAD_DIM, (n_layers, HEAD_DIM))
    grad_rel_bias = take(n_layers * n_ca_heads * n_rel, (n_layers, n_ca_heads, n_rel))
    loss = take(1, ())

    c_act_t = jnp.transpose(c_act_all)
    grad_w_ada = jnp.stack([
        _outer_sum(f"wada_grad_{l}", c_act_t,
                   lax.dynamic_slice_in_dim(dmod_all[:, l, :], chip * ada_c, ada_c, axis=1))
        for l in range(n_layers)])

    assert all(k in from_chip for k in partial)
    pos_idx = jnp.stack([chip, ac]).astype(jnp.int32)
    reduced = [_sum_chips(f"rs_sum_{w_i}", [chip_sum[(w_i, l)] for l in range(n_layers)],
                          [from_chip[(w_i, l)] for l in range(n_layers)], pos_idx) for w_i in range(4)]
    grad_w_in, grad_w_o, grad_w1, grad_w2 = _share_with_sibling("rs_share", reduced)

    def adam_big(name, w, g, m, v):
        shp = w.shape
        outs = _adamw(name, w.reshape(-1, shp[-1]), g.reshape(-1, shp[-1]), m.reshape(-1, shp[-1]),
                      v.reshape(-1, shp[-1]))
        return [o.reshape(shp) for o in outs]

    def pack(arrs):
        flat = jnp.concatenate([a.reshape(-1) for a in arrs])
        n = flat.shape[0]
        n_p = -(-n // (8 * LANES)) * (8 * LANES)
        return jnp.pad(flat, (0, n_p - n), constant_values=1.0).reshape(8, n_p // 8)

    small_w = [g_norm1, g_q, g_k, rel_bias, g_norm2, b_ada]
    small_g = [grad_g_norm1, grad_g_q, grad_g_k, grad_rel_bias, grad_g_norm2, grad_b_ada]
    small_m = [m_g_norm1, m_g_q, m_g_k, m_rel_bias, m_g_norm2, m_b_ada]
    small_v = [v_g_norm1, v_g_q, v_g_k, v_rel_bias, v_g_norm2, v_b_ada]
    packed = _adamw("adamw_small", pack(small_w), pack(small_g), pack(small_m), pack(small_v))

    def unpack(p):
        flat = p.reshape(-1)
        res, o = [], 0
        for a in small_w:
            res.append(flat[o:o + a.size].reshape(a.shape))
            o += a.size
        return res

    sd, sm, sv_ = unpack(packed[0]), unpack(packed[1]), unpack(packed[2])
    big = {
        "w_in": adam_big("adamw_w_in", w_in, grad_w_in, m_w_in, v_w_in),
        "w_o": adam_big("adamw_w_o", w_o, grad_w_o, m_w_o, v_w_o),
        "w1": adam_big("adamw_w1", w1, grad_w1, m_w1, v_w1),
        "w2": adam_big("adamw_w2", w2, grad_w2, m_w2, v_w2),
        "w_ada": adam_big("adamw_w_ada", w_ada, grad_w_ada, m_w_ada, v_w_ada),
    }

    def ordered(kind):
        sm_list = (sd, sm, sv_)[kind]
        return [sm_list[0], big["w_in"][kind], sm_list[1], sm_list[2], sm_list[3], big["w_o"][kind], sm_list[4],
                big["w1"][kind], big["w2"][kind], big["w_ada"][kind], sm_list[5]]

    grads = [grad_g_norm1, grad_w_in, grad_g_q, grad_g_k, grad_rel_bias, grad_w_o, grad_g_norm2, grad_w1, grad_w2,
             grad_w_ada, grad_b_ada]
    return (loss, grad_x, *grads, *ordered(0), *ordered(1), *ordered(2))
```
